```python
import math
import jax, jax.numpy as jnp
from jax import lax
import numpy as np

D_MODEL = 1024
BATCH = 32
SEQ = 2048
DEPTH = 4

N_MIXERS = 2
CONV_WIDTH = 3
N_HEADS = 16
N_KV_HEADS = 4
HEAD_DIM = D_MODEL // N_HEADS
GROUP = N_HEADS // N_KV_HEADS
WINDOW = 128
BLOCK = WINDOW
ROPE_THETA = 10000.0
D_FF = 2816
EPS = 1e-5
QKV_WIDTH = (N_HEADS + 2 * N_KV_HEADS) * HEAD_DIM
N_CONV_LAYERS = (DEPTH + 1) // 2
N_ATTN_LAYERS = DEPTH // 2

kernel_name = "hybrid_shortconv_swa_sink_convffn"


def rms_norm(x, g):
    xf = x.astype(jnp.float32)
    y = xf * lax.rsqrt(jnp.mean(xf * xf, axis=-1, keepdims=True) + EPS)
    return (y * g.astype(jnp.float32)).astype(x.dtype)


def causal_dwconv(x, w):
    c = x.shape[-1]
    return lax.conv_general_dilated(
        x, w.astype(x.dtype)[:, None, :], window_strides=(1,),
        padding=[(CONV_WIDTH - 1, 0)], dimension_numbers=("NWC", "WIO", "NWC"),
        feature_group_count=c)


def short_conv_mixer(h, w_in, w_conv, w_out):
    bcv = h @ w_in
    b_gate, c_gate, v = jnp.split(bcv, 3, axis=-1)
    y = b_gate * causal_dwconv(c_gate * v, w_conv)
    return y @ w_out


def rope(x, cos, sin):
    x1, x2 = jnp.split(x, 2, axis=-1)
    c = cos[None, :, None, :]
    s = sin[None, :, None, :]
    return jnp.concatenate([x1 * c - x2 * s, x2 * c + x1 * s], axis=-1)


def swa_sink_attention(h, w_qkv, b_qkv, sinks, w_o, b_o, cos, sin):
    bsz, seq, _ = h.shape
    nb = seq // BLOCK
    qkv = h @ w_qkv + b_qkv
    q_w, kv_w = N_HEADS * HEAD_DIM, N_KV_HEADS * HEAD_DIM
    q = qkv[..., :q_w].reshape(bsz, seq, N_HEADS, HEAD_DIM)
    k = qkv[..., q_w:q_w + kv_w].reshape(bsz, seq, N_KV_HEADS, HEAD_DIM)
    v = qkv[..., q_w + kv_w:].reshape(bsz, seq, N_KV_HEADS, HEAD_DIM)
    q = rope(q, cos, sin)
    k = rope(k, cos, sin)

    q = q.reshape(bsz, nb, BLOCK, N_KV_HEADS, GROUP, HEAD_DIM)
    pad = ((0, 0), (1, 0), (0, 0), (0, 0), (0, 0))
    kp = jnp.pad(k.reshape(bsz, nb, BLOCK, N_KV_HEADS, HEAD_DIM), pad)
    vp = jnp.pad(v.reshape(bsz, nb, BLOCK, N_KV_HEADS, HEAD_DIM), pad)
    k_band = jnp.concatenate([kp[:, :-1], kp[:, 1:]], axis=2)
    v_band = jnp.concatenate([vp[:, :-1], vp[:, 1:]], axis=2)

    scores = jnp.einsum("bnqkgd,bnskd->bnkgqs", q, k_band).astype(jnp.float32)
    scores = scores * (HEAD_DIM ** -0.5)

    blk = jnp.arange(nb)[:, None, None]
    qi = jnp.arange(BLOCK)[None, :, None]
    kj = jnp.arange(2 * BLOCK)[None, None, :]
    q_pos = blk * BLOCK + qi
    k_pos = (blk - 1) * BLOCK + kj
    valid = (k_pos <= q_pos) & (q_pos - k_pos < WINDOW) & (k_pos >= 0)
    scores = jnp.where(valid[None, :, None, None], scores, jnp.finfo(jnp.float32).min)

    sink = sinks.astype(jnp.float32).reshape(N_KV_HEADS, GROUP)[None, None, :, :, None, None]
    m = jnp.maximum(jnp.max(scores, axis=-1, keepdims=True), sink)
    p = jnp.exp(scores - m)
    denom = jnp.sum(p, axis=-1, keepdims=True) + jnp.exp(sink - m)
    probs = (p / denom).astype(v_band.dtype)

    o = jnp.einsum("bnkgqs,bnskd->bnqkgd", probs, v_band)
    o = o.reshape(bsz, seq, N_HEADS * HEAD_DIM)
    return o @ w_o + b_o


def conv_ffn(h, w_in, w_conv, w_down):
    gu = h @ w_in
    g, u = jnp.split(gu, 2, axis=-1)
    g = causal_dwconv(g, w_conv)
    return (jax.nn.silu(g) * u) @ w_down


def _fwd_setup_inputs(seed: int = 0) -> dict:
    key = jax.random.key(seed)
    ks = jax.random.split(key, 16)
    f32 = jnp.float32

    def w(k, shape, fan_in):
        return jax.random.normal(k, shape, f32) * (fan_in ** -0.5)

    return {
        "x": jax.random.normal(ks[0], (BATCH, SEQ, D_MODEL), f32),
        "norm_mix": 1.0 + 0.02 * jax.random.normal(ks[1], (DEPTH, D_MODEL), f32),
        "norm_ffn": 1.0 + 0.02 * jax.random.normal(ks[2], (DEPTH, D_MODEL), f32),
        "norm_final": 1.0 + 0.02 * jax.random.normal(ks[3], (D_MODEL,), f32),
        "conv_w_in": w(ks[4], (N_CONV_LAYERS, D_MODEL, 3 * D_MODEL), D_MODEL),
        "conv_w_conv": w(ks[5], (N_CONV_LAYERS, CONV_WIDTH, D_MODEL), CONV_WIDTH),
        "conv_w_out": w(ks[6], (N_CONV_LAYERS, D_MODEL, D_MODEL), D_MODEL),
        "attn_w_qkv": w(ks[7], (N_ATTN_LAYERS, D_MODEL, QKV_WIDTH), D_MODEL),
        "attn_b_qkv": 0.02 * jax.random.normal(ks[8], (N_ATTN_LAYERS, QKV_WIDTH), f32),
        "attn_sinks": 0.5 * jax.random.normal(ks[9], (N_ATTN_LAYERS, N_HEADS), f32),
        "attn_w_o": w(ks[10], (N_ATTN_LAYERS, N_HEADS * HEAD_DIM, D_MODEL), N_HEADS * HEAD_DIM),
        "attn_b_o": 0.02 * jax.random.normal(ks[11], (N_ATTN_LAYERS, D_MODEL), f32),
        "ffn_w_in": w(ks[12], (DEPTH, D_MODEL, 2 * D_FF), D_MODEL),
        "ffn_w_conv": w(ks[13], (DEPTH, CONV_WIDTH, D_FF), CONV_WIDTH),
        "ffn_w_down": w(ks[14], (DEPTH, D_FF, D_MODEL), D_FF),
    }


def _fwd_reference(x, norm_mix, norm_ffn, norm_final, conv_w_in, conv_w_conv, conv_w_out,
              attn_w_qkv, attn_b_qkv, attn_sinks, attn_w_o, attn_b_o,
              ffn_w_in, ffn_w_conv, ffn_w_down):
    seq = x.shape[1]
    pos = jnp.arange(seq, dtype=jnp.float32)
    inv_freq = 1.0 / (ROPE_THETA ** (jnp.arange(0, HEAD_DIM, 2, dtype=jnp.float32) / HEAD_DIM))
    ang = pos[:, None] * inv_freq[None, :]
    cos = jnp.cos(ang).astype(x.dtype)
    sin = jnp.sin(ang).astype(x.dtype)

    for i in range(DEPTH):
        h = rms_norm(x, norm_mix[i])
        j = i // N_MIXERS
        if i % N_MIXERS == 0:
            mix = short_conv_mixer(h, conv_w_in[j], conv_w_conv[j], conv_w_out[j])
        else:
            mix = swa_sink_attention(h, attn_w_qkv[j], attn_b_qkv[j], attn_sinks[j],
                                     attn_w_o[j], attn_b_o[j], cos, sin)
        x = x + mix
        x = x + conv_ffn(rms_norm(x, norm_ffn[i]), ffn_w_in[i], ffn_w_conv[i], ffn_w_down[i])
    return rms_norm(x, norm_final)


import jax as _jax
import jax.numpy as _jnp

TWIN_FORMAT = 'train_step'
FWD_PARAMS = ['x', 'norm_mix', 'norm_ffn', 'norm_final', 'conv_w_in', 'conv_w_conv', 'conv_w_out', 'attn_w_qkv', 'attn_b_qkv', 'attn_sinks', 'attn_w_o', 'attn_b_o', 'ffn_w_in', 'ffn_w_conv', 'ffn_w_down']
TWIN_WEIGHTS = ['norm_mix', 'norm_ffn', 'norm_final', 'conv_w_in', 'conv_w_conv', 'conv_w_out', 'attn_w_qkv', 'attn_b_qkv', 'attn_sinks', 'attn_w_o', 'attn_b_o', 'ffn_w_in', 'ffn_w_conv', 'ffn_w_down']
TWIN_DIFF_INPUT = 'x'
TWIN_INPUTS = ['x', 'norm_mix', 'norm_ffn', 'norm_final', 'conv_w_in', 'conv_w_conv', 'conv_w_out', 'attn_w_qkv', 'attn_b_qkv', 'attn_sinks', 'attn_w_o', 'attn_b_o', 'ffn_w_in', 'ffn_w_conv', 'ffn_w_down', 'loss_target', 'm_norm_mix', 'm_norm_ffn', 'm_norm_final', 'm_conv_w_in', 'm_conv_w_conv', 'm_conv_w_out', 'm_attn_w_qkv', 'm_attn_b_qkv', 'm_attn_sinks', 'm_attn_w_o', 'm_attn_b_o', 'm_ffn_w_in', 'm_ffn_w_conv', 'm_ffn_w_down', 'v_norm_mix', 'v_norm_ffn', 'v_norm_final', 'v_conv_w_in', 'v_conv_w_conv', 'v_conv_w_out', 'v_attn_w_qkv', 'v_attn_b_qkv', 'v_attn_sinks', 'v_attn_w_o', 'v_attn_b_o', 'v_ffn_w_in', 'v_ffn_w_conv', 'v_ffn_w_down']
TWIN_OUTPUTS = ['loss', 'grad_x', 'grad_norm_mix', 'grad_norm_ffn', 'grad_norm_final', 'grad_conv_w_in', 'grad_conv_w_conv', 'grad_conv_w_out', 'grad_attn_w_qkv', 'grad_attn_b_qkv', 'grad_attn_sinks', 'grad_attn_w_o', 'grad_attn_b_o', 'grad_ffn_w_in', 'grad_ffn_w_conv', 'grad_ffn_w_down', 'delta_norm_mix', 'delta_norm_ffn', 'delta_norm_final', 'delta_conv_w_in', 'delta_conv_w_conv', 'delta_conv_w_out', 'delta_attn_w_qkv', 'delta_attn_b_qkv', 'delta_attn_sinks', 'delta_attn_w_o', 'delta_attn_b_o', 'delta_ffn_w_in', 'delta_ffn_w_conv', 'delta_ffn_w_down', 'new_m_norm_mix', 'new_m_norm_ffn', 'new_m_norm_final', 'new_m_conv_w_in', 'new_m_conv_w_conv', 'new_m_conv_w_out', 'new_m_attn_w_qkv', 'new_m_attn_b_qkv', 'new_m_attn_sinks', 'new_m_attn_w_o', 'new_m_attn_b_o', 'new_m_ffn_w_in', 'new_m_ffn_w_conv', 'new_m_ffn_w_down', 'new_v_norm_mix', 'new_v_norm_ffn', 'new_v_norm_final', 'new_v_conv_w_in', 'new_v_conv_w_conv', 'new_v_conv_w_out', 'new_v_attn_w_qkv', 'new_v_attn_b_qkv', 'new_v_attn_sinks', 'new_v_attn_w_o', 'new_v_attn_b_o', 'new_v_ffn_w_in', 'new_v_ffn_w_conv', 'new_v_ffn_w_down']
TWIN_LEAF_KINDS = {'loss': 'loss', 'grad_x': 'grad_x', 'grad_norm_mix': 'grad_w', 'grad_norm_ffn': 'grad_w', 'grad_norm_final': 'grad_w', 'grad_conv_w_in': 'grad_w', 'grad_conv_w_conv': 'grad_w', 'grad_conv_w_out': 'grad_w', 'grad_attn_w_qkv': 'grad_w', 'grad_attn_b_qkv': 'grad_w', 'grad_attn_sinks': 'grad_w', 'grad_attn_w_o': 'grad_w', 'grad_attn_b_o': 'grad_w', 'grad_ffn_w_in': 'grad_w', 'grad_ffn_w_conv': 'grad_w', 'grad_ffn_w_down': 'grad_w', 'delta_norm_mix': 'delta_w', 'delta_norm_ffn': 'delta_w', 'delta_norm_final': 'delta_w', 'delta_conv_w_in': 'delta_w', 'delta_conv_w_conv': 'delta_w', 'delta_conv_w_out': 'delta_w', 'delta_attn_w_qkv': 'delta_w', 'delta_attn_b_qkv': 'delta_w', 'delta_attn_sinks': 'delta_w', 'delta_attn_w_o': 'delta_w', 'delta_attn_b_o': 'delta_w', 'delta_ffn_w_in': 'delta_w', 'delta_ffn_w_conv': 'delta_w', 'delta_ffn_w_down': 'delta_w', 'new_m_norm_mix': 'new_m', 'new_m_norm_ffn': 'new_m', 'new_m_norm_final': 'new_m', 'new_m_conv_w_in': 'new_m', 'new_m_conv_w_conv': 'new_m', 'new_m_conv_w_out': 'new_m', 'new_m_attn_w_qkv': 'new_m', 'new_m_attn_b_qkv': 'new_m', 'new_m_attn_sinks': 'new_m', 'new_m_attn_w_o': 'new_m', 'new_m_attn_b_o': 'new_m', 'new_m_ffn_w_in': 'new_m', 'new_m_ffn_w_conv': 'new_m', 'new_m_ffn_w_down': 'new_m', 'new_v_norm_mix': 'new_v', 'new_v_norm_ffn': 'new_v', 'new_v_norm_final': 'new_v', 'new_v_conv_w_in': 'new_v', 'new_v_conv_w_conv': 'new_v', 'new_v_conv_w_out': 'new_v', 'new_v_attn_w_qkv': 'new_v', 'new_v_attn_b_qkv': 'new_v', 'new_v_attn_sinks': 'new_v', 'new_v_attn_w_o': 'new_v', 'new_v_attn_b_o': 'new_v', 'new_v_ffn_w_in': 'new_v', 'new_v_ffn_w_conv': 'new_v', 'new_v_ffn_w_down': 'new_v'}


def _forward(args):
    return _fwd_reference(*[args[k] for k in FWD_PARAMS])


def _output_shape():
    out = _jax.eval_shape(lambda: _forward(_fwd_setup_inputs(0)))
    return out.shape, out.dtype

N_MICROBATCH = 1
ADAM_LR = 0.001
ADAM_B1 = 0.9
ADAM_B2 = 0.999
ADAM_EPS = 1e-08
ADAM_WD = 0.01
ADAM_STEP = 10
PER_EXAMPLE_BATCH_AXIS = {'x': 0, 'loss_target': 0}
SHARED_INPUTS = []
_WEIGHT_DTYPES = {'norm_mix': _jnp.float32, 'norm_ffn': _jnp.float32, 'norm_final': _jnp.float32, 'conv_w_in': _jnp.float32, 'conv_w_conv': _jnp.float32, 'conv_w_out': _jnp.float32, 'attn_w_qkv': _jnp.float32, 'attn_b_qkv': _jnp.float32, 'attn_sinks': _jnp.float32, 'attn_w_o': _jnp.float32, 'attn_b_o': _jnp.float32, 'ffn_w_in': _jnp.float32, 'ffn_w_conv': _jnp.float32, 'ffn_w_down': _jnp.float32}
MOMENT_SCALE = {'norm_mix': 2.764766e-01, 'norm_ffn': 1.703680e-01, 'norm_final': 6.403491e+01, 'conv_w_in': 2.237570e-01, 'conv_w_conv': 2.372324e-01, 'conv_w_out': 2.225517e-01, 'attn_w_qkv': 5.136522e-02, 'attn_b_qkv': 1.813299e-01, 'attn_sinks': 3.086283e-02, 'attn_w_o': 4.058175e-02, 'attn_b_o': 2.053953e-01, 'ffn_w_in': 6.831488e-02, 'ffn_w_conv': 6.868801e-02, 'ffn_w_down': 1.118803e-01}


def _to_microbatches(a, axis):
    t = _jnp.moveaxis(a, axis, 0)
    t = t.reshape((N_MICROBATCH, t.shape[0] // N_MICROBATCH) + t.shape[1:])
    return _jnp.moveaxis(t, 1, axis + 1)


def setup_inputs(seed: int = 0) -> dict:
    inp = _fwd_setup_inputs(seed)
    key = _jax.random.fold_in(_jax.random.key(seed), 7919)
    shape, _ = _output_shape()
    out = dict(inp)
    out["loss_target"] = _jax.random.normal(_jax.random.fold_in(key, 0), shape, _jnp.float32)
    for i, name in enumerate(TWIN_WEIGHTS):
        w = inp[name].astype(_jnp.float32)
        if MOMENT_SCALE is None:
            s = _jnp.sqrt(_jnp.mean(_jnp.square(w)) + 1e-30)
        else:
            s = MOMENT_SCALE[name]
        km, kv = _jax.random.split(_jax.random.fold_in(key, i + 1))
        out[name] = w
        out["m_" + name] = s * _jax.random.normal(km, w.shape, _jnp.float32)
        out["v_" + name] = (s * s) * _jax.random.uniform(kv, w.shape, _jnp.float32, 0.5, 1.5)
    if N_MICROBATCH > 1:
        for name, axis in PER_EXAMPLE_BATCH_AXIS.items():
            out[name] = _to_microbatches(out[name], axis)
    return {'x': out['x'], 'norm_mix': out['norm_mix'], 'norm_ffn': out['norm_ffn'], 'norm_final': out['norm_final'], 'conv_w_in': out['conv_w_in'], 'conv_w_conv': out['conv_w_conv'], 'conv_w_out': out['conv_w_out'], 'attn_w_qkv': out['attn_w_qkv'], 'attn_b_qkv': out['attn_b_qkv'], 'attn_sinks': out['attn_sinks'], 'attn_w_o': out['attn_w_o'], 'attn_b_o': out['attn_b_o'], 'ffn_w_in': out['ffn_w_in'], 'ffn_w_conv': out['ffn_w_conv'], 'ffn_w_down': out['ffn_w_down'], 'loss_target': out['loss_target'], 'm_norm_mix': out['m_norm_mix'], 'm_norm_ffn': out['m_norm_ffn'], 'm_norm_final': out['m_norm_final'], 'm_conv_w_in': out['m_conv_w_in'], 'm_conv_w_conv': out['m_conv_w_conv'], 'm_conv_w_out': out['m_conv_w_out'], 'm_attn_w_qkv': out['m_attn_w_qkv'], 'm_attn_b_qkv': out['m_attn_b_qkv'], 'm_attn_sinks': out['m_attn_sinks'], 'm_attn_w_o': out['m_attn_w_o'], 'm_attn_b_o': out['m_attn_b_o'], 'm_ffn_w_in': out['m_ffn_w_in'], 'm_ffn_w_conv': out['m_ffn_w_conv'], 'm_ffn_w_down': out['m_ffn_w_down'], 'v_norm_mix': out['v_norm_mix'], 'v_norm_ffn': out['v_norm_ffn'], 'v_norm_final': out['v_norm_final'], 'v_conv_w_in': out['v_conv_w_in'], 'v_conv_w_conv': out['v_conv_w_conv'], 'v_conv_w_out': out['v_conv_w_out'], 'v_attn_w_qkv': out['v_attn_w_qkv'], 'v_attn_b_qkv': out['v_attn_b_qkv'], 'v_attn_sinks': out['v_attn_sinks'], 'v_attn_w_o': out['v_attn_w_o'], 'v_attn_b_o': out['v_attn_b_o'], 'v_ffn_w_in': out['v_ffn_w_in'], 'v_ffn_w_conv': out['v_ffn_w_conv'], 'v_ffn_w_down': out['v_ffn_w_down']}


def _loss(weights, diff, rest, loss_target):
    with _jax.named_scope("forward"):
        args = {**rest, TWIN_DIFF_INPUT: diff, **{k: w.astype(_WEIGHT_DTYPES[k]) for k, w in weights.items()}}
        y = _forward(args)
    with _jax.named_scope("loss_head"):
        err = _jnp.square(y.astype(_jnp.float32) - loss_target)
        return 0.5 * _jnp.sum(_jnp.mean(err, axis=-1)) if err.ndim else 0.5 * err


def _adamw(w, g, m, v):
    m = ADAM_B1 * m + (1.0 - ADAM_B1) * g
    v = ADAM_B2 * v + (1.0 - ADAM_B2) * _jnp.square(g)
    m_hat = m / (1.0 - ADAM_B1 ** ADAM_STEP)
    v_hat = v / (1.0 - ADAM_B2 ** ADAM_STEP)
    delta = -ADAM_LR * (m_hat / (_jnp.sqrt(v_hat) + ADAM_EPS) + ADAM_WD * w)
    return delta, m, v


def reference(x, norm_mix, norm_ffn, norm_final, conv_w_in, conv_w_conv, conv_w_out, attn_w_qkv, attn_b_qkv, attn_sinks, attn_w_o, attn_b_o, ffn_w_in, ffn_w_conv, ffn_w_down, loss_target, m_norm_mix, m_norm_ffn, m_norm_final, m_conv_w_in, m_conv_w_conv, m_conv_w_out, m_attn_w_qkv, m_attn_b_qkv, m_attn_sinks, m_attn_w_o, m_attn_b_o, m_ffn_w_in, m_ffn_w_conv, m_ffn_w_down, v_norm_mix, v_norm_ffn, v_norm_final, v_conv_w_in, v_conv_w_conv, v_conv_w_out, v_attn_w_qkv, v_attn_b_qkv, v_attn_sinks, v_attn_w_o, v_attn_b_o, v_ffn_w_in, v_ffn_w_conv, v_ffn_w_down):
    given = dict(x=x, norm_mix=norm_mix, norm_ffn=norm_ffn, norm_final=norm_final, conv_w_in=conv_w_in, conv_w_conv=conv_w_conv, conv_w_out=conv_w_out, attn_w_qkv=attn_w_qkv, attn_b_qkv=attn_b_qkv, attn_sinks=attn_sinks, attn_w_o=attn_w_o, attn_b_o=attn_b_o, ffn_w_in=ffn_w_in, ffn_w_conv=ffn_w_conv, ffn_w_down=ffn_w_down, loss_target=loss_target, m_norm_mix=m_norm_mix, m_norm_ffn=m_norm_ffn, m_norm_final=m_norm_final, m_conv_w_in=m_conv_w_in, m_conv_w_conv=m_conv_w_conv, m_conv_w_out=m_conv_w_out, m_attn_w_qkv=m_attn_w_qkv, m_attn_b_qkv=m_attn_b_qkv, m_attn_sinks=m_attn_sinks, m_attn_w_o=m_attn_w_o, m_attn_b_o=m_attn_b_o, m_ffn_w_in=m_ffn_w_in, m_ffn_w_conv=m_ffn_w_conv, m_ffn_w_down=m_ffn_w_down, v_norm_mix=v_norm_mix, v_norm_ffn=v_norm_ffn, v_norm_final=v_norm_final, v_conv_w_in=v_conv_w_in, v_conv_w_conv=v_conv_w_conv, v_conv_w_out=v_conv_w_out, v_attn_w_qkv=v_attn_w_qkv, v_attn_b_qkv=v_attn_b_qkv, v_attn_sinks=v_attn_sinks, v_attn_w_o=v_attn_w_o, v_attn_b_o=v_attn_b_o, v_ffn_w_in=v_ffn_w_in, v_ffn_w_conv=v_ffn_w_conv, v_ffn_w_down=v_ffn_w_down)
    weights = {n: given[n] for n in TWIN_WEIGHTS}
    shared = {n: given[n] for n in SHARED_INPUTS}
    per_example = {n: given[n] for n in ['x']}
    grad_fn = _jax.value_and_grad(_loss, argnums=(0, 1))

    def one_microbatch(ex, loss_target):
        ex = dict(ex)
        diff = ex.pop(TWIN_DIFF_INPUT)
        return grad_fn(weights, diff, {**shared, **ex}, loss_target)

    if N_MICROBATCH == 1:
        loss, (grad_w, grad_x) = one_microbatch(per_example, given["loss_target"])
    else:
        def body(carry, xs):
            loss_sum, grad_sum = carry
            l_k, (gw_k, gx_k) = one_microbatch(xs[0], xs[1])
            with _jax.named_scope("update"):
                return (loss_sum + l_k, _jax.tree.map(_jnp.add, grad_sum, gw_k)), gx_k

        init = (_jnp.zeros((), _jnp.float32), _jax.tree.map(_jnp.zeros_like, weights))
        (loss, grad_w), grad_x = _jax.lax.scan(body, init, (per_example, given["loss_target"]))
    with _jax.named_scope("update"):
        delta_w, new_m, new_v = {}, {}, {}
        for n in TWIN_WEIGHTS:
            delta_w[n], new_m[n], new_v[n] = _adamw(weights[n], grad_w[n], given["m_" + n], given["v_" + n])
    return (loss, grad_x, *[grad_w[n] for n in TWIN_WEIGHTS], *[delta_w[n] for n in TWIN_WEIGHTS],
            *[new_m[n] for n in TWIN_WEIGHTS], *[new_v[n] for n in TWIN_WEIGHTS])
```

```python
import functools

import jax
import jax.numpy as jnp
from jax import lax
from jax.experimental import pallas as pl
from jax.experimental.pallas import tpu as pltpu

F32 = jnp.float32
BF16 = jnp.bfloat16
SDS = jax.ShapeDtypeStruct
MESH = pl.DeviceIdType.MESH
ANY = pl.BlockSpec(memory_space=pl.ANY)

N_DEV = 8
EPS = 1e-5
HEAD_DIM = 64
GROUP = 4
WINDOW = 128
ROPE_THETA = 10000.0
ADAM_LR, ADAM_B1, ADAM_B2, ADAM_EPS, ADAM_WD, ADAM_STEP = 0.001, 0.9, 0.999, 1e-08, 0.01, 10

V7X_VMEM_BYTES = 64 * 1024 * 1024
VMEM_LIMIT_BYTES = V7X_VMEM_BYTES - 8 * 1024 * 1024
LANES = 128
SUBLANES = 8
TOKEN_TILE = 512


def _cparams(n_axes=1):
    return pltpu.CompilerParams(dimension_semantics=("arbitrary",) * n_axes, vmem_limit_bytes=VMEM_LIMIT_BYTES)


def _resident(shape):
    zeros = (0,) * len(shape)
    return pl.BlockSpec(shape, lambda *_: zeros, pipeline_mode=pl.Buffered(1))


def _token_tile(seq):
    return min(TOKEN_TILE, seq // 2)


def _largest_divisor(m, cap, mult):
    best = None
    for d in range(mult, min(m, cap) + 1, mult):
        if m % d == 0:
            best = d
    return m if best is None else best


def _nt(a, b):
    return lax.dot_general(a, b, (((1,), (1,)), ((), ())), preferred_element_type=F32)


def _nn(a, b):
    return lax.dot_general(a, b, (((1,), (0,)), ((), ())), preferred_element_type=F32)


def _tn(a, b):
    return lax.dot_general(a, b, (((0,), (0,)), ((), ())), preferred_element_type=F32)


def _rms_parts(xv):
    r = lax.rsqrt(jnp.mean(xv * xv, axis=-1, keepdims=True) + EPS)
    return r, xv * r


def _rms_backward(dh, xh, r, gain, dres):
    u = dh * gain
    return dres + r * (u - xh * jnp.mean(u * xh, axis=-1, keepdims=True))


def _causal_conv3(ext_ref, xv, w_ref, n):
    ext_ref[8:8 + n, :] = xv
    return w_ref[2:3, :] * xv + w_ref[1:2, :] * ext_ref[7:7 + n, :] + w_ref[0:1, :] * ext_ref[6:6 + n, :]


def _anticausal_conv3(ext_ref, xv, w_ref, n):
    ext_ref[0:n, :] = xv
    sh1 = ext_ref[1:1 + n, :]
    sh2 = ext_ref[2:2 + n, :]
    return w_ref[2:3, :] * xv + w_ref[1:2, :] * sh1 + w_ref[0:1, :] * sh2, sh1, sh2


def _sigmoid(z):
    return 1.0 / (1.0 + jnp.exp(-z))


def _fwd_conv_mixer(x, gain, w_in_t, w_conv, w_out, seq, name):
    t_all, d = x.shape
    tt = _token_tile(seq)
    tps = seq // tt

    def body(x_ref, g_ref, win_ref, wc_ref, wout_ref, x1_ref, bcv_ref, ext_ref):
        i = pl.program_id(0)
        xv = x_ref[...]
        r, xh = _rms_parts(xv)
        h = (xh * g_ref[...]).astype(BF16)
        bcv = _nt(h, win_ref[...])
        bcv_ref[...] = bcv.astype(BF16)

        @pl.when(i % tps == 0)
        def _():
            ext_ref[0:8, :] = jnp.zeros((8, d), F32)

        cc = _causal_conv3(ext_ref, bcv[:, d:2 * d] * bcv[:, 2 * d:], wc_ref, tt)
        ext_ref[0:8, :] = ext_ref[tt:tt + 8, :]
        y = (bcv[:, :d] * cc).astype(BF16)
        x1_ref[...] = xv + _nn(y, wout_ref[...])

    return pl.pallas_call(
        body, name=name, grid=(t_all // tt,),
        in_specs=[pl.BlockSpec((tt, d), lambda i: (i, 0)), _resident((1, d)), _resident((3 * d, d)),
                  _resident((3, d)), _resident((d, d))],
        out_specs=[pl.BlockSpec((tt, d), lambda i: (i, 0)), pl.BlockSpec((tt, 3 * d), lambda i: (i, 0))],
        out_shape=[SDS((t_all, d), F32), SDS((t_all, 3 * d), BF16)],
        scratch_shapes=[pltpu.VMEM((tt + 8, d), F32)],
        compiler_params=_cparams(),
    )(x, gain, w_in_t, w_conv, w_out)


def _fwd_ffn(x, gain, w_in_t, w_conv, w_down, seq, name):
    t_all, d = x.shape
    f = w_down.shape[0]
    tt = _token_tile(seq) // 2
    tps = seq // tt

    def body(x_ref, g_ref, win_ref, wc_ref, wd_ref, x2_ref, gu_ref, ext_ref):
        i = pl.program_id(0)
        xv = x_ref[...]
        r, xh = _rms_parts(xv)
        h = (xh * g_ref[...]).astype(BF16)
        gu = _nt(h, win_ref[...])
        gu_ref[...] = gu.astype(BF16)

        @pl.when(i % tps == 0)
        def _():
            ext_ref[0:8, :] = jnp.zeros((8, f), F32)

        gc = _causal_conv3(ext_ref, gu[:, :f], wc_ref, tt)
        ext_ref[0:8, :] = ext_ref[tt:tt + 8, :]
        a = (gc * _sigmoid(gc) * gu[:, f:]).astype(BF16)
        x2_ref[...] = xv + _nn(a, wd_ref[...])

    return pl.pallas_call(
        body, name=name, grid=(t_all // tt,),
        in_specs=[pl.BlockSpec((tt, d), lambda i: (i, 0)), _resident((1, d)), _resident((2 * f, d)),
                  _resident((3, f)), _resident((f, d))],
        out_specs=[pl.BlockSpec((tt, d), lambda i: (i, 0)), pl.BlockSpec((tt, 2 * f), lambda i: (i, 0))],
        out_shape=[SDS((t_all, d), F32), SDS((t_all, 2 * f), BF16)],
        scratch_shapes=[pltpu.VMEM((tt + 8, f), F32)],
        compiler_params=_cparams(),
    )(x, gain, w_in_t, w_conv, w_down)


def _rope_partner(xs, lane_lo):
    return jnp.where(lane_lo, pltpu.roll(xs, LANES - HEAD_DIM // 2, 1), pltpu.roll(xs, HEAD_DIM // 2, 1))


def _fwd_qkv(x, gain, w_qkv_t, b_qkv, cos_t, sin_t, seq, name):
    t_all, d = x.shape
    width = w_qkv_t.shape[0]
    kvw = (width - d) // 2
    tt = _token_tile(seq)
    tps = seq // tt
    scale = HEAD_DIM ** -0.5

    def body(x_ref, g_ref, w_ref, b_ref, cos_ref, sin_ref, qkv_ref):
        xv = x_ref[...]
        r, xh = _rms_parts(xv)
        h = (xh * g_ref[...]).astype(BF16)
        qkv = _nt(h, w_ref[...]) + b_ref[...]
        cosv = cos_ref[...]
        sinv = sin_ref[...]
        lane_lo = (lax.broadcasted_iota(jnp.int32, (tt, LANES), 1) % HEAD_DIM) < HEAD_DIM // 2
        for s in range((d + kvw) // LANES):
            xs = qkv[:, s * LANES:(s + 1) * LANES]
            roped = xs * cosv + _rope_partner(xs, lane_lo) * sinv
            if s * LANES < d:
                roped = roped * scale
            qkv_ref[:, s * LANES:(s + 1) * LANES] = roped.astype(BF16)
        qkv_ref[:, d + kvw:] = qkv[:, d + kvw:].astype(BF16)

    return pl.pallas_call(
        body, name=name, grid=(t_all // tt,),
        in_specs=[pl.BlockSpec((tt, d), lambda i: (i, 0)), _resident((1, d)), _resident((width, d)),
                  _resident((1, width)), pl.BlockSpec((tt, LANES), lambda i: (i % tps, 0)),
                  pl.BlockSpec((tt, LANES), lambda i: (i % tps, 0))],
        out_specs=pl.BlockSpec((tt, width), lambda i: (i, 0)),
        out_shape=SDS((t_all, width), BF16),
        compiler_params=_cparams(),
    )(x, gain, w_qkv_t, b_qkv, cos_t, sin_t)


def _band_masks():
    rows = GROUP * WINDOW
    r = lax.broadcasted_iota(jnp.int32, (rows, 2 * WINDOW), 0) % WINDOW
    j = lax.broadcasted_iota(jnp.int32, (rows, 2 * WINDOW), 1)
    base = (j > r) & (j <= r + WINDOW)
    return base, base & (j >= WINDOW)


def _sink_column(sink_ref, kh):
    rows = GROUP * WINDOW
    g_of_row = lax.broadcasted_iota(jnp.int32, (rows, 1), 0) // WINDOW
    col = jnp.zeros((rows, 1), F32)
    for g in range(GROUP):
        col = jnp.where(g_of_row == g, sink_ref[kh * GROUP + g], col)
    return col


def _stack_heads(ref, row0, kh):
    return jnp.concatenate(
        [ref[row0:row0 + WINDOW, (kh * GROUP + g) * HEAD_DIM:(kh * GROUP + g + 1) * HEAD_DIM] for g in range(GROUP)],
        axis=0)


def _softmax_with_sink(s, valid, sink_col):
    s = jnp.where(valid, s, jnp.finfo(F32).min)
    m = jnp.maximum(jnp.max(s, axis=-1, keepdims=True), sink_col)
    p = jnp.exp(s - m)
    e_sink = jnp.exp(sink_col - m)
    inv = 1.0 / (jnp.sum(p, axis=-1, keepdims=True) + e_sink)
    return p * inv, e_sink * inv


def _fwd_attention(qkv, x, sinks, w_o, b_o, seq, name):
    t_all, d = x.shape
    width = qkv.shape[1]
    kvw = (width - d) // 2
    n_kv = kvw // HEAD_DIM
    tt = _token_tile(seq)
    tps = seq // tt
    nblk = tt // WINDOW

    def body(sink_ref, qkv_ref, kvp_ref, x_ref, wo_ref, bo_ref, x1_ref, o_ref, kvext_ref, oscr_ref):
        i = pl.program_id(0)
        kvext_ref[0:WINDOW, :] = kvp_ref[...]
        kvext_ref[WINDOW:, :] = qkv_ref[:, d:]
        base, base_first = _band_masks()
        not_first = jnp.full(base.shape, i % tps != 0)
        for n in range(nblk):
            valid = (base_first | (base & not_first)) if n == 0 else base
            for kh in range(n_kv):
                qs = _stack_heads(qkv_ref, n * WINDOW, kh)
                kb = kvext_ref[n * WINDOW:(n + 2) * WINDOW, kh * HEAD_DIM:(kh + 1) * HEAD_DIM]
                vb = kvext_ref[n * WINDOW:(n + 2) * WINDOW, kvw + kh * HEAD_DIM:kvw + (kh + 1) * HEAD_DIM]
                probs, _ = _softmax_with_sink(_nt(qs, kb), valid, _sink_column(sink_ref, kh))
                o_s = _nn(probs.astype(BF16), vb)
                for g in range(GROUP):
                    hd = kh * GROUP + g
                    oscr_ref[n * WINDOW:(n + 1) * WINDOW, hd * HEAD_DIM:(hd + 1) * HEAD_DIM] = (
                        o_s[g * WINDOW:(g + 1) * WINDOW].astype(BF16))
        o = oscr_ref[...]
        o_ref[...] = o
        x1_ref[...] = x_ref[...] + _nn(o, wo_ref[...]) + bo_ref[...]

    kv_blocks = tt // WINDOW
    return pl.pallas_call(
        body, name=name, grid=(t_all // tt,),
        in_specs=[pl.BlockSpec(memory_space=pltpu.SMEM),
                  pl.BlockSpec((tt, width), lambda i: (i, 0)),
                  pl.BlockSpec((WINDOW, 2 * kvw), lambda i: (jnp.maximum(i * kv_blocks - 1, 0), d // (2 * kvw))),
                  pl.BlockSpec((tt, d), lambda i: (i, 0)), _resident((d, d)), _resident((1, d))],
        out_specs=[pl.BlockSpec((tt, d), lambda i: (i, 0)), pl.BlockSpec((tt, d), lambda i: (i, 0))],
        out_shape=[SDS((t_all, d), F32), SDS((t_all, d), BF16)],
        scratch_shapes=[pltpu.VMEM((tt + WINDOW, 2 * kvw), BF16), pltpu.VMEM((tt, d), BF16)],
        compiler_params=_cparams(),
    )(sinks, qkv, qkv, x, w_o, b_o)


def _final_norm_loss(x, gain, target, name):
    t_all, d = x.shape
    tt = min(TOKEN_TILE, t_all)

    def body(x_ref, g_ref, t_ref, dx_ref, dg_ref, loss_ref):
        i = pl.program_id(0)
        xv = x_ref[...]
        r, xh = _rms_parts(xv)
        gain_v = g_ref[...]
        e = xh * gain_v - t_ref[...]
        dy = e * (1.0 / d)
        dx_ref[...] = _rms_backward(dy, xh, r, gain_v, 0.0)

        @pl.when(i == 0)
        def _():
            dg_ref[...] = jnp.zeros_like(dg_ref)
            loss_ref[...] = jnp.zeros_like(loss_ref)

        dg_ref[...] += jnp.sum(dy * xh, axis=0, keepdims=True)
        loss_ref[...] += (0.5 / d) * jnp.sum(e * e, axis=0, keepdims=True)

    return pl.pallas_call(
        body, name=name, grid=(t_all // tt,),
        in_specs=[pl.BlockSpec((tt, d), lambda i: (i, 0)), _resident((1, d)), pl.BlockSpec((tt, d), lambda i: (i, 0))],
        out_specs=[pl.BlockSpec((tt, d), lambda i: (i, 0)), pl.BlockSpec((1, d), lambda i: (0, 0)),
                   pl.BlockSpec((1, d), lambda i: (0, 0))],
        out_shape=[SDS((t_all, d), F32), SDS((1, d), F32), SDS((1, d), F32)],
        compiler_params=_cparams(),
    )(x, gain, target)


def _bwd_ffn_inner(dx2, gu, w_conv, w_down, seq, name):
    t_all, d = dx2.shape
    f = w_down.shape[0]
    tt = _token_tile(seq) // 2
    tps = seq // tt
    nt = t_all // tt
    halo = 16

    def body(dx_ref, gu_ref, prev_ref, wc_ref, wd_ref, dgu_ref, a_ref, dwc_ref, cext_ref, aext_ref):
        i = pl.program_id(0)
        ti = nt - 1 - i
        da = _nt(dx_ref[...].astype(BF16), wd_ref[...])
        gu_v = gu_ref[...].astype(F32)
        g = gu_v[:, :f]
        u = gu_v[:, f:]
        prev = prev_ref[...].astype(F32)[halo - 8:, :]
        cext_ref[0:8, :] = jnp.where(ti % tps == 0, 0.0, prev)
        gc = _causal_conv3(cext_ref, g, wc_ref, tt)
        sig = _sigmoid(gc)
        s = gc * sig
        a_ref[...] = (s * u).astype(BF16)
        dgc = da * u * (sig * (1.0 + gc * (1.0 - sig)))

        @pl.when(ti % tps == tps - 1)
        def _():
            aext_ref[tt:tt + 8, :] = jnp.zeros((8, f), F32)

        dg, sh1, sh2 = _anticausal_conv3(aext_ref, dgc, wc_ref, tt)
        aext_ref[tt:tt + 8, :] = aext_ref[0:8, :]
        dgu_ref[:, :f] = dg.astype(BF16)
        dgu_ref[:, f:] = (da * s).astype(BF16)

        @pl.when(i == 0)
        def _():
            dwc_ref[...] = jnp.zeros_like(dwc_ref)

        dwc_ref[0:1, :] += jnp.sum(g * sh2, axis=0, keepdims=True)
        dwc_ref[1:2, :] += jnp.sum(g * sh1, axis=0, keepdims=True)
        dwc_ref[2:3, :] += jnp.sum(g * dgc, axis=0, keepdims=True)

    return pl.pallas_call(
        body, name=name, grid=(nt,),
        in_specs=[pl.BlockSpec((tt, d), lambda i: (nt - 1 - i, 0)),
                  pl.BlockSpec((tt, 2 * f), lambda i: (nt - 1 - i, 0)),
                  pl.BlockSpec((halo, f), lambda i: (jnp.maximum((nt - 1 - i) * (tt // halo) - 1, 0), 0)),
                  _resident((3, f)), _resident((f, d))],
        out_specs=[pl.BlockSpec((tt, 2 * f), lambda i: (nt - 1 - i, 0)), pl.BlockSpec((tt, f), lambda i: (nt - 1 - i, 0)),
                   pl.BlockSpec((8, f), lambda i: (0, 0))],
        out_shape=[SDS((t_all, 2 * f), BF16), SDS((t_all, f), BF16), SDS((8, f), F32)],
        scratch_shapes=[pltpu.VMEM((tt + 8, f), F32), pltpu.VMEM((tt + 8, f), F32)],
        compiler_params=_cparams(),
    )(dx2, gu, gu, w_conv, w_down)


def _bwd_conv_inner(dx1, bcv, w_conv, w_out, seq, name):
    t_all, d = dx1.shape
    tt = _token_tile(seq)
    tps = seq // tt
    nt = t_all // tt
    halo = 16

    def body(dx_ref, bcv_ref, prev_ref, wc_ref, wout_ref, dbcv_ref, y_ref, dwc_ref, cext_ref, aext_ref):
        i = pl.program_id(0)
        ti = nt - 1 - i
        dy = _nt(dx_ref[...].astype(BF16), wout_ref[...])
        bcv_v = bcv_ref[...].astype(F32)
        b = bcv_v[:, :d]
        c = bcv_v[:, d:2 * d]
        v = bcv_v[:, 2 * d:]
        cv = c * v
        prev = prev_ref[...].astype(F32)[halo - 8:, :]
        cext_ref[0:8, :] = jnp.where(ti % tps == 0, 0.0, prev[:, d:2 * d] * prev[:, 2 * d:])
        cc = _causal_conv3(cext_ref, cv, wc_ref, tt)
        y_ref[...] = (b * cc).astype(BF16)
        dcc = dy * b

        @pl.when(ti % tps == tps - 1)
        def _():
            aext_ref[tt:tt + 8, :] = jnp.zeros((8, d), F32)

        dcv, sh1, sh2 = _anticausal_conv3(aext_ref, dcc, wc_ref, tt)
        aext_ref[tt:tt + 8, :] = aext_ref[0:8, :]
        dbcv_ref[:, :d] = (dy * cc).astype(BF16)
        dbcv_ref[:, d:2 * d] = (dcv * v).astype(BF16)
        dbcv_ref[:, 2 * d:] = (dcv * c).astype(BF16)

        @pl.when(i == 0)
        def _():
            dwc_ref[...] = jnp.zeros_like(dwc_ref)

        dwc_ref[0:1, :] += jnp.sum(cv * sh2, axis=0, keepdims=True)
        dwc_ref[1:2, :] += jnp.sum(cv * sh1, axis=0, keepdims=True)
        dwc_ref[2:3, :] += jnp.sum(cv * dcc, axis=0, keepdims=True)

    return pl.pallas_call(
        body, name=name, grid=(nt,),
        in_specs=[pl.BlockSpec((tt, d), lambda i: (nt - 1 - i, 0)),
                  pl.BlockSpec((tt, 3 * d), lambda i: (nt - 1 - i, 0)),
                  pl.BlockSpec((halo, 3 * d), lambda i: (jnp.maximum((nt - 1 - i) * (tt // halo) - 1, 0), 0)),
                  _resident((3, d)), _resident((d, d))],
        out_specs=[pl.BlockSpec((tt, 3 * d), lambda i: (nt - 1 - i, 0)), pl.BlockSpec((tt, d), lambda i: (nt - 1 - i, 0)),
                   pl.BlockSpec((8, d), lambda i: (0, 0))],
        out_shape=[SDS((t_all, 3 * d), BF16), SDS((t_all, d), BF16), SDS((8, d), F32)],
        scratch_shapes=[pltpu.VMEM((tt + 8, d), F32), pltpu.VMEM((tt + 8, d), F32)],
        compiler_params=_cparams(),
    )(dx1, bcv, bcv, w_conv, w_out)


def _bwd_attention_inner(dx1, qkv, sinks, w_o, cos_t, sin_t, seq, name):
    t_all, d = dx1.shape
    width = qkv.shape[1]
    kvw = (width - d) // 2
    n_kv = kvw // HEAD_DIM
    tt = _token_tile(seq)
    tps = seq // tt
    nt = t_all // tt
    nblk = tt // WINDOW
    scale = HEAD_DIM ** -0.5

    def body(sink_ref, dx_ref, qkv_ref, kvp_ref, cos_ref, sin_ref, wo_ref,
             dqkv_ref, dsink_ref, dbqkv_ref, dbo_ref,
             kvext_ref, dkvext_ref, carry_ref, dq_ref, do_ref):
        i = pl.program_id(0)
        ti = nt - 1 - i
        dxv = dx_ref[...]
        do_ref[...] = _nt(dxv.astype(BF16), wo_ref[...]).astype(BF16)
        kvext_ref[0:WINDOW, :] = kvp_ref[...]
        kvext_ref[WINDOW:, :] = qkv_ref[:, d:]
        dkvext_ref[...] = jnp.zeros_like(dkvext_ref)

        @pl.when(i == 0)
        def _():
            carry_ref[...] = jnp.zeros_like(carry_ref)
            dsink_ref[...] = jnp.zeros_like(dsink_ref)
            dbqkv_ref[...] = jnp.zeros_like(dbqkv_ref)
            dbo_ref[...] = jnp.zeros_like(dbo_ref)

        base, base_first = _band_masks()
        not_first = jnp.full(base.shape, ti % tps != 0)
        head_lane = lax.broadcasted_iota(jnp.int32, (1, LANES), 1)
        dsink = jnp.zeros((1, LANES), F32)
        for n in range(nblk):
            valid = (base_first | (base & not_first)) if n == 0 else base
            for kh in range(n_kv):
                qs = _stack_heads(qkv_ref, n * WINDOW, kh)
                dos = _stack_heads(do_ref, n * WINDOW, kh)
                kcols = slice(kh * HEAD_DIM, (kh + 1) * HEAD_DIM)
                vcols = slice(kvw + kh * HEAD_DIM, kvw + (kh + 1) * HEAD_DIM)
                band = slice(n * WINDOW, (n + 2) * WINDOW)
                kb = kvext_ref[band, kcols]
                vb = kvext_ref[band, vcols]
                probs, p_sink = _softmax_with_sink(_nt(qs, kb), valid, _sink_column(sink_ref, kh))
                dp = _nt(dos, vb)
                dsum = jnp.sum(probs * dp, axis=-1, keepdims=True)
                ds = (probs * (dp - dsum)).astype(BF16)
                dkvext_ref[band, vcols] += _tn(probs.astype(BF16), dos)
                dkvext_ref[band, kcols] += _tn(ds, qs)
                dq_s = _nn(ds, kb)
                sink_terms = p_sink * dsum
                for g in range(GROUP):
                    hd = kh * GROUP + g
                    dq_ref[n * WINDOW:(n + 1) * WINDOW, hd * HEAD_DIM:(hd + 1) * HEAD_DIM] = dq_s[g * WINDOW:(g + 1) * WINDOW]
                    dsink = dsink - jnp.where(head_lane == hd, jnp.sum(sink_terms[g * WINDOW:(g + 1) * WINDOW]), 0.0)
        dsink_ref[0:1, :] += dsink
        dkvext_ref[tt:tt + WINDOW, :] += carry_ref[...]
        carry_ref[...] = dkvext_ref[0:WINDOW, :]

        cosv = cos_ref[...]
        sinv = sin_ref[...]
        lane_lo = (lax.broadcasted_iota(jnp.int32, (tt, LANES), 1) % HEAD_DIM) < HEAD_DIM // 2
        for s in range((d + kvw) // LANES):
            if s * LANES < d:
                dy = dq_ref[:, s * LANES:(s + 1) * LANES] * scale
            else:
                dy = dkvext_ref[WINDOW:, s * LANES - d:(s + 1) * LANES - d]
            dpre = dy * cosv - _rope_partner(dy, lane_lo) * sinv
            dqkv_ref[:, s * LANES:(s + 1) * LANES] = dpre.astype(BF16)
            dbqkv_ref[0:1, s * LANES:(s + 1) * LANES] += jnp.sum(dpre, axis=0, keepdims=True)
        dv = dkvext_ref[WINDOW:, kvw:]
        dqkv_ref[:, d + kvw:] = dv.astype(BF16)
        dbqkv_ref[0:1, d + kvw:] += jnp.sum(dv, axis=0, keepdims=True)
        dbo_ref[...] += jnp.sum(dxv, axis=0, keepdims=True)

    kv_blocks = tt // WINDOW
    return pl.pallas_call(
        body, name=name, grid=(nt,),
        in_specs=[pl.BlockSpec(memory_space=pltpu.SMEM),
                  pl.BlockSpec((tt, d), lambda i: (nt - 1 - i, 0)),
                  pl.BlockSpec((tt, width), lambda i: (nt - 1 - i, 0)),
                  pl.BlockSpec((WINDOW, 2 * kvw), lambda i: (jnp.maximum((nt - 1 - i) * kv_blocks - 1, 0), d // (2 * kvw))),
                  pl.BlockSpec((tt, LANES), lambda i: ((nt - 1 - i) % tps, 0)),
                  pl.BlockSpec((tt, LANES), lambda i: ((nt - 1 - i) % tps, 0)),
                  _resident((d, d))],
        out_specs=[pl.BlockSpec((tt, width), lambda i: (nt - 1 - i, 0)), pl.BlockSpec((8, LANES), lambda i: (0, 0)),
                   pl.BlockSpec((1, width), lambda i: (0, 0)), pl.BlockSpec((1, d), lambda i: (0, 0))],
        out_shape=[SDS((t_all, width), BF16), SDS((8, LANES), F32), SDS((1, width), F32), SDS((1, d), F32)],
        scratch_shapes=[pltpu.VMEM((tt + WINDOW, 2 * kvw), BF16), pltpu.VMEM((tt + WINDOW, 2 * kvw), F32),
                        pltpu.VMEM((WINDOW, 2 * kvw), F32), pltpu.VMEM((tt, d), F32), pltpu.VMEM((tt, d), BF16)],
        compiler_params=_cparams(),
    )(sinks, dx1, qkv, qkv, cos_t, sin_t, w_o)


def _bwd_dense_norm(dy, w_t, x, gain, dres, name):
    t_all, d = x.shape
    n = dy.shape[1]
    tt = min(TOKEN_TILE, t_all)

    def body(dy_ref, w_ref, x_ref, g_ref, dres_ref, dx_ref, h_ref, dg_ref):
        i = pl.program_id(0)
        dh = _nn(dy_ref[...], w_ref[...])
        r, xh = _rms_parts(x_ref[...])
        gain_v = g_ref[...]
        h_ref[...] = (xh * gain_v).astype(BF16)
        dx_ref[...] = _rms_backward(dh, xh, r, gain_v, dres_ref[...])

        @pl.when(i == 0)
        def _():
            dg_ref[...] = jnp.zeros_like(dg_ref)

        dg_ref[...] += jnp.sum(dh * xh, axis=0, keepdims=True)

    return pl.pallas_call(
        body, name=name, grid=(t_all // tt,),
        in_specs=[pl.BlockSpec((tt, n), lambda i: (i, 0)), _resident((n, d)), pl.BlockSpec((tt, d), lambda i: (i, 0)),
                  _resident((1, d)), pl.BlockSpec((tt, d), lambda i: (i, 0))],
        out_specs=[pl.BlockSpec((tt, d), lambda i: (i, 0)), pl.BlockSpec((tt, d), lambda i: (i, 0)),
                   pl.BlockSpec((1, d), lambda i: (0, 0))],
        out_shape=[SDS((t_all, d), F32), SDS((t_all, d), BF16), SDS((1, d), F32)],
        compiler_params=_cparams(),
    )(dy, w_t, x, gain, dres)


def _tn_matmul(a, b, name):
    t_all, m = a.shape
    d = b.shape[1]
    tm = _largest_divisor(m, 1536, LANES)
    tt = min(TOKEN_TILE, t_all)
    n_t = t_all // tt

    def body(a_ref, b_ref, o_ref, acc_ref):
        t = pl.program_id(1)

        @pl.when(t == 0)
        def _():
            acc_ref[...] = jnp.zeros_like(acc_ref)

        acc_ref[...] += _tn(a_ref[...], b_ref[...].astype(BF16))

        @pl.when(t == n_t - 1)
        def _():
            o_ref[...] = acc_ref[...].astype(BF16)

    return pl.pallas_call(
        body, name=name, grid=(m // tm, n_t),
        in_specs=[pl.BlockSpec((tt, tm), lambda j, t: (t, j)), pl.BlockSpec((tt, d), lambda j, t: (t, 0))],
        out_specs=pl.BlockSpec((tm, d), lambda j, t: (j, 0)),
        out_shape=SDS((m, d), BF16),
        scratch_shapes=[pltpu.VMEM((tm, d), F32)],
        compiler_params=_cparams(2),
    )(a, b)


def _my_place():
    return lax.axis_index("x"), lax.axis_index("y"), lax.axis_index("c")


def _other_chips(x, y):
    return [(1 - x, y), (x, 1 - y), (1 - x, 1 - y)]


def _all_gather(blocks, name):
    n_arr = len(blocks)

    def body(*refs):
        in_refs = refs[:n_arr]
        out_refs = refs[n_arr:2 * n_arr]
        send_sems, recv_sems, local_sems = refs[2 * n_arr:]
        x, y, c = _my_place()
        me, sibling = (x, y, c), (x, y, 1 - c)
        chips = _other_chips(x, y)

        def slot(a, place):
            px, py, pc = place
            return out_refs[a].at[4 * px + 2 * py + pc]

        def copy(a, k, block, to, src=None):
            return pltpu.make_async_remote_copy(
                src_ref=slot(a, block) if src is None else src, dst_ref=slot(a, block),
                send_sem=send_sems.at[a, k], recv_sem=recv_sems.at[a, k], device_id=to, device_id_type=MESH)

        started = []
        local = []
        for a in range(n_arr):
            mine = pltpu.make_async_copy(in_refs[a], slot(a, me), local_sems.at[a])
            mine.start()
            local.append(mine)
            first = [copy(a, 0, me, sibling, src=in_refs[a])]
            first += [copy(a, 1 + j, me, (*chip, c), src=in_refs[a]) for j, chip in enumerate(chips)]
            for cp in first:
                cp.start()
            started += first
        for a in range(n_arr):
            for j, chip in enumerate(chips):
                copy(a, 1 + j, (*chip, c), me).wait_recv()
                passed = copy(a, 4 + j, (*chip, c), sibling)
                passed.start()
                started.append(passed)
        for a in range(n_arr):
            copy(a, 0, sibling, me).wait_recv()
            for j, chip in enumerate(chips):
                copy(a, 4 + j, (*chip, 1 - c), me).wait_recv()
        for cp in started:
            cp.wait_send()
        for mine in local:
            mine.wait()

    return pl.pallas_call(
        body, name=name,
        in_specs=[ANY] * n_arr, out_specs=[ANY] * n_arr,
        out_shape=[SDS((N_DEV,) + b.shape, b.dtype) for b in blocks],
        scratch_shapes=[pltpu.SemaphoreType.DMA((n_arr, 7)), pltpu.SemaphoreType.DMA((n_arr, 7)),
                        pltpu.SemaphoreType.DMA((n_arr,))],
    )(*blocks)


def _exchange_with_sibling(parts, name):
    _, rows, d = parts.shape

    def body(p_ref, land_ref, send_sems, recv_sems):
        x, y, c = _my_place()
        sibling = (x, y, 1 - c)
        copies = [pltpu.make_async_remote_copy(
            src_ref=p_ref.at[2 * k + (1 - c)], dst_ref=land_ref.at[k], send_sem=send_sems.at[k], recv_sem=recv_sems.at[k],
            device_id=sibling, device_id_type=MESH) for k in range(4)]
        for cp in copies:
            cp.start()
        for cp in copies:
            cp.wait_recv()
        for cp in copies:
            cp.wait_send()

    return pl.pallas_call(
        body, name=name, in_specs=[ANY], out_specs=ANY, out_shape=SDS((4, rows, d), parts.dtype),
        scratch_shapes=[pltpu.SemaphoreType.DMA((4,)), pltpu.SemaphoreType.DMA((4,))],
    )(parts)


def _add_sibling(parts, landed, core, name):
    _, rows, d = parts.shape
    tr = _largest_divisor(rows, 512, 16)

    def body(core_ref, p_ref, l_ref, q_ref):
        q_ref[...] = (p_ref[...].astype(F32) + l_ref[...].astype(F32)).astype(BF16)

    return pl.pallas_call(
        body, name=name,
        grid_spec=pltpu.PrefetchScalarGridSpec(
            num_scalar_prefetch=1, grid=(4, rows // tr),
            in_specs=[pl.BlockSpec((None, tr, d), lambda k, r, core_ref: (2 * k + core_ref[0], r, 0)),
                      pl.BlockSpec((None, tr, d), lambda k, r, core_ref: (k, r, 0))],
            out_specs=pl.BlockSpec((None, tr, d), lambda k, r, core_ref: (k, r, 0))),
        out_shape=SDS((4, rows, d), BF16),
        compiler_params=_cparams(2),
    )(core, parts, landed)


def _exchange_with_chips(q, name):
    _, rows, d = q.shape

    def body(q_ref, land_ref, send_sems, recv_sems):
        x, y, c = _my_place()
        copies = [pltpu.make_async_remote_copy(
            src_ref=q_ref.at[2 * px + py], dst_ref=land_ref.at[j], send_sem=send_sems.at[j], recv_sem=recv_sems.at[j],
            device_id=(px, py, c), device_id_type=MESH) for j, (px, py) in enumerate(_other_chips(x, y))]
        for cp in copies:
            cp.start()
        for cp in copies:
            cp.wait_recv()
        for cp in copies:
            cp.wait_send()

    return pl.pallas_call(
        body, name=name, in_specs=[ANY], out_specs=ANY, out_shape=SDS((3, rows, d), q.dtype),
        scratch_shapes=[pltpu.SemaphoreType.DMA((3,)), pltpu.SemaphoreType.DMA((3,))],
    )(q)


def _add_chips(q, landed, chip, name):
    _, rows, d = q.shape
    tr = _largest_divisor(rows, 512, 16)

    def body(chip_ref, q_ref, l_ref, o_ref):
        o_ref[...] = ((q_ref[...].astype(F32) + l_ref[0].astype(F32)) + l_ref[1].astype(F32)) + l_ref[2].astype(F32)

    return pl.pallas_call(
        body, name=name,
        grid_spec=pltpu.PrefetchScalarGridSpec(
            num_scalar_prefetch=1, grid=(rows // tr,),
            in_specs=[pl.BlockSpec((None, tr, d), lambda r, chip_ref: (chip_ref[0], r, 0)),
                      pl.BlockSpec((3, tr, d), lambda r, chip_ref: (0, r, 0))],
            out_specs=pl.BlockSpec((tr, d), lambda r, chip_ref: (r, 0))),
        out_shape=SDS((rows, d), F32),
        compiler_params=_cparams(1),
    )(chip, q, landed)


def _all_reduce_small(part, loss_rows, name):
    rows, lanes = part.shape
    lo, hi = loss_rows

    def body(x_ref, out_ref, loss_ref, gath_ref, send_sems, recv_sems):
        x, y, c = _my_place()
        me = 4 * x + 2 * y + c
        gath_ref[me] = x_ref[...]
        copies = []
        for k in range(1, N_DEV):
            peer = (x ^ ((k >> 2) & 1), y ^ ((k >> 1) & 1), c ^ (k & 1))
            cp = pltpu.make_async_remote_copy(
                src_ref=x_ref, dst_ref=gath_ref.at[me], send_sem=send_sems.at[k - 1], recv_sem=recv_sems.at[k - 1],
                device_id=peer, device_id_type=MESH)
            cp.start()
            copies.append(cp)
        for cp in copies:
            cp.wait_recv()
        for cp in copies:
            cp.wait_send()
        acc = gath_ref[0]
        for dev in range(1, N_DEV):
            acc = acc + gath_ref[dev]
        out_ref[...] = acc
        loss_ref[...] = jnp.full(loss_ref.shape, jnp.sum(acc[lo:hi, :]), F32)

    vmem = pl.BlockSpec(memory_space=pltpu.VMEM)
    return pl.pallas_call(
        body, name=name, in_specs=[vmem], out_specs=[vmem, vmem],
        out_shape=[SDS((rows, lanes), F32), SDS((SUBLANES, LANES), F32)],
        scratch_shapes=[pltpu.VMEM((N_DEV, rows, lanes), F32), pltpu.SemaphoreType.DMA((N_DEV - 1,)),
                        pltpu.SemaphoreType.DMA((N_DEV - 1,))],
    )(part)


def _adamw(w, g, m, v, name):
    rows, cols = w.shape
    tr = rows if rows % SUBLANES else _largest_divisor(rows, 512, SUBLANES)

    def body(w_ref, g_ref, m_ref, v_ref, d_ref, nm_ref, nv_ref):
        gv = g_ref[...]
        nm = ADAM_B1 * m_ref[...] + (1.0 - ADAM_B1) * gv
        nv = ADAM_B2 * v_ref[...] + (1.0 - ADAM_B2) * (gv * gv)
        m_hat = nm / (1.0 - ADAM_B1 ** ADAM_STEP)
        v_hat = nv / (1.0 - ADAM_B2 ** ADAM_STEP)
        d_ref[...] = -ADAM_LR * (m_hat / (jnp.sqrt(v_hat) + ADAM_EPS) + ADAM_WD * w_ref[...])
        nm_ref[...] = nm
        nv_ref[...] = nv

    spec = pl.BlockSpec((tr, cols), lambda i: (i, 0))
    return pl.pallas_call(
        body, name=name, grid=(rows // tr,), in_specs=[spec] * 4, out_specs=[spec] * 3,
        out_shape=[SDS((rows, cols), F32)] * 3, compiler_params=_cparams(),
    )(w, g, m, v)


def _adamw_nd(w, g, m, v, name):
    shape = w.shape
    two_d = (1, shape[0]) if len(shape) == 1 else (-1, shape[-1])
    outs = _adamw(w.reshape(two_d), g.reshape(two_d), m.reshape(two_d), v.reshape(two_d), name)
    return [o.reshape(shape) for o in outs]


def _rope_tables(seq):
    pos = jnp.arange(seq, dtype=F32)
    inv_freq = 1.0 / (ROPE_THETA ** (jnp.arange(0, HEAD_DIM, 2, dtype=F32) / HEAD_DIM))
    ang = pos[:, None] * inv_freq[None, :]
    cos, sin = jnp.cos(ang), jnp.sin(ang)
    reps = LANES // HEAD_DIM
    cos_t = jnp.tile(jnp.concatenate([cos, cos], axis=1), (1, reps))
    sin_t = jnp.tile(jnp.concatenate([-sin, sin], axis=1), (1, reps))
    return cos_t, sin_t


def _flat_pad(a):
    flat = a.reshape(1, -1)
    pad = (-flat.shape[1]) % LANES
    return jnp.pad(flat, ((0, 0), (0, pad))) if pad else flat


def kernel(x, norm_mix, norm_ffn, norm_final, conv_w_in, conv_w_conv, conv_w_out, attn_w_qkv, attn_b_qkv, attn_sinks, attn_w_o, attn_b_o, ffn_w_in, ffn_w_conv, ffn_w_down, loss_target, m_norm_mix, m_norm_ffn, m_norm_final, m_conv_w_in, m_conv_w_conv, m_conv_w_out, m_attn_w_qkv, m_attn_b_qkv, m_attn_sinks, m_attn_w_o, m_attn_b_o, m_ffn_w_in, m_ffn_w_conv, m_ffn_w_down, v_norm_mix, v_norm_ffn, v_norm_final, v_conv_w_in, v_conv_w_conv, v_conv_w_out, v_attn_w_qkv, v_attn_b_qkv, v_attn_sinks, v_attn_w_o, v_attn_b_o, v_ffn_w_in, v_ffn_w_conv, v_ffn_w_down):
    b_loc, seq, d = x.shape
    depth = norm_mix.shape[0]
    n_conv, n_attn = conv_w_in.shape[0], attn_w_qkv.shape[0]
    t_all = b_loc * seq
    my_x, my_y, my_c = _my_place()

    big = []
    pieces = []
    for j in range(n_conv):
        big.append(("conv_w_in", j, True)); pieces.append(conv_w_in[j].T)
        big.append(("conv_w_out", j, False)); pieces.append(conv_w_out[j])
    for j in range(n_attn):
        big.append(("attn_w_qkv", j, True)); pieces.append(attn_w_qkv[j].T)
        big.append(("attn_w_o", j, False)); pieces.append(attn_w_o[j])
    for i in range(depth):
        big.append(("ffn_w_in", i, True)); pieces.append(ffn_w_in[i].T)
        big.append(("ffn_w_down", i, False)); pieces.append(ffn_w_down[i])
    offsets = []
    off = 0
    for p in pieces:
        offsets.append((off, p.shape[0]))
        off += p.shape[0]
    pack = jnp.concatenate(pieces, axis=0).astype(BF16)
    small = jnp.concatenate([_flat_pad(conv_w_conv), _flat_pad(ffn_w_conv), _flat_pad(attn_b_qkv), _flat_pad(attn_b_o)], axis=1)
    gathered, small_g = _all_gather([pack, small], "gather_weights")

    weights = {}
    for (wname, layer, _), (o, n) in zip(big, offsets):
        weights[(wname, layer)] = gathered[:, o:o + n, :].reshape(N_DEV * n, d)

    def take_small(o, shape):
        size = shape[0] * shape[1] * shape[2]
        blk = small_g[:, 0, o:o + size].reshape((N_DEV,) + shape)
        return jnp.moveaxis(blk, 0, 2).reshape(shape[0], shape[1], N_DEV * shape[2])

    so = 0
    wc_conv_full = take_small(so, conv_w_conv.shape); so += _flat_pad(conv_w_conv).shape[1]
    wc_ffn_full = take_small(so, ffn_w_conv.shape); so += _flat_pad(ffn_w_conv).shape[1]
    b_qkv_full = take_small(so, (n_attn, 1, attn_b_qkv.shape[1]))[:, 0]; so += _flat_pad(attn_b_qkv).shape[1]
    b_o_full = take_small(so, (n_attn, 1, attn_b_o.shape[1]))[:, 0]

    cos_t, sin_t = _rope_tables(seq)

    xs = [x.reshape(t_all, d)]
    saved = []
    for i in range(depth):
        j = i // 2
        gain_mix = norm_mix[i][None, :]
        if i % 2 == 0:
            x1, bcv = _fwd_conv_mixer(xs[-1], gain_mix, weights[("conv_w_in", j)], wc_conv_full[j],
                                      weights[("conv_w_out", j)], seq, f"fwd_conv_{i}")
            mix_saved = (bcv,)
        else:
            qkv = _fwd_qkv(xs[-1], gain_mix, weights[("attn_w_qkv", j)], b_qkv_full[j][None, :], cos_t, sin_t, seq,
                           f"fwd_qkv_{i}")
            x1, o = _fwd_attention(qkv, xs[-1], attn_sinks[j], weights[("attn_w_o", j)], b_o_full[j][None, :], seq,
                                   f"fwd_attn_{i}")
            mix_saved = (qkv, o)
        x2, gu = _fwd_ffn(x1, norm_ffn[i][None, :], weights[("ffn_w_in", i)], wc_ffn_full[i], weights[("ffn_w_down", i)],
                          seq, f"fwd_ffn_{i}")
        saved.append((xs[-1], x1, mix_saved, gu))
        xs.append(x2)

    dx, dg_final, loss_lanes = _final_norm_loss(xs[-1], norm_final[None, :], loss_target.reshape(t_all, d), "loss_head")

    grads_t = {}
    dg_mix, dg_ffn = [None] * depth, [None] * depth
    dwc_conv, dwc_ffn = [None] * n_conv, [None] * depth
    db_qkv, db_o, dsinks = [None] * n_attn, [None] * n_attn, [None] * n_attn
    for i in reversed(range(depth)):
        j = i // 2
        x0, x1, mix_saved, gu = saved[i]
        dgu, act, dwc = _bwd_ffn_inner(dx, gu, wc_ffn_full[i], weights[("ffn_w_down", i)], seq, f"bwd_ffn_{i}")
        dwc_ffn[i] = dwc[:3]
        dx1, h2, dg_ffn[i] = _bwd_dense_norm(dgu, weights[("ffn_w_in", i)], x1, norm_ffn[i][None, :], dx, f"bwd_ffn_norm_{i}")
        grads_t[("ffn_w_in", i)] = _tn_matmul(dgu, h2, f"dw_ffn_in_{i}")
        grads_t[("ffn_w_down", i)] = _tn_matmul(act, dx, f"dw_ffn_down_{i}")
        if i % 2 == 0:
            (bcv,) = mix_saved
            dbcv, y, dwc = _bwd_conv_inner(dx1, bcv, wc_conv_full[j], weights[("conv_w_out", j)], seq, f"bwd_conv_{i}")
            dwc_conv[j] = dwc[:3]
            dx, h, dg_mix[i] = _bwd_dense_norm(dbcv, weights[("conv_w_in", j)], x0, norm_mix[i][None, :], dx1,
                                               f"bwd_conv_norm_{i}")
            grads_t[("conv_w_in", j)] = _tn_matmul(dbcv, h, f"dw_conv_in_{i}")
            grads_t[("conv_w_out", j)] = _tn_matmul(y, dx1, f"dw_conv_out_{i}")
        else:
            qkv, o = mix_saved
            dqkv, dsk, dbq, dbo = _bwd_attention_inner(dx1, qkv, attn_sinks[j], weights[("attn_w_o", j)], cos_t, sin_t, seq,
                                                       f"bwd_attn_{i}")
            dsinks[j], db_qkv[j], db_o[j] = dsk[0:1, :attn_sinks.shape[1]], dbq, dbo
            dx, h, dg_mix[i] = _bwd_dense_norm(dqkv, weights[("attn_w_qkv", j)], x0, norm_mix[i][None, :], dx1,
                                               f"bwd_attn_norm_{i}")
            grads_t[("attn_w_qkv", j)] = _tn_matmul(dqkv, h, f"dw_attn_qkv_{i}")
            grads_t[("attn_w_o", j)] = _tn_matmul(o, dx1, f"dw_attn_o_{i}")
    grad_x = dx.reshape(b_loc, seq, d)

    parts = jnp.concatenate([grads_t[(wname, layer)].reshape(N_DEV, n, d) for (wname, layer, _), (_, n) in zip(big, offsets)],
                            axis=1)
    core = jnp.reshape(my_c, (1,)).astype(jnp.int32)
    chip = jnp.reshape(2 * my_x + my_y, (1,)).astype(jnp.int32)
    landed = _exchange_with_sibling(parts, "reduce_sibling")
    q = _add_sibling(parts, landed, core, "reduce_sibling_add")
    landed2 = _exchange_with_chips(q, "reduce_chips")
    reduced = _add_chips(q, landed2, chip, "reduce_chips_add")

    small_parts = [jnp.concatenate(dg_mix, axis=0), jnp.concatenate(dg_ffn, axis=0), dg_final,
                   jnp.stack(dwc_conv), jnp.stack(dwc_ffn), jnp.concatenate(db_qkv, axis=0), jnp.concatenate(db_o, axis=0),
                   jnp.concatenate(dsinks, axis=0), loss_lanes]
    flats = [_flat_pad(p) for p in small_parts]
    bounds = []
    so = 0
    for fl in flats:
        bounds.append((so, so + fl.shape[1]))
        so += fl.shape[1]
    small_rows = so // LANES
    pad_rows = (-small_rows) % SUBLANES
    part_small = jnp.pad(jnp.concatenate(flats, axis=1).reshape(small_rows, LANES), ((0, pad_rows), (0, 0)))
    loss_rows = (bounds[-1][0] // LANES, bounds[-1][1] // LANES)
    summed, loss_tile = _all_reduce_small(part_small, loss_rows, "reduce_small")
    summed = summed.reshape(1, -1)

    def small_grad(k, shape):
        lo = bounds[k][0]
        size = 1
        for s_ in shape:
            size *= s_
        return summed[0, lo:lo + size].reshape(shape)

    me = 4 * my_x + 2 * my_y + my_c

    def my_cols(full, n_local):
        return lax.dynamic_slice_in_dim(full, me * n_local, n_local, axis=full.ndim - 1)

    g_norm_mix = small_grad(0, norm_mix.shape)
    g_norm_ffn = small_grad(1, norm_ffn.shape)
    g_norm_final = small_grad(2, norm_final.shape)
    g_conv_w_conv = my_cols(small_grad(3, (n_conv, 3, d)), conv_w_conv.shape[2])
    g_ffn_w_conv = my_cols(small_grad(4, (depth, 3, ffn_w_conv.shape[2] * N_DEV)), ffn_w_conv.shape[2])
    g_attn_b_qkv = my_cols(small_grad(5, (n_attn, attn_b_qkv.shape[1] * N_DEV)), attn_b_qkv.shape[1])
    g_attn_b_o = my_cols(small_grad(6, (n_attn, d)), attn_b_o.shape[1])
    g_attn_sinks = small_grad(7, attn_sinks.shape)
    loss = loss_tile[0, 0]

    def big_grad(wname, n_layers):
        out = []
        for layer in range(n_layers):
            idx = [k for k, (nm, ly, _) in enumerate(big) if nm == wname and ly == layer][0]
            o, n = offsets[idx]
            blk = reduced[o:o + n]
            out.append(blk.T if big[idx][2] else blk)
        return jnp.stack(out)

    grads = {
        "norm_mix": g_norm_mix, "norm_ffn": g_norm_ffn, "norm_final": g_norm_final,
        "conv_w_in": big_grad("conv_w_in", n_conv), "conv_w_conv": g_conv_w_conv, "conv_w_out": big_grad("conv_w_out", n_conv),
        "attn_w_qkv": big_grad("attn_w_qkv", n_attn), "attn_b_qkv": g_attn_b_qkv, "attn_sinks": g_attn_sinks,
        "attn_w_o": big_grad("attn_w_o", n_attn), "attn_b_o": g_attn_b_o,
        "ffn_w_in": big_grad("ffn_w_in", depth), "ffn_w_conv": g_ffn_w_conv, "ffn_w_down": big_grad("ffn_w_down", depth),
    }
    params = {
        "norm_mix": (norm_mix, m_norm_mix, v_norm_mix), "norm_ffn": (norm_ffn, m_norm_ffn, v_norm_ffn),
        "norm_final": (norm_final, m_norm_final, v_norm_final), "conv_w_in": (conv_w_in, m_conv_w_in, v_conv_w_in),
        "conv_w_conv": (conv_w_conv, m_conv_w_conv, v_conv_w_conv), "conv_w_out": (conv_w_out, m_conv_w_out, v_conv_w_out),
        "attn_w_qkv": (attn_w_qkv, m_attn_w_qkv, v_attn_w_qkv), "attn_b_qkv": (attn_b_qkv, m_attn_b_qkv, v_attn_b_qkv),
        "attn_sinks": (attn_sinks, m_attn_sinks, v_attn_sinks), "attn_w_o": (attn_w_o, m_attn_w_o, v_attn_w_o),
        "attn_b_o": (attn_b_o, m_attn_b_o, v_attn_b_o), "ffn_w_in": (ffn_w_in, m_ffn_w_in, v_ffn_w_in),
        "ffn_w_conv": (ffn_w_conv, m_ffn_w_conv, v_ffn_w_conv), "ffn_w_down": (ffn_w_down, m_ffn_w_down, v_ffn_w_down),
    }
    order = list(params)
    deltas, new_ms, new_vs = [], [], []
    for wname in order:
        w, m, v = params[wname]
        dlt, nm, nv = _adamw_nd(w, grads[wname], m, v, f"adamw_{wname}")
        deltas.append(dlt); new_ms.append(nm); new_vs.append(nv)
    return (loss, grad_x, *[grads[wname] for wname in order], *deltas, *new_ms, *new_vs)
```

```python
import functools

import jax
import jax.numpy as jnp
from jax import lax
from jax.experimental import pallas as pl
from jax.experimental.pallas import tpu as pltpu

F32 = jnp.float32
BF16 = jnp.bfloat16
SDS = jax.ShapeDtypeStruct
MESH = pl.DeviceIdType.MESH
ANY = pl.BlockSpec(memory_space=pl.ANY)

N_DEV = 8
EPS = 1e-5
HEAD_DIM = 64
GROUP = 4
WINDOW = 128
ROPE_THETA = 10000.0
ADAM_LR, ADAM_B1, ADAM_B2, ADAM_EPS, ADAM_WD, ADAM_STEP = 0.001, 0.9, 0.999, 1e-08, 0.01, 10

V7X_VMEM_BYTES = 64 * 1024 * 1024
VMEM_LIMIT_BYTES = V7X_VMEM_BYTES - 8 * 1024 * 1024
LANES = 128
SUBLANES = 8
TOKEN_TILE = 512


def _cparams(n_axes=1):
    return pltpu.CompilerParams(dimension_semantics=("arbitrary",) * n_axes, vmem_limit_bytes=VMEM_LIMIT_BYTES)


def _resident(shape):
    zeros = (0,) * len(shape)
    return pl.BlockSpec(shape, lambda *_: zeros, pipeline_mode=pl.Buffered(1))


def _token_tile(seq):
    return min(TOKEN_TILE, seq // 2)


def _largest_divisor(m, cap, mult):
    best = None
    for d in range(mult, min(m, cap) + 1, mult):
        if m % d == 0:
            best = d
    return m if best is None else best


def _nt(a, b):
    return lax.dot_general(a, b, (((1,), (1,)), ((), ())), preferred_element_type=F32)


def _nn(a, b):
    return lax.dot_general(a, b, (((1,), (0,)), ((), ())), preferred_element_type=F32)


def _tn(a, b):
    return lax.dot_general(a, b, (((0,), (0,)), ((), ())), preferred_element_type=F32)


def _rms_parts(xv):
    r = lax.rsqrt(jnp.mean(xv * xv, axis=-1, keepdims=True) + EPS)
    return r, xv * r


def _rms_backward(dh, xh, r, gain, dres):
    u = dh * gain
    return dres + r * (u - xh * jnp.mean(u * xh, axis=-1, keepdims=True))


def _causal_conv3(ext_ref, xv, w_ref, n):
    ext_ref[8:8 + n, :] = xv
    return w_ref[2:3, :] * xv + w_ref[1:2, :] * ext_ref[7:7 + n, :] + w_ref[0:1, :] * ext_ref[6:6 + n, :]


def _anticausal_conv3(ext_ref, xv, w_ref, n):
    ext_ref[0:n, :] = xv
    sh1 = ext_ref[1:1 + n, :]
    sh2 = ext_ref[2:2 + n, :]
    return w_ref[2:3, :] * xv + w_ref[1:2, :] * sh1 + w_ref[0:1, :] * sh2, sh1, sh2


def _sigmoid(z):
    return 1.0 / (1.0 + jnp.exp(-z))


def _fwd_conv_mixer(x, gain, w_in_t, w_conv, w_out, seq, name):
    t_all, d = x.shape
    tt = _token_tile(seq)
    tps = seq // tt

    def body(x_ref, g_ref, win_ref, wc_ref, wout_ref, x1_ref, bcv_ref, ext_ref):
        i = pl.program_id(0)
        xv = x_ref[...]
        r, xh = _rms_parts(xv)
        h = (xh * g_ref[...]).astype(BF16)
        bcv = _nt(h, win_ref[...])
        bcv_ref[...] = bcv.astype(BF16)

        @pl.when(i % tps == 0)
        def _():
            ext_ref[0:8, :] = jnp.zeros((8, d), F32)

        cc = _causal_conv3(ext_ref, bcv[:, d:2 * d] * bcv[:, 2 * d:], wc_ref, tt)
        ext_ref[0:8, :] = ext_ref[tt:tt + 8, :]
        y = (bcv[:, :d] * cc).astype(BF16)
        x1_ref[...] = xv + _nn(y, wout_ref[...])

    return pl.pallas_call(
        body, name=name, grid=(t_all // tt,),
        in_specs=[pl.BlockSpec((tt, d), lambda i: (i, 0)), _resident((1, d)), _resident((3 * d, d)),
                  _resident((3, d)), _resident((d, d))],
        out_specs=[pl.BlockSpec((tt, d), lambda i: (i, 0)), pl.BlockSpec((tt, 3 * d), lambda i: (i, 0))],
        out_shape=[SDS((t_all, d), F32), SDS((t_all, 3 * d), BF16)],
        scratch_shapes=[pltpu.VMEM((tt + 8, d), F32)],
        compiler_params=_cparams(),
    )(x, gain, w_in_t, w_conv, w_out)


def _fwd_ffn(x, gain, w_in_t, w_conv, w_down, seq, name):
    t_all, d = x.shape
    f = w_down.shape[0]
    tt = _token_tile(seq) // 2
    tps = seq // tt

    def body(x_ref, g_ref, win_ref, wc_ref, wd_ref, x2_ref, gu_ref, ext_ref):
        i = pl.program_id(0)
        xv = x_ref[...]
        r, xh = _rms_parts(xv)
        h = (xh * g_ref[...]).astype(BF16)
        gu = _nt(h, win_ref[...])
        gu_ref[...] = gu.astype(BF16)

        @pl.when(i % tps == 0)
        def _():
            ext_ref[0:8, :] = jnp.zeros((8, f), F32)

        gc = _causal_conv3(ext_ref, gu[:, :f], wc_ref, tt)
        ext_ref[0:8, :] = ext_ref[tt:tt + 8, :]
        a = (gc * _sigmoid(gc) * gu[:, f:]).astype(BF16)
        x2_ref[...] = xv + _nn(a, wd_ref[...])

    return pl.pallas_call(
        body, name=name, grid=(t_all // tt,),
        in_specs=[pl.BlockSpec((tt, d), lambda i: (i, 0)), _resident((1, d)), _resident((2 * f, d)),
                  _resident((3, f)), _resident((f, d))],
        out_specs=[pl.BlockSpec((tt, d), lambda i: (i, 0)), pl.BlockSpec((tt, 2 * f), lambda i: (i, 0))],
        out_shape=[SDS((t_all, d), F32), SDS((t_all, 2 * f), BF16)],
        scratch_shapes=[pltpu.VMEM((tt + 8, f), F32)],
        compiler_params=_cparams(),
    )(x, gain, w_in_t, w_conv, w_down)


def _rope_partner(xs, lane_lo):
    return jnp.where(lane_lo, pltpu.roll(xs, LANES - HEAD_DIM // 2, 1), pltpu.roll(xs, HEAD_DIM // 2, 1))


def _fwd_qkv(x, gain, w_qkv_t, b_qkv, cos_t, sin_t, seq, name):
    t_all, d = x.shape
    width = w_qkv_t.shape[0]
    kvw = (width - d) // 2
    tt = _token_tile(seq)
    tps = seq // tt
    scale = HEAD_DIM ** -0.5

    def body(x_ref, g_ref, w_ref, b_ref, cos_ref, sin_ref, qkv_ref):
        xv = x_ref[...]
        r, xh = _rms_parts(xv)
        h = (xh * g_ref[...]).astype(BF16)
        qkv = _nt(h, w_ref[...]) + b_ref[...]
        cosv = cos_ref[...]
        sinv = sin_ref[...]
        lane_lo = (lax.broadcasted_iota(jnp.int32, (tt, LANES), 1) % HEAD_DIM) < HEAD_DIM // 2
        for s in range((d + kvw) // LANES):
            xs = qkv[:, s * LANES:(s + 1) * LANES]
            roped = xs * cosv + _rope_partner(xs, lane_lo) * sinv
            if s * LANES < d:
                roped = roped * scale
            qkv_ref[:, s * LANES:(s + 1) * LANES] = roped.astype(BF16)
        qkv_ref[:, d + kvw:] = qkv[:, d + kvw:].astype(BF16)

    return pl.pallas_call(
        body, name=name, grid=(t_all // tt,),
        in_specs=[pl.BlockSpec((tt, d), lambda i: (i, 0)), _resident((1, d)), _resident((width, d)),
                  _resident((1, width)), pl.BlockSpec((tt, LANES), lambda i: (i % tps, 0)),
                  pl.BlockSpec((tt, LANES), lambda i: (i % tps, 0))],
        out_specs=pl.BlockSpec((tt, width), lambda i: (i, 0)),
        out_shape=SDS((t_all, width), BF16),
        compiler_params=_cparams(),
    )(x, gain, w_qkv_t, b_qkv, cos_t, sin_t)


def _band_masks():
    rows = GROUP * WINDOW
    r = lax.broadcasted_iota(jnp.int32, (rows, 2 * WINDOW), 0) % WINDOW
    j = lax.broadcasted_iota(jnp.int32, (rows, 2 * WINDOW), 1)
    base = (j > r) & (j <= r + WINDOW)
    return base, base & (j >= WINDOW)


def _sink_column(sink_ref, kh):
    rows = GROUP * WINDOW
    g_of_row = lax.broadcasted_iota(jnp.int32, (rows, 1), 0) // WINDOW
    col = jnp.zeros((rows, 1), F32)
    for g in range(GROUP):
        col = jnp.where(g_of_row == g, sink_ref[kh * GROUP + g], col)
    return col


def _stack_heads(ref, row0, kh):
    return jnp.concatenate(
        [ref[row0:row0 + WINDOW, (kh * GROUP + g) * HEAD_DIM:(kh * GROUP + g + 1) * HEAD_DIM] for g in range(GROUP)],
        axis=0)


def _softmax_with_sink(s, valid, sink_col):
    s = jnp.where(valid, s, jnp.finfo(F32).min)
    m = jnp.maximum(jnp.max(s, axis=-1, keepdims=True), sink_col)
    p = jnp.exp(s - m)
    e_sink = jnp.exp(sink_col - m)
    inv = 1.0 / (jnp.sum(p, axis=-1, keepdims=True) + e_sink)
    return p * inv, e_sink * inv


def _fwd_attention(qkv, x, sinks, w_o, b_o, seq, name):
    t_all, d = x.shape
    width = qkv.shape[1]
    kvw = (width - d) // 2
    n_kv = kvw // HEAD_DIM
    tt = _token_tile(seq)
    tps = seq // tt
    nblk = tt // WINDOW

    def body(sink_ref, qkv_ref, kvp_ref, x_ref, wo_ref, bo_ref, x1_ref, o_ref, kvext_ref, oscr_ref):
        i = pl.program_id(0)
        kvext_ref[0:WINDOW, :] = kvp_ref[...]
        kvext_ref[WINDOW:, :] = qkv_ref[:, d:]
        base, base_first = _band_masks()
        not_first = jnp.full(base.shape, i % tps != 0)
        for n in range(nblk):
            valid = (base_first | (base & not_first)) if n == 0 else base
            for kh in range(n_kv):
                qs = _stack_heads(qkv_ref, n * WINDOW, kh)
                kb = kvext_ref[n * WINDOW:(n + 2) * WINDOW, kh * HEAD_DIM:(kh + 1) * HEAD_DIM]
                vb = kvext_ref[n * WINDOW:(n + 2) * WINDOW, kvw + kh * HEAD_DIM:kvw + (kh + 1) * HEAD_DIM]
                probs, _ = _softmax_with_sink(_nt(qs, kb), valid, _sink_column(sink_ref, kh))
                o_s = _nn(probs.astype(BF16), vb)
                for g in range(GROUP):
                    hd = kh * GROUP + g
                    oscr_ref[n * WINDOW:(n + 1) * WINDOW, hd * HEAD_DIM:(hd + 1) * HEAD_DIM] = (
                        o_s[g * WINDOW:(g + 1) * WINDOW].astype(BF16))
        o = oscr_ref[...]
        o_ref[...] = o
        x1_ref[...] = x_ref[...] + _nn(o, wo_ref[...]) + bo_ref[...]

    kv_blocks = tt // WINDOW
    return pl.pallas_call(
        body, name=name, grid=(t_all // tt,),
        in_specs=[pl.BlockSpec(memory_space=pltpu.SMEM),
                  pl.BlockSpec((tt, width), lambda i: (i, 0)),
                  pl.BlockSpec((WINDOW, 2 * kvw), lambda i: (jnp.maximum(i * kv_blocks - 1, 0), d // (2 * kvw))),
                  pl.BlockSpec((tt, d), lambda i: (i, 0)), _resident((d, d)), _resident((1, d))],
        out_specs=[pl.BlockSpec((tt, d), lambda i: (i, 0)), pl.BlockSpec((tt, d), lambda i: (i, 0))],
        out_shape=[SDS((t_all, d), F32), SDS((t_all, d), BF16)],
        scratch_shapes=[pltpu.VMEM((tt + WINDOW, 2 * kvw), BF16), pltpu.VMEM((tt, d), BF16)],
        compiler_params=_cparams(),
    )(sinks, qkv, qkv, x, w_o, b_o)


def _final_norm_loss(x, gain, target, name):
    t_all, d = x.shape
    tt = min(TOKEN_TILE, t_all)

    def body(x_ref, g_ref, t_ref, dx_ref, dg_ref, loss_ref):
        i = pl.program_id(0)
        xv = x_ref[...]
        r, xh = _rms_parts(xv)
        gain_v = g_ref[...]
        e = xh * gain_v - t_ref[...]
        dy = e * (1.0 / d)
        dx_ref[...] = _rms_backward(dy, xh, r, gain_v, 0.0)

        @pl.when(i == 0)
        def _():
            dg_ref[...] = jnp.zeros_like(dg_ref)
            loss_ref[...] = jnp.zeros_like(loss_ref)

        dg_ref[...] += jnp.sum(dy * xh, axis=0, keepdims=True)
        loss_ref[...] += (0.5 / d) * jnp.sum(e * e, axis=0, keepdims=True)

    return pl.pallas_call(
        body, name=name, grid=(t_all // tt,),
        in_specs=[pl.BlockSpec((tt, d), lambda i: (i, 0)), _resident((1, d)), pl.BlockSpec((tt, d), lambda i: (i, 0))],
        out_specs=[pl.BlockSpec((tt, d), lambda i: (i, 0)), pl.BlockSpec((1, d), lambda i: (0, 0)),
                   pl.BlockSpec((1, d), lambda i: (0, 0))],
        out_shape=[SDS((t_all, d), F32), SDS((1, d), F32), SDS((1, d), F32)],
        compiler_params=_cparams(),
    )(x, gain, target)


def _bwd_ffn_inner(dx2, gu, w_conv, w_down, seq, name):
    t_all, d = dx2.shape
    f = w_down.shape[0]
    tt = _token_tile(seq) // 2
    tps = seq // tt
    nt = t_all // tt
    halo = 16

    def body(dx_ref, gu_ref, prev_ref, wc_ref, wd_ref, dgu_ref, a_ref, dwc_ref, cext_ref, aext_ref):
        i = pl.program_id(0)
        ti = nt - 1 - i
        da = _nt(dx_ref[...].astype(BF16), wd_ref[...])
        gu_v = gu_ref[...].astype(F32)
        g = gu_v[:, :f]
        u = gu_v[:, f:]
        prev = prev_ref[...].astype(F32)[halo - 8:, :]
        cext_ref[0:8, :] = jnp.where(ti % tps == 0, 0.0, prev)
        gc = _causal_conv3(cext_ref, g, wc_ref, tt)
        sig = _sigmoid(gc)
        s = gc * sig
        a_ref[...] = (s * u).astype(BF16)
        dgc = da * u * (sig * (1.0 + gc * (1.0 - sig)))

        @pl.when(ti % tps == tps - 1)
        def _():
            aext_ref[tt:tt + 8, :] = jnp.zeros((8, f), F32)

        dg, sh1, sh2 = _anticausal_conv3(aext_ref, dgc, wc_ref, tt)
        aext_ref[tt:tt + 8, :] = aext_ref[0:8, :]
        dgu_ref[:, :f] = dg.astype(BF16)
        dgu_ref[:, f:] = (da * s).astype(BF16)

        @pl.when(i == 0)
        def _():
            dwc_ref[...] = jnp.zeros_like(dwc_ref)

        dwc_ref[0:1, :] += jnp.sum(g * sh2, axis=0, keepdims=True)
        dwc_ref[1:2, :] += jnp.sum(g * sh1, axis=0, keepdims=True)
        dwc_ref[2:3, :] += jnp.sum(g * dgc, axis=0, keepdims=True)

    return pl.pallas_call(
        body, name=name, grid=(nt,),
        in_specs=[pl.BlockSpec((tt, d), lambda i: (nt - 1 - i, 0)),
                  pl.BlockSpec((tt, 2 * f), lambda i: (nt - 1 - i, 0)),
                  pl.BlockSpec((halo, f), lambda i: (jnp.maximum((nt - 1 - i) * (tt // halo) - 1, 0), 0)),
                  _resident((3, f)), _resident((f, d))],
        out_specs=[pl.BlockSpec((tt, 2 * f), lambda i: (nt - 1 - i, 0)), pl.BlockSpec((tt, f), lambda i: (nt - 1 - i, 0)),
                   pl.BlockSpec((8, f), lambda i: (0, 0))],
        out_shape=[SDS((t_all, 2 * f), BF16), SDS((t_all, f), BF16), SDS((8, f), F32)],
        scratch_shapes=[pltpu.VMEM((tt + 8, f), F32), pltpu.VMEM((tt + 8, f), F32)],
        compiler_params=_cparams(),
    )(dx2, gu, gu, w_conv, w_down)


def _bwd_conv_inner(dx1, bcv, w_conv, w_out, seq, name):
    t_all, d = dx1.shape
    tt = _token_tile(seq)
    tps = seq // tt
    nt = t_all // tt
    halo = 16

    def body(dx_ref, bcv_ref, prev_ref, wc_ref, wout_ref, dbcv_ref, y_ref, dwc_ref, cext_ref, aext_ref):
        i = pl.program_id(0)
        ti = nt - 1 - i
        dy = _nt(dx_ref[...].astype(BF16), wout_ref[...])
        bcv_v = bcv_ref[...].astype(F32)
        b = bcv_v[:, :d]
        c = bcv_v[:, d:2 * d]
        v = bcv_v[:, 2 * d:]
        cv = c * v
        prev = prev_ref[...].astype(F32)[halo - 8:, :]
        cext_ref[0:8, :] = jnp.where(ti % tps == 0, 0.0, prev[:, d:2 * d] * prev[:, 2 * d:])
        cc = _causal_conv3(cext_ref, cv, wc_ref, tt)
        y_ref[...] = (b * cc).astype(BF16)
        dcc = dy * b

        @pl.when(ti % tps == tps - 1)
        def _():
            aext_ref[tt:tt + 8, :] = jnp.zeros((8, d), F32)

        dcv, sh1, sh2 = _anticausal_conv3(aext_ref, dcc, wc_ref, tt)
        aext_ref[tt:tt + 8, :] = aext_ref[0:8, :]
        dbcv_ref[:, :d] = (dy * cc).astype(BF16)
        dbcv_ref[:, d:2 * d] = (dcv * v).astype(BF16)
        dbcv_ref[:, 2 * d:] = (dcv * c).astype(BF16)

        @pl.when(i == 0)
        def _():
            dwc_ref[...] = jnp.zeros_like(dwc_ref)

        dwc_ref[0:1, :] += jnp.sum(cv * sh2, axis=0, keepdims=True)
        dwc_ref[1:2, :] += jnp.sum(cv * sh1, axis=0, keepdims=True)
        dwc_ref[2:3, :] += jnp.sum(cv * dcc, axis=0, keepdims=True)

    return pl.pallas_call(
        body, name=name, grid=(nt,),
        in_specs=[pl.BlockSpec((tt, d), lambda i: (nt - 1 - i, 0)),
                  pl.BlockSpec((tt, 3 * d), lambda i: (nt - 1 - i, 0)),
                  pl.BlockSpec((halo, 3 * d), lambda i: (jnp.maximum((nt - 1 - i) * (tt // halo) - 1, 0), 0)),
                  _resident((3, d)), _resident((d, d))],
        out_specs=[pl.BlockSpec((tt, 3 * d), lambda i: (nt - 1 - i, 0)), pl.BlockSpec((tt, d), lambda i: (nt - 1 - i, 0)),
                   pl.BlockSpec((8, d), lambda i: (0, 0))],
        out_shape=[SDS((t_all, 3 * d), BF16), SDS((t_all, d), BF16), SDS((8, d), F32)],
        scratch_shapes=[pltpu.VMEM((tt + 8, d), F32), pltpu.VMEM((tt + 8, d), F32)],
        compiler_params=_cparams(),
    )(dx1, bcv, bcv, w_conv, w_out)


def _bwd_attention_inner(dx1, qkv, sinks, w_o, cos_t, sin_t, seq, name):
    t_all, d = dx1.shape
    width = qkv.shape[1]
    kvw = (width - d) // 2
    n_kv = kvw // HEAD_DIM
    tt = _token_tile(seq)
    tps = seq // tt
    nt = t_all // tt
    nblk = tt // WINDOW
    scale = HEAD_DIM ** -0.5

    def body(sink_ref, dx_ref, qkv_ref, kvp_ref, cos_ref, sin_ref, wo_ref,
             dqkv_ref, dsink_ref, dbqkv_ref, dbo_ref,
             kvext_ref, dkvext_ref, carry_ref, dq_ref, do_ref):
        i = pl.program_id(0)
        ti = nt - 1 - i
        dxv = dx_ref[...]
        do_ref[...] = _nt(dxv.astype(BF16), wo_ref[...]).astype(BF16)
        kvext_ref[0:WINDOW, :] = kvp_ref[...]
        kvext_ref[WINDOW:, :] = qkv_ref[:, d:]
        dkvext_ref[...] = jnp.zeros_like(dkvext_ref)

        @pl.when(i == 0)
        def _():
            carry_ref[...] = jnp.zeros_like(carry_ref)
            dsink_ref[...] = jnp.zeros_like(dsink_ref)
            dbqkv_ref[...] = jnp.zeros_like(dbqkv_ref)
            dbo_ref[...] = jnp.zeros_like(dbo_ref)

        base, base_first = _band_masks()
        not_first = jnp.full(base.shape, ti % tps != 0)
        head_lane = lax.broadcasted_iota(jnp.int32, (1, LANES), 1)
        dsink = jnp.zeros((1, LANES), F32)
        for n in range(nblk):
            valid = (base_first | (base & not_first)) if n == 0 else base
            for kh in range(n_kv):
                qs = _stack_heads(qkv_ref, n * WINDOW, kh)
                dos = _stack_heads(do_ref, n * WINDOW, kh)
                kcols = slice(kh * HEAD_DIM, (kh + 1) * HEAD_DIM)
                vcols = slice(kvw + kh * HEAD_DIM, kvw + (kh + 1) * HEAD_DIM)
                band = slice(n * WINDOW, (n + 2) * WINDOW)
                kb = kvext_ref[band, kcols]
                vb = kvext_ref[band, vcols]
                probs, p_sink = _softmax_with_sink(_nt(qs, kb), valid, _sink_column(sink_ref, kh))
                dp = _nt(dos, vb)
                dsum = jnp.sum(probs * dp, axis=-1, keepdims=True)
                ds = (probs * (dp - dsum)).astype(BF16)
                dkvext_ref[band, vcols] += _tn(probs.astype(BF16), dos)
                dkvext_ref[band, kcols] += _tn(ds, qs)
                dq_s = _nn(ds, kb)
                sink_terms = p_sink * dsum
                for g in range(GROUP):
                    hd = kh * GROUP + g
                    dq_ref[n * WINDOW:(n + 1) * WINDOW, hd * HEAD_DIM:(hd + 1) * HEAD_DIM] = dq_s[g * WINDOW:(g + 1) * WINDOW]
                    dsink = dsink - jnp.where(head_lane == hd, jnp.sum(sink_terms[g * WINDOW:(g + 1) * WINDOW]), 0.0)
        dsink_ref[0:1, :] += dsink
        dkvext_ref[tt:tt + WINDOW, :] += carry_ref[...]
        carry_ref[...] = dkvext_ref[0:WINDOW, :]

        cosv = cos_ref[...]
        sinv = sin_ref[...]
        lane_lo = (lax.broadcasted_iota(jnp.int32, (tt, LANES), 1) % HEAD_DIM) < HEAD_DIM // 2
        for s in range((d + kvw) // LANES):
            if s * LANES < d:
                dy = dq_ref[:, s * LANES:(s + 1) * LANES] * scale
            else:
                dy = dkvext_ref[WINDOW:, s * LANES - d:(s + 1) * LANES - d]
            dpre = dy * cosv - _rope_partner(dy, lane_lo) * sinv
            dqkv_ref[:, s * LANES:(s + 1) * LANES] = dpre.astype(BF16)
            dbqkv_ref[0:1, s * LANES:(s + 1) * LANES] += jnp.sum(dpre, axis=0, keepdims=True)
        dv = dkvext_ref[WINDOW:, kvw:]
        dqkv_ref[:, d + kvw:] = dv.astype(BF16)
        dbqkv_ref[0:1, d + kvw:] += jnp.sum(dv, axis=0, keepdims=True)
        dbo_ref[...] += jnp.sum(dxv, axis=0, keepdims=True)

    kv_blocks = tt // WINDOW
    return pl.pallas_call(
        body, name=name, grid=(nt,),
        in_specs=[pl.BlockSpec(memory_space=pltpu.SMEM),
                  pl.BlockSpec((tt, d), lambda i: (nt - 1 - i, 0)),
                  pl.BlockSpec((tt, width), lambda i: (nt - 1 - i, 0)),
                  pl.BlockSpec((WINDOW, 2 * kvw), lambda i: (jnp.maximum((nt - 1 - i) * kv_blocks - 1, 0), d // (2 * kvw))),
                  pl.BlockSpec((tt, LANES), lambda i: ((nt - 1 - i) % tps, 0)),
                  pl.BlockSpec((tt, LANES), lambda i: ((nt - 1 - i) % tps, 0)),
                  _resident((d, d))],
        out_specs=[pl.BlockSpec((tt, width), lambda i: (nt - 1 - i, 0)), pl.BlockSpec((8, LANES), lambda i: (0, 0)),
                   pl.BlockSpec((1, width), lambda i: (0, 0)), pl.BlockSpec((1, d), lambda i: (0, 0))],
        out_shape=[SDS((t_all, width), BF16), SDS((8, LANES), F32), SDS((1, width), F32), SDS((1, d), F32)],
        scratch_shapes=[pltpu.VMEM((tt + WINDOW, 2 * kvw), BF16), pltpu.VMEM((tt + WINDOW, 2 * kvw), F32),
                        pltpu.VMEM((WINDOW, 2 * kvw), F32), pltpu.VMEM((tt, d), F32), pltpu.VMEM((tt, d), BF16)],
        compiler_params=_cparams(),
    )(sinks, dx1, qkv, qkv, cos_t, sin_t, w_o)


def _bwd_dense_norm(dy, w_t, x, gain, dres, name):
    t_all, d = x.shape
    n = dy.shape[1]
    tt = min(TOKEN_TILE, t_all)

    def body(dy_ref, w_ref, x_ref, g_ref, dres_ref, dx_ref, h_ref, dg_ref):
        i = pl.program_id(0)
        dh = _nn(dy_ref[...], w_ref[...])
        r, xh = _rms_parts(x_ref[...])
        gain_v = g_ref[...]
        h_ref[...] = (xh * gain_v).astype(BF16)
        dx_ref[...] = _rms_backward(dh, xh, r, gain_v, dres_ref[...])

        @pl.when(i == 0)
        def _():
            dg_ref[...] = jnp.zeros_like(dg_ref)

        dg_ref[...] += jnp.sum(dh * xh, axis=0, keepdims=True)

    return pl.pallas_call(
        body, name=name, grid=(t_all // tt,),
        in_specs=[pl.BlockSpec((tt, n), lambda i: (i, 0)), _resident((n, d)), pl.BlockSpec((tt, d), lambda i: (i, 0)),
                  _resident((1, d)), pl.BlockSpec((tt, d), lambda i: (i, 0))],
        out_specs=[pl.BlockSpec((tt, d), lambda i: (i, 0)), pl.BlockSpec((tt, d), lambda i: (i, 0)),
                   pl.BlockSpec((1, d), lambda i: (0, 0))],
        out_shape=[SDS((t_all, d), F32), SDS((t_all, d), BF16), SDS((1, d), F32)],
        compiler_params=_cparams(),
    )(dy, w_t, x, gain, dres)


def _tn_matmul(a, b, name):
    t_all, m = a.shape
    d = b.shape[1]
    tm = _largest_divisor(m, 1536, LANES)
    tt = min(TOKEN_TILE, t_all)
    n_t = t_all // tt

    def body(a_ref, b_ref, o_ref, acc_ref):
        t = pl.program_id(1)

        @pl.when(t == 0)
        def _():
            acc_ref[...] = jnp.zeros_like(acc_ref)

        acc_ref[...] += _tn(a_ref[...], b_ref[...].astype(BF16))

        @pl.when(t == n_t - 1)
        def _():
            o_ref[...] = acc_ref[...].astype(BF16)

    return pl.pallas_call(
        body, name=name, grid=(m // tm, n_t),
        in_specs=[pl.BlockSpec((tt, tm), lambda j, t: (t, j)), pl.BlockSpec((tt, d), lambda j, t: (t, 0))],
        out_specs=pl.BlockSpec((tm, d), lambda j, t: (j, 0)),
        out_shape=SDS((m, d), BF16),
        scratch_shapes=[pltpu.VMEM((tm, d), F32)],
        compiler_params=_cparams(2),
    )(a, b)


def _my_place():
    return lax.axis_index("x"), lax.axis_index("y"), lax.axis_index("c")


def _other_chips(x, y):
    return [(1 - x, y), (x, 1 - y), (1 - x, 1 - y)]


def _all_gather(blocks, name):
    n_arr = len(blocks)

    def body(*refs):
        in_refs = refs[:n_arr]
        out_refs = refs[n_arr:2 * n_arr]
        send_sems, recv_sems, local_sems = refs[2 * n_arr:]
        x, y, c = _my_place()
        me, sibling = (x, y, c), (x, y, 1 - c)
        chips = _other_chips(x, y)

        def slot(a, place):
            px, py, pc = place
            return out_refs[a].at[4 * px + 2 * py + pc]

        def copy(a, k, block, to, src=None):
            return pltpu.make_async_remote_copy(
                src_ref=slot(a, block) if src is None else src, dst_ref=slot(a, block),
                send_sem=send_sems.at[a, k], recv_sem=recv_sems.at[a, k], device_id=to, device_id_type=MESH)

        started = []
        local = []
        for a in range(n_arr):
            mine = pltpu.make_async_copy(in_refs[a], slot(a, me), local_sems.at[a])
            mine.start()
            local.append(mine)
            first = [copy(a, 0, me, sibling, src=in_refs[a])]
            first += [copy(a, 1 + j, me, (*chip, c), src=in_refs[a]) for j, chip in enumerate(chips)]
            for cp in first:
                cp.start()
            started += first
        for a in range(n_arr):
            for j, chip in enumerate(chips):
                copy(a, 1 + j, (*chip, c), me).wait_recv()
                passed = copy(a, 4 + j, (*chip, c), sibling)
                passed.start()
                started.append(passed)
        for a in range(n_arr):
            copy(a, 0, sibling, me).wait_recv()
            for j, chip in enumerate(chips):
                copy(a, 4 + j, (*chip, 1 - c), me).wait_recv()
        for cp in started:
            cp.wait_send()
        for mine in local:
            mine.wait()

    return pl.pallas_call(
        body, name=name,
        in_specs=[ANY] * n_arr, out_specs=[ANY] * n_arr,
        out_shape=[SDS((N_DEV,) + b.shape, b.dtype) for b in blocks],
        scratch_shapes=[pltpu.SemaphoreType.DMA((n_arr, 7)), pltpu.SemaphoreType.DMA((n_arr, 7)),
                        pltpu.SemaphoreType.DMA((n_arr,))],
    )(*blocks)


def _peer_of(k, x, y, c):
    return x ^ ((k >> 2) & 1), y ^ ((k >> 1) & 1), c ^ (k & 1)


HBM = pl.BlockSpec(memory_space=pltpu.HBM)
SEM = pl.BlockSpec(memory_space=pltpu.SEMAPHORE)
DATAFLOW_EFFECT = pltpu.SideEffectType.DATAFLOW_SIDE_EFFECTING


def _peer_copies(src_ref, land_ref, send_sems, recv_sems, per_peer):
    x, y, c = _my_place()
    me = 4 * x + 2 * y + c
    copies = []
    for k in range(1, N_DEV):
        px, py, pc = _peer_of(k, x, y, c)
        peer = 4 * px + 2 * py + pc
        copies.append(pltpu.make_async_remote_copy(
            src_ref=src_ref.at[peer] if per_peer else src_ref, dst_ref=land_ref.at[me],
            send_sem=send_sems.at[k - 1], recv_sem=recv_sems.at[k - 1], device_id=(px, py, pc), device_id_type=MESH))
    return copies


def _exchange_start(src, per_peer, name):
    rows, d = src.shape[-2:]

    def body(src_ref, land_ref, send_sems, recv_sems, src_thru, land_thru, token):
        for cp in _peer_copies(src_ref, land_ref, send_sems, recv_sems, per_peer):
            cp.start()
        token[...] = jnp.zeros_like(token)

    return pl.pallas_call(
        body, name=name,
        out_shape=(pltpu.SemaphoreType.DMA((N_DEV - 1,)), pltpu.SemaphoreType.DMA((N_DEV - 1,)), pltpu.HBM(src.shape, src.dtype),
                   pltpu.HBM((N_DEV, rows, d), src.dtype), SDS((SUBLANES, LANES), F32)),
        in_specs=(HBM, HBM), out_specs=(SEM, SEM, HBM, HBM, pl.BlockSpec(memory_space=pltpu.VMEM)),
        input_output_aliases={0: 2, 1: 3},
        compiler_params=pltpu.CompilerParams(has_side_effects=DATAFLOW_EFFECT),
    )(pltpu.with_memory_space_constraint(src, pltpu.HBM),
      pltpu.with_memory_space_constraint(lax.empty((N_DEV, rows, d), src.dtype), pltpu.HBM))


def _exchange_wait(started, after, per_peer, name):
    send_sems, recv_sems, src_thru, land_thru, _ = started

    def body(src_ref, land_ref, send_sems, recv_sems, after_ref, src_out, land_out):
        for cp in _peer_copies(src_ref, land_ref, send_sems, recv_sems, per_peer):
            cp.wait_send()
            cp.wait_recv()

    return pl.pallas_call(
        body, name=name,
        out_shape=(pltpu.HBM(src_thru.shape, src_thru.dtype), pltpu.HBM(land_thru.shape, land_thru.dtype)),
        in_specs=(HBM, HBM, SEM, SEM, ANY), out_specs=(HBM, HBM), input_output_aliases={0: 0, 1: 1},
        compiler_params=pltpu.CompilerParams(has_side_effects=DATAFLOW_EFFECT),
    )(src_thru, land_thru, send_sems, recv_sems, after)


def _sum_slots(slots, name):
    _, rows, d = slots.shape
    tr = _largest_divisor(rows, 512, 16)

    def body(s_ref, o_ref):
        acc = s_ref[0].astype(F32)
        for dev in range(1, N_DEV):
            acc = acc + s_ref[dev].astype(F32)
        o_ref[...] = acc

    return pl.pallas_call(
        body, name=name, grid=(rows // tr,),
        in_specs=[pl.BlockSpec((N_DEV, tr, d), lambda r: (0, r, 0))], out_specs=pl.BlockSpec((tr, d), lambda r: (r, 0)),
        out_shape=SDS((rows, d), F32), compiler_params=_cparams(),
    )(slots)


def _all_reduce_small(part, loss_rows, name):
    rows, lanes = part.shape
    lo, hi = loss_rows

    def body(x_ref, out_ref, loss_ref, gath_ref, send_sems, recv_sems):
        x, y, c = _my_place()
        me = 4 * x + 2 * y + c
        gath_ref[me] = x_ref[...]
        copies = []
        for k in range(1, N_DEV):
            peer = (x ^ ((k >> 2) & 1), y ^ ((k >> 1) & 1), c ^ (k & 1))
            cp = pltpu.make_async_remote_copy(
                src_ref=x_ref, dst_ref=gath_ref.at[me], send_sem=send_sems.at[k - 1], recv_sem=recv_sems.at[k - 1],
                device_id=peer, device_id_type=MESH)
            cp.start()
            copies.append(cp)
        for cp in copies:
            cp.wait_recv()
        for cp in copies:
            cp.wait_send()
        acc = gath_ref[0]
        for dev in range(1, N_DEV):
            acc = acc + gath_ref[dev]
        out_ref[...] = acc
        loss_ref[...] = jnp.full(loss_ref.shape, jnp.sum(acc[lo:hi, :]), F32)

    vmem = pl.BlockSpec(memory_space=pltpu.VMEM)
    return pl.pallas_call(
        body, name=name, in_specs=[vmem], out_specs=[vmem, vmem],
        out_shape=[SDS((rows, lanes), F32), SDS((SUBLANES, LANES), F32)],
        scratch_shapes=[pltpu.VMEM((N_DEV, rows, lanes), F32), pltpu.SemaphoreType.DMA((N_DEV - 1,)),
                        pltpu.SemaphoreType.DMA((N_DEV - 1,))],
    )(part)


def _adamw(w, g, m, v, name):
    rows, cols = w.shape
    tr = rows if rows % SUBLANES else _largest_divisor(rows, 512, SUBLANES)

    def body(w_ref, g_ref, m_ref, v_ref, d_ref, nm_ref, nv_ref):
        gv = g_ref[...]
        nm = ADAM_B1 * m_ref[...] + (1.0 - ADAM_B1) * gv
        nv = ADAM_B2 * v_ref[...] + (1.0 - ADAM_B2) * (gv * gv)
        m_hat = nm / (1.0 - ADAM_B1 ** ADAM_STEP)
        v_hat = nv / (1.0 - ADAM_B2 ** ADAM_STEP)
        d_ref[...] = -ADAM_LR * (m_hat / (jnp.sqrt(v_hat) + ADAM_EPS) + ADAM_WD * w_ref[...])
        nm_ref[...] = nm
        nv_ref[...] = nv

    spec = pl.BlockSpec((tr, cols), lambda i: (i, 0))
    return pl.pallas_call(
        body, name=name, grid=(rows // tr,), in_specs=[spec] * 4, out_specs=[spec] * 3,
        out_shape=[SDS((rows, cols), F32)] * 3, compiler_params=_cparams(),
    )(w, g, m, v)


def _adamw_nd(w, g, m, v, name):
    shape = w.shape
    two_d = (1, shape[0]) if len(shape) == 1 else (-1, shape[-1])
    outs = _adamw(w.reshape(two_d), g.reshape(two_d), m.reshape(two_d), v.reshape(two_d), name)
    return [o.reshape(shape) for o in outs]


def _rope_tables(seq):
    pos = jnp.arange(seq, dtype=F32)
    inv_freq = 1.0 / (ROPE_THETA ** (jnp.arange(0, HEAD_DIM, 2, dtype=F32) / HEAD_DIM))
    ang = pos[:, None] * inv_freq[None, :]
    cos, sin = jnp.cos(ang), jnp.sin(ang)
    reps = LANES // HEAD_DIM
    cos_t = jnp.tile(jnp.concatenate([cos, cos], axis=1), (1, reps))
    sin_t = jnp.tile(jnp.concatenate([-sin, sin], axis=1), (1, reps))
    return cos_t, sin_t


def _flat_pad(a):
    flat = a.reshape(1, -1)
    pad = (-flat.shape[1]) % LANES
    return jnp.pad(flat, ((0, 0), (0, pad))) if pad else flat


def kernel(x, norm_mix, norm_ffn, norm_final, conv_w_in, conv_w_conv, conv_w_out, attn_w_qkv, attn_b_qkv, attn_sinks, attn_w_o, attn_b_o, ffn_w_in, ffn_w_conv, ffn_w_down, loss_target, m_norm_mix, m_norm_ffn, m_norm_final, m_conv_w_in, m_conv_w_conv, m_conv_w_out, m_attn_w_qkv, m_attn_b_qkv, m_attn_sinks, m_attn_w_o, m_attn_b_o, m_ffn_w_in, m_ffn_w_conv, m_ffn_w_down, v_norm_mix, v_norm_ffn, v_norm_final, v_conv_w_in, v_conv_w_conv, v_conv_w_out, v_attn_w_qkv, v_attn_b_qkv, v_attn_sinks, v_attn_w_o, v_attn_b_o, v_ffn_w_in, v_ffn_w_conv, v_ffn_w_down):
    b_loc, seq, d = x.shape
    depth = norm_mix.shape[0]
    n_conv, n_attn = conv_w_in.shape[0], attn_w_qkv.shape[0]
    t_all = b_loc * seq
    my_x, my_y, my_c = _my_place()

    me = 4 * my_x + 2 * my_y + my_c

    groups = []
    for i in range(depth):
        j = i // 2
        if i % 2 == 0:
            mix = [("conv_w_in", j, True, conv_w_in[j].T), ("conv_w_out", j, False, conv_w_out[j])]
        else:
            mix = [("attn_w_qkv", j, True, attn_w_qkv[j].T), ("attn_w_o", j, False, attn_w_o[j])]
        groups.append((("mix", i), mix))
        groups.append((("ffn", i), [("ffn_w_in", i, True, ffn_w_in[i].T), ("ffn_w_down", i, False, ffn_w_down[i])]))
    gather_started = {}
    start_tokens = 0.0
    for key, members in groups:
        pack = jnp.concatenate([m[3] for m in members], axis=0).astype(BF16)
        gather_started[key] = _exchange_start(pack, False, f"gather_start_{key[0]}_{key[1]}")
        start_tokens = start_tokens + gather_started[key][4][0, 0]

    weights = {}

    def finish_gather(key, after):
        src, land = _exchange_wait(gather_started[key], after, False, f"gather_wait_{key[0]}_{key[1]}")
        full = lax.dynamic_update_slice(land, src[None], (me, 0, 0))
        o = 0
        for wname, layer, _, shard in dict(groups)[key]:
            n = shard.shape[0]
            weights[(wname, layer)] = full[:, o:o + n, :].reshape(N_DEV * n, d)
            o += n

    small = jnp.concatenate([_flat_pad(conv_w_conv), _flat_pad(ffn_w_conv), _flat_pad(attn_b_qkv), _flat_pad(attn_b_o)], axis=1)
    (small_g,) = _all_gather([small], "gather_small")

    def take_small(o, shape):
        size = shape[0] * shape[1] * shape[2]
        blk = small_g[:, 0, o:o + size].reshape((N_DEV,) + shape)
        return jnp.moveaxis(blk, 0, 2).reshape(shape[0], shape[1], N_DEV * shape[2])

    so = 0
    wc_conv_full = take_small(so, conv_w_conv.shape); so += _flat_pad(conv_w_conv).shape[1]
    wc_ffn_full = take_small(so, ffn_w_conv.shape); so += _flat_pad(ffn_w_conv).shape[1]
    b_qkv_full = take_small(so, (n_attn, 1, attn_b_qkv.shape[1]))[:, 0]; so += _flat_pad(attn_b_qkv).shape[1]
    b_o_full = take_small(so, (n_attn, 1, attn_b_o.shape[1]))[:, 0]

    cos_t, sin_t = _rope_tables(seq)

    xs = [x.reshape(t_all, d)]
    saved = []
    for i in range(depth):
        j = i // 2
        gain_mix = norm_mix[i][None, :]
        if i == 0:
            gain_mix = gain_mix + start_tokens
        finish_gather(("mix", i), gain_mix if i == 0 else xs[-1])
        if i % 2 == 0:
            x1, bcv = _fwd_conv_mixer(xs[-1], gain_mix, weights[("conv_w_in", j)], wc_conv_full[j],
                                      weights[("conv_w_out", j)], seq, f"fwd_conv_{i}")
            mix_saved = (bcv,)
        else:
            qkv = _fwd_qkv(xs[-1], gain_mix, weights[("attn_w_qkv", j)], b_qkv_full[j][None, :], cos_t, sin_t, seq,
                           f"fwd_qkv_{i}")
            x1, o = _fwd_attention(qkv, xs[-1], attn_sinks[j], weights[("attn_w_o", j)], b_o_full[j][None, :], seq,
                                   f"fwd_attn_{i}")
            mix_saved = (qkv, o)
        finish_gather(("ffn", i), x1)
        x2, gu = _fwd_ffn(x1, norm_ffn[i][None, :], weights[("ffn_w_in", i)], wc_ffn_full[i], weights[("ffn_w_down", i)],
                          seq, f"fwd_ffn_{i}")
        saved.append((xs[-1], x1, mix_saved, gu))
        xs.append(x2)

    dx, dg_final, loss_lanes = _final_norm_loss(xs[-1], norm_final[None, :], loss_target.reshape(t_all, d), "loss_head")

    grads_t = {}
    dg_mix, dg_ffn = [None] * depth, [None] * depth
    dwc_conv, dwc_ffn = [None] * n_conv, [None] * depth
    db_qkv, db_o, dsinks = [None] * n_attn, [None] * n_attn, [None] * n_attn
    scatter_started = {}

    def start_scatter(key):
        parts = jnp.concatenate([grads_t[(wname, layer)].reshape(N_DEV, shard.shape[0], d)
                                 for wname, layer, _, shard in dict(groups)[key]], axis=1)
        scatter_started[key] = _exchange_start(parts, True, f"scatter_start_{key[0]}_{key[1]}")
        return scatter_started[key][4][0, 0]

    token = 0.0
    for i in reversed(range(depth)):
        j = i // 2
        x0, x1, mix_saved, gu = saved[i]
        dgu, act, dwc = _bwd_ffn_inner(dx, gu, wc_ffn_full[i] + token, weights[("ffn_w_down", i)], seq, f"bwd_ffn_{i}")
        dwc_ffn[i] = dwc[:3]
        dx1, h2, dg_ffn[i] = _bwd_dense_norm(dgu, weights[("ffn_w_in", i)], x1, norm_ffn[i][None, :], dx, f"bwd_ffn_norm_{i}")
        grads_t[("ffn_w_in", i)] = _tn_matmul(dgu, h2, f"dw_ffn_in_{i}")
        grads_t[("ffn_w_down", i)] = _tn_matmul(act, dx, f"dw_ffn_down_{i}")
        token = start_scatter(("ffn", i))
        if i % 2 == 0:
            (bcv,) = mix_saved
            dbcv, y, dwc = _bwd_conv_inner(dx1, bcv, wc_conv_full[j] + token, weights[("conv_w_out", j)], seq, f"bwd_conv_{i}")
            dwc_conv[j] = dwc[:3]
            dx, h, dg_mix[i] = _bwd_dense_norm(dbcv, weights[("conv_w_in", j)], x0, norm_mix[i][None, :], dx1,
                                               f"bwd_conv_norm_{i}")
            grads_t[("conv_w_in", j)] = _tn_matmul(dbcv, h, f"dw_conv_in_{i}")
            grads_t[("conv_w_out", j)] = _tn_matmul(y, dx1, f"dw_conv_out_{i}")
        else:
            qkv, o = mix_saved
            dqkv, dsk, dbq, dbo = _bwd_attention_inner(dx1, qkv, attn_sinks[j] + token, weights[("attn_w_o", j)], cos_t, sin_t,
                                                       seq, f"bwd_attn_{i}")
            dsinks[j], db_qkv[j], db_o[j] = dsk[0:1, :attn_sinks.shape[1]], dbq, dbo
            dx, h, dg_mix[i] = _bwd_dense_norm(dqkv, weights[("attn_w_qkv", j)], x0, norm_mix[i][None, :], dx1,
                                               f"bwd_attn_norm_{i}")
            grads_t[("attn_w_qkv", j)] = _tn_matmul(dqkv, h, f"dw_attn_qkv_{i}")
            grads_t[("attn_w_o", j)] = _tn_matmul(o, dx1, f"dw_attn_o_{i}")
        token = start_scatter(("mix", i))
    grad_x = dx.reshape(b_loc, seq, d)

    reduced = {}
    for key, members in reversed(groups):
        parts, land = _exchange_wait(scatter_started[key], dx, True, f"scatter_wait_{key[0]}_{key[1]}")
        mine = lax.dynamic_index_in_dim(parts, me, axis=0, keepdims=True)
        total = _sum_slots(lax.dynamic_update_slice(land, mine, (me, 0, 0)), f"scatter_sum_{key[0]}_{key[1]}")
        o = 0
        for wname, layer, transposed, shard in members:
            n = shard.shape[0]
            reduced[(wname, layer)] = total[o:o + n].T if transposed else total[o:o + n]
            o += n

    small_parts = [jnp.concatenate(dg_mix, axis=0), jnp.concatenate(dg_ffn, axis=0), dg_final,
                   jnp.stack(dwc_conv), jnp.stack(dwc_ffn), jnp.concatenate(db_qkv, axis=0), jnp.concatenate(db_o, axis=0),
                   jnp.concatenate(dsinks, axis=0), loss_lanes]
    flats = [_flat_pad(p) for p in small_parts]
    bounds = []
    so = 0
    for fl in flats:
        bounds.append((so, so + fl.shape[1]))
        so += fl.shape[1]
    small_rows = so // LANES
    pad_rows = (-small_rows) % SUBLANES
    part_small = jnp.pad(jnp.concatenate(flats, axis=1).reshape(small_rows, LANES), ((0, pad_rows), (0, 0)))
    loss_rows = (bounds[-1][0] // LANES, bounds[-1][1] // LANES)
    summed, loss_tile = _all_reduce_small(part_small, loss_rows, "reduce_small")
    summed = summed.reshape(1, -1)

    def small_grad(k, shape):
        lo = bounds[k][0]
        size = 1
        for s_ in shape:
            size *= s_
        return summed[0, lo:lo + size].reshape(shape)

    def my_cols(full, n_local):
        return lax.dynamic_slice_in_dim(full, me * n_local, n_local, axis=full.ndim - 1)

    g_norm_mix = small_grad(0, norm_mix.shape)
    g_norm_ffn = small_grad(1, norm_ffn.shape)
    g_norm_final = small_grad(2, norm_final.shape)
    g_conv_w_conv = my_cols(small_grad(3, (n_conv, 3, d)), conv_w_conv.shape[2])
    g_ffn_w_conv = my_cols(small_grad(4, (depth, 3, ffn_w_conv.shape[2] * N_DEV)), ffn_w_conv.shape[2])
    g_attn_b_qkv = my_cols(small_grad(5, (n_attn, attn_b_qkv.shape[1] * N_DEV)), attn_b_qkv.shape[1])
    g_attn_b_o = my_cols(small_grad(6, (n_attn, d)), attn_b_o.shape[1])
    g_attn_sinks = small_grad(7, attn_sinks.shape)
    loss = loss_tile[0, 0]

    def big_grad(wname, n_layers):
        return jnp.stack([reduced[(wname, layer)] for layer in range(n_layers)])

    grads = {
        "norm_mix": g_norm_mix, "norm_ffn": g_norm_ffn, "norm_final": g_norm_final,
        "conv_w_in": big_grad("conv_w_in", n_conv), "conv_w_conv": g_conv_w_conv, "conv_w_out": big_grad("conv_w_out", n_conv),
        "attn_w_qkv": big_grad("attn_w_qkv", n_attn), "attn_b_qkv": g_attn_b_qkv, "attn_sinks": g_attn_sinks,
        "attn_w_o": big_grad("attn_w_o", n_attn), "attn_b_o": g_attn_b_o,
        "ffn_w_in": big_grad("ffn_w_in", depth), "ffn_w_conv": g_ffn_w_conv, "ffn_w_down": big_grad("ffn_w_down", depth),
    }
    params = {
        "norm_mix": (norm_mix, m_norm_mix, v_norm_mix), "norm_ffn": (norm_ffn, m_norm_ffn, v_norm_ffn),
        "norm_final": (norm_final, m_norm_final, v_norm_final), "conv_w_in": (conv_w_in, m_conv_w_in, v_conv_w_in),
        "conv_w_conv": (conv_w_conv, m_conv_w_conv, v_conv_w_conv), "conv_w_out": (conv_w_out, m_conv_w_out, v_conv_w_out),
        "attn_w_qkv": (attn_w_qkv, m_attn_w_qkv, v_attn_w_qkv), "attn_b_qkv": (attn_b_qkv, m_attn_b_qkv, v_attn_b_qkv),
        "attn_sinks": (attn_sinks, m_attn_sinks, v_attn_sinks), "attn_w_o": (attn_w_o, m_attn_w_o, v_attn_w_o),
        "attn_b_o": (attn_b_o, m_attn_b_o, v_attn_b_o), "ffn_w_in": (ffn_w_in, m_ffn_w_in, v_ffn_w_in),
        "ffn_w_conv": (ffn_w_conv, m_ffn_w_conv, v_ffn_w_conv), "ffn_w_down": (ffn_w_down, m_ffn_w_down, v_ffn_w_down),
    }
    order = list(params)
    deltas, new_ms, new_vs = [], [], []
    for wname in order:
        w, m, v = params[wname]
        dlt, nm, nv = _adamw_nd(w, grads[wname], m, v, f"adamw_{wname}")
        deltas.append(dlt); new_ms.append(nm); new_vs.append(nv)
    return (loss, grad_x, *[grads[wname] for wname in order], *deltas, *new_ms, *new_vs)
```

```python
from typing import NamedTuple

import jax
import jax.numpy as jnp
from jax import lax
from jax.experimental import pallas as pl
from jax.experimental.pallas import tpu as pltpu

F32 = jnp.float32
BF16 = jnp.bfloat16
SDS = jax.ShapeDtypeStruct
MESH = pl.DeviceIdType.MESH
ANY = pl.BlockSpec(memory_space=pl.ANY)

N_DEV = 8
EPS = 1e-5
HEAD_DIM = 64
GROUP = 4
WINDOW = 128
ROPE_THETA = 10000.0
ADAM_LR, ADAM_B1, ADAM_B2, ADAM_EPS, ADAM_WD, ADAM_STEP = 0.001, 0.9, 0.999, 1e-08, 0.01, 10

V7X_VMEM_BYTES = 64 * 1024 * 1024
VMEM_LIMIT_BYTES = V7X_VMEM_BYTES - 8 * 1024 * 1024
LANES = 128
SUBLANES = 8
TOKEN_TILE = 512


def _cparams(n_axes=1):
    return pltpu.CompilerParams(dimension_semantics=("arbitrary",) * n_axes, vmem_limit_bytes=VMEM_LIMIT_BYTES)


def _resident(shape):
    zeros = (0,) * len(shape)
    return pl.BlockSpec(shape, lambda *_: zeros, pipeline_mode=pl.Buffered(1))


class _Rows(NamedTuple):
    arr: jax.Array
    off: int
    n: int


def _rows_spec(w):
    assert w.off % w.n == 0
    block = w.off // w.n
    return pl.BlockSpec((N_DEV, w.n, w.arr.shape[2]), lambda *_: (0, block, 0), pipeline_mode=pl.Buffered(1))


def _mat(ref):
    v = ref[...]
    return v.reshape(v.shape[0] * v.shape[1], v.shape[2])


def _token_tile(seq):
    return min(TOKEN_TILE, seq // 2)


def _largest_divisor(m, cap, mult):
    best = None
    for d in range(mult, min(m, cap) + 1, mult):
        if m % d == 0:
            best = d
    return m if best is None else best


def _nt(a, b):
    return lax.dot_general(a, b, (((1,), (1,)), ((), ())), preferred_element_type=F32)


def _nn(a, b):
    return lax.dot_general(a, b, (((1,), (0,)), ((), ())), preferred_element_type=F32)


def _tn(a, b):
    return lax.dot_general(a, b, (((0,), (0,)), ((), ())), preferred_element_type=F32)


def _rms_parts(xv):
    r = lax.rsqrt(jnp.mean(xv * xv, axis=-1, keepdims=True) + EPS)
    return r, xv * r


def _rms_backward(dh, xh, r, gain, dres):
    u = dh * gain
    return dres + r * (u - xh * jnp.mean(u * xh, axis=-1, keepdims=True))


def _causal_conv3(ext_ref, xv, w_ref, n):
    ext_ref[8:8 + n, :] = xv
    return w_ref[2:3, :] * xv + w_ref[1:2, :] * ext_ref[7:7 + n, :] + w_ref[0:1, :] * ext_ref[6:6 + n, :]


def _anticausal_conv3(ext_ref, xv, w_ref, n):
    ext_ref[0:n, :] = xv
    sh1 = ext_ref[1:1 + n, :]
    sh2 = ext_ref[2:2 + n, :]
    return w_ref[2:3, :] * xv + w_ref[1:2, :] * sh1 + w_ref[0:1, :] * sh2, sh1, sh2


def _sigmoid(z):
    return 1.0 / (1.0 + jnp.exp(-z))


def _fwd_conv_mixer(x, gain, w_in_t, w_conv, w_out, seq, name):
    t_all, d = x.shape
    tt = _token_tile(seq)
    tps = seq // tt

    def body(x_ref, g_ref, win_ref, wc_ref, wout_ref, x1_ref, bcv_ref, ext_ref):
        i = pl.program_id(0)
        xv = x_ref[...]
        r, xh = _rms_parts(xv)
        h = (xh * g_ref[...]).astype(BF16)
        bcv = _nt(h, _mat(win_ref))
        bcv_ref[...] = bcv.astype(BF16)

        @pl.when(i % tps == 0)
        def _():
            ext_ref[0:8, :] = jnp.zeros((8, d), F32)

        cc = _causal_conv3(ext_ref, bcv[:, d:2 * d] * bcv[:, 2 * d:], wc_ref, tt)
        ext_ref[0:8, :] = ext_ref[tt:tt + 8, :]
        y = (bcv[:, :d] * cc).astype(BF16)
        x1_ref[...] = xv + _nn(y, _mat(wout_ref))

    return pl.pallas_call(
        body, name=name, grid=(t_all // tt,),
        in_specs=[pl.BlockSpec((tt, d), lambda i: (i, 0)), _resident((1, d)), _rows_spec(w_in_t),
                  _resident((3, d)), _rows_spec(w_out)],
        out_specs=[pl.BlockSpec((tt, d), lambda i: (i, 0)), pl.BlockSpec((tt, 3 * d), lambda i: (i, 0))],
        out_shape=[SDS((t_all, d), F32), SDS((t_all, 3 * d), BF16)],
        scratch_shapes=[pltpu.VMEM((tt + 8, d), F32)],
        compiler_params=_cparams(),
    )(x, gain, w_in_t.arr, w_conv, w_out.arr)


def _fwd_ffn(x, gain, w_in_t, w_conv, w_down, seq, name):
    t_all, d = x.shape
    f = w_down.n * N_DEV
    tt = _token_tile(seq) // 2
    tps = seq // tt

    def body(x_ref, g_ref, win_ref, wc_ref, wd_ref, x2_ref, gu_ref, ext_ref):
        i = pl.program_id(0)
        xv = x_ref[...]
        r, xh = _rms_parts(xv)
        h = (xh * g_ref[...]).astype(BF16)
        gu = _nt(h, _mat(win_ref))
        gu_ref[...] = gu.astype(BF16)

        @pl.when(i % tps == 0)
        def _():
            ext_ref[0:8, :] = jnp.zeros((8, f), F32)

        gc = _causal_conv3(ext_ref, gu[:, :f], wc_ref, tt)
        ext_ref[0:8, :] = ext_ref[tt:tt + 8, :]
        a = (gc * _sigmoid(gc) * gu[:, f:]).astype(BF16)
        x2_ref[...] = xv + _nn(a, _mat(wd_ref))

    return pl.pallas_call(
        body, name=name, grid=(t_all // tt,),
        in_specs=[pl.BlockSpec((tt, d), lambda i: (i, 0)), _resident((1, d)), _rows_spec(w_in_t),
                  _resident((3, f)), _rows_spec(w_down)],
        out_specs=[pl.BlockSpec((tt, d), lambda i: (i, 0)), pl.BlockSpec((tt, 2 * f), lambda i: (i, 0))],
        out_shape=[SDS((t_all, d), F32), SDS((t_all, 2 * f), BF16)],
        scratch_shapes=[pltpu.VMEM((tt + 8, f), F32)],
        compiler_params=_cparams(),
    )(x, gain, w_in_t.arr, w_conv, w_down.arr)


def _rope_partner(xs, lane_lo):
    return jnp.where(lane_lo, pltpu.roll(xs, LANES - HEAD_DIM // 2, 1), pltpu.roll(xs, HEAD_DIM // 2, 1))


def _fwd_qkv(x, gain, w_qkv_t, b_qkv, cos_t, sin_t, seq, name):
    t_all, d = x.shape
    width = w_qkv_t.n * N_DEV
    kvw = (width - d) // 2
    tt = _token_tile(seq)
    tps = seq // tt
    scale = HEAD_DIM ** -0.5

    def body(x_ref, g_ref, w_ref, b_ref, cos_ref, sin_ref, qkv_ref):
        xv = x_ref[...]
        r, xh = _rms_parts(xv)
        h = (xh * g_ref[...]).astype(BF16)
        qkv = _nt(h, _mat(w_ref)) + b_ref[...]
        cosv = cos_ref[...]
        sinv = sin_ref[...]
        lane_lo = (lax.broadcasted_iota(jnp.int32, (tt, LANES), 1) % HEAD_DIM) < HEAD_DIM // 2
        for s in range((d + kvw) // LANES):
            xs = qkv[:, s * LANES:(s + 1) * LANES]
            roped = xs * cosv + _rope_partner(xs, lane_lo) * sinv
            if s * LANES < d:
                roped = roped * scale
            qkv_ref[:, s * LANES:(s + 1) * LANES] = roped.astype(BF16)
        qkv_ref[:, d + kvw:] = qkv[:, d + kvw:].astype(BF16)

    return pl.pallas_call(
        body, name=name, grid=(t_all // tt,),
        in_specs=[pl.BlockSpec((tt, d), lambda i: (i, 0)), _resident((1, d)), _rows_spec(w_qkv_t),
                  _resident((1, width)), pl.BlockSpec((tt, LANES), lambda i: (i % tps, 0)),
                  pl.BlockSpec((tt, LANES), lambda i: (i % tps, 0))],
        out_specs=pl.BlockSpec((tt, width), lambda i: (i, 0)),
        out_shape=SDS((t_all, width), BF16),
        compiler_params=_cparams(),
    )(x, gain, w_qkv_t.arr, b_qkv, cos_t, sin_t)


def _band_masks():
    rows = GROUP * WINDOW
    r = lax.broadcasted_iota(jnp.int32, (rows, 2 * WINDOW), 0) % WINDOW
    j = lax.broadcasted_iota(jnp.int32, (rows, 2 * WINDOW), 1)
    base = (j > r) & (j <= r + WINDOW)
    return base, base & (j >= WINDOW)


def _sink_column(sink_ref, kh):
    rows = GROUP * WINDOW
    g_of_row = lax.broadcasted_iota(jnp.int32, (rows, 1), 0) // WINDOW
    col = jnp.zeros((rows, 1), F32)
    for g in range(GROUP):
        col = jnp.where(g_of_row == g, sink_ref[kh * GROUP + g], col)
    return col


def _stack_heads(ref, row0, kh):
    return jnp.concatenate(
        [ref[row0:row0 + WINDOW, (kh * GROUP + g) * HEAD_DIM:(kh * GROUP + g + 1) * HEAD_DIM] for g in range(GROUP)],
        axis=0)


def _softmax_with_sink(s, valid, sink_col):
    s = jnp.where(valid, s, jnp.finfo(F32).min)
    m = jnp.maximum(jnp.max(s, axis=-1, keepdims=True), sink_col)
    p = jnp.exp(s - m)
    e_sink = jnp.exp(sink_col - m)
    inv = 1.0 / (jnp.sum(p, axis=-1, keepdims=True) + e_sink)
    return p * inv, e_sink * inv


def _fwd_attention(qkv, x, sinks, w_o, b_o, seq, name):
    t_all, d = x.shape
    width = qkv.shape[1]
    kvw = (width - d) // 2
    n_kv = kvw // HEAD_DIM
    tt = _token_tile(seq)
    tps = seq // tt
    nblk = tt // WINDOW

    def body(sink_ref, qkv_ref, kvp_ref, x_ref, wo_ref, bo_ref, x1_ref, o_ref, kvext_ref, oscr_ref):
        i = pl.program_id(0)
        kvext_ref[0:WINDOW, :] = kvp_ref[...]
        kvext_ref[WINDOW:, :] = qkv_ref[:, d:]
        base, base_first = _band_masks()
        not_first = jnp.full(base.shape, i % tps != 0)
        for n in range(nblk):
            valid = (base_first | (base & not_first)) if n == 0 else base
            for kh in range(n_kv):
                qs = _stack_heads(qkv_ref, n * WINDOW, kh)
                kb = kvext_ref[n * WINDOW:(n + 2) * WINDOW, kh * HEAD_DIM:(kh + 1) * HEAD_DIM]
                vb = kvext_ref[n * WINDOW:(n + 2) * WINDOW, kvw + kh * HEAD_DIM:kvw + (kh + 1) * HEAD_DIM]
                probs, _ = _softmax_with_sink(_nt(qs, kb), valid, _sink_column(sink_ref, kh))
                o_s = _nn(probs.astype(BF16), vb)
                for g in range(GROUP):
                    hd = kh * GROUP + g
                    oscr_ref[n * WINDOW:(n + 1) * WINDOW, hd * HEAD_DIM:(hd + 1) * HEAD_DIM] = (
                        o_s[g * WINDOW:(g + 1) * WINDOW].astype(BF16))
        o = oscr_ref[...]
        o_ref[...] = o
        x1_ref[...] = x_ref[...] + _nn(o, _mat(wo_ref)) + bo_ref[...]

    kv_blocks = tt // WINDOW
    return pl.pallas_call(
        body, name=name, grid=(t_all // tt,),
        in_specs=[pl.BlockSpec(memory_space=pltpu.SMEM),
                  pl.BlockSpec((tt, width), lambda i: (i, 0)),
                  pl.BlockSpec((WINDOW, 2 * kvw), lambda i: (jnp.maximum(i * kv_blocks - 1, 0), d // (2 * kvw))),
                  pl.BlockSpec((tt, d), lambda i: (i, 0)), _rows_spec(w_o), _resident((1, d))],
        out_specs=[pl.BlockSpec((tt, d), lambda i: (i, 0)), pl.BlockSpec((tt, d), lambda i: (i, 0))],
        out_shape=[SDS((t_all, d), F32), SDS((t_all, d), BF16)],
        scratch_shapes=[pltpu.VMEM((tt + WINDOW, 2 * kvw), BF16), pltpu.VMEM((tt, d), BF16)],
        compiler_params=_cparams(),
    )(sinks, qkv, qkv, x, w_o.arr, b_o)


def _final_norm_loss(x, gain, target, name):
    t_all, d = x.shape
    tt = min(TOKEN_TILE, t_all)

    def body(x_ref, g_ref, t_ref, dx_ref, dg_ref, loss_ref):
        i = pl.program_id(0)
        xv = x_ref[...]
        r, xh = _rms_parts(xv)
        gain_v = g_ref[...]
        e = xh * gain_v - t_ref[...]
        dy = e * (1.0 / d)
        dx_ref[...] = _rms_backward(dy, xh, r, gain_v, 0.0)

        @pl.when(i == 0)
        def _():
            dg_ref[...] = jnp.zeros_like(dg_ref)
            loss_ref[...] = jnp.zeros_like(loss_ref)

        dg_ref[...] += jnp.sum(dy * xh, axis=0, keepdims=True)
        loss_ref[...] += (0.5 / d) * jnp.sum(e * e, axis=0, keepdims=True)

    return pl.pallas_call(
        body, name=name, grid=(t_all // tt,),
        in_specs=[pl.BlockSpec((tt, d), lambda i: (i, 0)), _resident((1, d)), pl.BlockSpec((tt, d), lambda i: (i, 0))],
        out_specs=[pl.BlockSpec((tt, d), lambda i: (i, 0)), pl.BlockSpec((1, d), lambda i: (0, 0)),
                   pl.BlockSpec((1, d), lambda i: (0, 0))],
        out_shape=[SDS((t_all, d), F32), SDS((1, d), F32), SDS((1, d), F32)],
        compiler_params=_cparams(),
    )(x, gain, target)


def _bwd_ffn_inner(dx2, gu, w_conv, w_down, seq, name):
    t_all, d = dx2.shape
    f = w_down.n * N_DEV
    tt = _token_tile(seq) // 2
    tps = seq // tt
    nt = t_all // tt
    halo = 16

    def body(dx_ref, gu_ref, prev_ref, wc_ref, wd_ref, dgu_ref, a_ref, dwc_ref, cext_ref, aext_ref):
        i = pl.program_id(0)
        ti = nt - 1 - i
        da = _nt(dx_ref[...].astype(BF16), _mat(wd_ref))
        gu_v = gu_ref[...].astype(F32)
        g = gu_v[:, :f]
        u = gu_v[:, f:]
        prev = prev_ref[...].astype(F32)[halo - 8:, :]
        cext_ref[0:8, :] = jnp.where(ti % tps == 0, 0.0, prev)
        gc = _causal_conv3(cext_ref, g, wc_ref, tt)
        sig = _sigmoid(gc)
        s = gc * sig
        a_ref[...] = (s * u).astype(BF16)
        dgc = da * u * (sig * (1.0 + gc * (1.0 - sig)))

        @pl.when(ti % tps == tps - 1)
        def _():
            aext_ref[tt:tt + 8, :] = jnp.zeros((8, f), F32)

        dg, sh1, sh2 = _anticausal_conv3(aext_ref, dgc, wc_ref, tt)
        aext_ref[tt:tt + 8, :] = aext_ref[0:8, :]
        dgu_ref[:, :f] = dg.astype(BF16)
        dgu_ref[:, f:] = (da * s).astype(BF16)

        @pl.when(i == 0)
        def _():
            dwc_ref[...] = jnp.zeros_like(dwc_ref)

        dwc_ref[0:1, :] += jnp.sum(g * sh2, axis=0, keepdims=True)
        dwc_ref[1:2, :] += jnp.sum(g * sh1, axis=0, keepdims=True)
        dwc_ref[2:3, :] += jnp.sum(g * dgc, axis=0, keepdims=True)

    return pl.pallas_call(
        body, name=name, grid=(nt,),
        in_specs=[pl.BlockSpec((tt, d), lambda i: (nt - 1 - i, 0)),
                  pl.BlockSpec((tt, 2 * f), lambda i: (nt - 1 - i, 0)),
                  pl.BlockSpec((halo, f), lambda i: (jnp.maximum((nt - 1 - i) * (tt // halo) - 1, 0), 0)),
                  _resident((3, f)), _rows_spec(w_down)],
        out_specs=[pl.BlockSpec((tt, 2 * f), lambda i: (nt - 1 - i, 0)), pl.BlockSpec((tt, f), lambda i: (nt - 1 - i, 0)),
                   pl.BlockSpec((8, f), lambda i: (0, 0))],
        out_shape=[SDS((t_all, 2 * f), BF16), SDS((t_all, f), BF16), SDS((8, f), F32)],
        scratch_shapes=[pltpu.VMEM((tt + 8, f), F32), pltpu.VMEM((tt + 8, f), F32)],
        compiler_params=_cparams(),
    )(dx2, gu, gu, w_conv, w_down.arr)


def _bwd_conv_inner(dx1, bcv, w_conv, w_out, seq, name):
    t_all, d = dx1.shape
    tt = _token_tile(seq)
    tps = seq // tt
    nt = t_all // tt
    halo = 16

    def body(dx_ref, bcv_ref, prev_ref, wc_ref, wout_ref, dbcv_ref, y_ref, dwc_ref, cext_ref, aext_ref):
        i = pl.program_id(0)
        ti = nt - 1 - i
        dy = _nt(dx_ref[...].astype(BF16), _mat(wout_ref))
        bcv_v = bcv_ref[...].astype(F32)
        b = bcv_v[:, :d]
        c = bcv_v[:, d:2 * d]
        v = bcv_v[:, 2 * d:]
        cv = c * v
        prev = prev_ref[...].astype(F32)[halo - 8:, :]
        cext_ref[0:8, :] = jnp.where(ti % tps == 0, 0.0, prev[:, d:2 * d] * prev[:, 2 * d:])
        cc = _causal_conv3(cext_ref, cv, wc_ref, tt)
        y_ref[...] = (b * cc).astype(BF16)
        dcc = dy * b

        @pl.when(ti % tps == tps - 1)
        def _():
            aext_ref[tt:tt + 8, :] = jnp.zeros((8, d), F32)

        dcv, sh1, sh2 = _anticausal_conv3(aext_ref, dcc, wc_ref, tt)
        aext_ref[tt:tt + 8, :] = aext_ref[0:8, :]
        dbcv_ref[:, :d] = (dy * cc).astype(BF16)
        dbcv_ref[:, d:2 * d] = (dcv * v).astype(BF16)
        dbcv_ref[:, 2 * d:] = (dcv * c).astype(BF16)

        @pl.when(i == 0)
        def _():
            dwc_ref[...] = jnp.zeros_like(dwc_ref)

        dwc_ref[0:1, :] += jnp.sum(cv * sh2, axis=0, keepdims=True)
        dwc_ref[1:2, :] += jnp.sum(cv * sh1, axis=0, keepdims=True)
        dwc_ref[2:3, :] += jnp.sum(cv * dcc, axis=0, keepdims=True)

    return pl.pallas_call(
        body, name=name, grid=(nt,),
        in_specs=[pl.BlockSpec((tt, d), lambda i: (nt - 1 - i, 0)),
                  pl.BlockSpec((tt, 3 * d), lambda i: (nt - 1 - i, 0)),
                  pl.BlockSpec((halo, 3 * d), lambda i: (jnp.maximum((nt - 1 - i) * (tt // halo) - 1, 0), 0)),
                  _resident((3, d)), _rows_spec(w_out)],
        out_specs=[pl.BlockSpec((tt, 3 * d), lambda i: (nt - 1 - i, 0)), pl.BlockSpec((tt, d), lambda i: (nt - 1 - i, 0)),
                   pl.BlockSpec((8, d), lambda i: (0, 0))],
        out_shape=[SDS((t_all, 3 * d), BF16), SDS((t_all, d), BF16), SDS((8, d), F32)],
        scratch_shapes=[pltpu.VMEM((tt + 8, d), F32), pltpu.VMEM((tt + 8, d), F32)],
        compiler_params=_cparams(),
    )(dx1, bcv, bcv, w_conv, w_out.arr)


def _bwd_attention_inner(dx1, qkv, sinks, w_o, cos_t, sin_t, seq, name):
    t_all, d = dx1.shape
    width = qkv.shape[1]
    kvw = (width - d) // 2
    n_kv = kvw // HEAD_DIM
    tt = _token_tile(seq)
    tps = seq // tt
    nt = t_all // tt
    nblk = tt // WINDOW
    scale = HEAD_DIM ** -0.5

    def body(sink_ref, dx_ref, qkv_ref, kvp_ref, cos_ref, sin_ref, wo_ref,
             dqkv_ref, dsink_ref, dbqkv_ref, dbo_ref,
             kvext_ref, dkvext_ref, carry_ref, dq_ref, do_ref):
        i = pl.program_id(0)
        ti = nt - 1 - i
        dxv = dx_ref[...]
        do_ref[...] = _nt(dxv.astype(BF16), _mat(wo_ref)).astype(BF16)
        kvext_ref[0:WINDOW, :] = kvp_ref[...]
        kvext_ref[WINDOW:, :] = qkv_ref[:, d:]
        dkvext_ref[...] = jnp.zeros_like(dkvext_ref)

        @pl.when(i == 0)
        def _():
            carry_ref[...] = jnp.zeros_like(carry_ref)
            dsink_ref[...] = jnp.zeros_like(dsink_ref)
            dbqkv_ref[...] = jnp.zeros_like(dbqkv_ref)
            dbo_ref[...] = jnp.zeros_like(dbo_ref)

        base, base_first = _band_masks()
        not_first = jnp.full(base.shape, ti % tps != 0)
        head_lane = lax.broadcasted_iota(jnp.int32, (1, LANES), 1)
        dsink = jnp.zeros((1, LANES), F32)
        for n in range(nblk):
            valid = (base_first | (base & not_first)) if n == 0 else base
            for kh in range(n_kv):
                qs = _stack_heads(qkv_ref, n * WINDOW, kh)
                dos = _stack_heads(do_ref, n * WINDOW, kh)
                kcols = slice(kh * HEAD_DIM, (kh + 1) * HEAD_DIM)
                vcols = slice(kvw + kh * HEAD_DIM, kvw + (kh + 1) * HEAD_DIM)
                band = slice(n * WINDOW, (n + 2) * WINDOW)
                kb = kvext_ref[band, kcols]
                vb = kvext_ref[band, vcols]
                probs, p_sink = _softmax_with_sink(_nt(qs, kb), valid, _sink_column(sink_ref, kh))
                dp = _nt(dos, vb)
                dsum = jnp.sum(probs * dp, axis=-1, keepdims=True)
                ds = (probs * (dp - dsum)).astype(BF16)
                dkvext_ref[band, vcols] += _tn(probs.astype(BF16), dos)
                dkvext_ref[band, kcols] += _tn(ds, qs)
                dq_s = _nn(ds, kb)
                sink_terms = p_sink * dsum
                for g in range(GROUP):
                    hd = kh * GROUP + g
                    dq_ref[n * WINDOW:(n + 1) * WINDOW, hd * HEAD_DIM:(hd + 1) * HEAD_DIM] = dq_s[g * WINDOW:(g + 1) * WINDOW]
                    dsink = dsink - jnp.where(head_lane == hd, jnp.sum(sink_terms[g * WINDOW:(g + 1) * WINDOW]), 0.0)
        dsink_ref[0:1, :] += dsink
        dkvext_ref[tt:tt + WINDOW, :] += carry_ref[...]
        carry_ref[...] = dkvext_ref[0:WINDOW, :]

        cosv = cos_ref[...]
        sinv = sin_ref[...]
        lane_lo = (lax.broadcasted_iota(jnp.int32, (tt, LANES), 1) % HEAD_DIM) < HEAD_DIM // 2
        for s in range((d + kvw) // LANES):
            if s * LANES < d:
                dy = dq_ref[:, s * LANES:(s + 1) * LANES] * scale
            else:
                dy = dkvext_ref[WINDOW:, s * LANES - d:(s + 1) * LANES - d]
            dpre = dy * cosv - _rope_partner(dy, lane_lo) * sinv
            dqkv_ref[:, s * LANES:(s + 1) * LANES] = dpre.astype(BF16)
            dbqkv_ref[0:1, s * LANES:(s + 1) * LANES] += jnp.sum(dpre, axis=0, keepdims=True)
        dv = dkvext_ref[WINDOW:, kvw:]
        dqkv_ref[:, d + kvw:] = dv.astype(BF16)
        dbqkv_ref[0:1, d + kvw:] += jnp.sum(dv, axis=0, keepdims=True)
        dbo_ref[...] += jnp.sum(dxv, axis=0, keepdims=True)

    kv_blocks = tt // WINDOW
    return pl.pallas_call(
        body, name=name, grid=(nt,),
        in_specs=[pl.BlockSpec(memory_space=pltpu.SMEM),
                  pl.BlockSpec((tt, d), lambda i: (nt - 1 - i, 0)),
                  pl.BlockSpec((tt, width), lambda i: (nt - 1 - i, 0)),
                  pl.BlockSpec((WINDOW, 2 * kvw), lambda i: (jnp.maximum((nt - 1 - i) * kv_blocks - 1, 0), d // (2 * kvw))),
                  pl.BlockSpec((tt, LANES), lambda i: ((nt - 1 - i) % tps, 0)),
                  pl.BlockSpec((tt, LANES), lambda i: ((nt - 1 - i) % tps, 0)),
                  _rows_spec(w_o)],
        out_specs=[pl.BlockSpec((tt, width), lambda i: (nt - 1 - i, 0)), pl.BlockSpec((8, LANES), lambda i: (0, 0)),
                   pl.BlockSpec((1, width), lambda i: (0, 0)), pl.BlockSpec((1, d), lambda i: (0, 0))],
        out_shape=[SDS((t_all, width), BF16), SDS((8, LANES), F32), SDS((1, width), F32), SDS((1, d), F32)],
        scratch_shapes=[pltpu.VMEM((tt + WINDOW, 2 * kvw), BF16), pltpu.VMEM((tt + WINDOW, 2 * kvw), F32),
                        pltpu.VMEM((WINDOW, 2 * kvw), F32), pltpu.VMEM((tt, d), F32), pltpu.VMEM((tt, d), BF16)],
        compiler_params=_cparams(),
    )(sinks, dx1, qkv, qkv, cos_t, sin_t, w_o.arr)


def _bwd_dense_norm(dy, w_t, x, gain, dres, name):
    t_all, d = x.shape
    n = dy.shape[1]
    tt = min(TOKEN_TILE, t_all)

    def body(dy_ref, w_ref, x_ref, g_ref, dres_ref, dx_ref, h_ref, dg_ref):
        i = pl.program_id(0)
        dh = _nn(dy_ref[...], _mat(w_ref))
        r, xh = _rms_parts(x_ref[...])
        gain_v = g_ref[...]
        h_ref[...] = (xh * gain_v).astype(BF16)
        dx_ref[...] = _rms_backward(dh, xh, r, gain_v, dres_ref[...])

        @pl.when(i == 0)
        def _():
            dg_ref[...] = jnp.zeros_like(dg_ref)

        dg_ref[...] += jnp.sum(dh * xh, axis=0, keepdims=True)

    return pl.pallas_call(
        body, name=name, grid=(t_all // tt,),
        in_specs=[pl.BlockSpec((tt, n), lambda i: (i, 0)), _rows_spec(w_t), pl.BlockSpec((tt, d), lambda i: (i, 0)),
                  _resident((1, d)), pl.BlockSpec((tt, d), lambda i: (i, 0))],
        out_specs=[pl.BlockSpec((tt, d), lambda i: (i, 0)), pl.BlockSpec((tt, d), lambda i: (i, 0)),
                   pl.BlockSpec((1, d), lambda i: (0, 0))],
        out_shape=[SDS((t_all, d), F32), SDS((t_all, d), BF16), SDS((1, d), F32)],
        compiler_params=_cparams(),
    )(dy, w_t.arr, x, gain, dres)


def _tn_matmul(a, b, dest, name):
    t_all, m = a.shape
    d = b.shape[1]
    n = dest.n
    assert m == N_DEV * n and dest.off % n == 0
    k = max(kk for kk in (1, 2, 4, 8) if kk * n <= max(n, 1536))
    tm = k * n
    tt = min(TOKEN_TILE, t_all)
    n_t = t_all // tt
    fresh = not hasattr(dest.arr, "dtype")

    def body(a_ref, b_ref, *rest):
        o_ref, acc_ref = rest[-2:]
        t = pl.program_id(1)

        @pl.when(t == 0)
        def _():
            acc_ref[...] = jnp.zeros_like(acc_ref)

        acc_ref[...] += _tn(a_ref[...], b_ref[...].astype(BF16))

        @pl.when(t == n_t - 1)
        def _():
            o_ref[...] = acc_ref[...].astype(BF16).reshape(k, n, d)

    block = dest.off // n
    return pl.pallas_call(
        body, name=name, grid=(m // tm, n_t),
        in_specs=[pl.BlockSpec((tt, tm), lambda j, t: (t, j)), pl.BlockSpec((tt, d), lambda j, t: (t, 0))] + ([] if fresh else [ANY]),
        out_specs=pl.BlockSpec((k, n, d), lambda j, t: (j, block, 0)),
        out_shape=SDS(tuple(dest.arr) if fresh else dest.arr.shape, BF16),
        scratch_shapes=[pltpu.VMEM((tm, d), F32)],
        input_output_aliases={} if fresh else {2: 0},
        compiler_params=_cparams(2),
    )(*((a, b) if fresh else (a, b, dest.arr)))


def _my_place():
    return lax.axis_index("x"), lax.axis_index("y"), lax.axis_index("c")


def _other_chips(x, y):
    return [(1 - x, y), (x, 1 - y), (1 - x, 1 - y)]


def _all_gather(blocks, name):
    n_arr = len(blocks)

    def body(*refs):
        in_refs = refs[:n_arr]
        out_refs = refs[n_arr:2 * n_arr]
        send_sems, recv_sems, local_sems = refs[2 * n_arr:]
        x, y, c = _my_place()
        me, sibling = (x, y, c), (x, y, 1 - c)
        chips = _other_chips(x, y)

        def slot(a, place):
            px, py, pc = place
            return out_refs[a].at[4 * px + 2 * py + pc]

        def copy(a, k, block, to, src=None):
            return pltpu.make_async_remote_copy(
                src_ref=slot(a, block) if src is None else src, dst_ref=slot(a, block),
                send_sem=send_sems.at[a, k], recv_sem=recv_sems.at[a, k], device_id=to, device_id_type=MESH)

        started = []
        local = []
        for a in range(n_arr):
            mine = pltpu.make_async_copy(in_refs[a], slot(a, me), local_sems.at[a])
            mine.start()
            local.append(mine)
            first = [copy(a, 0, me, sibling, src=in_refs[a])]
            first += [copy(a, 1 + j, me, (*chip, c), src=in_refs[a]) for j, chip in enumerate(chips)]
            for cp in first:
                cp.start()
            started += first
        for a in range(n_arr):
            for j, chip in enumerate(chips):
                copy(a, 1 + j, (*chip, c), me).wait_recv()
                passed = copy(a, 4 + j, (*chip, c), sibling)
                passed.start()
                started.append(passed)
        for a in range(n_arr):
            copy(a, 0, sibling, me).wait_recv()
            for j, chip in enumerate(chips):
                copy(a, 4 + j, (*chip, 1 - c), me).wait_recv()
        for cp in started:
            cp.wait_send()
        for mine in local:
            mine.wait()

    return pl.pallas_call(
        body, name=name,
        in_specs=[ANY] * n_arr, out_specs=[ANY] * n_arr,
        out_shape=[SDS((N_DEV,) + b.shape, b.dtype) for b in blocks],
        scratch_shapes=[pltpu.SemaphoreType.DMA((n_arr, 7)), pltpu.SemaphoreType.DMA((n_arr, 7)),
                        pltpu.SemaphoreType.DMA((n_arr,))],
    )(*blocks)


def _peer_of(k, x, y, c):
    return x ^ ((k >> 2) & 1), y ^ ((k >> 1) & 1), c ^ (k & 1)


HBM = pl.BlockSpec(memory_space=pltpu.HBM)
SEM = pl.BlockSpec(memory_space=pltpu.SEMAPHORE)
DATAFLOW_EFFECT = pltpu.SideEffectType.DATAFLOW_SIDE_EFFECTING


def _peer_copies(src_ref, land_ref, send_sems, recv_sems, per_peer):
    x, y, c = _my_place()
    me = 4 * x + 2 * y + c
    copies = []
    for k in range(1, N_DEV):
        px, py, pc = _peer_of(k, x, y, c)
        peer = 4 * px + 2 * py + pc
        copies.append(pltpu.make_async_remote_copy(
            src_ref=src_ref.at[peer] if per_peer else src_ref, dst_ref=land_ref.at[me],
            send_sem=send_sems.at[k - 1], recv_sem=recv_sems.at[k - 1], device_id=(px, py, pc), device_id_type=MESH))
    own = pltpu.make_async_copy(src_ref.at[me] if per_peer else src_ref, land_ref.at[me], send_sems.at[N_DEV - 1])
    return copies, own


def _exchange_start(src, after, per_peer, name):
    rows, d = src.shape[-2:]

    def body(src_ref, land_ref, after_ref, send_sems, recv_sems, src_thru, land_thru, token):
        copies, own = _peer_copies(src_ref, land_ref, send_sems, recv_sems, per_peer)
        for cp in copies:
            cp.start()
        own.start()
        token[...] = jnp.zeros_like(token)

    return pl.pallas_call(
        body, name=name,
        out_shape=(pltpu.SemaphoreType.DMA((N_DEV,)), pltpu.SemaphoreType.DMA((N_DEV - 1,)), pltpu.HBM(src.shape, src.dtype),
                   pltpu.HBM((N_DEV, rows, d), src.dtype), SDS((SUBLANES, LANES), F32)),
        in_specs=(HBM, HBM, ANY), out_specs=(SEM, SEM, HBM, HBM, pl.BlockSpec(memory_space=pltpu.VMEM)),
        input_output_aliases={0: 2, 1: 3},
        compiler_params=pltpu.CompilerParams(has_side_effects=DATAFLOW_EFFECT),
    )(pltpu.with_memory_space_constraint(src, pltpu.HBM),
      pltpu.with_memory_space_constraint(lax.empty((N_DEV, rows, d), src.dtype), pltpu.HBM), after)


def _exchange_wait(started, after, per_peer, name):
    send_sems, recv_sems, src_thru, land_thru, _ = started

    def body(src_ref, land_ref, send_sems, recv_sems, after_ref, src_out, land_out):
        copies, own = _peer_copies(src_ref, land_ref, send_sems, recv_sems, per_peer)
        for cp in copies:
            cp.wait_send()
            cp.wait_recv()
        own.wait()

    return pl.pallas_call(
        body, name=name,
        out_shape=(pltpu.HBM(src_thru.shape, src_thru.dtype), pltpu.HBM(land_thru.shape, land_thru.dtype)),
        in_specs=(HBM, HBM, SEM, SEM, ANY), out_specs=(HBM, HBM), input_output_aliases={0: 0, 1: 1},
        compiler_params=pltpu.CompilerParams(has_side_effects=DATAFLOW_EFFECT),
    )(src_thru, land_thru, send_sems, recv_sems, after)


def _sum_slots(slots, name):
    _, rows, d = slots.shape
    tr = _largest_divisor(rows, 512, 16)

    def body(s_ref, o_ref):
        acc = s_ref[0].astype(F32)
        for dev in range(1, N_DEV):
            acc = acc + s_ref[dev].astype(F32)
        o_ref[...] = acc

    return pl.pallas_call(
        body, name=name, grid=(rows // tr,),
        in_specs=[pl.BlockSpec((N_DEV, tr, d), lambda r: (0, r, 0))], out_specs=pl.BlockSpec((tr, d), lambda r: (r, 0)),
        out_shape=SDS((rows, d), F32), compiler_params=_cparams(),
    )(slots)


def _all_reduce_small(part, loss_rows, name):
    rows, lanes = part.shape
    lo, hi = loss_rows

    def body(x_ref, out_ref, loss_ref, gath_ref, send_sems, recv_sems):
        x, y, c = _my_place()
        me = 4 * x + 2 * y + c
        gath_ref[me] = x_ref[...]
        copies = []
        for k in range(1, N_DEV):
            peer = (x ^ ((k >> 2) & 1), y ^ ((k >> 1) & 1), c ^ (k & 1))
            cp = pltpu.make_async_remote_copy(
                src_ref=x_ref, dst_ref=gath_ref.at[me], send_sem=send_sems.at[k - 1], recv_sem=recv_sems.at[k - 1],
                device_id=peer, device_id_type=MESH)
            cp.start()
            copies.append(cp)
        for cp in copies:
            cp.wait_recv()
        for cp in copies:
            cp.wait_send()
        acc = gath_ref[0]
        for dev in range(1, N_DEV):
            acc = acc + gath_ref[dev]
        out_ref[...] = acc
        loss_ref[...] = jnp.full(loss_ref.shape, jnp.sum(acc[lo:hi, :]), F32)

    vmem = pl.BlockSpec(memory_space=pltpu.VMEM)
    return pl.pallas_call(
        body, name=name, in_specs=[vmem], out_specs=[vmem, vmem],
        out_shape=[SDS((rows, lanes), F32), SDS((SUBLANES, LANES), F32)],
        scratch_shapes=[pltpu.VMEM((N_DEV, rows, lanes), F32), pltpu.SemaphoreType.DMA((N_DEV - 1,)),
                        pltpu.SemaphoreType.DMA((N_DEV - 1,))],
    )(part)


def _adamw(w, g, m, v, name):
    rows, cols = w.shape
    tr = rows if rows % SUBLANES else _largest_divisor(rows, 512, SUBLANES)

    def body(w_ref, g_ref, m_ref, v_ref, d_ref, nm_ref, nv_ref):
        gv = g_ref[...]
        nm = ADAM_B1 * m_ref[...] + (1.0 - ADAM_B1) * gv
        nv = ADAM_B2 * v_ref[...] + (1.0 - ADAM_B2) * (gv * gv)
        m_hat = nm / (1.0 - ADAM_B1 ** ADAM_STEP)
        v_hat = nv / (1.0 - ADAM_B2 ** ADAM_STEP)
        d_ref[...] = -ADAM_LR * (m_hat / (jnp.sqrt(v_hat) + ADAM_EPS) + ADAM_WD * w_ref[...])
        nm_ref[...] = nm
        nv_ref[...] = nv

    spec = pl.BlockSpec((tr, cols), lambda i: (i, 0))
    return pl.pallas_call(
        body, name=name, grid=(rows // tr,), in_specs=[spec] * 4, out_specs=[spec] * 3,
        out_shape=[SDS((rows, cols), F32)] * 3, compiler_params=_cparams(),
    )(w, g, m, v)


def _adamw_nd(w, g, m, v, name):
    shape = w.shape
    two_d = (1, shape[0]) if len(shape) == 1 else (-1, shape[-1])
    outs = _adamw(w.reshape(two_d), g.reshape(two_d), m.reshape(two_d), v.reshape(two_d), name)
    return [o.reshape(shape) for o in outs]


def _rope_tables(seq):
    pos = jnp.arange(seq, dtype=F32)
    inv_freq = 1.0 / (ROPE_THETA ** (jnp.arange(0, HEAD_DIM, 2, dtype=F32) / HEAD_DIM))
    ang = pos[:, None] * inv_freq[None, :]
    cos, sin = jnp.cos(ang), jnp.sin(ang)
    reps = LANES // HEAD_DIM
    cos_t = jnp.tile(jnp.concatenate([cos, cos], axis=1), (1, reps))
    sin_t = jnp.tile(jnp.concatenate([-sin, sin], axis=1), (1, reps))
    return cos_t, sin_t


def _flat_pad(a):
    flat = a.reshape(1, -1)
    pad = (-flat.shape[1]) % LANES
    return jnp.pad(flat, ((0, 0), (0, pad))) if pad else flat


def kernel(x, norm_mix, norm_ffn, norm_final, conv_w_in, conv_w_conv, conv_w_out, attn_w_qkv, attn_b_qkv, attn_sinks, attn_w_o, attn_b_o, ffn_w_in, ffn_w_conv, ffn_w_down, loss_target, m_norm_mix, m_norm_ffn, m_norm_final, m_conv_w_in, m_conv_w_conv, m_conv_w_out, m_attn_w_qkv, m_attn_b_qkv, m_attn_sinks, m_attn_w_o, m_attn_b_o, m_ffn_w_in, m_ffn_w_conv, m_ffn_w_down, v_norm_mix, v_norm_ffn, v_norm_final, v_conv_w_in, v_conv_w_conv, v_conv_w_out, v_attn_w_qkv, v_attn_b_qkv, v_attn_sinks, v_attn_w_o, v_attn_b_o, v_ffn_w_in, v_ffn_w_conv, v_ffn_w_down):
    b_loc, seq, d = x.shape
    depth = norm_mix.shape[0]
    n_conv, n_attn = conv_w_in.shape[0], attn_w_qkv.shape[0]
    t_all = b_loc * seq
    my_x, my_y, my_c = _my_place()

    me = 4 * my_x + 2 * my_y + my_c

    groups = []
    for i in range(depth):
        j = i // 2
        if i % 2 == 0:
            mix = [("conv_w_in", j, True, conv_w_in[j].T), ("conv_w_out", j, False, conv_w_out[j])]
        else:
            mix = [("attn_w_qkv", j, True, attn_w_qkv[j].T), ("attn_w_o", j, False, attn_w_o[j])]
        groups.append((("mix", i), mix))
        groups.append((("ffn", i), [("ffn_w_in", i, True, ffn_w_in[i].T), ("ffn_w_down", i, False, ffn_w_down[i])]))
    order = [key for key, _ in groups]
    members_of = dict(groups)

    def layout(key):
        offs, o = [], 0
        for _, _, _, shard in members_of[key]:
            n = shard.shape[0]
            o = -(-o // n) * n
            offs.append(o)
            o += n
        return offs, o

    small = jnp.concatenate([_flat_pad(conv_w_conv), _flat_pad(ffn_w_conv), _flat_pad(attn_b_qkv), _flat_pad(attn_b_o)], axis=1)
    (small_g,) = _all_gather([small], "gather_small")

    gather_started = {}

    def start_gather(idx, after):
        if idx >= len(order):
            return 0.0
        key = order[idx]
        offs, total = layout(key)
        pieces, o = [], 0
        for (_, _, _, shard), off in zip(members_of[key], offs):
            if off > o:
                pieces.append(jnp.zeros((off - o, d), shard.dtype))
            pieces.append(shard)
            o = off + shard.shape[0]
        pack = jnp.concatenate(pieces, axis=0).astype(BF16)
        gather_started[key] = _exchange_start(pack, after, False, f"gather_start_{key[0]}_{key[1]}")
        return gather_started[key][4][0, 0]

    weights = {}

    def finish_gather(key, after):
        _, land = _exchange_wait(gather_started[key], after, False, f"gather_wait_{key[0]}_{key[1]}")
        for (wname, layer, _, shard), off in zip(members_of[key], layout(key)[0]):
            weights[(wname, layer)] = _Rows(land, off, shard.shape[0])

    def take_small(o, shape):
        size = shape[0] * shape[1] * shape[2]
        blk = small_g[:, 0, o:o + size].reshape((N_DEV,) + shape)
        return jnp.moveaxis(blk, 0, 2).reshape(shape[0], shape[1], N_DEV * shape[2])

    so = 0
    wc_conv_full = take_small(so, conv_w_conv.shape); so += _flat_pad(conv_w_conv).shape[1]
    wc_ffn_full = take_small(so, ffn_w_conv.shape); so += _flat_pad(ffn_w_conv).shape[1]
    b_qkv_full = take_small(so, (n_attn, 1, attn_b_qkv.shape[1]))[:, 0]; so += _flat_pad(attn_b_qkv).shape[1]
    b_o_full = take_small(so, (n_attn, 1, attn_b_o.shape[1]))[:, 0]

    cos_t, sin_t = _rope_tables(seq)

    xs = [x.reshape(t_all, d)]
    saved = []
    token = start_gather(0, small_g) + start_gather(1, small_g)
    for i in range(depth):
        j = i // 2
        gain_mix = norm_mix[i][None, :] + token
        finish_gather(("mix", i), gain_mix if i == 0 else xs[-1])
        if i % 2 == 0:
            x1, bcv = _fwd_conv_mixer(xs[-1], gain_mix, weights[("conv_w_in", j)], wc_conv_full[j],
                                      weights[("conv_w_out", j)], seq, f"fwd_conv_{i}")
            mix_saved = (bcv,)
        else:
            qkv = _fwd_qkv(xs[-1], gain_mix, weights[("attn_w_qkv", j)], b_qkv_full[j][None, :], cos_t, sin_t, seq,
                           f"fwd_qkv_{i}")
            x1, o = _fwd_attention(qkv, xs[-1], attn_sinks[j], weights[("attn_w_o", j)], b_o_full[j][None, :], seq,
                                   f"fwd_attn_{i}")
            mix_saved = (qkv, o)
        token = start_gather(2 * i + 2, x1) + start_gather(2 * i + 3, x1)
        gain_ffn = norm_ffn[i][None, :] + token
        finish_gather(("ffn", i), gain_ffn)
        x2, gu = _fwd_ffn(x1, gain_ffn, weights[("ffn_w_in", i)], wc_ffn_full[i], weights[("ffn_w_down", i)],
                          seq, f"fwd_ffn_{i}")
        saved.append((xs[-1], x1, mix_saved, gu))
        xs.append(x2)
        token = 0.0

    dx, dg_final, loss_lanes = _final_norm_loss(xs[-1], norm_final[None, :], loss_target.reshape(t_all, d), "loss_head")

    dg_mix, dg_ffn = [None] * depth, [None] * depth
    dwc_conv, dwc_ffn = [None] * n_conv, [None] * depth
    db_qkv, db_o, dsinks = [None] * n_attn, [None] * n_attn, [None] * n_attn
    scatter_started = {}

    def weight_grads(key, operands):
        offs, total = layout(key)
        parts = (N_DEV, total, d)
        for (wname, layer, _, shard), off, (a, b) in zip(members_of[key], offs, operands):
            parts = _tn_matmul(a, b, _Rows(parts, off, shard.shape[0]), f"dw_{wname}_{layer}")
        scatter_started[key] = _exchange_start(parts, operands[0][1], True, f"scatter_start_{key[0]}_{key[1]}")
        return scatter_started[key][4][0, 0]

    token = 0.0
    for i in reversed(range(depth)):
        j = i // 2
        x0, x1, mix_saved, gu = saved[i]
        dgu, act, dwc = _bwd_ffn_inner(dx, gu, wc_ffn_full[i] + token, weights[("ffn_w_down", i)], seq, f"bwd_ffn_{i}")
        dwc_ffn[i] = dwc[:3]
        dx1, h2, dg_ffn[i] = _bwd_dense_norm(dgu, weights[("ffn_w_in", i)], x1, norm_ffn[i][None, :], dx, f"bwd_ffn_norm_{i}")
        token = weight_grads(("ffn", i), [(dgu, h2), (act, dx)])
        if i % 2 == 0:
            (bcv,) = mix_saved
            dbcv, y, dwc = _bwd_conv_inner(dx1, bcv, wc_conv_full[j] + token, weights[("conv_w_out", j)], seq, f"bwd_conv_{i}")
            dwc_conv[j] = dwc[:3]
            dx, h, dg_mix[i] = _bwd_dense_norm(dbcv, weights[("conv_w_in", j)], x0, norm_mix[i][None, :], dx1,
                                               f"bwd_conv_norm_{i}")
            token = weight_grads(("mix", i), [(dbcv, h), (y, dx1)])
        else:
            qkv, o = mix_saved
            dqkv, dsk, dbq, dbo = _bwd_attention_inner(dx1, qkv, attn_sinks[j] + token, weights[("attn_w_o", j)], cos_t, sin_t,
                                                       seq, f"bwd_attn_{i}")
            dsinks[j], db_qkv[j], db_o[j] = dsk[0:1, :attn_sinks.shape[1]], dbq, dbo
            dx, h, dg_mix[i] = _bwd_dense_norm(dqkv, weights[("attn_w_qkv", j)], x0, norm_mix[i][None, :], dx1,
                                               f"bwd_attn_norm_{i}")
            token = weight_grads(("mix", i), [(dqkv, h), (o, dx1)])
    grad_x = dx.reshape(b_loc, seq, d)

    reduced = {}
    for key, members in reversed(groups):
        _, land = _exchange_wait(scatter_started[key], dx, True, f"scatter_wait_{key[0]}_{key[1]}")
        total = _sum_slots(land, f"scatter_sum_{key[0]}_{key[1]}")
        for (wname, layer, transposed, shard), off in zip(members, layout(key)[0]):
            rows = total[off:off + shard.shape[0]]
            reduced[(wname, layer)] = rows.T if transposed else rows

    small_parts = [jnp.concatenate(dg_mix, axis=0), jnp.concatenate(dg_ffn, axis=0), dg_final,
                   jnp.stack(dwc_conv), jnp.stack(dwc_ffn), jnp.concatenate(db_qkv, axis=0), jnp.concatenate(db_o, axis=0),
                   jnp.concatenate(dsinks, axis=0), loss_lanes]
    flats = [_flat_pad(p) for p in small_parts]
    bounds = []
    so = 0
    for fl in flats:
        bounds.append((so, so + fl.shape[1]))
        so += fl.shape[1]
    small_rows = so // LANES
    pad_rows = (-small_rows) % SUBLANES
    part_small = jnp.pad(jnp.concatenate(flats, axis=1).reshape(small_rows, LANES), ((0, pad_rows), (0, 0)))
    loss_rows = (bounds[-1][0] // LANES, bounds[-1][1] // LANES)
    summed, loss_tile = _all_reduce_small(part_small, loss_rows, "reduce_small")
    summed = summed.reshape(1, -1)

    def small_grad(k, shape):
        lo = bounds[k][0]
        size = 1
        for s_ in shape:
            size *= s_
        return summed[0, lo:lo + size].reshape(shape)

    def my_cols(full, n_local):
        return lax.dynamic_slice_in_dim(full, me * n_local, n_local, axis=full.ndim - 1)

    g_norm_mix = small_grad(0, norm_mix.shape)
    g_norm_ffn = small_grad(1, norm_ffn.shape)
    g_norm_final = small_grad(2, norm_final.shape)
    g_conv_w_conv = my_cols(small_grad(3, (n_conv, 3, d)), conv_w_conv.shape[2])
    g_ffn_w_conv = my_cols(small_grad(4, (depth, 3, ffn_w_conv.shape[2] * N_DEV)), ffn_w_conv.shape[2])
    g_attn_b_qkv = my_cols(small_grad(5, (n_attn, attn_b_qkv.shape[1] * N_DEV)), attn_b_qkv.shape[1])
    g_attn_b_o = my_cols(small_grad(6, (n_attn, d)), attn_b_o.shape[1])
    g_attn_sinks = small_grad(7, attn_sinks.shape)
    loss = loss_tile[0, 0]

    def big_grad(wname, n_layers):
        return jnp.stack([reduced[(wname, layer)] for layer in range(n_layers)])

    grads = {
        "norm_mix": g_norm_mix, "norm_ffn": g_norm_ffn, "norm_final": g_norm_final,
        "conv_w_in": big_grad("conv_w_in", n_conv), "conv_w_conv": g_conv_w_conv, "conv_w_out": big_grad("conv_w_out", n_conv),
        "attn_w_qkv": big_grad("attn_w_qkv", n_attn), "attn_b_qkv": g_attn_b_qkv, "attn_sinks": g_attn_sinks,
        "attn_w_o": big_grad("attn_w_o", n_attn), "attn_b_o": g_attn_b_o,
        "ffn_w_in": big_grad("ffn_w_in", depth), "ffn_w_conv": g_ffn_w_conv, "ffn_w_down": big_grad("ffn_w_down", depth),
    }
    params = {
        "norm_mix": (norm_mix, m_norm_mix, v_norm_mix), "norm_ffn": (norm_ffn, m_norm_ffn, v_norm_ffn),
        "norm_final": (norm_final, m_norm_final, v_norm_final), "conv_w_in": (conv_w_in, m_conv_w_in, v_conv_w_in),
        "conv_w_conv": (conv_w_conv, m_conv_w_conv, v_conv_w_conv), "conv_w_out": (conv_w_out, m_conv_w_out, v_conv_w_out),
        "attn_w_qkv": (attn_w_qkv, m_attn_w_qkv, v_attn_w_qkv), "attn_b_qkv": (attn_b_qkv, m_attn_b_qkv, v_attn_b_qkv),
        "attn_sinks": (attn_sinks, m_attn_sinks, v_attn_sinks), "attn_w_o": (attn_w_o, m_attn_w_o, v_attn_w_o),
        "attn_b_o": (attn_b_o, m_attn_b_o, v_attn_b_o), "ffn_w_in": (ffn_w_in, m_ffn_w_in, v_ffn_w_in),
        "ffn_w_conv": (ffn_w_conv, m_ffn_w_conv, v_ffn_w_conv), "ffn_w_down": (ffn_w_down, m_ffn_w_down, v_ffn_w_down),
    }
    order = list(params)
    deltas, new_ms, new_vs = [], [], []
    for wname in order:
        w, m, v = params[wname]
        dlt, nm, nv = _adamw_nd(w, grads[wname], m, v, f"adamw_{wname}")
        deltas.append(dlt); new_ms.append(nm); new_vs.append(nv)
    return (loss, grad_x, *[grads[wname] for wname in order], *deltas, *new_ms, *new_vs)
```

```python
from typing import NamedTuple

import jax
import jax.numpy as jnp
from jax import lax
from jax.experimental import pallas as pl
from jax.experimental.pallas import tpu as pltpu

F32 = jnp.float32
BF16 = jnp.bfloat16
SDS = jax.ShapeDtypeStruct
MESH = pl.DeviceIdType.MESH
ANY = pl.BlockSpec(memory_space=pl.ANY)

N_DEV = 8
EPS = 1e-5
HEAD_DIM = 64
GROUP = 4
WINDOW = 128
ROPE_THETA = 10000.0
ADAM_LR, ADAM_B1, ADAM_B2, ADAM_EPS, ADAM_WD, ADAM_STEP = 0.001, 0.9, 0.999, 1e-08, 0.01, 10

V7X_VMEM_BYTES = 64 * 1024 * 1024
VMEM_LIMIT_BYTES = V7X_VMEM_BYTES - 8 * 1024 * 1024
LANES = 128
SUBLANES = 8
TOKEN_TILE = 512


def _cparams(n_axes=1):
    return pltpu.CompilerParams(dimension_semantics=("arbitrary",) * n_axes, vmem_limit_bytes=VMEM_LIMIT_BYTES)


def _resident(shape):
    zeros = (0,) * len(shape)
    return pl.BlockSpec(shape, lambda *_: zeros, pipeline_mode=pl.Buffered(1))


class _Rows(NamedTuple):
    arr: jax.Array
    off: int
    n: int


def _rows_spec(w):
    assert w.off % w.n == 0
    block = w.off // w.n
    return pl.BlockSpec((N_DEV, w.n, w.arr.shape[2]), lambda *_: (0, block, 0), pipeline_mode=pl.Buffered(1))


def _mat(ref):
    v = ref[...]
    return v.reshape(v.shape[0] * v.shape[1], v.shape[2])


def _token_tile(seq):
    return min(TOKEN_TILE, seq // 2)


def _largest_divisor(m, cap, mult):
    best = None
    for d in range(mult, min(m, cap) + 1, mult):
        if m % d == 0:
            best = d
    return m if best is None else best


def _nt(a, b):
    return lax.dot_general(a, b, (((1,), (1,)), ((), ())), preferred_element_type=F32)


def _nn(a, b):
    return lax.dot_general(a, b, (((1,), (0,)), ((), ())), preferred_element_type=F32)


def _tn(a, b):
    return lax.dot_general(a, b, (((0,), (0,)), ((), ())), preferred_element_type=F32)


def _rms_parts(xv):
    r = lax.rsqrt(jnp.mean(xv * xv, axis=-1, keepdims=True) + EPS)
    return r, xv * r


def _rms_backward(dh, xh, r, gain, dres):
    u = dh * gain
    return dres + r * (u - xh * jnp.mean(u * xh, axis=-1, keepdims=True))


def _causal_conv3(ext_ref, xv, w_ref, n):
    ext_ref[8:8 + n, :] = xv
    return w_ref[2:3, :] * xv + w_ref[1:2, :] * ext_ref[7:7 + n, :] + w_ref[0:1, :] * ext_ref[6:6 + n, :]


def _anticausal_conv3(ext_ref, xv, w_ref, n):
    ext_ref[0:n, :] = xv
    sh1 = ext_ref[1:1 + n, :]
    sh2 = ext_ref[2:2 + n, :]
    return w_ref[2:3, :] * xv + w_ref[1:2, :] * sh1 + w_ref[0:1, :] * sh2, sh1, sh2


def _sigmoid(z):
    return 1.0 / (1.0 + jnp.exp(-z))


def _fwd_conv_mixer(x, gain, w_in_t, w_conv, w_out, seq, name):
    t_all, d = x.shape
    tt = _token_tile(seq)
    tps = seq // tt

    def body(x_ref, g_ref, win_ref, wc_ref, wout_ref, x1_ref, bcv_ref, ext_ref):
        i = pl.program_id(0)
        xv = x_ref[...]
        r, xh = _rms_parts(xv)
        h = (xh * g_ref[...]).astype(BF16)
        bcv = _nt(h, _mat(win_ref))
        bcv_ref[...] = bcv.astype(BF16)

        @pl.when(i % tps == 0)
        def _():
            ext_ref[0:8, :] = jnp.zeros((8, d), F32)

        cc = _causal_conv3(ext_ref, bcv[:, d:2 * d] * bcv[:, 2 * d:], wc_ref, tt)
        ext_ref[0:8, :] = ext_ref[tt:tt + 8, :]
        y = (bcv[:, :d] * cc).astype(BF16)
        x1_ref[...] = xv + _nn(y, _mat(wout_ref))

    return pl.pallas_call(
        body, name=name, grid=(t_all // tt,),
        in_specs=[pl.BlockSpec((tt, d), lambda i: (i, 0)), _resident((1, d)), _rows_spec(w_in_t),
                  _resident((3, d)), _rows_spec(w_out)],
        out_specs=[pl.BlockSpec((tt, d), lambda i: (i, 0)), pl.BlockSpec((tt, 3 * d), lambda i: (i, 0))],
        out_shape=[SDS((t_all, d), F32), SDS((t_all, 3 * d), BF16)],
        scratch_shapes=[pltpu.VMEM((tt + 8, d), F32)],
        compiler_params=_cparams(),
    )(x, gain, w_in_t.arr, w_conv, w_out.arr)


def _fwd_ffn(x, gain, w_in_t, w_conv, w_down, seq, name):
    t_all, d = x.shape
    f = w_down.n * N_DEV
    tt = _token_tile(seq) // 2
    tps = seq // tt

    def body(x_ref, g_ref, win_ref, wc_ref, wd_ref, x2_ref, gu_ref, gc_ref, ext_ref):
        i = pl.program_id(0)
        xv = x_ref[...]
        r, xh = _rms_parts(xv)
        h = (xh * g_ref[...]).astype(BF16)
        gu = _nt(h, _mat(win_ref))
        gu_ref[...] = gu.astype(BF16)

        @pl.when(i % tps == 0)
        def _():
            ext_ref[0:8, :] = jnp.zeros((8, f), F32)

        gc = _causal_conv3(ext_ref, gu[:, :f], wc_ref, tt)
        ext_ref[0:8, :] = ext_ref[tt:tt + 8, :]
        gc_ref[...] = gc.astype(BF16)
        a = (gc * _sigmoid(gc) * gu[:, f:]).astype(BF16)
        x2_ref[...] = xv + _nn(a, _mat(wd_ref))

    return pl.pallas_call(
        body, name=name, grid=(t_all // tt,),
        in_specs=[pl.BlockSpec((tt, d), lambda i: (i, 0)), _resident((1, d)), _rows_spec(w_in_t),
                  _resident((3, f)), _rows_spec(w_down)],
        out_specs=[pl.BlockSpec((tt, d), lambda i: (i, 0)), pl.BlockSpec((tt, 2 * f), lambda i: (i, 0)),
                   pl.BlockSpec((tt, f), lambda i: (i, 0))],
        out_shape=[SDS((t_all, d), F32), SDS((t_all, 2 * f), BF16), SDS((t_all, f), BF16)],
        scratch_shapes=[pltpu.VMEM((tt + 8, f), F32)],
        compiler_params=_cparams(),
    )(x, gain, w_in_t.arr, w_conv, w_down.arr)


def _rope_partner(xs, lane_lo):
    return jnp.where(lane_lo, pltpu.roll(xs, LANES - HEAD_DIM // 2, 1), pltpu.roll(xs, HEAD_DIM // 2, 1))


def _fwd_qkv(x, gain, w_qkv_t, b_qkv, cos_t, sin_t, seq, name):
    t_all, d = x.shape
    width = w_qkv_t.n * N_DEV
    kvw = (width - d) // 2
    tt = _token_tile(seq)
    tps = seq // tt
    scale = HEAD_DIM ** -0.5

    def body(x_ref, g_ref, w_ref, b_ref, cos_ref, sin_ref, qkv_ref):
        xv = x_ref[...]
        r, xh = _rms_parts(xv)
        h = (xh * g_ref[...]).astype(BF16)
        qkv = _nt(h, _mat(w_ref)) + b_ref[...]
        cosv = cos_ref[...]
        sinv = sin_ref[...]
        lane_lo = (lax.broadcasted_iota(jnp.int32, (tt, LANES), 1) % HEAD_DIM) < HEAD_DIM // 2
        for s in range((d + kvw) // LANES):
            xs = qkv[:, s * LANES:(s + 1) * LANES]
            roped = xs * cosv + _rope_partner(xs, lane_lo) * sinv
            if s * LANES < d:
                roped = roped * scale
            qkv_ref[:, s * LANES:(s + 1) * LANES] = roped.astype(BF16)
        qkv_ref[:, d + kvw:] = qkv[:, d + kvw:].astype(BF16)

    return pl.pallas_call(
        body, name=name, grid=(t_all // tt,),
        in_specs=[pl.BlockSpec((tt, d), lambda i: (i, 0)), _resident((1, d)), _rows_spec(w_qkv_t),
                  _resident((1, width)), pl.BlockSpec((tt, LANES), lambda i: (i % tps, 0)),
                  pl.BlockSpec((tt, LANES), lambda i: (i % tps, 0))],
        out_specs=pl.BlockSpec((tt, width), lambda i: (i, 0)),
        out_shape=SDS((t_all, width), BF16),
        compiler_params=_cparams(),
    )(x, gain, w_qkv_t.arr, b_qkv, cos_t, sin_t)


def _band_masks():
    rows = GROUP * WINDOW
    r = lax.broadcasted_iota(jnp.int32, (rows, 2 * WINDOW), 0) % WINDOW
    j = lax.broadcasted_iota(jnp.int32, (rows, 2 * WINDOW), 1)
    base = (j > r) & (j <= r + WINDOW)
    return base, base & (j >= WINDOW)


def _sink_column(sink_ref, kh):
    rows = GROUP * WINDOW
    g_of_row = lax.broadcasted_iota(jnp.int32, (rows, 1), 0) // WINDOW
    col = jnp.zeros((rows, 1), F32)
    for g in range(GROUP):
        col = jnp.where(g_of_row == g, sink_ref[kh * GROUP + g], col)
    return col


def _stack_heads(ref, row0, kh):
    return jnp.concatenate(
        [ref[row0:row0 + WINDOW, (kh * GROUP + g) * HEAD_DIM:(kh * GROUP + g + 1) * HEAD_DIM] for g in range(GROUP)],
        axis=0)


def _softmax_with_sink(s, valid, sink_col):
    s = jnp.where(valid, s, jnp.finfo(F32).min)
    m = jnp.maximum(jnp.max(s, axis=-1, keepdims=True), sink_col)
    p = jnp.exp(s - m)
    e_sink = jnp.exp(sink_col - m)
    inv = 1.0 / (jnp.sum(p, axis=-1, keepdims=True) + e_sink)
    return p * inv, e_sink * inv


def _fwd_attention(qkv, x, sinks, w_o, b_o, seq, name):
    t_all, d = x.shape
    width = qkv.shape[1]
    kvw = (width - d) // 2
    n_kv = kvw // HEAD_DIM
    tt = _token_tile(seq)
    tps = seq // tt
    nblk = tt // WINDOW

    def body(sink_ref, qkv_ref, kvp_ref, x_ref, wo_ref, bo_ref, x1_ref, o_ref, kvext_ref, oscr_ref):
        i = pl.program_id(0)
        kvext_ref[0:WINDOW, :] = kvp_ref[...]
        kvext_ref[WINDOW:, :] = qkv_ref[:, d:]
        base, base_first = _band_masks()
        not_first = jnp.full(base.shape, i % tps != 0)
        for n in range(nblk):
            valid = (base_first | (base & not_first)) if n == 0 else base
            for kh in range(n_kv):
                qs = _stack_heads(qkv_ref, n * WINDOW, kh)
                kb = kvext_ref[n * WINDOW:(n + 2) * WINDOW, kh * HEAD_DIM:(kh + 1) * HEAD_DIM]
                vb = kvext_ref[n * WINDOW:(n + 2) * WINDOW, kvw + kh * HEAD_DIM:kvw + (kh + 1) * HEAD_DIM]
                probs, _ = _softmax_with_sink(_nt(qs, kb), valid, _sink_column(sink_ref, kh))
                o_s = _nn(probs.astype(BF16), vb)
                for g in range(GROUP):
                    hd = kh * GROUP + g
                    oscr_ref[n * WINDOW:(n + 1) * WINDOW, hd * HEAD_DIM:(hd + 1) * HEAD_DIM] = (
                        o_s[g * WINDOW:(g + 1) * WINDOW].astype(BF16))
        o = oscr_ref[...]
        o_ref[...] = o
        x1_ref[...] = x_ref[...] + _nn(o, _mat(wo_ref)) + bo_ref[...]

    kv_blocks = tt // WINDOW
    return pl.pallas_call(
        body, name=name, grid=(t_all // tt,),
        in_specs=[pl.BlockSpec(memory_space=pltpu.SMEM),
                  pl.BlockSpec((tt, width), lambda i: (i, 0)),
                  pl.BlockSpec((WINDOW, 2 * kvw), lambda i: (jnp.maximum(i * kv_blocks - 1, 0), d // (2 * kvw))),
                  pl.BlockSpec((tt, d), lambda i: (i, 0)), _rows_spec(w_o), _resident((1, d))],
        out_specs=[pl.BlockSpec((tt, d), lambda i: (i, 0)), pl.BlockSpec((tt, d), lambda i: (i, 0))],
        out_shape=[SDS((t_all, d), F32), SDS((t_all, d), BF16)],
        scratch_shapes=[pltpu.VMEM((tt + WINDOW, 2 * kvw), BF16), pltpu.VMEM((tt, d), BF16)],
        compiler_params=_cparams(),
    )(sinks, qkv, qkv, x, w_o.arr, b_o)


def _final_norm_loss(x, gain, target, name):
    t_all, d = x.shape
    tt = min(TOKEN_TILE, t_all)

    def body(x_ref, g_ref, t_ref, dx_ref, dg_ref, loss_ref):
        i = pl.program_id(0)
        xv = x_ref[...]
        r, xh = _rms_parts(xv)
        gain_v = g_ref[...]
        e = xh * gain_v - t_ref[...]
        dy = e * (1.0 / d)
        dx_ref[...] = _rms_backward(dy, xh, r, gain_v, 0.0)

        @pl.when(i == 0)
        def _():
            dg_ref[...] = jnp.zeros_like(dg_ref)
            loss_ref[...] = jnp.zeros_like(loss_ref)

        dg_ref[...] += jnp.sum(dy * xh, axis=0, keepdims=True)
        loss_ref[...] += (0.5 / d) * jnp.sum(e * e, axis=0, keepdims=True)

    return pl.pallas_call(
        body, name=name, grid=(t_all // tt,),
        in_specs=[pl.BlockSpec((tt, d), lambda i: (i, 0)), _resident((1, d)), pl.BlockSpec((tt, d), lambda i: (i, 0))],
        out_specs=[pl.BlockSpec((tt, d), lambda i: (i, 0)), pl.BlockSpec((1, d), lambda i: (0, 0)),
                   pl.BlockSpec((1, d), lambda i: (0, 0))],
        out_shape=[SDS((t_all, d), F32), SDS((1, d), F32), SDS((1, d), F32)],
        compiler_params=_cparams(),
    )(x, gain, target)


def _bwd_ffn_inner(dx2, gu, gc, w_conv, w_down, seq, name):
    t_all, d = dx2.shape
    f = w_down.n * N_DEV
    tt = _token_tile(seq) // 2
    tps = seq // tt
    nt = t_all // tt

    def body(dx_ref, gu_ref, gc_ref, wc_ref, wd_ref, dgu_ref, a_ref, dwc_ref, aext_ref):
        i = pl.program_id(0)
        ti = nt - 1 - i
        da = _nt(dx_ref[...].astype(BF16), _mat(wd_ref))
        gu_v = gu_ref[...].astype(F32)
        g = gu_v[:, :f]
        u = gu_v[:, f:]
        gc = gc_ref[...].astype(F32)
        sig = _sigmoid(gc)
        s = gc * sig
        a_ref[...] = (s * u).astype(BF16)
        dgc = da * u * (sig * (1.0 + gc * (1.0 - sig)))

        @pl.when(ti % tps == tps - 1)
        def _():
            aext_ref[tt:tt + 8, :] = jnp.zeros((8, f), F32)

        dg, sh1, sh2 = _anticausal_conv3(aext_ref, dgc, wc_ref, tt)
        aext_ref[tt:tt + 8, :] = aext_ref[0:8, :]
        dgu_ref[:, :f] = dg.astype(BF16)
        dgu_ref[:, f:] = (da * s).astype(BF16)

        @pl.when(i == 0)
        def _():
            dwc_ref[...] = jnp.zeros_like(dwc_ref)

        dwc_ref[0:1, :] += jnp.sum(g * sh2, axis=0, keepdims=True)
        dwc_ref[1:2, :] += jnp.sum(g * sh1, axis=0, keepdims=True)
        dwc_ref[2:3, :] += jnp.sum(g * dgc, axis=0, keepdims=True)

    return pl.pallas_call(
        body, name=name, grid=(nt,),
        in_specs=[pl.BlockSpec((tt, d), lambda i: (nt - 1 - i, 0)),
                  pl.BlockSpec((tt, 2 * f), lambda i: (nt - 1 - i, 0)),
                  pl.BlockSpec((tt, f), lambda i: (nt - 1 - i, 0)),
                  _resident((3, f)), _rows_spec(w_down)],
        out_specs=[pl.BlockSpec((tt, 2 * f), lambda i: (nt - 1 - i, 0)), pl.BlockSpec((tt, f), lambda i: (nt - 1 - i, 0)),
                   pl.BlockSpec((8, f), lambda i: (0, 0))],
        out_shape=[SDS((t_all, 2 * f), BF16), SDS((t_all, f), BF16), SDS((8, f), F32)],
        scratch_shapes=[pltpu.VMEM((tt + 8, f), F32)],
        compiler_params=_cparams(),
    )(dx2, gu, gc, w_conv, w_down.arr)


def _bwd_conv_inner(dx1, bcv, w_conv, w_out, seq, name):
    t_all, d = dx1.shape
    tt = _token_tile(seq)
    tps = seq // tt
    nt = t_all // tt
    halo = 16

    def body(dx_ref, bcv_ref, prev_ref, wc_ref, wout_ref, dbcv_ref, y_ref, dwc_ref, cext_ref, aext_ref):
        i = pl.program_id(0)
        ti = nt - 1 - i
        dy = _nt(dx_ref[...].astype(BF16), _mat(wout_ref))
        bcv_v = bcv_ref[...].astype(F32)
        b = bcv_v[:, :d]
        c = bcv_v[:, d:2 * d]
        v = bcv_v[:, 2 * d:]
        cv = c * v
        prev = prev_ref[...].astype(F32)[halo - 8:, :]
        cext_ref[0:8, :] = jnp.where(ti % tps == 0, 0.0, prev[:, d:2 * d] * prev[:, 2 * d:])
        cc = _causal_conv3(cext_ref, cv, wc_ref, tt)
        y_ref[...] = (b * cc).astype(BF16)
        dcc = dy * b

        @pl.when(ti % tps == tps - 1)
        def _():
            aext_ref[tt:tt + 8, :] = jnp.zeros((8, d), F32)

        dcv, sh1, sh2 = _anticausal_conv3(aext_ref, dcc, wc_ref, tt)
        aext_ref[tt:tt + 8, :] = aext_ref[0:8, :]
        dbcv_ref[:, :d] = (dy * cc).astype(BF16)
        dbcv_ref[:, d:2 * d] = (dcv * v).astype(BF16)
        dbcv_ref[:, 2 * d:] = (dcv * c).astype(BF16)

        @pl.when(i == 0)
        def _():
            dwc_ref[...] = jnp.zeros_like(dwc_ref)

        dwc_ref[0:1, :] += jnp.sum(cv * sh2, axis=0, keepdims=True)
        dwc_ref[1:2, :] += jnp.sum(cv * sh1, axis=0, keepdims=True)
        dwc_ref[2:3, :] += jnp.sum(cv * dcc, axis=0, keepdims=True)

    return pl.pallas_call(
        body, name=name, grid=(nt,),
        in_specs=[pl.BlockSpec((tt, d), lambda i: (nt - 1 - i, 0)),
                  pl.BlockSpec((tt, 3 * d), lambda i: (nt - 1 - i, 0)),
                  pl.BlockSpec((halo, 3 * d), lambda i: (jnp.maximum((nt - 1 - i) * (tt // halo) - 1, 0), 0)),
                  _resident((3, d)), _rows_spec(w_out)],
        out_specs=[pl.BlockSpec((tt, 3 * d), lambda i: (nt - 1 - i, 0)), pl.BlockSpec((tt, d), lambda i: (nt - 1 - i, 0)),
                   pl.BlockSpec((8, d), lambda i: (0, 0))],
        out_shape=[SDS((t_all, 3 * d), BF16), SDS((t_all, d), BF16), SDS((8, d), F32)],
        scratch_shapes=[pltpu.VMEM((tt + 8, d), F32), pltpu.VMEM((tt + 8, d), F32)],
        compiler_params=_cparams(),
    )(dx1, bcv, bcv, w_conv, w_out.arr)


def _bwd_attention_inner(dx1, qkv, sinks, w_o, cos_t, sin_t, seq, name):
    t_all, d = dx1.shape
    width = qkv.shape[1]
    kvw = (width - d) // 2
    n_kv = kvw // HEAD_DIM
    tt = _token_tile(seq)
    tps = seq // tt
    nt = t_all // tt
    nblk = tt // WINDOW
    scale = HEAD_DIM ** -0.5

    def body(sink_ref, dx_ref, qkv_ref, kvp_ref, cos_ref, sin_ref, wo_ref,
             dqkv_ref, dsink_ref, dbqkv_ref, dbo_ref,
             kvext_ref, dkvext_ref, carry_ref, dq_ref, do_ref):
        i = pl.program_id(0)
        ti = nt - 1 - i
        dxv = dx_ref[...]
        do_ref[...] = _nt(dxv.astype(BF16), _mat(wo_ref)).astype(BF16)
        kvext_ref[0:WINDOW, :] = kvp_ref[...]
        kvext_ref[WINDOW:, :] = qkv_ref[:, d:]
        dkvext_ref[...] = jnp.zeros_like(dkvext_ref)

        @pl.when(i == 0)
        def _():
            carry_ref[...] = jnp.zeros_like(carry_ref)
            dsink_ref[...] = jnp.zeros_like(dsink_ref)
            dbqkv_ref[...] = jnp.zeros_like(dbqkv_ref)
            dbo_ref[...] = jnp.zeros_like(dbo_ref)

        base, base_first = _band_masks()
        not_first = jnp.full(base.shape, ti % tps != 0)
        head_lane = lax.broadcasted_iota(jnp.int32, (1, LANES), 1)
        dsink = jnp.zeros((1, LANES), F32)
        for n in range(nblk):
            valid = (base_first | (base & not_first)) if n == 0 else base
            for kh in range(n_kv):
                qs = _stack_heads(qkv_ref, n * WINDOW, kh)
                dos = _stack_heads(do_ref, n * WINDOW, kh)
                kcols = slice(kh * HEAD_DIM, (kh + 1) * HEAD_DIM)
                vcols = slice(kvw + kh * HEAD_DIM, kvw + (kh + 1) * HEAD_DIM)
                band = slice(n * WINDOW, (n + 2) * WINDOW)
                kb = kvext_ref[band, kcols]
                vb = kvext_ref[band, vcols]
                probs, p_sink = _softmax_with_sink(_nt(qs, kb), valid, _sink_column(sink_ref, kh))
                dp = _nt(dos, vb)
                dsum = jnp.sum(probs * dp, axis=-1, keepdims=True)
                ds = (probs * (dp - dsum)).astype(BF16)
                dkvext_ref[band, vcols] += _tn(probs.astype(BF16), dos)
                dkvext_ref[band, kcols] += _tn(ds, qs)
                dq_s = _nn(ds, kb)
                sink_terms = p_sink * dsum
                for g in range(GROUP):
                    hd = kh * GROUP + g
                    dq_ref[n * WINDOW:(n + 1) * WINDOW, hd * HEAD_DIM:(hd + 1) * HEAD_DIM] = dq_s[g * WINDOW:(g + 1) * WINDOW]
                    dsink = dsink - jnp.where(head_lane == hd, jnp.sum(sink_terms[g * WINDOW:(g + 1) * WINDOW]), 0.0)
        dsink_ref[0:1, :] += dsink
        dkvext_ref[tt:tt + WINDOW, :] += carry_ref[...]
        carry_ref[...] = dkvext_ref[0:WINDOW, :]

        cosv = cos_ref[...]
        sinv = sin_ref[...]
        lane_lo = (lax.broadcasted_iota(jnp.int32, (tt, LANES), 1) % HEAD_DIM) < HEAD_DIM // 2
        for s in range((d + kvw) // LANES):
            if s * LANES < d:
                dy = dq_ref[:, s * LANES:(s + 1) * LANES] * scale
            else:
                dy = dkvext_ref[WINDOW:, s * LANES - d:(s + 1) * LANES - d]
            dpre = dy * cosv - _rope_partner(dy, lane_lo) * sinv
            dqkv_ref[:, s * LANES:(s + 1) * LANES] = dpre.astype(BF16)
            dbqkv_ref[0:1, s * LANES:(s + 1) * LANES] += jnp.sum(dpre, axis=0, keepdims=True)
        dv = dkvext_ref[WINDOW:, kvw:]
        dqkv_ref[:, d + kvw:] = dv.astype(BF16)
        dbqkv_ref[0:1, d + kvw:] += jnp.sum(dv, axis=0, keepdims=True)
        dbo_ref[...] += jnp.sum(dxv, axis=0, keepdims=True)

    kv_blocks = tt // WINDOW
    return pl.pallas_call(
        body, name=name, grid=(nt,),
        in_specs=[pl.BlockSpec(memory_space=pltpu.SMEM),
                  pl.BlockSpec((tt, d), lambda i: (nt - 1 - i, 0)),
                  pl.BlockSpec((tt, width), lambda i: (nt - 1 - i, 0)),
                  pl.BlockSpec((WINDOW, 2 * kvw), lambda i: (jnp.maximum((nt - 1 - i) * kv_blocks - 1, 0), d // (2 * kvw))),
                  pl.BlockSpec((tt, LANES), lambda i: ((nt - 1 - i) % tps, 0)),
                  pl.BlockSpec((tt, LANES), lambda i: ((nt - 1 - i) % tps, 0)),
                  _rows_spec(w_o)],
        out_specs=[pl.BlockSpec((tt, width), lambda i: (nt - 1 - i, 0)), pl.BlockSpec((8, LANES), lambda i: (0, 0)),
                   pl.BlockSpec((1, width), lambda i: (0, 0)), pl.BlockSpec((1, d), lambda i: (0, 0))],
        out_shape=[SDS((t_all, width), BF16), SDS((8, LANES), F32), SDS((1, width), F32), SDS((1, d), F32)],
        scratch_shapes=[pltpu.VMEM((tt + WINDOW, 2 * kvw), BF16), pltpu.VMEM((tt + WINDOW, 2 * kvw), F32),
                        pltpu.VMEM((WINDOW, 2 * kvw), F32), pltpu.VMEM((tt, d), F32), pltpu.VMEM((tt, d), BF16)],
        compiler_params=_cparams(),
    )(sinks, dx1, qkv, qkv, cos_t, sin_t, w_o.arr)


def _bwd_dense_norm(dy, w_t, x, gain, dres, name):
    t_all, d = x.shape
    n = dy.shape[1]
    tt = min(TOKEN_TILE, t_all)

    def body(dy_ref, w_ref, x_ref, g_ref, dres_ref, dx_ref, h_ref, dg_ref):
        i = pl.program_id(0)
        dh = _nn(dy_ref[...], _mat(w_ref))
        r, xh = _rms_parts(x_ref[...])
        gain_v = g_ref[...]
        h_ref[...] = (xh * gain_v).astype(BF16)
        dx_ref[...] = _rms_backward(dh, xh, r, gain_v, dres_ref[...])

        @pl.when(i == 0)
        def _():
            dg_ref[...] = jnp.zeros_like(dg_ref)

        dg_ref[...] += jnp.sum(dh * xh, axis=0, keepdims=True)

    return pl.pallas_call(
        body, name=name, grid=(t_all // tt,),
        in_specs=[pl.BlockSpec((tt, n), lambda i: (i, 0)), _rows_spec(w_t), pl.BlockSpec((tt, d), lambda i: (i, 0)),
                  _resident((1, d)), pl.BlockSpec((tt, d), lambda i: (i, 0))],
        out_specs=[pl.BlockSpec((tt, d), lambda i: (i, 0)), pl.BlockSpec((tt, d), lambda i: (i, 0)),
                   pl.BlockSpec((1, d), lambda i: (0, 0))],
        out_shape=[SDS((t_all, d), F32), SDS((t_all, d), BF16), SDS((1, d), F32)],
        compiler_params=_cparams(),
    )(dy, w_t.arr, x, gain, dres)


def _tn_matmul(a, b, dest, name):
    t_all, m = a.shape
    d = b.shape[1]
    n = dest.n
    assert m == N_DEV * n and dest.off % n == 0
    k = max(kk for kk in (1, 2, 4, 8) if kk * n <= max(n, 1536))
    tm = k * n
    tt = min(TOKEN_TILE, t_all)
    n_t = t_all // tt
    fresh = not hasattr(dest.arr, "dtype")

    def body(a_ref, b_ref, *rest):
        o_ref, acc_ref = rest[-2:]
        t = pl.program_id(1)

        @pl.when(t == 0)
        def _():
            acc_ref[...] = jnp.zeros_like(acc_ref)

        acc_ref[...] += _tn(a_ref[...], b_ref[...].astype(BF16))

        @pl.when(t == n_t - 1)
        def _():
            o_ref[...] = acc_ref[...].astype(BF16).reshape(k, n, d)

    block = dest.off // n
    return pl.pallas_call(
        body, name=name, grid=(m // tm, n_t),
        in_specs=[pl.BlockSpec((tt, tm), lambda j, t: (t, j)), pl.BlockSpec((tt, d), lambda j, t: (t, 0))] + ([] if fresh else [ANY]),
        out_specs=pl.BlockSpec((k, n, d), lambda j, t: (j, block, 0)),
        out_shape=SDS(tuple(dest.arr) if fresh else dest.arr.shape, BF16),
        scratch_shapes=[pltpu.VMEM((tm, d), F32)],
        input_output_aliases={} if fresh else {2: 0},
        compiler_params=_cparams(2),
    )(*((a, b) if fresh else (a, b, dest.arr)))


def _my_place():
    return lax.axis_index("x"), lax.axis_index("y"), lax.axis_index("c")


def _other_chips(x, y):
    return [(1 - x, y), (x, 1 - y), (1 - x, 1 - y)]


def _all_gather(blocks, name):
    n_arr = len(blocks)

    def body(*refs):
        in_refs = refs[:n_arr]
        out_refs = refs[n_arr:2 * n_arr]
        send_sems, recv_sems, local_sems = refs[2 * n_arr:]
        x, y, c = _my_place()
        me, sibling = (x, y, c), (x, y, 1 - c)
        chips = _other_chips(x, y)

        def slot(a, place):
            px, py, pc = place
            return out_refs[a].at[4 * px + 2 * py + pc]

        def copy(a, k, block, to, src=None):
            return pltpu.make_async_remote_copy(
                src_ref=slot(a, block) if src is None else src, dst_ref=slot(a, block),
                send_sem=send_sems.at[a, k], recv_sem=recv_sems.at[a, k], device_id=to, device_id_type=MESH)

        started = []
        local = []
        for a in range(n_arr):
            mine = pltpu.make_async_copy(in_refs[a], slot(a, me), local_sems.at[a])
            mine.start()
            local.append(mine)
            first = [copy(a, 0, me, sibling, src=in_refs[a])]
            first += [copy(a, 1 + j, me, (*chip, c), src=in_refs[a]) for j, chip in enumerate(chips)]
            for cp in first:
                cp.start()
            started += first
        for a in range(n_arr):
            for j, chip in enumerate(chips):
                copy(a, 1 + j, (*chip, c), me).wait_recv()
                passed = copy(a, 4 + j, (*chip, c), sibling)
                passed.start()
                started.append(passed)
        for a in range(n_arr):
            copy(a, 0, sibling, me).wait_recv()
            for j, chip in enumerate(chips):
                copy(a, 4 + j, (*chip, 1 - c), me).wait_recv()
        for cp in started:
            cp.wait_send()
        for mine in local:
            mine.wait()

    return pl.pallas_call(
        body, name=name,
        in_specs=[ANY] * n_arr, out_specs=[ANY] * n_arr,
        out_shape=[SDS((N_DEV,) + b.shape, b.dtype) for b in blocks],
        scratch_shapes=[pltpu.SemaphoreType.DMA((n_arr, 7)), pltpu.SemaphoreType.DMA((n_arr, 7)),
                        pltpu.SemaphoreType.DMA((n_arr,))],
    )(*blocks)


def _peer_of(k, x, y, c):
    return x ^ ((k >> 2) & 1), y ^ ((k >> 1) & 1), c ^ (k & 1)


HBM = pl.BlockSpec(memory_space=pltpu.HBM)
SEM = pl.BlockSpec(memory_space=pltpu.SEMAPHORE)
DATAFLOW_EFFECT = pltpu.SideEffectType.DATAFLOW_SIDE_EFFECTING


def _peer_copies(src_ref, land_ref, send_sems, recv_sems, per_peer):
    x, y, c = _my_place()
    me = 4 * x + 2 * y + c
    copies = []
    for k in range(1, N_DEV):
        px, py, pc = _peer_of(k, x, y, c)
        peer = 4 * px + 2 * py + pc
        copies.append(pltpu.make_async_remote_copy(
            src_ref=src_ref.at[peer] if per_peer else src_ref, dst_ref=land_ref.at[me],
            send_sem=send_sems.at[k - 1], recv_sem=recv_sems.at[k - 1], device_id=(px, py, pc), device_id_type=MESH))
    own = pltpu.make_async_copy(src_ref.at[me] if per_peer else src_ref, land_ref.at[me], send_sems.at[N_DEV - 1])
    return copies, own


def _exchange_start(src, after, per_peer, name):
    rows, d = src.shape[-2:]

    def body(src_ref, land_ref, after_ref, send_sems, recv_sems, src_thru, land_thru, token):
        copies, own = _peer_copies(src_ref, land_ref, send_sems, recv_sems, per_peer)
        for cp in copies:
            cp.start()
        own.start()
        token[...] = jnp.zeros_like(token)

    return pl.pallas_call(
        body, name=name,
        out_shape=(pltpu.SemaphoreType.DMA((N_DEV,)), pltpu.SemaphoreType.DMA((N_DEV - 1,)), pltpu.HBM(src.shape, src.dtype),
                   pltpu.HBM((N_DEV, rows, d), src.dtype), SDS((SUBLANES, LANES), F32)),
        in_specs=(HBM, HBM, ANY), out_specs=(SEM, SEM, HBM, HBM, pl.BlockSpec(memory_space=pltpu.VMEM)),
        input_output_aliases={0: 2, 1: 3},
        compiler_params=pltpu.CompilerParams(has_side_effects=DATAFLOW_EFFECT),
    )(pltpu.with_memory_space_constraint(src, pltpu.HBM),
      pltpu.with_memory_space_constraint(lax.empty((N_DEV, rows, d), src.dtype), pltpu.HBM), after)


def _exchange_wait(started, after, per_peer, name):
    send_sems, recv_sems, src_thru, land_thru, _ = started

    def body(src_ref, land_ref, send_sems, recv_sems, after_ref, src_out, land_out):
        copies, own = _peer_copies(src_ref, land_ref, send_sems, recv_sems, per_peer)
        for cp in copies:
            cp.wait_send()
            cp.wait_recv()
        own.wait()

    return pl.pallas_call(
        body, name=name,
        out_shape=(pltpu.HBM(src_thru.shape, src_thru.dtype), pltpu.HBM(land_thru.shape, land_thru.dtype)),
        in_specs=(HBM, HBM, SEM, SEM, ANY), out_specs=(HBM, HBM), input_output_aliases={0: 0, 1: 1},
        compiler_params=pltpu.CompilerParams(has_side_effects=DATAFLOW_EFFECT),
    )(src_thru, land_thru, send_sems, recv_sems, after)


def _sum_slots(slots, name):
    _, rows, d = slots.shape
    tr = _largest_divisor(rows, 512, 16)

    def body(s_ref, o_ref):
        acc = s_ref[0].astype(F32)
        for dev in range(1, N_DEV):
            acc = acc + s_ref[dev].astype(F32)
        o_ref[...] = acc

    return pl.pallas_call(
        body, name=name, grid=(rows // tr,),
        in_specs=[pl.BlockSpec((N_DEV, tr, d), lambda r: (0, r, 0))], out_specs=pl.BlockSpec((tr, d), lambda r: (r, 0)),
        out_shape=SDS((rows, d), F32), compiler_params=_cparams(),
    )(slots)


def _all_reduce_small(part, loss_rows, name):
    rows, lanes = part.shape
    lo, hi = loss_rows

    def body(x_ref, out_ref, loss_ref, gath_ref, send_sems, recv_sems):
        x, y, c = _my_place()
        me = 4 * x + 2 * y + c
        gath_ref[me] = x_ref[...]
        copies = []
        for k in range(1, N_DEV):
            peer = (x ^ ((k >> 2) & 1), y ^ ((k >> 1) & 1), c ^ (k & 1))
            cp = pltpu.make_async_remote_copy(
                src_ref=x_ref, dst_ref=gath_ref.at[me], send_sem=send_sems.at[k - 1], recv_sem=recv_sems.at[k - 1],
                device_id=peer, device_id_type=MESH)
            cp.start()
            copies.append(cp)
        for cp in copies:
            cp.wait_recv()
        for cp in copies:
            cp.wait_send()
        acc = gath_ref[0]
        for dev in range(1, N_DEV):
            acc = acc + gath_ref[dev]
        out_ref[...] = acc
        loss_ref[...] = jnp.full(loss_ref.shape, jnp.sum(acc[lo:hi, :]), F32)

    vmem = pl.BlockSpec(memory_space=pltpu.VMEM)
    return pl.pallas_call(
        body, name=name, in_specs=[vmem], out_specs=[vmem, vmem],
        out_shape=[SDS((rows, lanes), F32), SDS((SUBLANES, LANES), F32)],
        scratch_shapes=[pltpu.VMEM((N_DEV, rows, lanes), F32), pltpu.SemaphoreType.DMA((N_DEV - 1,)),
                        pltpu.SemaphoreType.DMA((N_DEV - 1,))],
    )(part)


def _adamw(w, g, m, v, name):
    rows, cols = w.shape
    tr = rows if rows % SUBLANES else _largest_divisor(rows, 512, SUBLANES)

    def body(w_ref, g_ref, m_ref, v_ref, d_ref, nm_ref, nv_ref):
        gv = g_ref[...]
        nm = ADAM_B1 * m_ref[...] + (1.0 - ADAM_B1) * gv
        nv = ADAM_B2 * v_ref[...] + (1.0 - ADAM_B2) * (gv * gv)
        m_hat = nm / (1.0 - ADAM_B1 ** ADAM_STEP)
        v_hat = nv / (1.0 - ADAM_B2 ** ADAM_STEP)
        d_ref[...] = -ADAM_LR * (m_hat / (jnp.sqrt(v_hat) + ADAM_EPS) + ADAM_WD * w_ref[...])
        nm_ref[...] = nm
        nv_ref[...] = nv

    spec = pl.BlockSpec((tr, cols), lambda i: (i, 0))
    return pl.pallas_call(
        body, name=name, grid=(rows // tr,), in_specs=[spec] * 4, out_specs=[spec] * 3,
        out_shape=[SDS((rows, cols), F32)] * 3, compiler_params=_cparams(),
    )(w, g, m, v)


def _adamw_nd(w, g, m, v, name):
    shape = w.shape
    two_d = (1, shape[0]) if len(shape) == 1 else (-1, shape[-1])
    outs = _adamw(w.reshape(two_d), g.reshape(two_d), m.reshape(two_d), v.reshape(two_d), name)
    return [o.reshape(shape) for o in outs]


def _rope_tables(seq):
    pos = jnp.arange(seq, dtype=F32)
    inv_freq = 1.0 / (ROPE_THETA ** (jnp.arange(0, HEAD_DIM, 2, dtype=F32) / HEAD_DIM))
    ang = pos[:, None] * inv_freq[None, :]
    cos, sin = jnp.cos(ang), jnp.sin(ang)
    reps = LANES // HEAD_DIM
    cos_t = jnp.tile(jnp.concatenate([cos, cos], axis=1), (1, reps))
    sin_t = jnp.tile(jnp.concatenate([-sin, sin], axis=1), (1, reps))
    return cos_t, sin_t


def _flat_pad(a):
    flat = a.reshape(1, -1)
    pad = (-flat.shape[1]) % LANES
    return jnp.pad(flat, ((0, 0), (0, pad))) if pad else flat


def kernel(x, norm_mix, norm_ffn, norm_final, conv_w_in, conv_w_conv, conv_w_out, attn_w_qkv, attn_b_qkv, attn_sinks, attn_w_o, attn_b_o, ffn_w_in, ffn_w_conv, ffn_w_down, loss_target, m_norm_mix, m_norm_ffn, m_norm_final, m_conv_w_in, m_conv_w_conv, m_conv_w_out, m_attn_w_qkv, m_attn_b_qkv, m_attn_sinks, m_attn_w_o, m_attn_b_o, m_ffn_w_in, m_ffn_w_conv, m_ffn_w_down, v_norm_mix, v_norm_ffn, v_norm_final, v_conv_w_in, v_conv_w_conv, v_conv_w_out, v_attn_w_qkv, v_attn_b_qkv, v_attn_sinks, v_attn_w_o, v_attn_b_o, v_ffn_w_in, v_ffn_w_conv, v_ffn_w_down):
    b_loc, seq, d = x.shape
    depth = norm_mix.shape[0]
    n_conv, n_attn = conv_w_in.shape[0], attn_w_qkv.shape[0]
    t_all = b_loc * seq
    my_x, my_y, my_c = _my_place()

    me = 4 * my_x + 2 * my_y + my_c

    groups = []
    for i in range(depth):
        j = i // 2
        if i % 2 == 0:
            mix = [("conv_w_in", j, True, conv_w_in[j].T), ("conv_w_out", j, False, conv_w_out[j])]
        else:
            mix = [("attn_w_qkv", j, True, attn_w_qkv[j].T), ("attn_w_o", j, False, attn_w_o[j])]
        groups.append((("mix", i), mix))
        groups.append((("ffn", i), [("ffn_w_in", i, True, ffn_w_in[i].T), ("ffn_w_down", i, False, ffn_w_down[i])]))
    order = [key for key, _ in groups]
    members_of = dict(groups)

    def layout(key):
        offs, o = [], 0
        for _, _, _, shard in members_of[key]:
            n = shard.shape[0]
            o = -(-o // n) * n
            offs.append(o)
            o += n
        return offs, o

    small = jnp.concatenate([_flat_pad(conv_w_conv), _flat_pad(ffn_w_conv), _flat_pad(attn_b_qkv), _flat_pad(attn_b_o)], axis=1)
    (small_g,) = _all_gather([small], "gather_small")

    gather_started = {}

    def start_gather(idx, after):
        if idx >= len(order):
            return 0.0
        key = order[idx]
        offs, total = layout(key)
        pieces, o = [], 0
        for (_, _, _, shard), off in zip(members_of[key], offs):
            if off > o:
                pieces.append(jnp.zeros((off - o, d), shard.dtype))
            pieces.append(shard)
            o = off + shard.shape[0]
        pack = jnp.concatenate(pieces, axis=0).astype(BF16)
        gather_started[key] = _exchange_start(pack, after, False, f"gather_start_{key[0]}_{key[1]}")
        return gather_started[key][4][0, 0]

    weights = {}

    def finish_gather(key, after):
        _, land = _exchange_wait(gather_started[key], after, False, f"gather_wait_{key[0]}_{key[1]}")
        for (wname, layer, _, shard), off in zip(members_of[key], layout(key)[0]):
            weights[(wname, layer)] = _Rows(land, off, shard.shape[0])

    def take_small(o, shape):
        size = shape[0] * shape[1] * shape[2]
        blk = small_g[:, 0, o:o + size].reshape((N_DEV,) + shape)
        return jnp.moveaxis(blk, 0, 2).reshape(shape[0], shape[1], N_DEV * shape[2])

    so = 0
    wc_conv_full = take_small(so, conv_w_conv.shape); so += _flat_pad(conv_w_conv).shape[1]
    wc_ffn_full = take_small(so, ffn_w_conv.shape); so += _flat_pad(ffn_w_conv).shape[1]
    b_qkv_full = take_small(so, (n_attn, 1, attn_b_qkv.shape[1]))[:, 0]; so += _flat_pad(attn_b_qkv).shape[1]
    b_o_full = take_small(so, (n_attn, 1, attn_b_o.shape[1]))[:, 0]

    cos_t, sin_t = _rope_tables(seq)

    xs = [x.reshape(t_all, d)]
    saved = []
    token = start_gather(0, small_g) + start_gather(1, small_g)
    for i in range(depth):
        j = i // 2
        gain_mix = norm_mix[i][None, :] + token
        finish_gather(("mix", i), gain_mix if i == 0 else xs[-1])
        if i % 2 == 0:
            x1, bcv = _fwd_conv_mixer(xs[-1], gain_mix, weights[("conv_w_in", j)], wc_conv_full[j],
                                      weights[("conv_w_out", j)], seq, f"fwd_conv_{i}")
            mix_saved = (bcv,)
        else:
            qkv = _fwd_qkv(xs[-1], gain_mix, weights[("attn_w_qkv", j)], b_qkv_full[j][None, :], cos_t, sin_t, seq,
                           f"fwd_qkv_{i}")
            x1, o = _fwd_attention(qkv, xs[-1], attn_sinks[j], weights[("attn_w_o", j)], b_o_full[j][None, :], seq,
                                   f"fwd_attn_{i}")
            mix_saved = (qkv, o)
        token = start_gather(2 * i + 2, x1) + start_gather(2 * i + 3, x1)
        gain_ffn = norm_ffn[i][None, :] + token
        finish_gather(("ffn", i), gain_ffn)
        x2, gu, gc = _fwd_ffn(x1, gain_ffn, weights[("ffn_w_in", i)], wc_ffn_full[i], weights[("ffn_w_down", i)],
                              seq, f"fwd_ffn_{i}")
        saved.append((xs[-1], x1, mix_saved, (gu, gc)))
        xs.append(x2)
        token = 0.0

    dx, dg_final, loss_lanes = _final_norm_loss(xs[-1], norm_final[None, :], loss_target.reshape(t_all, d), "loss_head")

    dg_mix, dg_ffn = [None] * depth, [None] * depth
    dwc_conv, dwc_ffn = [None] * n_conv, [None] * depth
    db_qkv, db_o, dsinks = [None] * n_attn, [None] * n_attn, [None] * n_attn
    scatter_started = {}

    def weight_grads(key, operands):
        offs, total = layout(key)
        parts = (N_DEV, total, d)
        for (wname, layer, _, shard), off, (a, b) in zip(members_of[key], offs, operands):
            parts = _tn_matmul(a, b, _Rows(parts, off, shard.shape[0]), f"dw_{wname}_{layer}")
        scatter_started[key] = _exchange_start(parts, operands[0][1], True, f"scatter_start_{key[0]}_{key[1]}")
        return scatter_started[key][4][0, 0]

    token = 0.0
    for i in reversed(range(depth)):
        j = i // 2
        x0, x1, mix_saved, (gu, gc) = saved[i]
        dgu, act, dwc = _bwd_ffn_inner(dx, gu, gc, wc_ffn_full[i] + token, weights[("ffn_w_down", i)], seq, f"bwd_ffn_{i}")
        dwc_ffn[i] = dwc[:3]
        dx1, h2, dg_ffn[i] = _bwd_dense_norm(dgu, weights[("ffn_w_in", i)], x1, norm_ffn[i][None, :], dx, f"bwd_ffn_norm_{i}")
        token = weight_grads(("ffn", i), [(dgu, h2), (act, dx)])
        if i % 2 == 0:
            (bcv,) = mix_saved
            dbcv, y, dwc = _bwd_conv_inner(dx1, bcv, wc_conv_full[j] + token, weights[("conv_w_out", j)], seq, f"bwd_conv_{i}")
            dwc_conv[j] = dwc[:3]
            dx, h, dg_mix[i] = _bwd_dense_norm(dbcv, weights[("conv_w_in", j)], x0, norm_mix[i][None, :], dx1,
                                               f"bwd_conv_norm_{i}")
            token = weight_grads(("mix", i), [(dbcv, h), (y, dx1)])
        else:
            qkv, o = mix_saved
            dqkv, dsk, dbq, dbo = _bwd_attention_inner(dx1, qkv, attn_sinks[j] + token, weights[("attn_w_o", j)], cos_t, sin_t,
                                                       seq, f"bwd_attn_{i}")
            dsinks[j], db_qkv[j], db_o[j] = dsk[0:1, :attn_sinks.shape[1]], dbq, dbo
            dx, h, dg_mix[i] = _bwd_dense_norm(dqkv, weights[("attn_w_qkv", j)], x0, norm_mix[i][None, :], dx1,
                                               f"bwd_attn_norm_{i}")
            token = weight_grads(("mix", i), [(dqkv, h), (o, dx1)])
    grad_x = dx.reshape(b_loc, seq, d)

    reduced = {}
    for key, members in reversed(groups):
        _, land = _exchange_wait(scatter_started[key], dx, True, f"scatter_wait_{key[0]}_{key[1]}")
        total = _sum_slots(land, f"scatter_sum_{key[0]}_{key[1]}")
        for (wname, layer, transposed, shard), off in zip(members, layout(key)[0]):
            rows = total[off:off + shard.shape[0]]
            reduced[(wname, layer)] = rows.T if transposed else rows

    small_parts = [jnp.concatenate(dg_mix, axis=0), jnp.concatenate(dg_ffn, axis=0), dg_final,
                   jnp.stack(dwc_conv), jnp.stack(dwc_ffn), jnp.concatenate(db_qkv, axis=0), jnp.concatenate(db_o, axis=0),
                   jnp.concatenate(dsinks, axis=0), loss_lanes]
    flats = [_flat_pad(p) for p in small_parts]
    bounds = []
    so = 0
    for fl in flats:
        bounds.append((so, so + fl.shape[1]))
        so += fl.shape[1]
    small_rows = so // LANES
    pad_rows = (-small_rows) % SUBLANES
    part_small = jnp.pad(jnp.concatenate(flats, axis=1).reshape(small_rows, LANES), ((0, pad_rows), (0, 0)))
    loss_rows = (bounds[-1][0] // LANES, bounds[-1][1] // LANES)
    summed, loss_tile = _all_reduce_small(part_small, loss_rows, "reduce_small")
    summed = summed.reshape(1, -1)

    def small_grad(k, shape):
        lo = bounds[k][0]
        size = 1
        for s_ in shape:
            size *= s_
        return summed[0, lo:lo + size].reshape(shape)

    def my_cols(full, n_local):
        return lax.dynamic_slice_in_dim(full, me * n_local, n_local, axis=full.ndim - 1)

    g_norm_mix = small_grad(0, norm_mix.shape)
    g_norm_ffn = small_grad(1, norm_ffn.shape)
    g_norm_final = small_grad(2, norm_final.shape)
    g_conv_w_conv = my_cols(small_grad(3, (n_conv, 3, d)), conv_w_conv.shape[2])
    g_ffn_w_conv = my_cols(small_grad(4, (depth, 3, ffn_w_conv.shape[2] * N_DEV)), ffn_w_conv.shape[2])
    g_attn_b_qkv = my_cols(small_grad(5, (n_attn, attn_b_qkv.shape[1] * N_DEV)), attn_b_qkv.shape[1])
    g_attn_b_o = my_cols(small_grad(6, (n_attn, d)), attn_b_o.shape[1])
    g_attn_sinks = small_grad(7, attn_sinks.shape)
    loss = loss_tile[0, 0]

    def big_grad(wname, n_layers):
        return jnp.stack([reduced[(wname, layer)] for layer in range(n_layers)])

    grads = {
        "norm_mix": g_norm_mix, "norm_ffn": g_norm_ffn, "norm_final": g_norm_final,
        "conv_w_in": big_grad("conv_w_in", n_conv), "conv_w_conv": g_conv_w_conv, "conv_w_out": big_grad("conv_w_out", n_conv),
        "attn_w_qkv": big_grad("attn_w_qkv", n_attn), "attn_b_qkv": g_attn_b_qkv, "attn_sinks": g_attn_sinks,
        "attn_w_o": big_grad("attn_w_o", n_attn), "attn_b_o": g_attn_b_o,
        "ffn_w_in": big_grad("ffn_w_in", depth), "ffn_w_conv": g_ffn_w_conv, "ffn_w_down": big_grad("ffn_w_down", depth),
    }
    params = {
        "norm_mix": (norm_mix, m_norm_mix, v_norm_mix), "norm_ffn": (norm_ffn, m_norm_ffn, v_norm_ffn),
        "norm_final": (norm_final, m_norm_final, v_norm_final), "conv_w_in": (conv_w_in, m_conv_w_in, v_conv_w_in),
        "conv_w_conv": (conv_w_conv, m_conv_w_conv, v_conv_w_conv), "conv_w_out": (conv_w_out, m_conv_w_out, v_conv_w_out),
        "attn_w_qkv": (attn_w_qkv, m_attn_w_qkv, v_attn_w_qkv), "attn_b_qkv": (attn_b_qkv, m_attn_b_qkv, v_attn_b_qkv),
        "attn_sinks": (attn_sinks, m_attn_sinks, v_attn_sinks), "attn_w_o": (attn_w_o, m_attn_w_o, v_attn_w_o),
        "attn_b_o": (attn_b_o, m_attn_b_o, v_attn_b_o), "ffn_w_in": (ffn_w_in, m_ffn_w_in, v_ffn_w_in),
        "ffn_w_conv": (ffn_w_conv, m_ffn_w_conv, v_ffn_w_conv), "ffn_w_down": (ffn_w_down, m_ffn_w_down, v_ffn_w_down),
    }
    order = list(params)
    deltas, new_ms, new_vs = [], [], []
    for wname in order:
        w, m, v = params[wname]
        dlt, nm, nv = _adamw_nd(w, grads[wname], m, v, f"adamw_{wname}")
        deltas.append(dlt); new_ms.append(nm); new_vs.append(nv)
    return (loss, grad_x, *[grads[wname] for wname in order], *deltas, *new_ms, *new_vs)
```

```python
from typing import NamedTuple

import jax
import jax.numpy as jnp
from jax import lax
from jax.experimental import pallas as pl
from jax.experimental.pallas import tpu as pltpu

F32 = jnp.float32
BF16 = jnp.bfloat16
SDS = jax.ShapeDtypeStruct
MESH = pl.DeviceIdType.MESH
ANY = pl.BlockSpec(memory_space=pl.ANY)

N_DEV = 8
EPS = 1e-5
HEAD_DIM = 64
GROUP = 4
WINDOW = 128
ROPE_THETA = 10000.0
ADAM_LR, ADAM_B1, ADAM_B2, ADAM_EPS, ADAM_WD, ADAM_STEP = 0.001, 0.9, 0.999, 1e-08, 0.01, 10

V7X_VMEM_BYTES = 64 * 1024 * 1024
VMEM_LIMIT_BYTES = V7X_VMEM_BYTES - 8 * 1024 * 1024
LANES = 128
SUBLANES = 8
TOKEN_TILE = 512
ROW_CHUNK = 32
MASKED_SCORE = -1e30


def _cparams(n_axes=1):
    return pltpu.CompilerParams(dimension_semantics=("arbitrary",) * n_axes, vmem_limit_bytes=VMEM_LIMIT_BYTES)


def _resident(shape):
    zeros = (0,) * len(shape)
    return pl.BlockSpec(shape, lambda *_: zeros, pipeline_mode=pl.Buffered(1))


class _Rows(NamedTuple):
    arr: jax.Array
    off: int
    n: int


def _rows_spec(w):
    assert w.off % w.n == 0
    block = w.off // w.n
    return pl.BlockSpec((N_DEV, w.n, w.arr.shape[2]), lambda *_: (0, block, 0), pipeline_mode=pl.Buffered(1))


def _mat(ref):
    v = ref[...]
    return v.reshape(v.shape[0] * v.shape[1], v.shape[2])


def _token_tile(seq):
    return min(TOKEN_TILE, seq // 2)


def _largest_divisor(m, cap, mult):
    best = None
    for d in range(mult, min(m, cap) + 1, mult):
        if m % d == 0:
            best = d
    return m if best is None else best


def _nt(a, b):
    return lax.dot_general(a, b, (((1,), (1,)), ((), ())), preferred_element_type=F32)


def _nn(a, b):
    return lax.dot_general(a, b, (((1,), (0,)), ((), ())), preferred_element_type=F32)


def _tn(a, b):
    return lax.dot_general(a, b, (((0,), (0,)), ((), ())), preferred_element_type=F32)


def _rms_parts(xv):
    r = lax.rsqrt(jnp.mean(xv * xv, axis=-1, keepdims=True) + EPS)
    return r, xv * r


def _rms_backward(dh, xh, r, gain, dres):
    u = dh * gain
    return dres + r * (u - xh * jnp.mean(u * xh, axis=-1, keepdims=True))


def _causal_conv3(ext_ref, xv, w_ref, n):
    ext_ref[8:8 + n, :] = xv
    return w_ref[2:3, :] * xv + w_ref[1:2, :] * ext_ref[7:7 + n, :] + w_ref[0:1, :] * ext_ref[6:6 + n, :]


def _anticausal_conv3(ext_ref, xv, w_ref, n):
    ext_ref[0:n, :] = xv
    sh1 = ext_ref[1:1 + n, :]
    sh2 = ext_ref[2:2 + n, :]
    return w_ref[2:3, :] * xv + w_ref[1:2, :] * sh1 + w_ref[0:1, :] * sh2, sh1, sh2


def _sigmoid(z):
    return 1.0 / (1.0 + jnp.exp(-z))


def _fwd_conv_mixer(x, gain, w_in_t, w_conv, w_out, seq, name):
    t_all, d = x.shape
    tt = _token_tile(seq)
    tps = seq // tt

    def body(x_ref, g_ref, win_ref, wc_ref, wout_ref, x1_ref, bcv_ref, ext_ref):
        i = pl.program_id(0)
        xv = x_ref[...]
        r, xh = _rms_parts(xv)
        h = (xh * g_ref[...]).astype(BF16)
        bcv = _nt(h, _mat(win_ref))
        bcv_ref[...] = bcv.astype(BF16)

        @pl.when(i % tps == 0)
        def _():
            ext_ref[0:8, :] = jnp.zeros((8, d), F32)

        cc = _causal_conv3(ext_ref, bcv[:, d:2 * d] * bcv[:, 2 * d:], wc_ref, tt)
        ext_ref[0:8, :] = ext_ref[tt:tt + 8, :]
        y = (bcv[:, :d] * cc).astype(BF16)
        x1_ref[...] = xv + _nn(y, _mat(wout_ref))

    return pl.pallas_call(
        body, name=name, grid=(t_all // tt,),
        in_specs=[pl.BlockSpec((tt, d), lambda i: (i, 0)), _resident((1, d)), _rows_spec(w_in_t),
                  _resident((3, d)), _rows_spec(w_out)],
        out_specs=[pl.BlockSpec((tt, d), lambda i: (i, 0)), pl.BlockSpec((tt, 3 * d), lambda i: (i, 0))],
        out_shape=[SDS((t_all, d), F32), SDS((t_all, 3 * d), BF16)],
        scratch_shapes=[pltpu.VMEM((tt + 8, d), F32)],
        compiler_params=_cparams(),
    )(x, gain, w_in_t.arr, w_conv, w_out.arr)


def _fwd_ffn(x, gain, w_in_t, w_conv, w_down, seq, name):
    t_all, d = x.shape
    f = w_down.n * N_DEV
    tt = _token_tile(seq) // 2
    tps = seq // tt

    def body(x_ref, g_ref, win_ref, wc_ref, wd_ref, x2_ref, gate_ref, s_ref, uds_ref, a_ref, ext_ref):
        i = pl.program_id(0)
        xv = x_ref[...]
        r, xh = _rms_parts(xv)
        h = (xh * g_ref[...]).astype(BF16)
        gu = _nt(h, _mat(win_ref))
        gate = gu[:, :f]
        u = gu[:, f:]
        gate_ref[...] = gate.astype(BF16)

        @pl.when(i % tps == 0)
        def _():
            ext_ref[0:8, :] = jnp.zeros((8, f), F32)

        gc = _causal_conv3(ext_ref, gate, wc_ref, tt)
        ext_ref[0:8, :] = ext_ref[tt:tt + 8, :]
        sig = _sigmoid(gc)
        s = gc * sig
        s_ref[...] = s.astype(BF16)
        uds_ref[...] = (u * (sig * (1.0 + gc * (1.0 - sig)))).astype(BF16)
        a = (s * u).astype(BF16)
        a_ref[...] = a
        x2_ref[...] = xv + _nn(a, _mat(wd_ref))

    wide = pl.BlockSpec((tt, f), lambda i: (i, 0))
    return pl.pallas_call(
        body, name=name, grid=(t_all // tt,),
        in_specs=[pl.BlockSpec((tt, d), lambda i: (i, 0)), _resident((1, d)), _rows_spec(w_in_t),
                  _resident((3, f)), _rows_spec(w_down)],
        out_specs=[pl.BlockSpec((tt, d), lambda i: (i, 0)), wide, wide, wide, wide],
        out_shape=[SDS((t_all, d), F32)] + [SDS((t_all, f), BF16)] * 4,
        scratch_shapes=[pltpu.VMEM((tt + 8, f), F32)],
        compiler_params=_cparams(),
    )(x, gain, w_in_t.arr, w_conv, w_down.arr)


def _rope_partner(xs, lane_lo):
    return jnp.where(lane_lo, pltpu.roll(xs, LANES - HEAD_DIM // 2, 1), pltpu.roll(xs, HEAD_DIM // 2, 1))


def _fwd_qkv(x, gain, w_qkv_t, b_qkv, cos_t, sin_t, seq, name):
    t_all, d = x.shape
    width = w_qkv_t.n * N_DEV
    kvw = (width - d) // 2
    tt = _token_tile(seq)
    tps = seq // tt
    scale = HEAD_DIM ** -0.5

    def body(x_ref, g_ref, w_ref, b_ref, cos_ref, sin_ref, qkv_ref):
        xv = x_ref[...]
        r, xh = _rms_parts(xv)
        h = (xh * g_ref[...]).astype(BF16)
        qkv = _nt(h, _mat(w_ref)) + b_ref[...]
        cosv = cos_ref[...]
        sinv = sin_ref[...]
        lane_lo = (lax.broadcasted_iota(jnp.int32, (tt, LANES), 1) % HEAD_DIM) < HEAD_DIM // 2
        for s in range((d + kvw) // LANES):
            xs = qkv[:, s * LANES:(s + 1) * LANES]
            roped = xs * cosv + _rope_partner(xs, lane_lo) * sinv
            if s * LANES < d:
                roped = roped * scale
            qkv_ref[:, s * LANES:(s + 1) * LANES] = roped.astype(BF16)
        qkv_ref[:, d + kvw:] = qkv[:, d + kvw:].astype(BF16)

    return pl.pallas_call(
        body, name=name, grid=(t_all // tt,),
        in_specs=[pl.BlockSpec((tt, d), lambda i: (i, 0)), _resident((1, d)), _rows_spec(w_qkv_t),
                  _resident((1, width)), pl.BlockSpec((tt, LANES), lambda i: (i % tps, 0)),
                  pl.BlockSpec((tt, LANES), lambda i: (i % tps, 0))],
        out_specs=pl.BlockSpec((tt, width), lambda i: (i, 0)),
        out_shape=SDS((t_all, width), BF16),
        compiler_params=_cparams(),
    )(x, gain, w_qkv_t.arr, b_qkv, cos_t, sin_t)


def _stack_heads(ref, row0, kh):
    return jnp.concatenate(
        [ref[row0:row0 + WINDOW, (kh * GROUP + g) * HEAD_DIM:(kh * GROUP + g + 1) * HEAD_DIM] for g in range(GROUP)],
        axis=0)


def _band_bias():
    r = lax.broadcasted_iota(jnp.int32, (WINDOW, 2 * WINDOW), 0)
    j = lax.broadcasted_iota(jnp.int32, (WINDOW, 2 * WINDOW), 1)
    base = (j > r) & (j <= r + WINDOW)
    return jnp.where(base, 0.0, MASKED_SCORE), jnp.where(base & (j >= WINDOW), 0.0, MASKED_SCORE)


def _softmax_with_sink(s, sink):
    m = jnp.maximum(jnp.max(s, axis=-1, keepdims=True), sink)
    p = jnp.exp(s - m)
    e_sink = jnp.exp(sink - m)
    inv = 1.0 / (jnp.sum(p, axis=-1, keepdims=True) + e_sink)
    return p * inv, e_sink * inv


def _fwd_attention(qkv, x, sinks, w_o, b_o, seq, name):
    t_all, d = x.shape
    width = qkv.shape[1]
    kvw = (width - d) // 2
    n_kv = kvw // HEAD_DIM
    tt = _token_tile(seq)
    tps = seq // tt
    nblk = tt // WINDOW

    def body(sink_ref, qkv_ref, kvp_ref, x_ref, wo_ref, bo_ref, x1_ref, o_ref, kvext_ref, oscr_ref, bias_ref, s_ref, p_ref):
        i = pl.program_id(0)

        @pl.when(i == 0)
        def _():
            bias_ref[0], bias_ref[1] = _band_bias()

        kvext_ref[0:WINDOW, :] = kvp_ref[...]
        kvext_ref[WINDOW:, :] = qkv_ref[:, d:]
        at_seq_start = (i % tps == 0).astype(jnp.int32)
        for n in range(nblk):
            for kh in range(n_kv):
                buf = (n * n_kv + kh) % 2
                qs = _stack_heads(qkv_ref, n * WINDOW, kh)
                kb = kvext_ref[n * WINDOW:(n + 2) * WINDOW, kh * HEAD_DIM:(kh + 1) * HEAD_DIM]
                vb = kvext_ref[n * WINDOW:(n + 2) * WINDOW, kvw + kh * HEAD_DIM:kvw + (kh + 1) * HEAD_DIM]
                s_ref[buf] = _nt(qs, kb)
                for r0 in range(0, GROUP * WINDOW, ROW_CHUNK):
                    q0 = r0 % WINDOW
                    bias = bias_ref[at_seq_start if n == 0 else 0, q0:q0 + ROW_CHUNK, :]
                    probs, _ = _softmax_with_sink(s_ref[buf, r0:r0 + ROW_CHUNK, :] + bias, sink_ref[kh * GROUP + r0 // WINDOW])
                    p_ref[buf, r0:r0 + ROW_CHUNK, :] = probs.astype(BF16)
                o_s = _nn(p_ref[buf], vb)
                for g in range(GROUP):
                    hd = kh * GROUP + g
                    oscr_ref[n * WINDOW:(n + 1) * WINDOW, hd * HEAD_DIM:(hd + 1) * HEAD_DIM] = (
                        o_s[g * WINDOW:(g + 1) * WINDOW].astype(BF16))
        o = oscr_ref[...]
        o_ref[...] = o
        x1_ref[...] = x_ref[...] + _nn(o, _mat(wo_ref)) + bo_ref[...]

    kv_blocks = tt // WINDOW
    return pl.pallas_call(
        body, name=name, grid=(t_all // tt,),
        in_specs=[pl.BlockSpec(memory_space=pltpu.SMEM),
                  pl.BlockSpec((tt, width), lambda i: (i, 0)),
                  pl.BlockSpec((WINDOW, 2 * kvw), lambda i: (jnp.maximum(i * kv_blocks - 1, 0), d // (2 * kvw))),
                  pl.BlockSpec((tt, d), lambda i: (i, 0)), _rows_spec(w_o), _resident((1, d))],
        out_specs=[pl.BlockSpec((tt, d), lambda i: (i, 0)), pl.BlockSpec((tt, d), lambda i: (i, 0))],
        out_shape=[SDS((t_all, d), F32), SDS((t_all, d), BF16)],
        scratch_shapes=[pltpu.VMEM((tt + WINDOW, 2 * kvw), BF16), pltpu.VMEM((tt, d), BF16),
                        pltpu.VMEM((2, WINDOW, 2 * WINDOW), F32), pltpu.VMEM((2, GROUP * WINDOW, 2 * WINDOW), F32),
                        pltpu.VMEM((2, GROUP * WINDOW, 2 * WINDOW), BF16)],
        compiler_params=_cparams(),
    )(sinks, qkv, qkv, x, w_o.arr, b_o)


def _final_norm_loss(x, gain, target, name):
    t_all, d = x.shape
    tt = min(TOKEN_TILE, t_all)

    def body(x_ref, g_ref, t_ref, dx_ref, dg_ref, loss_ref):
        i = pl.program_id(0)
        xv = x_ref[...]
        r, xh = _rms_parts(xv)
        gain_v = g_ref[...]
        e = xh * gain_v - t_ref[...]
        dy = e * (1.0 / d)
        dx_ref[...] = _rms_backward(dy, xh, r, gain_v, 0.0)

        @pl.when(i == 0)
        def _():
            dg_ref[...] = jnp.zeros_like(dg_ref)
            loss_ref[...] = jnp.zeros_like(loss_ref)

        dg_ref[...] += jnp.sum(dy * xh, axis=0, keepdims=True)
        loss_ref[...] += (0.5 / d) * jnp.sum(e * e, axis=0, keepdims=True)

    return pl.pallas_call(
        body, name=name, grid=(t_all // tt,),
        in_specs=[pl.BlockSpec((tt, d), lambda i: (i, 0)), _resident((1, d)), pl.BlockSpec((tt, d), lambda i: (i, 0))],
        out_specs=[pl.BlockSpec((tt, d), lambda i: (i, 0)), pl.BlockSpec((1, d), lambda i: (0, 0)),
                   pl.BlockSpec((1, d), lambda i: (0, 0))],
        out_shape=[SDS((t_all, d), F32), SDS((1, d), F32), SDS((1, d), F32)],
        compiler_params=_cparams(),
    )(x, gain, target)


def _bwd_ffn_inner(dx2, gate, s_act, uds, w_conv, w_down, seq, name):
    t_all, d = dx2.shape
    f = w_down.n * N_DEV
    tt = _token_tile(seq) // 2
    tps = seq // tt
    nt = t_all // tt

    def body(dx_ref, g_ref, s_ref, uds_ref, wc_ref, wd_ref, dgu_ref, dwc_ref, aext_ref):
        i = pl.program_id(0)
        ti = nt - 1 - i
        da = _nt(dx_ref[...].astype(BF16), _mat(wd_ref))
        g = g_ref[...].astype(F32)
        dgc = da * uds_ref[...].astype(F32)

        @pl.when(ti % tps == tps - 1)
        def _():
            aext_ref[tt:tt + 8, :] = jnp.zeros((8, f), F32)

        dg, sh1, sh2 = _anticausal_conv3(aext_ref, dgc, wc_ref, tt)
        aext_ref[tt:tt + 8, :] = aext_ref[0:8, :]
        dgu_ref[:, :f] = dg.astype(BF16)
        dgu_ref[:, f:] = (da * s_ref[...].astype(F32)).astype(BF16)

        @pl.when(i == 0)
        def _():
            dwc_ref[...] = jnp.zeros_like(dwc_ref)

        dwc_ref[0:1, :] += jnp.sum(g * sh2, axis=0, keepdims=True)
        dwc_ref[1:2, :] += jnp.sum(g * sh1, axis=0, keepdims=True)
        dwc_ref[2:3, :] += jnp.sum(g * dgc, axis=0, keepdims=True)

    return pl.pallas_call(
        body, name=name, grid=(nt,),
        in_specs=[pl.BlockSpec((tt, d), lambda i: (nt - 1 - i, 0))] + [pl.BlockSpec((tt, f), lambda i: (nt - 1 - i, 0))] * 3 + [
            _resident((3, f)), _rows_spec(w_down)],
        out_specs=[pl.BlockSpec((tt, 2 * f), lambda i: (nt - 1 - i, 0)), pl.BlockSpec((8, f), lambda i: (0, 0))],
        out_shape=[SDS((t_all, 2 * f), BF16), SDS((8, f), F32)],
        scratch_shapes=[pltpu.VMEM((tt + 8, f), F32)],
        compiler_params=_cparams(),
    )(dx2, gate, s_act, uds, w_conv, w_down.arr)


def _bwd_conv_inner(dx1, bcv, w_conv, w_out, seq, name):
    t_all, d = dx1.shape
    tt = _token_tile(seq)
    tps = seq // tt
    nt = t_all // tt
    halo = 16

    def body(dx_ref, bcv_ref, prev_ref, wc_ref, wout_ref, dbcv_ref, y_ref, dwc_ref, cext_ref, aext_ref):
        i = pl.program_id(0)
        ti = nt - 1 - i
        dy = _nt(dx_ref[...].astype(BF16), _mat(wout_ref))
        bcv_v = bcv_ref[...].astype(F32)
        b = bcv_v[:, :d]
        c = bcv_v[:, d:2 * d]
        v = bcv_v[:, 2 * d:]
        cv = c * v
        prev = prev_ref[...].astype(F32)[halo - 8:, :]
        cext_ref[0:8, :] = jnp.where(ti % tps == 0, 0.0, prev[:, d:2 * d] * prev[:, 2 * d:])
        cc = _causal_conv3(cext_ref, cv, wc_ref, tt)
        y_ref[...] = (b * cc).astype(BF16)
        dcc = dy * b

        @pl.when(ti % tps == tps - 1)
        def _():
            aext_ref[tt:tt + 8, :] = jnp.zeros((8, d), F32)

        dcv, sh1, sh2 = _anticausal_conv3(aext_ref, dcc, wc_ref, tt)
        aext_ref[tt:tt + 8, :] = aext_ref[0:8, :]
        dbcv_ref[:, :d] = (dy * cc).astype(BF16)
        dbcv_ref[:, d:2 * d] = (dcv * v).astype(BF16)
        dbcv_ref[:, 2 * d:] = (dcv * c).astype(BF16)

        @pl.when(i == 0)
        def _():
            dwc_ref[...] = jnp.zeros_like(dwc_ref)

        dwc_ref[0:1, :] += jnp.sum(cv * sh2, axis=0, keepdims=True)
        dwc_ref[1:2, :] += jnp.sum(cv * sh1, axis=0, keepdims=True)
        dwc_ref[2:3, :] += jnp.sum(cv * dcc, axis=0, keepdims=True)

    return pl.pallas_call(
        body, name=name, grid=(nt,),
        in_specs=[pl.BlockSpec((tt, d), lambda i: (nt - 1 - i, 0)),
                  pl.BlockSpec((tt, 3 * d), lambda i: (nt - 1 - i, 0)),
                  pl.BlockSpec((halo, 3 * d), lambda i: (jnp.maximum((nt - 1 - i) * (tt // halo) - 1, 0), 0)),
                  _resident((3, d)), _rows_spec(w_out)],
        out_specs=[pl.BlockSpec((tt, 3 * d), lambda i: (nt - 1 - i, 0)), pl.BlockSpec((tt, d), lambda i: (nt - 1 - i, 0)),
                   pl.BlockSpec((8, d), lambda i: (0, 0))],
        out_shape=[SDS((t_all, 3 * d), BF16), SDS((t_all, d), BF16), SDS((8, d), F32)],
        scratch_shapes=[pltpu.VMEM((tt + 8, d), F32), pltpu.VMEM((tt + 8, d), F32)],
        compiler_params=_cparams(),
    )(dx1, bcv, bcv, w_conv, w_out.arr)


def _bwd_attention_inner(dx1, qkv, sinks, w_o, cos_t, sin_t, seq, name):
    t_all, d = dx1.shape
    width = qkv.shape[1]
    kvw = (width - d) // 2
    n_kv = kvw // HEAD_DIM
    tt = _token_tile(seq)
    tps = seq // tt
    nt = t_all // tt
    nblk = tt // WINDOW
    scale = HEAD_DIM ** -0.5

    def body(sink_ref, dx_ref, qkv_ref, kvp_ref, cos_ref, sin_ref, wo_ref,
             dqkv_ref, dsink_ref, dbqkv_ref, dbo_ref,
             kvext_ref, dkvext_ref, carry_ref, dq_ref, do_ref, bias_ref, s_ref, dp_ref, p_ref, ds_ref):
        i = pl.program_id(0)
        ti = nt - 1 - i
        dxv = dx_ref[...]
        do_ref[...] = _nt(dxv.astype(BF16), _mat(wo_ref)).astype(BF16)
        kvext_ref[0:WINDOW, :] = kvp_ref[...]
        kvext_ref[WINDOW:, :] = qkv_ref[:, d:]
        dkvext_ref[...] = jnp.zeros_like(dkvext_ref)

        @pl.when(i == 0)
        def _():
            bias_ref[0], bias_ref[1] = _band_bias()
            carry_ref[...] = jnp.zeros_like(carry_ref)
            dsink_ref[...] = jnp.zeros_like(dsink_ref)
            dbqkv_ref[...] = jnp.zeros_like(dbqkv_ref)
            dbo_ref[...] = jnp.zeros_like(dbo_ref)

        at_seq_start = (ti % tps == 0).astype(jnp.int32)
        head_lane = lax.broadcasted_iota(jnp.int32, (1, LANES), 1)
        dsink = jnp.zeros((1, LANES), F32)
        for n in range(nblk):
            for kh in range(n_kv):
                buf = (n * n_kv + kh) % 2
                qs = _stack_heads(qkv_ref, n * WINDOW, kh)
                dos = _stack_heads(do_ref, n * WINDOW, kh)
                kcols = slice(kh * HEAD_DIM, (kh + 1) * HEAD_DIM)
                vcols = slice(kvw + kh * HEAD_DIM, kvw + (kh + 1) * HEAD_DIM)
                band = slice(n * WINDOW, (n + 2) * WINDOW)
                kb = kvext_ref[band, kcols]
                vb = kvext_ref[band, vcols]
                s_ref[buf] = _nt(qs, kb)
                dp_ref[buf] = _nt(dos, vb)
                sink_terms = [0.0] * GROUP
                for r0 in range(0, GROUP * WINDOW, ROW_CHUNK):
                    q0 = r0 % WINDOW
                    rows = slice(r0, r0 + ROW_CHUNK)
                    bias = bias_ref[at_seq_start if n == 0 else 0, q0:q0 + ROW_CHUNK, :]
                    probs, p_sink = _softmax_with_sink(s_ref[buf, rows, :] + bias, sink_ref[kh * GROUP + r0 // WINDOW])
                    dp = dp_ref[buf, rows, :]
                    dsum = jnp.sum(probs * dp, axis=-1, keepdims=True)
                    p_ref[buf, rows, :] = probs.astype(BF16)
                    ds_ref[buf, rows, :] = (probs * (dp - dsum)).astype(BF16)
                    sink_terms[r0 // WINDOW] = sink_terms[r0 // WINDOW] + p_sink * dsum
                ds = ds_ref[buf]
                dkvext_ref[band, vcols] += _tn(p_ref[buf], dos)
                dkvext_ref[band, kcols] += _tn(ds, qs)
                dq_s = _nn(ds, kb)
                for g in range(GROUP):
                    hd = kh * GROUP + g
                    dq_ref[n * WINDOW:(n + 1) * WINDOW, hd * HEAD_DIM:(hd + 1) * HEAD_DIM] = dq_s[g * WINDOW:(g + 1) * WINDOW]
                    dsink = dsink - jnp.where(head_lane == hd, jnp.sum(sink_terms[g]), 0.0)
        dsink_ref[0:1, :] += dsink
        dkvext_ref[tt:tt + WINDOW, :] += carry_ref[...]
        carry_ref[...] = dkvext_ref[0:WINDOW, :]

        cosv = cos_ref[...]
        sinv = sin_ref[...]
        lane_lo = (lax.broadcasted_iota(jnp.int32, (tt, LANES), 1) % HEAD_DIM) < HEAD_DIM // 2
        for s in range((d + kvw) // LANES):
            if s * LANES < d:
                dy = dq_ref[:, s * LANES:(s + 1) * LANES] * scale
            else:
                dy = dkvext_ref[WINDOW:, s * LANES - d:(s + 1) * LANES - d]
            dpre = dy * cosv - _rope_partner(dy, lane_lo) * sinv
            dqkv_ref[:, s * LANES:(s + 1) * LANES] = dpre.astype(BF16)
            dbqkv_ref[0:1, s * LANES:(s + 1) * LANES] += jnp.sum(dpre, axis=0, keepdims=True)
        dv = dkvext_ref[WINDOW:, kvw:]
        dqkv_ref[:, d + kvw:] = dv.astype(BF16)
        dbqkv_ref[0:1, d + kvw:] += jnp.sum(dv, axis=0, keepdims=True)
        dbo_ref[...] += jnp.sum(dxv, axis=0, keepdims=True)

    kv_blocks = tt // WINDOW
    return pl.pallas_call(
        body, name=name, grid=(nt,),
        in_specs=[pl.BlockSpec(memory_space=pltpu.SMEM),
                  pl.BlockSpec((tt, d), lambda i: (nt - 1 - i, 0)),
                  pl.BlockSpec((tt, width), lambda i: (nt - 1 - i, 0)),
                  pl.BlockSpec((WINDOW, 2 * kvw), lambda i: (jnp.maximum((nt - 1 - i) * kv_blocks - 1, 0), d // (2 * kvw))),
                  pl.BlockSpec((tt, LANES), lambda i: ((nt - 1 - i) % tps, 0)),
                  pl.BlockSpec((tt, LANES), lambda i: ((nt - 1 - i) % tps, 0)),
                  _rows_spec(w_o)],
        out_specs=[pl.BlockSpec((tt, width), lambda i: (nt - 1 - i, 0)), pl.BlockSpec((8, LANES), lambda i: (0, 0)),
                   pl.BlockSpec((1, width), lambda i: (0, 0)), pl.BlockSpec((1, d), lambda i: (0, 0))],
        out_shape=[SDS((t_all, width), BF16), SDS((8, LANES), F32), SDS((1, width), F32), SDS((1, d), F32)],
        scratch_shapes=[pltpu.VMEM((tt + WINDOW, 2 * kvw), BF16), pltpu.VMEM((tt + WINDOW, 2 * kvw), F32),
                        pltpu.VMEM((WINDOW, 2 * kvw), F32), pltpu.VMEM((tt, d), F32), pltpu.VMEM((tt, d), BF16),
                        pltpu.VMEM((2, WINDOW, 2 * WINDOW), F32), pltpu.VMEM((2, GROUP * WINDOW, 2 * WINDOW), F32),
                        pltpu.VMEM((2, GROUP * WINDOW, 2 * WINDOW), F32), pltpu.VMEM((2, GROUP * WINDOW, 2 * WINDOW), BF16),
                        pltpu.VMEM((2, GROUP * WINDOW, 2 * WINDOW), BF16)],
        compiler_params=_cparams(),
    )(sinks, dx1, qkv, qkv, cos_t, sin_t, w_o.arr)


def _bwd_dense_norm(dy, w_t, x, gain, dres, name):
    t_all, d = x.shape
    n = dy.shape[1]
    tt = min(TOKEN_TILE, t_all)

    def body(dy_ref, w_ref, x_ref, g_ref, dres_ref, dx_ref, h_ref, dg_ref):
        i = pl.program_id(0)
        dh = _nn(dy_ref[...], _mat(w_ref))
        r, xh = _rms_parts(x_ref[...])
        gain_v = g_ref[...]
        h_ref[...] = (xh * gain_v).astype(BF16)
        dx_ref[...] = _rms_backward(dh, xh, r, gain_v, dres_ref[...])

        @pl.when(i == 0)
        def _():
            dg_ref[...] = jnp.zeros_like(dg_ref)

        dg_ref[...] += jnp.sum(dh * xh, axis=0, keepdims=True)

    return pl.pallas_call(
        body, name=name, grid=(t_all // tt,),
        in_specs=[pl.BlockSpec((tt, n), lambda i: (i, 0)), _rows_spec(w_t), pl.BlockSpec((tt, d), lambda i: (i, 0)),
                  _resident((1, d)), pl.BlockSpec((tt, d), lambda i: (i, 0))],
        out_specs=[pl.BlockSpec((tt, d), lambda i: (i, 0)), pl.BlockSpec((tt, d), lambda i: (i, 0)),
                   pl.BlockSpec((1, d), lambda i: (0, 0))],
        out_shape=[SDS((t_all, d), F32), SDS((t_all, d), BF16), SDS((1, d), F32)],
        compiler_params=_cparams(),
    )(dy, w_t.arr, x, gain, dres)


def _tn_matmul(a, b, dest, name):
    t_all, m = a.shape
    d = b.shape[1]
    n = dest.n
    assert m == N_DEV * n and dest.off % n == 0
    k = max(kk for kk in (1, 2, 4, 8) if kk * n <= max(n, 1536))
    tm = k * n
    tt = min(TOKEN_TILE, t_all)
    n_t = t_all // tt
    fresh = not hasattr(dest.arr, "dtype")

    def body(a_ref, b_ref, *rest):
        o_ref, acc_ref = rest[-2:]
        t = pl.program_id(1)

        @pl.when(t == 0)
        def _():
            acc_ref[...] = jnp.zeros_like(acc_ref)

        acc_ref[...] += _tn(a_ref[...], b_ref[...].astype(BF16))

        @pl.when(t == n_t - 1)
        def _():
            o_ref[...] = acc_ref[...].astype(BF16).reshape(k, n, d)

    block = dest.off // n
    return pl.pallas_call(
        body, name=name, grid=(m // tm, n_t),
        in_specs=[pl.BlockSpec((tt, tm), lambda j, t: (t, j)), pl.BlockSpec((tt, d), lambda j, t: (t, 0))] + ([] if fresh else [ANY]),
        out_specs=pl.BlockSpec((k, n, d), lambda j, t: (j, block, 0)),
        out_shape=SDS(tuple(dest.arr) if fresh else dest.arr.shape, BF16),
        scratch_shapes=[pltpu.VMEM((tm, d), F32)],
        input_output_aliases={} if fresh else {2: 0},
        compiler_params=_cparams(2),
    )(*((a, b) if fresh else (a, b, dest.arr)))


def _my_place():
    return lax.axis_index("x"), lax.axis_index("y"), lax.axis_index("c")


def _other_chips(x, y):
    return [(1 - x, y), (x, 1 - y), (1 - x, 1 - y)]


def _all_gather(blocks, name):
    n_arr = len(blocks)

    def body(*refs):
        in_refs = refs[:n_arr]
        out_refs = refs[n_arr:2 * n_arr]
        send_sems, recv_sems, local_sems = refs[2 * n_arr:]
        x, y, c = _my_place()
        me, sibling = (x, y, c), (x, y, 1 - c)
        chips = _other_chips(x, y)

        def slot(a, place):
            px, py, pc = place
            return out_refs[a].at[4 * px + 2 * py + pc]

        def copy(a, k, block, to, src=None):
            return pltpu.make_async_remote_copy(
                src_ref=slot(a, block) if src is None else src, dst_ref=slot(a, block),
                send_sem=send_sems.at[a, k], recv_sem=recv_sems.at[a, k], device_id=to, device_id_type=MESH)

        started = []
        local = []
        for a in range(n_arr):
            mine = pltpu.make_async_copy(in_refs[a], slot(a, me), local_sems.at[a])
            mine.start()
            local.append(mine)
            first = [copy(a, 0, me, sibling, src=in_refs[a])]
            first += [copy(a, 1 + j, me, (*chip, c), src=in_refs[a]) for j, chip in enumerate(chips)]
            for cp in first:
                cp.start()
            started += first
        for a in range(n_arr):
            for j, chip in enumerate(chips):
                copy(a, 1 + j, (*chip, c), me).wait_recv()
                passed = copy(a, 4 + j, (*chip, c), sibling)
                passed.start()
                started.append(passed)
        for a in range(n_arr):
            copy(a, 0, sibling, me).wait_recv()
            for j, chip in enumerate(chips):
                copy(a, 4 + j, (*chip, 1 - c), me).wait_recv()
        for cp in started:
            cp.wait_send()
        for mine in local:
            mine.wait()

    return pl.pallas_call(
        body, name=name,
        in_specs=[ANY] * n_arr, out_specs=[ANY] * n_arr,
        out_shape=[SDS((N_DEV,) + b.shape, b.dtype) for b in blocks],
        scratch_shapes=[pltpu.SemaphoreType.DMA((n_arr, 7)), pltpu.SemaphoreType.DMA((n_arr, 7)),
                        pltpu.SemaphoreType.DMA((n_arr,))],
    )(*blocks)


def _peer_of(k, x, y, c):
    return x ^ ((k >> 2) & 1), y ^ ((k >> 1) & 1), c ^ (k & 1)


HBM = pl.BlockSpec(memory_space=pltpu.HBM)
SEM = pl.BlockSpec(memory_space=pltpu.SEMAPHORE)
DATAFLOW_EFFECT = pltpu.SideEffectType.DATAFLOW_SIDE_EFFECTING


def _peer_copies(src_ref, land_ref, send_sems, recv_sems, per_peer):
    x, y, c = _my_place()
    me = 4 * x + 2 * y + c
    copies = []
    for k in range(1, N_DEV):
        px, py, pc = _peer_of(k, x, y, c)
        peer = 4 * px + 2 * py + pc
        copies.append(pltpu.make_async_remote_copy(
            src_ref=src_ref.at[peer] if per_peer else src_ref, dst_ref=land_ref.at[me],
            send_sem=send_sems.at[k - 1], recv_sem=recv_sems.at[k - 1], device_id=(px, py, pc), device_id_type=MESH))
    own = pltpu.make_async_copy(src_ref.at[me] if per_peer else src_ref, land_ref.at[me], send_sems.at[N_DEV - 1])
    return copies, own


def _exchange_start(src, after, per_peer, name):
    rows, d = src.shape[-2:]

    def body(src_ref, land_ref, after_ref, send_sems, recv_sems, src_thru, land_thru, token):
        copies, own = _peer_copies(src_ref, land_ref, send_sems, recv_sems, per_peer)
        for cp in copies:
            cp.start()
        own.start()
        token[...] = jnp.zeros_like(token)

    return pl.pallas_call(
        body, name=name,
        out_shape=(pltpu.SemaphoreType.DMA((N_DEV,)), pltpu.SemaphoreType.DMA((N_DEV - 1,)), pltpu.HBM(src.shape, src.dtype),
                   pltpu.HBM((N_DEV, rows, d), src.dtype), SDS((SUBLANES, LANES), F32)),
        in_specs=(HBM, HBM, ANY), out_specs=(SEM, SEM, HBM, HBM, pl.BlockSpec(memory_space=pltpu.VMEM)),
        input_output_aliases={0: 2, 1: 3},
        compiler_params=pltpu.CompilerParams(has_side_effects=DATAFLOW_EFFECT),
    )(pltpu.with_memory_space_constraint(src, pltpu.HBM),
      pltpu.with_memory_space_constraint(lax.empty((N_DEV, rows, d), src.dtype), pltpu.HBM), after)


def _exchange_wait(started, after, per_peer, name):
    send_sems, recv_sems, src_thru, land_thru, _ = started

    def body(src_ref, land_ref, send_sems, recv_sems, after_ref, src_out, land_out):
        copies, own = _peer_copies(src_ref, land_ref, send_sems, recv_sems, per_peer)
        for cp in copies:
            cp.wait_send()
            cp.wait_recv()
        own.wait()

    return pl.pallas_call(
        body, name=name,
        out_shape=(pltpu.HBM(src_thru.shape, src_thru.dtype), pltpu.HBM(land_thru.shape, land_thru.dtype)),
        in_specs=(HBM, HBM, SEM, SEM, ANY), out_specs=(HBM, HBM), input_output_aliases={0: 0, 1: 1},
        compiler_params=pltpu.CompilerParams(has_side_effects=DATAFLOW_EFFECT),
    )(src_thru, land_thru, send_sems, recv_sems, after)


def _sum_slots(slots, name):
    _, rows, d = slots.shape
    tr = _largest_divisor(rows, 512, 16)

    def body(s_ref, o_ref):
        acc = s_ref[0].astype(F32)
        for dev in range(1, N_DEV):
            acc = acc + s_ref[dev].astype(F32)
        o_ref[...] = acc

    return pl.pallas_call(
        body, name=name, grid=(rows // tr,),
        in_specs=[pl.BlockSpec((N_DEV, tr, d), lambda r: (0, r, 0))], out_specs=pl.BlockSpec((tr, d), lambda r: (r, 0)),
        out_shape=SDS((rows, d), F32), compiler_params=_cparams(),
    )(slots)


def _all_reduce_small(part, loss_rows, name):
    rows, lanes = part.shape
    lo, hi = loss_rows

    def body(x_ref, out_ref, loss_ref, gath_ref, send_sems, recv_sems):
        x, y, c = _my_place()
        me = 4 * x + 2 * y + c
        gath_ref[me] = x_ref[...]
        copies = []
        for k in range(1, N_DEV):
            peer = (x ^ ((k >> 2) & 1), y ^ ((k >> 1) & 1), c ^ (k & 1))
            cp = pltpu.make_async_remote_copy(
                src_ref=x_ref, dst_ref=gath_ref.at[me], send_sem=send_sems.at[k - 1], recv_sem=recv_sems.at[k - 1],
                device_id=peer, device_id_type=MESH)
            cp.start()
            copies.append(cp)
        for cp in copies:
            cp.wait_recv()
        for cp in copies:
            cp.wait_send()
        acc = gath_ref[0]
        for dev in range(1, N_DEV):
            acc = acc + gath_ref[dev]
        out_ref[...] = acc
        loss_ref[...] = jnp.full(loss_ref.shape, jnp.sum(acc[lo:hi, :]), F32)

    vmem = pl.BlockSpec(memory_space=pltpu.VMEM)
    return pl.pallas_call(
        body, name=name, in_specs=[vmem], out_specs=[vmem, vmem],
        out_shape=[SDS((rows, lanes), F32), SDS((SUBLANES, LANES), F32)],
        scratch_shapes=[pltpu.VMEM((N_DEV, rows, lanes), F32), pltpu.SemaphoreType.DMA((N_DEV - 1,)),
                        pltpu.SemaphoreType.DMA((N_DEV - 1,))],
    )(part)


def _adamw(w, g, m, v, name):
    rows, cols = w.shape
    tr = rows if rows % SUBLANES else _largest_divisor(rows, 512, SUBLANES)

    def body(w_ref, g_ref, m_ref, v_ref, d_ref, nm_ref, nv_ref):
        gv = g_ref[...]
        nm = ADAM_B1 * m_ref[...] + (1.0 - ADAM_B1) * gv
        nv = ADAM_B2 * v_ref[...] + (1.0 - ADAM_B2) * (gv * gv)
        m_hat = nm / (1.0 - ADAM_B1 ** ADAM_STEP)
        v_hat = nv / (1.0 - ADAM_B2 ** ADAM_STEP)
        d_ref[...] = -ADAM_LR * (m_hat / (jnp.sqrt(v_hat) + ADAM_EPS) + ADAM_WD * w_ref[...])
        nm_ref[...] = nm
        nv_ref[...] = nv

    spec = pl.BlockSpec((tr, cols), lambda i: (i, 0))
    return pl.pallas_call(
        body, name=name, grid=(rows // tr,), in_specs=[spec] * 4, out_specs=[spec] * 3,
        out_shape=[SDS((rows, cols), F32)] * 3, compiler_params=_cparams(),
    )(w, g, m, v)


def _adamw_nd(w, g, m, v, name):
    shape = w.shape
    two_d = (1, shape[0]) if len(shape) == 1 else (-1, shape[-1])
    outs = _adamw(w.reshape(two_d), g.reshape(two_d), m.reshape(two_d), v.reshape(two_d), name)
    return [o.reshape(shape) for o in outs]


def _rope_tables(seq):
    pos = jnp.arange(seq, dtype=F32)
    inv_freq = 1.0 / (ROPE_THETA ** (jnp.arange(0, HEAD_DIM, 2, dtype=F32) / HEAD_DIM))
    ang = pos[:, None] * inv_freq[None, :]
    cos, sin = jnp.cos(ang), jnp.sin(ang)
    reps = LANES // HEAD_DIM
    cos_t = jnp.tile(jnp.concatenate([cos, cos], axis=1), (1, reps))
    sin_t = jnp.tile(jnp.concatenate([-sin, sin], axis=1), (1, reps))
    return cos_t, sin_t


def _flat_pad(a):
    flat = a.reshape(1, -1)
    pad = (-flat.shape[1]) % LANES
    return jnp.pad(flat, ((0, 0), (0, pad))) if pad else flat


def kernel(x, norm_mix, norm_ffn, norm_final, conv_w_in, conv_w_conv, conv_w_out, attn_w_qkv, attn_b_qkv, attn_sinks, attn_w_o, attn_b_o, ffn_w_in, ffn_w_conv, ffn_w_down, loss_target, m_norm_mix, m_norm_ffn, m_norm_final, m_conv_w_in, m_conv_w_conv, m_conv_w_out, m_attn_w_qkv, m_attn_b_qkv, m_attn_sinks, m_attn_w_o, m_attn_b_o, m_ffn_w_in, m_ffn_w_conv, m_ffn_w_down, v_norm_mix, v_norm_ffn, v_norm_final, v_conv_w_in, v_conv_w_conv, v_conv_w_out, v_attn_w_qkv, v_attn_b_qkv, v_attn_sinks, v_attn_w_o, v_attn_b_o, v_ffn_w_in, v_ffn_w_conv, v_ffn_w_down):
    b_loc, seq, d = x.shape
    depth = norm_mix.shape[0]
    n_conv, n_attn = conv_w_in.shape[0], attn_w_qkv.shape[0]
    t_all = b_loc * seq
    my_x, my_y, my_c = _my_place()

    me = 4 * my_x + 2 * my_y + my_c

    groups = []
    for i in range(depth):
        j = i // 2
        if i % 2 == 0:
            mix = [("conv_w_in", j, True, conv_w_in[j].T), ("conv_w_out", j, False, conv_w_out[j])]
        else:
            mix = [("attn_w_qkv", j, True, attn_w_qkv[j].T), ("attn_w_o", j, False, attn_w_o[j])]
        groups.append((("mix", i), mix))
        groups.append((("ffn", i), [("ffn_w_in", i, True, ffn_w_in[i].T), ("ffn_w_down", i, False, ffn_w_down[i])]))
    order = [key for key, _ in groups]
    members_of = dict(groups)

    def layout(key):
        offs, o = [], 0
        for _, _, _, shard in members_of[key]:
            n = shard.shape[0]
            o = -(-o // n) * n
            offs.append(o)
            o += n
        return offs, o

    small = jnp.concatenate([_flat_pad(conv_w_conv), _flat_pad(ffn_w_conv), _flat_pad(attn_b_qkv), _flat_pad(attn_b_o)], axis=1)
    (small_g,) = _all_gather([small], "gather_small")

    gather_started = {}

    def start_gather(idx, after):
        if idx >= len(order):
            return 0.0
        key = order[idx]
        offs, total = layout(key)
        pieces, o = [], 0
        for (_, _, _, shard), off in zip(members_of[key], offs):
            if off > o:
                pieces.append(jnp.zeros((off - o, d), shard.dtype))
            pieces.append(shard)
            o = off + shard.shape[0]
        pack = jnp.concatenate(pieces, axis=0).astype(BF16)
        gather_started[key] = _exchange_start(pack, after, False, f"gather_start_{key[0]}_{key[1]}")
        return gather_started[key][4][0, 0]

    weights = {}

    def finish_gather(key, after):
        _, land = _exchange_wait(gather_started[key], after, False, f"gather_wait_{key[0]}_{key[1]}")
        for (wname, layer, _, shard), off in zip(members_of[key], layout(key)[0]):
            weights[(wname, layer)] = _Rows(land, off, shard.shape[0])

    def take_small(o, shape):
        size = shape[0] * shape[1] * shape[2]
        blk = small_g[:, 0, o:o + size].reshape((N_DEV,) + shape)
        return jnp.moveaxis(blk, 0, 2).reshape(shape[0], shape[1], N_DEV * shape[2])

    so = 0
    wc_conv_full = take_small(so, conv_w_conv.shape); so += _flat_pad(conv_w_conv).shape[1]
    wc_ffn_full = take_small(so, ffn_w_conv.shape); so += _flat_pad(ffn_w_conv).shape[1]
    b_qkv_full = take_small(so, (n_attn, 1, attn_b_qkv.shape[1]))[:, 0]; so += _flat_pad(attn_b_qkv).shape[1]
    b_o_full = take_small(so, (n_attn, 1, attn_b_o.shape[1]))[:, 0]

    cos_t, sin_t = _rope_tables(seq)

    xs = [x.reshape(t_all, d)]
    saved = []
    token = start_gather(0, small_g) + start_gather(1, small_g)
    for i in range(depth):
        j = i // 2
        if i > 0:
            token = start_gather(2 * i + 2, xs[-1])
        gain_mix = norm_mix[i][None, :] + token
        finish_gather(("mix", i), gain_mix if i == 0 else xs[-1])
        if i % 2 == 0:
            x1, bcv = _fwd_conv_mixer(xs[-1], gain_mix, weights[("conv_w_in", j)], wc_conv_full[j],
                                      weights[("conv_w_out", j)], seq, f"fwd_conv_{i}")
            mix_saved = (bcv,)
        else:
            qkv = _fwd_qkv(xs[-1], gain_mix, weights[("attn_w_qkv", j)], b_qkv_full[j][None, :], cos_t, sin_t, seq,
                           f"fwd_qkv_{i}")
            x1, o = _fwd_attention(qkv, xs[-1], attn_sinks[j], weights[("attn_w_o", j)], b_o_full[j][None, :], seq,
                                   f"fwd_attn_{i}")
            mix_saved = (qkv, o)
        token = start_gather(2 * i + 3, x1) + (start_gather(2, x1) if i == 0 else 0.0)
        gain_ffn = norm_ffn[i][None, :] + token
        finish_gather(("ffn", i), gain_ffn)
        x2, *ffn_saved = _fwd_ffn(x1, gain_ffn, weights[("ffn_w_in", i)], wc_ffn_full[i], weights[("ffn_w_down", i)],
                                  seq, f"fwd_ffn_{i}")
        saved.append((xs[-1], x1, mix_saved, ffn_saved))
        xs.append(x2)
        token = 0.0

    dx, dg_final, loss_lanes = _final_norm_loss(xs[-1], norm_final[None, :], loss_target.reshape(t_all, d), "loss_head")

    dg_mix, dg_ffn = [None] * depth, [None] * depth
    dwc_conv, dwc_ffn = [None] * n_conv, [None] * depth
    db_qkv, db_o, dsinks = [None] * n_attn, [None] * n_attn, [None] * n_attn
    scatter_started = {}

    def weight_grads(key, operands):
        offs, total = layout(key)
        parts = (N_DEV, total, d)
        for (wname, layer, _, shard), off, (a, b) in zip(members_of[key], offs, operands):
            parts = _tn_matmul(a, b, _Rows(parts, off, shard.shape[0]), f"dw_{wname}_{layer}")
        scatter_started[key] = _exchange_start(parts, operands[0][1], True, f"scatter_start_{key[0]}_{key[1]}")
        return scatter_started[key][4][0, 0]

    token = 0.0
    for i in reversed(range(depth)):
        j = i // 2
        x0, x1, mix_saved, (gate, s_act, uds, act) = saved[i]
        dgu, dwc = _bwd_ffn_inner(dx, gate, s_act, uds, wc_ffn_full[i] + token, weights[("ffn_w_down", i)], seq, f"bwd_ffn_{i}")
        dwc_ffn[i] = dwc[:3]
        dx1, h2, dg_ffn[i] = _bwd_dense_norm(dgu, weights[("ffn_w_in", i)], x1, norm_ffn[i][None, :], dx, f"bwd_ffn_norm_{i}")
        token = weight_grads(("ffn", i), [(dgu, h2), (act, dx)])
        if i % 2 == 0:
            (bcv,) = mix_saved
            dbcv, y, dwc = _bwd_conv_inner(dx1, bcv, wc_conv_full[j] + token, weights[("conv_w_out", j)], seq, f"bwd_conv_{i}")
            dwc_conv[j] = dwc[:3]
            dx, h, dg_mix[i] = _bwd_dense_norm(dbcv, weights[("conv_w_in", j)], x0, norm_mix[i][None, :], dx1,
                                               f"bwd_conv_norm_{i}")
            token = weight_grads(("mix", i), [(dbcv, h), (y, dx1)])
        else:
            qkv, o = mix_saved
            dqkv, dsk, dbq, dbo = _bwd_attention_inner(dx1, qkv, attn_sinks[j] + token, weights[("attn_w_o", j)], cos_t, sin_t,
                                                       seq, f"bwd_attn_{i}")
            dsinks[j], db_qkv[j], db_o[j] = dsk[0:1, :attn_sinks.shape[1]], dbq, dbo
            dx, h, dg_mix[i] = _bwd_dense_norm(dqkv, weights[("attn_w_qkv", j)], x0, norm_mix[i][None, :], dx1,
                                               f"bwd_attn_norm_{i}")
            token = weight_grads(("mix", i), [(dqkv, h), (o, dx1)])
    grad_x = dx.reshape(b_loc, seq, d)

    reduced = {}
    for key, members in reversed(groups):
        _, land = _exchange_wait(scatter_started[key], dx, True, f"scatter_wait_{key[0]}_{key[1]}")
        total = _sum_slots(land, f"scatter_sum_{key[0]}_{key[1]}")
        for (wname, layer, transposed, shard), off in zip(members, layout(key)[0]):
            rows = total[off:off + shard.shape[0]]
            reduced[(wname, layer)] = rows.T if transposed else rows

    small_parts = [jnp.concatenate(dg_mix, axis=0), jnp.concatenate(dg_ffn, axis=0), dg_final,
                   jnp.stack(dwc_conv), jnp.stack(dwc_ffn), jnp.concatenate(db_qkv, axis=0), jnp.concatenate(db_o, axis=0),
                   jnp.concatenate(dsinks, axis=0), loss_lanes]
    flats = [_flat_pad(p) for p in small_parts]
    bounds = []
    so = 0
    for fl in flats:
        bounds.append((so, so + fl.shape[1]))
        so += fl.shape[1]
    small_rows = so // LANES
    pad_rows = (-small_rows) % SUBLANES
    part_small = jnp.pad(jnp.concatenate(flats, axis=1).reshape(small_rows, LANES), ((0, pad_rows), (0, 0)))
    loss_rows = (bounds[-1][0] // LANES, bounds[-1][1] // LANES)
    summed, loss_tile = _all_reduce_small(part_small, loss_rows, "reduce_small")
    summed = summed.reshape(1, -1)

    def small_grad(k, shape):
        lo = bounds[k][0]
        size = 1
        for s_ in shape:
            size *= s_
        return summed[0, lo:lo + size].reshape(shape)

    def my_cols(full, n_local):
        return lax.dynamic_slice_in_dim(full, me * n_local, n_local, axis=full.ndim - 1)

    g_norm_mix = small_grad(0, norm_mix.shape)
    g_norm_ffn = small_grad(1, norm_ffn.shape)
    g_norm_final = small_grad(2, norm_final.shape)
    g_conv_w_conv = my_cols(small_grad(3, (n_conv, 3, d)), conv_w_conv.shape[2])
    g_ffn_w_conv = my_cols(small_grad(4, (depth, 3, ffn_w_conv.shape[2] * N_DEV)), ffn_w_conv.shape[2])
    g_attn_b_qkv = my_cols(small_grad(5, (n_attn, attn_b_qkv.shape[1] * N_DEV)), attn_b_qkv.shape[1])
    g_attn_b_o = my_cols(small_grad(6, (n_attn, d)), attn_b_o.shape[1])
    g_attn_sinks = small_grad(7, attn_sinks.shape)
    loss = loss_tile[0, 0]

    def big_grad(wname, n_layers):
        return jnp.stack([reduced[(wname, layer)] for layer in range(n_layers)])

    grads = {
        "norm_mix": g_norm_mix, "norm_ffn": g_norm_ffn, "norm_final": g_norm_final,
        "conv_w_in": big_grad("conv_w_in", n_conv), "conv_w_conv": g_conv_w_conv, "conv_w_out": big_grad("conv_w_out", n_conv),
        "attn_w_qkv": big_grad("attn_w_qkv", n_attn), "attn_b_qkv": g_attn_b_qkv, "attn_sinks": g_attn_sinks,
        "attn_w_o": big_grad("attn_w_o", n_attn), "attn_b_o": g_attn_b_o,
        "ffn_w_in": big_grad("ffn_w_in", depth), "ffn_w_conv": g_ffn_w_conv, "ffn_w_down": big_grad("ffn_w_down", depth),
    }
    params = {
        "norm_mix": (norm_mix, m_norm_mix, v_norm_mix), "norm_ffn": (norm_ffn, m_norm_ffn, v_norm_ffn),
        "norm_final": (norm_final, m_norm_final, v_norm_final), "conv_w_in": (conv_w_in, m_conv_w_in, v_conv_w_in),
        "conv_w_conv": (conv_w_conv, m_conv_w_conv, v_conv_w_conv), "conv_w_out": (conv_w_out, m_conv_w_out, v_conv_w_out),
        "attn_w_qkv": (attn_w_qkv, m_attn_w_qkv, v_attn_w_qkv), "attn_b_qkv": (attn_b_qkv, m_attn_b_qkv, v_attn_b_qkv),
        "attn_sinks": (attn_sinks, m_attn_sinks, v_attn_sinks), "attn_w_o": (attn_w_o, m_attn_w_o, v_attn_w_o),
        "attn_b_o": (attn_b_o, m_attn_b_o, v_attn_b_o), "ffn_w_in": (ffn_w_in, m_ffn_w_in, v_ffn_w_in),
        "ffn_w_conv": (ffn_w_conv, m_ffn_w_conv, v_ffn_w_conv), "ffn_w_down": (ffn_w_down, m_ffn_w_down, v_ffn_w_down),
    }
    order = list(params)
    deltas, new_ms, new_vs = [], [], []
    for wname in order:
        w, m, v = params[wname]
        dlt, nm, nv = _adamw_nd(w, grads[wname], m, v, f"adamw_{wname}")
        deltas.append(dlt); new_ms.append(nm); new_vs.append(nv)
    return (loss, grad_x, *[grads[wname] for wname in order], *deltas, *new_ms, *new_vs)
```

```python
from typing import NamedTuple

import jax
import jax.numpy as jnp
from jax import lax
from jax.experimental import pallas as pl
from jax.experimental.pallas import tpu as pltpu

F32 = jnp.float32
BF16 = jnp.bfloat16
SDS = jax.ShapeDtypeStruct
MESH = pl.DeviceIdType.MESH
ANY = pl.BlockSpec(memory_space=pl.ANY)

N_DEV = 8
EPS = 1e-5
HEAD_DIM = 64
GROUP = 4
WINDOW = 128
ROPE_THETA = 10000.0
ADAM_LR, ADAM_B1, ADAM_B2, ADAM_EPS, ADAM_WD, ADAM_STEP = 0.001, 0.9, 0.999, 1e-08, 0.01, 10

V7X_VMEM_BYTES = 64 * 1024 * 1024
VMEM_LIMIT_BYTES = V7X_VMEM_BYTES - 8 * 1024 * 1024
LANES = 128
SUBLANES = 8
TOKEN_TILE = 512
TN_TOKEN_TILE = 2048
ROW_CHUNK = 32
MASKED_SCORE = -1e30


def _cparams(n_axes=1):
    return pltpu.CompilerParams(dimension_semantics=("arbitrary",) * n_axes, vmem_limit_bytes=VMEM_LIMIT_BYTES)


def _resident(shape):
    zeros = (0,) * len(shape)
    return pl.BlockSpec(shape, lambda *_: zeros, pipeline_mode=pl.Buffered(1))


class _Rows(NamedTuple):
    arr: jax.Array
    off: int
    n: int


def _rows_spec(w):
    assert w.off % w.n == 0
    block = w.off // w.n
    return pl.BlockSpec((N_DEV, w.n, w.arr.shape[2]), lambda *_: (0, block, 0), pipeline_mode=pl.Buffered(1))


def _mat(ref):
    v = ref[...]
    return v.reshape(v.shape[0] * v.shape[1], v.shape[2])


def _token_tile(seq):
    return min(TOKEN_TILE, seq // 2)


def _largest_divisor(m, cap, mult):
    best = None
    for d in range(mult, min(m, cap) + 1, mult):
        if m % d == 0:
            best = d
    return m if best is None else best


def _nt(a, b):
    return lax.dot_general(a, b, (((1,), (1,)), ((), ())), preferred_element_type=F32)


def _nn(a, b):
    return lax.dot_general(a, b, (((1,), (0,)), ((), ())), preferred_element_type=F32)


def _tn(a, b):
    return lax.dot_general(a, b, (((0,), (0,)), ((), ())), preferred_element_type=F32)


def _rms_parts(xv):
    r = lax.rsqrt(jnp.mean(xv * xv, axis=-1, keepdims=True) + EPS)
    return r, xv * r


def _rms_backward(dh, xh, r, gain, dres):
    u = dh * gain
    return dres + r * (u - xh * jnp.mean(u * xh, axis=-1, keepdims=True))


def _causal_conv3(ext_ref, xv, w_ref, n):
    ext_ref[8:8 + n, :] = xv
    return w_ref[2:3, :] * xv + w_ref[1:2, :] * ext_ref[7:7 + n, :] + w_ref[0:1, :] * ext_ref[6:6 + n, :]


def _anticausal_conv3(ext_ref, xv, w_ref, n):
    ext_ref[0:n, :] = xv
    sh1 = ext_ref[1:1 + n, :]
    sh2 = ext_ref[2:2 + n, :]
    return w_ref[2:3, :] * xv + w_ref[1:2, :] * sh1 + w_ref[0:1, :] * sh2, sh1, sh2


def _sigmoid(z):
    return 1.0 / (1.0 + jnp.exp(-z))


def _fwd_conv_mixer(x, gain, w_in_t, w_conv, w_out, seq, name):
    t_all, d = x.shape
    tt = _token_tile(seq)
    tps = seq // tt

    def body(x_ref, g_ref, win_ref, wc_ref, wout_ref, x1_ref, bcv_ref, cc_ref, y_ref, ext_ref):
        i = pl.program_id(0)
        xv = x_ref[...]
        r, xh = _rms_parts(xv)
        h = (xh * g_ref[...]).astype(BF16)
        bcv = _nt(h, _mat(win_ref))
        bcv_ref[...] = bcv.astype(BF16)

        @pl.when(i % tps == 0)
        def _():
            ext_ref[0:8, :] = jnp.zeros((8, d), F32)

        cc = _causal_conv3(ext_ref, bcv[:, d:2 * d] * bcv[:, 2 * d:], wc_ref, tt)
        ext_ref[0:8, :] = ext_ref[tt:tt + 8, :]
        cc_ref[...] = cc.astype(BF16)
        y = (bcv[:, :d] * cc).astype(BF16)
        y_ref[...] = y
        x1_ref[...] = xv + _nn(y, _mat(wout_ref))

    tile = pl.BlockSpec((tt, d), lambda i: (i, 0))
    return pl.pallas_call(
        body, name=name, grid=(t_all // tt,),
        in_specs=[tile, _resident((1, d)), _rows_spec(w_in_t), _resident((3, d)), _rows_spec(w_out)],
        out_specs=[tile, pl.BlockSpec((tt, 3 * d), lambda i: (i, 0)), tile, tile],
        out_shape=[SDS((t_all, d), F32), SDS((t_all, 3 * d), BF16), SDS((t_all, d), BF16), SDS((t_all, d), BF16)],
        scratch_shapes=[pltpu.VMEM((tt + 8, d), F32)],
        compiler_params=_cparams(),
    )(x, gain, w_in_t.arr, w_conv, w_out.arr)


def _fwd_ffn(x, gain, w_in_t, w_conv, w_down, seq, name):
    t_all, d = x.shape
    f = w_down.n * N_DEV
    tt = _token_tile(seq) // 2
    tps = seq // tt

    def body(x_ref, g_ref, win_ref, wc_ref, wd_ref, x2_ref, gate_ref, s_ref, uds_ref, a_ref, ext_ref):
        i = pl.program_id(0)
        xv = x_ref[...]
        r, xh = _rms_parts(xv)
        h = (xh * g_ref[...]).astype(BF16)
        gu = _nt(h, _mat(win_ref))
        gate = gu[:, :f]
        u = gu[:, f:]
        gate_ref[...] = gate.astype(BF16)

        @pl.when(i % tps == 0)
        def _():
            ext_ref[0:8, :] = jnp.zeros((8, f), F32)

        gc = _causal_conv3(ext_ref, gate, wc_ref, tt)
        ext_ref[0:8, :] = ext_ref[tt:tt + 8, :]
        sig = _sigmoid(gc)
        s = gc * sig
        s_ref[...] = s.astype(BF16)
        uds_ref[...] = (u * (sig * (1.0 + gc * (1.0 - sig)))).astype(BF16)
        a = (s * u).astype(BF16)
        a_ref[...] = a
        x2_ref[...] = xv + _nn(a, _mat(wd_ref))

    wide = pl.BlockSpec((tt, f), lambda i: (i, 0))
    return pl.pallas_call(
        body, name=name, grid=(t_all // tt,),
        in_specs=[pl.BlockSpec((tt, d), lambda i: (i, 0)), _resident((1, d)), _rows_spec(w_in_t),
                  _resident((3, f)), _rows_spec(w_down)],
        out_specs=[pl.BlockSpec((tt, d), lambda i: (i, 0)), wide, wide, wide, wide],
        out_shape=[SDS((t_all, d), F32)] + [SDS((t_all, f), BF16)] * 4,
        scratch_shapes=[pltpu.VMEM((tt + 8, f), F32)],
        compiler_params=_cparams(),
    )(x, gain, w_in_t.arr, w_conv, w_down.arr)


def _rope_partner(xs, lane_lo):
    return jnp.where(lane_lo, pltpu.roll(xs, LANES - HEAD_DIM // 2, 1), pltpu.roll(xs, HEAD_DIM // 2, 1))


def _fwd_qkv(x, gain, w_qkv_t, b_qkv, cos_t, sin_t, seq, name):
    t_all, d = x.shape
    width = w_qkv_t.n * N_DEV
    kvw = (width - d) // 2
    tt = _token_tile(seq)
    tps = seq // tt
    scale = HEAD_DIM ** -0.5

    def body(x_ref, g_ref, w_ref, b_ref, cos_ref, sin_ref, qkv_ref):
        xv = x_ref[...]
        r, xh = _rms_parts(xv)
        h = (xh * g_ref[...]).astype(BF16)
        qkv = _nt(h, _mat(w_ref)) + b_ref[...]
        cosv = cos_ref[...]
        sinv = sin_ref[...]
        lane_lo = (lax.broadcasted_iota(jnp.int32, (tt, LANES), 1) % HEAD_DIM) < HEAD_DIM // 2
        for s in range((d + kvw) // LANES):
            xs = qkv[:, s * LANES:(s + 1) * LANES]
            roped = xs * cosv + _rope_partner(xs, lane_lo) * sinv
            if s * LANES < d:
                roped = roped * scale
            qkv_ref[:, s * LANES:(s + 1) * LANES] = roped.astype(BF16)
        qkv_ref[:, d + kvw:] = qkv[:, d + kvw:].astype(BF16)

    return pl.pallas_call(
        body, name=name, grid=(t_all // tt,),
        in_specs=[pl.BlockSpec((tt, d), lambda i: (i, 0)), _resident((1, d)), _rows_spec(w_qkv_t),
                  _resident((1, width)), pl.BlockSpec((tt, LANES), lambda i: (i % tps, 0)),
                  pl.BlockSpec((tt, LANES), lambda i: (i % tps, 0))],
        out_specs=pl.BlockSpec((tt, width), lambda i: (i, 0)),
        out_shape=SDS((t_all, width), BF16),
        compiler_params=_cparams(),
    )(x, gain, w_qkv_t.arr, b_qkv, cos_t, sin_t)


def _stack_heads(ref, row0, kh):
    return jnp.concatenate(
        [ref[row0:row0 + WINDOW, (kh * GROUP + g) * HEAD_DIM:(kh * GROUP + g + 1) * HEAD_DIM] for g in range(GROUP)],
        axis=0)


def _band_bias():
    r = lax.broadcasted_iota(jnp.int32, (WINDOW, 2 * WINDOW), 0)
    j = lax.broadcasted_iota(jnp.int32, (WINDOW, 2 * WINDOW), 1)
    base = (j > r) & (j <= r + WINDOW)
    return jnp.where(base, 0.0, MASKED_SCORE), jnp.where(base & (j >= WINDOW), 0.0, MASKED_SCORE)


def _softmax_with_sink(s, sink):
    m = jnp.maximum(jnp.max(s, axis=-1, keepdims=True), sink)
    p = jnp.exp(s - m)
    e_sink = jnp.exp(sink - m)
    inv = 1.0 / (jnp.sum(p, axis=-1, keepdims=True) + e_sink)
    return p * inv, e_sink * inv


def _fwd_attention(qkv, x, sinks, w_o, b_o, seq, name):
    t_all, d = x.shape
    width = qkv.shape[1]
    kvw = (width - d) // 2
    n_kv = kvw // HEAD_DIM
    tt = _token_tile(seq)
    tps = seq // tt
    nblk = tt // WINDOW

    def body(sink_ref, qkv_ref, kvp_ref, x_ref, wo_ref, bo_ref, x1_ref, o_ref, kvext_ref, oscr_ref, bias_ref, s_ref, p_ref):
        i = pl.program_id(0)

        @pl.when(i == 0)
        def _():
            bias_ref[0], bias_ref[1] = _band_bias()

        kvext_ref[0:WINDOW, :] = kvp_ref[...]
        kvext_ref[WINDOW:, :] = qkv_ref[:, d:]
        at_seq_start = (i % tps == 0).astype(jnp.int32)
        for n in range(nblk):
            for kh in range(n_kv):
                buf = (n * n_kv + kh) % 2
                qs = _stack_heads(qkv_ref, n * WINDOW, kh)
                kb = kvext_ref[n * WINDOW:(n + 2) * WINDOW, kh * HEAD_DIM:(kh + 1) * HEAD_DIM]
                vb = kvext_ref[n * WINDOW:(n + 2) * WINDOW, kvw + kh * HEAD_DIM:kvw + (kh + 1) * HEAD_DIM]
                s_ref[buf] = _nt(qs, kb)
                for r0 in range(0, GROUP * WINDOW, ROW_CHUNK):
                    q0 = r0 % WINDOW
                    bias = bias_ref[at_seq_start if n == 0 else 0, q0:q0 + ROW_CHUNK, :]
                    probs, _ = _softmax_with_sink(s_ref[buf, r0:r0 + ROW_CHUNK, :] + bias, sink_ref[kh * GROUP + r0 // WINDOW])
                    p_ref[buf, r0:r0 + ROW_CHUNK, :] = probs.astype(BF16)
                o_s = _nn(p_ref[buf], vb)
                for g in range(GROUP):
                    hd = kh * GROUP + g
                    oscr_ref[n * WINDOW:(n + 1) * WINDOW, hd * HEAD_DIM:(hd + 1) * HEAD_DIM] = (
                        o_s[g * WINDOW:(g + 1) * WINDOW].astype(BF16))
        o = oscr_ref[...]
        o_ref[...] = o
        x1_ref[...] = x_ref[...] + _nn(o, _mat(wo_ref)) + bo_ref[...]

    kv_blocks = tt // WINDOW
    return pl.pallas_call(
        body, name=name, grid=(t_all // tt,),
        in_specs=[pl.BlockSpec(memory_space=pltpu.SMEM),
                  pl.BlockSpec((tt, width), lambda i: (i, 0)),
                  pl.BlockSpec((WINDOW, 2 * kvw), lambda i: (jnp.maximum(i * kv_blocks - 1, 0), d // (2 * kvw))),
                  pl.BlockSpec((tt, d), lambda i: (i, 0)), _rows_spec(w_o), _resident((1, d))],
        out_specs=[pl.BlockSpec((tt, d), lambda i: (i, 0)), pl.BlockSpec((tt, d), lambda i: (i, 0))],
        out_shape=[SDS((t_all, d), F32), SDS((t_all, d), BF16)],
        scratch_shapes=[pltpu.VMEM((tt + WINDOW, 2 * kvw), BF16), pltpu.VMEM((tt, d), BF16),
                        pltpu.VMEM((2, WINDOW, 2 * WINDOW), F32), pltpu.VMEM((2, GROUP * WINDOW, 2 * WINDOW), F32),
                        pltpu.VMEM((2, GROUP * WINDOW, 2 * WINDOW), BF16)],
        compiler_params=_cparams(),
    )(sinks, qkv, qkv, x, w_o.arr, b_o)


def _final_norm_loss(x, gain, target, name):
    t_all, d = x.shape
    tt = min(TOKEN_TILE, t_all)

    def body(x_ref, g_ref, t_ref, dx_ref, dg_ref, loss_ref):
        i = pl.program_id(0)
        xv = x_ref[...]
        r, xh = _rms_parts(xv)
        gain_v = g_ref[...]
        e = xh * gain_v - t_ref[...]
        dy = e * (1.0 / d)
        dx_ref[...] = _rms_backward(dy, xh, r, gain_v, 0.0)

        @pl.when(i == 0)
        def _():
            dg_ref[...] = jnp.zeros_like(dg_ref)
            loss_ref[...] = jnp.zeros_like(loss_ref)

        dg_ref[...] += jnp.sum(dy * xh, axis=0, keepdims=True)
        loss_ref[...] += (0.5 / d) * jnp.sum(e * e, axis=0, keepdims=True)

    return pl.pallas_call(
        body, name=name, grid=(t_all // tt,),
        in_specs=[pl.BlockSpec((tt, d), lambda i: (i, 0)), _resident((1, d)), pl.BlockSpec((tt, d), lambda i: (i, 0))],
        out_specs=[pl.BlockSpec((tt, d), lambda i: (i, 0)), pl.BlockSpec((1, d), lambda i: (0, 0)),
                   pl.BlockSpec((1, d), lambda i: (0, 0))],
        out_shape=[SDS((t_all, d), F32), SDS((1, d), F32), SDS((1, d), F32)],
        compiler_params=_cparams(),
    )(x, gain, target)


def _bwd_ffn_inner(dx2, gate, s_act, uds, w_conv, w_down, seq, name):
    t_all, d = dx2.shape
    f = w_down.n * N_DEV
    tt = _token_tile(seq) // 2
    tps = seq // tt
    nt = t_all // tt

    def body(dx_ref, g_ref, s_ref, uds_ref, wc_ref, wd_ref, dgu_ref, dwc_ref, aext_ref):
        i = pl.program_id(0)
        ti = nt - 1 - i
        da = _nt(dx_ref[...].astype(BF16), _mat(wd_ref))
        g = g_ref[...].astype(F32)
        dgc = da * uds_ref[...].astype(F32)

        @pl.when(ti % tps == tps - 1)
        def _():
            aext_ref[tt:tt + 8, :] = jnp.zeros((8, f), F32)

        dg, sh1, sh2 = _anticausal_conv3(aext_ref, dgc, wc_ref, tt)
        aext_ref[tt:tt + 8, :] = aext_ref[0:8, :]
        dgu_ref[:, :f] = dg.astype(BF16)
        dgu_ref[:, f:] = (da * s_ref[...].astype(F32)).astype(BF16)

        @pl.when(i == 0)
        def _():
            dwc_ref[...] = jnp.zeros_like(dwc_ref)

        dwc_ref[0:1, :] += jnp.sum(g * sh2, axis=0, keepdims=True)
        dwc_ref[1:2, :] += jnp.sum(g * sh1, axis=0, keepdims=True)
        dwc_ref[2:3, :] += jnp.sum(g * dgc, axis=0, keepdims=True)

    rev = lambda i: (nt - 1 - i, 0)
    return pl.pallas_call(
        body, name=name, grid=(nt,),
        in_specs=[pl.BlockSpec((tt, d), rev)] + [pl.BlockSpec((tt, f), rev)] * 3 + [_resident((3, f)), _rows_spec(w_down)],
        out_specs=[pl.BlockSpec((tt, 2 * f), rev), pl.BlockSpec((8, f), lambda i: (0, 0))],
        out_shape=[SDS((t_all, 2 * f), BF16), SDS((8, f), F32)],
        scratch_shapes=[pltpu.VMEM((tt + 8, f), F32)],
        compiler_params=_cparams(),
    )(dx2, gate, s_act, uds, w_conv, w_down.arr)


def _bwd_conv_inner(dx1, bcv, cc, w_conv, w_out, seq, name):
    t_all, d = dx1.shape
    tt = _token_tile(seq)
    tps = seq // tt
    nt = t_all // tt

    def body(dx_ref, bcv_ref, cc_ref, wc_ref, wout_ref, dbcv_ref, dwc_ref, aext_ref):
        i = pl.program_id(0)
        ti = nt - 1 - i
        dy = _nt(dx_ref[...].astype(BF16), _mat(wout_ref))
        bcv_v = bcv_ref[...].astype(F32)
        b = bcv_v[:, :d]
        c = bcv_v[:, d:2 * d]
        v = bcv_v[:, 2 * d:]
        cv = c * v
        dcc = dy * b

        @pl.when(ti % tps == tps - 1)
        def _():
            aext_ref[tt:tt + 8, :] = jnp.zeros((8, d), F32)

        dcv, sh1, sh2 = _anticausal_conv3(aext_ref, dcc, wc_ref, tt)
        aext_ref[tt:tt + 8, :] = aext_ref[0:8, :]
        dbcv_ref[:, :d] = (dy * cc_ref[...].astype(F32)).astype(BF16)
        dbcv_ref[:, d:2 * d] = (dcv * v).astype(BF16)
        dbcv_ref[:, 2 * d:] = (dcv * c).astype(BF16)

        @pl.when(i == 0)
        def _():
            dwc_ref[...] = jnp.zeros_like(dwc_ref)

        dwc_ref[0:1, :] += jnp.sum(cv * sh2, axis=0, keepdims=True)
        dwc_ref[1:2, :] += jnp.sum(cv * sh1, axis=0, keepdims=True)
        dwc_ref[2:3, :] += jnp.sum(cv * dcc, axis=0, keepdims=True)

    return pl.pallas_call(
        body, name=name, grid=(nt,),
        in_specs=[pl.BlockSpec((tt, d), lambda i: (nt - 1 - i, 0)),
                  pl.BlockSpec((tt, 3 * d), lambda i: (nt - 1 - i, 0)),
                  pl.BlockSpec((tt, d), lambda i: (nt - 1 - i, 0)),
                  _resident((3, d)), _rows_spec(w_out)],
        out_specs=[pl.BlockSpec((tt, 3 * d), lambda i: (nt - 1 - i, 0)), pl.BlockSpec((8, d), lambda i: (0, 0))],
        out_shape=[SDS((t_all, 3 * d), BF16), SDS((8, d), F32)],
        scratch_shapes=[pltpu.VMEM((tt + 8, d), F32)],
        compiler_params=_cparams(),
    )(dx1, bcv, cc, w_conv, w_out.arr)


def _bwd_attention_inner(dx1, qkv, sinks, w_o, cos_t, sin_t, seq, name):
    t_all, d = dx1.shape
    width = qkv.shape[1]
    kvw = (width - d) // 2
    n_kv = kvw // HEAD_DIM
    tt = _token_tile(seq)
    tps = seq // tt
    nt = t_all // tt
    nblk = tt // WINDOW
    scale = HEAD_DIM ** -0.5

    def body(sink_ref, dx_ref, qkv_ref, kvp_ref, cos_ref, sin_ref, wo_ref,
             dqkv_ref, dsink_ref, dbqkv_ref, dbo_ref,
             kvext_ref, dkvext_ref, carry_ref, dq_ref, do_ref, bias_ref, s_ref, dp_ref, p_ref, ds_ref):
        i = pl.program_id(0)
        ti = nt - 1 - i
        dxv = dx_ref[...]
        do_ref[...] = _nt(dxv.astype(BF16), _mat(wo_ref)).astype(BF16)
        kvext_ref[0:WINDOW, :] = kvp_ref[...]
        kvext_ref[WINDOW:, :] = qkv_ref[:, d:]
        dkvext_ref[...] = jnp.zeros_like(dkvext_ref)

        @pl.when(i == 0)
        def _():
            bias_ref[0], bias_ref[1] = _band_bias()
            carry_ref[...] = jnp.zeros_like(carry_ref)
            dsink_ref[...] = jnp.zeros_like(dsink_ref)
            dbqkv_ref[...] = jnp.zeros_like(dbqkv_ref)
            dbo_ref[...] = jnp.zeros_like(dbo_ref)

        at_seq_start = (ti % tps == 0).astype(jnp.int32)
        head_lane = lax.broadcasted_iota(jnp.int32, (1, LANES), 1)
        dsink = jnp.zeros((1, LANES), F32)
        for n in range(nblk):
            for kh in range(n_kv):
                buf = (n * n_kv + kh) % 2
                qs = _stack_heads(qkv_ref, n * WINDOW, kh)
                dos = _stack_heads(do_ref, n * WINDOW, kh)
                kcols = slice(kh * HEAD_DIM, (kh + 1) * HEAD_DIM)
                vcols = slice(kvw + kh * HEAD_DIM, kvw + (kh + 1) * HEAD_DIM)
                band = slice(n * WINDOW, (n + 2) * WINDOW)
                kb = kvext_ref[band, kcols]
                vb = kvext_ref[band, vcols]
                s_ref[buf] = _nt(qs, kb)
                dp_ref[buf] = _nt(dos, vb)
                sink_terms = [0.0] * GROUP
                for r0 in range(0, GROUP * WINDOW, ROW_CHUNK):
                    q0 = r0 % WINDOW
                    rows = slice(r0, r0 + ROW_CHUNK)
                    bias = bias_ref[at_seq_start if n == 0 else 0, q0:q0 + ROW_CHUNK, :]
                    probs, p_sink = _softmax_with_sink(s_ref[buf, rows, :] + bias, sink_ref[kh * GROUP + r0 // WINDOW])
                    dp = dp_ref[buf, rows, :]
                    dsum = jnp.sum(probs * dp, axis=-1, keepdims=True)
                    p_ref[buf, rows, :] = probs.astype(BF16)
                    ds_ref[buf, rows, :] = (probs * (dp - dsum)).astype(BF16)
                    sink_terms[r0 // WINDOW] = sink_terms[r0 // WINDOW] + p_sink * dsum
                ds = ds_ref[buf]
                dkvext_ref[band, vcols] += _tn(p_ref[buf], dos)
                dkvext_ref[band, kcols] += _tn(ds, qs)
                dq_s = _nn(ds, kb)
                for g in range(GROUP):
                    hd = kh * GROUP + g
                    dq_ref[n * WINDOW:(n + 1) * WINDOW, hd * HEAD_DIM:(hd + 1) * HEAD_DIM] = dq_s[g * WINDOW:(g + 1) * WINDOW]
                    dsink = dsink - jnp.where(head_lane == hd, jnp.sum(sink_terms[g]), 0.0)
        dsink_ref[0:1, :] += dsink
        dkvext_ref[tt:tt + WINDOW, :] += carry_ref[...]
        carry_ref[...] = dkvext_ref[0:WINDOW, :]

        cosv = cos_ref[...]
        sinv = sin_ref[...]
        lane_lo = (lax.broadcasted_iota(jnp.int32, (tt, LANES), 1) % HEAD_DIM) < HEAD_DIM // 2
        for s in range((d + kvw) // LANES):
            if s * LANES < d:
                dy = dq_ref[:, s * LANES:(s + 1) * LANES] * scale
            else:
                dy = dkvext_ref[WINDOW:, s * LANES - d:(s + 1) * LANES - d]
            dpre = dy * cosv - _rope_partner(dy, lane_lo) * sinv
            dqkv_ref[:, s * LANES:(s + 1) * LANES] = dpre.astype(BF16)
            dbqkv_ref[0:1, s * LANES:(s + 1) * LANES] += jnp.sum(dpre, axis=0, keepdims=True)
        dv = dkvext_ref[WINDOW:, kvw:]
        dqkv_ref[:, d + kvw:] = dv.astype(BF16)
        dbqkv_ref[0:1, d + kvw:] += jnp.sum(dv, axis=0, keepdims=True)
        dbo_ref[...] += jnp.sum(dxv, axis=0, keepdims=True)

    kv_blocks = tt // WINDOW
    return pl.pallas_call(
        body, name=name, grid=(nt,),
        in_specs=[pl.BlockSpec(memory_space=pltpu.SMEM),
                  pl.BlockSpec((tt, d), lambda i: (nt - 1 - i, 0)),
                  pl.BlockSpec((tt, width), lambda i: (nt - 1 - i, 0)),
                  pl.BlockSpec((WINDOW, 2 * kvw), lambda i: (jnp.maximum((nt - 1 - i) * kv_blocks - 1, 0), d // (2 * kvw))),
                  pl.BlockSpec((tt, LANES), lambda i: ((nt - 1 - i) % tps, 0)),
                  pl.BlockSpec((tt, LANES), lambda i: ((nt - 1 - i) % tps, 0)),
                  _rows_spec(w_o)],
        out_specs=[pl.BlockSpec((tt, width), lambda i: (nt - 1 - i, 0)), pl.BlockSpec((8, LANES), lambda i: (0, 0)),
                   pl.BlockSpec((1, width), lambda i: (0, 0)), pl.BlockSpec((1, d), lambda i: (0, 0))],
        out_shape=[SDS((t_all, width), BF16), SDS((8, LANES), F32), SDS((1, width), F32), SDS((1, d), F32)],
        scratch_shapes=[pltpu.VMEM((tt + WINDOW, 2 * kvw), BF16), pltpu.VMEM((tt + WINDOW, 2 * kvw), F32),
                        pltpu.VMEM((WINDOW, 2 * kvw), F32), pltpu.VMEM((tt, d), F32), pltpu.VMEM((tt, d), BF16),
                        pltpu.VMEM((2, WINDOW, 2 * WINDOW), F32), pltpu.VMEM((2, GROUP * WINDOW, 2 * WINDOW), F32),
                        pltpu.VMEM((2, GROUP * WINDOW, 2 * WINDOW), F32), pltpu.VMEM((2, GROUP * WINDOW, 2 * WINDOW), BF16),
                        pltpu.VMEM((2, GROUP * WINDOW, 2 * WINDOW), BF16)],
        compiler_params=_cparams(),
    )(sinks, dx1, qkv, qkv, cos_t, sin_t, w_o.arr)


def _bwd_dense_norm(dy, w_t, x, gain, dres, name):
    t_all, d = x.shape
    n = dy.shape[1]
    tt = min(TOKEN_TILE, t_all)

    def body(dy_ref, w_ref, x_ref, g_ref, dres_ref, dx_ref, h_ref, dg_ref):
        i = pl.program_id(0)
        dh = _nn(dy_ref[...], _mat(w_ref))
        r, xh = _rms_parts(x_ref[...])
        gain_v = g_ref[...]
        h_ref[...] = (xh * gain_v).astype(BF16)
        dx_ref[...] = _rms_backward(dh, xh, r, gain_v, dres_ref[...])

        @pl.when(i == 0)
        def _():
            dg_ref[...] = jnp.zeros_like(dg_ref)

        dg_ref[...] += jnp.sum(dh * xh, axis=0, keepdims=True)

    return pl.pallas_call(
        body, name=name, grid=(t_all // tt,),
        in_specs=[pl.BlockSpec((tt, n), lambda i: (i, 0)), _rows_spec(w_t), pl.BlockSpec((tt, d), lambda i: (i, 0)),
                  _resident((1, d)), pl.BlockSpec((tt, d), lambda i: (i, 0))],
        out_specs=[pl.BlockSpec((tt, d), lambda i: (i, 0)), pl.BlockSpec((tt, d), lambda i: (i, 0)),
                   pl.BlockSpec((1, d), lambda i: (0, 0))],
        out_shape=[SDS((t_all, d), F32), SDS((t_all, d), BF16), SDS((1, d), F32)],
        compiler_params=_cparams(),
    )(dy, w_t.arr, x, gain, dres)


def _tn_matmul(a, b, dest, name):
    t_all, m = a.shape
    d = b.shape[1]
    n = dest.n
    assert m == N_DEV * n and dest.off % n == 0
    k = max(kk for kk in (1, 2, 4, 8) if kk * n <= max(n, 1536))
    tm = k * n
    tt = min(TN_TOKEN_TILE, t_all)
    n_t = t_all // tt
    fresh = not hasattr(dest.arr, "dtype")

    def body(a_ref, b_ref, *rest):
        o_ref, acc_ref = rest[-2:]
        t = pl.program_id(1)

        @pl.when(t == 0)
        def _():
            acc_ref[...] = jnp.zeros_like(acc_ref)

        acc_ref[...] += _tn(a_ref[...], b_ref[...].astype(BF16))

        @pl.when(t == n_t - 1)
        def _():
            o_ref[...] = acc_ref[...].astype(BF16).reshape(k, n, d)

    block = dest.off // n
    return pl.pallas_call(
        body, name=name, grid=(m // tm, n_t),
        in_specs=[pl.BlockSpec((tt, tm), lambda j, t: (t, j)), pl.BlockSpec((tt, d), lambda j, t: (t, 0))] + ([] if fresh else [ANY]),
        out_specs=pl.BlockSpec((k, n, d), lambda j, t: (j, block, 0)),
        out_shape=SDS(tuple(dest.arr) if fresh else dest.arr.shape, BF16),
        scratch_shapes=[pltpu.VMEM((tm, d), F32)],
        input_output_aliases={} if fresh else {2: 0},
        compiler_params=_cparams(2),
    )(*((a, b) if fresh else (a, b, dest.arr)))


def _my_place():
    return lax.axis_index("x"), lax.axis_index("y"), lax.axis_index("c")


def _other_chips(x, y):
    return [(1 - x, y), (x, 1 - y), (1 - x, 1 - y)]


def _all_gather(blocks, name):
    n_arr = len(blocks)

    def body(*refs):
        in_refs = refs[:n_arr]
        out_refs = refs[n_arr:2 * n_arr]
        send_sems, recv_sems, local_sems = refs[2 * n_arr:]
        x, y, c = _my_place()
        me, sibling = (x, y, c), (x, y, 1 - c)
        chips = _other_chips(x, y)

        def slot(a, place):
            px, py, pc = place
            return out_refs[a].at[4 * px + 2 * py + pc]

        def copy(a, k, block, to, src=None):
            return pltpu.make_async_remote_copy(
                src_ref=slot(a, block) if src is None else src, dst_ref=slot(a, block),
                send_sem=send_sems.at[a, k], recv_sem=recv_sems.at[a, k], device_id=to, device_id_type=MESH)

        started = []
        local = []
        for a in range(n_arr):
            mine = pltpu.make_async_copy(in_refs[a], slot(a, me), local_sems.at[a])
            mine.start()
            local.append(mine)
            first = [copy(a, 0, me, sibling, src=in_refs[a])]
            first += [copy(a, 1 + j, me, (*chip, c), src=in_refs[a]) for j, chip in enumerate(chips)]
            for cp in first:
                cp.start()
            started += first
        for a in range(n_arr):
            for j, chip in enumerate(chips):
                copy(a, 1 + j, (*chip, c), me).wait_recv()
                passed = copy(a, 4 + j, (*chip, c), sibling)
                passed.start()
                started.append(passed)
        for a in range(n_arr):
            copy(a, 0, sibling, me).wait_recv()
            for j, chip in enumerate(chips):
                copy(a, 4 + j, (*chip, 1 - c), me).wait_recv()
        for cp in started:
            cp.wait_send()
        for mine in local:
            mine.wait()

    return pl.pallas_call(
        body, name=name,
        in_specs=[ANY] * n_arr, out_specs=[ANY] * n_arr,
        out_shape=[SDS((N_DEV,) + b.shape, b.dtype) for b in blocks],
        scratch_shapes=[pltpu.SemaphoreType.DMA((n_arr, 7)), pltpu.SemaphoreType.DMA((n_arr, 7)),
                        pltpu.SemaphoreType.DMA((n_arr,))],
    )(*blocks)


def _peer_of(k, x, y, c):
    return x ^ ((k >> 2) & 1), y ^ ((k >> 1) & 1), c ^ (k & 1)


HBM = pl.BlockSpec(memory_space=pltpu.HBM)
SEM = pl.BlockSpec(memory_space=pltpu.SEMAPHORE)
DATAFLOW_EFFECT = pltpu.SideEffectType.DATAFLOW_SIDE_EFFECTING


def _peer_copies(src_ref, land_ref, send_sems, recv_sems, per_peer):
    x, y, c = _my_place()
    me = 4 * x + 2 * y + c
    copies = []
    for k in range(1, N_DEV):
        px, py, pc = _peer_of(k, x, y, c)
        peer = 4 * px + 2 * py + pc
        copies.append(pltpu.make_async_remote_copy(
            src_ref=src_ref.at[peer] if per_peer else src_ref, dst_ref=land_ref.at[me],
            send_sem=send_sems.at[k - 1], recv_sem=recv_sems.at[k - 1], device_id=(px, py, pc), device_id_type=MESH))
    own = pltpu.make_async_copy(src_ref.at[me] if per_peer else src_ref, land_ref.at[me], send_sems.at[N_DEV - 1])
    return copies, own


def _exchange_start(src, after, per_peer, name):
    rows, d = src.shape[-2:]

    def body(src_ref, land_ref, after_ref, send_sems, recv_sems, src_thru, land_thru, token):
        copies, own = _peer_copies(src_ref, land_ref, send_sems, recv_sems, per_peer)
        for cp in copies:
            cp.start()
        own.start()
        token[...] = jnp.zeros_like(token)

    return pl.pallas_call(
        body, name=name,
        out_shape=(pltpu.SemaphoreType.DMA((N_DEV,)), pltpu.SemaphoreType.DMA((N_DEV - 1,)), pltpu.HBM(src.shape, src.dtype),
                   pltpu.HBM((N_DEV, rows, d), src.dtype), SDS((SUBLANES, LANES), F32)),
        in_specs=(HBM, HBM, ANY), out_specs=(SEM, SEM, HBM, HBM, pl.BlockSpec(memory_space=pltpu.VMEM)),
        input_output_aliases={0: 2, 1: 3},
        compiler_params=pltpu.CompilerParams(has_side_effects=DATAFLOW_EFFECT),
    )(pltpu.with_memory_space_constraint(src, pltpu.HBM),
      pltpu.with_memory_space_constraint(lax.empty((N_DEV, rows, d), src.dtype), pltpu.HBM), after)


def _exchange_wait(started, after, per_peer, name):
    send_sems, recv_sems, src_thru, land_thru, _ = started

    def body(src_ref, land_ref, send_sems, recv_sems, after_ref, src_out, land_out):
        copies, own = _peer_copies(src_ref, land_ref, send_sems, recv_sems, per_peer)
        for cp in copies:
            cp.wait_send()
            cp.wait_recv()
        own.wait()

    return pl.pallas_call(
        body, name=name,
        out_shape=(pltpu.HBM(src_thru.shape, src_thru.dtype), pltpu.HBM(land_thru.shape, land_thru.dtype)),
        in_specs=(HBM, HBM, SEM, SEM, ANY), out_specs=(HBM, HBM), input_output_aliases={0: 0, 1: 1},
        compiler_params=pltpu.CompilerParams(has_side_effects=DATAFLOW_EFFECT),
    )(src_thru, land_thru, send_sems, recv_sems, after)


def _sum_slots(slots, name):
    _, rows, d = slots.shape
    tr = _largest_divisor(rows, 512, 16)

    def body(s_ref, o_ref):
        acc = s_ref[0].astype(F32)
        for dev in range(1, N_DEV):
            acc = acc + s_ref[dev].astype(F32)
        o_ref[...] = acc

    return pl.pallas_call(
        body, name=name, grid=(rows // tr,),
        in_specs=[pl.BlockSpec((N_DEV, tr, d), lambda r: (0, r, 0))], out_specs=pl.BlockSpec((tr, d), lambda r: (r, 0)),
        out_shape=SDS((rows, d), F32), compiler_params=_cparams(),
    )(slots)


def _all_reduce_small(part, loss_rows, name):
    rows, lanes = part.shape
    lo, hi = loss_rows

    def body(x_ref, out_ref, loss_ref, gath_ref, send_sems, recv_sems):
        x, y, c = _my_place()
        me = 4 * x + 2 * y + c
        gath_ref[me] = x_ref[...]
        copies = []
        for k in range(1, N_DEV):
            peer = (x ^ ((k >> 2) & 1), y ^ ((k >> 1) & 1), c ^ (k & 1))
            cp = pltpu.make_async_remote_copy(
                src_ref=x_ref, dst_ref=gath_ref.at[me], send_sem=send_sems.at[k - 1], recv_sem=recv_sems.at[k - 1],
                device_id=peer, device_id_type=MESH)
            cp.start()
            copies.append(cp)
        for cp in copies:
            cp.wait_recv()
        for cp in copies:
            cp.wait_send()
        acc = gath_ref[0]
        for dev in range(1, N_DEV):
            acc = acc + gath_ref[dev]
        out_ref[...] = acc
        loss_ref[...] = jnp.full(loss_ref.shape, jnp.sum(acc[lo:hi, :]), F32)

    vmem = pl.BlockSpec(memory_space=pltpu.VMEM)
    return pl.pallas_call(
        body, name=name, in_specs=[vmem], out_specs=[vmem, vmem],
        out_shape=[SDS((rows, lanes), F32), SDS((SUBLANES, LANES), F32)],
        scratch_shapes=[pltpu.VMEM((N_DEV, rows, lanes), F32), pltpu.SemaphoreType.DMA((N_DEV - 1,)),
                        pltpu.SemaphoreType.DMA((N_DEV - 1,))],
    )(part)


def _adamw(w, g, m, v, name):
    rows, cols = w.shape
    tr = rows if rows % SUBLANES else _largest_divisor(rows, 512, SUBLANES)

    def body(w_ref, g_ref, m_ref, v_ref, d_ref, nm_ref, nv_ref):
        gv = g_ref[...]
        nm = ADAM_B1 * m_ref[...] + (1.0 - ADAM_B1) * gv
        nv = ADAM_B2 * v_ref[...] + (1.0 - ADAM_B2) * (gv * gv)
        m_hat = nm / (1.0 - ADAM_B1 ** ADAM_STEP)
        v_hat = nv / (1.0 - ADAM_B2 ** ADAM_STEP)
        d_ref[...] = -ADAM_LR * (m_hat / (jnp.sqrt(v_hat) + ADAM_EPS) + ADAM_WD * w_ref[...])
        nm_ref[...] = nm
        nv_ref[...] = nv

    spec = pl.BlockSpec((tr, cols), lambda i: (i, 0))
    return pl.pallas_call(
        body, name=name, grid=(rows // tr,), in_specs=[spec] * 4, out_specs=[spec] * 3,
        out_shape=[SDS((rows, cols), F32)] * 3, compiler_params=_cparams(),
    )(w, g, m, v)


def _adamw_nd(w, g, m, v, name):
    shape = w.shape
    two_d = (1, shape[0]) if len(shape) == 1 else (-1, shape[-1])
    outs = _adamw(w.reshape(two_d), g.reshape(two_d), m.reshape(two_d), v.reshape(two_d), name)
    return [o.reshape(shape) for o in outs]


def _rope_tables(seq):
    pos = jnp.arange(seq, dtype=F32)
    inv_freq = 1.0 / (ROPE_THETA ** (jnp.arange(0, HEAD_DIM, 2, dtype=F32) / HEAD_DIM))
    ang = pos[:, None] * inv_freq[None, :]
    cos, sin = jnp.cos(ang), jnp.sin(ang)
    reps = LANES // HEAD_DIM
    cos_t = jnp.tile(jnp.concatenate([cos, cos], axis=1), (1, reps))
    sin_t = jnp.tile(jnp.concatenate([-sin, sin], axis=1), (1, reps))
    return cos_t, sin_t


def _flat_pad(a):
    flat = a.reshape(1, -1)
    pad = (-flat.shape[1]) % LANES
    return jnp.pad(flat, ((0, 0), (0, pad))) if pad else flat


def kernel(x, norm_mix, norm_ffn, norm_final, conv_w_in, conv_w_conv, conv_w_out, attn_w_qkv, attn_b_qkv, attn_sinks, attn_w_o, attn_b_o, ffn_w_in, ffn_w_conv, ffn_w_down, loss_target, m_norm_mix, m_norm_ffn, m_norm_final, m_conv_w_in, m_conv_w_conv, m_conv_w_out, m_attn_w_qkv, m_attn_b_qkv, m_attn_sinks, m_attn_w_o, m_attn_b_o, m_ffn_w_in, m_ffn_w_conv, m_ffn_w_down, v_norm_mix, v_norm_ffn, v_norm_final, v_conv_w_in, v_conv_w_conv, v_conv_w_out, v_attn_w_qkv, v_attn_b_qkv, v_attn_sinks, v_attn_w_o, v_attn_b_o, v_ffn_w_in, v_ffn_w_conv, v_ffn_w_down):
    b_loc, seq, d = x.shape
    depth = norm_mix.shape[0]
    n_conv, n_attn = conv_w_in.shape[0], attn_w_qkv.shape[0]
    t_all = b_loc * seq
    my_x, my_y, my_c = _my_place()

    me = 4 * my_x + 2 * my_y + my_c

    groups = []
    for i in range(depth):
        j = i // 2
        if i % 2 == 0:
            mix = [("conv_w_in", j, True, conv_w_in[j].T), ("conv_w_out", j, False, conv_w_out[j])]
        else:
            mix = [("attn_w_qkv", j, True, attn_w_qkv[j].T), ("attn_w_o", j, False, attn_w_o[j])]
        groups.append((("mix", i), mix))
        groups.append((("ffn", i), [("ffn_w_in", i, True, ffn_w_in[i].T), ("ffn_w_down", i, False, ffn_w_down[i])]))
    order = [key for key, _ in groups]
    members_of = dict(groups)

    def layout(key):
        offs, o = [], 0
        for _, _, _, shard in members_of[key]:
            n = shard.shape[0]
            o = -(-o // n) * n
            offs.append(o)
            o += n
        return offs, o

    small = jnp.concatenate([_flat_pad(conv_w_conv), _flat_pad(ffn_w_conv), _flat_pad(attn_b_qkv), _flat_pad(attn_b_o)], axis=1)
    (small_g,) = _all_gather([small], "gather_small")

    gather_started = {}

    def start_gather(idx, after):
        if idx >= len(order):
            return 0.0
        key = order[idx]
        offs, total = layout(key)
        pieces, o = [], 0
        for (_, _, _, shard), off in zip(members_of[key], offs):
            if off > o:
                pieces.append(jnp.zeros((off - o, d), shard.dtype))
            pieces.append(shard)
            o = off + shard.shape[0]
        pack = jnp.concatenate(pieces, axis=0).astype(BF16)
        gather_started[key] = _exchange_start(pack, after, False, f"gather_start_{key[0]}_{key[1]}")
        return gather_started[key][4][0, 0]

    weights = {}

    def finish_gather(key, after):
        _, land = _exchange_wait(gather_started[key], after, False, f"gather_wait_{key[0]}_{key[1]}")
        for (wname, layer, _, shard), off in zip(members_of[key], layout(key)[0]):
            weights[(wname, layer)] = _Rows(land, off, shard.shape[0])

    def take_small(o, shape):
        size = shape[0] * shape[1] * shape[2]
        blk = small_g[:, 0, o:o + size].reshape((N_DEV,) + shape)
        return jnp.moveaxis(blk, 0, 2).reshape(shape[0], shape[1], N_DEV * shape[2])

    so = 0
    wc_conv_full = take_small(so, conv_w_conv.shape); so += _flat_pad(conv_w_conv).shape[1]
    wc_ffn_full = take_small(so, ffn_w_conv.shape); so += _flat_pad(ffn_w_conv).shape[1]
    b_qkv_full = take_small(so, (n_attn, 1, attn_b_qkv.shape[1]))[:, 0]; so += _flat_pad(attn_b_qkv).shape[1]
    b_o_full = take_small(so, (n_attn, 1, attn_b_o.shape[1]))[:, 0]

    cos_t, sin_t = _rope_tables(seq)

    xs = [x.reshape(t_all, d)]
    saved = []
    token = start_gather(0, small_g) + start_gather(1, small_g)
    for i in range(depth):
        j = i // 2
        if i > 0:
            token = start_gather(2 * i + 2, xs[-1])
        gain_mix = norm_mix[i][None, :] + token
        finish_gather(("mix", i), gain_mix if i == 0 else xs[-1])
        if i % 2 == 0:
            x1, *mix_saved = _fwd_conv_mixer(xs[-1], gain_mix, weights[("conv_w_in", j)], wc_conv_full[j],
                                             weights[("conv_w_out", j)], seq, f"fwd_conv_{i}")
        else:
            qkv = _fwd_qkv(xs[-1], gain_mix, weights[("attn_w_qkv", j)], b_qkv_full[j][None, :], cos_t, sin_t, seq,
                           f"fwd_qkv_{i}")
            x1, o = _fwd_attention(qkv, xs[-1], attn_sinks[j], weights[("attn_w_o", j)], b_o_full[j][None, :], seq,
                                   f"fwd_attn_{i}")
            mix_saved = (qkv, o)
        token = start_gather(2 * i + 3, x1) + (start_gather(2, x1) if i == 0 else 0.0)
        gain_ffn = norm_ffn[i][None, :] + token
        finish_gather(("ffn", i), gain_ffn)
        x2, *ffn_saved = _fwd_ffn(x1, gain_ffn, weights[("ffn_w_in", i)], wc_ffn_full[i], weights[("ffn_w_down", i)],
                                  seq, f"fwd_ffn_{i}")
        saved.append((xs[-1], x1, mix_saved, ffn_saved))
        xs.append(x2)
        token = 0.0

    dx, dg_final, loss_lanes = _final_norm_loss(xs[-1], norm_final[None, :], loss_target.reshape(t_all, d), "loss_head")

    dg_mix, dg_ffn = [None] * depth, [None] * depth
    dwc_conv, dwc_ffn = [None] * n_conv, [None] * depth
    db_qkv, db_o, dsinks = [None] * n_attn, [None] * n_attn, [None] * n_attn
    scatter_started = {}

    def weight_grads(key, operands):
        offs, total = layout(key)
        parts = (N_DEV, total, d)
        for (wname, layer, _, shard), off, (a, b) in zip(members_of[key], offs, operands):
            parts = _tn_matmul(a, b, _Rows(parts, off, shard.shape[0]), f"dw_{wname}_{layer}")
        scatter_started[key] = _exchange_start(parts, operands[0][1], True, f"scatter_start_{key[0]}_{key[1]}")
        return scatter_started[key][4][0, 0]

    token = 0.0
    for i in reversed(range(depth)):
        j = i // 2
        x0, x1, mix_saved, (gate, s_act, uds, act) = saved[i]
        dgu, dwc = _bwd_ffn_inner(dx, gate, s_act, uds, wc_ffn_full[i] + token, weights[("ffn_w_down", i)], seq, f"bwd_ffn_{i}")
        dwc_ffn[i] = dwc[:3]
        dx1, h2, dg_ffn[i] = _bwd_dense_norm(dgu, weights[("ffn_w_in", i)], x1, norm_ffn[i][None, :], dx, f"bwd_ffn_norm_{i}")
        token = weight_grads(("ffn", i), [(dgu, h2), (act, dx)])
        if i % 2 == 0:
            bcv, cc, y = mix_saved
            dbcv, dwc = _bwd_conv_inner(dx1, bcv, cc, wc_conv_full[j] + token, weights[("conv_w_out", j)], seq, f"bwd_conv_{i}")
            dwc_conv[j] = dwc[:3]
            dx, h, dg_mix[i] = _bwd_dense_norm(dbcv, weights[("conv_w_in", j)], x0, norm_mix[i][None, :], dx1,
                                               f"bwd_conv_norm_{i}")
            token = weight_grads(("mix", i), [(dbcv, h), (y, dx1)])
        else:
            qkv, o = mix_saved
            dqkv, dsk, dbq, dbo = _bwd_attention_inner(dx1, qkv, attn_sinks[j] + token, weights[("attn_w_o", j)], cos_t, sin_t,
                                                       seq, f"bwd_attn_{i}")
            dsinks[j], db_qkv[j], db_o[j] = dsk[0:1, :attn_sinks.shape[1]], dbq, dbo
            dx, h, dg_mix[i] = _bwd_dense_norm(dqkv, weights[("attn_w_qkv", j)], x0, norm_mix[i][None, :], dx1,
                                               f"bwd_attn_norm_{i}")
            token = weight_grads(("mix", i), [(dqkv, h), (o, dx1)])
    grad_x = dx.reshape(b_loc, seq, d)

    reduced = {}
    for key, members in reversed(groups):
        _, land = _exchange_wait(scatter_started[key], dx, True, f"scatter_wait_{key[0]}_{key[1]}")
        total = _sum_slots(land, f"scatter_sum_{key[0]}_{key[1]}")
        for (wname, layer, transposed, shard), off in zip(members, layout(key)[0]):
            rows = total[off:off + shard.shape[0]]
            reduced[(wname, layer)] = rows.T if transposed else rows

    small_parts = [jnp.concatenate(dg_mix, axis=0), jnp.concatenate(dg_ffn, axis=0), dg_final,
                   jnp.stack(dwc_conv), jnp.stack(dwc_ffn), jnp.concatenate(db_qkv, axis=0), jnp.concatenate(db_o, axis=0),
                   jnp.concatenate(dsinks, axis=0), loss_lanes]
    flats = [_flat_pad(p) for p in small_parts]
    bounds = []
    so = 0
    for fl in flats:
        bounds.append((so, so + fl.shape[1]))
        so += fl.shape[1]
    small_rows = so // LANES
    pad_rows = (-small_rows) % SUBLANES
    part_small = jnp.pad(jnp.concatenate(flats, axis=1).reshape(small_rows, LANES), ((0, pad_rows), (0, 0)))
    loss_rows = (bounds[-1][0] // LANES, bounds[-1][1] // LANES)
    summed, loss_tile = _all_reduce_small(part_small, loss_rows, "reduce_small")
    summed = summed.reshape(1, -1)

    def small_grad(k, shape):
        lo = bounds[k][0]
        size = 1
        for s_ in shape:
            size *= s_
        return summed[0, lo:lo + size].reshape(shape)

    def my_cols(full, n_local):
        return lax.dynamic_slice_in_dim(full, me * n_local, n_local, axis=full.ndim - 1)

    g_norm_mix = small_grad(0, norm_mix.shape)
    g_norm_ffn = small_grad(1, norm_ffn.shape)
    g_norm_final = small_grad(2, norm_final.shape)
    g_conv_w_conv = my_cols(small_grad(3, (n_conv, 3, d)), conv_w_conv.shape[2])
    g_ffn_w_conv = my_cols(small_grad(4, (depth, 3, ffn_w_conv.shape[2] * N_DEV)), ffn_w_conv.shape[2])
    g_attn_b_qkv = my_cols(small_grad(5, (n_attn, attn_b_qkv.shape[1] * N_DEV)), attn_b_qkv.shape[1])
    g_attn_b_o = my_cols(small_grad(6, (n_attn, d)), attn_b_o.shape[1])
    g_attn_sinks = small_grad(7, attn_sinks.shape)
    loss = loss_tile[0, 0]

    def big_grad(wname, n_layers):
        return jnp.stack([reduced[(wname, layer)] for layer in range(n_layers)])

    grads = {
        "norm_mix": g_norm_mix, "norm_ffn": g_norm_ffn, "norm_final": g_norm_final,
        "conv_w_in": big_grad("conv_w_in", n_conv), "conv_w_conv": g_conv_w_conv, "conv_w_out": big_grad("conv_w_out", n_conv),
        "attn_w_qkv": big_grad("attn_w_qkv", n_attn), "attn_b_qkv": g_attn_b_qkv, "attn_sinks": g_attn_sinks,
        "attn_w_o": big_grad("attn_w_o", n_attn), "attn_b_o": g_attn_b_o,
        "ffn_w_in": big_grad("ffn_w_in", depth), "ffn_w_conv": g_ffn_w_conv, "ffn_w_down": big_grad("ffn_w_down", depth),
    }
    params = {
        "norm_mix": (norm_mix, m_norm_mix, v_norm_mix), "norm_ffn": (norm_ffn, m_norm_ffn, v_norm_ffn),
        "norm_final": (norm_final, m_norm_final, v_norm_final), "conv_w_in": (conv_w_in, m_conv_w_in, v_conv_w_in),
        "conv_w_conv": (conv_w_conv, m_conv_w_conv, v_conv_w_conv), "conv_w_out": (conv_w_out, m_conv_w_out, v_conv_w_out),
        "attn_w_qkv": (attn_w_qkv, m_attn_w_qkv, v_attn_w_qkv), "attn_b_qkv": (attn_b_qkv, m_attn_b_qkv, v_attn_b_qkv),
        "attn_sinks": (attn_sinks, m_attn_sinks, v_attn_sinks), "attn_w_o": (attn_w_o, m_attn_w_o, v_attn_w_o),
        "attn_b_o": (attn_b_o, m_attn_b_o, v_attn_b_o), "ffn_w_in": (ffn_w_in, m_ffn_w_in, v_ffn_w_in),
        "ffn_w_conv": (ffn_w_conv, m_ffn_w_conv, v_ffn_w_conv), "ffn_w_down": (ffn_w_down, m_ffn_w_down, v_ffn_w_down),
    }
    order = list(params)
    deltas, new_ms, new_vs = [], [], []
    for wname in order:
        w, m, v = params[wname]
        dlt, nm, nv = _adamw_nd(w, grads[wname], m, v, f"adamw_{wname}")
        deltas.append(dlt); new_ms.append(nm); new_vs.append(nv)
    return (loss, grad_x, *[grads[wname] for wname in order], *deltas, *new_ms, *new_vs)
```

```python
from typing import NamedTuple

import jax
import jax.numpy as jnp
from jax import lax
from jax.experimental import pallas as pl
from jax.experimental.pallas import tpu as pltpu

F32 = jnp.float32
BF16 = jnp.bfloat16
SDS = jax.ShapeDtypeStruct
MESH = pl.DeviceIdType.MESH
ANY = pl.BlockSpec(memory_space=pl.ANY)

N_DEV = 8
EPS = 1e-5
HEAD_DIM = 64
GROUP = 4
WINDOW = 128
ROPE_THETA = 10000.0
ADAM_LR, ADAM_B1, ADAM_B2, ADAM_EPS, ADAM_WD, ADAM_STEP = 0.001, 0.9, 0.999, 1e-08, 0.01, 10

V7X_VMEM_BYTES = 64 * 1024 * 1024
VMEM_LIMIT_BYTES = V7X_VMEM_BYTES - 8 * 1024 * 1024
LANES = 128
SUBLANES = 8
TOKEN_TILE = 512
TN_TOKEN_TILE = 2048
ROW_CHUNK = 32
MASKED_SCORE = -1e30


def _cparams(n_axes=1):
    return pltpu.CompilerParams(dimension_semantics=("arbitrary",) * n_axes, vmem_limit_bytes=VMEM_LIMIT_BYTES)


def _resident(shape):
    zeros = (0,) * len(shape)
    return pl.BlockSpec(shape, lambda *_: zeros, pipeline_mode=pl.Buffered(1))


class _Rows(NamedTuple):
    arr: jax.Array
    off: int
    n: int


def _rows_spec(w):
    assert w.off % w.n == 0
    block = w.off // w.n
    return pl.BlockSpec((N_DEV, w.n, w.arr.shape[2]), lambda *_: (0, block, 0), pipeline_mode=pl.Buffered(1))


def _mat(ref):
    v = ref[...]
    return v.reshape(v.shape[0] * v.shape[1], v.shape[2])


def _token_tile(seq):
    return min(TOKEN_TILE, seq // 2)


def _largest_divisor(m, cap, mult):
    best = None
    for d in range(mult, min(m, cap) + 1, mult):
        if m % d == 0:
            best = d
    return m if best is None else best


def _nt(a, b):
    return lax.dot_general(a, b, (((1,), (1,)), ((), ())), preferred_element_type=F32)


def _nn(a, b):
    return lax.dot_general(a, b, (((1,), (0,)), ((), ())), preferred_element_type=F32)


def _tn(a, b):
    return lax.dot_general(a, b, (((0,), (0,)), ((), ())), preferred_element_type=F32)


def _rms_parts(xv):
    r = lax.rsqrt(jnp.mean(xv * xv, axis=-1, keepdims=True) + EPS)
    return r, xv * r


def _rms_backward(dh, xh, r, gain, dres):
    u = dh * gain
    return dres + r * (u - xh * jnp.mean(u * xh, axis=-1, keepdims=True))


def _causal_conv3(ext_ref, xv, w_ref, n):
    ext_ref[8:8 + n, :] = xv
    return w_ref[2:3, :] * xv + w_ref[1:2, :] * ext_ref[7:7 + n, :] + w_ref[0:1, :] * ext_ref[6:6 + n, :]


def _anticausal_conv3(ext_ref, xv, w_ref, n):
    ext_ref[0:n, :] = xv
    sh1 = ext_ref[1:1 + n, :]
    sh2 = ext_ref[2:2 + n, :]
    return w_ref[2:3, :] * xv + w_ref[1:2, :] * sh1 + w_ref[0:1, :] * sh2, sh1, sh2


def _sigmoid(z):
    return 1.0 / (1.0 + jnp.exp(-z))


def _fwd_conv_mixer(x, gain, w_in_t, w_conv, w_out, seq, name):
    t_all, d = x.shape
    tt = _token_tile(seq)
    tps = seq // tt

    def body(x_ref, g_ref, win_ref, wc_ref, wout_ref, x1_ref, bcv_ref, cc_ref, y_ref, ext_ref):
        i = pl.program_id(0)
        xv = x_ref[...]
        r, xh = _rms_parts(xv)
        h = (xh * g_ref[...]).astype(BF16)
        bcv = _nt(h, _mat(win_ref))
        bcv_ref[...] = bcv.astype(BF16)

        @pl.when(i % tps == 0)
        def _():
            ext_ref[0:8, :] = jnp.zeros((8, d), F32)

        cc = _causal_conv3(ext_ref, bcv[:, d:2 * d] * bcv[:, 2 * d:], wc_ref, tt)
        ext_ref[0:8, :] = ext_ref[tt:tt + 8, :]
        cc_ref[...] = cc.astype(BF16)
        y = (bcv[:, :d] * cc).astype(BF16)
        y_ref[...] = y
        x1_ref[...] = xv + _nn(y, _mat(wout_ref))

    tile = pl.BlockSpec((tt, d), lambda i: (i, 0))
    return pl.pallas_call(
        body, name=name, grid=(t_all // tt,),
        in_specs=[tile, _resident((1, d)), _rows_spec(w_in_t), _resident((3, d)), _rows_spec(w_out)],
        out_specs=[tile, pl.BlockSpec((tt, 3 * d), lambda i: (i, 0)), tile, tile],
        out_shape=[SDS((t_all, d), F32), SDS((t_all, 3 * d), BF16), SDS((t_all, d), BF16), SDS((t_all, d), BF16)],
        scratch_shapes=[pltpu.VMEM((tt + 8, d), F32)],
        compiler_params=_cparams(),
    )(x, gain, w_in_t.arr, w_conv, w_out.arr)


def _fwd_ffn(x, gain, w_in_t, w_conv, w_down, seq, name):
    t_all, d = x.shape
    f = w_down.n * N_DEV
    tt = _token_tile(seq) // 2
    tps = seq // tt

    def body(x_ref, g_ref, win_ref, wc_ref, wd_ref, x2_ref, gate_ref, s_ref, uds_ref, a_ref, ext_ref):
        i = pl.program_id(0)
        xv = x_ref[...]
        r, xh = _rms_parts(xv)
        h = (xh * g_ref[...]).astype(BF16)
        gu = _nt(h, _mat(win_ref))
        gate = gu[:, :f]
        u = gu[:, f:]
        gate_ref[...] = gate.astype(BF16)

        @pl.when(i % tps == 0)
        def _():
            ext_ref[0:8, :] = jnp.zeros((8, f), F32)

        gc = _causal_conv3(ext_ref, gate, wc_ref, tt)
        ext_ref[0:8, :] = ext_ref[tt:tt + 8, :]
        sig = _sigmoid(gc)
        s = gc * sig
        s_ref[...] = s.astype(BF16)
        uds_ref[...] = (u * (sig * (1.0 + gc * (1.0 - sig)))).astype(BF16)
        a = (s * u).astype(BF16)
        a_ref[...] = a
        x2_ref[...] = xv + _nn(a, _mat(wd_ref))

    wide = pl.BlockSpec((tt, f), lambda i: (i, 0))
    return pl.pallas_call(
        body, name=name, grid=(t_all // tt,),
        in_specs=[pl.BlockSpec((tt, d), lambda i: (i, 0)), _resident((1, d)), _rows_spec(w_in_t),
                  _resident((3, f)), _rows_spec(w_down)],
        out_specs=[pl.BlockSpec((tt, d), lambda i: (i, 0)), wide, wide, wide, wide],
        out_shape=[SDS((t_all, d), F32)] + [SDS((t_all, f), BF16)] * 4,
        scratch_shapes=[pltpu.VMEM((tt + 8, f), F32)],
        compiler_params=_cparams(),
    )(x, gain, w_in_t.arr, w_conv, w_down.arr)


def _rope_partner(xs, lane_lo):
    return jnp.where(lane_lo, pltpu.roll(xs, LANES - HEAD_DIM // 2, 1), pltpu.roll(xs, HEAD_DIM // 2, 1))


def _fwd_qkv(x, gain, w_qkv_t, b_qkv, cos_t, sin_t, seq, name):
    t_all, d = x.shape
    width = w_qkv_t.n * N_DEV
    kvw = (width - d) // 2
    tt = _token_tile(seq)
    tps = seq // tt
    scale = HEAD_DIM ** -0.5

    def body(x_ref, g_ref, w_ref, b_ref, cos_ref, sin_ref, qkv_ref):
        xv = x_ref[...]
        r, xh = _rms_parts(xv)
        h = (xh * g_ref[...]).astype(BF16)
        qkv = _nt(h, _mat(w_ref)) + b_ref[...]
        cosv = cos_ref[...]
        sinv = sin_ref[...]
        lane_lo = (lax.broadcasted_iota(jnp.int32, (tt, LANES), 1) % HEAD_DIM) < HEAD_DIM // 2
        for s in range((d + kvw) // LANES):
            xs = qkv[:, s * LANES:(s + 1) * LANES]
            roped = xs * cosv + _rope_partner(xs, lane_lo) * sinv
            if s * LANES < d:
                roped = roped * scale
            qkv_ref[:, s * LANES:(s + 1) * LANES] = roped.astype(BF16)
        qkv_ref[:, d + kvw:] = qkv[:, d + kvw:].astype(BF16)

    return pl.pallas_call(
        body, name=name, grid=(t_all // tt,),
        in_specs=[pl.BlockSpec((tt, d), lambda i: (i, 0)), _resident((1, d)), _rows_spec(w_qkv_t),
                  _resident((1, width)), pl.BlockSpec((tt, LANES), lambda i: (i % tps, 0)),
                  pl.BlockSpec((tt, LANES), lambda i: (i % tps, 0))],
        out_specs=pl.BlockSpec((tt, width), lambda i: (i, 0)),
        out_shape=SDS((t_all, width), BF16),
        compiler_params=_cparams(),
    )(x, gain, w_qkv_t.arr, b_qkv, cos_t, sin_t)


def _stack_heads(ref, row0, kh):
    return jnp.concatenate(
        [ref[row0:row0 + WINDOW, (kh * GROUP + g) * HEAD_DIM:(kh * GROUP + g + 1) * HEAD_DIM] for g in range(GROUP)],
        axis=0)


def _band_bias():
    r = lax.broadcasted_iota(jnp.int32, (WINDOW, 2 * WINDOW), 0)
    j = lax.broadcasted_iota(jnp.int32, (WINDOW, 2 * WINDOW), 1)
    base = (j > r) & (j <= r + WINDOW)
    return jnp.where(base, 0.0, MASKED_SCORE), jnp.where(base & (j >= WINDOW), 0.0, MASKED_SCORE)


def _softmax_with_sink(s, sink):
    m = jnp.maximum(jnp.max(s, axis=-1, keepdims=True), sink)
    p = jnp.exp(s - m)
    e_sink = jnp.exp(sink - m)
    inv = 1.0 / (jnp.sum(p, axis=-1, keepdims=True) + e_sink)
    return p * inv, e_sink * inv


def _fwd_attention(qkv, x, sinks, w_o, b_o, seq, name):
    t_all, d = x.shape
    width = qkv.shape[1]
    kvw = (width - d) // 2
    n_kv = kvw // HEAD_DIM
    tt = _token_tile(seq)
    tps = seq // tt
    nblk = tt // WINDOW

    def body(sink_ref, qkv_ref, kvp_ref, x_ref, wo_ref, bo_ref, x1_ref, o_ref, kvext_ref, oscr_ref, bias_ref, s_ref, p_ref):
        i = pl.program_id(0)

        @pl.when(i == 0)
        def _():
            bias_ref[0], bias_ref[1] = _band_bias()

        kvext_ref[0:WINDOW, :] = kvp_ref[...]
        kvext_ref[WINDOW:, :] = qkv_ref[:, d:]
        at_seq_start = (i % tps == 0).astype(jnp.int32)
        for n in range(nblk):
            for kh in range(n_kv):
                buf = (n * n_kv + kh) % 2
                qs = _stack_heads(qkv_ref, n * WINDOW, kh)
                kb = kvext_ref[n * WINDOW:(n + 2) * WINDOW, kh * HEAD_DIM:(kh + 1) * HEAD_DIM]
                vb = kvext_ref[n * WINDOW:(n + 2) * WINDOW, kvw + kh * HEAD_DIM:kvw + (kh + 1) * HEAD_DIM]
                s_ref[buf] = _nt(qs, kb)
                for r0 in range(0, GROUP * WINDOW, ROW_CHUNK):
                    q0 = r0 % WINDOW
                    bias = bias_ref[at_seq_start if n == 0 else 0, q0:q0 + ROW_CHUNK, :]
                    probs, _ = _softmax_with_sink(s_ref[buf, r0:r0 + ROW_CHUNK, :] + bias, sink_ref[kh * GROUP + r0 // WINDOW])
                    p_ref[buf, r0:r0 + ROW_CHUNK, :] = probs.astype(BF16)
                o_s = _nn(p_ref[buf], vb)
                for g in range(GROUP):
                    hd = kh * GROUP + g
                    oscr_ref[n * WINDOW:(n + 1) * WINDOW, hd * HEAD_DIM:(hd + 1) * HEAD_DIM] = (
                        o_s[g * WINDOW:(g + 1) * WINDOW].astype(BF16))
        o = oscr_ref[...]
        o_ref[...] = o
        x1_ref[...] = x_ref[...] + _nn(o, _mat(wo_ref)) + bo_ref[...]

    kv_blocks = tt // WINDOW
    return pl.pallas_call(
        body, name=name, grid=(t_all // tt,),
        in_specs=[pl.BlockSpec(memory_space=pltpu.SMEM),
                  pl.BlockSpec((tt, width), lambda i: (i, 0)),
                  pl.BlockSpec((WINDOW, 2 * kvw), lambda i: (jnp.maximum(i * kv_blocks - 1, 0), d // (2 * kvw))),
                  pl.BlockSpec((tt, d), lambda i: (i, 0)), _rows_spec(w_o), _resident((1, d))],
        out_specs=[pl.BlockSpec((tt, d), lambda i: (i, 0)), pl.BlockSpec((tt, d), lambda i: (i, 0))],
        out_shape=[SDS((t_all, d), F32), SDS((t_all, d), BF16)],
        scratch_shapes=[pltpu.VMEM((tt + WINDOW, 2 * kvw), BF16), pltpu.VMEM((tt, d), BF16),
                        pltpu.VMEM((2, WINDOW, 2 * WINDOW), F32), pltpu.VMEM((2, GROUP * WINDOW, 2 * WINDOW), F32),
                        pltpu.VMEM((2, GROUP * WINDOW, 2 * WINDOW), BF16)],
        compiler_params=_cparams(),
    )(sinks, qkv, qkv, x, w_o.arr, b_o)


def _final_norm_loss(x, gain, target, name):
    t_all, d = x.shape
    tt = min(TOKEN_TILE, t_all)

    def body(x_ref, g_ref, t_ref, dx_ref, dg_ref, loss_ref):
        i = pl.program_id(0)
        xv = x_ref[...]
        r, xh = _rms_parts(xv)
        gain_v = g_ref[...]
        e = xh * gain_v - t_ref[...]
        dy = e * (1.0 / d)
        dx_ref[...] = _rms_backward(dy, xh, r, gain_v, 0.0)

        @pl.when(i == 0)
        def _():
            dg_ref[...] = jnp.zeros_like(dg_ref)
            loss_ref[...] = jnp.zeros_like(loss_ref)

        dg_ref[...] += jnp.sum(dy * xh, axis=0, keepdims=True)
        loss_ref[...] += (0.5 / d) * jnp.sum(e * e, axis=0, keepdims=True)

    return pl.pallas_call(
        body, name=name, grid=(t_all // tt,),
        in_specs=[pl.BlockSpec((tt, d), lambda i: (i, 0)), _resident((1, d)), pl.BlockSpec((tt, d), lambda i: (i, 0))],
        out_specs=[pl.BlockSpec((tt, d), lambda i: (i, 0)), pl.BlockSpec((1, d), lambda i: (0, 0)),
                   pl.BlockSpec((1, d), lambda i: (0, 0))],
        out_shape=[SDS((t_all, d), F32), SDS((1, d), F32), SDS((1, d), F32)],
        compiler_params=_cparams(),
    )(x, gain, target)


def _bwd_ffn_inner(dx2, gate, s_act, uds, w_conv, w_down, seq, name):
    t_all, d = dx2.shape
    f = w_down.n * N_DEV
    tt = _token_tile(seq) // 2
    tps = seq // tt
    nt = t_all // tt

    def body(dx_ref, g_ref, s_ref, uds_ref, wc_ref, wd_ref, dgu_ref, dwc_ref, aext_ref):
        i = pl.program_id(0)
        ti = nt - 1 - i
        da = _nt(dx_ref[...].astype(BF16), _mat(wd_ref))
        g = g_ref[...].astype(F32)
        dgc = da * uds_ref[...].astype(F32)

        @pl.when(ti % tps == tps - 1)
        def _():
            aext_ref[tt:tt + 8, :] = jnp.zeros((8, f), F32)

        dg, sh1, sh2 = _anticausal_conv3(aext_ref, dgc, wc_ref, tt)
        aext_ref[tt:tt + 8, :] = aext_ref[0:8, :]
        dgu_ref[:, :f] = dg.astype(BF16)
        dgu_ref[:, f:] = (da * s_ref[...].astype(F32)).astype(BF16)

        @pl.when(i == 0)
        def _():
            dwc_ref[...] = jnp.zeros_like(dwc_ref)

        dwc_ref[0:1, :] += jnp.sum(g * sh2, axis=0, keepdims=True)
        dwc_ref[1:2, :] += jnp.sum(g * sh1, axis=0, keepdims=True)
        dwc_ref[2:3, :] += jnp.sum(g * dgc, axis=0, keepdims=True)

    rev = lambda i: (nt - 1 - i, 0)
    return pl.pallas_call(
        body, name=name, grid=(nt,),
        in_specs=[pl.BlockSpec((tt, d), rev)] + [pl.BlockSpec((tt, f), rev)] * 3 + [_resident((3, f)), _rows_spec(w_down)],
        out_specs=[pl.BlockSpec((tt, 2 * f), rev), pl.BlockSpec((8, f), lambda i: (0, 0))],
        out_shape=[SDS((t_all, 2 * f), BF16), SDS((8, f), F32)],
        scratch_shapes=[pltpu.VMEM((tt + 8, f), F32)],
        compiler_params=_cparams(),
    )(dx2, gate, s_act, uds, w_conv, w_down.arr)


def _bwd_conv_inner(dx1, bcv, cc, w_conv, w_out, seq, name):
    t_all, d = dx1.shape
    tt = _token_tile(seq)
    tps = seq // tt
    nt = t_all // tt

    def body(dx_ref, bcv_ref, cc_ref, wc_ref, wout_ref, dbcv_ref, dwc_ref, aext_ref):
        i = pl.program_id(0)
        ti = nt - 1 - i
        dy = _nt(dx_ref[...].astype(BF16), _mat(wout_ref))
        bcv_v = bcv_ref[...].astype(F32)
        b = bcv_v[:, :d]
        c = bcv_v[:, d:2 * d]
        v = bcv_v[:, 2 * d:]
        cv = c * v
        dcc = dy * b

        @pl.when(ti % tps == tps - 1)
        def _():
            aext_ref[tt:tt + 8, :] = jnp.zeros((8, d), F32)

        dcv, sh1, sh2 = _anticausal_conv3(aext_ref, dcc, wc_ref, tt)
        aext_ref[tt:tt + 8, :] = aext_ref[0:8, :]
        dbcv_ref[:, :d] = (dy * cc_ref[...].astype(F32)).astype(BF16)
        dbcv_ref[:, d:2 * d] = (dcv * v).astype(BF16)
        dbcv_ref[:, 2 * d:] = (dcv * c).astype(BF16)

        @pl.when(i == 0)
        def _():
            dwc_ref[...] = jnp.zeros_like(dwc_ref)

        dwc_ref[0:1, :] += jnp.sum(cv * sh2, axis=0, keepdims=True)
        dwc_ref[1:2, :] += jnp.sum(cv * sh1, axis=0, keepdims=True)
        dwc_ref[2:3, :] += jnp.sum(cv * dcc, axis=0, keepdims=True)

    return pl.pallas_call(
        body, name=name, grid=(nt,),
        in_specs=[pl.BlockSpec((tt, d), lambda i: (nt - 1 - i, 0)),
                  pl.BlockSpec((tt, 3 * d), lambda i: (nt - 1 - i, 0)),
                  pl.BlockSpec((tt, d), lambda i: (nt - 1 - i, 0)),
                  _resident((3, d)), _rows_spec(w_out)],
        out_specs=[pl.BlockSpec((tt, 3 * d), lambda i: (nt - 1 - i, 0)), pl.BlockSpec((8, d), lambda i: (0, 0))],
        out_shape=[SDS((t_all, 3 * d), BF16), SDS((8, d), F32)],
        scratch_shapes=[pltpu.VMEM((tt + 8, d), F32)],
        compiler_params=_cparams(),
    )(dx1, bcv, cc, w_conv, w_out.arr)


def _bwd_attention_inner(dx1, qkv, sinks, w_o, cos_t, sin_t, seq, name):
    t_all, d = dx1.shape
    width = qkv.shape[1]
    kvw = (width - d) // 2
    n_kv = kvw // HEAD_DIM
    tt = _token_tile(seq)
    tps = seq // tt
    nt = t_all // tt
    nblk = tt // WINDOW
    scale = HEAD_DIM ** -0.5

    def body(sink_ref, dx_ref, qkv_ref, kvp_ref, cos_ref, sin_ref, wo_ref,
             dqkv_ref, dsink_ref, dbqkv_ref, dbo_ref,
             kvext_ref, dkvext_ref, carry_ref, dq_ref, do_ref, bias_ref, s_ref, dp_ref, p_ref, ds_ref):
        i = pl.program_id(0)
        ti = nt - 1 - i
        dxv = dx_ref[...]
        do_ref[...] = _nt(dxv.astype(BF16), _mat(wo_ref)).astype(BF16)
        kvext_ref[0:WINDOW, :] = kvp_ref[...]
        kvext_ref[WINDOW:, :] = qkv_ref[:, d:]
        dkvext_ref[...] = jnp.zeros_like(dkvext_ref)

        @pl.when(i == 0)
        def _():
            bias_ref[0], bias_ref[1] = _band_bias()
            carry_ref[...] = jnp.zeros_like(carry_ref)
            dsink_ref[...] = jnp.zeros_like(dsink_ref)
            dbqkv_ref[...] = jnp.zeros_like(dbqkv_ref)
            dbo_ref[...] = jnp.zeros_like(dbo_ref)

        at_seq_start = (ti % tps == 0).astype(jnp.int32)
        head_lane = lax.broadcasted_iota(jnp.int32, (1, LANES), 1)
        dsink = jnp.zeros((1, LANES), F32)
        for n in range(nblk):
            for kh in range(n_kv):
                buf = (n * n_kv + kh) % 2
                qs = _stack_heads(qkv_ref, n * WINDOW, kh)
                dos = _stack_heads(do_ref, n * WINDOW, kh)
                kcols = slice(kh * HEAD_DIM, (kh + 1) * HEAD_DIM)
                vcols = slice(kvw + kh * HEAD_DIM, kvw + (kh + 1) * HEAD_DIM)
                band = slice(n * WINDOW, (n + 2) * WINDOW)
                kb = kvext_ref[band, kcols]
                vb = kvext_ref[band, vcols]
                s_ref[buf] = _nt(qs, kb)
                dp_ref[buf] = _nt(dos, vb)
                sink_terms = [0.0] * GROUP
                for r0 in range(0, GROUP * WINDOW, ROW_CHUNK):
                    q0 = r0 % WINDOW
                    rows = slice(r0, r0 + ROW_CHUNK)
                    bias = bias_ref[at_seq_start if n == 0 else 0, q0:q0 + ROW_CHUNK, :]
                    probs, p_sink = _softmax_with_sink(s_ref[buf, rows, :] + bias, sink_ref[kh * GROUP + r0 // WINDOW])
                    dp = dp_ref[buf, rows, :]
                    dsum = jnp.sum(probs * dp, axis=-1, keepdims=True)
                    p_ref[buf, rows, :] = probs.astype(BF16)
                    ds_ref[buf, rows, :] = (probs * (dp - dsum)).astype(BF16)
                    sink_terms[r0 // WINDOW] = sink_terms[r0 // WINDOW] + p_sink * dsum
                ds = ds_ref[buf]
                dkvext_ref[band, vcols] += _tn(p_ref[buf], dos)
                dkvext_ref[band, kcols] += _tn(ds, qs)
                dq_s = _nn(ds, kb)
                for g in range(GROUP):
                    hd = kh * GROUP + g
                    dq_ref[n * WINDOW:(n + 1) * WINDOW, hd * HEAD_DIM:(hd + 1) * HEAD_DIM] = dq_s[g * WINDOW:(g + 1) * WINDOW]
                    dsink = dsink - jnp.where(head_lane == hd, jnp.sum(sink_terms[g]), 0.0)
        dsink_ref[0:1, :] += dsink
        dkvext_ref[tt:tt + WINDOW, :] += carry_ref[...]
        carry_ref[...] = dkvext_ref[0:WINDOW, :]

        cosv = cos_ref[...]
        sinv = sin_ref[...]
        lane_lo = (lax.broadcasted_iota(jnp.int32, (tt, LANES), 1) % HEAD_DIM) < HEAD_DIM // 2
        for s in range((d + kvw) // LANES):
            if s * LANES < d:
                dy = dq_ref[:, s * LANES:(s + 1) * LANES] * scale
            else:
                dy = dkvext_ref[WINDOW:, s * LANES - d:(s + 1) * LANES - d]
            dpre = dy * cosv - _rope_partner(dy, lane_lo) * sinv
            dqkv_ref[:, s * LANES:(s + 1) * LANES] = dpre.astype(BF16)
            dbqkv_ref[0:1, s * LANES:(s + 1) * LANES] += jnp.sum(dpre, axis=0, keepdims=True)
        dv = dkvext_ref[WINDOW:, kvw:]
        dqkv_ref[:, d + kvw:] = dv.astype(BF16)
        dbqkv_ref[0:1, d + kvw:] += jnp.sum(dv, axis=0, keepdims=True)
        dbo_ref[...] += jnp.sum(dxv, axis=0, keepdims=True)

    kv_blocks = tt // WINDOW
    return pl.pallas_call(
        body, name=name, grid=(nt,),
        in_specs=[pl.BlockSpec(memory_space=pltpu.SMEM),
                  pl.BlockSpec((tt, d), lambda i: (nt - 1 - i, 0)),
                  pl.BlockSpec((tt, width), lambda i: (nt - 1 - i, 0)),
                  pl.BlockSpec((WINDOW, 2 * kvw), lambda i: (jnp.maximum((nt - 1 - i) * kv_blocks - 1, 0), d // (2 * kvw))),
                  pl.BlockSpec((tt, LANES), lambda i: ((nt - 1 - i) % tps, 0)),
                  pl.BlockSpec((tt, LANES), lambda i: ((nt - 1 - i) % tps, 0)),
                  _rows_spec(w_o)],
        out_specs=[pl.BlockSpec((tt, width), lambda i: (nt - 1 - i, 0)), pl.BlockSpec((8, LANES), lambda i: (0, 0)),
                   pl.BlockSpec((1, width), lambda i: (0, 0)), pl.BlockSpec((1, d), lambda i: (0, 0))],
        out_shape=[SDS((t_all, width), BF16), SDS((8, LANES), F32), SDS((1, width), F32), SDS((1, d), F32)],
        scratch_shapes=[pltpu.VMEM((tt + WINDOW, 2 * kvw), BF16), pltpu.VMEM((tt + WINDOW, 2 * kvw), F32),
                        pltpu.VMEM((WINDOW, 2 * kvw), F32), pltpu.VMEM((tt, d), F32), pltpu.VMEM((tt, d), BF16),
                        pltpu.VMEM((2, WINDOW, 2 * WINDOW), F32), pltpu.VMEM((2, GROUP * WINDOW, 2 * WINDOW), F32),
                        pltpu.VMEM((2, GROUP * WINDOW, 2 * WINDOW), F32), pltpu.VMEM((2, GROUP * WINDOW, 2 * WINDOW), BF16),
                        pltpu.VMEM((2, GROUP * WINDOW, 2 * WINDOW), BF16)],
        compiler_params=_cparams(),
    )(sinks, dx1, qkv, qkv, cos_t, sin_t, w_o.arr)


def _bwd_dense_norm(dy, w_t, x, gain, dres, name):
    t_all, d = x.shape
    n = dy.shape[1]
    tt = min(TOKEN_TILE, t_all)

    def body(dy_ref, w_ref, x_ref, g_ref, dres_ref, dx_ref, h_ref, dg_ref):
        i = pl.program_id(0)
        dh = _nn(dy_ref[...], _mat(w_ref))
        r, xh = _rms_parts(x_ref[...])
        gain_v = g_ref[...]
        h_ref[...] = (xh * gain_v).astype(BF16)
        dx_ref[...] = _rms_backward(dh, xh, r, gain_v, dres_ref[...])

        @pl.when(i == 0)
        def _():
            dg_ref[...] = jnp.zeros_like(dg_ref)

        dg_ref[...] += jnp.sum(dh * xh, axis=0, keepdims=True)

    return pl.pallas_call(
        body, name=name, grid=(t_all // tt,),
        in_specs=[pl.BlockSpec((tt, n), lambda i: (i, 0)), _rows_spec(w_t), pl.BlockSpec((tt, d), lambda i: (i, 0)),
                  _resident((1, d)), pl.BlockSpec((tt, d), lambda i: (i, 0))],
        out_specs=[pl.BlockSpec((tt, d), lambda i: (i, 0)), pl.BlockSpec((tt, d), lambda i: (i, 0)),
                   pl.BlockSpec((1, d), lambda i: (0, 0))],
        out_shape=[SDS((t_all, d), F32), SDS((t_all, d), BF16), SDS((1, d), F32)],
        compiler_params=_cparams(),
    )(dy, w_t.arr, x, gain, dres)


def _tn_matmul(a, b, dest, name):
    t_all, m = a.shape
    d = b.shape[1]
    n = dest.n
    assert m == N_DEV * n and dest.off % n == 0
    k = max(kk for kk in (1, 2, 4, 8) if kk * n <= max(n, 1536))
    tm = k * n
    tt = min(TN_TOKEN_TILE, t_all)
    n_t = t_all // tt
    fresh = not hasattr(dest.arr, "dtype")

    def body(a_ref, b_ref, *rest):
        o_ref, acc_ref = rest[-2:]
        t = pl.program_id(1)

        @pl.when(t == 0)
        def _():
            acc_ref[...] = jnp.zeros_like(acc_ref)

        acc_ref[...] += _tn(a_ref[...], b_ref[...].astype(BF16))

        @pl.when(t == n_t - 1)
        def _():
            o_ref[...] = acc_ref[...].astype(BF16).reshape(k, n, d)

    block = dest.off // n
    return pl.pallas_call(
        body, name=name, grid=(m // tm, n_t),
        in_specs=[pl.BlockSpec((tt, tm), lambda j, t: (t, j)), pl.BlockSpec((tt, d), lambda j, t: (t, 0))] + ([] if fresh else [ANY]),
        out_specs=pl.BlockSpec((k, n, d), lambda j, t: (j, block, 0)),
        out_shape=SDS(tuple(dest.arr) if fresh else dest.arr.shape, BF16),
        scratch_shapes=[pltpu.VMEM((tm, d), F32)],
        input_output_aliases={} if fresh else {2: 0},
        compiler_params=_cparams(2),
    )(*((a, b) if fresh else (a, b, dest.arr)))


def _my_place():
    return lax.axis_index("x"), lax.axis_index("y"), lax.axis_index("c")


def _other_chips(x, y):
    return [(1 - x, y), (x, 1 - y), (1 - x, 1 - y)]


def _all_gather(blocks, name):
    n_arr = len(blocks)

    def body(*refs):
        in_refs = refs[:n_arr]
        out_refs = refs[n_arr:2 * n_arr]
        send_sems, recv_sems, local_sems = refs[2 * n_arr:]
        x, y, c = _my_place()
        me, sibling = (x, y, c), (x, y, 1 - c)
        chips = _other_chips(x, y)

        def slot(a, place):
            px, py, pc = place
            return out_refs[a].at[4 * px + 2 * py + pc]

        def copy(a, k, block, to, src=None):
            return pltpu.make_async_remote_copy(
                src_ref=slot(a, block) if src is None else src, dst_ref=slot(a, block),
                send_sem=send_sems.at[a, k], recv_sem=recv_sems.at[a, k], device_id=to, device_id_type=MESH)

        started = []
        local = []
        for a in range(n_arr):
            mine = pltpu.make_async_copy(in_refs[a], slot(a, me), local_sems.at[a])
            mine.start()
            local.append(mine)
            first = [copy(a, 0, me, sibling, src=in_refs[a])]
            first += [copy(a, 1 + j, me, (*chip, c), src=in_refs[a]) for j, chip in enumerate(chips)]
            for cp in first:
                cp.start()
            started += first
        for a in range(n_arr):
            for j, chip in enumerate(chips):
                copy(a, 1 + j, (*chip, c), me).wait_recv()
                passed = copy(a, 4 + j, (*chip, c), sibling)
                passed.start()
                started.append(passed)
        for a in range(n_arr):
            copy(a, 0, sibling, me).wait_recv()
            for j, chip in enumerate(chips):
                copy(a, 4 + j, (*chip, 1 - c), me).wait_recv()
        for cp in started:
            cp.wait_send()
        for mine in local:
            mine.wait()

    return pl.pallas_call(
        body, name=name,
        in_specs=[ANY] * n_arr, out_specs=[ANY] * n_arr,
        out_shape=[SDS((N_DEV,) + b.shape, b.dtype) for b in blocks],
        scratch_shapes=[pltpu.SemaphoreType.DMA((n_arr, 7)), pltpu.SemaphoreType.DMA((n_arr, 7)),
                        pltpu.SemaphoreType.DMA((n_arr,))],
    )(*blocks)


def _peer_of(k, x, y, c):
    return x ^ ((k >> 2) & 1), y ^ ((k >> 1) & 1), c ^ (k & 1)


HBM = pl.BlockSpec(memory_space=pltpu.HBM)
SEM = pl.BlockSpec(memory_space=pltpu.SEMAPHORE)
DATAFLOW_EFFECT = pltpu.SideEffectType.DATAFLOW_SIDE_EFFECTING


def _peer_copies(src_ref, land_ref, send_sems, recv_sems, per_peer):
    x, y, c = _my_place()
    me = 4 * x + 2 * y + c
    copies = []
    for k in range(1, N_DEV):
        px, py, pc = _peer_of(k, x, y, c)
        peer = 4 * px + 2 * py + pc
        copies.append(pltpu.make_async_remote_copy(
            src_ref=src_ref.at[peer] if per_peer else src_ref, dst_ref=land_ref.at[me],
            send_sem=send_sems.at[k - 1], recv_sem=recv_sems.at[k - 1], device_id=(px, py, pc), device_id_type=MESH))
    own = pltpu.make_async_copy(src_ref.at[me] if per_peer else src_ref, land_ref.at[me], send_sems.at[N_DEV - 1])
    return copies, own


def _exchange_start(src, after, per_peer, name):
    rows, d = src.shape[-2:]

    def body(src_ref, land_ref, after_ref, send_sems, recv_sems, src_thru, land_thru, token):
        copies, own = _peer_copies(src_ref, land_ref, send_sems, recv_sems, per_peer)
        for cp in copies:
            cp.start()
        own.start()
        token[...] = jnp.zeros_like(token)

    return pl.pallas_call(
        body, name=name,
        out_shape=(pltpu.SemaphoreType.DMA((N_DEV,)), pltpu.SemaphoreType.DMA((N_DEV - 1,)), pltpu.HBM(src.shape, src.dtype),
                   pltpu.HBM((N_DEV, rows, d), src.dtype), SDS((SUBLANES, LANES), F32)),
        in_specs=(HBM, HBM, ANY), out_specs=(SEM, SEM, HBM, HBM, pl.BlockSpec(memory_space=pltpu.VMEM)),
        input_output_aliases={0: 2, 1: 3},
        compiler_params=pltpu.CompilerParams(has_side_effects=DATAFLOW_EFFECT),
    )(pltpu.with_memory_space_constraint(src, pltpu.HBM),
      pltpu.with_memory_space_constraint(lax.empty((N_DEV, rows, d), src.dtype), pltpu.HBM), after)


def _exchange_wait(started, after, per_peer, name):
    send_sems, recv_sems, src_thru, land_thru, _ = started

    def body(src_ref, land_ref, send_sems, recv_sems, after_ref, src_out, land_out):
        copies, own = _peer_copies(src_ref, land_ref, send_sems, recv_sems, per_peer)
        for cp in copies:
            cp.wait_send()
            cp.wait_recv()
        own.wait()

    return pl.pallas_call(
        body, name=name,
        out_shape=(pltpu.HBM(src_thru.shape, src_thru.dtype), pltpu.HBM(land_thru.shape, land_thru.dtype)),
        in_specs=(HBM, HBM, SEM, SEM, ANY), out_specs=(HBM, HBM), input_output_aliases={0: 0, 1: 1},
        compiler_params=pltpu.CompilerParams(has_side_effects=DATAFLOW_EFFECT),
    )(src_thru, land_thru, send_sems, recv_sems, after)


def _sum_slots(slots, name):
    _, rows, d = slots.shape
    tr = _largest_divisor(rows, 512, 16)

    def body(s_ref, o_ref):
        acc = s_ref[0].astype(F32)
        for dev in range(1, N_DEV):
            acc = acc + s_ref[dev].astype(F32)
        o_ref[...] = acc

    return pl.pallas_call(
        body, name=name, grid=(rows // tr,),
        in_specs=[pl.BlockSpec((N_DEV, tr, d), lambda r: (0, r, 0))], out_specs=pl.BlockSpec((tr, d), lambda r: (r, 0)),
        out_shape=SDS((rows, d), F32), compiler_params=_cparams(),
    )(slots)


def _all_reduce_small(part, loss_rows, name):
    rows, lanes = part.shape
    lo, hi = loss_rows

    def body(x_ref, out_ref, loss_ref, gath_ref, send_sems, recv_sems):
        x, y, c = _my_place()
        me = 4 * x + 2 * y + c
        gath_ref[me] = x_ref[...]
        copies = []
        for k in range(1, N_DEV):
            peer = (x ^ ((k >> 2) & 1), y ^ ((k >> 1) & 1), c ^ (k & 1))
            cp = pltpu.make_async_remote_copy(
                src_ref=x_ref, dst_ref=gath_ref.at[me], send_sem=send_sems.at[k - 1], recv_sem=recv_sems.at[k - 1],
                device_id=peer, device_id_type=MESH)
            cp.start()
            copies.append(cp)
        for cp in copies:
            cp.wait_recv()
        for cp in copies:
            cp.wait_send()
        acc = gath_ref[0]
        for dev in range(1, N_DEV):
            acc = acc + gath_ref[dev]
        out_ref[...] = acc
        loss_ref[...] = jnp.full(loss_ref.shape, jnp.sum(acc[lo:hi, :]), F32)

    vmem = pl.BlockSpec(memory_space=pltpu.VMEM)
    return pl.pallas_call(
        body, name=name, in_specs=[vmem], out_specs=[vmem, vmem],
        out_shape=[SDS((rows, lanes), F32), SDS((SUBLANES, LANES), F32)],
        scratch_shapes=[pltpu.VMEM((N_DEV, rows, lanes), F32), pltpu.SemaphoreType.DMA((N_DEV - 1,)),
                        pltpu.SemaphoreType.DMA((N_DEV - 1,))],
    )(part)


def _adamw(w, g, m, v, name):
    rows, cols = w.shape
    tr = rows if rows % SUBLANES else _largest_divisor(rows, 512, SUBLANES)

    def body(w_ref, g_ref, m_ref, v_ref, d_ref, nm_ref, nv_ref):
        gv = g_ref[...]
        nm = ADAM_B1 * m_ref[...] + (1.0 - ADAM_B1) * gv
        nv = ADAM_B2 * v_ref[...] + (1.0 - ADAM_B2) * (gv * gv)
        m_hat = nm / (1.0 - ADAM_B1 ** ADAM_STEP)
        v_hat = nv / (1.0 - ADAM_B2 ** ADAM_STEP)
        d_ref[...] = -ADAM_LR * (m_hat / (jnp.sqrt(v_hat) + ADAM_EPS) + ADAM_WD * w_ref[...])
        nm_ref[...] = nm
        nv_ref[...] = nv

    spec = pl.BlockSpec((tr, cols), lambda i: (i, 0))
    return pl.pallas_call(
        body, name=name, grid=(rows // tr,), in_specs=[spec] * 4, out_specs=[spec] * 3,
        out_shape=[SDS((rows, cols), F32)] * 3, compiler_params=_cparams(),
    )(w, g, m, v)


def _adamw_nd(w, g, m, v, name):
    shape = w.shape
    two_d = (1, shape[0]) if len(shape) == 1 else (-1, shape[-1])
    outs = _adamw(w.reshape(two_d), g.reshape(two_d), m.reshape(two_d), v.reshape(two_d), name)
    return [o.reshape(shape) for o in outs]


def _rope_tables(seq):
    pos = jnp.arange(seq, dtype=F32)
    inv_freq = 1.0 / (ROPE_THETA ** (jnp.arange(0, HEAD_DIM, 2, dtype=F32) / HEAD_DIM))
    ang = pos[:, None] * inv_freq[None, :]
    cos, sin = jnp.cos(ang), jnp.sin(ang)
    reps = LANES // HEAD_DIM
    cos_t = jnp.tile(jnp.concatenate([cos, cos], axis=1), (1, reps))
    sin_t = jnp.tile(jnp.concatenate([-sin, sin], axis=1), (1, reps))
    return cos_t, sin_t


def _flat_pad(a):
    flat = a.reshape(1, -1)
    pad = (-flat.shape[1]) % LANES
    return jnp.pad(flat, ((0, 0), (0, pad))) if pad else flat


def kernel(x, norm_mix, norm_ffn, norm_final, conv_w_in, conv_w_conv, conv_w_out, attn_w_qkv, attn_b_qkv, attn_sinks, attn_w_o, attn_b_o, ffn_w_in, ffn_w_conv, ffn_w_down, loss_target, m_norm_mix, m_norm_ffn, m_norm_final, m_conv_w_in, m_conv_w_conv, m_conv_w_out, m_attn_w_qkv, m_attn_b_qkv, m_attn_sinks, m_attn_w_o, m_attn_b_o, m_ffn_w_in, m_ffn_w_conv, m_ffn_w_down, v_norm_mix, v_norm_ffn, v_norm_final, v_conv_w_in, v_conv_w_conv, v_conv_w_out, v_attn_w_qkv, v_attn_b_qkv, v_attn_sinks, v_attn_w_o, v_attn_b_o, v_ffn_w_in, v_ffn_w_conv, v_ffn_w_down):
    b_loc, seq, d = x.shape
    depth = norm_mix.shape[0]
    n_conv, n_attn = conv_w_in.shape[0], attn_w_qkv.shape[0]
    t_all = b_loc * seq
    my_x, my_y, my_c = _my_place()

    me = 4 * my_x + 2 * my_y + my_c

    groups = []
    for i in range(depth):
        j = i // 2
        if i % 2 == 0:
            mix = [("conv_w_in", j, True, conv_w_in[j].T), ("conv_w_out", j, False, conv_w_out[j])]
        else:
            mix = [("attn_w_qkv", j, True, attn_w_qkv[j].T), ("attn_w_o", j, False, attn_w_o[j])]
        groups.append((("mix", i), mix))
        groups.append((("ffn", i), [("ffn_w_in", i, True, ffn_w_in[i].T), ("ffn_w_down", i, False, ffn_w_down[i])]))
    order = [key for key, _ in groups]
    members_of = dict(groups)

    def layout(key):
        offs, o = [], 0
        for _, _, _, shard in members_of[key]:
            n = shard.shape[0]
            o = -(-o // n) * n
            offs.append(o)
            o += n
        return offs, o

    small = jnp.concatenate([_flat_pad(conv_w_conv), _flat_pad(ffn_w_conv), _flat_pad(attn_b_qkv), _flat_pad(attn_b_o)], axis=1)
    (small_g,) = _all_gather([small], "gather_small")

    gather_started = {}

    def start_gather(idx, after):
        if idx >= len(order):
            return 0.0
        key = order[idx]
        offs, total = layout(key)
        pieces, o = [], 0
        for (_, _, _, shard), off in zip(members_of[key], offs):
            if off > o:
                pieces.append(jnp.zeros((off - o, d), shard.dtype))
            pieces.append(shard)
            o = off + shard.shape[0]
        pack = jnp.concatenate(pieces, axis=0).astype(BF16)
        gather_started[key] = _exchange_start(pack, after, False, f"gather_start_{key[0]}_{key[1]}")
        return gather_started[key][4][0, 0]

    weights = {}

    def finish_gather(key, after):
        _, land = _exchange_wait(gather_started[key], after, False, f"gather_wait_{key[0]}_{key[1]}")
        for (wname, layer, _, shard), off in zip(members_of[key], layout(key)[0]):
            weights[(wname, layer)] = _Rows(land, off, shard.shape[0])

    def take_small(o, shape):
        size = shape[0] * shape[1] * shape[2]
        blk = small_g[:, 0, o:o + size].reshape((N_DEV,) + shape)
        return jnp.moveaxis(blk, 0, 2).reshape(shape[0], shape[1], N_DEV * shape[2])

    so = 0
    wc_conv_full = take_small(so, conv_w_conv.shape); so += _flat_pad(conv_w_conv).shape[1]
    wc_ffn_full = take_small(so, ffn_w_conv.shape); so += _flat_pad(ffn_w_conv).shape[1]
    b_qkv_full = take_small(so, (n_attn, 1, attn_b_qkv.shape[1]))[:, 0]; so += _flat_pad(attn_b_qkv).shape[1]
    b_o_full = take_small(so, (n_attn, 1, attn_b_o.shape[1]))[:, 0]

    cos_t, sin_t = _rope_tables(seq)

    xs = [x.reshape(t_all, d)]
    saved = []
    token = start_gather(0, small_g) + start_gather(1, small_g)
    for i in range(depth):
        j = i // 2
        if i > 0:
            token = start_gather(2 * i + 2, xs[-1])
        gain_mix = norm_mix[i][None, :] + token
        finish_gather(("mix", i), gain_mix if i == 0 else xs[-1])
        if i % 2 == 0:
            x1, *mix_saved = _fwd_conv_mixer(xs[-1], gain_mix, weights[("conv_w_in", j)], wc_conv_full[j],
                                             weights[("conv_w_out", j)], seq, f"fwd_conv_{i}")
        else:
            qkv = _fwd_qkv(xs[-1], gain_mix, weights[("attn_w_qkv", j)], b_qkv_full[j][None, :], cos_t, sin_t, seq,
                           f"fwd_qkv_{i}")
            x1, o = _fwd_attention(qkv, xs[-1], attn_sinks[j], weights[("attn_w_o", j)], b_o_full[j][None, :], seq,
                                   f"fwd_attn_{i}")
            mix_saved = (qkv, o)
        token = start_gather(2 * i + 3, x1) + (start_gather(2, x1) if i == 0 else 0.0)
        gain_ffn = norm_ffn[i][None, :] + token
        finish_gather(("ffn", i), gain_ffn)
        x2, *ffn_saved = _fwd_ffn(x1, gain_ffn, weights[("ffn_w_in", i)], wc_ffn_full[i], weights[("ffn_w_down", i)],
                                  seq, f"fwd_ffn_{i}")
        saved.append((xs[-1], x1, mix_saved, ffn_saved))
        xs.append(x2)
        token = 0.0

    dx, dg_final, loss_lanes = _final_norm_loss(xs[-1], norm_final[None, :], loss_target.reshape(t_all, d), "loss_head")

    dg_mix, dg_ffn = [None] * depth, [None] * depth
    dwc_conv, dwc_ffn = [None] * n_conv, [None] * depth
    db_qkv, db_o, dsinks = [None] * n_attn, [None] * n_attn, [None] * n_attn
    scatter_started = {}

    def weight_grads(key, operands):
        offs, total = layout(key)
        parts = (N_DEV, total, d)
        for (wname, layer, _, shard), off, (a, b) in zip(members_of[key], offs, operands):
            parts = _tn_matmul(a, b, _Rows(parts, off, shard.shape[0]), f"dw_{wname}_{layer}")
        scatter_started[key] = _exchange_start(parts, operands[0][1], True, f"scatter_start_{key[0]}_{key[1]}")
        return scatter_started[key][4][0, 0]

    token = 0.0
    for i in reversed(range(depth)):
        j = i // 2
        x0, x1, mix_saved, (gate, s_act, uds, act) = saved[i]
        dgu, dwc = _bwd_ffn_inner(dx, gate, s_act, uds, wc_ffn_full[i] + token, weights[("ffn_w_down", i)], seq, f"bwd_ffn_{i}")
        dwc_ffn[i] = dwc[:3]
        dx1, h2, dg_ffn[i] = _bwd_dense_norm(dgu, weights[("ffn_w_in", i)], x1, norm_ffn[i][None, :], dx, f"bwd_ffn_norm_{i}")
        token = weight_grads(("ffn", i), [(dgu, h2), (act, dx)])
        if i % 2 == 0:
            bcv, cc, y = mix_saved
            dbcv, dwc = _bwd_conv_inner(dx1, bcv, cc, wc_conv_full[j] + token, weights[("conv_w_out", j)], seq, f"bwd_conv_{i}")
            dwc_conv[j] = dwc[:3]
            dx, h, dg_mix[i] = _bwd_dense_norm(dbcv, weights[("conv_w_in", j)], x0, norm_mix[i][None, :], dx1,
                                               f"bwd_conv_norm_{i}")
            token = weight_grads(("mix", i), [(dbcv, h), (y, dx1)])
        else:
            qkv, o = mix_saved
            dqkv, dsk, dbq, dbo = _bwd_attention_inner(dx1, qkv, attn_sinks[j] + token, weights[("attn_w_o", j)], cos_t, sin_t,
                                                       seq, f"bwd_attn_{i}")
            dsinks[j], db_qkv[j], db_o[j] = dsk[0:1, :attn_sinks.shape[1]], dbq, dbo
            dx, h, dg_mix[i] = _bwd_dense_norm(dqkv, weights[("attn_w_qkv", j)], x0, norm_mix[i][None, :], dx1,
                                               f"bwd_attn_norm_{i}")
            token = weight_grads(("mix", i), [(dqkv, h), (o, dx1)])
    grad_x = dx.reshape(b_loc, seq, d)

    reduced = {}

    def finish_scatter(key, after):
        _, land = _exchange_wait(scatter_started[key], after, True, f"scatter_wait_{key[0]}_{key[1]}")
        total = _sum_slots(land, f"scatter_sum_{key[0]}_{key[1]}")
        for (wname, layer, transposed, shard), off in zip(members_of[key], layout(key)[0]):
            rows = total[off:off + shard.shape[0]]
            reduced[(wname, layer)] = rows.T if transposed else rows

    last_key = order[0]
    for key in reversed(order[1:]):
        finish_scatter(key, dx)

    small_parts = [jnp.concatenate(dg_mix, axis=0), jnp.concatenate(dg_ffn, axis=0), dg_final,
                   jnp.stack(dwc_conv), jnp.stack(dwc_ffn), jnp.concatenate(db_qkv, axis=0), jnp.concatenate(db_o, axis=0),
                   jnp.concatenate(dsinks, axis=0), loss_lanes]
    flats = [_flat_pad(p) for p in small_parts]
    bounds = []
    so = 0
    for fl in flats:
        bounds.append((so, so + fl.shape[1]))
        so += fl.shape[1]
    small_rows = so // LANES
    pad_rows = (-small_rows) % SUBLANES
    part_small = jnp.pad(jnp.concatenate(flats, axis=1).reshape(small_rows, LANES), ((0, pad_rows), (0, 0)))
    loss_rows = (bounds[-1][0] // LANES, bounds[-1][1] // LANES)
    summed, loss_tile = _all_reduce_small(part_small, loss_rows, "reduce_small")
    summed = summed.reshape(1, -1)

    def small_grad(k, shape):
        lo = bounds[k][0]
        size = 1
        for s_ in shape:
            size *= s_
        return summed[0, lo:lo + size].reshape(shape)

    def my_cols(full, n_local):
        return lax.dynamic_slice_in_dim(full, me * n_local, n_local, axis=full.ndim - 1)

    g_norm_mix = small_grad(0, norm_mix.shape)
    g_norm_ffn = small_grad(1, norm_ffn.shape)
    g_norm_final = small_grad(2, norm_final.shape)
    g_conv_w_conv = my_cols(small_grad(3, (n_conv, 3, d)), conv_w_conv.shape[2])
    g_ffn_w_conv = my_cols(small_grad(4, (depth, 3, ffn_w_conv.shape[2] * N_DEV)), ffn_w_conv.shape[2])
    g_attn_b_qkv = my_cols(small_grad(5, (n_attn, attn_b_qkv.shape[1] * N_DEV)), attn_b_qkv.shape[1])
    g_attn_b_o = my_cols(small_grad(6, (n_attn, d)), attn_b_o.shape[1])
    g_attn_sinks = small_grad(7, attn_sinks.shape)
    loss = loss_tile[0, 0]

    def big_grad(wname, n_layers):
        return jnp.stack([reduced[(wname, layer)] for layer in range(n_layers)])

    grads = {
        "norm_mix": g_norm_mix, "norm_ffn": g_norm_ffn, "norm_final": g_norm_final, "conv_w_conv": g_conv_w_conv,
        "attn_w_qkv": big_grad("attn_w_qkv", n_attn), "attn_b_qkv": g_attn_b_qkv, "attn_sinks": g_attn_sinks,
        "attn_w_o": big_grad("attn_w_o", n_attn), "attn_b_o": g_attn_b_o,
        "ffn_w_in": big_grad("ffn_w_in", depth), "ffn_w_conv": g_ffn_w_conv, "ffn_w_down": big_grad("ffn_w_down", depth),
    }
    params = {
        "norm_mix": (norm_mix, m_norm_mix, v_norm_mix), "norm_ffn": (norm_ffn, m_norm_ffn, v_norm_ffn),
        "norm_final": (norm_final, m_norm_final, v_norm_final), "conv_w_in": (conv_w_in, m_conv_w_in, v_conv_w_in),
        "conv_w_conv": (conv_w_conv, m_conv_w_conv, v_conv_w_conv), "conv_w_out": (conv_w_out, m_conv_w_out, v_conv_w_out),
        "attn_w_qkv": (attn_w_qkv, m_attn_w_qkv, v_attn_w_qkv), "attn_b_qkv": (attn_b_qkv, m_attn_b_qkv, v_attn_b_qkv),
        "attn_sinks": (attn_sinks, m_attn_sinks, v_attn_sinks), "attn_w_o": (attn_w_o, m_attn_w_o, v_attn_w_o),
        "attn_b_o": (attn_b_o, m_attn_b_o, v_attn_b_o), "ffn_w_in": (ffn_w_in, m_ffn_w_in, v_ffn_w_in),
        "ffn_w_conv": (ffn_w_conv, m_ffn_w_conv, v_ffn_w_conv), "ffn_w_down": (ffn_w_down, m_ffn_w_down, v_ffn_w_down),
    }
    names = list(params)
    updates = {}

    def update(wname):
        w, m, v = params[wname]
        updates[wname] = _adamw_nd(w, grads[wname], m, v, f"adamw_{wname}")

    last_names = sorted({wname for wname, _, _, _ in members_of[last_key]})
    for wname in names:
        if wname not in last_names:
            update(wname)
    finish_scatter(last_key, updates["ffn_w_in"][0])
    for wname in last_names:
        grads[wname] = big_grad(wname, params[wname][0].shape[0])
        update(wname)
    return (loss, grad_x, *[grads[wname] for wname in names], *[updates[wname][0] for wname in names],
            *[updates[wname][1] for wname in names], *[updates[wname][2] for wname in names])
```

```python
from typing import NamedTuple

import jax
import jax.numpy as jnp
from jax import lax
from jax.experimental import pallas as pl
from jax.experimental.pallas import tpu as pltpu

F32 = jnp.float32
BF16 = jnp.bfloat16
SDS = jax.ShapeDtypeStruct
MESH = pl.DeviceIdType.MESH
ANY = pl.BlockSpec(memory_space=pl.ANY)

N_DEV = 8
EPS = 1e-5
HEAD_DIM = 64
GROUP = 4
WINDOW = 128
ROPE_THETA = 10000.0
ADAM_LR, ADAM_B1, ADAM_B2, ADAM_EPS, ADAM_WD, ADAM_STEP = 0.001, 0.9, 0.999, 1e-08, 0.01, 10

V7X_VMEM_BYTES = 64 * 1024 * 1024
VMEM_LIMIT_BYTES = V7X_VMEM_BYTES - 8 * 1024 * 1024
LANES = 128
SUBLANES = 8
TOKEN_TILE = 512
TN_TOKEN_TILE = 2048
MASKED_SCORE = -1e30


def _cparams(n_axes=1):
    return pltpu.CompilerParams(dimension_semantics=("arbitrary",) * n_axes, vmem_limit_bytes=VMEM_LIMIT_BYTES)


def _resident(shape):
    zeros = (0,) * len(shape)
    return pl.BlockSpec(shape, lambda *_: zeros, pipeline_mode=pl.Buffered(1))


class _Rows(NamedTuple):
    arr: jax.Array
    off: int
    n: int


def _rows_spec(w):
    assert w.off % w.n == 0
    block = w.off // w.n
    return pl.BlockSpec((N_DEV, w.n, w.arr.shape[2]), lambda *_: (0, block, 0), pipeline_mode=pl.Buffered(1))


def _mat(ref):
    v = ref[...]
    return v.reshape(v.shape[0] * v.shape[1], v.shape[2])


def _token_tile(seq):
    return min(TOKEN_TILE, seq // 2)


def _largest_divisor(m, cap, mult):
    best = None
    for d in range(mult, min(m, cap) + 1, mult):
        if m % d == 0:
            best = d
    return m if best is None else best


def _nt(a, b):
    return lax.dot_general(a, b, (((1,), (1,)), ((), ())), preferred_element_type=F32)


def _nn(a, b):
    return lax.dot_general(a, b, (((1,), (0,)), ((), ())), preferred_element_type=F32)


def _tn(a, b):
    return lax.dot_general(a, b, (((0,), (0,)), ((), ())), preferred_element_type=F32)


def _rms_parts(xv):
    r = lax.rsqrt(jnp.mean(xv * xv, axis=-1, keepdims=True) + EPS)
    return r, xv * r


def _rms_backward(dh, xh, r, gain, dres):
    u = dh * gain
    return dres + r * (u - xh * jnp.mean(u * xh, axis=-1, keepdims=True))


def _causal_conv3(ext_ref, xv, w_ref, n):
    ext_ref[8:8 + n, :] = xv
    return w_ref[2:3, :] * xv + w_ref[1:2, :] * ext_ref[7:7 + n, :] + w_ref[0:1, :] * ext_ref[6:6 + n, :]


def _anticausal_conv3(ext_ref, xv, w_ref, n):
    ext_ref[0:n, :] = xv
    sh1 = ext_ref[1:1 + n, :]
    sh2 = ext_ref[2:2 + n, :]
    return w_ref[2:3, :] * xv + w_ref[1:2, :] * sh1 + w_ref[0:1, :] * sh2, sh1, sh2


def _sigmoid(z):
    return 1.0 / (1.0 + jnp.exp(-z))


def _fwd_conv_mixer(x, gain, w_in_t, w_conv, w_out, seq, name):
    t_all, d = x.shape
    tt = _token_tile(seq)
    tps = seq // tt

    def body(x_ref, g_ref, win_ref, wc_ref, wout_ref, x1_ref, bcv_ref, cc_ref, y_ref, ext_ref):
        i = pl.program_id(0)
        xv = x_ref[...]
        r, xh = _rms_parts(xv)
        h = (xh * g_ref[...]).astype(BF16)
        bcv = _nt(h, _mat(win_ref))
        bcv_ref[...] = bcv.astype(BF16)

        @pl.when(i % tps == 0)
        def _():
            ext_ref[0:8, :] = jnp.zeros((8, d), F32)

        cc = _causal_conv3(ext_ref, bcv[:, d:2 * d] * bcv[:, 2 * d:], wc_ref, tt)
        ext_ref[0:8, :] = ext_ref[tt:tt + 8, :]
        cc_ref[...] = cc.astype(BF16)
        y = (bcv[:, :d] * cc).astype(BF16)
        y_ref[...] = y
        x1_ref[...] = xv + _nn(y, _mat(wout_ref))

    tile = pl.BlockSpec((tt, d), lambda i: (i, 0))
    return pl.pallas_call(
        body, name=name, grid=(t_all // tt,),
        in_specs=[tile, _resident((1, d)), _rows_spec(w_in_t), _resident((3, d)), _rows_spec(w_out)],
        out_specs=[tile, pl.BlockSpec((tt, 3 * d), lambda i: (i, 0)), tile, tile],
        out_shape=[SDS((t_all, d), F32), SDS((t_all, 3 * d), BF16), SDS((t_all, d), BF16), SDS((t_all, d), BF16)],
        scratch_shapes=[pltpu.VMEM((tt + 8, d), F32)],
        compiler_params=_cparams(),
    )(x, gain, w_in_t.arr, w_conv, w_out.arr)


def _fwd_ffn(x, gain, w_in_t, w_conv, w_down, seq, name):
    t_all, d = x.shape
    f = w_down.n * N_DEV
    tt = _token_tile(seq) // 2
    tps = seq // tt

    def body(x_ref, g_ref, win_ref, wc_ref, wd_ref, x2_ref, gate_ref, s_ref, uds_ref, a_ref, ext_ref):
        i = pl.program_id(0)
        xv = x_ref[...]
        r, xh = _rms_parts(xv)
        h = (xh * g_ref[...]).astype(BF16)
        gu = _nt(h, _mat(win_ref))
        gate = gu[:, :f]
        u = gu[:, f:]
        gate_ref[...] = gate.astype(BF16)

        @pl.when(i % tps == 0)
        def _():
            ext_ref[0:8, :] = jnp.zeros((8, f), F32)

        gc = _causal_conv3(ext_ref, gate, wc_ref, tt)
        ext_ref[0:8, :] = ext_ref[tt:tt + 8, :]
        sig = _sigmoid(gc)
        s = gc * sig
        s_ref[...] = s.astype(BF16)
        uds_ref[...] = (u * (sig * (1.0 + gc * (1.0 - sig)))).astype(BF16)
        a = (s * u).astype(BF16)
        a_ref[...] = a
        x2_ref[...] = xv + _nn(a, _mat(wd_ref))

    wide = pl.BlockSpec((tt, f), lambda i: (i, 0))
    return pl.pallas_call(
        body, name=name, grid=(t_all // tt,),
        in_specs=[pl.BlockSpec((tt, d), lambda i: (i, 0)), _resident((1, d)), _rows_spec(w_in_t),
                  _resident((3, f)), _rows_spec(w_down)],
        out_specs=[pl.BlockSpec((tt, d), lambda i: (i, 0)), wide, wide, wide, wide],
        out_shape=[SDS((t_all, d), F32)] + [SDS((t_all, f), BF16)] * 4,
        scratch_shapes=[pltpu.VMEM((tt + 8, f), F32)],
        compiler_params=_cparams(),
    )(x, gain, w_in_t.arr, w_conv, w_down.arr)


def _rope_partner(xs, lane_lo):
    return jnp.where(lane_lo, pltpu.roll(xs, LANES - HEAD_DIM // 2, 1), pltpu.roll(xs, HEAD_DIM // 2, 1))


def _fwd_qkv(x, gain, w_qkv_t, b_qkv, cos_t, sin_t, seq, name):
    t_all, d = x.shape
    width = w_qkv_t.n * N_DEV
    kvw = (width - d) // 2
    tt = _token_tile(seq)
    tps = seq // tt
    scale = HEAD_DIM ** -0.5

    def body(x_ref, g_ref, w_ref, b_ref, cos_ref, sin_ref, qkv_ref):
        xv = x_ref[...]
        r, xh = _rms_parts(xv)
        h = (xh * g_ref[...]).astype(BF16)
        qkv = _nt(h, _mat(w_ref)) + b_ref[...]
        cosv = cos_ref[...]
        sinv = sin_ref[...]
        lane_lo = (lax.broadcasted_iota(jnp.int32, (tt, LANES), 1) % HEAD_DIM) < HEAD_DIM // 2
        for s in range((d + kvw) // LANES):
            xs = qkv[:, s * LANES:(s + 1) * LANES]
            roped = xs * cosv + _rope_partner(xs, lane_lo) * sinv
            if s * LANES < d:
                roped = roped * scale
            qkv_ref[:, s * LANES:(s + 1) * LANES] = roped.astype(BF16)
        qkv_ref[:, d + kvw:] = qkv[:, d + kvw:].astype(BF16)

    return pl.pallas_call(
        body, name=name, grid=(t_all // tt,),
        in_specs=[pl.BlockSpec((tt, d), lambda i: (i, 0)), _resident((1, d)), _rows_spec(w_qkv_t),
                  _resident((1, width)), pl.BlockSpec((tt, LANES), lambda i: (i % tps, 0)),
                  pl.BlockSpec((tt, LANES), lambda i: (i % tps, 0))],
        out_specs=pl.BlockSpec((tt, width), lambda i: (i, 0)),
        out_shape=SDS((t_all, width), BF16),
        compiler_params=_cparams(),
    )(x, gain, w_qkv_t.arr, b_qkv, cos_t, sin_t)


def _stack_heads(ref, row0, kh):
    return jnp.concatenate(
        [ref[row0:row0 + WINDOW, (kh * GROUP + g) * HEAD_DIM:(kh * GROUP + g + 1) * HEAD_DIM] for g in range(GROUP)],
        axis=0)


def _band_bias():
    r = lax.broadcasted_iota(jnp.int32, (WINDOW, 2 * WINDOW), 0)
    j = lax.broadcasted_iota(jnp.int32, (WINDOW, 2 * WINDOW), 1)
    base = (j > r) & (j <= r + WINDOW)
    return jnp.where(base, 0.0, MASKED_SCORE), jnp.where(base & (j >= WINDOW), 0.0, MASKED_SCORE)


def _fwd_attention(qkv, x, sinks, w_o, b_o, seq, name):
    t_all, d = x.shape
    width = qkv.shape[1]
    kvw = (width - d) // 2
    n_kv = kvw // HEAD_DIM
    tt = _token_tile(seq)
    tps = seq // tt
    nblk = tt // WINDOW

    def body(sink_ref, qkv_ref, kvp_ref, x_ref, wo_ref, bo_ref, x1_ref, o_ref, kvext_ref, oscr_ref, bias_ref, s_ref, p_ref):
        i = pl.program_id(0)

        @pl.when(i == 0)
        def _():
            base, first = _band_bias()
            bias_ref[0], bias_ref[1] = base.T, first.T

        kvext_ref[0:WINDOW, :] = kvp_ref[...]
        kvext_ref[WINDOW:, :] = qkv_ref[:, d:]
        at_seq_start = (i % tps == 0).astype(jnp.int32)
        steps = [(n, kh) for n in range(nblk) for kh in range(n_kv)]

        def scores(step):
            n, kh = steps[step]
            qs = _stack_heads(qkv_ref, n * WINDOW, kh)
            kb = kvext_ref[n * WINDOW:(n + 2) * WINDOW, kh * HEAD_DIM:(kh + 1) * HEAD_DIM]
            s_ref[step % 2] = _nt(kb, qs)

        scores(0)
        for step, (n, kh) in enumerate(steps):
            buf = step % 2
            if step + 1 < len(steps):
                scores(step + 1)
            vb = kvext_ref[n * WINDOW:(n + 2) * WINDOW, kvw + kh * HEAD_DIM:kvw + (kh + 1) * HEAD_DIM]
            bias = bias_ref[at_seq_start if n == 0 else 0]
            for g in range(GROUP):
                cols = slice(g * WINDOW, (g + 1) * WINDOW)
                sink = sink_ref[kh * GROUP + g]
                sv = s_ref[buf, :, cols] + bias
                m = jnp.maximum(jnp.max(sv, axis=0, keepdims=True), sink)
                p = jnp.exp(sv - m)
                inv = 1.0 / (jnp.sum(p, axis=0, keepdims=True) + jnp.exp(sink - m))
                p_ref[buf, :, cols] = (p * inv).astype(BF16)
            o_s = _tn(vb, p_ref[buf]).T
            for g in range(GROUP):
                hd = kh * GROUP + g
                oscr_ref[n * WINDOW:(n + 1) * WINDOW, hd * HEAD_DIM:(hd + 1) * HEAD_DIM] = (
                    o_s[g * WINDOW:(g + 1) * WINDOW].astype(BF16))
        o = oscr_ref[...]
        o_ref[...] = o
        x1_ref[...] = x_ref[...] + _nn(o, _mat(wo_ref)) + bo_ref[...]

    kv_blocks = tt // WINDOW
    return pl.pallas_call(
        body, name=name, grid=(t_all // tt,),
        in_specs=[pl.BlockSpec(memory_space=pltpu.SMEM),
                  pl.BlockSpec((tt, width), lambda i: (i, 0)),
                  pl.BlockSpec((WINDOW, 2 * kvw), lambda i: (jnp.maximum(i * kv_blocks - 1, 0), d // (2 * kvw))),
                  pl.BlockSpec((tt, d), lambda i: (i, 0)), _rows_spec(w_o), _resident((1, d))],
        out_specs=[pl.BlockSpec((tt, d), lambda i: (i, 0)), pl.BlockSpec((tt, d), lambda i: (i, 0))],
        out_shape=[SDS((t_all, d), F32), SDS((t_all, d), BF16)],
        scratch_shapes=[pltpu.VMEM((tt + WINDOW, 2 * kvw), BF16), pltpu.VMEM((tt, d), BF16),
                        pltpu.VMEM((2, 2 * WINDOW, WINDOW), F32), pltpu.VMEM((2, 2 * WINDOW, GROUP * WINDOW), F32),
                        pltpu.VMEM((2, 2 * WINDOW, GROUP * WINDOW), BF16)],
        compiler_params=_cparams(),
    )(sinks, qkv, qkv, x, w_o.arr, b_o)


def _final_norm_loss(x, gain, target, name):
    t_all, d = x.shape
    tt = min(TOKEN_TILE, t_all)

    def body(x_ref, g_ref, t_ref, dx_ref, dg_ref, loss_ref):
        i = pl.program_id(0)
        xv = x_ref[...]
        r, xh = _rms_parts(xv)
        gain_v = g_ref[...]
        e = xh * gain_v - t_ref[...]
        dy = e * (1.0 / d)
        dx_ref[...] = _rms_backward(dy, xh, r, gain_v, 0.0)

        @pl.when(i == 0)
        def _():
            dg_ref[...] = jnp.zeros_like(dg_ref)
            loss_ref[...] = jnp.zeros_like(loss_ref)

        dg_ref[...] += jnp.sum(dy * xh, axis=0, keepdims=True)
        loss_ref[...] += (0.5 / d) * jnp.sum(e * e, axis=0, keepdims=True)

    return pl.pallas_call(
        body, name=name, grid=(t_all // tt,),
        in_specs=[pl.BlockSpec((tt, d), lambda i: (i, 0)), _resident((1, d)), pl.BlockSpec((tt, d), lambda i: (i, 0))],
        out_specs=[pl.BlockSpec((tt, d), lambda i: (i, 0)), pl.BlockSpec((1, d), lambda i: (0, 0)),
                   pl.BlockSpec((1, d), lambda i: (0, 0))],
        out_shape=[SDS((t_all, d), F32), SDS((1, d), F32), SDS((1, d), F32)],
        compiler_params=_cparams(),
    )(x, gain, target)


def _bwd_ffn_inner(dx2, gate, s_act, uds, w_conv, w_down, seq, name):
    t_all, d = dx2.shape
    f = w_down.n * N_DEV
    tt = _token_tile(seq) // 2
    tps = seq // tt
    nt = t_all // tt

    def body(dx_ref, g_ref, s_ref, uds_ref, wc_ref, wd_ref, dgu_ref, dwc_ref, aext_ref):
        i = pl.program_id(0)
        ti = nt - 1 - i
        da = _nt(dx_ref[...].astype(BF16), _mat(wd_ref))
        g = g_ref[...].astype(F32)
        dgc = da * uds_ref[...].astype(F32)

        @pl.when(ti % tps == tps - 1)
        def _():
            aext_ref[tt:tt + 8, :] = jnp.zeros((8, f), F32)

        dg, sh1, sh2 = _anticausal_conv3(aext_ref, dgc, wc_ref, tt)
        aext_ref[tt:tt + 8, :] = aext_ref[0:8, :]
        dgu_ref[:, :f] = dg.astype(BF16)
        dgu_ref[:, f:] = (da * s_ref[...].astype(F32)).astype(BF16)

        @pl.when(i == 0)
        def _():
            dwc_ref[...] = jnp.zeros_like(dwc_ref)

        dwc_ref[0:1, :] += jnp.sum(g * sh2, axis=0, keepdims=True)
        dwc_ref[1:2, :] += jnp.sum(g * sh1, axis=0, keepdims=True)
        dwc_ref[2:3, :] += jnp.sum(g * dgc, axis=0, keepdims=True)

    rev = lambda i: (nt - 1 - i, 0)
    return pl.pallas_call(
        body, name=name, grid=(nt,),
        in_specs=[pl.BlockSpec((tt, d), rev)] + [pl.BlockSpec((tt, f), rev)] * 3 + [_resident((3, f)), _rows_spec(w_down)],
        out_specs=[pl.BlockSpec((tt, 2 * f), rev), pl.BlockSpec((8, f), lambda i: (0, 0))],
        out_shape=[SDS((t_all, 2 * f), BF16), SDS((8, f), F32)],
        scratch_shapes=[pltpu.VMEM((tt + 8, f), F32)],
        compiler_params=_cparams(),
    )(dx2, gate, s_act, uds, w_conv, w_down.arr)


def _bwd_conv_inner(dx1, bcv, cc, w_conv, w_out, seq, name):
    t_all, d = dx1.shape
    tt = _token_tile(seq)
    tps = seq // tt
    nt = t_all // tt

    def body(dx_ref, bcv_ref, cc_ref, wc_ref, wout_ref, dbcv_ref, dwc_ref, aext_ref):
        i = pl.program_id(0)
        ti = nt - 1 - i
        dy = _nt(dx_ref[...].astype(BF16), _mat(wout_ref))
        bcv_v = bcv_ref[...].astype(F32)
        b = bcv_v[:, :d]
        c = bcv_v[:, d:2 * d]
        v = bcv_v[:, 2 * d:]
        cv = c * v
        dcc = dy * b

        @pl.when(ti % tps == tps - 1)
        def _():
            aext_ref[tt:tt + 8, :] = jnp.zeros((8, d), F32)

        dcv, sh1, sh2 = _anticausal_conv3(aext_ref, dcc, wc_ref, tt)
        aext_ref[tt:tt + 8, :] = aext_ref[0:8, :]
        dbcv_ref[:, :d] = (dy * cc_ref[...].astype(F32)).astype(BF16)
        dbcv_ref[:, d:2 * d] = (dcv * v).astype(BF16)
        dbcv_ref[:, 2 * d:] = (dcv * c).astype(BF16)

        @pl.when(i == 0)
        def _():
            dwc_ref[...] = jnp.zeros_like(dwc_ref)

        dwc_ref[0:1, :] += jnp.sum(cv * sh2, axis=0, keepdims=True)
        dwc_ref[1:2, :] += jnp.sum(cv * sh1, axis=0, keepdims=True)
        dwc_ref[2:3, :] += jnp.sum(cv * dcc, axis=0, keepdims=True)

    return pl.pallas_call(
        body, name=name, grid=(nt,),
        in_specs=[pl.BlockSpec((tt, d), lambda i: (nt - 1 - i, 0)),
                  pl.BlockSpec((tt, 3 * d), lambda i: (nt - 1 - i, 0)),
                  pl.BlockSpec((tt, d), lambda i: (nt - 1 - i, 0)),
                  _resident((3, d)), _rows_spec(w_out)],
        out_specs=[pl.BlockSpec((tt, 3 * d), lambda i: (nt - 1 - i, 0)), pl.BlockSpec((8, d), lambda i: (0, 0))],
        out_shape=[SDS((t_all, 3 * d), BF16), SDS((8, d), F32)],
        scratch_shapes=[pltpu.VMEM((tt + 8, d), F32)],
        compiler_params=_cparams(),
    )(dx1, bcv, cc, w_conv, w_out.arr)


def _bwd_attention_inner(dx1, qkv, sinks, w_o, cos_t, sin_t, seq, name):
    t_all, d = dx1.shape
    width = qkv.shape[1]
    kvw = (width - d) // 2
    n_kv = kvw // HEAD_DIM
    tt = _token_tile(seq)
    tps = seq // tt
    nt = t_all // tt
    nblk = tt // WINDOW
    scale = HEAD_DIM ** -0.5

    def body(sink_ref, dx_ref, qkv_ref, kvp_ref, cos_ref, sin_ref, wo_ref,
             dqkv_ref, dsink_ref, dbqkv_ref, dbo_ref,
             kvext_ref, dkvext_ref, carry_ref, dq_ref, do_ref, bias_ref, s_ref, dp_ref, p_ref, ds_ref):
        i = pl.program_id(0)
        ti = nt - 1 - i
        dxv = dx_ref[...]
        do_ref[...] = _nt(dxv.astype(BF16), _mat(wo_ref)).astype(BF16)
        kvext_ref[0:WINDOW, :] = kvp_ref[...]
        kvext_ref[WINDOW:, :] = qkv_ref[:, d:]
        dkvext_ref[...] = jnp.zeros_like(dkvext_ref)

        @pl.when(i == 0)
        def _():
            base, first = _band_bias()
            bias_ref[0], bias_ref[1] = base.T, first.T
            carry_ref[...] = jnp.zeros_like(carry_ref)
            dsink_ref[...] = jnp.zeros_like(dsink_ref)
            dbqkv_ref[...] = jnp.zeros_like(dbqkv_ref)
            dbo_ref[...] = jnp.zeros_like(dbo_ref)

        at_seq_start = (ti % tps == 0).astype(jnp.int32)
        head_lane = lax.broadcasted_iota(jnp.int32, (1, LANES), 1)
        dsink = jnp.zeros((1, LANES), F32)
        for n in range(nblk):
            for kh in range(n_kv):
                buf = (n * n_kv + kh) % 2
                qs = _stack_heads(qkv_ref, n * WINDOW, kh)
                dos = _stack_heads(do_ref, n * WINDOW, kh)
                kcols = slice(kh * HEAD_DIM, (kh + 1) * HEAD_DIM)
                vcols = slice(kvw + kh * HEAD_DIM, kvw + (kh + 1) * HEAD_DIM)
                band = slice(n * WINDOW, (n + 2) * WINDOW)
                kb = kvext_ref[band, kcols]
                vb = kvext_ref[band, vcols]
                s_ref[buf] = _nt(kb, qs)
                dp_ref[buf] = _nt(vb, dos)
                bias = bias_ref[at_seq_start if n == 0 else 0]
                for g in range(GROUP):
                    hd = kh * GROUP + g
                    cols = slice(g * WINDOW, (g + 1) * WINDOW)
                    sink = sink_ref[hd]
                    sv = s_ref[buf, :, cols] + bias
                    m = jnp.maximum(jnp.max(sv, axis=0, keepdims=True), sink)
                    p = jnp.exp(sv - m)
                    e_sink = jnp.exp(sink - m)
                    inv = 1.0 / (jnp.sum(p, axis=0, keepdims=True) + e_sink)
                    probs = p * inv
                    dp = dp_ref[buf, :, cols]
                    dsum = jnp.sum(probs * dp, axis=0, keepdims=True)
                    p_ref[buf, :, cols] = probs.astype(BF16)
                    ds_ref[buf, :, cols] = (probs * (dp - dsum)).astype(BF16)
                    dsink = dsink - jnp.where(head_lane == hd, jnp.sum(e_sink * inv * dsum), 0.0)
                ds_t = ds_ref[buf]
                dkvext_ref[band, vcols] += _nn(p_ref[buf], dos)
                dkvext_ref[band, kcols] += _nn(ds_t, qs)
                dq_s = _tn(kb, ds_t).T
                for g in range(GROUP):
                    hd = kh * GROUP + g
                    dq_ref[n * WINDOW:(n + 1) * WINDOW, hd * HEAD_DIM:(hd + 1) * HEAD_DIM] = dq_s[g * WINDOW:(g + 1) * WINDOW]
        dsink_ref[0:1, :] += dsink
        dkvext_ref[tt:tt + WINDOW, :] += carry_ref[...]
        carry_ref[...] = dkvext_ref[0:WINDOW, :]

        cosv = cos_ref[...]
        sinv = sin_ref[...]
        lane_lo = (lax.broadcasted_iota(jnp.int32, (tt, LANES), 1) % HEAD_DIM) < HEAD_DIM // 2
        for s in range((d + kvw) // LANES):
            if s * LANES < d:
                dy = dq_ref[:, s * LANES:(s + 1) * LANES] * scale
            else:
                dy = dkvext_ref[WINDOW:, s * LANES - d:(s + 1) * LANES - d]
            dpre = dy * cosv - _rope_partner(dy, lane_lo) * sinv
            dqkv_ref[:, s * LANES:(s + 1) * LANES] = dpre.astype(BF16)
            dbqkv_ref[0:1, s * LANES:(s + 1) * LANES] += jnp.sum(dpre, axis=0, keepdims=True)
        dv = dkvext_ref[WINDOW:, kvw:]
        dqkv_ref[:, d + kvw:] = dv.astype(BF16)
        dbqkv_ref[0:1, d + kvw:] += jnp.sum(dv, axis=0, keepdims=True)
        dbo_ref[...] += jnp.sum(dxv, axis=0, keepdims=True)

    kv_blocks = tt // WINDOW
    return pl.pallas_call(
        body, name=name, grid=(nt,),
        in_specs=[pl.BlockSpec(memory_space=pltpu.SMEM),
                  pl.BlockSpec((tt, d), lambda i: (nt - 1 - i, 0)),
                  pl.BlockSpec((tt, width), lambda i: (nt - 1 - i, 0)),
                  pl.BlockSpec((WINDOW, 2 * kvw), lambda i: (jnp.maximum((nt - 1 - i) * kv_blocks - 1, 0), d // (2 * kvw))),
                  pl.BlockSpec((tt, LANES), lambda i: ((nt - 1 - i) % tps, 0)),
                  pl.BlockSpec((tt, LANES), lambda i: ((nt - 1 - i) % tps, 0)),
                  _rows_spec(w_o)],
        out_specs=[pl.BlockSpec((tt, width), lambda i: (nt - 1 - i, 0)), pl.BlockSpec((8, LANES), lambda i: (0, 0)),
                   pl.BlockSpec((1, width), lambda i: (0, 0)), pl.BlockSpec((1, d), lambda i: (0, 0))],
        out_shape=[SDS((t_all, width), BF16), SDS((8, LANES), F32), SDS((1, width), F32), SDS((1, d), F32)],
        scratch_shapes=[pltpu.VMEM((tt + WINDOW, 2 * kvw), BF16), pltpu.VMEM((tt + WINDOW, 2 * kvw), F32),
                        pltpu.VMEM((WINDOW, 2 * kvw), F32), pltpu.VMEM((tt, d), F32), pltpu.VMEM((tt, d), BF16),
                        pltpu.VMEM((2, 2 * WINDOW, WINDOW), F32), pltpu.VMEM((2, 2 * WINDOW, GROUP * WINDOW), F32),
                        pltpu.VMEM((2, 2 * WINDOW, GROUP * WINDOW), F32), pltpu.VMEM((2, 2 * WINDOW, GROUP * WINDOW), BF16),
                        pltpu.VMEM((2, 2 * WINDOW, GROUP * WINDOW), BF16)],
        compiler_params=_cparams(),
    )(sinks, dx1, qkv, qkv, cos_t, sin_t, w_o.arr)


def _bwd_dense_norm(dy, w_t, x, gain, dres, name):
    t_all, d = x.shape
    n = dy.shape[1]
    tt = min(TOKEN_TILE, t_all)

    def body(dy_ref, w_ref, x_ref, g_ref, dres_ref, dx_ref, h_ref, dg_ref):
        i = pl.program_id(0)
        dh = _nn(dy_ref[...], _mat(w_ref))
        r, xh = _rms_parts(x_ref[...])
        gain_v = g_ref[...]
        h_ref[...] = (xh * gain_v).astype(BF16)
        dx_ref[...] = _rms_backward(dh, xh, r, gain_v, dres_ref[...])

        @pl.when(i == 0)
        def _():
            dg_ref[...] = jnp.zeros_like(dg_ref)

        dg_ref[...] += jnp.sum(dh * xh, axis=0, keepdims=True)

    return pl.pallas_call(
        body, name=name, grid=(t_all // tt,),
        in_specs=[pl.BlockSpec((tt, n), lambda i: (i, 0)), _rows_spec(w_t), pl.BlockSpec((tt, d), lambda i: (i, 0)),
                  _resident((1, d)), pl.BlockSpec((tt, d), lambda i: (i, 0))],
        out_specs=[pl.BlockSpec((tt, d), lambda i: (i, 0)), pl.BlockSpec((tt, d), lambda i: (i, 0)),
                   pl.BlockSpec((1, d), lambda i: (0, 0))],
        out_shape=[SDS((t_all, d), F32), SDS((t_all, d), BF16), SDS((1, d), F32)],
        compiler_params=_cparams(),
    )(dy, w_t.arr, x, gain, dres)


def _tn_matmul(a, b, dest, name):
    t_all, m = a.shape
    d = b.shape[1]
    n = dest.n
    assert m == N_DEV * n and dest.off % n == 0
    k = max(kk for kk in (1, 2, 4, 8) if kk * n <= max(n, 1536))
    tm = k * n
    tt = min(TN_TOKEN_TILE, t_all)
    n_t = t_all // tt
    fresh = not hasattr(dest.arr, "dtype")

    def body(a_ref, b_ref, *rest):
        o_ref, acc_ref = rest[-2:]
        t = pl.program_id(1)

        @pl.when(t == 0)
        def _():
            acc_ref[...] = jnp.zeros_like(acc_ref)

        acc_ref[...] += _tn(a_ref[...], b_ref[...].astype(BF16))

        @pl.when(t == n_t - 1)
        def _():
            o_ref[...] = acc_ref[...].astype(BF16).reshape(k, n, d)

    block = dest.off // n
    return pl.pallas_call(
        body, name=name, grid=(m // tm, n_t),
        in_specs=[pl.BlockSpec((tt, tm), lambda j, t: (t, j)), pl.BlockSpec((tt, d), lambda j, t: (t, 0))] + ([] if fresh else [ANY]),
        out_specs=pl.BlockSpec((k, n, d), lambda j, t: (j, block, 0)),
        out_shape=SDS(tuple(dest.arr) if fresh else dest.arr.shape, BF16),
        scratch_shapes=[pltpu.VMEM((tm, d), F32)],
        input_output_aliases={} if fresh else {2: 0},
        compiler_params=_cparams(2),
    )(*((a, b) if fresh else (a, b, dest.arr)))


def _my_place():
    return lax.axis_index("x"), lax.axis_index("y"), lax.axis_index("c")


def _other_chips(x, y):
    return [(1 - x, y), (x, 1 - y), (1 - x, 1 - y)]


def _all_gather(blocks, name):
    n_arr = len(blocks)

    def body(*refs):
        in_refs = refs[:n_arr]
        out_refs = refs[n_arr:2 * n_arr]
        send_sems, recv_sems, local_sems = refs[2 * n_arr:]
        x, y, c = _my_place()
        me, sibling = (x, y, c), (x, y, 1 - c)
        chips = _other_chips(x, y)

        def slot(a, place):
            px, py, pc = place
            return out_refs[a].at[4 * px + 2 * py + pc]

        def copy(a, k, block, to, src=None):
            return pltpu.make_async_remote_copy(
                src_ref=slot(a, block) if src is None else src, dst_ref=slot(a, block),
                send_sem=send_sems.at[a, k], recv_sem=recv_sems.at[a, k], device_id=to, device_id_type=MESH)

        started = []
        local = []
        for a in range(n_arr):
            mine = pltpu.make_async_copy(in_refs[a], slot(a, me), local_sems.at[a])
            mine.start()
            local.append(mine)
            first = [copy(a, 0, me, sibling, src=in_refs[a])]
            first += [copy(a, 1 + j, me, (*chip, c), src=in_refs[a]) for j, chip in enumerate(chips)]
            for cp in first:
                cp.start()
            started += first
        for a in range(n_arr):
            for j, chip in enumerate(chips):
                copy(a, 1 + j, (*chip, c), me).wait_recv()
                passed = copy(a, 4 + j, (*chip, c), sibling)
                passed.start()
                started.append(passed)
        for a in range(n_arr):
            copy(a, 0, sibling, me).wait_recv()
            for j, chip in enumerate(chips):
                copy(a, 4 + j, (*chip, 1 - c), me).wait_recv()
        for cp in started:
            cp.wait_send()
        for mine in local:
            mine.wait()

    return pl.pallas_call(
        body, name=name,
        in_specs=[ANY] * n_arr, out_specs=[ANY] * n_arr,
        out_shape=[SDS((N_DEV,) + b.shape, b.dtype) for b in blocks],
        scratch_shapes=[pltpu.SemaphoreType.DMA((n_arr, 7)), pltpu.SemaphoreType.DMA((n_arr, 7)),
                        pltpu.SemaphoreType.DMA((n_arr,))],
    )(*blocks)


def _peer_of(k, x, y, c):
    return x ^ ((k >> 2) & 1), y ^ ((k >> 1) & 1), c ^ (k & 1)


HBM = pl.BlockSpec(memory_space=pltpu.HBM)
SEM = pl.BlockSpec(memory_space=pltpu.SEMAPHORE)
DATAFLOW_EFFECT = pltpu.SideEffectType.DATAFLOW_SIDE_EFFECTING


def _peer_copies(src_ref, land_ref, send_sems, recv_sems, per_peer):
    x, y, c = _my_place()
    me = 4 * x + 2 * y + c
    copies = []
    for k in range(1, N_DEV):
        px, py, pc = _peer_of(k, x, y, c)
        peer = 4 * px + 2 * py + pc
        copies.append(pltpu.make_async_remote_copy(
            src_ref=src_ref.at[peer] if per_peer else src_ref, dst_ref=land_ref.at[me],
            send_sem=send_sems.at[k - 1], recv_sem=recv_sems.at[k - 1], device_id=(px, py, pc), device_id_type=MESH))
    own = pltpu.make_async_copy(src_ref.at[me] if per_peer else src_ref, land_ref.at[me], send_sems.at[N_DEV - 1])
    return copies, own


def _exchange_start(src, after, per_peer, name):
    rows, d = src.shape[-2:]

    def body(src_ref, land_ref, after_ref, send_sems, recv_sems, src_thru, land_thru, token):
        copies, own = _peer_copies(src_ref, land_ref, send_sems, recv_sems, per_peer)
        for cp in copies:
            cp.start()
        own.start()
        token[...] = jnp.zeros_like(token)

    return pl.pallas_call(
        body, name=name,
        out_shape=(pltpu.SemaphoreType.DMA((N_DEV,)), pltpu.SemaphoreType.DMA((N_DEV - 1,)), pltpu.HBM(src.shape, src.dtype),
                   pltpu.HBM((N_DEV, rows, d), src.dtype), SDS((SUBLANES, LANES), F32)),
        in_specs=(HBM, HBM, ANY), out_specs=(SEM, SEM, HBM, HBM, pl.BlockSpec(memory_space=pltpu.VMEM)),
        input_output_aliases={0: 2, 1: 3},
        compiler_params=pltpu.CompilerParams(has_side_effects=DATAFLOW_EFFECT),
    )(pltpu.with_memory_space_constraint(src, pltpu.HBM),
      pltpu.with_memory_space_constraint(lax.empty((N_DEV, rows, d), src.dtype), pltpu.HBM), after)


def _exchange_wait(started, after, per_peer, name):
    send_sems, recv_sems, src_thru, land_thru, _ = started

    def body(src_ref, land_ref, send_sems, recv_sems, after_ref, src_out, land_out):
        copies, own = _peer_copies(src_ref, land_ref, send_sems, recv_sems, per_peer)
        for cp in copies:
            cp.wait_send()
            cp.wait_recv()
        own.wait()

    return pl.pallas_call(
        body, name=name,
        out_shape=(pltpu.HBM(src_thru.shape, src_thru.dtype), pltpu.HBM(land_thru.shape, land_thru.dtype)),
        in_specs=(HBM, HBM, SEM, SEM, ANY), out_specs=(HBM, HBM), input_output_aliases={0: 0, 1: 1},
        compiler_params=pltpu.CompilerParams(has_side_effects=DATAFLOW_EFFECT),
    )(src_thru, land_thru, send_sems, recv_sems, after)


def _sum_slots(slots, name):
    _, rows, d = slots.shape
    tr = _largest_divisor(rows, 512, 16)

    def body(s_ref, o_ref):
        acc = s_ref[0].astype(F32)
        for dev in range(1, N_DEV):
            acc = acc + s_ref[dev].astype(F32)
        o_ref[...] = acc

    return pl.pallas_call(
        body, name=name, grid=(rows // tr,),
        in_specs=[pl.BlockSpec((N_DEV, tr, d), lambda r: (0, r, 0))], out_specs=pl.BlockSpec((tr, d), lambda r: (r, 0)),
        out_shape=SDS((rows, d), F32), compiler_params=_cparams(),
    )(slots)


def _all_reduce_small(part, loss_rows, name):
    rows, lanes = part.shape
    lo, hi = loss_rows

    def body(x_ref, out_ref, loss_ref, gath_ref, send_sems, recv_sems):
        x, y, c = _my_place()
        me = 4 * x + 2 * y + c
        gath_ref[me] = x_ref[...]
        copies = []
        for k in range(1, N_DEV):
            peer = (x ^ ((k >> 2) & 1), y ^ ((k >> 1) & 1), c ^ (k & 1))
            cp = pltpu.make_async_remote_copy(
                src_ref=x_ref, dst_ref=gath_ref.at[me], send_sem=send_sems.at[k - 1], recv_sem=recv_sems.at[k - 1],
                device_id=peer, device_id_type=MESH)
            cp.start()
            copies.append(cp)
        for cp in copies:
            cp.wait_recv()
        for cp in copies:
            cp.wait_send()
        acc = gath_ref[0]
        for dev in range(1, N_DEV):
            acc = acc + gath_ref[dev]
        out_ref[...] = acc
        loss_ref[...] = jnp.full(loss_ref.shape, jnp.sum(acc[lo:hi, :]), F32)

    vmem = pl.BlockSpec(memory_space=pltpu.VMEM)
    return pl.pallas_call(
        body, name=name, in_specs=[vmem], out_specs=[vmem, vmem],
        out_shape=[SDS((rows, lanes), F32), SDS((SUBLANES, LANES), F32)],
        scratch_shapes=[pltpu.VMEM((N_DEV, rows, lanes), F32), pltpu.SemaphoreType.DMA((N_DEV - 1,)),
                        pltpu.SemaphoreType.DMA((N_DEV - 1,))],
    )(part)


def _adamw(w, g, m, v, name):
    rows, cols = w.shape
    tr = rows if rows % SUBLANES else _largest_divisor(rows, 512, SUBLANES)

    def body(w_ref, g_ref, m_ref, v_ref, d_ref, nm_ref, nv_ref):
        gv = g_ref[...]
        nm = ADAM_B1 * m_ref[...] + (1.0 - ADAM_B1) * gv
        nv = ADAM_B2 * v_ref[...] + (1.0 - ADAM_B2) * (gv * gv)
        m_hat = nm / (1.0 - ADAM_B1 ** ADAM_STEP)
        v_hat = nv / (1.0 - ADAM_B2 ** ADAM_STEP)
        d_ref[...] = -ADAM_LR * (m_hat / (jnp.sqrt(v_hat) + ADAM_EPS) + ADAM_WD * w_ref[...])
        nm_ref[...] = nm
        nv_ref[...] = nv

    spec = pl.BlockSpec((tr, cols), lambda i: (i, 0))
    return pl.pallas_call(
        body, name=name, grid=(rows // tr,), in_specs=[spec] * 4, out_specs=[spec] * 3,
        out_shape=[SDS((rows, cols), F32)] * 3, compiler_params=_cparams(),
    )(w, g, m, v)


def _adamw_nd(w, g, m, v, name):
    shape = w.shape
    two_d = (1, shape[0]) if len(shape) == 1 else (-1, shape[-1])
    outs = _adamw(w.reshape(two_d), g.reshape(two_d), m.reshape(two_d), v.reshape(two_d), name)
    return [o.reshape(shape) for o in outs]


def _rope_tables(seq):
    pos = jnp.arange(seq, dtype=F32)
    inv_freq = 1.0 / (ROPE_THETA ** (jnp.arange(0, HEAD_DIM, 2, dtype=F32) / HEAD_DIM))
    ang = pos[:, None] * inv_freq[None, :]
    cos, sin = jnp.cos(ang), jnp.sin(ang)
    reps = LANES // HEAD_DIM
    cos_t = jnp.tile(jnp.concatenate([cos, cos], axis=1), (1, reps))
    sin_t = jnp.tile(jnp.concatenate([-sin, sin], axis=1), (1, reps))
    return cos_t, sin_t


def _flat_pad(a):
    flat = a.reshape(1, -1)
    pad = (-flat.shape[1]) % LANES
    return jnp.pad(flat, ((0, 0), (0, pad))) if pad else flat


def kernel(x, norm_mix, norm_ffn, norm_final, conv_w_in, conv_w_conv, conv_w_out, attn_w_qkv, attn_b_qkv, attn_sinks, attn_w_o, attn_b_o, ffn_w_in, ffn_w_conv, ffn_w_down, loss_target, m_norm_mix, m_norm_ffn, m_norm_final, m_conv_w_in, m_conv_w_conv, m_conv_w_out, m_attn_w_qkv, m_attn_b_qkv, m_attn_sinks, m_attn_w_o, m_attn_b_o, m_ffn_w_in, m_ffn_w_conv, m_ffn_w_down, v_norm_mix, v_norm_ffn, v_norm_final, v_conv_w_in, v_conv_w_conv, v_conv_w_out, v_attn_w_qkv, v_attn_b_qkv, v_attn_sinks, v_attn_w_o, v_attn_b_o, v_ffn_w_in, v_ffn_w_conv, v_ffn_w_down):
    b_loc, seq, d = x.shape
    depth = norm_mix.shape[0]
    n_conv, n_attn = conv_w_in.shape[0], attn_w_qkv.shape[0]
    t_all = b_loc * seq
    my_x, my_y, my_c = _my_place()

    me = 4 * my_x + 2 * my_y + my_c

    groups = []
    for i in range(depth):
        j = i // 2
        if i % 2 == 0:
            mix = [("conv_w_in", j, True, conv_w_in[j].T), ("conv_w_out", j, False, conv_w_out[j])]
        else:
            mix = [("attn_w_qkv", j, True, attn_w_qkv[j].T), ("attn_w_o", j, False, attn_w_o[j])]
        groups.append((("mix", i), mix))
        groups.append((("ffn", i), [("ffn_w_in", i, True, ffn_w_in[i].T), ("ffn_w_down", i, False, ffn_w_down[i])]))
    order = [key for key, _ in groups]
    members_of = dict(groups)

    def layout(key):
        offs, o = [], 0
        for _, _, _, shard in members_of[key]:
            n = shard.shape[0]
            o = -(-o // n) * n
            offs.append(o)
            o += n
        return offs, o

    small = jnp.concatenate([_flat_pad(conv_w_conv), _flat_pad(ffn_w_conv), _flat_pad(attn_b_qkv), _flat_pad(attn_b_o)], axis=1)
    (small_g,) = _all_gather([small], "gather_small")

    gather_started = {}

    def start_gather(idx, after):
        if idx >= len(order):
            return 0.0
        key = order[idx]
        offs, total = layout(key)
        pieces, o = [], 0
        for (_, _, _, shard), off in zip(members_of[key], offs):
            if off > o:
                pieces.append(jnp.zeros((off - o, d), shard.dtype))
            pieces.append(shard)
            o = off + shard.shape[0]
        pack = jnp.concatenate(pieces, axis=0).astype(BF16)
        gather_started[key] = _exchange_start(pack, after, False, f"gather_start_{key[0]}_{key[1]}")
        return gather_started[key][4][0, 0]

    weights = {}

    def finish_gather(key, after):
        _, land = _exchange_wait(gather_started[key], after, False, f"gather_wait_{key[0]}_{key[1]}")
        for (wname, layer, _, shard), off in zip(members_of[key], layout(key)[0]):
            weights[(wname, layer)] = _Rows(land, off, shard.shape[0])

    def take_small(o, shape):
        size = shape[0] * shape[1] * shape[2]
        blk = small_g[:, 0, o:o + size].reshape((N_DEV,) + shape)
        return jnp.moveaxis(blk, 0, 2).reshape(shape[0], shape[1], N_DEV * shape[2])

    so = 0
    wc_conv_full = take_small(so, conv_w_conv.shape); so += _flat_pad(conv_w_conv).shape[1]
    wc_ffn_full = take_small(so, ffn_w_conv.shape); so += _flat_pad(ffn_w_conv).shape[1]
    b_qkv_full = take_small(so, (n_attn, 1, attn_b_qkv.shape[1]))[:, 0]; so += _flat_pad(attn_b_qkv).shape[1]
    b_o_full = take_small(so, (n_attn, 1, attn_b_o.shape[1]))[:, 0]

    cos_t, sin_t = _rope_tables(seq)

    xs = [x.reshape(t_all, d)]
    saved = []
    token = start_gather(0, small_g) + start_gather(1, small_g)
    for i in range(depth):
        j = i // 2
        if i > 0:
            token = start_gather(2 * i + 2, xs[-1])
        gain_mix = norm_mix[i][None, :] + token
        finish_gather(("mix", i), gain_mix if i == 0 else xs[-1])
        if i % 2 == 0:
            x1, *mix_saved = _fwd_conv_mixer(xs[-1], gain_mix, weights[("conv_w_in", j)], wc_conv_full[j],
                                             weights[("conv_w_out", j)], seq, f"fwd_conv_{i}")
        else:
            qkv = _fwd_qkv(xs[-1], gain_mix, weights[("attn_w_qkv", j)], b_qkv_full[j][None, :], cos_t, sin_t, seq,
                           f"fwd_qkv_{i}")
            x1, o = _fwd_attention(qkv, xs[-1], attn_sinks[j], weights[("attn_w_o", j)], b_o_full[j][None, :], seq,
                                   f"fwd_attn_{i}")
            mix_saved = (qkv, o)
        token = start_gather(2 * i + 3, x1) + (start_gather(2, x1) if i == 0 else 0.0)
        gain_ffn = norm_ffn[i][None, :] + token
        finish_gather(("ffn", i), gain_ffn)
        x2, *ffn_saved = _fwd_ffn(x1, gain_ffn, weights[("ffn_w_in", i)], wc_ffn_full[i], weights[("ffn_w_down", i)],
                                  seq, f"fwd_ffn_{i}")
        saved.append((xs[-1], x1, mix_saved, ffn_saved))
        xs.append(x2)
        token = 0.0

    dx, dg_final, loss_lanes = _final_norm_loss(xs[-1], norm_final[None, :], loss_target.reshape(t_all, d), "loss_head")

    dg_mix, dg_ffn = [None] * depth, [None] * depth
    dwc_conv, dwc_ffn = [None] * n_conv, [None] * depth
    db_qkv, db_o, dsinks = [None] * n_attn, [None] * n_attn, [None] * n_attn
    scatter_started = {}

    def weight_grads(key, operands):
        offs, total = layout(key)
        parts = (N_DEV, total, d)
        for (wname, layer, _, shard), off, (a, b) in zip(members_of[key], offs, operands):
            parts = _tn_matmul(a, b, _Rows(parts, off, shard.shape[0]), f"dw_{wname}_{layer}")
        scatter_started[key] = _exchange_start(parts, operands[0][1], True, f"scatter_start_{key[0]}_{key[1]}")
        return scatter_started[key][4][0, 0]

    token = 0.0
    for i in reversed(range(depth)):
        j = i // 2
        x0, x1, mix_saved, (gate, s_act, uds, act) = saved[i]
        dgu, dwc = _bwd_ffn_inner(dx, gate, s_act, uds, wc_ffn_full[i] + token, weights[("ffn_w_down", i)], seq, f"bwd_ffn_{i}")
        dwc_ffn[i] = dwc[:3]
        dx1, h2, dg_ffn[i] = _bwd_dense_norm(dgu, weights[("ffn_w_in", i)], x1, norm_ffn[i][None, :], dx, f"bwd_ffn_norm_{i}")
        token = weight_grads(("ffn", i), [(dgu, h2), (act, dx)])
        if i % 2 == 0:
            bcv, cc, y = mix_saved
            dbcv, dwc = _bwd_conv_inner(dx1, bcv, cc, wc_conv_full[j] + token, weights[("conv_w_out", j)], seq, f"bwd_conv_{i}")
            dwc_conv[j] = dwc[:3]
            dx, h, dg_mix[i] = _bwd_dense_norm(dbcv, weights[("conv_w_in", j)], x0, norm_mix[i][None, :], dx1,
                                               f"bwd_conv_norm_{i}")
            token = weight_grads(("mix", i), [(dbcv, h), (y, dx1)])
        else:
            qkv, o = mix_saved
            dqkv, dsk, dbq, dbo = _bwd_attention_inner(dx1, qkv, attn_sinks[j] + token, weights[("attn_w_o", j)], cos_t, sin_t,
                                                       seq, f"bwd_attn_{i}")
            dsinks[j], db_qkv[j], db_o[j] = dsk[0:1, :attn_sinks.shape[1]], dbq, dbo
            dx, h, dg_mix[i] = _bwd_dense_norm(dqkv, weights[("attn_w_qkv", j)], x0, norm_mix[i][None, :], dx1,
                                               f"bwd_attn_norm_{i}")
            token = weight_grads(("mix", i), [(dqkv, h), (o, dx1)])
    grad_x = dx.reshape(b_loc, seq, d)

    reduced = {}

    def finish_scatter(key, after):
        _, land = _exchange_wait(scatter_started[key], after, True, f"scatter_wait_{key[0]}_{key[1]}")
        total = _sum_slots(land, f"scatter_sum_{key[0]}_{key[1]}")
        for (wname, layer, transposed, shard), off in zip(members_of[key], layout(key)[0]):
            rows = total[off:off + shard.shape[0]]
            reduced[(wname, layer)] = rows.T if transposed else rows

    last_key = order[0]
    for key in reversed(order[1:]):
        finish_scatter(key, dx)

    small_parts = [jnp.concatenate(dg_mix, axis=0), jnp.concatenate(dg_ffn, axis=0), dg_final,
                   jnp.stack(dwc_conv), jnp.stack(dwc_ffn), jnp.concatenate(db_qkv, axis=0), jnp.concatenate(db_o, axis=0),
                   jnp.concatenate(dsinks, axis=0), loss_lanes]
    flats = [_flat_pad(p) for p in small_parts]
    bounds = []
    so = 0
    for fl in flats:
        bounds.append((so, so + fl.shape[1]))
        so += fl.shape[1]
    small_rows = so // LANES
    pad_rows = (-small_rows) % SUBLANES
    part_small = jnp.pad(jnp.concatenate(flats, axis=1).reshape(small_rows, LANES), ((0, pad_rows), (0, 0)))
    loss_rows = (bounds[-1][0] // LANES, bounds[-1][1] // LANES)
    summed, loss_tile = _all_reduce_small(part_small, loss_rows, "reduce_small")
    summed = summed.reshape(1, -1)

    def small_grad(k, shape):
        lo = bounds[k][0]
        size = 1
        for s_ in shape:
            size *= s_
        return summed[0, lo:lo + size].reshape(shape)

    def my_cols(full, n_local):
        return lax.dynamic_slice_in_dim(full, me * n_local, n_local, axis=full.ndim - 1)

    g_norm_mix = small_grad(0, norm_mix.shape)
    g_norm_ffn = small_grad(1, norm_ffn.shape)
    g_norm_final = small_grad(2, norm_final.shape)
    g_conv_w_conv = my_cols(small_grad(3, (n_conv, 3, d)), conv_w_conv.shape[2])
    g_ffn_w_conv = my_cols(small_grad(4, (depth, 3, ffn_w_conv.shape[2] * N_DEV)), ffn_w_conv.shape[2])
    g_attn_b_qkv = my_cols(small_grad(5, (n_attn, attn_b_qkv.shape[1] * N_DEV)), attn_b_qkv.shape[1])
    g_attn_b_o = my_cols(small_grad(6, (n_attn, d)), attn_b_o.shape[1])
    g_attn_sinks = small_grad(7, attn_sinks.shape)
    loss = loss_tile[0, 0]

    def big_grad(wname, n_layers):
        return jnp.stack([reduced[(wname, layer)] for layer in range(n_layers)])

    grads = {
        "norm_mix": g_norm_mix, "norm_ffn": g_norm_ffn, "norm_final": g_norm_final, "conv_w_conv": g_conv_w_conv,
        "attn_w_qkv": big_grad("attn_w_qkv", n_attn), "attn_b_qkv": g_attn_b_qkv, "attn_sinks": g_attn_sinks,
        "attn_w_o": big_grad("attn_w_o", n_attn), "attn_b_o": g_attn_b_o,
        "ffn_w_in": big_grad("ffn_w_in", depth), "ffn_w_conv": g_ffn_w_conv, "ffn_w_down": big_grad("ffn_w_down", depth),
    }
    params = {
        "norm_mix": (norm_mix, m_norm_mix, v_norm_mix), "norm_ffn": (norm_ffn, m_norm_ffn, v_norm_ffn),
        "norm_final": (norm_final, m_norm_final, v_norm_final), "conv_w_in": (conv_w_in, m_conv_w_in, v_conv_w_in),
        "conv_w_conv": (conv_w_conv, m_conv_w_conv, v_conv_w_conv), "conv_w_out": (conv_w_out, m_conv_w_out, v_conv_w_out),
        "attn_w_qkv": (attn_w_qkv, m_attn_w_qkv, v_attn_w_qkv), "attn_b_qkv": (attn_b_qkv, m_attn_b_qkv, v_attn_b_qkv),
        "attn_sinks": (attn_sinks, m_attn_sinks, v_attn_sinks), "attn_w_o": (attn_w_o, m_attn_w_o, v_attn_w_o),
        "attn_b_o": (attn_b_o, m_attn_b_o, v_attn_b_o), "ffn_w_in": (ffn_w_in, m_ffn_w_in, v_ffn_w_in),
        "ffn_w_conv": (ffn_w_conv, m_ffn_w_conv, v_ffn_w_conv), "ffn_w_down": (ffn_w_down, m_ffn_w_down, v_ffn_w_down),
    }
    names = list(params)
    updates = {}

    def update(wname):
        w, m, v = params[wname]
        updates[wname] = _adamw_nd(w, grads[wname], m, v, f"adamw_{wname}")

    last_names = sorted({wname for wname, _, _, _ in members_of[last_key]})
    for wname in names:
        if wname not in last_names:
            update(wname)
    finish_scatter(last_key, updates["ffn_w_in"][0])
    for wname in last_names:
        grads[wname] = big_grad(wname, params[wname][0].shape[0])
        update(wname)
    return (loss, grad_x, *[grads[wname] for wname in names], *[updates[wname][0] for wname in names],
            *[updates[wname][1] for wname in names], *[updates[wname][2] for wname in names])
```

```python
from typing import NamedTuple

import jax
import jax.numpy as jnp
from jax import lax
from jax.experimental import pallas as pl
from jax.experimental.pallas import tpu as pltpu

F32 = jnp.float32
BF16 = jnp.bfloat16
SDS = jax.ShapeDtypeStruct
MESH = pl.DeviceIdType.MESH
ANY = pl.BlockSpec(memory_space=pl.ANY)

N_DEV = 8
EPS = 1e-5
HEAD_DIM = 64
GROUP = 4
WINDOW = 128
ROPE_THETA = 10000.0
ADAM_LR, ADAM_B1, ADAM_B2, ADAM_EPS, ADAM_WD, ADAM_STEP = 0.001, 0.9, 0.999, 1e-08, 0.01, 10

V7X_VMEM_BYTES = 64 * 1024 * 1024
VMEM_LIMIT_BYTES = V7X_VMEM_BYTES - 8 * 1024 * 1024
LANES = 128
SUBLANES = 8
TOKEN_TILE = 512
TN_TOKEN_TILE = 2048
MASKED_SCORE = -1e30


def _cparams(n_axes=1):
    return pltpu.CompilerParams(dimension_semantics=("arbitrary",) * n_axes, vmem_limit_bytes=VMEM_LIMIT_BYTES)


def _resident(shape):
    zeros = (0,) * len(shape)
    return pl.BlockSpec(shape, lambda *_: zeros, pipeline_mode=pl.Buffered(1))


class _Rows(NamedTuple):
    arr: jax.Array
    off: int
    n: int


def _rows_spec(w):
    assert w.off % w.n == 0
    block = w.off // w.n
    return pl.BlockSpec((N_DEV, w.n, w.arr.shape[2]), lambda *_: (0, block, 0), pipeline_mode=pl.Buffered(1))


def _mat(ref):
    v = ref[...]
    return v.reshape(v.shape[0] * v.shape[1], v.shape[2])


def _token_tile(seq):
    return min(TOKEN_TILE, seq // 2)


def _largest_divisor(m, cap, mult):
    best = None
    for d in range(mult, min(m, cap) + 1, mult):
        if m % d == 0:
            best = d
    return m if best is None else best


def _nt(a, b):
    return lax.dot_general(a, b, (((1,), (1,)), ((), ())), preferred_element_type=F32)


def _nn(a, b):
    return lax.dot_general(a, b, (((1,), (0,)), ((), ())), preferred_element_type=F32)


def _tn(a, b):
    return lax.dot_general(a, b, (((0,), (0,)), ((), ())), preferred_element_type=F32)


def _rms_parts(xv):
    r = lax.rsqrt(jnp.mean(xv * xv, axis=-1, keepdims=True) + EPS)
    return r, xv * r


def _rms_backward(dh, xh, r, gain, dres):
    u = dh * gain
    return dres + r * (u - xh * jnp.mean(u * xh, axis=-1, keepdims=True))


def _shifted_rows(xv, edge, k, down):
    n = xv.shape[0]
    row = lax.broadcasted_iota(jnp.int32, edge.shape, 0)
    if down:
        rolled = pltpu.roll(xv, k, 0)
        head = jnp.where(row < k, pltpu.roll(edge, k, 0), rolled[0:SUBLANES])
        return jnp.concatenate([head, rolled[SUBLANES:]], axis=0)
    rolled = pltpu.roll(xv, n - k, 0)
    tail = jnp.where(row >= SUBLANES - k, pltpu.roll(edge, SUBLANES - k, 0), rolled[n - SUBLANES:])
    return jnp.concatenate([rolled[:n - SUBLANES], tail], axis=0)


def _causal_conv3(edge_ref, xv, w_ref):
    before = edge_ref[...]
    y = (w_ref[2:3, :] * xv + w_ref[1:2, :] * _shifted_rows(xv, before, 1, True)
         + w_ref[0:1, :] * _shifted_rows(xv, before, 2, True))
    edge_ref[...] = xv[xv.shape[0] - SUBLANES:, :]
    return y


def _anticausal_conv3(edge_ref, xv, w_ref):
    after = edge_ref[...]
    sh1 = _shifted_rows(xv, after, 1, False)
    sh2 = _shifted_rows(xv, after, 2, False)
    edge_ref[...] = xv[0:SUBLANES, :]
    return w_ref[2:3, :] * xv + w_ref[1:2, :] * sh1 + w_ref[0:1, :] * sh2, sh1, sh2


def _sigmoid(z):
    return 1.0 / (1.0 + jnp.exp(-z))


def _fwd_conv_mixer(x, gain, w_in_t, w_conv, w_out, seq, name):
    t_all, d = x.shape
    tt = _token_tile(seq)
    tps = seq // tt

    def body(x_ref, g_ref, win_ref, wc_ref, wout_ref, x1_ref, bcv_ref, cc_ref, y_ref, ext_ref):
        i = pl.program_id(0)
        xv = x_ref[...]
        r, xh = _rms_parts(xv)
        h = (xh * g_ref[...]).astype(BF16)
        bcv = _nt(h, _mat(win_ref))
        bcv_ref[...] = bcv.astype(BF16)

        @pl.when(i % tps == 0)
        def _():
            ext_ref[...] = jnp.zeros_like(ext_ref)

        cc = _causal_conv3(ext_ref, bcv[:, d:2 * d] * bcv[:, 2 * d:], wc_ref)
        cc_ref[...] = cc.astype(BF16)
        y = (bcv[:, :d] * cc).astype(BF16)
        y_ref[...] = y
        x1_ref[...] = xv + _nn(y, _mat(wout_ref))

    tile = pl.BlockSpec((tt, d), lambda i: (i, 0))
    return pl.pallas_call(
        body, name=name, grid=(t_all // tt,),
        in_specs=[tile, _resident((1, d)), _rows_spec(w_in_t), _resident((3, d)), _rows_spec(w_out)],
        out_specs=[tile, pl.BlockSpec((tt, 3 * d), lambda i: (i, 0)), tile, tile],
        out_shape=[SDS((t_all, d), F32), SDS((t_all, 3 * d), BF16), SDS((t_all, d), BF16), SDS((t_all, d), BF16)],
        scratch_shapes=[pltpu.VMEM((SUBLANES, d), F32)],
        compiler_params=_cparams(),
    )(x, gain, w_in_t.arr, w_conv, w_out.arr)


def _fwd_ffn(x, gain, w_in_t, w_conv, w_down, seq, name):
    t_all, d = x.shape
    f = w_down.n * N_DEV
    tt = _token_tile(seq) // 2
    tps = seq // tt

    def body(x_ref, g_ref, win_ref, wc_ref, wd_ref, x2_ref, gate_ref, s_ref, uds_ref, a_ref, ext_ref):
        i = pl.program_id(0)
        xv = x_ref[...]
        r, xh = _rms_parts(xv)
        h = (xh * g_ref[...]).astype(BF16)
        gu = _nt(h, _mat(win_ref))
        gate = gu[:, :f]
        u = gu[:, f:]
        gate_ref[...] = gate.astype(BF16)

        @pl.when(i % tps == 0)
        def _():
            ext_ref[...] = jnp.zeros_like(ext_ref)

        gc = _causal_conv3(ext_ref, gate, wc_ref)
        sig = _sigmoid(gc)
        s = gc * sig
        s_ref[...] = s.astype(BF16)
        uds_ref[...] = (u * (sig * (1.0 + gc * (1.0 - sig)))).astype(BF16)
        a = (s * u).astype(BF16)
        a_ref[...] = a
        x2_ref[...] = xv + _nn(a, _mat(wd_ref))

    wide = pl.BlockSpec((tt, f), lambda i: (i, 0))
    return pl.pallas_call(
        body, name=name, grid=(t_all // tt,),
        in_specs=[pl.BlockSpec((tt, d), lambda i: (i, 0)), _resident((1, d)), _rows_spec(w_in_t),
                  _resident((3, f)), _rows_spec(w_down)],
        out_specs=[pl.BlockSpec((tt, d), lambda i: (i, 0)), wide, wide, wide, wide],
        out_shape=[SDS((t_all, d), F32)] + [SDS((t_all, f), BF16)] * 4,
        scratch_shapes=[pltpu.VMEM((SUBLANES, f), F32)],
        compiler_params=_cparams(),
    )(x, gain, w_in_t.arr, w_conv, w_down.arr)


def _rope_partner(xs, lane_lo):
    return jnp.where(lane_lo, pltpu.roll(xs, LANES - HEAD_DIM // 2, 1), pltpu.roll(xs, HEAD_DIM // 2, 1))


def _fwd_qkv(x, gain, w_qkv_t, b_qkv, cos_t, sin_t, seq, name):
    t_all, d = x.shape
    width = w_qkv_t.n * N_DEV
    kvw = (width - d) // 2
    tt = _token_tile(seq)
    tps = seq // tt
    scale = HEAD_DIM ** -0.5

    def body(x_ref, g_ref, w_ref, b_ref, cos_ref, sin_ref, qkv_ref):
        xv = x_ref[...]
        r, xh = _rms_parts(xv)
        h = (xh * g_ref[...]).astype(BF16)
        qkv = _nt(h, _mat(w_ref)) + b_ref[...]
        cosv = cos_ref[...]
        sinv = sin_ref[...]
        lane_lo = (lax.broadcasted_iota(jnp.int32, (tt, LANES), 1) % HEAD_DIM) < HEAD_DIM // 2
        for s in range((d + kvw) // LANES):
            xs = qkv[:, s * LANES:(s + 1) * LANES]
            roped = xs * cosv + _rope_partner(xs, lane_lo) * sinv
            if s * LANES < d:
                roped = roped * scale
            qkv_ref[:, s * LANES:(s + 1) * LANES] = roped.astype(BF16)
        qkv_ref[:, d + kvw:] = qkv[:, d + kvw:].astype(BF16)

    return pl.pallas_call(
        body, name=name, grid=(t_all // tt,),
        in_specs=[pl.BlockSpec((tt, d), lambda i: (i, 0)), _resident((1, d)), _rows_spec(w_qkv_t),
                  _resident((1, width)), pl.BlockSpec((tt, LANES), lambda i: (i % tps, 0)),
                  pl.BlockSpec((tt, LANES), lambda i: (i % tps, 0))],
        out_specs=pl.BlockSpec((tt, width), lambda i: (i, 0)),
        out_shape=SDS((t_all, width), BF16),
        compiler_params=_cparams(),
    )(x, gain, w_qkv_t.arr, b_qkv, cos_t, sin_t)


def _stack_heads(ref, row0, kh):
    return jnp.concatenate(
        [ref[row0:row0 + WINDOW, (kh * GROUP + g) * HEAD_DIM:(kh * GROUP + g + 1) * HEAD_DIM] for g in range(GROUP)],
        axis=0)


def _band_bias():
    r = lax.broadcasted_iota(jnp.int32, (WINDOW, 2 * WINDOW), 0)
    j = lax.broadcasted_iota(jnp.int32, (WINDOW, 2 * WINDOW), 1)
    base = (j > r) & (j <= r + WINDOW)
    return jnp.where(base, 0.0, MASKED_SCORE), jnp.where(base & (j >= WINDOW), 0.0, MASKED_SCORE)


def _fwd_attention(qkv, x, sinks, w_o, b_o, seq, name):
    t_all, d = x.shape
    width = qkv.shape[1]
    kvw = (width - d) // 2
    n_kv = kvw // HEAD_DIM
    tt = _token_tile(seq)
    tps = seq // tt
    nblk = tt // WINDOW

    def body(sink_ref, qkv_ref, kvp_ref, x_ref, wo_ref, bo_ref, x1_ref, o_ref, kvext_ref, oscr_ref, bias_ref, s_ref, p_ref):
        i = pl.program_id(0)

        @pl.when(i == 0)
        def _():
            base, first = _band_bias()
            bias_ref[0], bias_ref[1] = base.T, first.T

        kvext_ref[0:WINDOW, :] = kvp_ref[...]
        kvext_ref[WINDOW:, :] = qkv_ref[:, d:]
        at_seq_start = (i % tps == 0).astype(jnp.int32)
        steps = [(n, kh) for n in range(nblk) for kh in range(n_kv)]

        def scores(step):
            n, kh = steps[step]
            qs = _stack_heads(qkv_ref, n * WINDOW, kh)
            kb = kvext_ref[n * WINDOW:(n + 2) * WINDOW, kh * HEAD_DIM:(kh + 1) * HEAD_DIM]
            s_ref[step % 2] = _nt(kb, qs)

        scores(0)
        for step, (n, kh) in enumerate(steps):
            buf = step % 2
            if step + 1 < len(steps):
                scores(step + 1)
            vb = kvext_ref[n * WINDOW:(n + 2) * WINDOW, kvw + kh * HEAD_DIM:kvw + (kh + 1) * HEAD_DIM]
            bias = bias_ref[at_seq_start if n == 0 else 0]
            for g in range(GROUP):
                cols = slice(g * WINDOW, (g + 1) * WINDOW)
                sink = sink_ref[kh * GROUP + g]
                sv = s_ref[buf, :, cols] + bias
                m = jnp.maximum(jnp.max(sv, axis=0, keepdims=True), sink)
                p = jnp.exp(sv - m)
                inv = 1.0 / (jnp.sum(p, axis=0, keepdims=True) + jnp.exp(sink - m))
                p_ref[buf, :, cols] = (p * inv).astype(BF16)
            o_s = _tn(vb, p_ref[buf]).T
            for g in range(GROUP):
                hd = kh * GROUP + g
                oscr_ref[n * WINDOW:(n + 1) * WINDOW, hd * HEAD_DIM:(hd + 1) * HEAD_DIM] = (
                    o_s[g * WINDOW:(g + 1) * WINDOW].astype(BF16))
        o = oscr_ref[...]
        o_ref[...] = o
        x1_ref[...] = x_ref[...] + _nn(o, _mat(wo_ref)) + bo_ref[...]

    kv_blocks = tt // WINDOW
    return pl.pallas_call(
        body, name=name, grid=(t_all // tt,),
        in_specs=[pl.BlockSpec(memory_space=pltpu.SMEM),
                  pl.BlockSpec((tt, width), lambda i: (i, 0)),
                  pl.BlockSpec((WINDOW, 2 * kvw), lambda i: (jnp.maximum(i * kv_blocks - 1, 0), d // (2 * kvw))),
                  pl.BlockSpec((tt, d), lambda i: (i, 0)), _rows_spec(w_o), _resident((1, d))],
        out_specs=[pl.BlockSpec((tt, d), lambda i: (i, 0)), pl.BlockSpec((tt, d), lambda i: (i, 0))],
        out_shape=[SDS((t_all, d), F32), SDS((t_all, d), BF16)],
        scratch_shapes=[pltpu.VMEM((tt + WINDOW, 2 * kvw), BF16), pltpu.VMEM((tt, d), BF16),
                        pltpu.VMEM((2, 2 * WINDOW, WINDOW), F32), pltpu.VMEM((2, 2 * WINDOW, GROUP * WINDOW), F32),
                        pltpu.VMEM((2, 2 * WINDOW, GROUP * WINDOW), BF16)],
        compiler_params=_cparams(),
    )(sinks, qkv, qkv, x, w_o.arr, b_o)


def _final_norm_loss(x, gain, target, name):
    t_all, d = x.shape
    tt = min(TOKEN_TILE, t_all)

    def body(x_ref, g_ref, t_ref, dx_ref, dg_ref, loss_ref):
        i = pl.program_id(0)
        xv = x_ref[...]
        r, xh = _rms_parts(xv)
        gain_v = g_ref[...]
        e = xh * gain_v - t_ref[...]
        dy = e * (1.0 / d)
        dx_ref[...] = _rms_backward(dy, xh, r, gain_v, 0.0)

        @pl.when(i == 0)
        def _():
            dg_ref[...] = jnp.zeros_like(dg_ref)
            loss_ref[...] = jnp.zeros_like(loss_ref)

        dg_ref[...] += jnp.sum(dy * xh, axis=0, keepdims=True)
        loss_ref[...] += (0.5 / d) * jnp.sum(e * e, axis=0, keepdims=True)

    return pl.pallas_call(
        body, name=name, grid=(t_all // tt,),
        in_specs=[pl.BlockSpec((tt, d), lambda i: (i, 0)), _resident((1, d)), pl.BlockSpec((tt, d), lambda i: (i, 0))],
        out_specs=[pl.BlockSpec((tt, d), lambda i: (i, 0)), pl.BlockSpec((1, d), lambda i: (0, 0)),
                   pl.BlockSpec((1, d), lambda i: (0, 0))],
        out_shape=[SDS((t_all, d), F32), SDS((1, d), F32), SDS((1, d), F32)],
        compiler_params=_cparams(),
    )(x, gain, target)


def _bwd_ffn_inner(dx2, gate, s_act, uds, w_conv, w_down, seq, name):
    t_all, d = dx2.shape
    f = w_down.n * N_DEV
    tt = _token_tile(seq) // 2
    tps = seq // tt
    nt = t_all // tt

    def body(dx_ref, g_ref, s_ref, uds_ref, wc_ref, wd_ref, dgu_ref, dwc_ref, aext_ref):
        i = pl.program_id(0)
        ti = nt - 1 - i
        da = _nt(dx_ref[...].astype(BF16), _mat(wd_ref))
        g = g_ref[...].astype(F32)
        dgc = da * uds_ref[...].astype(F32)

        @pl.when(ti % tps == tps - 1)
        def _():
            aext_ref[...] = jnp.zeros_like(aext_ref)

        dg, sh1, sh2 = _anticausal_conv3(aext_ref, dgc, wc_ref)
        dgu_ref[:, :f] = dg.astype(BF16)
        dgu_ref[:, f:] = (da * s_ref[...].astype(F32)).astype(BF16)

        @pl.when(i == 0)
        def _():
            dwc_ref[...] = jnp.zeros_like(dwc_ref)

        dwc_ref[0:1, :] += jnp.sum(g * sh2, axis=0, keepdims=True)
        dwc_ref[1:2, :] += jnp.sum(g * sh1, axis=0, keepdims=True)
        dwc_ref[2:3, :] += jnp.sum(g * dgc, axis=0, keepdims=True)

    rev = lambda i: (nt - 1 - i, 0)
    return pl.pallas_call(
        body, name=name, grid=(nt,),
        in_specs=[pl.BlockSpec((tt, d), rev)] + [pl.BlockSpec((tt, f), rev)] * 3 + [_resident((3, f)), _rows_spec(w_down)],
        out_specs=[pl.BlockSpec((tt, 2 * f), rev), pl.BlockSpec((8, f), lambda i: (0, 0))],
        out_shape=[SDS((t_all, 2 * f), BF16), SDS((8, f), F32)],
        scratch_shapes=[pltpu.VMEM((SUBLANES, f), F32)],
        compiler_params=_cparams(),
    )(dx2, gate, s_act, uds, w_conv, w_down.arr)


def _bwd_conv_inner(dx1, bcv, cc, w_conv, w_out, seq, name):
    t_all, d = dx1.shape
    tt = _token_tile(seq)
    tps = seq // tt
    nt = t_all // tt

    def body(dx_ref, bcv_ref, cc_ref, wc_ref, wout_ref, dbcv_ref, dwc_ref, aext_ref):
        i = pl.program_id(0)
        ti = nt - 1 - i
        dy = _nt(dx_ref[...].astype(BF16), _mat(wout_ref))
        bcv_v = bcv_ref[...].astype(F32)
        b = bcv_v[:, :d]
        c = bcv_v[:, d:2 * d]
        v = bcv_v[:, 2 * d:]
        cv = c * v
        dcc = dy * b

        @pl.when(ti % tps == tps - 1)
        def _():
            aext_ref[...] = jnp.zeros_like(aext_ref)

        dcv, sh1, sh2 = _anticausal_conv3(aext_ref, dcc, wc_ref)
        dbcv_ref[:, :d] = (dy * cc_ref[...].astype(F32)).astype(BF16)
        dbcv_ref[:, d:2 * d] = (dcv * v).astype(BF16)
        dbcv_ref[:, 2 * d:] = (dcv * c).astype(BF16)

        @pl.when(i == 0)
        def _():
            dwc_ref[...] = jnp.zeros_like(dwc_ref)

        dwc_ref[0:1, :] += jnp.sum(cv * sh2, axis=0, keepdims=True)
        dwc_ref[1:2, :] += jnp.sum(cv * sh1, axis=0, keepdims=True)
        dwc_ref[2:3, :] += jnp.sum(cv * dcc, axis=0, keepdims=True)

    return pl.pallas_call(
        body, name=name, grid=(nt,),
        in_specs=[pl.BlockSpec((tt, d), lambda i: (nt - 1 - i, 0)),
                  pl.BlockSpec((tt, 3 * d), lambda i: (nt - 1 - i, 0)),
                  pl.BlockSpec((tt, d), lambda i: (nt - 1 - i, 0)),
                  _resident((3, d)), _rows_spec(w_out)],
        out_specs=[pl.BlockSpec((tt, 3 * d), lambda i: (nt - 1 - i, 0)), pl.BlockSpec((8, d), lambda i: (0, 0))],
        out_shape=[SDS((t_all, 3 * d), BF16), SDS((8, d), F32)],
        scratch_shapes=[pltpu.VMEM((SUBLANES, d), F32)],
        compiler_params=_cparams(),
    )(dx1, bcv, cc, w_conv, w_out.arr)


def _bwd_attention_inner(dx1, qkv, sinks, w_o, cos_t, sin_t, seq, name):
    t_all, d = dx1.shape
    width = qkv.shape[1]
    kvw = (width - d) // 2
    n_kv = kvw // HEAD_DIM
    tt = _token_tile(seq)
    tps = seq // tt
    nt = t_all // tt
    nblk = tt // WINDOW
    scale = HEAD_DIM ** -0.5

    def body(sink_ref, dx_ref, qkv_ref, kvp_ref, cos_ref, sin_ref, wo_ref,
             dqkv_ref, dsink_ref, dbqkv_ref, dbo_ref,
             kvext_ref, dkvext_ref, carry_ref, dq_ref, do_ref, bias_ref, s_ref, dp_ref, p_ref, ds_ref):
        i = pl.program_id(0)
        ti = nt - 1 - i
        dxv = dx_ref[...]
        do_ref[...] = _nt(dxv.astype(BF16), _mat(wo_ref)).astype(BF16)
        kvext_ref[0:WINDOW, :] = kvp_ref[...]
        kvext_ref[WINDOW:, :] = qkv_ref[:, d:]
        dkvext_ref[...] = jnp.zeros_like(dkvext_ref)

        @pl.when(i == 0)
        def _():
            base, first = _band_bias()
            bias_ref[0], bias_ref[1] = base.T, first.T
            carry_ref[...] = jnp.zeros_like(carry_ref)
            dsink_ref[...] = jnp.zeros_like(dsink_ref)
            dbqkv_ref[...] = jnp.zeros_like(dbqkv_ref)
            dbo_ref[...] = jnp.zeros_like(dbo_ref)

        at_seq_start = (ti % tps == 0).astype(jnp.int32)
        head_lane = lax.broadcasted_iota(jnp.int32, (1, LANES), 1)
        dsink = jnp.zeros((1, LANES), F32)
        for n in range(nblk):
            for kh in range(n_kv):
                buf = (n * n_kv + kh) % 2
                qs = _stack_heads(qkv_ref, n * WINDOW, kh)
                dos = _stack_heads(do_ref, n * WINDOW, kh)
                kcols = slice(kh * HEAD_DIM, (kh + 1) * HEAD_DIM)
                vcols = slice(kvw + kh * HEAD_DIM, kvw + (kh + 1) * HEAD_DIM)
                band = slice(n * WINDOW, (n + 2) * WINDOW)
                kb = kvext_ref[band, kcols]
                vb = kvext_ref[band, vcols]
                s_ref[buf] = _nt(kb, qs)
                dp_ref[buf] = _nt(vb, dos)
                bias = bias_ref[at_seq_start if n == 0 else 0]
                for g in range(GROUP):
                    hd = kh * GROUP + g
                    cols = slice(g * WINDOW, (g + 1) * WINDOW)
                    sink = sink_ref[hd]
                    sv = s_ref[buf, :, cols] + bias
                    m = jnp.maximum(jnp.max(sv, axis=0, keepdims=True), sink)
                    p = jnp.exp(sv - m)
                    e_sink = jnp.exp(sink - m)
                    inv = 1.0 / (jnp.sum(p, axis=0, keepdims=True) + e_sink)
                    probs = p * inv
                    dp = dp_ref[buf, :, cols]
                    dsum = jnp.sum(probs * dp, axis=0, keepdims=True)
                    p_ref[buf, :, cols] = probs.astype(BF16)
                    ds_ref[buf, :, cols] = (probs * (dp - dsum)).astype(BF16)
                    dsink = dsink - jnp.where(head_lane == hd, jnp.sum(e_sink * inv * dsum), 0.0)
                ds_t = ds_ref[buf]
                dkvext_ref[band, vcols] += _nn(p_ref[buf], dos)
                dkvext_ref[band, kcols] += _nn(ds_t, qs)
                dq_s = _tn(kb, ds_t).T
                for g in range(GROUP):
                    hd = kh * GROUP + g
                    dq_ref[n * WINDOW:(n + 1) * WINDOW, hd * HEAD_DIM:(hd + 1) * HEAD_DIM] = dq_s[g * WINDOW:(g + 1) * WINDOW]
        dsink_ref[0:1, :] += dsink
        dkvext_ref[tt:tt + WINDOW, :] += carry_ref[...]
        carry_ref[...] = dkvext_ref[0:WINDOW, :]

        cosv = cos_ref[...]
        sinv = sin_ref[...]
        lane_lo = (lax.broadcasted_iota(jnp.int32, (tt, LANES), 1) % HEAD_DIM) < HEAD_DIM // 2
        for s in range((d + kvw) // LANES):
            if s * LANES < d:
                dy = dq_ref[:, s * LANES:(s + 1) * LANES] * scale
            else:
                dy = dkvext_ref[WINDOW:, s * LANES - d:(s + 1) * LANES - d]
            dpre = dy * cosv - _rope_partner(dy, lane_lo) * sinv
            dqkv_ref[:, s * LANES:(s + 1) * LANES] = dpre.astype(BF16)
            dbqkv_ref[0:1, s * LANES:(s + 1) * LANES] += jnp.sum(dpre, axis=0, keepdims=True)
        dv = dkvext_ref[WINDOW:, kvw:]
        dqkv_ref[:, d + kvw:] = dv.astype(BF16)
        dbqkv_ref[0:1, d + kvw:] += jnp.sum(dv, axis=0, keepdims=True)
        dbo_ref[...] += jnp.sum(dxv, axis=0, keepdims=True)

    kv_blocks = tt // WINDOW
    return pl.pallas_call(
        body, name=name, grid=(nt,),
        in_specs=[pl.BlockSpec(memory_space=pltpu.SMEM),
                  pl.BlockSpec((tt, d), lambda i: (nt - 1 - i, 0)),
                  pl.BlockSpec((tt, width), lambda i: (nt - 1 - i, 0)),
                  pl.BlockSpec((WINDOW, 2 * kvw), lambda i: (jnp.maximum((nt - 1 - i) * kv_blocks - 1, 0), d // (2 * kvw))),
                  pl.BlockSpec((tt, LANES), lambda i: ((nt - 1 - i) % tps, 0)),
                  pl.BlockSpec((tt, LANES), lambda i: ((nt - 1 - i) % tps, 0)),
                  _rows_spec(w_o)],
        out_specs=[pl.BlockSpec((tt, width), lambda i: (nt - 1 - i, 0)), pl.BlockSpec((8, LANES), lambda i: (0, 0)),
                   pl.BlockSpec((1, width), lambda i: (0, 0)), pl.BlockSpec((1, d), lambda i: (0, 0))],
        out_shape=[SDS((t_all, width), BF16), SDS((8, LANES), F32), SDS((1, width), F32), SDS((1, d), F32)],
        scratch_shapes=[pltpu.VMEM((tt + WINDOW, 2 * kvw), BF16), pltpu.VMEM((tt + WINDOW, 2 * kvw), F32),
                        pltpu.VMEM((WINDOW, 2 * kvw), F32), pltpu.VMEM((tt, d), F32), pltpu.VMEM((tt, d), BF16),
                        pltpu.VMEM((2, 2 * WINDOW, WINDOW), F32), pltpu.VMEM((2, 2 * WINDOW, GROUP * WINDOW), F32),
                        pltpu.VMEM((2, 2 * WINDOW, GROUP * WINDOW), F32), pltpu.VMEM((2, 2 * WINDOW, GROUP * WINDOW), BF16),
                        pltpu.VMEM((2, 2 * WINDOW, GROUP * WINDOW), BF16)],
        compiler_params=_cparams(),
    )(sinks, dx1, qkv, qkv, cos_t, sin_t, w_o.arr)


def _bwd_dense_norm(dy, w_t, x, gain, dres, name):
    t_all, d = x.shape
    n = dy.shape[1]
    tt = min(TOKEN_TILE, t_all)

    def body(dy_ref, w_ref, x_ref, g_ref, dres_ref, dx_ref, h_ref, dg_ref):
        i = pl.program_id(0)
        dh = _nn(dy_ref[...], _mat(w_ref))
        r, xh = _rms_parts(x_ref[...])
        gain_v = g_ref[...]
        h_ref[...] = (xh * gain_v).astype(BF16)
        dx_ref[...] = _rms_backward(dh, xh, r, gain_v, dres_ref[...])

        @pl.when(i == 0)
        def _():
            dg_ref[...] = jnp.zeros_like(dg_ref)

        dg_ref[...] += jnp.sum(dh * xh, axis=0, keepdims=True)

    return pl.pallas_call(
        body, name=name, grid=(t_all // tt,),
        in_specs=[pl.BlockSpec((tt, n), lambda i: (i, 0)), _rows_spec(w_t), pl.BlockSpec((tt, d), lambda i: (i, 0)),
                  _resident((1, d)), pl.BlockSpec((tt, d), lambda i: (i, 0))],
        out_specs=[pl.BlockSpec((tt, d), lambda i: (i, 0)), pl.BlockSpec((tt, d), lambda i: (i, 0)),
                   pl.BlockSpec((1, d), lambda i: (0, 0))],
        out_shape=[SDS((t_all, d), F32), SDS((t_all, d), BF16), SDS((1, d), F32)],
        compiler_params=_cparams(),
    )(dy, w_t.arr, x, gain, dres)


def _tn_matmul(a, b, dest, name):
    t_all, m = a.shape
    d = b.shape[1]
    n = dest.n
    assert m == N_DEV * n and dest.off % n == 0
    k = max(kk for kk in (1, 2, 4, 8) if kk * n <= max(n, 1536))
    tm = k * n
    tt = min(TN_TOKEN_TILE, t_all)
    n_t = t_all // tt
    fresh = not hasattr(dest.arr, "dtype")

    def body(a_ref, b_ref, *rest):
        o_ref, acc_ref = rest[-2:]
        t = pl.program_id(1)

        @pl.when(t == 0)
        def _():
            acc_ref[...] = jnp.zeros_like(acc_ref)

        acc_ref[...] += _tn(a_ref[...], b_ref[...].astype(BF16))

        @pl.when(t == n_t - 1)
        def _():
            o_ref[...] = acc_ref[...].astype(BF16).reshape(k, n, d)

    block = dest.off // n
    return pl.pallas_call(
        body, name=name, grid=(m // tm, n_t),
        in_specs=[pl.BlockSpec((tt, tm), lambda j, t: (t, j)), pl.BlockSpec((tt, d), lambda j, t: (t, 0))] + ([] if fresh else [ANY]),
        out_specs=pl.BlockSpec((k, n, d), lambda j, t: (j, block, 0)),
        out_shape=SDS(tuple(dest.arr) if fresh else dest.arr.shape, BF16),
        scratch_shapes=[pltpu.VMEM((tm, d), F32)],
        input_output_aliases={} if fresh else {2: 0},
        compiler_params=_cparams(2),
    )(*((a, b) if fresh else (a, b, dest.arr)))


def _my_place():
    return lax.axis_index("x"), lax.axis_index("y"), lax.axis_index("c")


def _other_chips(x, y):
    return [(1 - x, y), (x, 1 - y), (1 - x, 1 - y)]


def _all_gather(blocks, name):
    n_arr = len(blocks)

    def body(*refs):
        in_refs = refs[:n_arr]
        out_refs = refs[n_arr:2 * n_arr]
        send_sems, recv_sems, local_sems = refs[2 * n_arr:]
        x, y, c = _my_place()
        me, sibling = (x, y, c), (x, y, 1 - c)
        chips = _other_chips(x, y)

        def slot(a, place):
            px, py, pc = place
            return out_refs[a].at[4 * px + 2 * py + pc]

        def copy(a, k, block, to, src=None):
            return pltpu.make_async_remote_copy(
                src_ref=slot(a, block) if src is None else src, dst_ref=slot(a, block),
                send_sem=send_sems.at[a, k], recv_sem=recv_sems.at[a, k], device_id=to, device_id_type=MESH)

        started = []
        local = []
        for a in range(n_arr):
            mine = pltpu.make_async_copy(in_refs[a], slot(a, me), local_sems.at[a])
            mine.start()
            local.append(mine)
            first = [copy(a, 0, me, sibling, src=in_refs[a])]
            first += [copy(a, 1 + j, me, (*chip, c), src=in_refs[a]) for j, chip in enumerate(chips)]
            for cp in first:
                cp.start()
            started += first
        for a in range(n_arr):
            for j, chip in enumerate(chips):
                copy(a, 1 + j, (*chip, c), me).wait_recv()
                passed = copy(a, 4 + j, (*chip, c), sibling)
                passed.start()
                started.append(passed)
        for a in range(n_arr):
            copy(a, 0, sibling, me).wait_recv()
            for j, chip in enumerate(chips):
                copy(a, 4 + j, (*chip, 1 - c), me).wait_recv()
        for cp in started:
            cp.wait_send()
        for mine in local:
            mine.wait()

    return pl.pallas_call(
        body, name=name,
        in_specs=[ANY] * n_arr, out_specs=[ANY] * n_arr,
        out_shape=[SDS((N_DEV,) + b.shape, b.dtype) for b in blocks],
        scratch_shapes=[pltpu.SemaphoreType.DMA((n_arr, 7)), pltpu.SemaphoreType.DMA((n_arr, 7)),
                        pltpu.SemaphoreType.DMA((n_arr,))],
    )(*blocks)


def _peer_of(k, x, y, c):
    return x ^ ((k >> 2) & 1), y ^ ((k >> 1) & 1), c ^ (k & 1)


HBM = pl.BlockSpec(memory_space=pltpu.HBM)
SEM = pl.BlockSpec(memory_space=pltpu.SEMAPHORE)
DATAFLOW_EFFECT = pltpu.SideEffectType.DATAFLOW_SIDE_EFFECTING


def _peer_copies(src_ref, land_ref, send_sems, recv_sems, per_peer):
    x, y, c = _my_place()
    me = 4 * x + 2 * y + c
    copies = []
    for k in range(1, N_DEV):
        px, py, pc = _peer_of(k, x, y, c)
        peer = 4 * px + 2 * py + pc
        copies.append(pltpu.make_async_remote_copy(
            src_ref=src_ref.at[peer] if per_peer else src_ref, dst_ref=land_ref.at[me],
            send_sem=send_sems.at[k - 1], recv_sem=recv_sems.at[k - 1], device_id=(px, py, pc), device_id_type=MESH))
    own = pltpu.make_async_copy(src_ref.at[me] if per_peer else src_ref, land_ref.at[me], send_sems.at[N_DEV - 1])
    return copies, own


def _exchange_start(src, after, per_peer, name):
    rows, d = src.shape[-2:]

    def body(src_ref, land_ref, after_ref, send_sems, recv_sems, src_thru, land_thru, token):
        copies, own = _peer_copies(src_ref, land_ref, send_sems, recv_sems, per_peer)
        for cp in copies:
            cp.start()
        own.start()
        token[...] = jnp.zeros_like(token)

    return pl.pallas_call(
        body, name=name,
        out_shape=(pltpu.SemaphoreType.DMA((N_DEV,)), pltpu.SemaphoreType.DMA((N_DEV - 1,)), pltpu.HBM(src.shape, src.dtype),
                   pltpu.HBM((N_DEV, rows, d), src.dtype), SDS((SUBLANES, LANES), F32)),
        in_specs=(HBM, HBM, ANY), out_specs=(SEM, SEM, HBM, HBM, pl.BlockSpec(memory_space=pltpu.VMEM)),
        input_output_aliases={0: 2, 1: 3},
        compiler_params=pltpu.CompilerParams(has_side_effects=DATAFLOW_EFFECT),
    )(pltpu.with_memory_space_constraint(src, pltpu.HBM),
      pltpu.with_memory_space_constraint(lax.empty((N_DEV, rows, d), src.dtype), pltpu.HBM), after)


def _exchange_wait(started, after, per_peer, name):
    send_sems, recv_sems, src_thru, land_thru, _ = started

    def body(src_ref, land_ref, send_sems, recv_sems, after_ref, src_out, land_out):
        copies, own = _peer_copies(src_ref, land_ref, send_sems, recv_sems, per_peer)
        for cp in copies:
            cp.wait_send()
            cp.wait_recv()
        own.wait()

    return pl.pallas_call(
        body, name=name,
        out_shape=(pltpu.HBM(src_thru.shape, src_thru.dtype), pltpu.HBM(land_thru.shape, land_thru.dtype)),
        in_specs=(HBM, HBM, SEM, SEM, ANY), out_specs=(HBM, HBM), input_output_aliases={0: 0, 1: 1},
        compiler_params=pltpu.CompilerParams(has_side_effects=DATAFLOW_EFFECT),
    )(src_thru, land_thru, send_sems, recv_sems, after)


def _sum_slots(slots, name):
    _, rows, d = slots.shape
    tr = _largest_divisor(rows, 512, 16)

    def body(s_ref, o_ref):
        acc = s_ref[0].astype(F32)
        for dev in range(1, N_DEV):
            acc = acc + s_ref[dev].astype(F32)
        o_ref[...] = acc

    return pl.pallas_call(
        body, name=name, grid=(rows // tr,),
        in_specs=[pl.BlockSpec((N_DEV, tr, d), lambda r: (0, r, 0))], out_specs=pl.BlockSpec((tr, d), lambda r: (r, 0)),
        out_shape=SDS((rows, d), F32), compiler_params=_cparams(),
    )(slots)


def _all_reduce_small(part, loss_rows, name):
    rows, lanes = part.shape
    lo, hi = loss_rows

    def body(x_ref, out_ref, loss_ref, gath_ref, send_sems, recv_sems):
        x, y, c = _my_place()
        me = 4 * x + 2 * y + c
        gath_ref[me] = x_ref[...]
        copies = []
        for k in range(1, N_DEV):
            peer = (x ^ ((k >> 2) & 1), y ^ ((k >> 1) & 1), c ^ (k & 1))
            cp = pltpu.make_async_remote_copy(
                src_ref=x_ref, dst_ref=gath_ref.at[me], send_sem=send_sems.at[k - 1], recv_sem=recv_sems.at[k - 1],
                device_id=peer, device_id_type=MESH)
            cp.start()
            copies.append(cp)
        for cp in copies:
            cp.wait_recv()
        for cp in copies:
            cp.wait_send()
        acc = gath_ref[0]
        for dev in range(1, N_DEV):
            acc = acc + gath_ref[dev]
        out_ref[...] = acc
        loss_ref[...] = jnp.full(loss_ref.shape, jnp.sum(acc[lo:hi, :]), F32)

    vmem = pl.BlockSpec(memory_space=pltpu.VMEM)
    return pl.pallas_call(
        body, name=name, in_specs=[vmem], out_specs=[vmem, vmem],
        out_shape=[SDS((rows, lanes), F32), SDS((SUBLANES, LANES), F32)],
        scratch_shapes=[pltpu.VMEM((N_DEV, rows, lanes), F32), pltpu.SemaphoreType.DMA((N_DEV - 1,)),
                        pltpu.SemaphoreType.DMA((N_DEV - 1,))],
    )(part)


def _adamw(w, g, m, v, name):
    rows, cols = w.shape
    tr = rows if rows % SUBLANES else _largest_divisor(rows, 512, SUBLANES)

    def body(w_ref, g_ref, m_ref, v_ref, d_ref, nm_ref, nv_ref):
        gv = g_ref[...]
        nm = ADAM_B1 * m_ref[...] + (1.0 - ADAM_B1) * gv
        nv = ADAM_B2 * v_ref[...] + (1.0 - ADAM_B2) * (gv * gv)
        m_hat = nm / (1.0 - ADAM_B1 ** ADAM_STEP)
        v_hat = nv / (1.0 - ADAM_B2 ** ADAM_STEP)
        d_ref[...] = -ADAM_LR * (m_hat / (jnp.sqrt(v_hat) + ADAM_EPS) + ADAM_WD * w_ref[...])
        nm_ref[...] = nm
        nv_ref[...] = nv

    spec = pl.BlockSpec((tr, cols), lambda i: (i, 0))
    return pl.pallas_call(
        body, name=name, grid=(rows // tr,), in_specs=[spec] * 4, out_specs=[spec] * 3,
        out_shape=[SDS((rows, cols), F32)] * 3, compiler_params=_cparams(),
    )(w, g, m, v)


def _adamw_nd(w, g, m, v, name):
    shape = w.shape
    two_d = (1, shape[0]) if len(shape) == 1 else (-1, shape[-1])
    outs = _adamw(w.reshape(two_d), g.reshape(two_d), m.reshape(two_d), v.reshape(two_d), name)
    return [o.reshape(shape) for o in outs]


def _rope_tables(seq):
    pos = jnp.arange(seq, dtype=F32)
    inv_freq = 1.0 / (ROPE_THETA ** (jnp.arange(0, HEAD_DIM, 2, dtype=F32) / HEAD_DIM))
    ang = pos[:, None] * inv_freq[None, :]
    cos, sin = jnp.cos(ang), jnp.sin(ang)
    reps = LANES // HEAD_DIM
    cos_t = jnp.tile(jnp.concatenate([cos, cos], axis=1), (1, reps))
    sin_t = jnp.tile(jnp.concatenate([-sin, sin], axis=1), (1, reps))
    return cos_t, sin_t


def _flat_pad(a):
    flat = a.reshape(1, -1)
    pad = (-flat.shape[1]) % LANES
    return jnp.pad(flat, ((0, 0), (0, pad))) if pad else flat


def kernel(x, norm_mix, norm_ffn, norm_final, conv_w_in, conv_w_conv, conv_w_out, attn_w_qkv, attn_b_qkv, attn_sinks, attn_w_o, attn_b_o, ffn_w_in, ffn_w_conv, ffn_w_down, loss_target, m_norm_mix, m_norm_ffn, m_norm_final, m_conv_w_in, m_conv_w_conv, m_conv_w_out, m_attn_w_qkv, m_attn_b_qkv, m_attn_sinks, m_attn_w_o, m_attn_b_o, m_ffn_w_in, m_ffn_w_conv, m_ffn_w_down, v_norm_mix, v_norm_ffn, v_norm_final, v_conv_w_in, v_conv_w_conv, v_conv_w_out, v_attn_w_qkv, v_attn_b_qkv, v_attn_sinks, v_attn_w_o, v_attn_b_o, v_ffn_w_in, v_ffn_w_conv, v_ffn_w_down):
    b_loc, seq, d = x.shape
    depth = norm_mix.shape[0]
    n_conv, n_attn = conv_w_in.shape[0], attn_w_qkv.shape[0]
    t_all = b_loc * seq
    my_x, my_y, my_c = _my_place()

    me = 4 * my_x + 2 * my_y + my_c

    groups = []
    for i in range(depth):
        j = i // 2
        if i % 2 == 0:
            mix = [("conv_w_in", j, True, conv_w_in[j].T), ("conv_w_out", j, False, conv_w_out[j])]
        else:
            mix = [("attn_w_qkv", j, True, attn_w_qkv[j].T), ("attn_w_o", j, False, attn_w_o[j])]
        groups.append((("mix", i), mix))
        groups.append((("ffn", i), [("ffn_w_in", i, True, ffn_w_in[i].T), ("ffn_w_down", i, False, ffn_w_down[i])]))
    order = [key for key, _ in groups]
    members_of = dict(groups)

    def layout(key):
        offs, o = [], 0
        for _, _, _, shard in members_of[key]:
            n = shard.shape[0]
            o = -(-o // n) * n
            offs.append(o)
            o += n
        return offs, o

    small = jnp.concatenate([_flat_pad(conv_w_conv), _flat_pad(ffn_w_conv), _flat_pad(attn_b_qkv), _flat_pad(attn_b_o)], axis=1)
    (small_g,) = _all_gather([small], "gather_small")

    gather_started = {}

    def start_gather(idx, after):
        if idx >= len(order):
            return 0.0
        key = order[idx]
        offs, total = layout(key)
        pieces, o = [], 0
        for (_, _, _, shard), off in zip(members_of[key], offs):
            if off > o:
                pieces.append(jnp.zeros((off - o, d), shard.dtype))
            pieces.append(shard)
            o = off + shard.shape[0]
        pack = jnp.concatenate(pieces, axis=0).astype(BF16)
        gather_started[key] = _exchange_start(pack, after, False, f"gather_start_{key[0]}_{key[1]}")
        return gather_started[key][4][0, 0]

    weights = {}

    def finish_gather(key, after):
        _, land = _exchange_wait(gather_started[key], after, False, f"gather_wait_{key[0]}_{key[1]}")
        for (wname, layer, _, shard), off in zip(members_of[key], layout(key)[0]):
            weights[(wname, layer)] = _Rows(land, off, shard.shape[0])

    def take_small(o, shape):
        size = shape[0] * shape[1] * shape[2]
        blk = small_g[:, 0, o:o + size].reshape((N_DEV,) + shape)
        return jnp.moveaxis(blk, 0, 2).reshape(shape[0], shape[1], N_DEV * shape[2])

    so = 0
    wc_conv_full = take_small(so, conv_w_conv.shape); so += _flat_pad(conv_w_conv).shape[1]
    wc_ffn_full = take_small(so, ffn_w_conv.shape); so += _flat_pad(ffn_w_conv).shape[1]
    b_qkv_full = take_small(so, (n_attn, 1, attn_b_qkv.shape[1]))[:, 0]; so += _flat_pad(attn_b_qkv).shape[1]
    b_o_full = take_small(so, (n_attn, 1, attn_b_o.shape[1]))[:, 0]

    cos_t, sin_t = _rope_tables(seq)

    xs = [x.reshape(t_all, d)]
    saved = []
    token = start_gather(0, small_g) + start_gather(1, small_g)
    for i in range(depth):
        j = i // 2
        if i > 0:
            token = start_gather(2 * i + 2, xs[-1])
        gain_mix = norm_mix[i][None, :] + token
        finish_gather(("mix", i), gain_mix if i == 0 else xs[-1])
        if i % 2 == 0:
            x1, *mix_saved = _fwd_conv_mixer(xs[-1], gain_mix, weights[("conv_w_in", j)], wc_conv_full[j],
                                             weights[("conv_w_out", j)], seq, f"fwd_conv_{i}")
        else:
            qkv = _fwd_qkv(xs[-1], gain_mix, weights[("attn_w_qkv", j)], b_qkv_full[j][None, :], cos_t, sin_t, seq,
                           f"fwd_qkv_{i}")
            x1, o = _fwd_attention(qkv, xs[-1], attn_sinks[j], weights[("attn_w_o", j)], b_o_full[j][None, :], seq,
                                   f"fwd_attn_{i}")
            mix_saved = (qkv, o)
        token = start_gather(2 * i + 3, x1) + (start_gather(2, x1) if i == 0 else 0.0)
        gain_ffn = norm_ffn[i][None, :] + token
        finish_gather(("ffn", i), gain_ffn)
        x2, *ffn_saved = _fwd_ffn(x1, gain_ffn, weights[("ffn_w_in", i)], wc_ffn_full[i], weights[("ffn_w_down", i)],
                                  seq, f"fwd_ffn_{i}")
        saved.append((xs[-1], x1, mix_saved, ffn_saved))
        xs.append(x2)
        token = 0.0

    dx, dg_final, loss_lanes = _final_norm_loss(xs[-1], norm_final[None, :], loss_target.reshape(t_all, d), "loss_head")

    dg_mix, dg_ffn = [None] * depth, [None] * depth
    dwc_conv, dwc_ffn = [None] * n_conv, [None] * depth
    db_qkv, db_o, dsinks = [None] * n_attn, [None] * n_attn, [None] * n_attn
    scatter_started = {}

    def weight_grads(key, operands):
        offs, total = layout(key)
        parts = (N_DEV, total, d)
        for (wname, layer, _, shard), off, (a, b) in zip(members_of[key], offs, operands):
            parts = _tn_matmul(a, b, _Rows(parts, off, shard.shape[0]), f"dw_{wname}_{layer}")
        scatter_started[key] = _exchange_start(parts, operands[0][1], True, f"scatter_start_{key[0]}_{key[1]}")
        return scatter_started[key][4][0, 0]

    token = 0.0
    for i in reversed(range(depth)):
        j = i // 2
        x0, x1, mix_saved, (gate, s_act, uds, act) = saved[i]
        dgu, dwc = _bwd_ffn_inner(dx, gate, s_act, uds, wc_ffn_full[i] + token, weights[("ffn_w_down", i)], seq, f"bwd_ffn_{i}")
        dwc_ffn[i] = dwc[:3]
        dx1, h2, dg_ffn[i] = _bwd_dense_norm(dgu, weights[("ffn_w_in", i)], x1, norm_ffn[i][None, :], dx, f"bwd_ffn_norm_{i}")
        token = weight_grads(("ffn", i), [(dgu, h2), (act, dx)])
        if i % 2 == 0:
            bcv, cc, y = mix_saved
            dbcv, dwc = _bwd_conv_inner(dx1, bcv, cc, wc_conv_full[j] + token, weights[("conv_w_out", j)], seq, f"bwd_conv_{i}")
            dwc_conv[j] = dwc[:3]
            dx, h, dg_mix[i] = _bwd_dense_norm(dbcv, weights[("conv_w_in", j)], x0, norm_mix[i][None, :], dx1,
                                               f"bwd_conv_norm_{i}")
            token = weight_grads(("mix", i), [(dbcv, h), (y, dx1)])
        else:
            qkv, o = mix_saved
            dqkv, dsk, dbq, dbo = _bwd_attention_inner(dx1, qkv, attn_sinks[j] + token, weights[("attn_w_o", j)], cos_t, sin_t,
                                                       seq, f"bwd_attn_{i}")
            dsinks[j], db_qkv[j], db_o[j] = dsk[0:1, :attn_sinks.shape[1]], dbq, dbo
            dx, h, dg_mix[i] = _bwd_dense_norm(dqkv, weights[("attn_w_qkv", j)], x0, norm_mix[i][None, :], dx1,
                                               f"bwd_attn_norm_{i}")
            token = weight_grads(("mix", i), [(dqkv, h), (o, dx1)])
    grad_x = dx.reshape(b_loc, seq, d)

    reduced = {}

    def finish_scatter(key, after):
        _, land = _exchange_wait(scatter_started[key], after, True, f"scatter_wait_{key[0]}_{key[1]}")
        total = _sum_slots(land, f"scatter_sum_{key[0]}_{key[1]}")
        for (wname, layer, transposed, shard), off in zip(members_of[key], layout(key)[0]):
            rows = total[off:off + shard.shape[0]]
            reduced[(wname, layer)] = rows.T if transposed else rows

    last_key = order[0]
    for key in reversed(order[1:]):
        finish_scatter(key, dx)

    small_parts = [jnp.concatenate(dg_mix, axis=0), jnp.concatenate(dg_ffn, axis=0), dg_final,
                   jnp.stack(dwc_conv), jnp.stack(dwc_ffn), jnp.concatenate(db_qkv, axis=0), jnp.concatenate(db_o, axis=0),
                   jnp.concatenate(dsinks, axis=0), loss_lanes]
    flats = [_flat_pad(p) for p in small_parts]
    bounds = []
    so = 0
    for fl in flats:
        bounds.append((so, so + fl.shape[1]))
        so += fl.shape[1]
    small_rows = so // LANES
    pad_rows = (-small_rows) % SUBLANES
    part_small = jnp.pad(jnp.concatenate(flats, axis=1).reshape(small_rows, LANES), ((0, pad_rows), (0, 0)))
    loss_rows = (bounds[-1][0] // LANES, bounds[-1][1] // LANES)
    summed, loss_tile = _all_reduce_small(part_small, loss_rows, "reduce_small")
    summed = summed.reshape(1, -1)

    def small_grad(k, shape):
        lo = bounds[k][0]
        size = 1
        for s_ in shape:
            size *= s_
        return summed[0, lo:lo + size].reshape(shape)

    def my_cols(full, n_local):
        return lax.dynamic_slice_in_dim(full, me * n_local, n_local, axis=full.ndim - 1)

    g_norm_mix = small_grad(0, norm_mix.shape)
    g_norm_ffn = small_grad(1, norm_ffn.shape)
    g_norm_final = small_grad(2, norm_final.shape)
    g_conv_w_conv = my_cols(small_grad(3, (n_conv, 3, d)), conv_w_conv.shape[2])
    g_ffn_w_conv = my_cols(small_grad(4, (depth, 3, ffn_w_conv.shape[2] * N_DEV)), ffn_w_conv.shape[2])
    g_attn_b_qkv = my_cols(small_grad(5, (n_attn, attn_b_qkv.shape[1] * N_DEV)), attn_b_qkv.shape[1])
    g_attn_b_o = my_cols(small_grad(6, (n_attn, d)), attn_b_o.shape[1])
    g_attn_sinks = small_grad(7, attn_sinks.shape)
    loss = loss_tile[0, 0]

    def big_grad(wname, n_layers):
        return jnp.stack([reduced[(wname, layer)] for layer in range(n_layers)])

    grads = {
        "norm_mix": g_norm_mix, "norm_ffn": g_norm_ffn, "norm_final": g_norm_final, "conv_w_conv": g_conv_w_conv,
        "attn_w_qkv": big_grad("attn_w_qkv", n_attn), "attn_b_qkv": g_attn_b_qkv, "attn_sinks": g_attn_sinks,
        "attn_w_o": big_grad("attn_w_o", n_attn), "attn_b_o": g_attn_b_o,
        "ffn_w_in": big_grad("ffn_w_in", depth), "ffn_w_conv": g_ffn_w_conv, "ffn_w_down": big_grad("ffn_w_down", depth),
    }
    params = {
        "norm_mix": (norm_mix, m_norm_mix, v_norm_mix), "norm_ffn": (norm_ffn, m_norm_ffn, v_norm_ffn),
        "norm_final": (norm_final, m_norm_final, v_norm_final), "conv_w_in": (conv_w_in, m_conv_w_in, v_conv_w_in),
        "conv_w_conv": (conv_w_conv, m_conv_w_conv, v_conv_w_conv), "conv_w_out": (conv_w_out, m_conv_w_out, v_conv_w_out),
        "attn_w_qkv": (attn_w_qkv, m_attn_w_qkv, v_attn_w_qkv), "attn_b_qkv": (attn_b_qkv, m_attn_b_qkv, v_attn_b_qkv),
        "attn_sinks": (attn_sinks, m_attn_sinks, v_attn_sinks), "attn_w_o": (attn_w_o, m_attn_w_o, v_attn_w_o),
        "attn_b_o": (attn_b_o, m_attn_b_o, v_attn_b_o), "ffn_w_in": (ffn_w_in, m_ffn_w_in, v_ffn_w_in),
        "ffn_w_conv": (ffn_w_conv, m_ffn_w_conv, v_ffn_w_conv), "ffn_w_down": (ffn_w_down, m_ffn_w_down, v_ffn_w_down),
    }
    names = list(params)
    updates = {}

    def update(wname):
        w, m, v = params[wname]
        updates[wname] = _adamw_nd(w, grads[wname], m, v, f"adamw_{wname}")

    last_names = sorted({wname for wname, _, _, _ in members_of[last_key]})
    for wname in names:
        if wname not in last_names:
            update(wname)
    finish_scatter(last_key, updates["ffn_w_in"][0])
    for wname in last_names:
        grads[wname] = big_grad(wname, params[wname][0].shape[0])
        update(wname)
    return (loss, grad_x, *[grads[wname] for wname in names], *[updates[wname][0] for wname in names],
            *[updates[wname][1] for wname in names], *[updates[wname][2] for wname in names])
```

```python
from typing import NamedTuple

import jax
import jax.numpy as jnp
from jax import lax
from jax.experimental import pallas as pl
from jax.experimental.pallas import tpu as pltpu

F32 = jnp.float32
BF16 = jnp.bfloat16
SDS = jax.ShapeDtypeStruct
MESH = pl.DeviceIdType.MESH
ANY = pl.BlockSpec(memory_space=pl.ANY)

N_DEV = 8
EPS = 1e-5
HEAD_DIM = 64
GROUP = 4
WINDOW = 128
ROPE_THETA = 10000.0
ADAM_LR, ADAM_B1, ADAM_B2, ADAM_EPS, ADAM_WD, ADAM_STEP = 0.001, 0.9, 0.999, 1e-08, 0.01, 10

V7X_VMEM_BYTES = 64 * 1024 * 1024
VMEM_LIMIT_BYTES = V7X_VMEM_BYTES - 8 * 1024 * 1024
LANES = 128
SUBLANES = 8
TOKEN_TILE = 512
TN_TOKEN_TILE = 2048
MASKED_SCORE = -1e30


def _cparams(n_axes=1):
    return pltpu.CompilerParams(dimension_semantics=("arbitrary",) * n_axes, vmem_limit_bytes=VMEM_LIMIT_BYTES)


def _resident(shape):
    zeros = (0,) * len(shape)
    return pl.BlockSpec(shape, lambda *_: zeros, pipeline_mode=pl.Buffered(1))


class _Rows(NamedTuple):
    arr: jax.Array
    off: int
    n: int


def _rows_spec(w):
    assert w.off % w.n == 0
    block = w.off // w.n
    return pl.BlockSpec((N_DEV, w.n, w.arr.shape[2]), lambda *_: (0, block, 0), pipeline_mode=pl.Buffered(1))


def _mat(ref):
    v = ref[...]
    return v.reshape(v.shape[0] * v.shape[1], v.shape[2])


def _token_tile(seq):
    return min(TOKEN_TILE, seq // 2)


def _largest_divisor(m, cap, mult):
    best = None
    for d in range(mult, min(m, cap) + 1, mult):
        if m % d == 0:
            best = d
    return m if best is None else best


def _nt(a, b):
    return lax.dot_general(a, b, (((1,), (1,)), ((), ())), preferred_element_type=F32)


def _nn(a, b):
    return lax.dot_general(a, b, (((1,), (0,)), ((), ())), preferred_element_type=F32)


def _tn(a, b):
    return lax.dot_general(a, b, (((0,), (0,)), ((), ())), preferred_element_type=F32)


def _rms_parts(xv):
    r = lax.rsqrt(jnp.mean(xv * xv, axis=-1, keepdims=True) + EPS)
    return r, xv * r


def _rms_backward(dh, xh, r, gain, dres):
    u = dh * gain
    return dres + r * (u - xh * jnp.mean(u * xh, axis=-1, keepdims=True))


def _shifted_rows(xv, edge, k, down):
    n = xv.shape[0]
    row = lax.broadcasted_iota(jnp.int32, edge.shape, 0)
    if down:
        rolled = pltpu.roll(xv, k, 0)
        head = jnp.where(row < k, pltpu.roll(edge, k, 0), rolled[0:SUBLANES])
        return jnp.concatenate([head, rolled[SUBLANES:]], axis=0)
    rolled = pltpu.roll(xv, n - k, 0)
    tail = jnp.where(row >= SUBLANES - k, pltpu.roll(edge, SUBLANES - k, 0), rolled[n - SUBLANES:])
    return jnp.concatenate([rolled[:n - SUBLANES], tail], axis=0)


def _causal_conv3(edge_ref, xv, w_ref):
    before = edge_ref[...]
    y = (w_ref[2:3, :] * xv + w_ref[1:2, :] * _shifted_rows(xv, before, 1, True)
         + w_ref[0:1, :] * _shifted_rows(xv, before, 2, True))
    edge_ref[...] = xv[xv.shape[0] - SUBLANES:, :]
    return y


def _sigmoid(z):
    return 1.0 / (1.0 + jnp.exp(-z))


def _fwd_conv_mixer(x, gain, w_in_t, w_conv, w_out, seq, name):
    t_all, d = x.shape
    tt = _token_tile(seq)
    tps = seq // tt

    def body(x_ref, g_ref, win_ref, wc_ref, wout_ref, x1_ref, bcv_ref, cc_ref, y_ref, ext_ref):
        i = pl.program_id(0)
        xv = x_ref[...]
        r, xh = _rms_parts(xv)
        h = (xh * g_ref[...]).astype(BF16)
        bcv = _nt(h, _mat(win_ref))
        bcv_ref[...] = bcv.astype(BF16)

        @pl.when(i % tps == 0)
        def _():
            ext_ref[...] = jnp.zeros_like(ext_ref)

        cc = _causal_conv3(ext_ref, bcv[:, d:2 * d] * bcv[:, 2 * d:], wc_ref)
        cc_ref[...] = cc.astype(BF16)
        y = (bcv[:, :d] * cc).astype(BF16)
        y_ref[...] = y
        x1_ref[...] = xv + _nn(y, _mat(wout_ref))

    tile = pl.BlockSpec((tt, d), lambda i: (i, 0))
    return pl.pallas_call(
        body, name=name, grid=(t_all // tt,),
        in_specs=[tile, _resident((1, d)), _rows_spec(w_in_t), _resident((3, d)), _rows_spec(w_out)],
        out_specs=[tile, pl.BlockSpec((tt, 3 * d), lambda i: (i, 0)), tile, tile],
        out_shape=[SDS((t_all, d), F32), SDS((t_all, 3 * d), BF16), SDS((t_all, d), BF16), SDS((t_all, d), BF16)],
        scratch_shapes=[pltpu.VMEM((SUBLANES, d), F32)],
        compiler_params=_cparams(),
    )(x, gain, w_in_t.arr, w_conv, w_out.arr)


def _fwd_ffn(x, gain, w_in_t, w_conv, w_down, seq, name):
    t_all, d = x.shape
    f = w_down.n * N_DEV
    tt = _token_tile(seq) // 2
    tps = seq // tt

    def body(x_ref, g_ref, win_ref, wc_ref, wd_ref, x2_ref, gate_ref, s_ref, uds_ref, a_ref, ext_ref):
        i = pl.program_id(0)
        xv = x_ref[...]
        r, xh = _rms_parts(xv)
        h = (xh * g_ref[...]).astype(BF16)
        gu = _nt(h, _mat(win_ref))
        gate = gu[:, :f]
        u = gu[:, f:]
        gate_ref[...] = gate.astype(BF16)

        @pl.when(i % tps == 0)
        def _():
            ext_ref[...] = jnp.zeros_like(ext_ref)

        gc = _causal_conv3(ext_ref, gate, wc_ref)
        sig = _sigmoid(gc)
        s = gc * sig
        s_ref[...] = s.astype(BF16)
        uds_ref[...] = (u * (sig * (1.0 + gc * (1.0 - sig)))).astype(BF16)
        a = (s * u).astype(BF16)
        a_ref[...] = a
        x2_ref[...] = xv + _nn(a, _mat(wd_ref))

    wide = pl.BlockSpec((tt, f), lambda i: (i, 0))
    return pl.pallas_call(
        body, name=name, grid=(t_all // tt,),
        in_specs=[pl.BlockSpec((tt, d), lambda i: (i, 0)), _resident((1, d)), _rows_spec(w_in_t),
                  _resident((3, f)), _rows_spec(w_down)],
        out_specs=[pl.BlockSpec((tt, d), lambda i: (i, 0)), wide, wide, wide, wide],
        out_shape=[SDS((t_all, d), F32)] + [SDS((t_all, f), BF16)] * 4,
        scratch_shapes=[pltpu.VMEM((SUBLANES, f), F32)],
        compiler_params=_cparams(),
    )(x, gain, w_in_t.arr, w_conv, w_down.arr)


def _rope_partner(xs, lane_lo):
    return jnp.where(lane_lo, pltpu.roll(xs, LANES - HEAD_DIM // 2, 1), pltpu.roll(xs, HEAD_DIM // 2, 1))


def _fwd_qkv(x, gain, w_qkv_t, b_qkv, cos_t, sin_t, seq, name):
    t_all, d = x.shape
    width = w_qkv_t.n * N_DEV
    kvw = (width - d) // 2
    tt = _token_tile(seq)
    tps = seq // tt
    scale = HEAD_DIM ** -0.5

    def body(x_ref, g_ref, w_ref, b_ref, cos_ref, sin_ref, qkv_ref):
        xv = x_ref[...]
        r, xh = _rms_parts(xv)
        h = (xh * g_ref[...]).astype(BF16)
        qkv = _nt(h, _mat(w_ref)) + b_ref[...]
        cosv = cos_ref[...]
        sinv = sin_ref[...]
        lane_lo = (lax.broadcasted_iota(jnp.int32, (tt, LANES), 1) % HEAD_DIM) < HEAD_DIM // 2
        for s in range((d + kvw) // LANES):
            xs = qkv[:, s * LANES:(s + 1) * LANES]
            roped = xs * cosv + _rope_partner(xs, lane_lo) * sinv
            if s * LANES < d:
                roped = roped * scale
            qkv_ref[:, s * LANES:(s + 1) * LANES] = roped.astype(BF16)
        qkv_ref[:, d + kvw:] = qkv[:, d + kvw:].astype(BF16)

    return pl.pallas_call(
        body, name=name, grid=(t_all // tt,),
        in_specs=[pl.BlockSpec((tt, d), lambda i: (i, 0)), _resident((1, d)), _rows_spec(w_qkv_t),
                  _resident((1, width)), pl.BlockSpec((tt, LANES), lambda i: (i % tps, 0)),
                  pl.BlockSpec((tt, LANES), lambda i: (i % tps, 0))],
        out_specs=pl.BlockSpec((tt, width), lambda i: (i, 0)),
        out_shape=SDS((t_all, width), BF16),
        compiler_params=_cparams(),
    )(x, gain, w_qkv_t.arr, b_qkv, cos_t, sin_t)


def _stack_heads(ref, row0, kh):
    return jnp.concatenate(
        [ref[row0:row0 + WINDOW, (kh * GROUP + g) * HEAD_DIM:(kh * GROUP + g + 1) * HEAD_DIM] for g in range(GROUP)],
        axis=0)


def _band_bias():
    r = lax.broadcasted_iota(jnp.int32, (WINDOW, 2 * WINDOW), 0)
    j = lax.broadcasted_iota(jnp.int32, (WINDOW, 2 * WINDOW), 1)
    base = (j > r) & (j <= r + WINDOW)
    return jnp.where(base, 0.0, MASKED_SCORE), jnp.where(base & (j >= WINDOW), 0.0, MASKED_SCORE)


def _fwd_attention(qkv, x, sinks, w_o, b_o, seq, name):
    t_all, d = x.shape
    width = qkv.shape[1]
    kvw = (width - d) // 2
    n_kv = kvw // HEAD_DIM
    tt = _token_tile(seq)
    tps = seq // tt
    nblk = tt // WINDOW

    def body(sink_ref, qkv_ref, kvp_ref, x_ref, wo_ref, bo_ref, x1_ref, o_ref, kvext_ref, oscr_ref, bias_ref, s_ref, p_ref):
        i = pl.program_id(0)

        @pl.when(i == 0)
        def _():
            base, first = _band_bias()
            bias_ref[0], bias_ref[1] = base.T, first.T

        kvext_ref[0:WINDOW, :] = kvp_ref[...]
        kvext_ref[WINDOW:, :] = qkv_ref[:, d:]
        at_seq_start = (i % tps == 0).astype(jnp.int32)
        steps = [(n, kh) for n in range(nblk) for kh in range(n_kv)]

        def scores(step):
            n, kh = steps[step]
            qs = _stack_heads(qkv_ref, n * WINDOW, kh)
            kb = kvext_ref[n * WINDOW:(n + 2) * WINDOW, kh * HEAD_DIM:(kh + 1) * HEAD_DIM]
            s_ref[step % 2] = _nt(kb, qs)

        scores(0)
        for step, (n, kh) in enumerate(steps):
            buf = step % 2
            if step + 1 < len(steps):
                scores(step + 1)
            vb = kvext_ref[n * WINDOW:(n + 2) * WINDOW, kvw + kh * HEAD_DIM:kvw + (kh + 1) * HEAD_DIM]
            bias = bias_ref[at_seq_start if n == 0 else 0]
            for g in range(GROUP):
                cols = slice(g * WINDOW, (g + 1) * WINDOW)
                sink = sink_ref[kh * GROUP + g]
                sv = s_ref[buf, :, cols] + bias
                m = jnp.maximum(jnp.max(sv, axis=0, keepdims=True), sink)
                p = jnp.exp(sv - m)
                inv = 1.0 / (jnp.sum(p, axis=0, keepdims=True) + jnp.exp(sink - m))
                p_ref[buf, :, cols] = (p * inv).astype(BF16)
            o_s = _tn(vb, p_ref[buf]).T
            for g in range(GROUP):
                hd = kh * GROUP + g
                oscr_ref[n * WINDOW:(n + 1) * WINDOW, hd * HEAD_DIM:(hd + 1) * HEAD_DIM] = (
                    o_s[g * WINDOW:(g + 1) * WINDOW].astype(BF16))
        o = oscr_ref[...]
        o_ref[...] = o
        x1_ref[...] = x_ref[...] + _nn(o, _mat(wo_ref)) + bo_ref[...]

    kv_blocks = tt // WINDOW
    return pl.pallas_call(
        body, name=name, grid=(t_all // tt,),
        in_specs=[pl.BlockSpec(memory_space=pltpu.SMEM),
                  pl.BlockSpec((tt, width), lambda i: (i, 0)),
                  pl.BlockSpec((WINDOW, 2 * kvw), lambda i: (jnp.maximum(i * kv_blocks - 1, 0), d // (2 * kvw))),
                  pl.BlockSpec((tt, d), lambda i: (i, 0)), _rows_spec(w_o), _resident((1, d))],
        out_specs=[pl.BlockSpec((tt, d), lambda i: (i, 0)), pl.BlockSpec((tt, d), lambda i: (i, 0))],
        out_shape=[SDS((t_all, d), F32), SDS((t_all, d), BF16)],
        scratch_shapes=[pltpu.VMEM((tt + WINDOW, 2 * kvw), BF16), pltpu.VMEM((tt, d), BF16),
                        pltpu.VMEM((2, 2 * WINDOW, WINDOW), F32), pltpu.VMEM((2, 2 * WINDOW, GROUP * WINDOW), F32),
                        pltpu.VMEM((2, 2 * WINDOW, GROUP * WINDOW), BF16)],
        compiler_params=_cparams(),
    )(sinks, qkv, qkv, x, w_o.arr, b_o)


def _final_norm_loss(x, gain, target, name):
    t_all, d = x.shape
    tt = min(TOKEN_TILE, t_all)

    def body(x_ref, g_ref, t_ref, dx_ref, dg_ref, loss_ref):
        i = pl.program_id(0)
        xv = x_ref[...]
        r, xh = _rms_parts(xv)
        gain_v = g_ref[...]
        e = xh * gain_v - t_ref[...]
        dy = e * (1.0 / d)
        dx_ref[...] = _rms_backward(dy, xh, r, gain_v, 0.0)

        @pl.when(i == 0)
        def _():
            dg_ref[...] = jnp.zeros_like(dg_ref)
            loss_ref[...] = jnp.zeros_like(loss_ref)

        dg_ref[...] += jnp.sum(dy * xh, axis=0, keepdims=True)
        loss_ref[...] += (0.5 / d) * jnp.sum(e * e, axis=0, keepdims=True)

    return pl.pallas_call(
        body, name=name, grid=(t_all // tt,),
        in_specs=[pl.BlockSpec((tt, d), lambda i: (i, 0)), _resident((1, d)), pl.BlockSpec((tt, d), lambda i: (i, 0))],
        out_specs=[pl.BlockSpec((tt, d), lambda i: (i, 0)), pl.BlockSpec((1, d), lambda i: (0, 0)),
                   pl.BlockSpec((1, d), lambda i: (0, 0))],
        out_shape=[SDS((t_all, d), F32), SDS((1, d), F32), SDS((1, d), F32)],
        compiler_params=_cparams(),
    )(x, gain, target)


def _bwd_ffn_inner(dx2, gate, s_act, uds, w_conv, w_down, seq, name):
    t_all, d = dx2.shape
    f = w_down.n * N_DEV
    tt = _token_tile(seq) // 2
    tps = seq // tt
    nt = t_all // tt

    def body(dx_ref, g_ref, s_ref, uds_ref, wc_ref, wd_ref, dgu_ref, dwc_ref, aext_ref, da_ref):
        i = pl.program_id(0)
        ti = nt - 1 - i
        da_ref[...] = _nt(dx_ref[...].astype(BF16), _mat(wd_ref))

        @pl.when(ti % tps == tps - 1)
        def _():
            aext_ref[...] = jnp.zeros_like(aext_ref)

        @pl.when(i == 0)
        def _():
            dwc_ref[...] = jnp.zeros_like(dwc_ref)

        for c in range(f // LANES):
            cols = slice(c * LANES, (c + 1) * LANES)
            da = da_ref[:, cols]
            g = g_ref[:, cols].astype(F32)
            dgc = da * uds_ref[:, cols].astype(F32)
            after = aext_ref[:, cols]
            sh1 = _shifted_rows(dgc, after, 1, False)
            sh2 = _shifted_rows(dgc, after, 2, False)
            aext_ref[:, cols] = dgc[0:SUBLANES, :]
            dg = wc_ref[2:3, cols] * dgc + wc_ref[1:2, cols] * sh1 + wc_ref[0:1, cols] * sh2
            dgu_ref[:, cols] = dg.astype(BF16)
            dgu_ref[:, f + c * LANES:f + (c + 1) * LANES] = (da * s_ref[:, cols].astype(F32)).astype(BF16)
            dwc_ref[0:1, cols] += jnp.sum(g * sh2, axis=0, keepdims=True)
            dwc_ref[1:2, cols] += jnp.sum(g * sh1, axis=0, keepdims=True)
            dwc_ref[2:3, cols] += jnp.sum(g * dgc, axis=0, keepdims=True)

    rev = lambda i: (nt - 1 - i, 0)
    return pl.pallas_call(
        body, name=name, grid=(nt,),
        in_specs=[pl.BlockSpec((tt, d), rev)] + [pl.BlockSpec((tt, f), rev)] * 3 + [_resident((3, f)), _rows_spec(w_down)],
        out_specs=[pl.BlockSpec((tt, 2 * f), rev), pl.BlockSpec((8, f), lambda i: (0, 0))],
        out_shape=[SDS((t_all, 2 * f), BF16), SDS((8, f), F32)],
        scratch_shapes=[pltpu.VMEM((SUBLANES, f), F32), pltpu.VMEM((tt, f), F32)],
        compiler_params=_cparams(),
    )(dx2, gate, s_act, uds, w_conv, w_down.arr)


def _bwd_conv_inner(dx1, bcv, cc, w_conv, w_out, seq, name):
    t_all, d = dx1.shape
    tt = _token_tile(seq)
    tps = seq // tt
    nt = t_all // tt

    def body(dx_ref, bcv_ref, cc_ref, wc_ref, wout_ref, dbcv_ref, dwc_ref, aext_ref, dy_ref):
        i = pl.program_id(0)
        ti = nt - 1 - i
        dy_ref[...] = _nt(dx_ref[...].astype(BF16), _mat(wout_ref))

        @pl.when(ti % tps == tps - 1)
        def _():
            aext_ref[...] = jnp.zeros_like(aext_ref)

        @pl.when(i == 0)
        def _():
            dwc_ref[...] = jnp.zeros_like(dwc_ref)

        for s in range(d // LANES):
            cols = slice(s * LANES, (s + 1) * LANES)
            ccols = slice(d + s * LANES, d + (s + 1) * LANES)
            vcols = slice(2 * d + s * LANES, 2 * d + (s + 1) * LANES)
            dy = dy_ref[:, cols]
            c = bcv_ref[:, ccols].astype(F32)
            v = bcv_ref[:, vcols].astype(F32)
            cv = c * v
            dcc = dy * bcv_ref[:, cols].astype(F32)
            after = aext_ref[:, cols]
            sh1 = _shifted_rows(dcc, after, 1, False)
            sh2 = _shifted_rows(dcc, after, 2, False)
            aext_ref[:, cols] = dcc[0:SUBLANES, :]
            dcv = wc_ref[2:3, cols] * dcc + wc_ref[1:2, cols] * sh1 + wc_ref[0:1, cols] * sh2
            dbcv_ref[:, cols] = (dy * cc_ref[:, cols].astype(F32)).astype(BF16)
            dbcv_ref[:, ccols] = (dcv * v).astype(BF16)
            dbcv_ref[:, vcols] = (dcv * c).astype(BF16)
            dwc_ref[0:1, cols] += jnp.sum(cv * sh2, axis=0, keepdims=True)
            dwc_ref[1:2, cols] += jnp.sum(cv * sh1, axis=0, keepdims=True)
            dwc_ref[2:3, cols] += jnp.sum(cv * dcc, axis=0, keepdims=True)

    return pl.pallas_call(
        body, name=name, grid=(nt,),
        in_specs=[pl.BlockSpec((tt, d), lambda i: (nt - 1 - i, 0)),
                  pl.BlockSpec((tt, 3 * d), lambda i: (nt - 1 - i, 0)),
                  pl.BlockSpec((tt, d), lambda i: (nt - 1 - i, 0)),
                  _resident((3, d)), _rows_spec(w_out)],
        out_specs=[pl.BlockSpec((tt, 3 * d), lambda i: (nt - 1 - i, 0)), pl.BlockSpec((8, d), lambda i: (0, 0))],
        out_shape=[SDS((t_all, 3 * d), BF16), SDS((8, d), F32)],
        scratch_shapes=[pltpu.VMEM((SUBLANES, d), F32), pltpu.VMEM((tt, d), F32)],
        compiler_params=_cparams(),
    )(dx1, bcv, cc, w_conv, w_out.arr)


def _bwd_attention_inner(dx1, qkv, sinks, w_o, cos_t, sin_t, seq, name):
    t_all, d = dx1.shape
    width = qkv.shape[1]
    kvw = (width - d) // 2
    n_kv = kvw // HEAD_DIM
    tt = _token_tile(seq)
    tps = seq // tt
    nt = t_all // tt
    nblk = tt // WINDOW
    scale = HEAD_DIM ** -0.5

    def body(sink_ref, dx_ref, qkv_ref, kvp_ref, cos_ref, sin_ref, wo_ref,
             dqkv_ref, dsink_ref, dbqkv_ref, dbo_ref,
             kvext_ref, dkvext_ref, carry_ref, dq_ref, do_ref, bias_ref, s_ref, dp_ref, p_ref, ds_ref):
        i = pl.program_id(0)
        ti = nt - 1 - i
        dxv = dx_ref[...]
        do_ref[...] = _nt(dxv.astype(BF16), _mat(wo_ref)).astype(BF16)
        kvext_ref[0:WINDOW, :] = kvp_ref[...]
        kvext_ref[WINDOW:, :] = qkv_ref[:, d:]
        dkvext_ref[...] = jnp.zeros_like(dkvext_ref)

        @pl.when(i == 0)
        def _():
            base, first = _band_bias()
            bias_ref[0], bias_ref[1] = base.T, first.T
            carry_ref[...] = jnp.zeros_like(carry_ref)
            dsink_ref[...] = jnp.zeros_like(dsink_ref)
            dbqkv_ref[...] = jnp.zeros_like(dbqkv_ref)
            dbo_ref[...] = jnp.zeros_like(dbo_ref)

        at_seq_start = (ti % tps == 0).astype(jnp.int32)
        head_lane = lax.broadcasted_iota(jnp.int32, (1, LANES), 1)
        dsink = jnp.zeros((1, LANES), F32)
        for n in range(nblk):
            for kh in range(n_kv):
                buf = (n * n_kv + kh) % 2
                qs = _stack_heads(qkv_ref, n * WINDOW, kh)
                dos = _stack_heads(do_ref, n * WINDOW, kh)
                kcols = slice(kh * HEAD_DIM, (kh + 1) * HEAD_DIM)
                vcols = slice(kvw + kh * HEAD_DIM, kvw + (kh + 1) * HEAD_DIM)
                band = slice(n * WINDOW, (n + 2) * WINDOW)
                kb = kvext_ref[band, kcols]
                vb = kvext_ref[band, vcols]
                s_ref[buf] = _nt(kb, qs)
                dp_ref[buf] = _nt(vb, dos)
                bias = bias_ref[at_seq_start if n == 0 else 0]
                for g in range(GROUP):
                    hd = kh * GROUP + g
                    cols = slice(g * WINDOW, (g + 1) * WINDOW)
                    sink = sink_ref[hd]
                    sv = s_ref[buf, :, cols] + bias
                    m = jnp.maximum(jnp.max(sv, axis=0, keepdims=True), sink)
                    p = jnp.exp(sv - m)
                    e_sink = jnp.exp(sink - m)
                    inv = 1.0 / (jnp.sum(p, axis=0, keepdims=True) + e_sink)
                    probs = p * inv
                    dp = dp_ref[buf, :, cols]
                    dsum = jnp.sum(probs * dp, axis=0, keepdims=True)
                    p_ref[buf, :, cols] = probs.astype(BF16)
                    ds_ref[buf, :, cols] = (probs * (dp - dsum)).astype(BF16)
                    dsink = dsink - jnp.where(head_lane == hd, jnp.sum(e_sink * inv * dsum), 0.0)
                ds_t = ds_ref[buf]
                dkvext_ref[band, vcols] += _nn(p_ref[buf], dos)
                dkvext_ref[band, kcols] += _nn(ds_t, qs)
                dq_s = _tn(kb, ds_t).T
                for g in range(GROUP):
                    hd = kh * GROUP + g
                    dq_ref[n * WINDOW:(n + 1) * WINDOW, hd * HEAD_DIM:(hd + 1) * HEAD_DIM] = dq_s[g * WINDOW:(g + 1) * WINDOW]
        dsink_ref[0:1, :] += dsink
        dkvext_ref[tt:tt + WINDOW, :] += carry_ref[...]
        carry_ref[...] = dkvext_ref[0:WINDOW, :]

        cosv = cos_ref[...]
        sinv = sin_ref[...]
        lane_lo = (lax.broadcasted_iota(jnp.int32, (tt, LANES), 1) % HEAD_DIM) < HEAD_DIM // 2
        for s in range((d + kvw) // LANES):
            if s * LANES < d:
                dy = dq_ref[:, s * LANES:(s + 1) * LANES] * scale
            else:
                dy = dkvext_ref[WINDOW:, s * LANES - d:(s + 1) * LANES - d]
            dpre = dy * cosv - _rope_partner(dy, lane_lo) * sinv
            dqkv_ref[:, s * LANES:(s + 1) * LANES] = dpre.astype(BF16)
            dbqkv_ref[0:1, s * LANES:(s + 1) * LANES] += jnp.sum(dpre, axis=0, keepdims=True)
        dv = dkvext_ref[WINDOW:, kvw:]
        dqkv_ref[:, d + kvw:] = dv.astype(BF16)
        dbqkv_ref[0:1, d + kvw:] += jnp.sum(dv, axis=0, keepdims=True)
        dbo_ref[...] += jnp.sum(dxv, axis=0, keepdims=True)

    kv_blocks = tt // WINDOW
    return pl.pallas_call(
        body, name=name, grid=(nt,),
        in_specs=[pl.BlockSpec(memory_space=pltpu.SMEM),
                  pl.BlockSpec((tt, d), lambda i: (nt - 1 - i, 0)),
                  pl.BlockSpec((tt, width), lambda i: (nt - 1 - i, 0)),
                  pl.BlockSpec((WINDOW, 2 * kvw), lambda i: (jnp.maximum((nt - 1 - i) * kv_blocks - 1, 0), d // (2 * kvw))),
                  pl.BlockSpec((tt, LANES), lambda i: ((nt - 1 - i) % tps, 0)),
                  pl.BlockSpec((tt, LANES), lambda i: ((nt - 1 - i) % tps, 0)),
                  _rows_spec(w_o)],
        out_specs=[pl.BlockSpec((tt, width), lambda i: (nt - 1 - i, 0)), pl.BlockSpec((8, LANES), lambda i: (0, 0)),
                   pl.BlockSpec((1, width), lambda i: (0, 0)), pl.BlockSpec((1, d), lambda i: (0, 0))],
        out_shape=[SDS((t_all, width), BF16), SDS((8, LANES), F32), SDS((1, width), F32), SDS((1, d), F32)],
        scratch_shapes=[pltpu.VMEM((tt + WINDOW, 2 * kvw), BF16), pltpu.VMEM((tt + WINDOW, 2 * kvw), F32),
                        pltpu.VMEM((WINDOW, 2 * kvw), F32), pltpu.VMEM((tt, d), F32), pltpu.VMEM((tt, d), BF16),
                        pltpu.VMEM((2, 2 * WINDOW, WINDOW), F32), pltpu.VMEM((2, 2 * WINDOW, GROUP * WINDOW), F32),
                        pltpu.VMEM((2, 2 * WINDOW, GROUP * WINDOW), F32), pltpu.VMEM((2, 2 * WINDOW, GROUP * WINDOW), BF16),
                        pltpu.VMEM((2, 2 * WINDOW, GROUP * WINDOW), BF16)],
        compiler_params=_cparams(),
    )(sinks, dx1, qkv, qkv, cos_t, sin_t, w_o.arr)


def _bwd_dense_norm(dy, w_t, x, gain, dres, name):
    t_all, d = x.shape
    n = dy.shape[1]
    tt = min(TOKEN_TILE, t_all)

    def body(dy_ref, w_ref, x_ref, g_ref, dres_ref, dx_ref, h_ref, dg_ref):
        i = pl.program_id(0)
        dh = _nn(dy_ref[...], _mat(w_ref))
        r, xh = _rms_parts(x_ref[...])
        gain_v = g_ref[...]
        h_ref[...] = (xh * gain_v).astype(BF16)
        dx_ref[...] = _rms_backward(dh, xh, r, gain_v, dres_ref[...])

        @pl.when(i == 0)
        def _():
            dg_ref[...] = jnp.zeros_like(dg_ref)

        dg_ref[...] += jnp.sum(dh * xh, axis=0, keepdims=True)

    return pl.pallas_call(
        body, name=name, grid=(t_all // tt,),
        in_specs=[pl.BlockSpec((tt, n), lambda i: (i, 0)), _rows_spec(w_t), pl.BlockSpec((tt, d), lambda i: (i, 0)),
                  _resident((1, d)), pl.BlockSpec((tt, d), lambda i: (i, 0))],
        out_specs=[pl.BlockSpec((tt, d), lambda i: (i, 0)), pl.BlockSpec((tt, d), lambda i: (i, 0)),
                   pl.BlockSpec((1, d), lambda i: (0, 0))],
        out_shape=[SDS((t_all, d), F32), SDS((t_all, d), BF16), SDS((1, d), F32)],
        compiler_params=_cparams(),
    )(dy, w_t.arr, x, gain, dres)


def _tn_matmul(a, b, dest, name):
    t_all, m = a.shape
    d = b.shape[1]
    n = dest.n
    assert m == N_DEV * n and dest.off % n == 0
    k = max(kk for kk in (1, 2, 4, 8) if kk * n <= max(n, 1536))
    tm = k * n
    tt = min(TN_TOKEN_TILE, t_all)
    n_t = t_all // tt
    fresh = not hasattr(dest.arr, "dtype")

    def body(a_ref, b_ref, *rest):
        o_ref, acc_ref = rest[-2:]
        t = pl.program_id(1)

        @pl.when(t == 0)
        def _():
            acc_ref[...] = jnp.zeros_like(acc_ref)

        acc_ref[...] += _tn(a_ref[...], b_ref[...].astype(BF16))

        @pl.when(t == n_t - 1)
        def _():
            o_ref[...] = acc_ref[...].astype(BF16).reshape(k, n, d)

    block = dest.off // n
    return pl.pallas_call(
        body, name=name, grid=(m // tm, n_t),
        in_specs=[pl.BlockSpec((tt, tm), lambda j, t: (t, j)), pl.BlockSpec((tt, d), lambda j, t: (t, 0))] + ([] if fresh else [ANY]),
        out_specs=pl.BlockSpec((k, n, d), lambda j, t: (j, block, 0)),
        out_shape=SDS(tuple(dest.arr) if fresh else dest.arr.shape, BF16),
        scratch_shapes=[pltpu.VMEM((tm, d), F32)],
        input_output_aliases={} if fresh else {2: 0},
        compiler_params=_cparams(2),
    )(*((a, b) if fresh else (a, b, dest.arr)))


def _my_place():
    return lax.axis_index("x"), lax.axis_index("y"), lax.axis_index("c")


def _other_chips(x, y):
    return [(1 - x, y), (x, 1 - y), (1 - x, 1 - y)]


def _all_gather(blocks, name):
    n_arr = len(blocks)

    def body(*refs):
        in_refs = refs[:n_arr]
        out_refs = refs[n_arr:2 * n_arr]
        send_sems, recv_sems, local_sems = refs[2 * n_arr:]
        x, y, c = _my_place()
        me, sibling = (x, y, c), (x, y, 1 - c)
        chips = _other_chips(x, y)

        def slot(a, place):
            px, py, pc = place
            return out_refs[a].at[4 * px + 2 * py + pc]

        def copy(a, k, block, to, src=None):
            return pltpu.make_async_remote_copy(
                src_ref=slot(a, block) if src is None else src, dst_ref=slot(a, block),
                send_sem=send_sems.at[a, k], recv_sem=recv_sems.at[a, k], device_id=to, device_id_type=MESH)

        started = []
        local = []
        for a in range(n_arr):
            mine = pltpu.make_async_copy(in_refs[a], slot(a, me), local_sems.at[a])
            mine.start()
            local.append(mine)
            first = [copy(a, 0, me, sibling, src=in_refs[a])]
            first += [copy(a, 1 + j, me, (*chip, c), src=in_refs[a]) for j, chip in enumerate(chips)]
            for cp in first:
                cp.start()
            started += first
        for a in range(n_arr):
            for j, chip in enumerate(chips):
                copy(a, 1 + j, (*chip, c), me).wait_recv()
                passed = copy(a, 4 + j, (*chip, c), sibling)
                passed.start()
                started.append(passed)
        for a in range(n_arr):
            copy(a, 0, sibling, me).wait_recv()
            for j, chip in enumerate(chips):
                copy(a, 4 + j, (*chip, 1 - c), me).wait_recv()
        for cp in started:
            cp.wait_send()
        for mine in local:
            mine.wait()

    return pl.pallas_call(
        body, name=name,
        in_specs=[ANY] * n_arr, out_specs=[ANY] * n_arr,
        out_shape=[SDS((N_DEV,) + b.shape, b.dtype) for b in blocks],
        scratch_shapes=[pltpu.SemaphoreType.DMA((n_arr, 7)), pltpu.SemaphoreType.DMA((n_arr, 7)),
                        pltpu.SemaphoreType.DMA((n_arr,))],
    )(*blocks)


def _peer_of(k, x, y, c):
    return x ^ ((k >> 2) & 1), y ^ ((k >> 1) & 1), c ^ (k & 1)


HBM = pl.BlockSpec(memory_space=pltpu.HBM)
SEM = pl.BlockSpec(memory_space=pltpu.SEMAPHORE)
DATAFLOW_EFFECT = pltpu.SideEffectType.DATAFLOW_SIDE_EFFECTING


def _peer_copies(src_ref, land_ref, send_sems, recv_sems, per_peer):
    x, y, c = _my_place()
    me = 4 * x + 2 * y + c
    copies = []
    for k in range(1, N_DEV):
        px, py, pc = _peer_of(k, x, y, c)
        peer = 4 * px + 2 * py + pc
        copies.append(pltpu.make_async_remote_copy(
            src_ref=src_ref.at[peer] if per_peer else src_ref, dst_ref=land_ref.at[me],
            send_sem=send_sems.at[k - 1], recv_sem=recv_sems.at[k - 1], device_id=(px, py, pc), device_id_type=MESH))
    own = pltpu.make_async_copy(src_ref.at[me] if per_peer else src_ref, land_ref.at[me], send_sems.at[N_DEV - 1])
    return copies, own


def _exchange_start(src, after, per_peer, name):
    rows, d = src.shape[-2:]

    def body(src_ref, land_ref, after_ref, send_sems, recv_sems, src_thru, land_thru, token):
        copies, own = _peer_copies(src_ref, land_ref, send_sems, recv_sems, per_peer)
        for cp in copies:
            cp.start()
        own.start()
        token[...] = jnp.zeros_like(token)

    return pl.pallas_call(
        body, name=name,
        out_shape=(pltpu.SemaphoreType.DMA((N_DEV,)), pltpu.SemaphoreType.DMA((N_DEV - 1,)), pltpu.HBM(src.shape, src.dtype),
                   pltpu.HBM((N_DEV, rows, d), src.dtype), SDS((SUBLANES, LANES), F32)),
        in_specs=(HBM, HBM, ANY), out_specs=(SEM, SEM, HBM, HBM, pl.BlockSpec(memory_space=pltpu.VMEM)),
        input_output_aliases={0: 2, 1: 3},
        compiler_params=pltpu.CompilerParams(has_side_effects=DATAFLOW_EFFECT),
    )(pltpu.with_memory_space_constraint(src, pltpu.HBM),
      pltpu.with_memory_space_constraint(lax.empty((N_DEV, rows, d), src.dtype), pltpu.HBM), after)


def _exchange_wait(started, after, per_peer, name):
    send_sems, recv_sems, src_thru, land_thru, _ = started

    def body(src_ref, land_ref, send_sems, recv_sems, after_ref, src_out, land_out):
        copies, own = _peer_copies(src_ref, land_ref, send_sems, recv_sems, per_peer)
        for cp in copies:
            cp.wait_send()
            cp.wait_recv()
        own.wait()

    return pl.pallas_call(
        body, name=name,
        out_shape=(pltpu.HBM(src_thru.shape, src_thru.dtype), pltpu.HBM(land_thru.shape, land_thru.dtype)),
        in_specs=(HBM, HBM, SEM, SEM, ANY), out_specs=(HBM, HBM), input_output_aliases={0: 0, 1: 1},
        compiler_params=pltpu.CompilerParams(has_side_effects=DATAFLOW_EFFECT),
    )(src_thru, land_thru, send_sems, recv_sems, after)


def _sum_slots(slots, name):
    _, rows, d = slots.shape
    tr = _largest_divisor(rows, 512, 16)

    def body(s_ref, o_ref):
        acc = s_ref[0].astype(F32)
        for dev in range(1, N_DEV):
            acc = acc + s_ref[dev].astype(F32)
        o_ref[...] = acc

    return pl.pallas_call(
        body, name=name, grid=(rows // tr,),
        in_specs=[pl.BlockSpec((N_DEV, tr, d), lambda r: (0, r, 0))], out_specs=pl.BlockSpec((tr, d), lambda r: (r, 0)),
        out_shape=SDS((rows, d), F32), compiler_params=_cparams(),
    )(slots)


def _all_reduce_small(part, loss_rows, name):
    rows, lanes = part.shape
    lo, hi = loss_rows

    def body(x_ref, out_ref, loss_ref, gath_ref, send_sems, recv_sems):
        x, y, c = _my_place()
        me = 4 * x + 2 * y + c
        gath_ref[me] = x_ref[...]
        copies = []
        for k in range(1, N_DEV):
            peer = (x ^ ((k >> 2) & 1), y ^ ((k >> 1) & 1), c ^ (k & 1))
            cp = pltpu.make_async_remote_copy(
                src_ref=x_ref, dst_ref=gath_ref.at[me], send_sem=send_sems.at[k - 1], recv_sem=recv_sems.at[k - 1],
                device_id=peer, device_id_type=MESH)
            cp.start()
            copies.append(cp)
        for cp in copies:
            cp.wait_recv()
        for cp in copies:
            cp.wait_send()
        acc = gath_ref[0]
        for dev in range(1, N_DEV):
            acc = acc + gath_ref[dev]
        out_ref[...] = acc
        loss_ref[...] = jnp.full(loss_ref.shape, jnp.sum(acc[lo:hi, :]), F32)

    vmem = pl.BlockSpec(memory_space=pltpu.VMEM)
    return pl.pallas_call(
        body, name=name, in_specs=[vmem], out_specs=[vmem, vmem],
        out_shape=[SDS((rows, lanes), F32), SDS((SUBLANES, LANES), F32)],
        scratch_shapes=[pltpu.VMEM((N_DEV, rows, lanes), F32), pltpu.SemaphoreType.DMA((N_DEV - 1,)),
                        pltpu.SemaphoreType.DMA((N_DEV - 1,))],
    )(part)


def _adamw(w, g, m, v, name):
    rows, cols = w.shape
    tr = rows if rows % SUBLANES else _largest_divisor(rows, 512, SUBLANES)

    def body(w_ref, g_ref, m_ref, v_ref, d_ref, nm_ref, nv_ref):
        gv = g_ref[...]
        nm = ADAM_B1 * m_ref[...] + (1.0 - ADAM_B1) * gv
        nv = ADAM_B2 * v_ref[...] + (1.0 - ADAM_B2) * (gv * gv)
        m_hat = nm / (1.0 - ADAM_B1 ** ADAM_STEP)
        v_hat = nv / (1.0 - ADAM_B2 ** ADAM_STEP)
        d_ref[...] = -ADAM_LR * (m_hat / (jnp.sqrt(v_hat) + ADAM_EPS) + ADAM_WD * w_ref[...])
        nm_ref[...] = nm
        nv_ref[...] = nv

    spec = pl.BlockSpec((tr, cols), lambda i: (i, 0))
    return pl.pallas_call(
        body, name=name, grid=(rows // tr,), in_specs=[spec] * 4, out_specs=[spec] * 3,
        out_shape=[SDS((rows, cols), F32)] * 3, compiler_params=_cparams(),
    )(w, g, m, v)


def _adamw_nd(w, g, m, v, name):
    shape = w.shape
    two_d = (1, shape[0]) if len(shape) == 1 else (-1, shape[-1])
    outs = _adamw(w.reshape(two_d), g.reshape(two_d), m.reshape(two_d), v.reshape(two_d), name)
    return [o.reshape(shape) for o in outs]


def _rope_tables(seq):
    pos = jnp.arange(seq, dtype=F32)
    inv_freq = 1.0 / (ROPE_THETA ** (jnp.arange(0, HEAD_DIM, 2, dtype=F32) / HEAD_DIM))
    ang = pos[:, None] * inv_freq[None, :]
    cos, sin = jnp.cos(ang), jnp.sin(ang)
    reps = LANES // HEAD_DIM
    cos_t = jnp.tile(jnp.concatenate([cos, cos], axis=1), (1, reps))
    sin_t = jnp.tile(jnp.concatenate([-sin, sin], axis=1), (1, reps))
    return cos_t, sin_t


def _flat_pad(a):
    flat = a.reshape(1, -1)
    pad = (-flat.shape[1]) % LANES
    return jnp.pad(flat, ((0, 0), (0, pad))) if pad else flat


def kernel(x, norm_mix, norm_ffn, norm_final, conv_w_in, conv_w_conv, conv_w_out, attn_w_qkv, attn_b_qkv, attn_sinks, attn_w_o, attn_b_o, ffn_w_in, ffn_w_conv, ffn_w_down, loss_target, m_norm_mix, m_norm_ffn, m_norm_final, m_conv_w_in, m_conv_w_conv, m_conv_w_out, m_attn_w_qkv, m_attn_b_qkv, m_attn_sinks, m_attn_w_o, m_attn_b_o, m_ffn_w_in, m_ffn_w_conv, m_ffn_w_down, v_norm_mix, v_norm_ffn, v_norm_final, v_conv_w_in, v_conv_w_conv, v_conv_w_out, v_attn_w_qkv, v_attn_b_qkv, v_attn_sinks, v_attn_w_o, v_attn_b_o, v_ffn_w_in, v_ffn_w_conv, v_ffn_w_down):
    b_loc, seq, d = x.shape
    depth = norm_mix.shape[0]
    n_conv, n_attn = conv_w_in.shape[0], attn_w_qkv.shape[0]
    t_all = b_loc * seq
    my_x, my_y, my_c = _my_place()

    me = 4 * my_x + 2 * my_y + my_c

    groups = []
    for i in range(depth):
        j = i // 2
        if i % 2 == 0:
            mix = [("conv_w_in", j, True, conv_w_in[j].T), ("conv_w_out", j, False, conv_w_out[j])]
        else:
            mix = [("attn_w_qkv", j, True, attn_w_qkv[j].T), ("attn_w_o", j, False, attn_w_o[j])]
        groups.append((("mix", i), mix))
        groups.append((("ffn", i), [("ffn_w_in", i, True, ffn_w_in[i].T), ("ffn_w_down", i, False, ffn_w_down[i])]))
    order = [key for key, _ in groups]
    members_of = dict(groups)

    def layout(key):
        offs, o = [], 0
        for _, _, _, shard in members_of[key]:
            n = shard.shape[0]
            o = -(-o // n) * n
            offs.append(o)
            o += n
        return offs, o

    small = jnp.concatenate([_flat_pad(conv_w_conv), _flat_pad(ffn_w_conv), _flat_pad(attn_b_qkv), _flat_pad(attn_b_o)], axis=1)
    (small_g,) = _all_gather([small], "gather_small")

    gather_started = {}

    def start_gather(idx, after):
        if idx >= len(order):
            return 0.0
        key = order[idx]
        offs, total = layout(key)
        pieces, o = [], 0
        for (_, _, _, shard), off in zip(members_of[key], offs):
            if off > o:
                pieces.append(jnp.zeros((off - o, d), shard.dtype))
            pieces.append(shard)
            o = off + shard.shape[0]
        pack = jnp.concatenate(pieces, axis=0).astype(BF16)
        gather_started[key] = _exchange_start(pack, after, False, f"gather_start_{key[0]}_{key[1]}")
        return gather_started[key][4][0, 0]

    weights = {}

    def finish_gather(key, after):
        _, land = _exchange_wait(gather_started[key], after, False, f"gather_wait_{key[0]}_{key[1]}")
        for (wname, layer, _, shard), off in zip(members_of[key], layout(key)[0]):
            weights[(wname, layer)] = _Rows(land, off, shard.shape[0])

    def take_small(o, shape):
        size = shape[0] * shape[1] * shape[2]
        blk = small_g[:, 0, o:o + size].reshape((N_DEV,) + shape)
        return jnp.moveaxis(blk, 0, 2).reshape(shape[0], shape[1], N_DEV * shape[2])

    so = 0
    wc_conv_full = take_small(so, conv_w_conv.shape); so += _flat_pad(conv_w_conv).shape[1]
    wc_ffn_full = take_small(so, ffn_w_conv.shape); so += _flat_pad(ffn_w_conv).shape[1]
    b_qkv_full = take_small(so, (n_attn, 1, attn_b_qkv.shape[1]))[:, 0]; so += _flat_pad(attn_b_qkv).shape[1]
    b_o_full = take_small(so, (n_attn, 1, attn_b_o.shape[1]))[:, 0]

    cos_t, sin_t = _rope_tables(seq)

    xs = [x.reshape(t_all, d)]
    saved = []
    token = start_gather(0, small_g) + start_gather(1, small_g)
    for i in range(depth):
        j = i // 2
        if i > 0:
            token = start_gather(2 * i + 2, xs[-1])
        gain_mix = norm_mix[i][None, :] + token
        finish_gather(("mix", i), gain_mix if i == 0 else xs[-1])
        if i % 2 == 0:
            x1, *mix_saved = _fwd_conv_mixer(xs[-1], gain_mix, weights[("conv_w_in", j)], wc_conv_full[j],
                                             weights[("conv_w_out", j)], seq, f"fwd_conv_{i}")
        else:
            qkv = _fwd_qkv(xs[-1], gain_mix, weights[("attn_w_qkv", j)], b_qkv_full[j][None, :], cos_t, sin_t, seq,
                           f"fwd_qkv_{i}")
            x1, o = _fwd_attention(qkv, xs[-1], attn_sinks[j], weights[("attn_w_o", j)], b_o_full[j][None, :], seq,
                                   f"fwd_attn_{i}")
            mix_saved = (qkv, o)
        token = start_gather(2 * i + 3, x1) + (start_gather(2, x1) if i == 0 else 0.0)
        gain_ffn = norm_ffn[i][None, :] + token
        finish_gather(("ffn", i), gain_ffn)
        x2, *ffn_saved = _fwd_ffn(x1, gain_ffn, weights[("ffn_w_in", i)], wc_ffn_full[i], weights[("ffn_w_down", i)],
                                  seq, f"fwd_ffn_{i}")
        saved.append((xs[-1], x1, mix_saved, ffn_saved))
        xs.append(x2)
        token = 0.0

    dx, dg_final, loss_lanes = _final_norm_loss(xs[-1], norm_final[None, :], loss_target.reshape(t_all, d), "loss_head")

    dg_mix, dg_ffn = [None] * depth, [None] * depth
    dwc_conv, dwc_ffn = [None] * n_conv, [None] * depth
    db_qkv, db_o, dsinks = [None] * n_attn, [None] * n_attn, [None] * n_attn
    scatter_started = {}

    def weight_grads(key, operands):
        offs, total = layout(key)
        parts = (N_DEV, total, d)
        for (wname, layer, _, shard), off, (a, b) in zip(members_of[key], offs, operands):
            parts = _tn_matmul(a, b, _Rows(parts, off, shard.shape[0]), f"dw_{wname}_{layer}")
        scatter_started[key] = _exchange_start(parts, operands[0][1], True, f"scatter_start_{key[0]}_{key[1]}")
        return scatter_started[key][4][0, 0]

    token = 0.0
    for i in reversed(range(depth)):
        j = i // 2
        x0, x1, mix_saved, (gate, s_act, uds, act) = saved[i]
        dgu, dwc = _bwd_ffn_inner(dx, gate, s_act, uds, wc_ffn_full[i] + token, weights[("ffn_w_down", i)], seq, f"bwd_ffn_{i}")
        dwc_ffn[i] = dwc[:3]
        dx1, h2, dg_ffn[i] = _bwd_dense_norm(dgu, weights[("ffn_w_in", i)], x1, norm_ffn[i][None, :], dx, f"bwd_ffn_norm_{i}")
        token = weight_grads(("ffn", i), [(dgu, h2), (act, dx)])
        if i % 2 == 0:
            bcv, cc, y = mix_saved
            dbcv, dwc = _bwd_conv_inner(dx1, bcv, cc, wc_conv_full[j] + token, weights[("conv_w_out", j)], seq, f"bwd_conv_{i}")
            dwc_conv[j] = dwc[:3]
            dx, h, dg_mix[i] = _bwd_dense_norm(dbcv, weights[("conv_w_in", j)], x0, norm_mix[i][None, :], dx1,
                                               f"bwd_conv_norm_{i}")
            token = weight_grads(("mix", i), [(dbcv, h), (y, dx1)])
        else:
            qkv, o = mix_saved
            dqkv, dsk, dbq, dbo = _bwd_attention_inner(dx1, qkv, attn_sinks[j] + token, weights[("attn_w_o", j)], cos_t, sin_t,
                                                       seq, f"bwd_attn_{i}")
            dsinks[j], db_qkv[j], db_o[j] = dsk[0:1, :attn_sinks.shape[1]], dbq, dbo
            dx, h, dg_mix[i] = _bwd_dense_norm(dqkv, weights[("attn_w_qkv", j)], x0, norm_mix[i][None, :], dx1,
                                               f"bwd_attn_norm_{i}")
            token = weight_grads(("mix", i), [(dqkv, h), (o, dx1)])
    grad_x = dx.reshape(b_loc, seq, d)

    reduced = {}

    def finish_scatter(key, after):
        _, land = _exchange_wait(scatter_started[key], after, True, f"scatter_wait_{key[0]}_{key[1]}")
        total = _sum_slots(land, f"scatter_sum_{key[0]}_{key[1]}")
        for (wname, layer, transposed, shard), off in zip(members_of[key], layout(key)[0]):
            rows = total[off:off + shard.shape[0]]
            reduced[(wname, layer)] = rows.T if transposed else rows

    last_key = order[0]
    for key in reversed(order[1:]):
        finish_scatter(key, dx)

    small_parts = [jnp.concatenate(dg_mix, axis=0), jnp.concatenate(dg_ffn, axis=0), dg_final,
                   jnp.stack(dwc_conv), jnp.stack(dwc_ffn), jnp.concatenate(db_qkv, axis=0), jnp.concatenate(db_o, axis=0),
                   jnp.concatenate(dsinks, axis=0), loss_lanes]
    flats = [_flat_pad(p) for p in small_parts]
    bounds = []
    so = 0
    for fl in flats:
        bounds.append((so, so + fl.shape[1]))
        so += fl.shape[1]
    small_rows = so // LANES
    pad_rows = (-small_rows) % SUBLANES
    part_small = jnp.pad(jnp.concatenate(flats, axis=1).reshape(small_rows, LANES), ((0, pad_rows), (0, 0)))
    loss_rows = (bounds[-1][0] // LANES, bounds[-1][1] // LANES)
    summed, loss_tile = _all_reduce_small(part_small, loss_rows, "reduce_small")
    summed = summed.reshape(1, -1)

    def small_grad(k, shape):
        lo = bounds[k][0]
        size = 1
        for s_ in shape:
            size *= s_
        return summed[0, lo:lo + size].reshape(shape)

    def my_cols(full, n_local):
        return lax.dynamic_slice_in_dim(full, me * n_local, n_local, axis=full.ndim - 1)

    g_norm_mix = small_grad(0, norm_mix.shape)
    g_norm_ffn = small_grad(1, norm_ffn.shape)
    g_norm_final = small_grad(2, norm_final.shape)
    g_conv_w_conv = my_cols(small_grad(3, (n_conv, 3, d)), conv_w_conv.shape[2])
    g_ffn_w_conv = my_cols(small_grad(4, (depth, 3, ffn_w_conv.shape[2] * N_DEV)), ffn_w_conv.shape[2])
    g_attn_b_qkv = my_cols(small_grad(5, (n_attn, attn_b_qkv.shape[1] * N_DEV)), attn_b_qkv.shape[1])
    g_attn_b_o = my_cols(small_grad(6, (n_attn, d)), attn_b_o.shape[1])
    g_attn_sinks = small_grad(7, attn_sinks.shape)
    loss = loss_tile[0, 0]

    def big_grad(wname, n_layers):
        return jnp.stack([reduced[(wname, layer)] for layer in range(n_layers)])

    grads = {
        "norm_mix": g_norm_mix, "norm_ffn": g_norm_ffn, "norm_final": g_norm_final, "conv_w_conv": g_conv_w_conv,
        "attn_w_qkv": big_grad("attn_w_qkv", n_attn), "attn_b_qkv": g_attn_b_qkv, "attn_sinks": g_attn_sinks,
        "attn_w_o": big_grad("attn_w_o", n_attn), "attn_b_o": g_attn_b_o,
        "ffn_w_in": big_grad("ffn_w_in", depth), "ffn_w_conv": g_ffn_w_conv, "ffn_w_down": big_grad("ffn_w_down", depth),
    }
    params = {
        "norm_mix": (norm_mix, m_norm_mix, v_norm_mix), "norm_ffn": (norm_ffn, m_norm_ffn, v_norm_ffn),
        "norm_final": (norm_final, m_norm_final, v_norm_final), "conv_w_in": (conv_w_in, m_conv_w_in, v_conv_w_in),
        "conv_w_conv": (conv_w_conv, m_conv_w_conv, v_conv_w_conv), "conv_w_out": (conv_w_out, m_conv_w_out, v_conv_w_out),
        "attn_w_qkv": (attn_w_qkv, m_attn_w_qkv, v_attn_w_qkv), "attn_b_qkv": (attn_b_qkv, m_attn_b_qkv, v_attn_b_qkv),
        "attn_sinks": (attn_sinks, m_attn_sinks, v_attn_sinks), "attn_w_o": (attn_w_o, m_attn_w_o, v_attn_w_o),
        "attn_b_o": (attn_b_o, m_attn_b_o, v_attn_b_o), "ffn_w_in": (ffn_w_in, m_ffn_w_in, v_ffn_w_in),
        "ffn_w_conv": (ffn_w_conv, m_ffn_w_conv, v_ffn_w_conv), "ffn_w_down": (ffn_w_down, m_ffn_w_down, v_ffn_w_down),
    }
    names = list(params)
    updates = {}

    def update(wname):
        w, m, v = params[wname]
        updates[wname] = _adamw_nd(w, grads[wname], m, v, f"adamw_{wname}")

    last_names = sorted({wname for wname, _, _, _ in members_of[last_key]})
    for wname in names:
        if wname not in last_names:
            update(wname)
    finish_scatter(last_key, updates["ffn_w_in"][0])
    for wname in last_names:
        grads[wname] = big_grad(wname, params[wname][0].shape[0])
        update(wname)
    return (loss, grad_x, *[grads[wname] for wname in names], *[updates[wname][0] for wname in names],
            *[updates[wname][1] for wname in names], *[updates[wname][2] for wname in names])
```

```python
from typing import NamedTuple

import jax
import jax.numpy as jnp
from jax import lax
from jax.experimental import pallas as pl
from jax.experimental.pallas import tpu as pltpu

F32 = jnp.float32
BF16 = jnp.bfloat16
SDS = jax.ShapeDtypeStruct
MESH = pl.DeviceIdType.MESH
ANY = pl.BlockSpec(memory_space=pl.ANY)

N_DEV = 8
EPS = 1e-5
HEAD_DIM = 64
GROUP = 4
WINDOW = 128
ROPE_THETA = 10000.0
ADAM_LR, ADAM_B1, ADAM_B2, ADAM_EPS, ADAM_WD, ADAM_STEP = 0.001, 0.9, 0.999, 1e-08, 0.01, 10

V7X_VMEM_BYTES = 64 * 1024 * 1024
VMEM_LIMIT_BYTES = V7X_VMEM_BYTES - 8 * 1024 * 1024
LANES = 128
SUBLANES = 8
TOKEN_TILE = 512
TN_TOKEN_TILE = 2048
MASKED_SCORE = -1e30


def _cparams(n_axes=1):
    return pltpu.CompilerParams(dimension_semantics=("arbitrary",) * n_axes, vmem_limit_bytes=VMEM_LIMIT_BYTES)


def _resident(shape):
    zeros = (0,) * len(shape)
    return pl.BlockSpec(shape, lambda *_: zeros, pipeline_mode=pl.Buffered(1))


class _Rows(NamedTuple):
    arr: jax.Array
    off: int
    n: int


def _rows_spec(w):
    assert w.off % w.n == 0
    block = w.off // w.n
    return pl.BlockSpec((N_DEV, w.n, w.arr.shape[2]), lambda *_: (0, block, 0), pipeline_mode=pl.Buffered(1))


def _mat(ref):
    v = ref[...]
    return v.reshape(v.shape[0] * v.shape[1], v.shape[2])


def _token_tile(seq):
    return min(TOKEN_TILE, seq // 2)


def _largest_divisor(m, cap, mult):
    best = None
    for d in range(mult, min(m, cap) + 1, mult):
        if m % d == 0:
            best = d
    return m if best is None else best


def _nt(a, b):
    return lax.dot_general(a, b, (((1,), (1,)), ((), ())), preferred_element_type=F32)


def _nn(a, b):
    return lax.dot_general(a, b, (((1,), (0,)), ((), ())), preferred_element_type=F32)


def _tn(a, b):
    return lax.dot_general(a, b, (((0,), (0,)), ((), ())), preferred_element_type=F32)


def _rms_parts(xv):
    r = lax.rsqrt(jnp.mean(xv * xv, axis=-1, keepdims=True) + EPS)
    return r, xv * r


def _rms_backward(dh, xh, r, gain, dres):
    u = dh * gain
    return dres + r * (u - xh * jnp.mean(u * xh, axis=-1, keepdims=True))


def _shifted_rows(xv, edge, k, down):
    n = xv.shape[0]
    row = lax.broadcasted_iota(jnp.int32, edge.shape, 0)
    if down:
        rolled = pltpu.roll(xv, k, 0)
        head = jnp.where(row < k, pltpu.roll(edge, k, 0), rolled[0:SUBLANES])
        return jnp.concatenate([head, rolled[SUBLANES:]], axis=0)
    rolled = pltpu.roll(xv, n - k, 0)
    tail = jnp.where(row >= SUBLANES - k, pltpu.roll(edge, SUBLANES - k, 0), rolled[n - SUBLANES:])
    return jnp.concatenate([rolled[:n - SUBLANES], tail], axis=0)


def _causal_conv3(edge_ref, xv, w_ref):
    before = edge_ref[...]
    y = (w_ref[2:3, :] * xv + w_ref[1:2, :] * _shifted_rows(xv, before, 1, True)
         + w_ref[0:1, :] * _shifted_rows(xv, before, 2, True))
    edge_ref[...] = xv[xv.shape[0] - SUBLANES:, :]
    return y


def _sigmoid(z):
    return 1.0 / (1.0 + jnp.exp(-z))


def _fwd_conv_mixer(x, gain, w_in_t, w_conv, w_out, seq, name):
    t_all, d = x.shape
    tt = _token_tile(seq)
    tps = seq // tt

    def body(x_ref, g_ref, win_ref, wc_ref, wout_ref, x1_ref, bcv_ref, cc_ref, y_ref, ext_ref):
        i = pl.program_id(0)
        xv = x_ref[...]
        r, xh = _rms_parts(xv)
        h = (xh * g_ref[...]).astype(BF16)
        bcv = _nt(h, _mat(win_ref))
        bcv_ref[...] = bcv.astype(BF16)

        @pl.when(i % tps == 0)
        def _():
            ext_ref[...] = jnp.zeros_like(ext_ref)

        cc = _causal_conv3(ext_ref, bcv[:, d:2 * d] * bcv[:, 2 * d:], wc_ref)
        cc_ref[...] = cc.astype(BF16)
        y = (bcv[:, :d] * cc).astype(BF16)
        y_ref[...] = y
        x1_ref[...] = xv + _nn(y, _mat(wout_ref))

    tile = pl.BlockSpec((tt, d), lambda i: (i, 0))
    return pl.pallas_call(
        body, name=name, grid=(t_all // tt,),
        in_specs=[tile, _resident((1, d)), _rows_spec(w_in_t), _resident((3, d)), _rows_spec(w_out)],
        out_specs=[tile, pl.BlockSpec((tt, 3 * d), lambda i: (i, 0)), tile, tile],
        out_shape=[SDS((t_all, d), F32), SDS((t_all, 3 * d), BF16), SDS((t_all, d), BF16), SDS((t_all, d), BF16)],
        scratch_shapes=[pltpu.VMEM((SUBLANES, d), F32)],
        compiler_params=_cparams(),
    )(x, gain, w_in_t.arr, w_conv, w_out.arr)


def _fwd_ffn(x, gain, w_in_t, w_conv, w_down, seq, name):
    t_all, d = x.shape
    f = w_down.n * N_DEV
    tt = _token_tile(seq) // 2
    tps = seq // tt

    def body(x_ref, g_ref, win_ref, wc_ref, wd_ref, x2_ref, gate_ref, s_ref, uds_ref, a_ref, ext_ref):
        i = pl.program_id(0)
        xv = x_ref[...]
        r, xh = _rms_parts(xv)
        h = (xh * g_ref[...]).astype(BF16)

        @pl.when(i % tps == 0)
        def _():
            ext_ref[...] = jnp.zeros_like(ext_ref)

        w_in = _mat(win_ref)
        w_d = _mat(wd_ref)
        acc = xv
        half = f // 2
        for c0 in (0, half):
            cols = slice(c0, c0 + half)
            gate = _nt(h, w_in[c0:c0 + half])
            u = _nt(h, w_in[f + c0:f + c0 + half])
            gate_ref[:, cols] = gate.astype(BF16)
            before = ext_ref[:, cols]
            gc = (wc_ref[2:3, cols] * gate + wc_ref[1:2, cols] * _shifted_rows(gate, before, 1, True)
                  + wc_ref[0:1, cols] * _shifted_rows(gate, before, 2, True))
            ext_ref[:, cols] = gate[tt - SUBLANES:, :]
            sig = _sigmoid(gc)
            s = gc * sig
            s_ref[:, cols] = s.astype(BF16)
            uds_ref[:, cols] = (u * (sig * (1.0 + gc * (1.0 - sig)))).astype(BF16)
            a = (s * u).astype(BF16)
            a_ref[:, cols] = a
            acc = acc + _nn(a, w_d[c0:c0 + half])
        x2_ref[...] = acc

    wide = pl.BlockSpec((tt, f), lambda i: (i, 0))
    return pl.pallas_call(
        body, name=name, grid=(t_all // tt,),
        in_specs=[pl.BlockSpec((tt, d), lambda i: (i, 0)), _resident((1, d)), _rows_spec(w_in_t),
                  _resident((3, f)), _rows_spec(w_down)],
        out_specs=[pl.BlockSpec((tt, d), lambda i: (i, 0)), wide, wide, wide, wide],
        out_shape=[SDS((t_all, d), F32)] + [SDS((t_all, f), BF16)] * 4,
        scratch_shapes=[pltpu.VMEM((SUBLANES, f), F32)],
        compiler_params=_cparams(),
    )(x, gain, w_in_t.arr, w_conv, w_down.arr)


def _rope_partner(xs, lane_lo):
    return jnp.where(lane_lo, pltpu.roll(xs, LANES - HEAD_DIM // 2, 1), pltpu.roll(xs, HEAD_DIM // 2, 1))


def _fwd_qkv(x, gain, w_qkv_t, b_qkv, cos_t, sin_t, seq, name):
    t_all, d = x.shape
    width = w_qkv_t.n * N_DEV
    kvw = (width - d) // 2
    tt = _token_tile(seq)
    tps = seq // tt
    scale = HEAD_DIM ** -0.5

    def body(x_ref, g_ref, w_ref, b_ref, cos_ref, sin_ref, qkv_ref):
        xv = x_ref[...]
        r, xh = _rms_parts(xv)
        h = (xh * g_ref[...]).astype(BF16)
        qkv = _nt(h, _mat(w_ref)) + b_ref[...]
        cosv = cos_ref[...]
        sinv = sin_ref[...]
        lane_lo = (lax.broadcasted_iota(jnp.int32, (tt, LANES), 1) % HEAD_DIM) < HEAD_DIM // 2
        for s in range((d + kvw) // LANES):
            xs = qkv[:, s * LANES:(s + 1) * LANES]
            roped = xs * cosv + _rope_partner(xs, lane_lo) * sinv
            if s * LANES < d:
                roped = roped * scale
            qkv_ref[:, s * LANES:(s + 1) * LANES] = roped.astype(BF16)
        qkv_ref[:, d + kvw:] = qkv[:, d + kvw:].astype(BF16)

    return pl.pallas_call(
        body, name=name, grid=(t_all // tt,),
        in_specs=[pl.BlockSpec((tt, d), lambda i: (i, 0)), _resident((1, d)), _rows_spec(w_qkv_t),
                  _resident((1, width)), pl.BlockSpec((tt, LANES), lambda i: (i % tps, 0)),
                  pl.BlockSpec((tt, LANES), lambda i: (i % tps, 0))],
        out_specs=pl.BlockSpec((tt, width), lambda i: (i, 0)),
        out_shape=SDS((t_all, width), BF16),
        compiler_params=_cparams(),
    )(x, gain, w_qkv_t.arr, b_qkv, cos_t, sin_t)


def _stack_heads(ref, row0, kh):
    return jnp.concatenate(
        [ref[row0:row0 + WINDOW, (kh * GROUP + g) * HEAD_DIM:(kh * GROUP + g + 1) * HEAD_DIM] for g in range(GROUP)],
        axis=0)


def _band_bias():
    r = lax.broadcasted_iota(jnp.int32, (WINDOW, 2 * WINDOW), 0)
    j = lax.broadcasted_iota(jnp.int32, (WINDOW, 2 * WINDOW), 1)
    base = (j > r) & (j <= r + WINDOW)
    return jnp.where(base, 0.0, MASKED_SCORE), jnp.where(base & (j >= WINDOW), 0.0, MASKED_SCORE)


def _fwd_attention(qkv, x, sinks, w_o, b_o, seq, name):
    t_all, d = x.shape
    width = qkv.shape[1]
    kvw = (width - d) // 2
    n_kv = kvw // HEAD_DIM
    tt = _token_tile(seq)
    tps = seq // tt
    nblk = tt // WINDOW

    def body(sink_ref, qkv_ref, kvp_ref, x_ref, wo_ref, bo_ref, x1_ref, o_ref, kvext_ref, oscr_ref, bias_ref, s_ref, p_ref):
        i = pl.program_id(0)

        @pl.when(i == 0)
        def _():
            base, first = _band_bias()
            bias_ref[0], bias_ref[1] = base.T, first.T

        kvext_ref[0:WINDOW, :] = kvp_ref[...]
        kvext_ref[WINDOW:, :] = qkv_ref[:, d:]
        at_seq_start = (i % tps == 0).astype(jnp.int32)
        steps = [(n, kh) for n in range(nblk) for kh in range(n_kv)]

        def scores(step):
            n, kh = steps[step]
            qs = _stack_heads(qkv_ref, n * WINDOW, kh)
            kb = kvext_ref[n * WINDOW:(n + 2) * WINDOW, kh * HEAD_DIM:(kh + 1) * HEAD_DIM]
            s_ref[step % 2] = _nt(kb, qs)

        scores(0)
        for step, (n, kh) in enumerate(steps):
            buf = step % 2
            if step + 1 < len(steps):
                scores(step + 1)
            vb = kvext_ref[n * WINDOW:(n + 2) * WINDOW, kvw + kh * HEAD_DIM:kvw + (kh + 1) * HEAD_DIM]
            bias = bias_ref[at_seq_start if n == 0 else 0]
            for g in range(GROUP):
                cols = slice(g * WINDOW, (g + 1) * WINDOW)
                sink = sink_ref[kh * GROUP + g]
                sv = s_ref[buf, :, cols] + bias
                m = jnp.maximum(jnp.max(sv, axis=0, keepdims=True), sink)
                p = jnp.exp(sv - m)
                inv = 1.0 / (jnp.sum(p, axis=0, keepdims=True) + jnp.exp(sink - m))
                p_ref[buf, :, cols] = (p * inv).astype(BF16)
            o_s = _tn(vb, p_ref[buf]).T
            for g in range(GROUP):
                hd = kh * GROUP + g
                oscr_ref[n * WINDOW:(n + 1) * WINDOW, hd * HEAD_DIM:(hd + 1) * HEAD_DIM] = (
                    o_s[g * WINDOW:(g + 1) * WINDOW].astype(BF16))
        o = oscr_ref[...]
        o_ref[...] = o
        x1_ref[...] = x_ref[...] + _nn(o, _mat(wo_ref)) + bo_ref[...]

    kv_blocks = tt // WINDOW
    return pl.pallas_call(
        body, name=name, grid=(t_all // tt,),
        in_specs=[pl.BlockSpec(memory_space=pltpu.SMEM),
                  pl.BlockSpec((tt, width), lambda i: (i, 0)),
                  pl.BlockSpec((WINDOW, 2 * kvw), lambda i: (jnp.maximum(i * kv_blocks - 1, 0), d // (2 * kvw))),
                  pl.BlockSpec((tt, d), lambda i: (i, 0)), _rows_spec(w_o), _resident((1, d))],
        out_specs=[pl.BlockSpec((tt, d), lambda i: (i, 0)), pl.BlockSpec((tt, d), lambda i: (i, 0))],
        out_shape=[SDS((t_all, d), F32), SDS((t_all, d), BF16)],
        scratch_shapes=[pltpu.VMEM((tt + WINDOW, 2 * kvw), BF16), pltpu.VMEM((tt, d), BF16),
                        pltpu.VMEM((2, 2 * WINDOW, WINDOW), F32), pltpu.VMEM((2, 2 * WINDOW, GROUP * WINDOW), F32),
                        pltpu.VMEM((2, 2 * WINDOW, GROUP * WINDOW), BF16)],
        compiler_params=_cparams(),
    )(sinks, qkv, qkv, x, w_o.arr, b_o)


def _final_norm_loss(x, gain, target, name):
    t_all, d = x.shape
    tt = min(TOKEN_TILE, t_all)

    def body(x_ref, g_ref, t_ref, dx_ref, dg_ref, loss_ref):
        i = pl.program_id(0)
        xv = x_ref[...]
        r, xh = _rms_parts(xv)
        gain_v = g_ref[...]
        e = xh * gain_v - t_ref[...]
        dy = e * (1.0 / d)
        dx_ref[...] = _rms_backward(dy, xh, r, gain_v, 0.0)

        @pl.when(i == 0)
        def _():
            dg_ref[...] = jnp.zeros_like(dg_ref)
            loss_ref[...] = jnp.zeros_like(loss_ref)

        dg_ref[...] += jnp.sum(dy * xh, axis=0, keepdims=True)
        loss_ref[...] += (0.5 / d) * jnp.sum(e * e, axis=0, keepdims=True)

    return pl.pallas_call(
        body, name=name, grid=(t_all // tt,),
        in_specs=[pl.BlockSpec((tt, d), lambda i: (i, 0)), _resident((1, d)), pl.BlockSpec((tt, d), lambda i: (i, 0))],
        out_specs=[pl.BlockSpec((tt, d), lambda i: (i, 0)), pl.BlockSpec((1, d), lambda i: (0, 0)),
                   pl.BlockSpec((1, d), lambda i: (0, 0))],
        out_shape=[SDS((t_all, d), F32), SDS((1, d), F32), SDS((1, d), F32)],
        compiler_params=_cparams(),
    )(x, gain, target)


def _bwd_ffn_inner(dx2, gate, s_act, uds, w_conv, w_down, seq, name):
    t_all, d = dx2.shape
    f = w_down.n * N_DEV
    tt = _token_tile(seq)
    tps = seq // tt
    nt = t_all // tt

    def body(dx_ref, g_ref, s_ref, uds_ref, wc_ref, wd_ref, dgu_ref, dwc_ref, aext_ref, da_ref):
        i = pl.program_id(0)
        ti = nt - 1 - i
        da_ref[...] = _nt(dx_ref[...].astype(BF16), _mat(wd_ref))

        @pl.when(ti % tps == tps - 1)
        def _():
            aext_ref[...] = jnp.zeros_like(aext_ref)

        @pl.when(i == 0)
        def _():
            dwc_ref[...] = jnp.zeros_like(dwc_ref)

        for c in range(f // LANES):
            cols = slice(c * LANES, (c + 1) * LANES)
            da = da_ref[:, cols]
            g = g_ref[:, cols].astype(F32)
            dgc = da * uds_ref[:, cols].astype(F32)
            after = aext_ref[:, cols]
            sh1 = _shifted_rows(dgc, after, 1, False)
            sh2 = _shifted_rows(dgc, after, 2, False)
            aext_ref[:, cols] = dgc[0:SUBLANES, :]
            dg = wc_ref[2:3, cols] * dgc + wc_ref[1:2, cols] * sh1 + wc_ref[0:1, cols] * sh2
            dgu_ref[:, cols] = dg.astype(BF16)
            dgu_ref[:, f + c * LANES:f + (c + 1) * LANES] = (da * s_ref[:, cols].astype(F32)).astype(BF16)
            dwc_ref[0:1, cols] += jnp.sum(g * sh2, axis=0, keepdims=True)
            dwc_ref[1:2, cols] += jnp.sum(g * sh1, axis=0, keepdims=True)
            dwc_ref[2:3, cols] += jnp.sum(g * dgc, axis=0, keepdims=True)

    rev = lambda i: (nt - 1 - i, 0)
    return pl.pallas_call(
        body, name=name, grid=(nt,),
        in_specs=[pl.BlockSpec((tt, d), rev)] + [pl.BlockSpec((tt, f), rev)] * 3 + [_resident((3, f)), _rows_spec(w_down)],
        out_specs=[pl.BlockSpec((tt, 2 * f), rev), pl.BlockSpec((8, f), lambda i: (0, 0))],
        out_shape=[SDS((t_all, 2 * f), BF16), SDS((8, f), F32)],
        scratch_shapes=[pltpu.VMEM((SUBLANES, f), F32), pltpu.VMEM((tt, f), F32)],
        compiler_params=_cparams(),
    )(dx2, gate, s_act, uds, w_conv, w_down.arr)


def _bwd_conv_inner(dx1, bcv, cc, w_conv, w_out, seq, name):
    t_all, d = dx1.shape
    tt = _token_tile(seq)
    tps = seq // tt
    nt = t_all // tt

    def body(dx_ref, bcv_ref, cc_ref, wc_ref, wout_ref, dbcv_ref, dwc_ref, aext_ref, dy_ref):
        i = pl.program_id(0)
        ti = nt - 1 - i
        dy_ref[...] = _nt(dx_ref[...].astype(BF16), _mat(wout_ref))

        @pl.when(ti % tps == tps - 1)
        def _():
            aext_ref[...] = jnp.zeros_like(aext_ref)

        @pl.when(i == 0)
        def _():
            dwc_ref[...] = jnp.zeros_like(dwc_ref)

        for s in range(d // LANES):
            cols = slice(s * LANES, (s + 1) * LANES)
            ccols = slice(d + s * LANES, d + (s + 1) * LANES)
            vcols = slice(2 * d + s * LANES, 2 * d + (s + 1) * LANES)
            dy = dy_ref[:, cols]
            c = bcv_ref[:, ccols].astype(F32)
            v = bcv_ref[:, vcols].astype(F32)
            cv = c * v
            dcc = dy * bcv_ref[:, cols].astype(F32)
            after = aext_ref[:, cols]
            sh1 = _shifted_rows(dcc, after, 1, False)
            sh2 = _shifted_rows(dcc, after, 2, False)
            aext_ref[:, cols] = dcc[0:SUBLANES, :]
            dcv = wc_ref[2:3, cols] * dcc + wc_ref[1:2, cols] * sh1 + wc_ref[0:1, cols] * sh2
            dbcv_ref[:, cols] = (dy * cc_ref[:, cols].astype(F32)).astype(BF16)
            dbcv_ref[:, ccols] = (dcv * v).astype(BF16)
            dbcv_ref[:, vcols] = (dcv * c).astype(BF16)
            dwc_ref[0:1, cols] += jnp.sum(cv * sh2, axis=0, keepdims=True)
            dwc_ref[1:2, cols] += jnp.sum(cv * sh1, axis=0, keepdims=True)
            dwc_ref[2:3, cols] += jnp.sum(cv * dcc, axis=0, keepdims=True)

    return pl.pallas_call(
        body, name=name, grid=(nt,),
        in_specs=[pl.BlockSpec((tt, d), lambda i: (nt - 1 - i, 0)),
                  pl.BlockSpec((tt, 3 * d), lambda i: (nt - 1 - i, 0)),
                  pl.BlockSpec((tt, d), lambda i: (nt - 1 - i, 0)),
                  _resident((3, d)), _rows_spec(w_out)],
        out_specs=[pl.BlockSpec((tt, 3 * d), lambda i: (nt - 1 - i, 0)), pl.BlockSpec((8, d), lambda i: (0, 0))],
        out_shape=[SDS((t_all, 3 * d), BF16), SDS((8, d), F32)],
        scratch_shapes=[pltpu.VMEM((SUBLANES, d), F32), pltpu.VMEM((tt, d), F32)],
        compiler_params=_cparams(),
    )(dx1, bcv, cc, w_conv, w_out.arr)


def _bwd_attention_inner(dx1, qkv, sinks, w_o, cos_t, sin_t, seq, name):
    t_all, d = dx1.shape
    width = qkv.shape[1]
    kvw = (width - d) // 2
    n_kv = kvw // HEAD_DIM
    tt = _token_tile(seq)
    tps = seq // tt
    nt = t_all // tt
    nblk = tt // WINDOW
    scale = HEAD_DIM ** -0.5

    def body(sink_ref, dx_ref, qkv_ref, kvp_ref, cos_ref, sin_ref, wo_ref,
             dqkv_ref, dsink_ref, dbqkv_ref, dbo_ref,
             kvext_ref, dkvext_ref, carry_ref, dq_ref, do_ref, bias_ref, s_ref, dp_ref, p_ref, ds_ref):
        i = pl.program_id(0)
        ti = nt - 1 - i
        dxv = dx_ref[...]
        do_ref[...] = _nt(dxv.astype(BF16), _mat(wo_ref)).astype(BF16)
        kvext_ref[0:WINDOW, :] = kvp_ref[...]
        kvext_ref[WINDOW:, :] = qkv_ref[:, d:]
        dkvext_ref[...] = jnp.zeros_like(dkvext_ref)

        @pl.when(i == 0)
        def _():
            base, first = _band_bias()
            bias_ref[0], bias_ref[1] = base.T, first.T
            carry_ref[...] = jnp.zeros_like(carry_ref)
            dsink_ref[...] = jnp.zeros_like(dsink_ref)
            dbqkv_ref[...] = jnp.zeros_like(dbqkv_ref)
            dbo_ref[...] = jnp.zeros_like(dbo_ref)

        at_seq_start = (ti % tps == 0).astype(jnp.int32)
        head_lane = lax.broadcasted_iota(jnp.int32, (1, LANES), 1)
        dsink = jnp.zeros((1, LANES), F32)
        for n in range(nblk):
            for kh in range(n_kv):
                buf = (n * n_kv + kh) % 2
                qs = _stack_heads(qkv_ref, n * WINDOW, kh)
                dos = _stack_heads(do_ref, n * WINDOW, kh)
                kcols = slice(kh * HEAD_DIM, (kh + 1) * HEAD_DIM)
                vcols = slice(kvw + kh * HEAD_DIM, kvw + (kh + 1) * HEAD_DIM)
                band = slice(n * WINDOW, (n + 2) * WINDOW)
                kb = kvext_ref[band, kcols]
                vb = kvext_ref[band, vcols]
                s_ref[buf] = _nt(kb, qs)
                dp_ref[buf] = _nt(vb, dos)
                bias = bias_ref[at_seq_start if n == 0 else 0]
                for g in range(GROUP):
                    hd = kh * GROUP + g
                    cols = slice(g * WINDOW, (g + 1) * WINDOW)
                    sink = sink_ref[hd]
                    sv = s_ref[buf, :, cols] + bias
                    m = jnp.maximum(jnp.max(sv, axis=0, keepdims=True), sink)
                    p = jnp.exp(sv - m)
                    e_sink = jnp.exp(sink - m)
                    inv = 1.0 / (jnp.sum(p, axis=0, keepdims=True) + e_sink)
                    probs = p * inv
                    dp = dp_ref[buf, :, cols]
                    dsum = jnp.sum(probs * dp, axis=0, keepdims=True)
                    p_ref[buf, :, cols] = probs.astype(BF16)
                    ds_ref[buf, :, cols] = (probs * (dp - dsum)).astype(BF16)
                    dsink = dsink - jnp.where(head_lane == hd, jnp.sum(e_sink * inv * dsum), 0.0)
                ds_t = ds_ref[buf]
                dkvext_ref[band, vcols] += _nn(p_ref[buf], dos)
                dkvext_ref[band, kcols] += _nn(ds_t, qs)
                dq_s = _tn(kb, ds_t).T
                for g in range(GROUP):
                    hd = kh * GROUP + g
                    dq_ref[n * WINDOW:(n + 1) * WINDOW, hd * HEAD_DIM:(hd + 1) * HEAD_DIM] = dq_s[g * WINDOW:(g + 1) * WINDOW]
        dsink_ref[0:1, :] += dsink
        dkvext_ref[tt:tt + WINDOW, :] += carry_ref[...]
        carry_ref[...] = dkvext_ref[0:WINDOW, :]

        cosv = cos_ref[...]
        sinv = sin_ref[...]
        lane_lo = (lax.broadcasted_iota(jnp.int32, (tt, LANES), 1) % HEAD_DIM) < HEAD_DIM // 2
        for s in range((d + kvw) // LANES):
            if s * LANES < d:
                dy = dq_ref[:, s * LANES:(s + 1) * LANES] * scale
            else:
                dy = dkvext_ref[WINDOW:, s * LANES - d:(s + 1) * LANES - d]
            dpre = dy * cosv - _rope_partner(dy, lane_lo) * sinv
            dqkv_ref[:, s * LANES:(s + 1) * LANES] = dpre.astype(BF16)
            dbqkv_ref[0:1, s * LANES:(s + 1) * LANES] += jnp.sum(dpre, axis=0, keepdims=True)
        dv = dkvext_ref[WINDOW:, kvw:]
        dqkv_ref[:, d + kvw:] = dv.astype(BF16)
        dbqkv_ref[0:1, d + kvw:] += jnp.sum(dv, axis=0, keepdims=True)
        dbo_ref[...] += jnp.sum(dxv, axis=0, keepdims=True)

    kv_blocks = tt // WINDOW
    return pl.pallas_call(
        body, name=name, grid=(nt,),
        in_specs=[pl.BlockSpec(memory_space=pltpu.SMEM),
                  pl.BlockSpec((tt, d), lambda i: (nt - 1 - i, 0)),
                  pl.BlockSpec((tt, width), lambda i: (nt - 1 - i, 0)),
                  pl.BlockSpec((WINDOW, 2 * kvw), lambda i: (jnp.maximum((nt - 1 - i) * kv_blocks - 1, 0), d // (2 * kvw))),
                  pl.BlockSpec((tt, LANES), lambda i: ((nt - 1 - i) % tps, 0)),
                  pl.BlockSpec((tt, LANES), lambda i: ((nt - 1 - i) % tps, 0)),
                  _rows_spec(w_o)],
        out_specs=[pl.BlockSpec((tt, width), lambda i: (nt - 1 - i, 0)), pl.BlockSpec((8, LANES), lambda i: (0, 0)),
                   pl.BlockSpec((1, width), lambda i: (0, 0)), pl.BlockSpec((1, d), lambda i: (0, 0))],
        out_shape=[SDS((t_all, width), BF16), SDS((8, LANES), F32), SDS((1, width), F32), SDS((1, d), F32)],
        scratch_shapes=[pltpu.VMEM((tt + WINDOW, 2 * kvw), BF16), pltpu.VMEM((tt + WINDOW, 2 * kvw), F32),
                        pltpu.VMEM((WINDOW, 2 * kvw), F32), pltpu.VMEM((tt, d), F32), pltpu.VMEM((tt, d), BF16),
                        pltpu.VMEM((2, 2 * WINDOW, WINDOW), F32), pltpu.VMEM((2, 2 * WINDOW, GROUP * WINDOW), F32),
                        pltpu.VMEM((2, 2 * WINDOW, GROUP * WINDOW), F32), pltpu.VMEM((2, 2 * WINDOW, GROUP * WINDOW), BF16),
                        pltpu.VMEM((2, 2 * WINDOW, GROUP * WINDOW), BF16)],
        compiler_params=_cparams(),
    )(sinks, dx1, qkv, qkv, cos_t, sin_t, w_o.arr)


def _bwd_dense_norm(dy, w_t, x, gain, dres, name):
    t_all, d = x.shape
    n = dy.shape[1]
    tt = min(TOKEN_TILE, t_all)

    def body(dy_ref, w_ref, x_ref, g_ref, dres_ref, dx_ref, h_ref, dg_ref):
        i = pl.program_id(0)
        dh = _nn(dy_ref[...], _mat(w_ref))
        r, xh = _rms_parts(x_ref[...])
        gain_v = g_ref[...]
        h_ref[...] = (xh * gain_v).astype(BF16)
        dx_ref[...] = _rms_backward(dh, xh, r, gain_v, dres_ref[...])

        @pl.when(i == 0)
        def _():
            dg_ref[...] = jnp.zeros_like(dg_ref)

        dg_ref[...] += jnp.sum(dh * xh, axis=0, keepdims=True)

    return pl.pallas_call(
        body, name=name, grid=(t_all // tt,),
        in_specs=[pl.BlockSpec((tt, n), lambda i: (i, 0)), _rows_spec(w_t), pl.BlockSpec((tt, d), lambda i: (i, 0)),
                  _resident((1, d)), pl.BlockSpec((tt, d), lambda i: (i, 0))],
        out_specs=[pl.BlockSpec((tt, d), lambda i: (i, 0)), pl.BlockSpec((tt, d), lambda i: (i, 0)),
                   pl.BlockSpec((1, d), lambda i: (0, 0))],
        out_shape=[SDS((t_all, d), F32), SDS((t_all, d), BF16), SDS((1, d), F32)],
        compiler_params=_cparams(),
    )(dy, w_t.arr, x, gain, dres)


def _tn_matmul(a, b, dest, name):
    t_all, m = a.shape
    d = b.shape[1]
    n = dest.n
    assert m == N_DEV * n and dest.off % n == 0
    k = max(kk for kk in (1, 2, 4, 8) if kk * n <= max(n, 1536))
    tm = k * n
    tt = min(TN_TOKEN_TILE, t_all)
    n_t = t_all // tt
    fresh = not hasattr(dest.arr, "dtype")

    def body(a_ref, b_ref, *rest):
        o_ref, acc_ref = rest[-2:]
        t = pl.program_id(1)

        @pl.when(t == 0)
        def _():
            acc_ref[...] = jnp.zeros_like(acc_ref)

        acc_ref[...] += _tn(a_ref[...], b_ref[...].astype(BF16))

        @pl.when(t == n_t - 1)
        def _():
            o_ref[...] = acc_ref[...].astype(BF16).reshape(k, n, d)

    block = dest.off // n
    return pl.pallas_call(
        body, name=name, grid=(m // tm, n_t),
        in_specs=[pl.BlockSpec((tt, tm), lambda j, t: (t, j)), pl.BlockSpec((tt, d), lambda j, t: (t, 0))] + ([] if fresh else [ANY]),
        out_specs=pl.BlockSpec((k, n, d), lambda j, t: (j, block, 0)),
        out_shape=SDS(tuple(dest.arr) if fresh else dest.arr.shape, BF16),
        scratch_shapes=[pltpu.VMEM((tm, d), F32)],
        input_output_aliases={} if fresh else {2: 0},
        compiler_params=_cparams(2),
    )(*((a, b) if fresh else (a, b, dest.arr)))


def _my_place():
    return lax.axis_index("x"), lax.axis_index("y"), lax.axis_index("c")


def _other_chips(x, y):
    return [(1 - x, y), (x, 1 - y), (1 - x, 1 - y)]


def _all_gather(blocks, name):
    n_arr = len(blocks)

    def body(*refs):
        in_refs = refs[:n_arr]
        out_refs = refs[n_arr:2 * n_arr]
        send_sems, recv_sems, local_sems = refs[2 * n_arr:]
        x, y, c = _my_place()
        me, sibling = (x, y, c), (x, y, 1 - c)
        chips = _other_chips(x, y)

        def slot(a, place):
            px, py, pc = place
            return out_refs[a].at[4 * px + 2 * py + pc]

        def copy(a, k, block, to, src=None):
            return pltpu.make_async_remote_copy(
                src_ref=slot(a, block) if src is None else src, dst_ref=slot(a, block),
                send_sem=send_sems.at[a, k], recv_sem=recv_sems.at[a, k], device_id=to, device_id_type=MESH)

        started = []
        local = []
        for a in range(n_arr):
            mine = pltpu.make_async_copy(in_refs[a], slot(a, me), local_sems.at[a])
            mine.start()
            local.append(mine)
            first = [copy(a, 0, me, sibling, src=in_refs[a])]
            first += [copy(a, 1 + j, me, (*chip, c), src=in_refs[a]) for j, chip in enumerate(chips)]
            for cp in first:
                cp.start()
            started += first
        for a in range(n_arr):
            for j, chip in enumerate(chips):
                copy(a, 1 + j, (*chip, c), me).wait_recv()
                passed = copy(a, 4 + j, (*chip, c), sibling)
                passed.start()
                started.append(passed)
        for a in range(n_arr):
            copy(a, 0, sibling, me).wait_recv()
            for j, chip in enumerate(chips):
                copy(a, 4 + j, (*chip, 1 - c), me).wait_recv()
        for cp in started:
            cp.wait_send()
        for mine in local:
            mine.wait()

    return pl.pallas_call(
        body, name=name,
        in_specs=[ANY] * n_arr, out_specs=[ANY] * n_arr,
        out_shape=[SDS((N_DEV,) + b.shape, b.dtype) for b in blocks],
        scratch_shapes=[pltpu.SemaphoreType.DMA((n_arr, 7)), pltpu.SemaphoreType.DMA((n_arr, 7)),
                        pltpu.SemaphoreType.DMA((n_arr,))],
    )(*blocks)


def _peer_of(k, x, y, c):
    return x ^ ((k >> 2) & 1), y ^ ((k >> 1) & 1), c ^ (k & 1)


HBM = pl.BlockSpec(memory_space=pltpu.HBM)
SEM = pl.BlockSpec(memory_space=pltpu.SEMAPHORE)
DATAFLOW_EFFECT = pltpu.SideEffectType.DATAFLOW_SIDE_EFFECTING


def _peer_copies(src_ref, land_ref, send_sems, recv_sems, per_peer):
    x, y, c = _my_place()
    me = 4 * x + 2 * y + c
    copies = []
    for k in range(1, N_DEV):
        px, py, pc = _peer_of(k, x, y, c)
        peer = 4 * px + 2 * py + pc
        copies.append(pltpu.make_async_remote_copy(
            src_ref=src_ref.at[peer] if per_peer else src_ref, dst_ref=land_ref.at[me],
            send_sem=send_sems.at[k - 1], recv_sem=recv_sems.at[k - 1], device_id=(px, py, pc), device_id_type=MESH))
    own = pltpu.make_async_copy(src_ref.at[me] if per_peer else src_ref, land_ref.at[me], send_sems.at[N_DEV - 1])
    return copies, own


def _exchange_start(src, after, per_peer, name):
    rows, d = src.shape[-2:]

    def body(src_ref, land_ref, after_ref, send_sems, recv_sems, src_thru, land_thru, token):
        copies, own = _peer_copies(src_ref, land_ref, send_sems, recv_sems, per_peer)
        for cp in copies:
            cp.start()
        own.start()
        token[...] = jnp.zeros_like(token)

    return pl.pallas_call(
        body, name=name,
        out_shape=(pltpu.SemaphoreType.DMA((N_DEV,)), pltpu.SemaphoreType.DMA((N_DEV - 1,)), pltpu.HBM(src.shape, src.dtype),
                   pltpu.HBM((N_DEV, rows, d), src.dtype), SDS((SUBLANES, LANES), F32)),
        in_specs=(HBM, HBM, ANY), out_specs=(SEM, SEM, HBM, HBM, pl.BlockSpec(memory_space=pltpu.VMEM)),
        input_output_aliases={0: 2, 1: 3},
        compiler_params=pltpu.CompilerParams(has_side_effects=DATAFLOW_EFFECT),
    )(pltpu.with_memory_space_constraint(src, pltpu.HBM),
      pltpu.with_memory_space_constraint(lax.empty((N_DEV, rows, d), src.dtype), pltpu.HBM), after)


def _exchange_wait(started, after, per_peer, name):
    send_sems, recv_sems, src_thru, land_thru, _ = started

    def body(src_ref, land_ref, send_sems, recv_sems, after_ref, src_out, land_out):
        copies, own = _peer_copies(src_ref, land_ref, send_sems, recv_sems, per_peer)
        for cp in copies:
            cp.wait_send()
            cp.wait_recv()
        own.wait()

    return pl.pallas_call(
        body, name=name,
        out_shape=(pltpu.HBM(src_thru.shape, src_thru.dtype), pltpu.HBM(land_thru.shape, land_thru.dtype)),
        in_specs=(HBM, HBM, SEM, SEM, ANY), out_specs=(HBM, HBM), input_output_aliases={0: 0, 1: 1},
        compiler_params=pltpu.CompilerParams(has_side_effects=DATAFLOW_EFFECT),
    )(src_thru, land_thru, send_sems, recv_sems, after)


def _sum_slots(slots, name):
    _, rows, d = slots.shape
    tr = _largest_divisor(rows, 512, 16)

    def body(s_ref, o_ref):
        acc = s_ref[0].astype(F32)
        for dev in range(1, N_DEV):
            acc = acc + s_ref[dev].astype(F32)
        o_ref[...] = acc

    return pl.pallas_call(
        body, name=name, grid=(rows // tr,),
        in_specs=[pl.BlockSpec((N_DEV, tr, d), lambda r: (0, r, 0))], out_specs=pl.BlockSpec((tr, d), lambda r: (r, 0)),
        out_shape=SDS((rows, d), F32), compiler_params=_cparams(),
    )(slots)


def _all_reduce_small(part, loss_rows, name):
    rows, lanes = part.shape
    lo, hi = loss_rows

    def body(x_ref, out_ref, loss_ref, gath_ref, send_sems, recv_sems):
        x, y, c = _my_place()
        me = 4 * x + 2 * y + c
        gath_ref[me] = x_ref[...]
        copies = []
        for k in range(1, N_DEV):
            peer = (x ^ ((k >> 2) & 1), y ^ ((k >> 1) & 1), c ^ (k & 1))
            cp = pltpu.make_async_remote_copy(
                src_ref=x_ref, dst_ref=gath_ref.at[me], send_sem=send_sems.at[k - 1], recv_sem=recv_sems.at[k - 1],
                device_id=peer, device_id_type=MESH)
            cp.start()
            copies.append(cp)
        for cp in copies:
            cp.wait_recv()
        for cp in copies:
            cp.wait_send()
        acc = gath_ref[0]
        for dev in range(1, N_DEV):
            acc = acc + gath_ref[dev]
        out_ref[...] = acc
        loss_ref[...] = jnp.full(loss_ref.shape, jnp.sum(acc[lo:hi, :]), F32)

    vmem = pl.BlockSpec(memory_space=pltpu.VMEM)
    return pl.pallas_call(
        body, name=name, in_specs=[vmem], out_specs=[vmem, vmem],
        out_shape=[SDS((rows, lanes), F32), SDS((SUBLANES, LANES), F32)],
        scratch_shapes=[pltpu.VMEM((N_DEV, rows, lanes), F32), pltpu.SemaphoreType.DMA((N_DEV - 1,)),
                        pltpu.SemaphoreType.DMA((N_DEV - 1,))],
    )(part)


def _adamw(w, g, m, v, name):
    rows, cols = w.shape
    tr = rows if rows % SUBLANES else _largest_divisor(rows, 512, SUBLANES)

    def body(w_ref, g_ref, m_ref, v_ref, d_ref, nm_ref, nv_ref):
        gv = g_ref[...]
        nm = ADAM_B1 * m_ref[...] + (1.0 - ADAM_B1) * gv
        nv = ADAM_B2 * v_ref[...] + (1.0 - ADAM_B2) * (gv * gv)
        m_hat = nm / (1.0 - ADAM_B1 ** ADAM_STEP)
        v_hat = nv / (1.0 - ADAM_B2 ** ADAM_STEP)
        d_ref[...] = -ADAM_LR * (m_hat / (jnp.sqrt(v_hat) + ADAM_EPS) + ADAM_WD * w_ref[...])
        nm_ref[...] = nm
        nv_ref[...] = nv

    spec = pl.BlockSpec((tr, cols), lambda i: (i, 0))
    return pl.pallas_call(
        body, name=name, grid=(rows // tr,), in_specs=[spec] * 4, out_specs=[spec] * 3,
        out_shape=[SDS((rows, cols), F32)] * 3, compiler_params=_cparams(),
    )(w, g, m, v)


def _adamw_nd(w, g, m, v, name):
    shape = w.shape
    two_d = (1, shape[0]) if len(shape) == 1 else (-1, shape[-1])
    outs = _adamw(w.reshape(two_d), g.reshape(two_d), m.reshape(two_d), v.reshape(two_d), name)
    return [o.reshape(shape) for o in outs]


def _rope_tables(seq):
    pos = jnp.arange(seq, dtype=F32)
    inv_freq = 1.0 / (ROPE_THETA ** (jnp.arange(0, HEAD_DIM, 2, dtype=F32) / HEAD_DIM))
    ang = pos[:, None] * inv_freq[None, :]
    cos, sin = jnp.cos(ang), jnp.sin(ang)
    reps = LANES // HEAD_DIM
    cos_t = jnp.tile(jnp.concatenate([cos, cos], axis=1), (1, reps))
    sin_t = jnp.tile(jnp.concatenate([-sin, sin], axis=1), (1, reps))
    return cos_t, sin_t


def _flat_pad(a):
    flat = a.reshape(1, -1)
    pad = (-flat.shape[1]) % LANES
    return jnp.pad(flat, ((0, 0), (0, pad))) if pad else flat


def kernel(x, norm_mix, norm_ffn, norm_final, conv_w_in, conv_w_conv, conv_w_out, attn_w_qkv, attn_b_qkv, attn_sinks, attn_w_o, attn_b_o, ffn_w_in, ffn_w_conv, ffn_w_down, loss_target, m_norm_mix, m_norm_ffn, m_norm_final, m_conv_w_in, m_conv_w_conv, m_conv_w_out, m_attn_w_qkv, m_attn_b_qkv, m_attn_sinks, m_attn_w_o, m_attn_b_o, m_ffn_w_in, m_ffn_w_conv, m_ffn_w_down, v_norm_mix, v_norm_ffn, v_norm_final, v_conv_w_in, v_conv_w_conv, v_conv_w_out, v_attn_w_qkv, v_attn_b_qkv, v_attn_sinks, v_attn_w_o, v_attn_b_o, v_ffn_w_in, v_ffn_w_conv, v_ffn_w_down):
    b_loc, seq, d = x.shape
    depth = norm_mix.shape[0]
    n_conv, n_attn = conv_w_in.shape[0], attn_w_qkv.shape[0]
    t_all = b_loc * seq
    my_x, my_y, my_c = _my_place()

    me = 4 * my_x + 2 * my_y + my_c

    groups = []
    for i in range(depth):
        j = i // 2
        if i % 2 == 0:
            mix = [("conv_w_in", j, True, conv_w_in[j].T), ("conv_w_out", j, False, conv_w_out[j])]
        else:
            mix = [("attn_w_qkv", j, True, attn_w_qkv[j].T), ("attn_w_o", j, False, attn_w_o[j])]
        groups.append((("mix", i), mix))
        groups.append((("ffn", i), [("ffn_w_in", i, True, ffn_w_in[i].T), ("ffn_w_down", i, False, ffn_w_down[i])]))
    order = [key for key, _ in groups]
    members_of = dict(groups)

    def layout(key):
        offs, o = [], 0
        for _, _, _, shard in members_of[key]:
            n = shard.shape[0]
            o = -(-o // n) * n
            offs.append(o)
            o += n
        return offs, o

    small = jnp.concatenate([_flat_pad(conv_w_conv), _flat_pad(ffn_w_conv), _flat_pad(attn_b_qkv), _flat_pad(attn_b_o)], axis=1)
    (small_g,) = _all_gather([small], "gather_small")

    gather_started = {}

    def start_gather(idx, after):
        if idx >= len(order):
            return 0.0
        key = order[idx]
        offs, total = layout(key)
        pieces, o = [], 0
        for (_, _, _, shard), off in zip(members_of[key], offs):
            if off > o:
                pieces.append(jnp.zeros((off - o, d), shard.dtype))
            pieces.append(shard)
            o = off + shard.shape[0]
        pack = jnp.concatenate(pieces, axis=0).astype(BF16)
        gather_started[key] = _exchange_start(pack, after, False, f"gather_start_{key[0]}_{key[1]}")
        return gather_started[key][4][0, 0]

    weights = {}

    def finish_gather(key, after):
        _, land = _exchange_wait(gather_started[key], after, False, f"gather_wait_{key[0]}_{key[1]}")
        for (wname, layer, _, shard), off in zip(members_of[key], layout(key)[0]):
            weights[(wname, layer)] = _Rows(land, off, shard.shape[0])

    def take_small(o, shape):
        size = shape[0] * shape[1] * shape[2]
        blk = small_g[:, 0, o:o + size].reshape((N_DEV,) + shape)
        return jnp.moveaxis(blk, 0, 2).reshape(shape[0], shape[1], N_DEV * shape[2])

    so = 0
    wc_conv_full = take_small(so, conv_w_conv.shape); so += _flat_pad(conv_w_conv).shape[1]
    wc_ffn_full = take_small(so, ffn_w_conv.shape); so += _flat_pad(ffn_w_conv).shape[1]
    b_qkv_full = take_small(so, (n_attn, 1, attn_b_qkv.shape[1]))[:, 0]; so += _flat_pad(attn_b_qkv).shape[1]
    b_o_full = take_small(so, (n_attn, 1, attn_b_o.shape[1]))[:, 0]

    cos_t, sin_t = _rope_tables(seq)

    xs = [x.reshape(t_all, d)]
    saved = []
    token = start_gather(0, small_g) + start_gather(1, small_g)
    for i in range(depth):
        j = i // 2
        if i > 0:
            token = start_gather(2 * i + 2, xs[-1])
        gain_mix = norm_mix[i][None, :] + token
        finish_gather(("mix", i), gain_mix if i == 0 else xs[-1])
        if i % 2 == 0:
            x1, *mix_saved = _fwd_conv_mixer(xs[-1], gain_mix, weights[("conv_w_in", j)], wc_conv_full[j],
                                             weights[("conv_w_out", j)], seq, f"fwd_conv_{i}")
        else:
            qkv = _fwd_qkv(xs[-1], gain_mix, weights[("attn_w_qkv", j)], b_qkv_full[j][None, :], cos_t, sin_t, seq,
                           f"fwd_qkv_{i}")
            x1, o = _fwd_attention(qkv, xs[-1], attn_sinks[j], weights[("attn_w_o", j)], b_o_full[j][None, :], seq,
                                   f"fwd_attn_{i}")
            mix_saved = (qkv, o)
        token = start_gather(2 * i + 3, x1) + (start_gather(2, x1) if i == 0 else 0.0)
        gain_ffn = norm_ffn[i][None, :] + token
        finish_gather(("ffn", i), gain_ffn)
        x2, *ffn_saved = _fwd_ffn(x1, gain_ffn, weights[("ffn_w_in", i)], wc_ffn_full[i], weights[("ffn_w_down", i)],
                                  seq, f"fwd_ffn_{i}")
        saved.append((xs[-1], x1, mix_saved, ffn_saved))
        xs.append(x2)
        token = 0.0

    dx, dg_final, loss_lanes = _final_norm_loss(xs[-1], norm_final[None, :], loss_target.reshape(t_all, d), "loss_head")

    dg_mix, dg_ffn = [None] * depth, [None] * depth
    dwc_conv, dwc_ffn = [None] * n_conv, [None] * depth
    db_qkv, db_o, dsinks = [None] * n_attn, [None] * n_attn, [None] * n_attn
    scatter_started = {}

    def weight_grads(key, operands):
        offs, total = layout(key)
        parts = (N_DEV, total, d)
        for (wname, layer, _, shard), off, (a, b) in zip(members_of[key], offs, operands):
            parts = _tn_matmul(a, b, _Rows(parts, off, shard.shape[0]), f"dw_{wname}_{layer}")
        scatter_started[key] = _exchange_start(parts, operands[0][1], True, f"scatter_start_{key[0]}_{key[1]}")
        return scatter_started[key][4][0, 0]

    token = 0.0
    for i in reversed(range(depth)):
        j = i // 2
        x0, x1, mix_saved, (gate, s_act, uds, act) = saved[i]
        dgu, dwc = _bwd_ffn_inner(dx, gate, s_act, uds, wc_ffn_full[i] + token, weights[("ffn_w_down", i)], seq, f"bwd_ffn_{i}")
        dwc_ffn[i] = dwc[:3]
        dx1, h2, dg_ffn[i] = _bwd_dense_norm(dgu, weights[("ffn_w_in", i)], x1, norm_ffn[i][None, :], dx, f"bwd_ffn_norm_{i}")
        token = weight_grads(("ffn", i), [(dgu, h2), (act, dx)])
        if i % 2 == 0:
            bcv, cc, y = mix_saved
            dbcv, dwc = _bwd_conv_inner(dx1, bcv, cc, wc_conv_full[j] + token, weights[("conv_w_out", j)], seq, f"bwd_conv_{i}")
            dwc_conv[j] = dwc[:3]
            dx, h, dg_mix[i] = _bwd_dense_norm(dbcv, weights[("conv_w_in", j)], x0, norm_mix[i][None, :], dx1,
                                               f"bwd_conv_norm_{i}")
            token = weight_grads(("mix", i), [(dbcv, h), (y, dx1)])
        else:
            qkv, o = mix_saved
            dqkv, dsk, dbq, dbo = _bwd_attention_inner(dx1, qkv, attn_sinks[j] + token, weights[("attn_w_o", j)], cos_t, sin_t,
                                                       seq, f"bwd_attn_{i}")
            dsinks[j], db_qkv[j], db_o[j] = dsk[0:1, :attn_sinks.shape[1]], dbq, dbo
            dx, h, dg_mix[i] = _bwd_dense_norm(dqkv, weights[("attn_w_qkv", j)], x0, norm_mix[i][None, :], dx1,
                                               f"bwd_attn_norm_{i}")
            token = weight_grads(("mix", i), [(dqkv, h), (o, dx1)])
    grad_x = dx.reshape(b_loc, seq, d)

    reduced = {}

    def finish_scatter(key, after):
        _, land = _exchange_wait(scatter_started[key], after, True, f"scatter_wait_{key[0]}_{key[1]}")
        total = _sum_slots(land, f"scatter_sum_{key[0]}_{key[1]}")
        for (wname, layer, transposed, shard), off in zip(members_of[key], layout(key)[0]):
            rows = total[off:off + shard.shape[0]]
            reduced[(wname, layer)] = rows.T if transposed else rows

    last_key = order[0]
    for key in reversed(order[1:]):
        finish_scatter(key, dx)

    small_parts = [jnp.concatenate(dg_mix, axis=0), jnp.concatenate(dg_ffn, axis=0), dg_final,
                   jnp.stack(dwc_conv), jnp.stack(dwc_ffn), jnp.concatenate(db_qkv, axis=0), jnp.concatenate(db_o, axis=0),
                   jnp.concatenate(dsinks, axis=0), loss_lanes]
    flats = [_flat_pad(p) for p in small_parts]
    bounds = []
    so = 0
    for fl in flats:
        bounds.append((so, so + fl.shape[1]))
        so += fl.shape[1]
    small_rows = so // LANES
    pad_rows = (-small_rows) % SUBLANES
    part_small = jnp.pad(jnp.concatenate(flats, axis=1).reshape(small_rows, LANES), ((0, pad_rows), (0, 0)))
    loss_rows = (bounds[-1][0] // LANES, bounds[-1][1] // LANES)
    summed, loss_tile = _all_reduce_small(part_small, loss_rows, "reduce_small")
    summed = summed.reshape(1, -1)

    def small_grad(k, shape):
        lo = bounds[k][0]
        size = 1
        for s_ in shape:
            size *= s_
        return summed[0, lo:lo + size].reshape(shape)

    def my_cols(full, n_local):
        return lax.dynamic_slice_in_dim(full, me * n_local, n_local, axis=full.ndim - 1)

    g_norm_mix = small_grad(0, norm_mix.shape)
    g_norm_ffn = small_grad(1, norm_ffn.shape)
    g_norm_final = small_grad(2, norm_final.shape)
    g_conv_w_conv = my_cols(small_grad(3, (n_conv, 3, d)), conv_w_conv.shape[2])
    g_ffn_w_conv = my_cols(small_grad(4, (depth, 3, ffn_w_conv.shape[2] * N_DEV)), ffn_w_conv.shape[2])
    g_attn_b_qkv = my_cols(small_grad(5, (n_attn, attn_b_qkv.shape[1] * N_DEV)), attn_b_qkv.shape[1])
    g_attn_b_o = my_cols(small_grad(6, (n_attn, d)), attn_b_o.shape[1])
    g_attn_sinks = small_grad(7, attn_sinks.shape)
    loss = loss_tile[0, 0]

    def big_grad(wname, n_layers):
        return jnp.stack([reduced[(wname, layer)] for layer in range(n_layers)])

    grads = {
        "norm_mix": g_norm_mix, "norm_ffn": g_norm_ffn, "norm_final": g_norm_final, "conv_w_conv": g_conv_w_conv,
        "attn_w_qkv": big_grad("attn_w_qkv", n_attn), "attn_b_qkv": g_attn_b_qkv, "attn_sinks": g_attn_sinks,
        "attn_w_o": big_grad("attn_w_o", n_attn), "attn_b_o": g_attn_b_o,
        "ffn_w_in": big_grad("ffn_w_in", depth), "ffn_w_conv": g_ffn_w_conv, "ffn_w_down": big_grad("ffn_w_down", depth),
    }
    params = {
        "norm_mix": (norm_mix, m_norm_mix, v_norm_mix), "norm_ffn": (norm_ffn, m_norm_ffn, v_norm_ffn),
        "norm_final": (norm_final, m_norm_final, v_norm_final), "conv_w_in": (conv_w_in, m_conv_w_in, v_conv_w_in),
        "conv_w_conv": (conv_w_conv, m_conv_w_conv, v_conv_w_conv), "conv_w_out": (conv_w_out, m_conv_w_out, v_conv_w_out),
        "attn_w_qkv": (attn_w_qkv, m_attn_w_qkv, v_attn_w_qkv), "attn_b_qkv": (attn_b_qkv, m_attn_b_qkv, v_attn_b_qkv),
        "attn_sinks": (attn_sinks, m_attn_sinks, v_attn_sinks), "attn_w_o": (attn_w_o, m_attn_w_o, v_attn_w_o),
        "attn_b_o": (attn_b_o, m_attn_b_o, v_attn_b_o), "ffn_w_in": (ffn_w_in, m_ffn_w_in, v_ffn_w_in),
        "ffn_w_conv": (ffn_w_conv, m_ffn_w_conv, v_ffn_w_conv), "ffn_w_down": (ffn_w_down, m_ffn_w_down, v_ffn_w_down),
    }
    names = list(params)
    updates = {}

    def update(wname):
        w, m, v = params[wname]
        updates[wname] = _adamw_nd(w, grads[wname], m, v, f"adamw_{wname}")

    last_names = sorted({wname for wname, _, _, _ in members_of[last_key]})
    for wname in names:
        if wname not in last_names:
            update(wname)
    finish_scatter(last_key, updates["ffn_w_in"][0])
    for wname in last_names:
        grads[wname] = big_grad(wname, params[wname][0].shape[0])
        update(wname)
    return (loss, grad_x, *[grads[wname] for wname in names], *[updates[wname][0] for wname in names],
            *[updates[wname][1] for wname in names], *[updates[wname][2] for wname in names])
```

```python
from typing import NamedTuple

import jax
import jax.numpy as jnp
from jax import lax
from jax.experimental import pallas as pl
from jax.experimental.pallas import tpu as pltpu

F32 = jnp.float32
BF16 = jnp.bfloat16
SDS = jax.ShapeDtypeStruct
MESH = pl.DeviceIdType.MESH
ANY = pl.BlockSpec(memory_space=pl.ANY)

N_DEV = 8
EPS = 1e-5
HEAD_DIM = 64
GROUP = 4
WINDOW = 128
ROPE_THETA = 10000.0
ADAM_LR, ADAM_B1, ADAM_B2, ADAM_EPS, ADAM_WD, ADAM_STEP = 0.001, 0.9, 0.999, 1e-08, 0.01, 10

V7X_VMEM_BYTES = 64 * 1024 * 1024
VMEM_LIMIT_BYTES = V7X_VMEM_BYTES - 8 * 1024 * 1024
LANES = 128
SUBLANES = 8
TOKEN_TILE = 512
TN_TOKEN_TILE = 2048
MASKED_SCORE = -1e30


def _cparams(n_axes=1):
    return pltpu.CompilerParams(dimension_semantics=("arbitrary",) * n_axes, vmem_limit_bytes=VMEM_LIMIT_BYTES)


def _resident(shape):
    zeros = (0,) * len(shape)
    return pl.BlockSpec(shape, lambda *_: zeros, pipeline_mode=pl.Buffered(1))


class _Rows(NamedTuple):
    arr: jax.Array
    off: int
    n: int


def _rows_spec(w):
    assert w.off % w.n == 0
    block = w.off // w.n
    return pl.BlockSpec((N_DEV, w.n, w.arr.shape[2]), lambda *_: (0, block, 0), pipeline_mode=pl.Buffered(1))


def _mat(ref):
    v = ref[...]
    return v.reshape(v.shape[0] * v.shape[1], v.shape[2])


def _token_tile(seq):
    return min(TOKEN_TILE, seq // 2)


def _largest_divisor(m, cap, mult):
    best = None
    for d in range(mult, min(m, cap) + 1, mult):
        if m % d == 0:
            best = d
    return m if best is None else best


def _nt(a, b):
    return lax.dot_general(a, b, (((1,), (1,)), ((), ())), preferred_element_type=F32)


def _nn(a, b):
    return lax.dot_general(a, b, (((1,), (0,)), ((), ())), preferred_element_type=F32)


def _tn(a, b):
    return lax.dot_general(a, b, (((0,), (0,)), ((), ())), preferred_element_type=F32)


def _rms_parts(xv):
    r = lax.rsqrt(jnp.mean(xv * xv, axis=-1, keepdims=True) + EPS)
    return r, xv * r


def _rms_backward(dh, xh, r, gain, dres):
    u = dh * gain
    return dres + r * (u - xh * jnp.mean(u * xh, axis=-1, keepdims=True))


def _shifted_rows(xv, edge, k, down):
    n = xv.shape[0]
    row = lax.broadcasted_iota(jnp.int32, edge.shape, 0)
    if down:
        rolled = pltpu.roll(xv, k, 0)
        head = jnp.where(row < k, pltpu.roll(edge, k, 0), rolled[0:SUBLANES])
        return jnp.concatenate([head, rolled[SUBLANES:]], axis=0)
    rolled = pltpu.roll(xv, n - k, 0)
    tail = jnp.where(row >= SUBLANES - k, pltpu.roll(edge, SUBLANES - k, 0), rolled[n - SUBLANES:])
    return jnp.concatenate([rolled[:n - SUBLANES], tail], axis=0)


def _causal_conv3(edge_ref, xv, w_ref):
    before = edge_ref[...]
    y = (w_ref[2:3, :] * xv + w_ref[1:2, :] * _shifted_rows(xv, before, 1, True)
         + w_ref[0:1, :] * _shifted_rows(xv, before, 2, True))
    edge_ref[...] = xv[xv.shape[0] - SUBLANES:, :]
    return y


def _sigmoid(z):
    return 1.0 / (1.0 + jnp.exp(-z))


def _fwd_conv_mixer(x, gain, w_in_t, w_conv, w_out, seq, name):
    t_all, d = x.shape
    tt = _token_tile(seq)
    tps = seq // tt

    def body(x_ref, g_ref, win_ref, wc_ref, wout_ref, x1_ref, bcv_ref, cc_ref, y_ref, ext_ref):
        i = pl.program_id(0)
        xv = x_ref[...]
        r, xh = _rms_parts(xv)
        h = (xh * g_ref[...]).astype(BF16)
        bcv = _nt(h, _mat(win_ref))
        bcv_ref[...] = bcv.astype(BF16)

        @pl.when(i % tps == 0)
        def _():
            ext_ref[...] = jnp.zeros_like(ext_ref)

        cc = _causal_conv3(ext_ref, bcv[:, d:2 * d] * bcv[:, 2 * d:], wc_ref)
        cc_ref[...] = cc.astype(BF16)
        y = (bcv[:, :d] * cc).astype(BF16)
        y_ref[...] = y
        x1_ref[...] = xv + _nn(y, _mat(wout_ref))

    tile = pl.BlockSpec((tt, d), lambda i: (i, 0))
    return pl.pallas_call(
        body, name=name, grid=(t_all // tt,),
        in_specs=[tile, _resident((1, d)), _rows_spec(w_in_t), _resident((3, d)), _rows_spec(w_out)],
        out_specs=[tile, pl.BlockSpec((tt, 3 * d), lambda i: (i, 0)), tile, tile],
        out_shape=[SDS((t_all, d), F32), SDS((t_all, 3 * d), BF16), SDS((t_all, d), BF16), SDS((t_all, d), BF16)],
        scratch_shapes=[pltpu.VMEM((SUBLANES, d), F32)],
        compiler_params=_cparams(),
    )(x, gain, w_in_t.arr, w_conv, w_out.arr)


def _fwd_ffn(x, gain, w_in_t, w_conv, w_down, seq, name):
    t_all, d = x.shape
    f = w_down.n * N_DEV
    tt = _token_tile(seq) // 2
    tps = seq // tt

    def body(x_ref, g_ref, win_ref, wc_ref, wd_ref, x2_ref, gate_ref, s_ref, uds_ref, a_ref, ext_ref):
        i = pl.program_id(0)
        xv = x_ref[...]
        r, xh = _rms_parts(xv)
        h = (xh * g_ref[...]).astype(BF16)
        gu = _nt(h, _mat(win_ref))
        gate = gu[:, :f]
        u = gu[:, f:]
        gate_ref[...] = gate.astype(BF16)

        @pl.when(i % tps == 0)
        def _():
            ext_ref[...] = jnp.zeros_like(ext_ref)

        gc = _causal_conv3(ext_ref, gate, wc_ref)
        sig = _sigmoid(gc)
        s = gc * sig
        s_ref[...] = s.astype(BF16)
        uds_ref[...] = (u * (sig * (1.0 + gc * (1.0 - sig)))).astype(BF16)
        a = (s * u).astype(BF16)
        a_ref[...] = a
        x2_ref[...] = xv + _nn(a, _mat(wd_ref))

    wide = pl.BlockSpec((tt, f), lambda i: (i, 0))
    return pl.pallas_call(
        body, name=name, grid=(t_all // tt,),
        in_specs=[pl.BlockSpec((tt, d), lambda i: (i, 0)), _resident((1, d)), _rows_spec(w_in_t),
                  _resident((3, f)), _rows_spec(w_down)],
        out_specs=[pl.BlockSpec((tt, d), lambda i: (i, 0)), wide, wide, wide, wide],
        out_shape=[SDS((t_all, d), F32)] + [SDS((t_all, f), BF16)] * 4,
        scratch_shapes=[pltpu.VMEM((SUBLANES, f), F32)],
        compiler_params=_cparams(),
    )(x, gain, w_in_t.arr, w_conv, w_down.arr)


def _rope_partner(xs, lane_lo):
    return jnp.where(lane_lo, pltpu.roll(xs, LANES - HEAD_DIM // 2, 1), pltpu.roll(xs, HEAD_DIM // 2, 1))


def _fwd_qkv(x, gain, w_qkv_t, b_qkv, cos_t, sin_t, seq, name):
    t_all, d = x.shape
    width = w_qkv_t.n * N_DEV
    kvw = (width - d) // 2
    tt = _token_tile(seq)
    tps = seq // tt
    scale = HEAD_DIM ** -0.5

    def body(x_ref, g_ref, w_ref, b_ref, cos_ref, sin_ref, qkv_ref):
        xv = x_ref[...]
        r, xh = _rms_parts(xv)
        h = (xh * g_ref[...]).astype(BF16)
        qkv = _nt(h, _mat(w_ref)) + b_ref[...]
        cosv = cos_ref[...]
        sinv = sin_ref[...]
        lane_lo = (lax.broadcasted_iota(jnp.int32, (tt, LANES), 1) % HEAD_DIM) < HEAD_DIM // 2
        for s in range((d + kvw) // LANES):
            xs = qkv[:, s * LANES:(s + 1) * LANES]
            roped = xs * cosv + _rope_partner(xs, lane_lo) * sinv
            if s * LANES < d:
                roped = roped * scale
            qkv_ref[:, s * LANES:(s + 1) * LANES] = roped.astype(BF16)
        qkv_ref[:, d + kvw:] = qkv[:, d + kvw:].astype(BF16)

    return pl.pallas_call(
        body, name=name, grid=(t_all // tt,),
        in_specs=[pl.BlockSpec((tt, d), lambda i: (i, 0)), _resident((1, d)), _rows_spec(w_qkv_t),
                  _resident((1, width)), pl.BlockSpec((tt, LANES), lambda i: (i % tps, 0)),
                  pl.BlockSpec((tt, LANES), lambda i: (i % tps, 0))],
        out_specs=pl.BlockSpec((tt, width), lambda i: (i, 0)),
        out_shape=SDS((t_all, width), BF16),
        compiler_params=_cparams(),
    )(x, gain, w_qkv_t.arr, b_qkv, cos_t, sin_t)


def _stack_heads(ref, row0, kh):
    return jnp.concatenate(
        [ref[row0:row0 + WINDOW, (kh * GROUP + g) * HEAD_DIM:(kh * GROUP + g + 1) * HEAD_DIM] for g in range(GROUP)],
        axis=0)


def _band_bias():
    r = lax.broadcasted_iota(jnp.int32, (WINDOW, 2 * WINDOW), 0)
    j = lax.broadcasted_iota(jnp.int32, (WINDOW, 2 * WINDOW), 1)
    base = (j > r) & (j <= r + WINDOW)
    return jnp.where(base, 0.0, MASKED_SCORE), jnp.where(base & (j >= WINDOW), 0.0, MASKED_SCORE)


def _fwd_attention(qkv, x, sinks, w_o, b_o, seq, name):
    t_all, d = x.shape
    width = qkv.shape[1]
    kvw = (width - d) // 2
    n_kv = kvw // HEAD_DIM
    tt = _token_tile(seq)
    tps = seq // tt
    nblk = tt // WINDOW

    def body(sink_ref, qkv_ref, kvp_ref, x_ref, wo_ref, bo_ref, x1_ref, o_ref, kvext_ref, oscr_ref, bias_ref, s_ref, p_ref):
        i = pl.program_id(0)

        @pl.when(i == 0)
        def _():
            base, first = _band_bias()
            bias_ref[0], bias_ref[1] = base.T, first.T

        kvext_ref[0:WINDOW, :] = kvp_ref[...]
        kvext_ref[WINDOW:, :] = qkv_ref[:, d:]
        at_seq_start = (i % tps == 0).astype(jnp.int32)
        steps = [(n, kh) for n in range(nblk) for kh in range(n_kv)]

        def scores(step):
            n, kh = steps[step]
            qs = _stack_heads(qkv_ref, n * WINDOW, kh)
            kb = kvext_ref[n * WINDOW:(n + 2) * WINDOW, kh * HEAD_DIM:(kh + 1) * HEAD_DIM]
            s_ref[step % 2] = _nt(kb, qs)

        scores(0)
        for step, (n, kh) in enumerate(steps):
            buf = step % 2
            if step + 1 < len(steps):
                scores(step + 1)
            vb = kvext_ref[n * WINDOW:(n + 2) * WINDOW, kvw + kh * HEAD_DIM:kvw + (kh + 1) * HEAD_DIM]
            bias = bias_ref[at_seq_start if n == 0 else 0]
            for g in range(GROUP):
                cols = slice(g * WINDOW, (g + 1) * WINDOW)
                sink = sink_ref[kh * GROUP + g]
                sv = s_ref[buf, :, cols] + bias
                m = jnp.maximum(jnp.max(sv, axis=0, keepdims=True), sink)
                p = jnp.exp(sv - m)
                inv = 1.0 / (jnp.sum(p, axis=0, keepdims=True) + jnp.exp(sink - m))
                p_ref[buf, :, cols] = (p * inv).astype(BF16)
            o_s = _tn(vb, p_ref[buf]).T
            for g in range(GROUP):
                hd = kh * GROUP + g
                oscr_ref[n * WINDOW:(n + 1) * WINDOW, hd * HEAD_DIM:(hd + 1) * HEAD_DIM] = (
                    o_s[g * WINDOW:(g + 1) * WINDOW].astype(BF16))
        o = oscr_ref[...]
        o_ref[...] = o
        x1_ref[...] = x_ref[...] + _nn(o, _mat(wo_ref)) + bo_ref[...]

    kv_blocks = tt // WINDOW
    return pl.pallas_call(
        body, name=name, grid=(t_all // tt,),
        in_specs=[pl.BlockSpec(memory_space=pltpu.SMEM),
                  pl.BlockSpec((tt, width), lambda i: (i, 0)),
                  pl.BlockSpec((WINDOW, 2 * kvw), lambda i: (jnp.maximum(i * kv_blocks - 1, 0), d // (2 * kvw))),
                  pl.BlockSpec((tt, d), lambda i: (i, 0)), _rows_spec(w_o), _resident((1, d))],
        out_specs=[pl.BlockSpec((tt, d), lambda i: (i, 0)), pl.BlockSpec((tt, d), lambda i: (i, 0))],
        out_shape=[SDS((t_all, d), F32), SDS((t_all, d), BF16)],
        scratch_shapes=[pltpu.VMEM((tt + WINDOW, 2 * kvw), BF16), pltpu.VMEM((tt, d), BF16),
                        pltpu.VMEM((2, 2 * WINDOW, WINDOW), F32), pltpu.VMEM((2, 2 * WINDOW, GROUP * WINDOW), F32),
                        pltpu.VMEM((2, 2 * WINDOW, GROUP * WINDOW), BF16)],
        compiler_params=_cparams(),
    )(sinks, qkv, qkv, x, w_o.arr, b_o)


def _final_norm_loss(x, gain, target, name):
    t_all, d = x.shape
    tt = min(TOKEN_TILE, t_all)

    def body(x_ref, g_ref, t_ref, dx_ref, dg_ref, loss_ref):
        i = pl.program_id(0)
        xv = x_ref[...]
        r, xh = _rms_parts(xv)
        gain_v = g_ref[...]
        e = xh * gain_v - t_ref[...]
        dy = e * (1.0 / d)
        dx_ref[...] = _rms_backward(dy, xh, r, gain_v, 0.0)

        @pl.when(i == 0)
        def _():
            dg_ref[...] = jnp.zeros_like(dg_ref)
            loss_ref[...] = jnp.zeros_like(loss_ref)

        dg_ref[...] += jnp.sum(dy * xh, axis=0, keepdims=True)
        loss_ref[...] += (0.5 / d) * jnp.sum(e * e, axis=0, keepdims=True)

    return pl.pallas_call(
        body, name=name, grid=(t_all // tt,),
        in_specs=[pl.BlockSpec((tt, d), lambda i: (i, 0)), _resident((1, d)), pl.BlockSpec((tt, d), lambda i: (i, 0))],
        out_specs=[pl.BlockSpec((tt, d), lambda i: (i, 0)), pl.BlockSpec((1, d), lambda i: (0, 0)),
                   pl.BlockSpec((1, d), lambda i: (0, 0))],
        out_shape=[SDS((t_all, d), F32), SDS((1, d), F32), SDS((1, d), F32)],
        compiler_params=_cparams(),
    )(x, gain, target)


def _bwd_ffn_inner(dx2, gate, s_act, uds, w_conv, w_down, seq, name):
    t_all, d = dx2.shape
    f = w_down.n * N_DEV
    tt = _token_tile(seq)
    tps = seq // tt
    nt = t_all // tt

    def body(dx_ref, g_ref, s_ref, uds_ref, wc_ref, wd_ref, dgu_ref, dwc_ref, aext_ref, da_ref):
        i = pl.program_id(0)
        ti = nt - 1 - i
        da_ref[...] = _nt(dx_ref[...].astype(BF16), _mat(wd_ref))

        @pl.when(ti % tps == tps - 1)
        def _():
            aext_ref[...] = jnp.zeros_like(aext_ref)

        @pl.when(i == 0)
        def _():
            dwc_ref[...] = jnp.zeros_like(dwc_ref)

        for c in range(f // LANES):
            cols = slice(c * LANES, (c + 1) * LANES)
            da = da_ref[:, cols]
            g = g_ref[:, cols].astype(F32)
            dgc = da * uds_ref[:, cols].astype(F32)
            after = aext_ref[:, cols]
            sh1 = _shifted_rows(dgc, after, 1, False)
            sh2 = _shifted_rows(dgc, after, 2, False)
            aext_ref[:, cols] = dgc[0:SUBLANES, :]
            dg = wc_ref[2:3, cols] * dgc + wc_ref[1:2, cols] * sh1 + wc_ref[0:1, cols] * sh2
            dgu_ref[:, cols] = dg.astype(BF16)
            dgu_ref[:, f + c * LANES:f + (c + 1) * LANES] = (da * s_ref[:, cols].astype(F32)).astype(BF16)
            dwc_ref[0:1, cols] += jnp.sum(g * sh2, axis=0, keepdims=True)
            dwc_ref[1:2, cols] += jnp.sum(g * sh1, axis=0, keepdims=True)
            dwc_ref[2:3, cols] += jnp.sum(g * dgc, axis=0, keepdims=True)

    rev = lambda i: (nt - 1 - i, 0)
    return pl.pallas_call(
        body, name=name, grid=(nt,),
        in_specs=[pl.BlockSpec((tt, d), rev)] + [pl.BlockSpec((tt, f), rev)] * 3 + [_resident((3, f)), _rows_spec(w_down)],
        out_specs=[pl.BlockSpec((tt, 2 * f), rev), pl.BlockSpec((8, f), lambda i: (0, 0))],
        out_shape=[SDS((t_all, 2 * f), BF16), SDS((8, f), F32)],
        scratch_shapes=[pltpu.VMEM((SUBLANES, f), F32), pltpu.VMEM((tt, f), F32)],
        compiler_params=_cparams(),
    )(dx2, gate, s_act, uds, w_conv, w_down.arr)


def _bwd_conv_inner(dx1, bcv, cc, w_conv, w_out, seq, name):
    t_all, d = dx1.shape
    tt = _token_tile(seq)
    tps = seq // tt
    nt = t_all // tt

    def body(dx_ref, bcv_ref, cc_ref, wc_ref, wout_ref, dbcv_ref, dwc_ref, aext_ref, dy_ref):
        i = pl.program_id(0)
        ti = nt - 1 - i
        dy_ref[...] = _nt(dx_ref[...].astype(BF16), _mat(wout_ref))

        @pl.when(ti % tps == tps - 1)
        def _():
            aext_ref[...] = jnp.zeros_like(aext_ref)

        @pl.when(i == 0)
        def _():
            dwc_ref[...] = jnp.zeros_like(dwc_ref)

        for s in range(d // LANES):
            cols = slice(s * LANES, (s + 1) * LANES)
            ccols = slice(d + s * LANES, d + (s + 1) * LANES)
            vcols = slice(2 * d + s * LANES, 2 * d + (s + 1) * LANES)
            dy = dy_ref[:, cols]
            c = bcv_ref[:, ccols].astype(F32)
            v = bcv_ref[:, vcols].astype(F32)
            cv = c * v
            dcc = dy * bcv_ref[:, cols].astype(F32)
            after = aext_ref[:, cols]
            sh1 = _shifted_rows(dcc, after, 1, False)
            sh2 = _shifted_rows(dcc, after, 2, False)
            aext_ref[:, cols] = dcc[0:SUBLANES, :]
            dcv = wc_ref[2:3, cols] * dcc + wc_ref[1:2, cols] * sh1 + wc_ref[0:1, cols] * sh2
            dbcv_ref[:, cols] = (dy * cc_ref[:, cols].astype(F32)).astype(BF16)
            dbcv_ref[:, ccols] = (dcv * v).astype(BF16)
            dbcv_ref[:, vcols] = (dcv * c).astype(BF16)
            dwc_ref[0:1, cols] += jnp.sum(cv * sh2, axis=0, keepdims=True)
            dwc_ref[1:2, cols] += jnp.sum(cv * sh1, axis=0, keepdims=True)
            dwc_ref[2:3, cols] += jnp.sum(cv * dcc, axis=0, keepdims=True)

    return pl.pallas_call(
        body, name=name, grid=(nt,),
        in_specs=[pl.BlockSpec((tt, d), lambda i: (nt - 1 - i, 0)),
                  pl.BlockSpec((tt, 3 * d), lambda i: (nt - 1 - i, 0)),
                  pl.BlockSpec((tt, d), lambda i: (nt - 1 - i, 0)),
                  _resident((3, d)), _rows_spec(w_out)],
        out_specs=[pl.BlockSpec((tt, 3 * d), lambda i: (nt - 1 - i, 0)), pl.BlockSpec((8, d), lambda i: (0, 0))],
        out_shape=[SDS((t_all, 3 * d), BF16), SDS((8, d), F32)],
        scratch_shapes=[pltpu.VMEM((SUBLANES, d), F32), pltpu.VMEM((tt, d), F32)],
        compiler_params=_cparams(),
    )(dx1, bcv, cc, w_conv, w_out.arr)


def _bwd_attention_inner(dx1, qkv, sinks, w_o, cos_t, sin_t, seq, name):
    t_all, d = dx1.shape
    width = qkv.shape[1]
    kvw = (width - d) // 2
    n_kv = kvw // HEAD_DIM
    tt = _token_tile(seq)
    tps = seq // tt
    nt = t_all // tt
    nblk = tt // WINDOW
    scale = HEAD_DIM ** -0.5

    def body(sink_ref, dx_ref, qkv_ref, kvp_ref, cos_ref, sin_ref, wo_ref,
             dqkv_ref, dsink_ref, dbqkv_ref, dbo_ref,
             kvext_ref, dkvext_ref, carry_ref, dq_ref, do_ref, bias_ref, s_ref, dp_ref, p_ref, ds_ref):
        i = pl.program_id(0)
        ti = nt - 1 - i
        dxv = dx_ref[...]
        do_ref[...] = _nt(dxv.astype(BF16), _mat(wo_ref)).astype(BF16)
        kvext_ref[0:WINDOW, :] = kvp_ref[...]
        kvext_ref[WINDOW:, :] = qkv_ref[:, d:]
        dkvext_ref[...] = jnp.zeros_like(dkvext_ref)

        @pl.when(i == 0)
        def _():
            base, first = _band_bias()
            bias_ref[0], bias_ref[1] = base.T, first.T
            carry_ref[...] = jnp.zeros_like(carry_ref)
            dsink_ref[...] = jnp.zeros_like(dsink_ref)
            dbqkv_ref[...] = jnp.zeros_like(dbqkv_ref)
            dbo_ref[...] = jnp.zeros_like(dbo_ref)

        at_seq_start = (ti % tps == 0).astype(jnp.int32)
        head_lane = lax.broadcasted_iota(jnp.int32, (1, LANES), 1)
        dsink = jnp.zeros((1, LANES), F32)
        for n in range(nblk):
            for kh in range(n_kv):
                buf = (n * n_kv + kh) % 2
                qs = _stack_heads(qkv_ref, n * WINDOW, kh)
                dos = _stack_heads(do_ref, n * WINDOW, kh)
                kcols = slice(kh * HEAD_DIM, (kh + 1) * HEAD_DIM)
                vcols = slice(kvw + kh * HEAD_DIM, kvw + (kh + 1) * HEAD_DIM)
                band = slice(n * WINDOW, (n + 2) * WINDOW)
                kb = kvext_ref[band, kcols]
                vb = kvext_ref[band, vcols]
                s_ref[buf] = _nt(kb, qs)
                dp_ref[buf] = _nt(vb, dos)
                bias = bias_ref[at_seq_start if n == 0 else 0]
                for g in range(GROUP):
                    hd = kh * GROUP + g
                    cols = slice(g * WINDOW, (g + 1) * WINDOW)
                    sink = sink_ref[hd]
                    sv = s_ref[buf, :, cols] + bias
                    m = jnp.maximum(jnp.max(sv, axis=0, keepdims=True), sink)
                    p = jnp.exp(sv - m)
                    e_sink = jnp.exp(sink - m)
                    inv = 1.0 / (jnp.sum(p, axis=0, keepdims=True) + e_sink)
                    probs = p * inv
                    dp = dp_ref[buf, :, cols]
                    dsum = jnp.sum(probs * dp, axis=0, keepdims=True)
                    p_ref[buf, :, cols] = probs.astype(BF16)
                    ds_ref[buf, :, cols] = (probs * (dp - dsum)).astype(BF16)
                    dsink = dsink - jnp.where(head_lane == hd, jnp.sum(e_sink * inv * dsum), 0.0)
                ds_t = ds_ref[buf]
                dkvext_ref[band, vcols] += _nn(p_ref[buf], dos)
                dkvext_ref[band, kcols] += _nn(ds_t, qs)
                dq_s = _tn(kb, ds_t).T
                for g in range(GROUP):
                    hd = kh * GROUP + g
                    dq_ref[n * WINDOW:(n + 1) * WINDOW, hd * HEAD_DIM:(hd + 1) * HEAD_DIM] = dq_s[g * WINDOW:(g + 1) * WINDOW]
        dsink_ref[0:1, :] += dsink
        dkvext_ref[tt:tt + WINDOW, :] += carry_ref[...]
        carry_ref[...] = dkvext_ref[0:WINDOW, :]

        cosv = cos_ref[...]
        sinv = sin_ref[...]
        lane_lo = (lax.broadcasted_iota(jnp.int32, (tt, LANES), 1) % HEAD_DIM) < HEAD_DIM // 2
        for s in range((d + kvw) // LANES):
            if s * LANES < d:
                dy = dq_ref[:, s * LANES:(s + 1) * LANES] * scale
            else:
                dy = dkvext_ref[WINDOW:, s * LANES - d:(s + 1) * LANES - d]
            dpre = dy * cosv - _rope_partner(dy, lane_lo) * sinv
            dqkv_ref[:, s * LANES:(s + 1) * LANES] = dpre.astype(BF16)
            dbqkv_ref[0:1, s * LANES:(s + 1) * LANES] += jnp.sum(dpre, axis=0, keepdims=True)
        dv = dkvext_ref[WINDOW:, kvw:]
        dqkv_ref[:, d + kvw:] = dv.astype(BF16)
        dbqkv_ref[0:1, d + kvw:] += jnp.sum(dv, axis=0, keepdims=True)
        dbo_ref[...] += jnp.sum(dxv, axis=0, keepdims=True)

    kv_blocks = tt // WINDOW
    return pl.pallas_call(
        body, name=name, grid=(nt,),
        in_specs=[pl.BlockSpec(memory_space=pltpu.SMEM),
                  pl.BlockSpec((tt, d), lambda i: (nt - 1 - i, 0)),
                  pl.BlockSpec((tt, width), lambda i: (nt - 1 - i, 0)),
                  pl.BlockSpec((WINDOW, 2 * kvw), lambda i: (jnp.maximum((nt - 1 - i) * kv_blocks - 1, 0), d // (2 * kvw))),
                  pl.BlockSpec((tt, LANES), lambda i: ((nt - 1 - i) % tps, 0)),
                  pl.BlockSpec((tt, LANES), lambda i: ((nt - 1 - i) % tps, 0)),
                  _rows_spec(w_o)],
        out_specs=[pl.BlockSpec((tt, width), lambda i: (nt - 1 - i, 0)), pl.BlockSpec((8, LANES), lambda i: (0, 0)),
                   pl.BlockSpec((1, width), lambda i: (0, 0)), pl.BlockSpec((1, d), lambda i: (0, 0))],
        out_shape=[SDS((t_all, width), BF16), SDS((8, LANES), F32), SDS((1, width), F32), SDS((1, d), F32)],
        scratch_shapes=[pltpu.VMEM((tt + WINDOW, 2 * kvw), BF16), pltpu.VMEM((tt + WINDOW, 2 * kvw), F32),
                        pltpu.VMEM((WINDOW, 2 * kvw), F32), pltpu.VMEM((tt, d), F32), pltpu.VMEM((tt, d), BF16),
                        pltpu.VMEM((2, 2 * WINDOW, WINDOW), F32), pltpu.VMEM((2, 2 * WINDOW, GROUP * WINDOW), F32),
                        pltpu.VMEM((2, 2 * WINDOW, GROUP * WINDOW), F32), pltpu.VMEM((2, 2 * WINDOW, GROUP * WINDOW), BF16),
                        pltpu.VMEM((2, 2 * WINDOW, GROUP * WINDOW), BF16)],
        compiler_params=_cparams(),
    )(sinks, dx1, qkv, qkv, cos_t, sin_t, w_o.arr)


def _bwd_dense_norm(dy, w_t, x, gain, dres, name):
    t_all, d = x.shape
    n = dy.shape[1]
    tt = min(TOKEN_TILE, t_all)

    def body(dy_ref, w_ref, x_ref, g_ref, dres_ref, dx_ref, h_ref, dg_ref):
        i = pl.program_id(0)
        dh = _nn(dy_ref[...], _mat(w_ref))
        r, xh = _rms_parts(x_ref[...])
        gain_v = g_ref[...]
        h_ref[...] = (xh * gain_v).astype(BF16)
        dx_ref[...] = _rms_backward(dh, xh, r, gain_v, dres_ref[...])

        @pl.when(i == 0)
        def _():
            dg_ref[...] = jnp.zeros_like(dg_ref)

        dg_ref[...] += jnp.sum(dh * xh, axis=0, keepdims=True)

    return pl.pallas_call(
        body, name=name, grid=(t_all // tt,),
        in_specs=[pl.BlockSpec((tt, n), lambda i: (i, 0)), _rows_spec(w_t), pl.BlockSpec((tt, d), lambda i: (i, 0)),
                  _resident((1, d)), pl.BlockSpec((tt, d), lambda i: (i, 0))],
        out_specs=[pl.BlockSpec((tt, d), lambda i: (i, 0)), pl.BlockSpec((tt, d), lambda i: (i, 0)),
                   pl.BlockSpec((1, d), lambda i: (0, 0))],
        out_shape=[SDS((t_all, d), F32), SDS((t_all, d), BF16), SDS((1, d), F32)],
        compiler_params=_cparams(),
    )(dy, w_t.arr, x, gain, dres)


def _tn_matmul(a, b, dest, name):
    t_all, m = a.shape
    d = b.shape[1]
    n = dest.n
    assert m == N_DEV * n and dest.off % n == 0
    k = max(kk for kk in (1, 2, 4, 8) if kk * n <= max(n, 1536))
    tm = k * n
    tt = min(TN_TOKEN_TILE, t_all)
    n_t = t_all // tt
    fresh = not hasattr(dest.arr, "dtype")

    def body(a_ref, b_ref, *rest):
        o_ref, acc_ref = rest[-2:]
        t = pl.program_id(1)

        @pl.when(t == 0)
        def _():
            acc_ref[...] = jnp.zeros_like(acc_ref)

        acc_ref[...] += _tn(a_ref[...], b_ref[...].astype(BF16))

        @pl.when(t == n_t - 1)
        def _():
            o_ref[...] = acc_ref[...].astype(BF16).reshape(k, n, d)

    block = dest.off // n
    return pl.pallas_call(
        body, name=name, grid=(m // tm, n_t),
        in_specs=[pl.BlockSpec((tt, tm), lambda j, t: (t, j)), pl.BlockSpec((tt, d), lambda j, t: (t, 0))] + ([] if fresh else [ANY]),
        out_specs=pl.BlockSpec((k, n, d), lambda j, t: (j, block, 0)),
        out_shape=SDS(tuple(dest.arr) if fresh else dest.arr.shape, BF16),
        scratch_shapes=[pltpu.VMEM((tm, d), F32)],
        input_output_aliases={} if fresh else {2: 0},
        compiler_params=_cparams(2),
    )(*((a, b) if fresh else (a, b, dest.arr)))


def _my_place():
    return lax.axis_index("x"), lax.axis_index("y"), lax.axis_index("c")


def _other_chips(x, y):
    return [(1 - x, y), (x, 1 - y), (1 - x, 1 - y)]


def _all_gather(blocks, name):
    n_arr = len(blocks)

    def body(*refs):
        in_refs = refs[:n_arr]
        out_refs = refs[n_arr:2 * n_arr]
        send_sems, recv_sems, local_sems = refs[2 * n_arr:]
        x, y, c = _my_place()
        me, sibling = (x, y, c), (x, y, 1 - c)
        chips = _other_chips(x, y)

        def slot(a, place):
            px, py, pc = place
            return out_refs[a].at[4 * px + 2 * py + pc]

        def copy(a, k, block, to, src=None):
            return pltpu.make_async_remote_copy(
                src_ref=slot(a, block) if src is None else src, dst_ref=slot(a, block),
                send_sem=send_sems.at[a, k], recv_sem=recv_sems.at[a, k], device_id=to, device_id_type=MESH)

        started = []
        local = []
        for a in range(n_arr):
            mine = pltpu.make_async_copy(in_refs[a], slot(a, me), local_sems.at[a])
            mine.start()
            local.append(mine)
            first = [copy(a, 0, me, sibling, src=in_refs[a])]
            first += [copy(a, 1 + j, me, (*chip, c), src=in_refs[a]) for j, chip in enumerate(chips)]
            for cp in first:
                cp.start()
            started += first
        for a in range(n_arr):
            for j, chip in enumerate(chips):
                copy(a, 1 + j, (*chip, c), me).wait_recv()
                passed = copy(a, 4 + j, (*chip, c), sibling)
                passed.start()
                started.append(passed)
        for a in range(n_arr):
            copy(a, 0, sibling, me).wait_recv()
            for j, chip in enumerate(chips):
                copy(a, 4 + j, (*chip, 1 - c), me).wait_recv()
        for cp in started:
            cp.wait_send()
        for mine in local:
            mine.wait()

    return pl.pallas_call(
        body, name=name,
        in_specs=[ANY] * n_arr, out_specs=[ANY] * n_arr,
        out_shape=[SDS((N_DEV,) + b.shape, b.dtype) for b in blocks],
        scratch_shapes=[pltpu.SemaphoreType.DMA((n_arr, 7)), pltpu.SemaphoreType.DMA((n_arr, 7)),
                        pltpu.SemaphoreType.DMA((n_arr,))],
    )(*blocks)


def _peer_of(k, x, y, c):
    return x ^ ((k >> 2) & 1), y ^ ((k >> 1) & 1), c ^ (k & 1)


HBM = pl.BlockSpec(memory_space=pltpu.HBM)
SEM = pl.BlockSpec(memory_space=pltpu.SEMAPHORE)
DATAFLOW_EFFECT = pltpu.SideEffectType.DATAFLOW_SIDE_EFFECTING


def _peer_copies(src_ref, land_ref, send_sems, recv_sems, per_peer):
    x, y, c = _my_place()
    me = 4 * x + 2 * y + c
    copies = []
    for k in range(1, N_DEV):
        px, py, pc = _peer_of(k, x, y, c)
        peer = 4 * px + 2 * py + pc
        copies.append(pltpu.make_async_remote_copy(
            src_ref=src_ref.at[peer] if per_peer else src_ref, dst_ref=land_ref.at[me],
            send_sem=send_sems.at[k - 1], recv_sem=recv_sems.at[k - 1], device_id=(px, py, pc), device_id_type=MESH))
    own = pltpu.make_async_copy(src_ref.at[me] if per_peer else src_ref, land_ref.at[me], send_sems.at[N_DEV - 1])
    return copies, own


def _exchange_start(src, after, per_peer, name):
    rows, d = src.shape[-2:]

    def body(src_ref, land_ref, after_ref, send_sems, recv_sems, src_thru, land_thru, token):
        copies, own = _peer_copies(src_ref, land_ref, send_sems, recv_sems, per_peer)
        for cp in copies:
            cp.start()
        own.start()
        token[...] = jnp.zeros_like(token)

    return pl.pallas_call(
        body, name=name,
        out_shape=(pltpu.SemaphoreType.DMA((N_DEV,)), pltpu.SemaphoreType.DMA((N_DEV - 1,)), pltpu.HBM(src.shape, src.dtype),
                   pltpu.HBM((N_DEV, rows, d), src.dtype), SDS((SUBLANES, LANES), F32)),
        in_specs=(HBM, HBM, ANY), out_specs=(SEM, SEM, HBM, HBM, pl.BlockSpec(memory_space=pltpu.VMEM)),
        input_output_aliases={0: 2, 1: 3},
        compiler_params=pltpu.CompilerParams(has_side_effects=DATAFLOW_EFFECT),
    )(pltpu.with_memory_space_constraint(src, pltpu.HBM),
      pltpu.with_memory_space_constraint(lax.empty((N_DEV, rows, d), src.dtype), pltpu.HBM), after)


def _exchange_wait(started, after, per_peer, name):
    send_sems, recv_sems, src_thru, land_thru, _ = started

    def body(src_ref, land_ref, send_sems, recv_sems, after_ref, src_out, land_out):
        copies, own = _peer_copies(src_ref, land_ref, send_sems, recv_sems, per_peer)
        for cp in copies:
            cp.wait_send()
            cp.wait_recv()
        own.wait()

    return pl.pallas_call(
        body, name=name,
        out_shape=(pltpu.HBM(src_thru.shape, src_thru.dtype), pltpu.HBM(land_thru.shape, land_thru.dtype)),
        in_specs=(HBM, HBM, SEM, SEM, ANY), out_specs=(HBM, HBM), input_output_aliases={0: 0, 1: 1},
        compiler_params=pltpu.CompilerParams(has_side_effects=DATAFLOW_EFFECT),
    )(src_thru, land_thru, send_sems, recv_sems, after)


def _sum_slots(slots, name):
    _, rows, d = slots.shape
    tr = _largest_divisor(rows, 512, 16)

    def body(s_ref, o_ref):
        acc = s_ref[0].astype(F32)
        for dev in range(1, N_DEV):
            acc = acc + s_ref[dev].astype(F32)
        o_ref[...] = acc

    return pl.pallas_call(
        body, name=name, grid=(rows // tr,),
        in_specs=[pl.BlockSpec((N_DEV, tr, d), lambda r: (0, r, 0))], out_specs=pl.BlockSpec((tr, d), lambda r: (r, 0)),
        out_shape=SDS((rows, d), F32), compiler_params=_cparams(),
    )(slots)


def _all_reduce_small(part, loss_rows, name):
    rows, lanes = part.shape
    lo, hi = loss_rows

    def body(x_ref, out_ref, loss_ref, gath_ref, send_sems, recv_sems):
        x, y, c = _my_place()
        me = 4 * x + 2 * y + c
        gath_ref[me] = x_ref[...]
        copies = []
        for k in range(1, N_DEV):
            peer = (x ^ ((k >> 2) & 1), y ^ ((k >> 1) & 1), c ^ (k & 1))
            cp = pltpu.make_async_remote_copy(
                src_ref=x_ref, dst_ref=gath_ref.at[me], send_sem=send_sems.at[k - 1], recv_sem=recv_sems.at[k - 1],
                device_id=peer, device_id_type=MESH)
            cp.start()
            copies.append(cp)
        for cp in copies:
            cp.wait_recv()
        for cp in copies:
            cp.wait_send()
        acc = gath_ref[0]
        for dev in range(1, N_DEV):
            acc = acc + gath_ref[dev]
        out_ref[...] = acc
        loss_ref[...] = jnp.full(loss_ref.shape, jnp.sum(acc[lo:hi, :]), F32)

    vmem = pl.BlockSpec(memory_space=pltpu.VMEM)
    return pl.pallas_call(
        body, name=name, in_specs=[vmem], out_specs=[vmem, vmem],
        out_shape=[SDS((rows, lanes), F32), SDS((SUBLANES, LANES), F32)],
        scratch_shapes=[pltpu.VMEM((N_DEV, rows, lanes), F32), pltpu.SemaphoreType.DMA((N_DEV - 1,)),
                        pltpu.SemaphoreType.DMA((N_DEV - 1,))],
    )(part)


def _adamw(w, g, m, v, name):
    rows, cols = w.shape
    tr = rows if rows % SUBLANES else _largest_divisor(rows, 512, SUBLANES)

    def body(w_ref, g_ref, m_ref, v_ref, d_ref, nm_ref, nv_ref):
        gv = g_ref[...]
        nm = ADAM_B1 * m_ref[...] + (1.0 - ADAM_B1) * gv
        nv = ADAM_B2 * v_ref[...] + (1.0 - ADAM_B2) * (gv * gv)
        m_hat = nm / (1.0 - ADAM_B1 ** ADAM_STEP)
        v_hat = nv / (1.0 - ADAM_B2 ** ADAM_STEP)
        d_ref[...] = -ADAM_LR * (m_hat / (jnp.sqrt(v_hat) + ADAM_EPS) + ADAM_WD * w_ref[...])
        nm_ref[...] = nm
        nv_ref[...] = nv

    spec = pl.BlockSpec((tr, cols), lambda i: (i, 0))
    return pl.pallas_call(
        body, name=name, grid=(rows // tr,), in_specs=[spec] * 4, out_specs=[spec] * 3,
        out_shape=[SDS((rows, cols), F32)] * 3, compiler_params=_cparams(),
    )(w, g, m, v)


def _adamw_nd(w, g, m, v, name):
    shape = w.shape
    two_d = (1, shape[0]) if len(shape) == 1 else (-1, shape[-1])
    outs = _adamw(w.reshape(two_d), g.reshape(two_d), m.reshape(two_d), v.reshape(two_d), name)
    return [o.reshape(shape) for o in outs]


def _rope_tables(seq):
    pos = jnp.arange(seq, dtype=F32)
    inv_freq = 1.0 / (ROPE_THETA ** (jnp.arange(0, HEAD_DIM, 2, dtype=F32) / HEAD_DIM))
    ang = pos[:, None] * inv_freq[None, :]
    cos, sin = jnp.cos(ang), jnp.sin(ang)
    reps = LANES // HEAD_DIM
    cos_t = jnp.tile(jnp.concatenate([cos, cos], axis=1), (1, reps))
    sin_t = jnp.tile(jnp.concatenate([-sin, sin], axis=1), (1, reps))
    return cos_t, sin_t


def _flat_pad(a):
    flat = a.reshape(1, -1)
    pad = (-flat.shape[1]) % LANES
    return jnp.pad(flat, ((0, 0), (0, pad))) if pad else flat


def kernel(x, norm_mix, norm_ffn, norm_final, conv_w_in, conv_w_conv, conv_w_out, attn_w_qkv, attn_b_qkv, attn_sinks, attn_w_o, attn_b_o, ffn_w_in, ffn_w_conv, ffn_w_down, loss_target, m_norm_mix, m_norm_ffn, m_norm_final, m_conv_w_in, m_conv_w_conv, m_conv_w_out, m_attn_w_qkv, m_attn_b_qkv, m_attn_sinks, m_attn_w_o, m_attn_b_o, m_ffn_w_in, m_ffn_w_conv, m_ffn_w_down, v_norm_mix, v_norm_ffn, v_norm_final, v_conv_w_in, v_conv_w_conv, v_conv_w_out, v_attn_w_qkv, v_attn_b_qkv, v_attn_sinks, v_attn_w_o, v_attn_b_o, v_ffn_w_in, v_ffn_w_conv, v_ffn_w_down):
    b_loc, seq, d = x.shape
    depth = norm_mix.shape[0]
    n_conv, n_attn = conv_w_in.shape[0], attn_w_qkv.shape[0]
    t_all = b_loc * seq
    my_x, my_y, my_c = _my_place()

    me = 4 * my_x + 2 * my_y + my_c

    groups = []
    for i in range(depth):
        j = i // 2
        if i % 2 == 0:
            mix = [("conv_w_in", j, True, conv_w_in[j].T), ("conv_w_out", j, False, conv_w_out[j])]
        else:
            mix = [("attn_w_qkv", j, True, attn_w_qkv[j].T), ("attn_w_o", j, False, attn_w_o[j])]
        groups.append((("mix", i), mix))
        groups.append((("ffn", i), [("ffn_w_in", i, True, ffn_w_in[i].T), ("ffn_w_down", i, False, ffn_w_down[i])]))
    order = [key for key, _ in groups]
    members_of = dict(groups)

    def layout(key):
        offs, o = [], 0
        for _, _, _, shard in members_of[key]:
            n = shard.shape[0]
            o = -(-o // n) * n
            offs.append(o)
            o += n
        return offs, o

    small = jnp.concatenate([_flat_pad(conv_w_conv), _flat_pad(ffn_w_conv), _flat_pad(attn_b_qkv), _flat_pad(attn_b_o)], axis=1)
    (small_g,) = _all_gather([small], "gather_small")

    gather_started = {}

    def start_gather(idx, after):
        if idx >= len(order):
            return 0.0
        key = order[idx]
        offs, total = layout(key)
        pieces, o = [], 0
        for (_, _, _, shard), off in zip(members_of[key], offs):
            if off > o:
                pieces.append(jnp.zeros((off - o, d), shard.dtype))
            pieces.append(shard)
            o = off + shard.shape[0]
        pack = jnp.concatenate(pieces, axis=0).astype(BF16)
        gather_started[key] = _exchange_start(pack, after, False, f"gather_start_{key[0]}_{key[1]}")
        return gather_started[key][4][0, 0]

    weights = {}

    def finish_gather(key, after):
        _, land = _exchange_wait(gather_started[key], after, False, f"gather_wait_{key[0]}_{key[1]}")
        for (wname, layer, _, shard), off in zip(members_of[key], layout(key)[0]):
            weights[(wname, layer)] = _Rows(land, off, shard.shape[0])

    def take_small(o, shape):
        size = shape[0] * shape[1] * shape[2]
        blk = small_g[:, 0, o:o + size].reshape((N_DEV,) + shape)
        return jnp.moveaxis(blk, 0, 2).reshape(shape[0], shape[1], N_DEV * shape[2])

    so = 0
    wc_conv_full = take_small(so, conv_w_conv.shape); so += _flat_pad(conv_w_conv).shape[1]
    wc_ffn_full = take_small(so, ffn_w_conv.shape); so += _flat_pad(ffn_w_conv).shape[1]
    b_qkv_full = take_small(so, (n_attn, 1, attn_b_qkv.shape[1]))[:, 0]; so += _flat_pad(attn_b_qkv).shape[1]
    b_o_full = take_small(so, (n_attn, 1, attn_b_o.shape[1]))[:, 0]

    cos_t, sin_t = _rope_tables(seq)

    xs = [x.reshape(t_all, d)]
    saved = []
    token = start_gather(0, small_g) + start_gather(1, small_g)
    for i in range(depth):
        j = i // 2
        if i > 0:
            token = start_gather(2 * i + 2, xs[-1])
        gain_mix = norm_mix[i][None, :] + token
        finish_gather(("mix", i), gain_mix if i == 0 else xs[-1])
        if i % 2 == 0:
            x1, *mix_saved = _fwd_conv_mixer(xs[-1], gain_mix, weights[("conv_w_in", j)], wc_conv_full[j],
                                             weights[("conv_w_out", j)], seq, f"fwd_conv_{i}")
        else:
            qkv = _fwd_qkv(xs[-1], gain_mix, weights[("attn_w_qkv", j)], b_qkv_full[j][None, :], cos_t, sin_t, seq,
                           f"fwd_qkv_{i}")
            x1, o = _fwd_attention(qkv, xs[-1], attn_sinks[j], weights[("attn_w_o", j)], b_o_full[j][None, :], seq,
                                   f"fwd_attn_{i}")
            mix_saved = (qkv, o)
        token = start_gather(2 * i + 3, x1) + (start_gather(2, x1) if i == 0 else 0.0)
        gain_ffn = norm_ffn[i][None, :] + token
        finish_gather(("ffn", i), gain_ffn)
        x2, *ffn_saved = _fwd_ffn(x1, gain_ffn, weights[("ffn_w_in", i)], wc_ffn_full[i], weights[("ffn_w_down", i)],
                                  seq, f"fwd_ffn_{i}")
        saved.append((xs[-1], x1, mix_saved, ffn_saved))
        xs.append(x2)
        token = 0.0

    dx, dg_final, loss_lanes = _final_norm_loss(xs[-1], norm_final[None, :], loss_target.reshape(t_all, d), "loss_head")

    dg_mix, dg_ffn = [None] * depth, [None] * depth
    dwc_conv, dwc_ffn = [None] * n_conv, [None] * depth
    db_qkv, db_o, dsinks = [None] * n_attn, [None] * n_attn, [None] * n_attn
    scatter_started = {}

    def weight_grads(key, operands):
        offs, total = layout(key)
        parts = (N_DEV, total, d)
        for (wname, layer, _, shard), off, (a, b) in zip(members_of[key], offs, operands):
            parts = _tn_matmul(a, b, _Rows(parts, off, shard.shape[0]), f"dw_{wname}_{layer}")
        scatter_started[key] = _exchange_start(parts, operands[0][1], True, f"scatter_start_{key[0]}_{key[1]}")
        return scatter_started[key][4][0, 0]

    token = 0.0
    for i in reversed(range(depth)):
        j = i // 2
        x0, x1, mix_saved, (gate, s_act, uds, act) = saved[i]
        dgu, dwc = _bwd_ffn_inner(dx, gate, s_act, uds, wc_ffn_full[i] + token, weights[("ffn_w_down", i)], seq, f"bwd_ffn_{i}")
        dwc_ffn[i] = dwc[:3]
        dx1, h2, dg_ffn[i] = _bwd_dense_norm(dgu, weights[("ffn_w_in", i)], x1, norm_ffn[i][None, :], dx, f"bwd_ffn_norm_{i}")
        token = weight_grads(("ffn", i), [(dgu, h2), (act, dx)])
        if i % 2 == 0:
            bcv, cc, y = mix_saved
            dbcv, dwc = _bwd_conv_inner(dx1, bcv, cc, wc_conv_full[j] + token, weights[("conv_w_out", j)], seq, f"bwd_conv_{i}")
            dwc_conv[j] = dwc[:3]
            dx, h, dg_mix[i] = _bwd_dense_norm(dbcv, weights[("conv_w_in", j)], x0, norm_mix[i][None, :], dx1,
                                               f"bwd_conv_norm_{i}")
            token = weight_grads(("mix", i), [(dbcv, h), (y, dx1)])
        else:
            qkv, o = mix_saved
            dqkv, dsk, dbq, dbo = _bwd_attention_inner(dx1, qkv, attn_sinks[j] + token, weights[("attn_w_o", j)], cos_t, sin_t,
                                                       seq, f"bwd_attn_{i}")
            dsinks[j], db_qkv[j], db_o[j] = dsk[0:1, :attn_sinks.shape[1]], dbq, dbo
            dx, h, dg_mix[i] = _bwd_dense_norm(dqkv, weights[("attn_w_qkv", j)], x0, norm_mix[i][None, :], dx1,
                                               f"bwd_attn_norm_{i}")
            token = weight_grads(("mix", i), [(dqkv, h), (o, dx1)])
    grad_x = dx.reshape(b_loc, seq, d)

    reduced = {}

    def finish_scatter(key, after):
        _, land = _exchange_wait(scatter_started[key], after, True, f"scatter_wait_{key[0]}_{key[1]}")
        total = _sum_slots(land, f"scatter_sum_{key[0]}_{key[1]}")
        for (wname, layer, transposed, shard), off in zip(members_of[key], layout(key)[0]):
            rows = total[off:off + shard.shape[0]]
            reduced[(wname, layer)] = rows.T if transposed else rows

    last_key = order[0]
    for key in reversed(order[1:]):
        finish_scatter(key, dx)

    small_parts = [jnp.concatenate(dg_mix, axis=0), jnp.concatenate(dg_ffn, axis=0), dg_final,
                   jnp.stack(dwc_conv), jnp.stack(dwc_ffn), jnp.concatenate(db_qkv, axis=0), jnp.concatenate(db_o, axis=0),
                   jnp.concatenate(dsinks, axis=0), loss_lanes]
    flats = [_flat_pad(p) for p in small_parts]
    bounds = []
    so = 0
    for fl in flats:
        bounds.append((so, so + fl.shape[1]))
        so += fl.shape[1]
    small_rows = so // LANES
    pad_rows = (-small_rows) % SUBLANES
    part_small = jnp.pad(jnp.concatenate(flats, axis=1).reshape(small_rows, LANES), ((0, pad_rows), (0, 0)))
    loss_rows = (bounds[-1][0] // LANES, bounds[-1][1] // LANES)
    summed, loss_tile = _all_reduce_small(part_small, loss_rows, "reduce_small")
    summed = summed.reshape(1, -1)

    def small_grad(k, shape):
        lo = bounds[k][0]
        size = 1
        for s_ in shape:
            size *= s_
        return summed[0, lo:lo + size].reshape(shape)

    def my_cols(full, n_local):
        return lax.dynamic_slice_in_dim(full, me * n_local, n_local, axis=full.ndim - 1)

    g_norm_mix = small_grad(0, norm_mix.shape)
    g_norm_ffn = small_grad(1, norm_ffn.shape)
    g_norm_final = small_grad(2, norm_final.shape)
    g_conv_w_conv = my_cols(small_grad(3, (n_conv, 3, d)), conv_w_conv.shape[2])
    g_ffn_w_conv = my_cols(small_grad(4, (depth, 3, ffn_w_conv.shape[2] * N_DEV)), ffn_w_conv.shape[2])
    g_attn_b_qkv = my_cols(small_grad(5, (n_attn, attn_b_qkv.shape[1] * N_DEV)), attn_b_qkv.shape[1])
    g_attn_b_o = my_cols(small_grad(6, (n_attn, d)), attn_b_o.shape[1])
    g_attn_sinks = small_grad(7, attn_sinks.shape)
    loss = loss_tile[0, 0]

    def big_grad(wname, n_layers):
        return jnp.stack([reduced[(wname, layer)] for layer in range(n_layers)])

    grads = {
        "norm_mix": g_norm_mix, "norm_ffn": g_norm_ffn, "norm_final": g_norm_final, "conv_w_conv": g_conv_w_conv,
        "attn_w_qkv": big_grad("attn_w_qkv", n_attn), "attn_b_qkv": g_attn_b_qkv, "attn_sinks": g_attn_sinks,
        "attn_w_o": big_grad("attn_w_o", n_attn), "attn_b_o": g_attn_b_o,
        "ffn_w_in": big_grad("ffn_w_in", depth), "ffn_w_conv": g_ffn_w_conv, "ffn_w_down": big_grad("ffn_w_down", depth),
    }
    params = {
        "norm_mix": (norm_mix, m_norm_mix, v_norm_mix), "norm_ffn": (norm_ffn, m_norm_ffn, v_norm_ffn),
        "norm_final": (norm_final, m_norm_final, v_norm_final), "conv_w_in": (conv_w_in, m_conv_w_in, v_conv_w_in),
        "conv_w_conv": (conv_w_conv, m_conv_w_conv, v_conv_w_conv), "conv_w_out": (conv_w_out, m_conv_w_out, v_conv_w_out),
        "attn_w_qkv": (attn_w_qkv, m_attn_w_qkv, v_attn_w_qkv), "attn_b_qkv": (attn_b_qkv, m_attn_b_qkv, v_attn_b_qkv),
        "attn_sinks": (attn_sinks, m_attn_sinks, v_attn_sinks), "attn_w_o": (attn_w_o, m_attn_w_o, v_attn_w_o),
        "attn_b_o": (attn_b_o, m_attn_b_o, v_attn_b_o), "ffn_w_in": (ffn_w_in, m_ffn_w_in, v_ffn_w_in),
        "ffn_w_conv": (ffn_w_conv, m_ffn_w_conv, v_ffn_w_conv), "ffn_w_down": (ffn_w_down, m_ffn_w_down, v_ffn_w_down),
    }
    names = list(params)
    updates = {}

    def update(wname):
        w, m, v = params[wname]
        updates[wname] = _adamw_nd(w, grads[wname], m, v, f"adamw_{wname}")

    last_names = sorted({wname for wname, _, _, _ in members_of[last_key]})
    for wname in names:
        if wname not in last_names:
            update(wname)
    finish_scatter(last_key, updates["ffn_w_in"][0])
    for wname in last_names:
        grads[wname] = big_grad(wname, params[wname][0].shape[0])
        update(wname)
    return (loss, grad_x, *[grads[wname] for wname in names], *[updates[wname][0] for wname in names],
            *[updates[wname][1] for wname in names], *[updates[wname][2] for wname in names])
```

```python
from typing import NamedTuple

import jax
import jax.numpy as jnp
from jax import lax
from jax.experimental import pallas as pl
from jax.experimental.pallas import tpu as pltpu

F32 = jnp.float32
BF16 = jnp.bfloat16
SDS = jax.ShapeDtypeStruct
MESH = pl.DeviceIdType.MESH
ANY = pl.BlockSpec(memory_space=pl.ANY)

N_DEV = 8
EPS = 1e-5
HEAD_DIM = 64
GROUP = 4
WINDOW = 128
ROPE_THETA = 10000.0
ADAM_LR, ADAM_B1, ADAM_B2, ADAM_EPS, ADAM_WD, ADAM_STEP = 0.001, 0.9, 0.999, 1e-08, 0.01, 10

V7X_VMEM_BYTES = 64 * 1024 * 1024
VMEM_LIMIT_BYTES = V7X_VMEM_BYTES - 8 * 1024 * 1024
LANES = 128
SUBLANES = 8
TOKEN_TILE = 512
TN_TOKEN_TILE = 2048
MASKED_SCORE = -1e30


def _cparams(n_axes=1):
    return pltpu.CompilerParams(dimension_semantics=("arbitrary",) * n_axes, vmem_limit_bytes=VMEM_LIMIT_BYTES)


def _resident(shape):
    zeros = (0,) * len(shape)
    return pl.BlockSpec(shape, lambda *_: zeros, pipeline_mode=pl.Buffered(1))


class _Rows(NamedTuple):
    arr: jax.Array
    off: int
    n: int


def _rows_spec(w):
    assert w.off % w.n == 0
    block = w.off // w.n
    return pl.BlockSpec((N_DEV, w.n, w.arr.shape[2]), lambda *_: (0, block, 0), pipeline_mode=pl.Buffered(1))


def _mat(ref):
    v = ref[...]
    return v.reshape(v.shape[0] * v.shape[1], v.shape[2])


def _token_tile(seq):
    return min(TOKEN_TILE, seq // 2)


def _largest_divisor(m, cap, mult):
    best = None
    for d in range(mult, min(m, cap) + 1, mult):
        if m % d == 0:
            best = d
    return m if best is None else best


def _nt(a, b):
    return lax.dot_general(a, b, (((1,), (1,)), ((), ())), preferred_element_type=F32)


def _nn(a, b):
    return lax.dot_general(a, b, (((1,), (0,)), ((), ())), preferred_element_type=F32)


def _tn(a, b):
    return lax.dot_general(a, b, (((0,), (0,)), ((), ())), preferred_element_type=F32)


def _rms_parts(xv):
    r = lax.rsqrt(jnp.mean(xv * xv, axis=-1, keepdims=True) + EPS)
    return r, xv * r


def _rms_backward(dh, xh, r, gain, dres):
    u = dh * gain
    return dres + r * (u - xh * jnp.mean(u * xh, axis=-1, keepdims=True))


def _shifted_rows(xv, edge, k, down):
    n = xv.shape[0]
    row = lax.broadcasted_iota(jnp.int32, edge.shape, 0)
    if down:
        rolled = pltpu.roll(xv, k, 0)
        head = jnp.where(row < k, pltpu.roll(edge, k, 0), rolled[0:SUBLANES])
        return jnp.concatenate([head, rolled[SUBLANES:]], axis=0)
    rolled = pltpu.roll(xv, n - k, 0)
    tail = jnp.where(row >= SUBLANES - k, pltpu.roll(edge, SUBLANES - k, 0), rolled[n - SUBLANES:])
    return jnp.concatenate([rolled[:n - SUBLANES], tail], axis=0)


def _causal_conv3(edge_ref, xv, w_ref):
    before = edge_ref[...]
    y = (w_ref[2:3, :] * xv + w_ref[1:2, :] * _shifted_rows(xv, before, 1, True)
         + w_ref[0:1, :] * _shifted_rows(xv, before, 2, True))
    edge_ref[...] = xv[xv.shape[0] - SUBLANES:, :]
    return y


def _sigmoid(z):
    return 1.0 / (1.0 + jnp.exp(-z))


def _fwd_conv_mixer(x, gain, w_in_t, w_conv, w_out, seq, name):
    t_all, d = x.shape
    tt = _token_tile(seq)
    tps = seq // tt

    def body(x_ref, g_ref, win_ref, wc_ref, wout_ref, x1_ref, bcv_ref, cc_ref, y_ref, ext_ref):
        i = pl.program_id(0)
        xv = x_ref[...]
        r, xh = _rms_parts(xv)
        h = (xh * g_ref[...]).astype(BF16)
        bcv = _nt(h, _mat(win_ref))
        bcv_ref[...] = bcv.astype(BF16)

        @pl.when(i % tps == 0)
        def _():
            ext_ref[...] = jnp.zeros_like(ext_ref)

        cc = _causal_conv3(ext_ref, bcv[:, d:2 * d] * bcv[:, 2 * d:], wc_ref)
        cc_ref[...] = cc.astype(BF16)
        y = (bcv[:, :d] * cc).astype(BF16)
        y_ref[...] = y
        x1_ref[...] = xv + _nn(y, _mat(wout_ref))

    tile = pl.BlockSpec((tt, d), lambda i: (i, 0))
    return pl.pallas_call(
        body, name=name, grid=(t_all // tt,),
        in_specs=[tile, _resident((1, d)), _rows_spec(w_in_t), _resident((3, d)), _rows_spec(w_out)],
        out_specs=[tile, pl.BlockSpec((tt, 3 * d), lambda i: (i, 0)), tile, tile],
        out_shape=[SDS((t_all, d), F32), SDS((t_all, 3 * d), BF16), SDS((t_all, d), BF16), SDS((t_all, d), BF16)],
        scratch_shapes=[pltpu.VMEM((SUBLANES, d), F32)],
        compiler_params=_cparams(),
    )(x, gain, w_in_t.arr, w_conv, w_out.arr)


def _fwd_ffn(x, gain, w_in_t, w_conv, w_down, seq, name):
    t_all, d = x.shape
    f = w_down.n * N_DEV
    tt = _token_tile(seq) // 2
    tps = seq // tt

    def body(x_ref, g_ref, win_ref, wc_ref, wd_ref, x2_ref, gate_ref, s_ref, uds_ref, a_ref, ext_ref):
        i = pl.program_id(0)
        xv = x_ref[...]
        r, xh = _rms_parts(xv)
        h = (xh * g_ref[...]).astype(BF16)
        gu = _nt(h, _mat(win_ref))
        gate = gu[:, :f]
        u = gu[:, f:]
        gate_ref[...] = gate.astype(BF16)

        @pl.when(i % tps == 0)
        def _():
            ext_ref[...] = jnp.zeros_like(ext_ref)

        gc = _causal_conv3(ext_ref, gate, wc_ref)
        sig = _sigmoid(gc)
        s = gc * sig
        s_ref[...] = s.astype(BF16)
        uds_ref[...] = (u * (sig * (1.0 + gc * (1.0 - sig)))).astype(BF16)
        a = (s * u).astype(BF16)
        a_ref[...] = a
        x2_ref[...] = xv + _nn(a, _mat(wd_ref))

    wide = pl.BlockSpec((tt, f), lambda i: (i, 0))
    return pl.pallas_call(
        body, name=name, grid=(t_all // tt,),
        in_specs=[pl.BlockSpec((tt, d), lambda i: (i, 0)), _resident((1, d)), _rows_spec(w_in_t),
                  _resident((3, f)), _rows_spec(w_down)],
        out_specs=[pl.BlockSpec((tt, d), lambda i: (i, 0)), wide, wide, wide, wide],
        out_shape=[SDS((t_all, d), F32)] + [SDS((t_all, f), BF16)] * 4,
        scratch_shapes=[pltpu.VMEM((SUBLANES, f), F32)],
        compiler_params=_cparams(),
    )(x, gain, w_in_t.arr, w_conv, w_down.arr)


def _rope_partner(xs, lane_lo):
    return jnp.where(lane_lo, pltpu.roll(xs, LANES - HEAD_DIM // 2, 1), pltpu.roll(xs, HEAD_DIM // 2, 1))


def _fwd_qkv(x, gain, w_qkv_t, b_qkv, cos_t, sin_t, seq, name):
    t_all, d = x.shape
    width = w_qkv_t.n * N_DEV
    kvw = (width - d) // 2
    tt = _token_tile(seq)
    tps = seq // tt
    scale = HEAD_DIM ** -0.5

    def body(x_ref, g_ref, w_ref, b_ref, cos_ref, sin_ref, qkv_ref):
        xv = x_ref[...]
        r, xh = _rms_parts(xv)
        h = (xh * g_ref[...]).astype(BF16)
        qkv = _nt(h, _mat(w_ref)) + b_ref[...]
        cosv = cos_ref[...]
        sinv = sin_ref[...]
        lane_lo = (lax.broadcasted_iota(jnp.int32, (tt, LANES), 1) % HEAD_DIM) < HEAD_DIM // 2
        for s in range((d + kvw) // LANES):
            xs = qkv[:, s * LANES:(s + 1) * LANES]
            roped = xs * cosv + _rope_partner(xs, lane_lo) * sinv
            if s * LANES < d:
                roped = roped * scale
            qkv_ref[:, s * LANES:(s + 1) * LANES] = roped.astype(BF16)
        qkv_ref[:, d + kvw:] = qkv[:, d + kvw:].astype(BF16)

    return pl.pallas_call(
        body, name=name, grid=(t_all // tt,),
        in_specs=[pl.BlockSpec((tt, d), lambda i: (i, 0)), _resident((1, d)), _rows_spec(w_qkv_t),
                  _resident((1, width)), pl.BlockSpec((tt, LANES), lambda i: (i % tps, 0)),
                  pl.BlockSpec((tt, LANES), lambda i: (i % tps, 0))],
        out_specs=pl.BlockSpec((tt, width), lambda i: (i, 0)),
        out_shape=SDS((t_all, width), BF16),
        compiler_params=_cparams(),
    )(x, gain, w_qkv_t.arr, b_qkv, cos_t, sin_t)


def _stack_heads(ref, row0, kh):
    return jnp.concatenate(
        [ref[row0:row0 + WINDOW, (kh * GROUP + g) * HEAD_DIM:(kh * GROUP + g + 1) * HEAD_DIM] for g in range(GROUP)],
        axis=0)


def _band_bias():
    r = lax.broadcasted_iota(jnp.int32, (WINDOW, 2 * WINDOW), 0)
    j = lax.broadcasted_iota(jnp.int32, (WINDOW, 2 * WINDOW), 1)
    base = (j > r) & (j <= r + WINDOW)
    return jnp.where(base, 0.0, MASKED_SCORE), jnp.where(base & (j >= WINDOW), 0.0, MASKED_SCORE)


def _fwd_attention(qkv, x, sinks, w_o, b_o, seq, name):
    t_all, d = x.shape
    width = qkv.shape[1]
    kvw = (width - d) // 2
    n_kv = kvw // HEAD_DIM
    tt = _token_tile(seq)
    tps = seq // tt
    nblk = tt // WINDOW

    def body(sink_ref, qkv_ref, kvp_ref, x_ref, wo_ref, bo_ref, x1_ref, o_ref, p_ref, psink_ref,
             kvext_ref, oscr_ref, bias_ref, s_ref):
        i = pl.program_id(0)

        @pl.when(i == 0)
        def _():
            base, first = _band_bias()
            bias_ref[0], bias_ref[1] = base.T, first.T

        kvext_ref[0:WINDOW, :] = kvp_ref[...]
        kvext_ref[WINDOW:, :] = qkv_ref[:, d:]
        at_seq_start = (i % tps == 0).astype(jnp.int32)
        steps = [(n, kh) for n in range(nblk) for kh in range(n_kv)]

        def scores(step):
            n, kh = steps[step]
            qs = _stack_heads(qkv_ref, n * WINDOW, kh)
            kb = kvext_ref[n * WINDOW:(n + 2) * WINDOW, kh * HEAD_DIM:(kh + 1) * HEAD_DIM]
            s_ref[step % 2] = _nt(kb, qs)

        scores(0)
        for step, (n, kh) in enumerate(steps):
            buf = step % 2
            if step + 1 < len(steps):
                scores(step + 1)
            vb = kvext_ref[n * WINDOW:(n + 2) * WINDOW, kvw + kh * HEAD_DIM:kvw + (kh + 1) * HEAD_DIM]
            bias = bias_ref[at_seq_start if n == 0 else 0]
            for g in range(GROUP):
                cols = slice(g * WINDOW, (g + 1) * WINDOW)
                sink = sink_ref[kh * GROUP + g]
                sv = s_ref[buf, :, cols] + bias
                m = jnp.maximum(jnp.max(sv, axis=0, keepdims=True), sink)
                p = jnp.exp(sv - m)
                e_sink = jnp.exp(sink - m)
                inv = 1.0 / (jnp.sum(p, axis=0, keepdims=True) + e_sink)
                p_ref[n, kh, :, cols] = (p * inv).astype(BF16)
                psink_ref[n, kh, g:g + 1, :] = e_sink * inv
            o_s = _tn(vb, p_ref[n, kh]).T
            for g in range(GROUP):
                hd = kh * GROUP + g
                oscr_ref[n * WINDOW:(n + 1) * WINDOW, hd * HEAD_DIM:(hd + 1) * HEAD_DIM] = (
                    o_s[g * WINDOW:(g + 1) * WINDOW].astype(BF16))
        o = oscr_ref[...]
        o_ref[...] = o
        x1_ref[...] = x_ref[...] + _nn(o, _mat(wo_ref)) + bo_ref[...]

    kv_blocks = tt // WINDOW
    return pl.pallas_call(
        body, name=name, grid=(t_all // tt,),
        in_specs=[pl.BlockSpec(memory_space=pltpu.SMEM),
                  pl.BlockSpec((tt, width), lambda i: (i, 0)),
                  pl.BlockSpec((WINDOW, 2 * kvw), lambda i: (jnp.maximum(i * kv_blocks - 1, 0), d // (2 * kvw))),
                  pl.BlockSpec((tt, d), lambda i: (i, 0)), _rows_spec(w_o), _resident((1, d))],
        out_specs=[pl.BlockSpec((tt, d), lambda i: (i, 0)), pl.BlockSpec((tt, d), lambda i: (i, 0)),
                   pl.BlockSpec((nblk, n_kv, 2 * WINDOW, GROUP * WINDOW), lambda i: (i, 0, 0, 0)),
                   pl.BlockSpec((nblk, n_kv, GROUP, WINDOW), lambda i: (i, 0, 0, 0))],
        out_shape=[SDS((t_all, d), F32), SDS((t_all, d), BF16),
                   SDS((t_all // WINDOW, n_kv, 2 * WINDOW, GROUP * WINDOW), BF16),
                   SDS((t_all // WINDOW, n_kv, GROUP, WINDOW), F32)],
        scratch_shapes=[pltpu.VMEM((tt + WINDOW, 2 * kvw), BF16), pltpu.VMEM((tt, d), BF16),
                        pltpu.VMEM((2, 2 * WINDOW, WINDOW), F32), pltpu.VMEM((2, 2 * WINDOW, GROUP * WINDOW), F32)],
        compiler_params=_cparams(),
    )(sinks, qkv, qkv, x, w_o.arr, b_o)


def _final_norm_loss(x, gain, target, name):
    t_all, d = x.shape
    tt = min(TOKEN_TILE, t_all)

    def body(x_ref, g_ref, t_ref, dx_ref, dg_ref, loss_ref):
        i = pl.program_id(0)
        xv = x_ref[...]
        r, xh = _rms_parts(xv)
        gain_v = g_ref[...]
        e = xh * gain_v - t_ref[...]
        dy = e * (1.0 / d)
        dx_ref[...] = _rms_backward(dy, xh, r, gain_v, 0.0)

        @pl.when(i == 0)
        def _():
            dg_ref[...] = jnp.zeros_like(dg_ref)
            loss_ref[...] = jnp.zeros_like(loss_ref)

        dg_ref[...] += jnp.sum(dy * xh, axis=0, keepdims=True)
        loss_ref[...] += (0.5 / d) * jnp.sum(e * e, axis=0, keepdims=True)

    return pl.pallas_call(
        body, name=name, grid=(t_all // tt,),
        in_specs=[pl.BlockSpec((tt, d), lambda i: (i, 0)), _resident((1, d)), pl.BlockSpec((tt, d), lambda i: (i, 0))],
        out_specs=[pl.BlockSpec((tt, d), lambda i: (i, 0)), pl.BlockSpec((1, d), lambda i: (0, 0)),
                   pl.BlockSpec((1, d), lambda i: (0, 0))],
        out_shape=[SDS((t_all, d), F32), SDS((1, d), F32), SDS((1, d), F32)],
        compiler_params=_cparams(),
    )(x, gain, target)


def _bwd_ffn_inner(dx2, gate, s_act, uds, w_conv, w_down, seq, name):
    t_all, d = dx2.shape
    f = w_down.n * N_DEV
    tt = _token_tile(seq)
    tps = seq // tt
    nt = t_all // tt

    def body(dx_ref, g_ref, s_ref, uds_ref, wc_ref, wd_ref, dgu_ref, dwc_ref, aext_ref, da_ref):
        i = pl.program_id(0)
        ti = nt - 1 - i
        da_ref[...] = _nt(dx_ref[...].astype(BF16), _mat(wd_ref))

        @pl.when(ti % tps == tps - 1)
        def _():
            aext_ref[...] = jnp.zeros_like(aext_ref)

        @pl.when(i == 0)
        def _():
            dwc_ref[...] = jnp.zeros_like(dwc_ref)

        for c in range(f // LANES):
            cols = slice(c * LANES, (c + 1) * LANES)
            da = da_ref[:, cols]
            g = g_ref[:, cols].astype(F32)
            dgc = da * uds_ref[:, cols].astype(F32)
            after = aext_ref[:, cols]
            sh1 = _shifted_rows(dgc, after, 1, False)
            sh2 = _shifted_rows(dgc, after, 2, False)
            aext_ref[:, cols] = dgc[0:SUBLANES, :]
            dg = wc_ref[2:3, cols] * dgc + wc_ref[1:2, cols] * sh1 + wc_ref[0:1, cols] * sh2
            dgu_ref[:, cols] = dg.astype(BF16)
            dgu_ref[:, f + c * LANES:f + (c + 1) * LANES] = (da * s_ref[:, cols].astype(F32)).astype(BF16)
            dwc_ref[0:1, cols] += jnp.sum(g * sh2, axis=0, keepdims=True)
            dwc_ref[1:2, cols] += jnp.sum(g * sh1, axis=0, keepdims=True)
            dwc_ref[2:3, cols] += jnp.sum(g * dgc, axis=0, keepdims=True)

    rev = lambda i: (nt - 1 - i, 0)
    return pl.pallas_call(
        body, name=name, grid=(nt,),
        in_specs=[pl.BlockSpec((tt, d), rev)] + [pl.BlockSpec((tt, f), rev)] * 3 + [_resident((3, f)), _rows_spec(w_down)],
        out_specs=[pl.BlockSpec((tt, 2 * f), rev), pl.BlockSpec((8, f), lambda i: (0, 0))],
        out_shape=[SDS((t_all, 2 * f), BF16), SDS((8, f), F32)],
        scratch_shapes=[pltpu.VMEM((SUBLANES, f), F32), pltpu.VMEM((tt, f), F32)],
        compiler_params=_cparams(),
    )(dx2, gate, s_act, uds, w_conv, w_down.arr)


def _bwd_conv_inner(dx1, bcv, cc, w_conv, w_out, seq, name):
    t_all, d = dx1.shape
    tt = _token_tile(seq)
    tps = seq // tt
    nt = t_all // tt

    def body(dx_ref, bcv_ref, cc_ref, wc_ref, wout_ref, dbcv_ref, dwc_ref, aext_ref, dy_ref):
        i = pl.program_id(0)
        ti = nt - 1 - i
        dy_ref[...] = _nt(dx_ref[...].astype(BF16), _mat(wout_ref))

        @pl.when(ti % tps == tps - 1)
        def _():
            aext_ref[...] = jnp.zeros_like(aext_ref)

        @pl.when(i == 0)
        def _():
            dwc_ref[...] = jnp.zeros_like(dwc_ref)

        for s in range(d // LANES):
            cols = slice(s * LANES, (s + 1) * LANES)
            ccols = slice(d + s * LANES, d + (s + 1) * LANES)
            vcols = slice(2 * d + s * LANES, 2 * d + (s + 1) * LANES)
            dy = dy_ref[:, cols]
            c = bcv_ref[:, ccols].astype(F32)
            v = bcv_ref[:, vcols].astype(F32)
            cv = c * v
            dcc = dy * bcv_ref[:, cols].astype(F32)
            after = aext_ref[:, cols]
            sh1 = _shifted_rows(dcc, after, 1, False)
            sh2 = _shifted_rows(dcc, after, 2, False)
            aext_ref[:, cols] = dcc[0:SUBLANES, :]
            dcv = wc_ref[2:3, cols] * dcc + wc_ref[1:2, cols] * sh1 + wc_ref[0:1, cols] * sh2
            dbcv_ref[:, cols] = (dy * cc_ref[:, cols].astype(F32)).astype(BF16)
            dbcv_ref[:, ccols] = (dcv * v).astype(BF16)
            dbcv_ref[:, vcols] = (dcv * c).astype(BF16)
            dwc_ref[0:1, cols] += jnp.sum(cv * sh2, axis=0, keepdims=True)
            dwc_ref[1:2, cols] += jnp.sum(cv * sh1, axis=0, keepdims=True)
            dwc_ref[2:3, cols] += jnp.sum(cv * dcc, axis=0, keepdims=True)

    return pl.pallas_call(
        body, name=name, grid=(nt,),
        in_specs=[pl.BlockSpec((tt, d), lambda i: (nt - 1 - i, 0)),
                  pl.BlockSpec((tt, 3 * d), lambda i: (nt - 1 - i, 0)),
                  pl.BlockSpec((tt, d), lambda i: (nt - 1 - i, 0)),
                  _resident((3, d)), _rows_spec(w_out)],
        out_specs=[pl.BlockSpec((tt, 3 * d), lambda i: (nt - 1 - i, 0)), pl.BlockSpec((8, d), lambda i: (0, 0))],
        out_shape=[SDS((t_all, 3 * d), BF16), SDS((8, d), F32)],
        scratch_shapes=[pltpu.VMEM((SUBLANES, d), F32), pltpu.VMEM((tt, d), F32)],
        compiler_params=_cparams(),
    )(dx1, bcv, cc, w_conv, w_out.arr)


def _bwd_attention_inner(dx1, qkv, probs_t, p_sink, w_o, cos_t, sin_t, seq, name):
    t_all, d = dx1.shape
    width = qkv.shape[1]
    kvw = (width - d) // 2
    n_kv = kvw // HEAD_DIM
    tt = _token_tile(seq)
    tps = seq // tt
    nt = t_all // tt
    nblk = tt // WINDOW
    scale = HEAD_DIM ** -0.5

    def body(dx_ref, qkv_ref, kvp_ref, p_ref, psink_ref, cos_ref, sin_ref, wo_ref,
             dqkv_ref, dsink_ref, dbqkv_ref, dbo_ref,
             kvext_ref, dkvext_ref, carry_ref, dq_ref, do_ref, dp_ref, ds_ref):
        i = pl.program_id(0)
        dxv = dx_ref[...]
        do_ref[...] = _nt(dxv.astype(BF16), _mat(wo_ref)).astype(BF16)
        kvext_ref[0:WINDOW, :] = kvp_ref[...]
        kvext_ref[WINDOW:, :] = qkv_ref[:, d:]
        dkvext_ref[...] = jnp.zeros_like(dkvext_ref)

        @pl.when(i == 0)
        def _():
            carry_ref[...] = jnp.zeros_like(carry_ref)
            dsink_ref[...] = jnp.zeros_like(dsink_ref)
            dbqkv_ref[...] = jnp.zeros_like(dbqkv_ref)
            dbo_ref[...] = jnp.zeros_like(dbo_ref)

        head_lane = lax.broadcasted_iota(jnp.int32, (1, LANES), 1)
        dsink = jnp.zeros((1, LANES), F32)
        for n in range(nblk):
            for kh in range(n_kv):
                buf = (n * n_kv + kh) % 2
                qs = _stack_heads(qkv_ref, n * WINDOW, kh)
                dos = _stack_heads(do_ref, n * WINDOW, kh)
                kcols = slice(kh * HEAD_DIM, (kh + 1) * HEAD_DIM)
                vcols = slice(kvw + kh * HEAD_DIM, kvw + (kh + 1) * HEAD_DIM)
                band = slice(n * WINDOW, (n + 2) * WINDOW)
                kb = kvext_ref[band, kcols]
                vb = kvext_ref[band, vcols]
                dp_ref[buf] = _nt(vb, dos)
                for g in range(GROUP):
                    hd = kh * GROUP + g
                    cols = slice(g * WINDOW, (g + 1) * WINDOW)
                    probs = p_ref[n, kh, :, cols].astype(F32)
                    dp = dp_ref[buf, :, cols]
                    dsum = jnp.sum(probs * dp, axis=0, keepdims=True)
                    ds_ref[buf, :, cols] = (probs * (dp - dsum)).astype(BF16)
                    dsink = dsink - jnp.where(head_lane == hd, jnp.sum(psink_ref[n, kh, g:g + 1, :] * dsum), 0.0)
                ds_t = ds_ref[buf]
                dkvext_ref[band, vcols] += _nn(p_ref[n, kh], dos)
                dkvext_ref[band, kcols] += _nn(ds_t, qs)
                dq_s = _tn(kb, ds_t).T
                for g in range(GROUP):
                    hd = kh * GROUP + g
                    dq_ref[n * WINDOW:(n + 1) * WINDOW, hd * HEAD_DIM:(hd + 1) * HEAD_DIM] = dq_s[g * WINDOW:(g + 1) * WINDOW]
        dsink_ref[0:1, :] += dsink
        dkvext_ref[tt:tt + WINDOW, :] += carry_ref[...]
        carry_ref[...] = dkvext_ref[0:WINDOW, :]

        cosv = cos_ref[...]
        sinv = sin_ref[...]
        lane_lo = (lax.broadcasted_iota(jnp.int32, (tt, LANES), 1) % HEAD_DIM) < HEAD_DIM // 2
        for s in range((d + kvw) // LANES):
            if s * LANES < d:
                dy = dq_ref[:, s * LANES:(s + 1) * LANES] * scale
            else:
                dy = dkvext_ref[WINDOW:, s * LANES - d:(s + 1) * LANES - d]
            dpre = dy * cosv - _rope_partner(dy, lane_lo) * sinv
            dqkv_ref[:, s * LANES:(s + 1) * LANES] = dpre.astype(BF16)
            dbqkv_ref[0:1, s * LANES:(s + 1) * LANES] += jnp.sum(dpre, axis=0, keepdims=True)
        dv = dkvext_ref[WINDOW:, kvw:]
        dqkv_ref[:, d + kvw:] = dv.astype(BF16)
        dbqkv_ref[0:1, d + kvw:] += jnp.sum(dv, axis=0, keepdims=True)
        dbo_ref[...] += jnp.sum(dxv, axis=0, keepdims=True)

    kv_blocks = tt // WINDOW
    return pl.pallas_call(
        body, name=name, grid=(nt,),
        in_specs=[pl.BlockSpec((tt, d), lambda i: (nt - 1 - i, 0)),
                  pl.BlockSpec((tt, width), lambda i: (nt - 1 - i, 0)),
                  pl.BlockSpec((WINDOW, 2 * kvw), lambda i: (jnp.maximum((nt - 1 - i) * kv_blocks - 1, 0), d // (2 * kvw))),
                  pl.BlockSpec((nblk, n_kv, 2 * WINDOW, GROUP * WINDOW), lambda i: (nt - 1 - i, 0, 0, 0)),
                  pl.BlockSpec((nblk, n_kv, GROUP, WINDOW), lambda i: (nt - 1 - i, 0, 0, 0)),
                  pl.BlockSpec((tt, LANES), lambda i: ((nt - 1 - i) % tps, 0)),
                  pl.BlockSpec((tt, LANES), lambda i: ((nt - 1 - i) % tps, 0)),
                  _rows_spec(w_o)],
        out_specs=[pl.BlockSpec((tt, width), lambda i: (nt - 1 - i, 0)), pl.BlockSpec((8, LANES), lambda i: (0, 0)),
                   pl.BlockSpec((1, width), lambda i: (0, 0)), pl.BlockSpec((1, d), lambda i: (0, 0))],
        out_shape=[SDS((t_all, width), BF16), SDS((8, LANES), F32), SDS((1, width), F32), SDS((1, d), F32)],
        scratch_shapes=[pltpu.VMEM((tt + WINDOW, 2 * kvw), BF16), pltpu.VMEM((tt + WINDOW, 2 * kvw), F32),
                        pltpu.VMEM((WINDOW, 2 * kvw), F32), pltpu.VMEM((tt, d), F32), pltpu.VMEM((tt, d), BF16),
                        pltpu.VMEM((2, 2 * WINDOW, GROUP * WINDOW), F32), pltpu.VMEM((2, 2 * WINDOW, GROUP * WINDOW), BF16)],
        compiler_params=_cparams(),
    )(dx1, qkv, qkv, probs_t, p_sink, cos_t, sin_t, w_o.arr)


def _bwd_dense_norm(dy, w_t, x, gain, dres, name):
    t_all, d = x.shape
    n = dy.shape[1]
    tt = min(TOKEN_TILE, t_all)

    def body(dy_ref, w_ref, x_ref, g_ref, dres_ref, dx_ref, h_ref, dg_ref):
        i = pl.program_id(0)
        dh = _nn(dy_ref[...], _mat(w_ref))
        r, xh = _rms_parts(x_ref[...])
        gain_v = g_ref[...]
        h_ref[...] = (xh * gain_v).astype(BF16)
        dx_ref[...] = _rms_backward(dh, xh, r, gain_v, dres_ref[...])

        @pl.when(i == 0)
        def _():
            dg_ref[...] = jnp.zeros_like(dg_ref)

        dg_ref[...] += jnp.sum(dh * xh, axis=0, keepdims=True)

    return pl.pallas_call(
        body, name=name, grid=(t_all // tt,),
        in_specs=[pl.BlockSpec((tt, n), lambda i: (i, 0)), _rows_spec(w_t), pl.BlockSpec((tt, d), lambda i: (i, 0)),
                  _resident((1, d)), pl.BlockSpec((tt, d), lambda i: (i, 0))],
        out_specs=[pl.BlockSpec((tt, d), lambda i: (i, 0)), pl.BlockSpec((tt, d), lambda i: (i, 0)),
                   pl.BlockSpec((1, d), lambda i: (0, 0))],
        out_shape=[SDS((t_all, d), F32), SDS((t_all, d), BF16), SDS((1, d), F32)],
        compiler_params=_cparams(),
    )(dy, w_t.arr, x, gain, dres)


def _tn_matmul(a, b, dest, name):
    t_all, m = a.shape
    d = b.shape[1]
    n = dest.n
    assert m == N_DEV * n and dest.off % n == 0
    k = max(kk for kk in (1, 2, 4, 8) if kk * n <= max(n, 1536))
    tm = k * n
    tt = min(TN_TOKEN_TILE, t_all)
    n_t = t_all // tt
    fresh = not hasattr(dest.arr, "dtype")

    def body(a_ref, b_ref, *rest):
        o_ref, acc_ref = rest[-2:]
        t = pl.program_id(1)

        @pl.when(t == 0)
        def _():
            acc_ref[...] = jnp.zeros_like(acc_ref)

        acc_ref[...] += _tn(a_ref[...], b_ref[...].astype(BF16))

        @pl.when(t == n_t - 1)
        def _():
            o_ref[...] = acc_ref[...].astype(BF16).reshape(k, n, d)

    block = dest.off // n
    return pl.pallas_call(
        body, name=name, grid=(m // tm, n_t),
        in_specs=[pl.BlockSpec((tt, tm), lambda j, t: (t, j)), pl.BlockSpec((tt, d), lambda j, t: (t, 0))] + ([] if fresh else [ANY]),
        out_specs=pl.BlockSpec((k, n, d), lambda j, t: (j, block, 0)),
        out_shape=SDS(tuple(dest.arr) if fresh else dest.arr.shape, BF16),
        scratch_shapes=[pltpu.VMEM((tm, d), F32)],
        input_output_aliases={} if fresh else {2: 0},
        compiler_params=_cparams(2),
    )(*((a, b) if fresh else (a, b, dest.arr)))


def _my_place():
    return lax.axis_index("x"), lax.axis_index("y"), lax.axis_index("c")


def _other_chips(x, y):
    return [(1 - x, y), (x, 1 - y), (1 - x, 1 - y)]


def _all_gather(blocks, name):
    n_arr = len(blocks)

    def body(*refs):
        in_refs = refs[:n_arr]
        out_refs = refs[n_arr:2 * n_arr]
        send_sems, recv_sems, local_sems = refs[2 * n_arr:]
        x, y, c = _my_place()
        me, sibling = (x, y, c), (x, y, 1 - c)
        chips = _other_chips(x, y)

        def slot(a, place):
            px, py, pc = place
            return out_refs[a].at[4 * px + 2 * py + pc]

        def copy(a, k, block, to, src=None):
            return pltpu.make_async_remote_copy(
                src_ref=slot(a, block) if src is None else src, dst_ref=slot(a, block),
                send_sem=send_sems.at[a, k], recv_sem=recv_sems.at[a, k], device_id=to, device_id_type=MESH)

        started = []
        local = []
        for a in range(n_arr):
            mine = pltpu.make_async_copy(in_refs[a], slot(a, me), local_sems.at[a])
            mine.start()
            local.append(mine)
            first = [copy(a, 0, me, sibling, src=in_refs[a])]
            first += [copy(a, 1 + j, me, (*chip, c), src=in_refs[a]) for j, chip in enumerate(chips)]
            for cp in first:
                cp.start()
            started += first
        for a in range(n_arr):
            for j, chip in enumerate(chips):
                copy(a, 1 + j, (*chip, c), me).wait_recv()
                passed = copy(a, 4 + j, (*chip, c), sibling)
                passed.start()
                started.append(passed)
        for a in range(n_arr):
            copy(a, 0, sibling, me).wait_recv()
            for j, chip in enumerate(chips):
                copy(a, 4 + j, (*chip, 1 - c), me).wait_recv()
        for cp in started:
            cp.wait_send()
        for mine in local:
            mine.wait()

    return pl.pallas_call(
        body, name=name,
        in_specs=[ANY] * n_arr, out_specs=[ANY] * n_arr,
        out_shape=[SDS((N_DEV,) + b.shape, b.dtype) for b in blocks],
        scratch_shapes=[pltpu.SemaphoreType.DMA((n_arr, 7)), pltpu.SemaphoreType.DMA((n_arr, 7)),
                        pltpu.SemaphoreType.DMA((n_arr,))],
    )(*blocks)


def _peer_of(k, x, y, c):
    return x ^ ((k >> 2) & 1), y ^ ((k >> 1) & 1), c ^ (k & 1)


HBM = pl.BlockSpec(memory_space=pltpu.HBM)
SEM = pl.BlockSpec(memory_space=pltpu.SEMAPHORE)
DATAFLOW_EFFECT = pltpu.SideEffectType.DATAFLOW_SIDE_EFFECTING


def _peer_copies(src_ref, land_ref, send_sems, recv_sems, per_peer):
    x, y, c = _my_place()
    me = 4 * x + 2 * y + c
    copies = []
    for k in range(1, N_DEV):
        px, py, pc = _peer_of(k, x, y, c)
        peer = 4 * px + 2 * py + pc
        copies.append(pltpu.make_async_remote_copy(
            src_ref=src_ref.at[peer] if per_peer else src_ref, dst_ref=land_ref.at[me],
            send_sem=send_sems.at[k - 1], recv_sem=recv_sems.at[k - 1], device_id=(px, py, pc), device_id_type=MESH))
    own = pltpu.make_async_copy(src_ref.at[me] if per_peer else src_ref, land_ref.at[me], send_sems.at[N_DEV - 1])
    return copies, own


def _exchange_start(src, after, per_peer, name):
    rows, d = src.shape[-2:]

    def body(src_ref, land_ref, after_ref, send_sems, recv_sems, src_thru, land_thru, token):
        copies, own = _peer_copies(src_ref, land_ref, send_sems, recv_sems, per_peer)
        for cp in copies:
            cp.start()
        own.start()
        token[...] = jnp.zeros_like(token)

    return pl.pallas_call(
        body, name=name,
        out_shape=(pltpu.SemaphoreType.DMA((N_DEV,)), pltpu.SemaphoreType.DMA((N_DEV - 1,)), pltpu.HBM(src.shape, src.dtype),
                   pltpu.HBM((N_DEV, rows, d), src.dtype), SDS((SUBLANES, LANES), F32)),
        in_specs=(HBM, HBM, ANY), out_specs=(SEM, SEM, HBM, HBM, pl.BlockSpec(memory_space=pltpu.VMEM)),
        input_output_aliases={0: 2, 1: 3},
        compiler_params=pltpu.CompilerParams(has_side_effects=DATAFLOW_EFFECT),
    )(pltpu.with_memory_space_constraint(src, pltpu.HBM),
      pltpu.with_memory_space_constraint(lax.empty((N_DEV, rows, d), src.dtype), pltpu.HBM), after)


def _exchange_wait(started, after, per_peer, name):
    send_sems, recv_sems, src_thru, land_thru, _ = started

    def body(src_ref, land_ref, send_sems, recv_sems, after_ref, src_out, land_out):
        copies, own = _peer_copies(src_ref, land_ref, send_sems, recv_sems, per_peer)
        for cp in copies:
            cp.wait_send()
            cp.wait_recv()
        own.wait()

    return pl.pallas_call(
        body, name=name,
        out_shape=(pltpu.HBM(src_thru.shape, src_thru.dtype), pltpu.HBM(land_thru.shape, land_thru.dtype)),
        in_specs=(HBM, HBM, SEM, SEM, ANY), out_specs=(HBM, HBM), input_output_aliases={0: 0, 1: 1},
        compiler_params=pltpu.CompilerParams(has_side_effects=DATAFLOW_EFFECT),
    )(src_thru, land_thru, send_sems, recv_sems, after)


def _sum_slots(slots, name):
    _, rows, d = slots.shape
    tr = _largest_divisor(rows, 512, 16)

    def body(s_ref, o_ref):
        acc = s_ref[0].astype(F32)
        for dev in range(1, N_DEV):
            acc = acc + s_ref[dev].astype(F32)
        o_ref[...] = acc

    return pl.pallas_call(
        body, name=name, grid=(rows // tr,),
        in_specs=[pl.BlockSpec((N_DEV, tr, d), lambda r: (0, r, 0))], out_specs=pl.BlockSpec((tr, d), lambda r: (r, 0)),
        out_shape=SDS((rows, d), F32), compiler_params=_cparams(),
    )(slots)


def _all_reduce_small(part, loss_rows, name):
    rows, lanes = part.shape
    lo, hi = loss_rows

    def body(x_ref, out_ref, loss_ref, gath_ref, send_sems, recv_sems):
        x, y, c = _my_place()
        me = 4 * x + 2 * y + c
        gath_ref[me] = x_ref[...]
        copies = []
        for k in range(1, N_DEV):
            peer = (x ^ ((k >> 2) & 1), y ^ ((k >> 1) & 1), c ^ (k & 1))
            cp = pltpu.make_async_remote_copy(
                src_ref=x_ref, dst_ref=gath_ref.at[me], send_sem=send_sems.at[k - 1], recv_sem=recv_sems.at[k - 1],
                device_id=peer, device_id_type=MESH)
            cp.start()
            copies.append(cp)
        for cp in copies:
            cp.wait_recv()
        for cp in copies:
            cp.wait_send()
        acc = gath_ref[0]
        for dev in range(1, N_DEV):
            acc = acc + gath_ref[dev]
        out_ref[...] = acc
        loss_ref[...] = jnp.full(loss_ref.shape, jnp.sum(acc[lo:hi, :]), F32)

    vmem = pl.BlockSpec(memory_space=pltpu.VMEM)
    return pl.pallas_call(
        body, name=name, in_specs=[vmem], out_specs=[vmem, vmem],
        out_shape=[SDS((rows, lanes), F32), SDS((SUBLANES, LANES), F32)],
        scratch_shapes=[pltpu.VMEM((N_DEV, rows, lanes), F32), pltpu.SemaphoreType.DMA((N_DEV - 1,)),
                        pltpu.SemaphoreType.DMA((N_DEV - 1,))],
    )(part)


def _adamw(w, g, m, v, name):
    rows, cols = w.shape
    tr = rows if rows % SUBLANES else _largest_divisor(rows, 512, SUBLANES)

    def body(w_ref, g_ref, m_ref, v_ref, d_ref, nm_ref, nv_ref):
        gv = g_ref[...]
        nm = ADAM_B1 * m_ref[...] + (1.0 - ADAM_B1) * gv
        nv = ADAM_B2 * v_ref[...] + (1.0 - ADAM_B2) * (gv * gv)
        m_hat = nm / (1.0 - ADAM_B1 ** ADAM_STEP)
        v_hat = nv / (1.0 - ADAM_B2 ** ADAM_STEP)
        d_ref[...] = -ADAM_LR * (m_hat / (jnp.sqrt(v_hat) + ADAM_EPS) + ADAM_WD * w_ref[...])
        nm_ref[...] = nm
        nv_ref[...] = nv

    spec = pl.BlockSpec((tr, cols), lambda i: (i, 0))
    return pl.pallas_call(
        body, name=name, grid=(rows // tr,), in_specs=[spec] * 4, out_specs=[spec] * 3,
        out_shape=[SDS((rows, cols), F32)] * 3, compiler_params=_cparams(),
    )(w, g, m, v)


def _adamw_nd(w, g, m, v, name):
    shape = w.shape
    two_d = (1, shape[0]) if len(shape) == 1 else (-1, shape[-1])
    outs = _adamw(w.reshape(two_d), g.reshape(two_d), m.reshape(two_d), v.reshape(two_d), name)
    return [o.reshape(shape) for o in outs]


def _rope_tables(seq):
    pos = jnp.arange(seq, dtype=F32)
    inv_freq = 1.0 / (ROPE_THETA ** (jnp.arange(0, HEAD_DIM, 2, dtype=F32) / HEAD_DIM))
    ang = pos[:, None] * inv_freq[None, :]
    cos, sin = jnp.cos(ang), jnp.sin(ang)
    reps = LANES // HEAD_DIM
    cos_t = jnp.tile(jnp.concatenate([cos, cos], axis=1), (1, reps))
    sin_t = jnp.tile(jnp.concatenate([-sin, sin], axis=1), (1, reps))
    return cos_t, sin_t


def _flat_pad(a):
    flat = a.reshape(1, -1)
    pad = (-flat.shape[1]) % LANES
    return jnp.pad(flat, ((0, 0), (0, pad))) if pad else flat


def kernel(x, norm_mix, norm_ffn, norm_final, conv_w_in, conv_w_conv, conv_w_out, attn_w_qkv, attn_b_qkv, attn_sinks, attn_w_o, attn_b_o, ffn_w_in, ffn_w_conv, ffn_w_down, loss_target, m_norm_mix, m_norm_ffn, m_norm_final, m_conv_w_in, m_conv_w_conv, m_conv_w_out, m_attn_w_qkv, m_attn_b_qkv, m_attn_sinks, m_attn_w_o, m_attn_b_o, m_ffn_w_in, m_ffn_w_conv, m_ffn_w_down, v_norm_mix, v_norm_ffn, v_norm_final, v_conv_w_in, v_conv_w_conv, v_conv_w_out, v_attn_w_qkv, v_attn_b_qkv, v_attn_sinks, v_attn_w_o, v_attn_b_o, v_ffn_w_in, v_ffn_w_conv, v_ffn_w_down):
    b_loc, seq, d = x.shape
    depth = norm_mix.shape[0]
    n_conv, n_attn = conv_w_in.shape[0], attn_w_qkv.shape[0]
    t_all = b_loc * seq
    my_x, my_y, my_c = _my_place()

    me = 4 * my_x + 2 * my_y + my_c

    groups = []
    for i in range(depth):
        j = i // 2
        if i % 2 == 0:
            mix = [("conv_w_in", j, True, conv_w_in[j].T), ("conv_w_out", j, False, conv_w_out[j])]
        else:
            mix = [("attn_w_qkv", j, True, attn_w_qkv[j].T), ("attn_w_o", j, False, attn_w_o[j])]
        groups.append((("mix", i), mix))
        groups.append((("ffn", i), [("ffn_w_in", i, True, ffn_w_in[i].T), ("ffn_w_down", i, False, ffn_w_down[i])]))
    order = [key for key, _ in groups]
    members_of = dict(groups)

    def layout(key):
        offs, o = [], 0
        for _, _, _, shard in members_of[key]:
            n = shard.shape[0]
            o = -(-o // n) * n
            offs.append(o)
            o += n
        return offs, o

    small = jnp.concatenate([_flat_pad(conv_w_conv), _flat_pad(ffn_w_conv), _flat_pad(attn_b_qkv), _flat_pad(attn_b_o)], axis=1)
    (small_g,) = _all_gather([small], "gather_small")

    gather_started = {}

    def start_gather(idx, after):
        if idx >= len(order):
            return 0.0
        key = order[idx]
        offs, total = layout(key)
        pieces, o = [], 0
        for (_, _, _, shard), off in zip(members_of[key], offs):
            if off > o:
                pieces.append(jnp.zeros((off - o, d), shard.dtype))
            pieces.append(shard)
            o = off + shard.shape[0]
        pack = jnp.concatenate(pieces, axis=0).astype(BF16)
        gather_started[key] = _exchange_start(pack, after, False, f"gather_start_{key[0]}_{key[1]}")
        return gather_started[key][4][0, 0]

    weights = {}

    def finish_gather(key, after):
        _, land = _exchange_wait(gather_started[key], after, False, f"gather_wait_{key[0]}_{key[1]}")
        for (wname, layer, _, shard), off in zip(members_of[key], layout(key)[0]):
            weights[(wname, layer)] = _Rows(land, off, shard.shape[0])

    def take_small(o, shape):
        size = shape[0] * shape[1] * shape[2]
        blk = small_g[:, 0, o:o + size].reshape((N_DEV,) + shape)
        return jnp.moveaxis(blk, 0, 2).reshape(shape[0], shape[1], N_DEV * shape[2])

    so = 0
    wc_conv_full = take_small(so, conv_w_conv.shape); so += _flat_pad(conv_w_conv).shape[1]
    wc_ffn_full = take_small(so, ffn_w_conv.shape); so += _flat_pad(ffn_w_conv).shape[1]
    b_qkv_full = take_small(so, (n_attn, 1, attn_b_qkv.shape[1]))[:, 0]; so += _flat_pad(attn_b_qkv).shape[1]
    b_o_full = take_small(so, (n_attn, 1, attn_b_o.shape[1]))[:, 0]

    cos_t, sin_t = _rope_tables(seq)

    xs = [x.reshape(t_all, d)]
    saved = []
    token = start_gather(0, small_g) + start_gather(1, small_g)
    for i in range(depth):
        j = i // 2
        if i > 0:
            token = start_gather(2 * i + 2, xs[-1])
        gain_mix = norm_mix[i][None, :] + token
        finish_gather(("mix", i), gain_mix if i == 0 else xs[-1])
        if i % 2 == 0:
            x1, *mix_saved = _fwd_conv_mixer(xs[-1], gain_mix, weights[("conv_w_in", j)], wc_conv_full[j],
                                             weights[("conv_w_out", j)], seq, f"fwd_conv_{i}")
        else:
            qkv = _fwd_qkv(xs[-1], gain_mix, weights[("attn_w_qkv", j)], b_qkv_full[j][None, :], cos_t, sin_t, seq,
                           f"fwd_qkv_{i}")
            x1, o, probs_t, p_sink = _fwd_attention(qkv, xs[-1], attn_sinks[j], weights[("attn_w_o", j)],
                                                    b_o_full[j][None, :], seq, f"fwd_attn_{i}")
            mix_saved = (qkv, o, probs_t, p_sink)
        token = start_gather(2 * i + 3, x1) + (start_gather(2, x1) if i == 0 else 0.0)
        gain_ffn = norm_ffn[i][None, :] + token
        finish_gather(("ffn", i), gain_ffn)
        x2, *ffn_saved = _fwd_ffn(x1, gain_ffn, weights[("ffn_w_in", i)], wc_ffn_full[i], weights[("ffn_w_down", i)],
                                  seq, f"fwd_ffn_{i}")
        saved.append((xs[-1], x1, mix_saved, ffn_saved))
        xs.append(x2)
        token = 0.0

    dx, dg_final, loss_lanes = _final_norm_loss(xs[-1], norm_final[None, :], loss_target.reshape(t_all, d), "loss_head")

    dg_mix, dg_ffn = [None] * depth, [None] * depth
    dwc_conv, dwc_ffn = [None] * n_conv, [None] * depth
    db_qkv, db_o, dsinks = [None] * n_attn, [None] * n_attn, [None] * n_attn
    scatter_started = {}

    def weight_grads(key, operands):
        offs, total = layout(key)
        parts = (N_DEV, total, d)
        for (wname, layer, _, shard), off, (a, b) in zip(members_of[key], offs, operands):
            parts = _tn_matmul(a, b, _Rows(parts, off, shard.shape[0]), f"dw_{wname}_{layer}")
        scatter_started[key] = _exchange_start(parts, operands[0][1], True, f"scatter_start_{key[0]}_{key[1]}")
        return scatter_started[key][4][0, 0]

    token = 0.0
    for i in reversed(range(depth)):
        j = i // 2
        x0, x1, mix_saved, (gate, s_act, uds, act) = saved[i]
        dgu, dwc = _bwd_ffn_inner(dx, gate, s_act, uds, wc_ffn_full[i] + token, weights[("ffn_w_down", i)], seq, f"bwd_ffn_{i}")
        dwc_ffn[i] = dwc[:3]
        dx1, h2, dg_ffn[i] = _bwd_dense_norm(dgu, weights[("ffn_w_in", i)], x1, norm_ffn[i][None, :], dx, f"bwd_ffn_norm_{i}")
        token = weight_grads(("ffn", i), [(dgu, h2), (act, dx)])
        if i % 2 == 0:
            bcv, cc, y = mix_saved
            dbcv, dwc = _bwd_conv_inner(dx1, bcv, cc, wc_conv_full[j] + token, weights[("conv_w_out", j)], seq, f"bwd_conv_{i}")
            dwc_conv[j] = dwc[:3]
            dx, h, dg_mix[i] = _bwd_dense_norm(dbcv, weights[("conv_w_in", j)], x0, norm_mix[i][None, :], dx1,
                                               f"bwd_conv_norm_{i}")
            token = weight_grads(("mix", i), [(dbcv, h), (y, dx1)])
        else:
            qkv, o, probs_t, p_sink = mix_saved
            dqkv, dsk, dbq, dbo = _bwd_attention_inner(dx1, qkv, probs_t, p_sink + token, weights[("attn_w_o", j)], cos_t,
                                                       sin_t, seq, f"bwd_attn_{i}")
            dsinks[j], db_qkv[j], db_o[j] = dsk[0:1, :attn_sinks.shape[1]], dbq, dbo
            dx, h, dg_mix[i] = _bwd_dense_norm(dqkv, weights[("attn_w_qkv", j)], x0, norm_mix[i][None, :], dx1,
                                               f"bwd_attn_norm_{i}")
            token = weight_grads(("mix", i), [(dqkv, h), (o, dx1)])
    grad_x = dx.reshape(b_loc, seq, d)

    reduced = {}

    def finish_scatter(key, after):
        _, land = _exchange_wait(scatter_started[key], after, True, f"scatter_wait_{key[0]}_{key[1]}")
        total = _sum_slots(land, f"scatter_sum_{key[0]}_{key[1]}")
        for (wname, layer, transposed, shard), off in zip(members_of[key], layout(key)[0]):
            rows = total[off:off + shard.shape[0]]
            reduced[(wname, layer)] = rows.T if transposed else rows

    last_key = order[0]
    for key in reversed(order[1:]):
        finish_scatter(key, dx)

    small_parts = [jnp.concatenate(dg_mix, axis=0), jnp.concatenate(dg_ffn, axis=0), dg_final,
                   jnp.stack(dwc_conv), jnp.stack(dwc_ffn), jnp.concatenate(db_qkv, axis=0), jnp.concatenate(db_o, axis=0),
                   jnp.concatenate(dsinks, axis=0), loss_lanes]
    flats = [_flat_pad(p) for p in small_parts]
    bounds = []
    so = 0
    for fl in flats:
        bounds.append((so, so + fl.shape[1]))
        so += fl.shape[1]
    small_rows = so // LANES
    pad_rows = (-small_rows) % SUBLANES
    part_small = jnp.pad(jnp.concatenate(flats, axis=1).reshape(small_rows, LANES), ((0, pad_rows), (0, 0)))
    loss_rows = (bounds[-1][0] // LANES, bounds[-1][1] // LANES)
    summed, loss_tile = _all_reduce_small(part_small, loss_rows, "reduce_small")
    summed = summed.reshape(1, -1)

    def small_grad(k, shape):
        lo = bounds[k][0]
        size = 1
        for s_ in shape:
            size *= s_
        return summed[0, lo:lo + size].reshape(shape)

    def my_cols(full, n_local):
        return lax.dynamic_slice_in_dim(full, me * n_local, n_local, axis=full.ndim - 1)

    g_norm_mix = small_grad(0, norm_mix.shape)
    g_norm_ffn = small_grad(1, norm_ffn.shape)
    g_norm_final = small_grad(2, norm_final.shape)
    g_conv_w_conv = my_cols(small_grad(3, (n_conv, 3, d)), conv_w_conv.shape[2])
    g_ffn_w_conv = my_cols(small_grad(4, (depth, 3, ffn_w_conv.shape[2] * N_DEV)), ffn_w_conv.shape[2])
    g_attn_b_qkv = my_cols(small_grad(5, (n_attn, attn_b_qkv.shape[1] * N_DEV)), attn_b_qkv.shape[1])
    g_attn_b_o = my_cols(small_grad(6, (n_attn, d)), attn_b_o.shape[1])
    g_attn_sinks = small_grad(7, attn_sinks.shape)
    loss = loss_tile[0, 0]

    def big_grad(wname, n_layers):
        return jnp.stack([reduced[(wname, layer)] for layer in range(n_layers)])

    grads = {
        "norm_mix": g_norm_mix, "norm_ffn": g_norm_ffn, "norm_final": g_norm_final, "conv_w_conv": g_conv_w_conv,
        "attn_w_qkv": big_grad("attn_w_qkv", n_attn), "attn_b_qkv": g_attn_b_qkv, "attn_sinks": g_attn_sinks,
        "attn_w_o": big_grad("attn_w_o", n_attn), "attn_b_o": g_attn_b_o,
        "ffn_w_in": big_grad("ffn_w_in", depth), "ffn_w_conv": g_ffn_w_conv, "ffn_w_down": big_grad("ffn_w_down", depth),
    }
    params = {
        "norm_mix": (norm_mix, m_norm_mix, v_norm_mix), "norm_ffn": (norm_ffn, m_norm_ffn, v_norm_ffn),
        "norm_final": (norm_final, m_norm_final, v_norm_final), "conv_w_in": (conv_w_in, m_conv_w_in, v_conv_w_in),
        "conv_w_conv": (conv_w_conv, m_conv_w_conv, v_conv_w_conv), "conv_w_out": (conv_w_out, m_conv_w_out, v_conv_w_out),
        "attn_w_qkv": (attn_w_qkv, m_attn_w_qkv, v_attn_w_qkv), "attn_b_qkv": (attn_b_qkv, m_attn_b_qkv, v_attn_b_qkv),
        "attn_sinks": (attn_sinks, m_attn_sinks, v_attn_sinks), "attn_w_o": (attn_w_o, m_attn_w_o, v_attn_w_o),
        "attn_b_o": (attn_b_o, m_attn_b_o, v_attn_b_o), "ffn_w_in": (ffn_w_in, m_ffn_w_in, v_ffn_w_in),
        "ffn_w_conv": (ffn_w_conv, m_ffn_w_conv, v_ffn_w_conv), "ffn_w_down": (ffn_w_down, m_ffn_w_down, v_ffn_w_down),
    }
    names = list(params)
    updates = {}

    def update(wname):
        w, m, v = params[wname]
        updates[wname] = _adamw_nd(w, grads[wname], m, v, f"adamw_{wname}")

    last_names = sorted({wname for wname, _, _, _ in members_of[last_key]})
    for wname in names:
        if wname not in last_names:
            update(wname)
    finish_scatter(last_key, updates["ffn_w_in"][0])
    for wname in last_names:
        grads[wname] = big_grad(wname, params[wname][0].shape[0])
        update(wname)
    return (loss, grad_x, *[grads[wname] for wname in names], *[updates[wname][0] for wname in names],
            *[updates[wname][1] for wname in names], *[updates[wname][2] for wname in names])
```

```python
from typing import NamedTuple

import jax
import jax.numpy as jnp
from jax import lax
from jax.experimental import pallas as pl
from jax.experimental.pallas import tpu as pltpu

F32 = jnp.float32
BF16 = jnp.bfloat16
SDS = jax.ShapeDtypeStruct
MESH = pl.DeviceIdType.MESH
ANY = pl.BlockSpec(memory_space=pl.ANY)

N_DEV = 8
EPS = 1e-5
HEAD_DIM = 64
GROUP = 4
WINDOW = 128
ROPE_THETA = 10000.0
ADAM_LR, ADAM_B1, ADAM_B2, ADAM_EPS, ADAM_WD, ADAM_STEP = 0.001, 0.9, 0.999, 1e-08, 0.01, 10

V7X_VMEM_BYTES = 64 * 1024 * 1024
VMEM_LIMIT_BYTES = V7X_VMEM_BYTES - 8 * 1024 * 1024
LANES = 128
SUBLANES = 8
TOKEN_TILE = 512
TN_TOKEN_TILE = 2048
MASKED_SCORE = -1e30


def _cparams(n_axes=1):
    return pltpu.CompilerParams(dimension_semantics=("arbitrary",) * n_axes, vmem_limit_bytes=VMEM_LIMIT_BYTES)


def _resident(shape):
    zeros = (0,) * len(shape)
    return pl.BlockSpec(shape, lambda *_: zeros, pipeline_mode=pl.Buffered(1))


class _Rows(NamedTuple):
    arr: jax.Array
    off: int
    n: int


def _rows_spec(w):
    assert w.off % w.n == 0
    block = w.off // w.n
    return pl.BlockSpec((N_DEV, w.n, w.arr.shape[2]), lambda *_: (0, block, 0), pipeline_mode=pl.Buffered(1))


def _mat(ref):
    v = ref[...]
    return v.reshape(v.shape[0] * v.shape[1], v.shape[2])


def _token_tile(seq):
    return min(TOKEN_TILE, seq // 2)


def _largest_divisor(m, cap, mult):
    best = None
    for d in range(mult, min(m, cap) + 1, mult):
        if m % d == 0:
            best = d
    return m if best is None else best


def _nt(a, b):
    return lax.dot_general(a, b, (((1,), (1,)), ((), ())), preferred_element_type=F32)


def _nn(a, b):
    return lax.dot_general(a, b, (((1,), (0,)), ((), ())), preferred_element_type=F32)


def _tn(a, b):
    return lax.dot_general(a, b, (((0,), (0,)), ((), ())), preferred_element_type=F32)


def _rms_parts(xv):
    r = lax.rsqrt(jnp.mean(xv * xv, axis=-1, keepdims=True) + EPS)
    return r, xv * r


def _rms_backward(dh, xh, r, gain, dres):
    u = dh * gain
    return dres + r * (u - xh * jnp.mean(u * xh, axis=-1, keepdims=True))


def _shifted_rows(xv, edge, k, down):
    n = xv.shape[0]
    row = lax.broadcasted_iota(jnp.int32, edge.shape, 0)
    if down:
        rolled = pltpu.roll(xv, k, 0)
        head = jnp.where(row < k, pltpu.roll(edge, k, 0), rolled[0:SUBLANES])
        return jnp.concatenate([head, rolled[SUBLANES:]], axis=0)
    rolled = pltpu.roll(xv, n - k, 0)
    tail = jnp.where(row >= SUBLANES - k, pltpu.roll(edge, SUBLANES - k, 0), rolled[n - SUBLANES:])
    return jnp.concatenate([rolled[:n - SUBLANES], tail], axis=0)


def _causal_conv3(edge_ref, xv, w_ref):
    before = edge_ref[...]
    y = (w_ref[2:3, :] * xv + w_ref[1:2, :] * _shifted_rows(xv, before, 1, True)
         + w_ref[0:1, :] * _shifted_rows(xv, before, 2, True))
    edge_ref[...] = xv[xv.shape[0] - SUBLANES:, :]
    return y


def _sigmoid(z):
    return 1.0 / (1.0 + jnp.exp(-z))


def _fwd_conv_mixer(x, gain, w_in_t, w_conv, w_out, seq, name):
    t_all, d = x.shape
    tt = _token_tile(seq)
    tps = seq // tt

    def body(x_ref, g_ref, win_ref, wc_ref, wout_ref, x1_ref, bcv_ref, cc_ref, y_ref, ext_ref):
        i = pl.program_id(0)
        xv = x_ref[...]
        r, xh = _rms_parts(xv)
        h = (xh * g_ref[...]).astype(BF16)
        bcv = _nt(h, _mat(win_ref))
        bcv_ref[...] = bcv.astype(BF16)

        @pl.when(i % tps == 0)
        def _():
            ext_ref[...] = jnp.zeros_like(ext_ref)

        cc = _causal_conv3(ext_ref, bcv[:, d:2 * d] * bcv[:, 2 * d:], wc_ref)
        cc_ref[...] = cc.astype(BF16)
        y = (bcv[:, :d] * cc).astype(BF16)
        y_ref[...] = y
        x1_ref[...] = xv + _nn(y, _mat(wout_ref))

    tile = pl.BlockSpec((tt, d), lambda i: (i, 0))
    return pl.pallas_call(
        body, name=name, grid=(t_all // tt,),
        in_specs=[tile, _resident((1, d)), _rows_spec(w_in_t), _resident((3, d)), _rows_spec(w_out)],
        out_specs=[tile, pl.BlockSpec((tt, 3 * d), lambda i: (i, 0)), tile, tile],
        out_shape=[SDS((t_all, d), F32), SDS((t_all, 3 * d), BF16), SDS((t_all, d), BF16), SDS((t_all, d), BF16)],
        scratch_shapes=[pltpu.VMEM((SUBLANES, d), F32)],
        compiler_params=_cparams(),
    )(x, gain, w_in_t.arr, w_conv, w_out.arr)


def _fwd_ffn(x, gain, w_in_t, w_conv, w_down, seq, name):
    t_all, d = x.shape
    f = w_down.n * N_DEV
    tt = _token_tile(seq) // 2
    tps = seq // tt

    def body(x_ref, g_ref, win_ref, wc_ref, wd_ref, x2_ref, gate_ref, s_ref, uds_ref, a_ref, ext_ref):
        i = pl.program_id(0)
        xv = x_ref[...]
        r, xh = _rms_parts(xv)
        h = (xh * g_ref[...]).astype(BF16)
        gu = _nt(h, _mat(win_ref))
        gate = gu[:, :f]
        u = gu[:, f:]
        gate_ref[...] = gate.astype(BF16)

        @pl.when(i % tps == 0)
        def _():
            ext_ref[...] = jnp.zeros_like(ext_ref)

        gc = _causal_conv3(ext_ref, gate, wc_ref)
        sig = _sigmoid(gc)
        s = gc * sig
        s_ref[...] = s.astype(BF16)
        uds_ref[...] = (u * (sig * (1.0 + gc * (1.0 - sig)))).astype(BF16)
        a = (s * u).astype(BF16)
        a_ref[...] = a
        x2_ref[...] = xv + _nn(a, _mat(wd_ref))

    wide = pl.BlockSpec((tt, f), lambda i: (i, 0))
    return pl.pallas_call(
        body, name=name, grid=(t_all // tt,),
        in_specs=[pl.BlockSpec((tt, d), lambda i: (i, 0)), _resident((1, d)), _rows_spec(w_in_t),
                  _resident((3, f)), _rows_spec(w_down)],
        out_specs=[pl.BlockSpec((tt, d), lambda i: (i, 0)), wide, wide, wide, wide],
        out_shape=[SDS((t_all, d), F32)] + [SDS((t_all, f), BF16)] * 4,
        scratch_shapes=[pltpu.VMEM((SUBLANES, f), F32)],
        compiler_params=_cparams(),
    )(x, gain, w_in_t.arr, w_conv, w_down.arr)


def _rope_partner(xs, lane_lo):
    return jnp.where(lane_lo, pltpu.roll(xs, LANES - HEAD_DIM // 2, 1), pltpu.roll(xs, HEAD_DIM // 2, 1))


def _fwd_qkv(x, gain, w_qkv_t, b_qkv, cos_t, sin_t, seq, name):
    t_all, d = x.shape
    width = w_qkv_t.n * N_DEV
    kvw = (width - d) // 2
    tt = _token_tile(seq)
    tps = seq // tt
    scale = HEAD_DIM ** -0.5

    def body(x_ref, g_ref, w_ref, b_ref, cos_ref, sin_ref, qkv_ref):
        xv = x_ref[...]
        r, xh = _rms_parts(xv)
        h = (xh * g_ref[...]).astype(BF16)
        qkv = _nt(h, _mat(w_ref)) + b_ref[...]
        cosv = cos_ref[...]
        sinv = sin_ref[...]
        lane_lo = (lax.broadcasted_iota(jnp.int32, (tt, LANES), 1) % HEAD_DIM) < HEAD_DIM // 2
        for s in range((d + kvw) // LANES):
            xs = qkv[:, s * LANES:(s + 1) * LANES]
            roped = xs * cosv + _rope_partner(xs, lane_lo) * sinv
            if s * LANES < d:
                roped = roped * scale
            qkv_ref[:, s * LANES:(s + 1) * LANES] = roped.astype(BF16)
        qkv_ref[:, d + kvw:] = qkv[:, d + kvw:].astype(BF16)

    return pl.pallas_call(
        body, name=name, grid=(t_all // tt,),
        in_specs=[pl.BlockSpec((tt, d), lambda i: (i, 0)), _resident((1, d)), _rows_spec(w_qkv_t),
                  _resident((1, width)), pl.BlockSpec((tt, LANES), lambda i: (i % tps, 0)),
                  pl.BlockSpec((tt, LANES), lambda i: (i % tps, 0))],
        out_specs=pl.BlockSpec((tt, width), lambda i: (i, 0)),
        out_shape=SDS((t_all, width), BF16),
        compiler_params=_cparams(),
    )(x, gain, w_qkv_t.arr, b_qkv, cos_t, sin_t)


def _stack_heads(ref, row0, kh):
    return jnp.concatenate(
        [ref[row0:row0 + WINDOW, (kh * GROUP + g) * HEAD_DIM:(kh * GROUP + g + 1) * HEAD_DIM] for g in range(GROUP)],
        axis=0)


def _band_bias():
    r = lax.broadcasted_iota(jnp.int32, (WINDOW, 2 * WINDOW), 0)
    j = lax.broadcasted_iota(jnp.int32, (WINDOW, 2 * WINDOW), 1)
    base = (j > r) & (j <= r + WINDOW)
    return jnp.where(base, 0.0, MASKED_SCORE), jnp.where(base & (j >= WINDOW), 0.0, MASKED_SCORE)


def _fwd_attention(qkv, x, sinks, w_o, b_o, seq, name):
    t_all, d = x.shape
    width = qkv.shape[1]
    kvw = (width - d) // 2
    n_kv = kvw // HEAD_DIM
    tt = _token_tile(seq)
    tps = seq // tt
    nblk = tt // WINDOW

    def body(sink_ref, qkv_ref, kvp_ref, x_ref, wo_ref, bo_ref, x1_ref, o_ref, p_ref, psink_ref,
             kvext_ref, oscr_ref, bias_ref, s_ref):
        i = pl.program_id(0)

        @pl.when(i == 0)
        def _():
            base, first = _band_bias()
            bias_ref[0], bias_ref[1] = base.T, first.T

        kvext_ref[0:WINDOW, :] = kvp_ref[...]
        kvext_ref[WINDOW:, :] = qkv_ref[:, d:]
        at_seq_start = (i % tps == 0).astype(jnp.int32)
        steps = [(n, kh) for n in range(nblk) for kh in range(n_kv)]

        def scores(step):
            n, kh = steps[step]
            qs = _stack_heads(qkv_ref, n * WINDOW, kh)
            kb = kvext_ref[n * WINDOW:(n + 2) * WINDOW, kh * HEAD_DIM:(kh + 1) * HEAD_DIM]
            s_ref[step % 2] = _nt(kb, qs)

        scores(0)
        for step, (n, kh) in enumerate(steps):
            buf = step % 2
            if step + 1 < len(steps):
                scores(step + 1)
            vb = kvext_ref[n * WINDOW:(n + 2) * WINDOW, kvw + kh * HEAD_DIM:kvw + (kh + 1) * HEAD_DIM]
            bias = bias_ref[at_seq_start if n == 0 else 0]
            for g in range(GROUP):
                cols = slice(g * WINDOW, (g + 1) * WINDOW)
                sink = sink_ref[kh * GROUP + g]
                sv = s_ref[buf, :, cols] + bias
                m = jnp.maximum(jnp.max(sv, axis=0, keepdims=True), sink)
                p = jnp.exp(sv - m)
                e_sink = jnp.exp(sink - m)
                inv = 1.0 / (jnp.sum(p, axis=0, keepdims=True) + e_sink)
                p_ref[n, kh, :, cols] = (p * inv).astype(BF16)
                psink_ref[n, kh, g:g + 1, :] = e_sink * inv
            o_s = _tn(vb, p_ref[n, kh]).T
            for g in range(GROUP):
                hd = kh * GROUP + g
                oscr_ref[n * WINDOW:(n + 1) * WINDOW, hd * HEAD_DIM:(hd + 1) * HEAD_DIM] = (
                    o_s[g * WINDOW:(g + 1) * WINDOW].astype(BF16))
        o = oscr_ref[...]
        o_ref[...] = o
        x1_ref[...] = x_ref[...] + _nn(o, _mat(wo_ref)) + bo_ref[...]

    kv_blocks = tt // WINDOW
    return pl.pallas_call(
        body, name=name, grid=(t_all // tt,),
        in_specs=[pl.BlockSpec(memory_space=pltpu.SMEM),
                  pl.BlockSpec((tt, width), lambda i: (i, 0)),
                  pl.BlockSpec((WINDOW, 2 * kvw), lambda i: (jnp.maximum(i * kv_blocks - 1, 0), d // (2 * kvw))),
                  pl.BlockSpec((tt, d), lambda i: (i, 0)), _rows_spec(w_o), _resident((1, d))],
        out_specs=[pl.BlockSpec((tt, d), lambda i: (i, 0)), pl.BlockSpec((tt, d), lambda i: (i, 0)),
                   pl.BlockSpec((nblk, n_kv, 2 * WINDOW, GROUP * WINDOW), lambda i: (i, 0, 0, 0)),
                   pl.BlockSpec((nblk, n_kv, GROUP, WINDOW), lambda i: (i, 0, 0, 0))],
        out_shape=[SDS((t_all, d), F32), SDS((t_all, d), BF16),
                   SDS((t_all // WINDOW, n_kv, 2 * WINDOW, GROUP * WINDOW), BF16),
                   SDS((t_all // WINDOW, n_kv, GROUP, WINDOW), F32)],
        scratch_shapes=[pltpu.VMEM((tt + WINDOW, 2 * kvw), BF16), pltpu.VMEM((tt, d), BF16),
                        pltpu.VMEM((2, 2 * WINDOW, WINDOW), F32), pltpu.VMEM((2, 2 * WINDOW, GROUP * WINDOW), F32)],
        compiler_params=_cparams(),
    )(sinks, qkv, qkv, x, w_o.arr, b_o)


def _final_norm_loss(x, gain, target, name):
    t_all, d = x.shape
    tt = min(TOKEN_TILE, t_all)

    def body(x_ref, g_ref, t_ref, dx_ref, dg_ref, loss_ref):
        i = pl.program_id(0)
        xv = x_ref[...]
        r, xh = _rms_parts(xv)
        gain_v = g_ref[...]
        e = xh * gain_v - t_ref[...]
        dy = e * (1.0 / d)
        dx_ref[...] = _rms_backward(dy, xh, r, gain_v, 0.0)

        @pl.when(i == 0)
        def _():
            dg_ref[...] = jnp.zeros_like(dg_ref)
            loss_ref[...] = jnp.zeros_like(loss_ref)

        dg_ref[...] += jnp.sum(dy * xh, axis=0, keepdims=True)
        loss_ref[...] += (0.5 / d) * jnp.sum(e * e, axis=0, keepdims=True)

    return pl.pallas_call(
        body, name=name, grid=(t_all // tt,),
        in_specs=[pl.BlockSpec((tt, d), lambda i: (i, 0)), _resident((1, d)), pl.BlockSpec((tt, d), lambda i: (i, 0))],
        out_specs=[pl.BlockSpec((tt, d), lambda i: (i, 0)), pl.BlockSpec((1, d), lambda i: (0, 0)),
                   pl.BlockSpec((1, d), lambda i: (0, 0))],
        out_shape=[SDS((t_all, d), F32), SDS((1, d), F32), SDS((1, d), F32)],
        compiler_params=_cparams(),
    )(x, gain, target)


def _bwd_ffn_inner(dx2, gate, s_act, uds, w_conv, w_down, seq, name):
    t_all, d = dx2.shape
    f = w_down.n * N_DEV
    tt = _token_tile(seq)
    tps = seq // tt
    nt = t_all // tt

    def body(dx_ref, g_ref, s_ref, uds_ref, wc_ref, wd_ref, dgu_ref, dwc_ref, aext_ref, da_ref):
        i = pl.program_id(0)
        ti = nt - 1 - i
        da_ref[...] = _nt(dx_ref[...].astype(BF16), _mat(wd_ref))

        @pl.when(ti % tps == tps - 1)
        def _():
            aext_ref[...] = jnp.zeros_like(aext_ref)

        @pl.when(i == 0)
        def _():
            dwc_ref[...] = jnp.zeros_like(dwc_ref)

        for c in range(f // LANES):
            cols = slice(c * LANES, (c + 1) * LANES)
            da = da_ref[:, cols]
            g = g_ref[:, cols].astype(F32)
            dgc = da * uds_ref[:, cols].astype(F32)
            after = aext_ref[:, cols]
            sh1 = _shifted_rows(dgc, after, 1, False)
            sh2 = _shifted_rows(dgc, after, 2, False)
            aext_ref[:, cols] = dgc[0:SUBLANES, :]
            dg = wc_ref[2:3, cols] * dgc + wc_ref[1:2, cols] * sh1 + wc_ref[0:1, cols] * sh2
            dgu_ref[:, cols] = dg.astype(BF16)
            dgu_ref[:, f + c * LANES:f + (c + 1) * LANES] = (da * s_ref[:, cols].astype(F32)).astype(BF16)
            dwc_ref[0:1, cols] += jnp.sum(g * sh2, axis=0, keepdims=True)
            dwc_ref[1:2, cols] += jnp.sum(g * sh1, axis=0, keepdims=True)
            dwc_ref[2:3, cols] += jnp.sum(g * dgc, axis=0, keepdims=True)

    rev = lambda i: (nt - 1 - i, 0)
    return pl.pallas_call(
        body, name=name, grid=(nt,),
        in_specs=[pl.BlockSpec((tt, d), rev)] + [pl.BlockSpec((tt, f), rev)] * 3 + [_resident((3, f)), _rows_spec(w_down)],
        out_specs=[pl.BlockSpec((tt, 2 * f), rev), pl.BlockSpec((8, f), lambda i: (0, 0))],
        out_shape=[SDS((t_all, 2 * f), BF16), SDS((8, f), F32)],
        scratch_shapes=[pltpu.VMEM((SUBLANES, f), F32), pltpu.VMEM((tt, f), F32)],
        compiler_params=_cparams(),
    )(dx2, gate, s_act, uds, w_conv, w_down.arr)


def _bwd_conv_inner(dx1, bcv, cc, w_conv, w_out, seq, name):
    t_all, d = dx1.shape
    tt = _token_tile(seq)
    tps = seq // tt
    nt = t_all // tt

    def body(dx_ref, bcv_ref, cc_ref, wc_ref, wout_ref, dbcv_ref, dwc_ref, aext_ref, dy_ref):
        i = pl.program_id(0)
        ti = nt - 1 - i
        dy_ref[...] = _nt(dx_ref[...].astype(BF16), _mat(wout_ref))

        @pl.when(ti % tps == tps - 1)
        def _():
            aext_ref[...] = jnp.zeros_like(aext_ref)

        @pl.when(i == 0)
        def _():
            dwc_ref[...] = jnp.zeros_like(dwc_ref)

        for s in range(d // LANES):
            cols = slice(s * LANES, (s + 1) * LANES)
            ccols = slice(d + s * LANES, d + (s + 1) * LANES)
            vcols = slice(2 * d + s * LANES, 2 * d + (s + 1) * LANES)
            dy = dy_ref[:, cols]
            c = bcv_ref[:, ccols].astype(F32)
            v = bcv_ref[:, vcols].astype(F32)
            cv = c * v
            dcc = dy * bcv_ref[:, cols].astype(F32)
            after = aext_ref[:, cols]
            sh1 = _shifted_rows(dcc, after, 1, False)
            sh2 = _shifted_rows(dcc, after, 2, False)
            aext_ref[:, cols] = dcc[0:SUBLANES, :]
            dcv = wc_ref[2:3, cols] * dcc + wc_ref[1:2, cols] * sh1 + wc_ref[0:1, cols] * sh2
            dbcv_ref[:, cols] = (dy * cc_ref[:, cols].astype(F32)).astype(BF16)
            dbcv_ref[:, ccols] = (dcv * v).astype(BF16)
            dbcv_ref[:, vcols] = (dcv * c).astype(BF16)
            dwc_ref[0:1, cols] += jnp.sum(cv * sh2, axis=0, keepdims=True)
            dwc_ref[1:2, cols] += jnp.sum(cv * sh1, axis=0, keepdims=True)
            dwc_ref[2:3, cols] += jnp.sum(cv * dcc, axis=0, keepdims=True)

    return pl.pallas_call(
        body, name=name, grid=(nt,),
        in_specs=[pl.BlockSpec((tt, d), lambda i: (nt - 1 - i, 0)),
                  pl.BlockSpec((tt, 3 * d), lambda i: (nt - 1 - i, 0)),
                  pl.BlockSpec((tt, d), lambda i: (nt - 1 - i, 0)),
                  _resident((3, d)), _rows_spec(w_out)],
        out_specs=[pl.BlockSpec((tt, 3 * d), lambda i: (nt - 1 - i, 0)), pl.BlockSpec((8, d), lambda i: (0, 0))],
        out_shape=[SDS((t_all, 3 * d), BF16), SDS((8, d), F32)],
        scratch_shapes=[pltpu.VMEM((SUBLANES, d), F32), pltpu.VMEM((tt, d), F32)],
        compiler_params=_cparams(),
    )(dx1, bcv, cc, w_conv, w_out.arr)


def _bwd_attention_inner(dx1, qkv, probs_t, p_sink, w_o, cos_t, sin_t, seq, name):
    t_all, d = dx1.shape
    width = qkv.shape[1]
    kvw = (width - d) // 2
    n_kv = kvw // HEAD_DIM
    tt = _token_tile(seq)
    tps = seq // tt
    nt = t_all // tt
    nblk = tt // WINDOW
    scale = HEAD_DIM ** -0.5

    def body(dx_ref, qkv_ref, kvp_ref, p_ref, psink_ref, cos_ref, sin_ref, wo_ref,
             dqkv_ref, dsink_ref, dbqkv_ref, dbo_ref,
             kvext_ref, dkvext_ref, carry_ref, dq_ref, do_ref, dp_ref, ds_ref):
        i = pl.program_id(0)
        dxv = dx_ref[...]
        do_ref[...] = _nt(dxv.astype(BF16), _mat(wo_ref)).astype(BF16)
        kvext_ref[0:WINDOW, :] = kvp_ref[...]
        kvext_ref[WINDOW:, :] = qkv_ref[:, d:]
        dkvext_ref[...] = jnp.zeros_like(dkvext_ref)

        @pl.when(i == 0)
        def _():
            carry_ref[...] = jnp.zeros_like(carry_ref)
            dsink_ref[...] = jnp.zeros_like(dsink_ref)
            dbqkv_ref[...] = jnp.zeros_like(dbqkv_ref)
            dbo_ref[...] = jnp.zeros_like(dbo_ref)

        head_lane = lax.broadcasted_iota(jnp.int32, (1, LANES), 1)
        dsink = jnp.zeros((1, LANES), F32)
        for n in range(nblk):
            for kh in range(n_kv):
                buf = (n * n_kv + kh) % 2
                qs = _stack_heads(qkv_ref, n * WINDOW, kh)
                dos = _stack_heads(do_ref, n * WINDOW, kh)
                kcols = slice(kh * HEAD_DIM, (kh + 1) * HEAD_DIM)
                vcols = slice(kvw + kh * HEAD_DIM, kvw + (kh + 1) * HEAD_DIM)
                band = slice(n * WINDOW, (n + 2) * WINDOW)
                kb = kvext_ref[band, kcols]
                vb = kvext_ref[band, vcols]
                dp_ref[buf] = _nt(vb, dos)
                for g in range(GROUP):
                    hd = kh * GROUP + g
                    cols = slice(g * WINDOW, (g + 1) * WINDOW)
                    probs = p_ref[n, kh, :, cols].astype(F32)
                    dp = dp_ref[buf, :, cols]
                    dsum = jnp.sum(probs * dp, axis=0, keepdims=True)
                    ds_ref[buf, :, cols] = (probs * (dp - dsum)).astype(BF16)
                    dsink = dsink - jnp.where(head_lane == hd, jnp.sum(psink_ref[n, kh, g:g + 1, :] * dsum), 0.0)
                ds_t = ds_ref[buf]
                dkvext_ref[band, vcols] += _nn(p_ref[n, kh], dos)
                dkvext_ref[band, kcols] += _nn(ds_t, qs)
                dq_s = _tn(kb, ds_t).T
                for g in range(GROUP):
                    hd = kh * GROUP + g
                    dq_ref[n * WINDOW:(n + 1) * WINDOW, hd * HEAD_DIM:(hd + 1) * HEAD_DIM] = dq_s[g * WINDOW:(g + 1) * WINDOW]
        dsink_ref[0:1, :] += dsink
        dkvext_ref[tt:tt + WINDOW, :] += carry_ref[...]
        carry_ref[...] = dkvext_ref[0:WINDOW, :]

        cosv = cos_ref[...]
        sinv = sin_ref[...]
        lane_lo = (lax.broadcasted_iota(jnp.int32, (tt, LANES), 1) % HEAD_DIM) < HEAD_DIM // 2
        for s in range((d + kvw) // LANES):
            if s * LANES < d:
                dy = dq_ref[:, s * LANES:(s + 1) * LANES] * scale
            else:
                dy = dkvext_ref[WINDOW:, s * LANES - d:(s + 1) * LANES - d]
            dpre = dy * cosv - _rope_partner(dy, lane_lo) * sinv
            dqkv_ref[:, s * LANES:(s + 1) * LANES] = dpre.astype(BF16)
            dbqkv_ref[0:1, s * LANES:(s + 1) * LANES] += jnp.sum(dpre, axis=0, keepdims=True)
        dv = dkvext_ref[WINDOW:, kvw:]
        dqkv_ref[:, d + kvw:] = dv.astype(BF16)
        dbqkv_ref[0:1, d + kvw:] += jnp.sum(dv, axis=0, keepdims=True)
        dbo_ref[...] += jnp.sum(dxv, axis=0, keepdims=True)

    kv_blocks = tt // WINDOW
    return pl.pallas_call(
        body, name=name, grid=(nt,),
        in_specs=[pl.BlockSpec((tt, d), lambda i: (nt - 1 - i, 0)),
                  pl.BlockSpec((tt, width), lambda i: (nt - 1 - i, 0)),
                  pl.BlockSpec((WINDOW, 2 * kvw), lambda i: (jnp.maximum((nt - 1 - i) * kv_blocks - 1, 0), d // (2 * kvw))),
                  pl.BlockSpec((nblk, n_kv, 2 * WINDOW, GROUP * WINDOW), lambda i: (nt - 1 - i, 0, 0, 0)),
                  pl.BlockSpec((nblk, n_kv, GROUP, WINDOW), lambda i: (nt - 1 - i, 0, 0, 0)),
                  pl.BlockSpec((tt, LANES), lambda i: ((nt - 1 - i) % tps, 0)),
                  pl.BlockSpec((tt, LANES), lambda i: ((nt - 1 - i) % tps, 0)),
                  _rows_spec(w_o)],
        out_specs=[pl.BlockSpec((tt, width), lambda i: (nt - 1 - i, 0)), pl.BlockSpec((8, LANES), lambda i: (0, 0)),
                   pl.BlockSpec((1, width), lambda i: (0, 0)), pl.BlockSpec((1, d), lambda i: (0, 0))],
        out_shape=[SDS((t_all, width), BF16), SDS((8, LANES), F32), SDS((1, width), F32), SDS((1, d), F32)],
        scratch_shapes=[pltpu.VMEM((tt + WINDOW, 2 * kvw), BF16), pltpu.VMEM((tt + WINDOW, 2 * kvw), F32),
                        pltpu.VMEM((WINDOW, 2 * kvw), F32), pltpu.VMEM((tt, d), F32), pltpu.VMEM((tt, d), BF16),
                        pltpu.VMEM((2, 2 * WINDOW, GROUP * WINDOW), F32), pltpu.VMEM((2, 2 * WINDOW, GROUP * WINDOW), BF16)],
        compiler_params=_cparams(),
    )(dx1, qkv, qkv, probs_t, p_sink, cos_t, sin_t, w_o.arr)


def _bwd_dense_norm(dy, w_t, x, gain, dres, name):
    t_all, d = x.shape
    n = dy.shape[1]
    tt = min(TOKEN_TILE, t_all)

    def body(dy_ref, w_ref, x_ref, g_ref, dres_ref, dx_ref, h_ref, dg_ref):
        i = pl.program_id(0)
        dh = _nn(dy_ref[...], _mat(w_ref))
        r, xh = _rms_parts(x_ref[...])
        gain_v = g_ref[...]
        h_ref[...] = (xh * gain_v).astype(BF16)
        dx_ref[...] = _rms_backward(dh, xh, r, gain_v, dres_ref[...])

        @pl.when(i == 0)
        def _():
            dg_ref[...] = jnp.zeros_like(dg_ref)

        dg_ref[...] += jnp.sum(dh * xh, axis=0, keepdims=True)

    return pl.pallas_call(
        body, name=name, grid=(t_all // tt,),
        in_specs=[pl.BlockSpec((tt, n), lambda i: (i, 0)), _rows_spec(w_t), pl.BlockSpec((tt, d), lambda i: (i, 0)),
                  _resident((1, d)), pl.BlockSpec((tt, d), lambda i: (i, 0))],
        out_specs=[pl.BlockSpec((tt, d), lambda i: (i, 0)), pl.BlockSpec((tt, d), lambda i: (i, 0)),
                   pl.BlockSpec((1, d), lambda i: (0, 0))],
        out_shape=[SDS((t_all, d), F32), SDS((t_all, d), BF16), SDS((1, d), F32)],
        compiler_params=_cparams(),
    )(dy, w_t.arr, x, gain, dres)


def _tn_matmul(a, b, dest, name):
    t_all, m = a.shape
    d = b.shape[1]
    n = dest.n
    assert m == N_DEV * n and dest.off % n == 0
    k = max(kk for kk in (1, 2, 4, 8) if kk * n <= max(n, 1536))
    tm = k * n
    tt = min(TN_TOKEN_TILE, t_all)
    n_t = t_all // tt
    fresh = not hasattr(dest.arr, "dtype")

    def body(a_ref, b_ref, *rest):
        o_ref, acc_ref = rest[-2:]
        t = pl.program_id(1)

        @pl.when(t == 0)
        def _():
            acc_ref[...] = jnp.zeros_like(acc_ref)

        acc_ref[...] += _tn(a_ref[...], b_ref[...].astype(BF16))

        @pl.when(t == n_t - 1)
        def _():
            o_ref[...] = acc_ref[...].astype(BF16).reshape(k, n, d)

    block = dest.off // n
    return pl.pallas_call(
        body, name=name, grid=(m // tm, n_t),
        in_specs=[pl.BlockSpec((tt, tm), lambda j, t: (t, j)), pl.BlockSpec((tt, d), lambda j, t: (t, 0))] + ([] if fresh else [ANY]),
        out_specs=pl.BlockSpec((k, n, d), lambda j, t: (j, block, 0)),
        out_shape=SDS(tuple(dest.arr) if fresh else dest.arr.shape, BF16),
        scratch_shapes=[pltpu.VMEM((tm, d), F32)],
        input_output_aliases={} if fresh else {2: 0},
        compiler_params=_cparams(2),
    )(*((a, b) if fresh else (a, b, dest.arr)))


def _my_place():
    return lax.axis_index("x"), lax.axis_index("y"), lax.axis_index("c")


def _other_chips(x, y):
    return [(1 - x, y), (x, 1 - y), (1 - x, 1 - y)]


def _all_gather(blocks, name):
    n_arr = len(blocks)

    def body(*refs):
        in_refs = refs[:n_arr]
        out_refs = refs[n_arr:2 * n_arr]
        send_sems, recv_sems, local_sems = refs[2 * n_arr:]
        x, y, c = _my_place()
        me, sibling = (x, y, c), (x, y, 1 - c)
        chips = _other_chips(x, y)

        def slot(a, place):
            px, py, pc = place
            return out_refs[a].at[4 * px + 2 * py + pc]

        def copy(a, k, block, to, src=None):
            return pltpu.make_async_remote_copy(
                src_ref=slot(a, block) if src is None else src, dst_ref=slot(a, block),
                send_sem=send_sems.at[a, k], recv_sem=recv_sems.at[a, k], device_id=to, device_id_type=MESH)

        started = []
        local = []
        for a in range(n_arr):
            mine = pltpu.make_async_copy(in_refs[a], slot(a, me), local_sems.at[a])
            mine.start()
            local.append(mine)
            first = [copy(a, 0, me, sibling, src=in_refs[a])]
            first += [copy(a, 1 + j, me, (*chip, c), src=in_refs[a]) for j, chip in enumerate(chips)]
            for cp in first:
                cp.start()
            started += first
        for a in range(n_arr):
            for j, chip in enumerate(chips):
                copy(a, 1 + j, (*chip, c), me).wait_recv()
                passed = copy(a, 4 + j, (*chip, c), sibling)
                passed.start()
                started.append(passed)
        for a in range(n_arr):
            copy(a, 0, sibling, me).wait_recv()
            for j, chip in enumerate(chips):
                copy(a, 4 + j, (*chip, 1 - c), me).wait_recv()
        for cp in started:
            cp.wait_send()
        for mine in local:
            mine.wait()

    return pl.pallas_call(
        body, name=name,
        in_specs=[ANY] * n_arr, out_specs=[ANY] * n_arr,
        out_shape=[SDS((N_DEV,) + b.shape, b.dtype) for b in blocks],
        scratch_shapes=[pltpu.SemaphoreType.DMA((n_arr, 7)), pltpu.SemaphoreType.DMA((n_arr, 7)),
                        pltpu.SemaphoreType.DMA((n_arr,))],
    )(*blocks)


def _peer_of(k, x, y, c):
    return x ^ ((k >> 2) & 1), y ^ ((k >> 1) & 1), c ^ (k & 1)


HBM = pl.BlockSpec(memory_space=pltpu.HBM)
SEM = pl.BlockSpec(memory_space=pltpu.SEMAPHORE)
DATAFLOW_EFFECT = pltpu.SideEffectType.DATAFLOW_SIDE_EFFECTING


def _peer_copies(src_ref, land_ref, send_sems, recv_sems, per_peer):
    x, y, c = _my_place()
    me = 4 * x + 2 * y + c
    copies = []
    for k in range(1, N_DEV):
        px, py, pc = _peer_of(k, x, y, c)
        peer = 4 * px + 2 * py + pc
        copies.append(pltpu.make_async_remote_copy(
            src_ref=src_ref.at[peer] if per_peer else src_ref, dst_ref=land_ref.at[me],
            send_sem=send_sems.at[k - 1], recv_sem=recv_sems.at[k - 1], device_id=(px, py, pc), device_id_type=MESH))
    own = pltpu.make_async_copy(src_ref.at[me] if per_peer else src_ref, land_ref.at[me], send_sems.at[N_DEV - 1])
    return copies, own


def _exchange_start(src, after, per_peer, name):
    rows, d = src.shape[-2:]

    def body(src_ref, land_ref, after_ref, send_sems, recv_sems, src_thru, land_thru, token):
        copies, own = _peer_copies(src_ref, land_ref, send_sems, recv_sems, per_peer)
        for cp in copies:
            cp.start()
        own.start()
        token[...] = jnp.zeros_like(token)

    return pl.pallas_call(
        body, name=name,
        out_shape=(pltpu.SemaphoreType.DMA((N_DEV,)), pltpu.SemaphoreType.DMA((N_DEV - 1,)), pltpu.HBM(src.shape, src.dtype),
                   pltpu.HBM((N_DEV, rows, d), src.dtype), SDS((SUBLANES, LANES), F32)),
        in_specs=(HBM, HBM, ANY), out_specs=(SEM, SEM, HBM, HBM, pl.BlockSpec(memory_space=pltpu.VMEM)),
        input_output_aliases={0: 2, 1: 3},
        compiler_params=pltpu.CompilerParams(has_side_effects=DATAFLOW_EFFECT),
    )(pltpu.with_memory_space_constraint(src, pltpu.HBM),
      pltpu.with_memory_space_constraint(lax.empty((N_DEV, rows, d), src.dtype), pltpu.HBM), after)


def _exchange_wait(started, after, per_peer, name):
    send_sems, recv_sems, src_thru, land_thru, _ = started

    def body(src_ref, land_ref, send_sems, recv_sems, after_ref, src_out, land_out):
        copies, own = _peer_copies(src_ref, land_ref, send_sems, recv_sems, per_peer)
        for cp in copies:
            cp.wait_send()
            cp.wait_recv()
        own.wait()

    return pl.pallas_call(
        body, name=name,
        out_shape=(pltpu.HBM(src_thru.shape, src_thru.dtype), pltpu.HBM(land_thru.shape, land_thru.dtype)),
        in_specs=(HBM, HBM, SEM, SEM, ANY), out_specs=(HBM, HBM), input_output_aliases={0: 0, 1: 1},
        compiler_params=pltpu.CompilerParams(has_side_effects=DATAFLOW_EFFECT),
    )(src_thru, land_thru, send_sems, recv_sems, after)


def _sum_slots(slots, name):
    _, rows, d = slots.shape
    tr = _largest_divisor(rows, 512, 16)

    def body(s_ref, o_ref):
        acc = s_ref[0].astype(F32)
        for dev in range(1, N_DEV):
            acc = acc + s_ref[dev].astype(F32)
        o_ref[...] = acc

    return pl.pallas_call(
        body, name=name, grid=(rows // tr,),
        in_specs=[pl.BlockSpec((N_DEV, tr, d), lambda r: (0, r, 0))], out_specs=pl.BlockSpec((tr, d), lambda r: (r, 0)),
        out_shape=SDS((rows, d), F32), compiler_params=_cparams(),
    )(slots)


def _all_reduce_small(part, loss_rows, name):
    rows, lanes = part.shape
    lo, hi = loss_rows

    def body(x_ref, out_ref, loss_ref, gath_ref, send_sems, recv_sems):
        x, y, c = _my_place()
        me = 4 * x + 2 * y + c
        gath_ref[me] = x_ref[...]
        copies = []
        for k in range(1, N_DEV):
            peer = (x ^ ((k >> 2) & 1), y ^ ((k >> 1) & 1), c ^ (k & 1))
            cp = pltpu.make_async_remote_copy(
                src_ref=x_ref, dst_ref=gath_ref.at[me], send_sem=send_sems.at[k - 1], recv_sem=recv_sems.at[k - 1],
                device_id=peer, device_id_type=MESH)
            cp.start()
            copies.append(cp)
        for cp in copies:
            cp.wait_recv()
        for cp in copies:
            cp.wait_send()
        acc = gath_ref[0]
        for dev in range(1, N_DEV):
            acc = acc + gath_ref[dev]
        out_ref[...] = acc
        loss_ref[...] = jnp.full(loss_ref.shape, jnp.sum(acc[lo:hi, :]), F32)

    vmem = pl.BlockSpec(memory_space=pltpu.VMEM)
    return pl.pallas_call(
        body, name=name, in_specs=[vmem], out_specs=[vmem, vmem],
        out_shape=[SDS((rows, lanes), F32), SDS((SUBLANES, LANES), F32)],
        scratch_shapes=[pltpu.VMEM((N_DEV, rows, lanes), F32), pltpu.SemaphoreType.DMA((N_DEV - 1,)),
                        pltpu.SemaphoreType.DMA((N_DEV - 1,))],
    )(part)


def _adamw(w, g, m, v, name):
    rows, cols = w.shape
    tr = rows if rows % SUBLANES else _largest_divisor(rows, 512, SUBLANES)

    def body(w_ref, g_ref, m_ref, v_ref, d_ref, nm_ref, nv_ref):
        gv = g_ref[...]
        nm = ADAM_B1 * m_ref[...] + (1.0 - ADAM_B1) * gv
        nv = ADAM_B2 * v_ref[...] + (1.0 - ADAM_B2) * (gv * gv)
        m_hat = nm / (1.0 - ADAM_B1 ** ADAM_STEP)
        v_hat = nv / (1.0 - ADAM_B2 ** ADAM_STEP)
        d_ref[...] = -ADAM_LR * (m_hat / (jnp.sqrt(v_hat) + ADAM_EPS) + ADAM_WD * w_ref[...])
        nm_ref[...] = nm
        nv_ref[...] = nv

    spec = pl.BlockSpec((tr, cols), lambda i: (i, 0))
    return pl.pallas_call(
        body, name=name, grid=(rows // tr,), in_specs=[spec] * 4, out_specs=[spec] * 3,
        out_shape=[SDS((rows, cols), F32)] * 3, compiler_params=_cparams(),
    )(w, g, m, v)


def _adamw_nd(w, g, m, v, name):
    shape = w.shape
    two_d = (1, shape[0]) if len(shape) == 1 else (-1, shape[-1])
    outs = _adamw(w.reshape(two_d), g.reshape(two_d), m.reshape(two_d), v.reshape(two_d), name)
    return [o.reshape(shape) for o in outs]


def _rope_tables(seq):
    pos = jnp.arange(seq, dtype=F32)
    inv_freq = 1.0 / (ROPE_THETA ** (jnp.arange(0, HEAD_DIM, 2, dtype=F32) / HEAD_DIM))
    ang = pos[:, None] * inv_freq[None, :]
    cos, sin = jnp.cos(ang), jnp.sin(ang)
    reps = LANES // HEAD_DIM
    cos_t = jnp.tile(jnp.concatenate([cos, cos], axis=1), (1, reps))
    sin_t = jnp.tile(jnp.concatenate([-sin, sin], axis=1), (1, reps))
    return cos_t, sin_t


def _flat_pad(a):
    flat = a.reshape(1, -1)
    pad = (-flat.shape[1]) % LANES
    return jnp.pad(flat, ((0, 0), (0, pad))) if pad else flat


def kernel(x, norm_mix, norm_ffn, norm_final, conv_w_in, conv_w_conv, conv_w_out, attn_w_qkv, attn_b_qkv, attn_sinks, attn_w_o, attn_b_o, ffn_w_in, ffn_w_conv, ffn_w_down, loss_target, m_norm_mix, m_norm_ffn, m_norm_final, m_conv_w_in, m_conv_w_conv, m_conv_w_out, m_attn_w_qkv, m_attn_b_qkv, m_attn_sinks, m_attn_w_o, m_attn_b_o, m_ffn_w_in, m_ffn_w_conv, m_ffn_w_down, v_norm_mix, v_norm_ffn, v_norm_final, v_conv_w_in, v_conv_w_conv, v_conv_w_out, v_attn_w_qkv, v_attn_b_qkv, v_attn_sinks, v_attn_w_o, v_attn_b_o, v_ffn_w_in, v_ffn_w_conv, v_ffn_w_down):
    b_loc, seq, d = x.shape
    depth = norm_mix.shape[0]
    n_conv, n_attn = conv_w_in.shape[0], attn_w_qkv.shape[0]
    t_all = b_loc * seq
    my_x, my_y, my_c = _my_place()

    me = 4 * my_x + 2 * my_y + my_c

    groups = []
    for i in range(depth):
        j = i // 2
        if i % 2 == 0:
            mix = [("conv_w_in", j, True, conv_w_in[j].T), ("conv_w_out", j, False, conv_w_out[j])]
        else:
            mix = [("attn_w_qkv", j, True, attn_w_qkv[j].T), ("attn_w_o", j, False, attn_w_o[j])]
        groups.append((("mix", i), mix))
        groups.append((("ffn", i), [("ffn_w_in", i, True, ffn_w_in[i].T), ("ffn_w_down", i, False, ffn_w_down[i])]))
    order = [key for key, _ in groups]
    members_of = dict(groups)

    def layout(key):
        offs, o = [], 0
        for _, _, _, shard in members_of[key]:
            n = shard.shape[0]
            o = -(-o // n) * n
            offs.append(o)
            o += n
        return offs, o

    small = jnp.concatenate([_flat_pad(conv_w_conv), _flat_pad(ffn_w_conv), _flat_pad(attn_b_qkv), _flat_pad(attn_b_o)], axis=1)
    (small_g,) = _all_gather([small], "gather_small")

    gather_started = {}

    def start_gather(idx, after):
        if idx >= len(order):
            return 0.0
        key = order[idx]
        offs, total = layout(key)
        pieces, o = [], 0
        for (_, _, _, shard), off in zip(members_of[key], offs):
            if off > o:
                pieces.append(jnp.zeros((off - o, d), shard.dtype))
            pieces.append(shard)
            o = off + shard.shape[0]
        pack = jnp.concatenate(pieces, axis=0).astype(BF16)
        gather_started[key] = _exchange_start(pack, after, False, f"gather_start_{key[0]}_{key[1]}")
        return gather_started[key][4][0, 0]

    weights = {}

    def finish_gather(key, after):
        _, land = _exchange_wait(gather_started[key], after, False, f"gather_wait_{key[0]}_{key[1]}")
        for (wname, layer, _, shard), off in zip(members_of[key], layout(key)[0]):
            weights[(wname, layer)] = _Rows(land, off, shard.shape[0])

    def take_small(o, shape):
        size = shape[0] * shape[1] * shape[2]
        blk = small_g[:, 0, o:o + size].reshape((N_DEV,) + shape)
        return jnp.moveaxis(blk, 0, 2).reshape(shape[0], shape[1], N_DEV * shape[2])

    so = 0
    wc_conv_full = take_small(so, conv_w_conv.shape); so += _flat_pad(conv_w_conv).shape[1]
    wc_ffn_full = take_small(so, ffn_w_conv.shape); so += _flat_pad(ffn_w_conv).shape[1]
    b_qkv_full = take_small(so, (n_attn, 1, attn_b_qkv.shape[1]))[:, 0]; so += _flat_pad(attn_b_qkv).shape[1]
    b_o_full = take_small(so, (n_attn, 1, attn_b_o.shape[1]))[:, 0]

    cos_t, sin_t = _rope_tables(seq)

    xs = [x.reshape(t_all, d)]
    saved = []
    token = start_gather(0, small_g)
    for i in range(depth):
        j = i // 2
        if i > 0:
            token = start_gather(2 * i + 2, xs[-1])
        gain_mix = norm_mix[i][None, :] + token
        finish_gather(("mix", i), gain_mix if i == 0 else xs[-1])
        if i == 0:
            gain_mix = gain_mix + start_gather(1, weights[("conv_w_out", 0)].arr)
        if i % 2 == 0:
            x1, *mix_saved = _fwd_conv_mixer(xs[-1], gain_mix, weights[("conv_w_in", j)], wc_conv_full[j],
                                             weights[("conv_w_out", j)], seq, f"fwd_conv_{i}")
        else:
            qkv = _fwd_qkv(xs[-1], gain_mix, weights[("attn_w_qkv", j)], b_qkv_full[j][None, :], cos_t, sin_t, seq,
                           f"fwd_qkv_{i}")
            x1, o, probs_t, p_sink = _fwd_attention(qkv, xs[-1], attn_sinks[j], weights[("attn_w_o", j)],
                                                    b_o_full[j][None, :], seq, f"fwd_attn_{i}")
            mix_saved = (qkv, o, probs_t, p_sink)
        token = start_gather(2 * i + 3, x1) + (start_gather(2, x1) if i == 0 else 0.0)
        gain_ffn = norm_ffn[i][None, :] + token
        finish_gather(("ffn", i), gain_ffn)
        x2, *ffn_saved = _fwd_ffn(x1, gain_ffn, weights[("ffn_w_in", i)], wc_ffn_full[i], weights[("ffn_w_down", i)],
                                  seq, f"fwd_ffn_{i}")
        saved.append((xs[-1], x1, mix_saved, ffn_saved))
        xs.append(x2)
        token = 0.0

    dx, dg_final, loss_lanes = _final_norm_loss(xs[-1], norm_final[None, :], loss_target.reshape(t_all, d), "loss_head")

    dg_mix, dg_ffn = [None] * depth, [None] * depth
    dwc_conv, dwc_ffn = [None] * n_conv, [None] * depth
    db_qkv, db_o, dsinks = [None] * n_attn, [None] * n_attn, [None] * n_attn
    scatter_started = {}

    def weight_grads(key, operands):
        offs, total = layout(key)
        parts = (N_DEV, total, d)
        for (wname, layer, _, shard), off, (a, b) in zip(members_of[key], offs, operands):
            parts = _tn_matmul(a, b, _Rows(parts, off, shard.shape[0]), f"dw_{wname}_{layer}")
        scatter_started[key] = _exchange_start(parts, operands[0][1], True, f"scatter_start_{key[0]}_{key[1]}")
        return scatter_started[key][4][0, 0]

    token = 0.0
    for i in reversed(range(depth)):
        j = i // 2
        x0, x1, mix_saved, (gate, s_act, uds, act) = saved[i]
        dgu, dwc = _bwd_ffn_inner(dx, gate, s_act, uds, wc_ffn_full[i] + token, weights[("ffn_w_down", i)], seq, f"bwd_ffn_{i}")
        dwc_ffn[i] = dwc[:3]
        dx1, h2, dg_ffn[i] = _bwd_dense_norm(dgu, weights[("ffn_w_in", i)], x1, norm_ffn[i][None, :], dx, f"bwd_ffn_norm_{i}")
        token = weight_grads(("ffn", i), [(dgu, h2), (act, dx)])
        if i % 2 == 0:
            bcv, cc, y = mix_saved
            dbcv, dwc = _bwd_conv_inner(dx1, bcv, cc, wc_conv_full[j] + token, weights[("conv_w_out", j)], seq, f"bwd_conv_{i}")
            dwc_conv[j] = dwc[:3]
            dx, h, dg_mix[i] = _bwd_dense_norm(dbcv, weights[("conv_w_in", j)], x0, norm_mix[i][None, :], dx1,
                                               f"bwd_conv_norm_{i}")
            token = weight_grads(("mix", i), [(dbcv, h), (y, dx1)])
        else:
            qkv, o, probs_t, p_sink = mix_saved
            dqkv, dsk, dbq, dbo = _bwd_attention_inner(dx1, qkv, probs_t, p_sink + token, weights[("attn_w_o", j)], cos_t,
                                                       sin_t, seq, f"bwd_attn_{i}")
            dsinks[j], db_qkv[j], db_o[j] = dsk[0:1, :attn_sinks.shape[1]], dbq, dbo
            dx, h, dg_mix[i] = _bwd_dense_norm(dqkv, weights[("attn_w_qkv", j)], x0, norm_mix[i][None, :], dx1,
                                               f"bwd_attn_norm_{i}")
            token = weight_grads(("mix", i), [(dqkv, h), (o, dx1)])
    grad_x = dx.reshape(b_loc, seq, d)

    reduced = {}

    def finish_scatter(key, after):
        _, land = _exchange_wait(scatter_started[key], after, True, f"scatter_wait_{key[0]}_{key[1]}")
        total = _sum_slots(land, f"scatter_sum_{key[0]}_{key[1]}")
        for (wname, layer, transposed, shard), off in zip(members_of[key], layout(key)[0]):
            rows = total[off:off + shard.shape[0]]
            reduced[(wname, layer)] = rows.T if transposed else rows

    last_key = order[0]
    for key in reversed(order[1:]):
        finish_scatter(key, dx)

    small_parts = [jnp.concatenate(dg_mix, axis=0), jnp.concatenate(dg_ffn, axis=0), dg_final,
                   jnp.stack(dwc_conv), jnp.stack(dwc_ffn), jnp.concatenate(db_qkv, axis=0), jnp.concatenate(db_o, axis=0),
                   jnp.concatenate(dsinks, axis=0), loss_lanes]
    flats = [_flat_pad(p) for p in small_parts]
    bounds = []
    so = 0
    for fl in flats:
        bounds.append((so, so + fl.shape[1]))
        so += fl.shape[1]
    small_rows = so // LANES
    pad_rows = (-small_rows) % SUBLANES
    part_small = jnp.pad(jnp.concatenate(flats, axis=1).reshape(small_rows, LANES), ((0, pad_rows), (0, 0)))
    loss_rows = (bounds[-1][0] // LANES, bounds[-1][1] // LANES)
    summed, loss_tile = _all_reduce_small(part_small, loss_rows, "reduce_small")
    summed = summed.reshape(1, -1)

    def small_grad(k, shape):
        lo = bounds[k][0]
        size = 1
        for s_ in shape:
            size *= s_
        return summed[0, lo:lo + size].reshape(shape)

    def my_cols(full, n_local):
        return lax.dynamic_slice_in_dim(full, me * n_local, n_local, axis=full.ndim - 1)

    g_norm_mix = small_grad(0, norm_mix.shape)
    g_norm_ffn = small_grad(1, norm_ffn.shape)
    g_norm_final = small_grad(2, norm_final.shape)
    g_conv_w_conv = my_cols(small_grad(3, (n_conv, 3, d)), conv_w_conv.shape[2])
    g_ffn_w_conv = my_cols(small_grad(4, (depth, 3, ffn_w_conv.shape[2] * N_DEV)), ffn_w_conv.shape[2])
    g_attn_b_qkv = my_cols(small_grad(5, (n_attn, attn_b_qkv.shape[1] * N_DEV)), attn_b_qkv.shape[1])
    g_attn_b_o = my_cols(small_grad(6, (n_attn, d)), attn_b_o.shape[1])
    g_attn_sinks = small_grad(7, attn_sinks.shape)
    loss = loss_tile[0, 0]

    def big_grad(wname, n_layers):
        return jnp.stack([reduced[(wname, layer)] for layer in range(n_layers)])

    grads = {
        "norm_mix": g_norm_mix, "norm_ffn": g_norm_ffn, "norm_final": g_norm_final, "conv_w_conv": g_conv_w_conv,
        "attn_w_qkv": big_grad("attn_w_qkv", n_attn), "attn_b_qkv": g_attn_b_qkv, "attn_sinks": g_attn_sinks,
        "attn_w_o": big_grad("attn_w_o", n_attn), "attn_b_o": g_attn_b_o,
        "ffn_w_in": big_grad("ffn_w_in", depth), "ffn_w_conv": g_ffn_w_conv, "ffn_w_down": big_grad("ffn_w_down", depth),
    }
    params = {
        "norm_mix": (norm_mix, m_norm_mix, v_norm_mix), "norm_ffn": (norm_ffn, m_norm_ffn, v_norm_ffn),
        "norm_final": (norm_final, m_norm_final, v_norm_final), "conv_w_in": (conv_w_in, m_conv_w_in, v_conv_w_in),
        "conv_w_conv": (conv_w_conv, m_conv_w_conv, v_conv_w_conv), "conv_w_out": (conv_w_out, m_conv_w_out, v_conv_w_out),
        "attn_w_qkv": (attn_w_qkv, m_attn_w_qkv, v_attn_w_qkv), "attn_b_qkv": (attn_b_qkv, m_attn_b_qkv, v_attn_b_qkv),
        "attn_sinks": (attn_sinks, m_attn_sinks, v_attn_sinks), "attn_w_o": (attn_w_o, m_attn_w_o, v_attn_w_o),
        "attn_b_o": (attn_b_o, m_attn_b_o, v_attn_b_o), "ffn_w_in": (ffn_w_in, m_ffn_w_in, v_ffn_w_in),
        "ffn_w_conv": (ffn_w_conv, m_ffn_w_conv, v_ffn_w_conv), "ffn_w_down": (ffn_w_down, m_ffn_w_down, v_ffn_w_down),
    }
    names = list(params)
    updates = {}

    def update(wname):
        w, m, v = params[wname]
        updates[wname] = _adamw_nd(w, grads[wname], m, v, f"adamw_{wname}")

    last_names = sorted({wname for wname, _, _, _ in members_of[last_key]})
    for wname in names:
        if wname not in last_names:
            update(wname)
    finish_scatter(last_key, updates["ffn_w_in"][0])
    for wname in last_names:
        grads[wname] = big_grad(wname, params[wname][0].shape[0])
        update(wname)
    return (loss, grad_x, *[grads[wname] for wname in names], *[updates[wname][0] for wname in names],
            *[updates[wname][1] for wname in names], *[updates[wname][2] for wname in names])
```

```python
from typing import NamedTuple

import jax
import jax.numpy as jnp
from jax import lax
from jax.experimental import pallas as pl
from jax.experimental.pallas import tpu as pltpu

F32 = jnp.float32
BF16 = jnp.bfloat16
SDS = jax.ShapeDtypeStruct
MESH = pl.DeviceIdType.MESH
ANY = pl.BlockSpec(memory_space=pl.ANY)

N_DEV = 8
EPS = 1e-5
HEAD_DIM = 64
GROUP = 4
WINDOW = 128
ROPE_THETA = 10000.0
ADAM_LR, ADAM_B1, ADAM_B2, ADAM_EPS, ADAM_WD, ADAM_STEP = 0.001, 0.9, 0.999, 1e-08, 0.01, 10

V7X_VMEM_BYTES = 64 * 1024 * 1024
VMEM_LIMIT_BYTES = V7X_VMEM_BYTES - 8 * 1024 * 1024
LANES = 128
SUBLANES = 8
TOKEN_TILE = 512
TN_TOKEN_TILE = 2048
MASKED_SCORE = -1e30


def _cparams(n_axes=1):
    return pltpu.CompilerParams(dimension_semantics=("arbitrary",) * n_axes, vmem_limit_bytes=VMEM_LIMIT_BYTES)


def _resident(shape):
    zeros = (0,) * len(shape)
    return pl.BlockSpec(shape, lambda *_: zeros, pipeline_mode=pl.Buffered(1))


class _Rows(NamedTuple):
    arr: jax.Array
    off: int
    n: int


def _rows_spec(w):
    assert w.off % w.n == 0
    block = w.off // w.n
    return pl.BlockSpec((N_DEV, w.n, w.arr.shape[2]), lambda *_: (0, block, 0), pipeline_mode=pl.Buffered(1))


def _mat(ref):
    v = ref[...]
    return v.reshape(v.shape[0] * v.shape[1], v.shape[2])


def _token_tile(seq):
    return min(TOKEN_TILE, seq // 2)


def _largest_divisor(m, cap, mult):
    best = None
    for d in range(mult, min(m, cap) + 1, mult):
        if m % d == 0:
            best = d
    return m if best is None else best


def _nt(a, b):
    return lax.dot_general(a, b, (((1,), (1,)), ((), ())), preferred_element_type=F32)


def _nn(a, b):
    return lax.dot_general(a, b, (((1,), (0,)), ((), ())), preferred_element_type=F32)


def _tn(a, b):
    return lax.dot_general(a, b, (((0,), (0,)), ((), ())), preferred_element_type=F32)


def _rms_parts(xv):
    r = lax.rsqrt(jnp.mean(xv * xv, axis=-1, keepdims=True) + EPS)
    return r, xv * r


def _rms_backward(dh, xh, r, gain, dres):
    u = dh * gain
    return dres + r * (u - xh * jnp.mean(u * xh, axis=-1, keepdims=True))


def _shifted_rows(xv, edge, k, down):
    n = xv.shape[0]
    row = lax.broadcasted_iota(jnp.int32, edge.shape, 0)
    if down:
        rolled = pltpu.roll(xv, k, 0)
        head = jnp.where(row < k, pltpu.roll(edge, k, 0), rolled[0:SUBLANES])
        return jnp.concatenate([head, rolled[SUBLANES:]], axis=0)
    rolled = pltpu.roll(xv, n - k, 0)
    tail = jnp.where(row >= SUBLANES - k, pltpu.roll(edge, SUBLANES - k, 0), rolled[n - SUBLANES:])
    return jnp.concatenate([rolled[:n - SUBLANES], tail], axis=0)


def _causal_conv3(edge_ref, xv, w_ref):
    before = edge_ref[...]
    y = (w_ref[2:3, :] * xv + w_ref[1:2, :] * _shifted_rows(xv, before, 1, True)
         + w_ref[0:1, :] * _shifted_rows(xv, before, 2, True))
    edge_ref[...] = xv[xv.shape[0] - SUBLANES:, :]
    return y


def _sigmoid(z):
    return 1.0 / (1.0 + jnp.exp(-z))


def _fwd_conv_mixer(x, gain, w_in_t, w_conv, w_out, seq, name):
    t_all, d = x.shape
    tt = _token_tile(seq)
    tps = seq // tt

    def body(x_ref, g_ref, win_ref, wc_ref, wout_ref, x1_ref, bcv_ref, cc_ref, y_ref, ext_ref):
        i = pl.program_id(0)
        xv = x_ref[...]
        r, xh = _rms_parts(xv)
        h = (xh * g_ref[...]).astype(BF16)
        bcv = _nt(h, _mat(win_ref))
        bcv_ref[...] = bcv.astype(BF16)

        @pl.when(i % tps == 0)
        def _():
            ext_ref[...] = jnp.zeros_like(ext_ref)

        cc = _causal_conv3(ext_ref, bcv[:, d:2 * d] * bcv[:, 2 * d:], wc_ref)
        cc_ref[...] = cc.astype(BF16)
        y = (bcv[:, :d] * cc).astype(BF16)
        y_ref[...] = y
        x1_ref[...] = xv + _nn(y, _mat(wout_ref))

    tile = pl.BlockSpec((tt, d), lambda i: (i, 0))
    return pl.pallas_call(
        body, name=name, grid=(t_all // tt,),
        in_specs=[tile, _resident((1, d)), _rows_spec(w_in_t), _resident((3, d)), _rows_spec(w_out)],
        out_specs=[tile, pl.BlockSpec((tt, 3 * d), lambda i: (i, 0)), tile, tile],
        out_shape=[SDS((t_all, d), F32), SDS((t_all, 3 * d), BF16), SDS((t_all, d), BF16), SDS((t_all, d), BF16)],
        scratch_shapes=[pltpu.VMEM((SUBLANES, d), F32)],
        compiler_params=_cparams(),
    )(x, gain, w_in_t.arr, w_conv, w_out.arr)


def _fwd_ffn(x, gain, w_in_t, w_conv, w_down, seq, name):
    t_all, d = x.shape
    f = w_down.n * N_DEV
    tt = _token_tile(seq) // 2
    tps = seq // tt

    def body(x_ref, g_ref, win_ref, wc_ref, wd_ref, x2_ref, gate_ref, s_ref, uds_ref, a_ref, ext_ref):
        i = pl.program_id(0)
        xv = x_ref[...]
        r, xh = _rms_parts(xv)
        h = (xh * g_ref[...]).astype(BF16)
        gu = _nt(h, _mat(win_ref))
        gate = gu[:, :f]
        u = gu[:, f:]
        gate_ref[...] = gate.astype(BF16)

        @pl.when(i % tps == 0)
        def _():
            ext_ref[...] = jnp.zeros_like(ext_ref)

        gc = _causal_conv3(ext_ref, gate, wc_ref)
        sig = _sigmoid(gc)
        s = gc * sig
        s_ref[...] = s.astype(BF16)
        uds_ref[...] = (u * (sig * (1.0 + gc * (1.0 - sig)))).astype(BF16)
        a = (s * u).astype(BF16)
        a_ref[...] = a
        x2_ref[...] = xv + _nn(a, _mat(wd_ref))

    wide = pl.BlockSpec((tt, f), lambda i: (i, 0))
    return pl.pallas_call(
        body, name=name, grid=(t_all // tt,),
        in_specs=[pl.BlockSpec((tt, d), lambda i: (i, 0)), _resident((1, d)), _rows_spec(w_in_t),
                  _resident((3, f)), _rows_spec(w_down)],
        out_specs=[pl.BlockSpec((tt, d), lambda i: (i, 0)), wide, wide, wide, wide],
        out_shape=[SDS((t_all, d), F32)] + [SDS((t_all, f), BF16)] * 4,
        scratch_shapes=[pltpu.VMEM((SUBLANES, f), F32)],
        compiler_params=_cparams(),
    )(x, gain, w_in_t.arr, w_conv, w_down.arr)


def _rope_partner(xs, lane_lo):
    return jnp.where(lane_lo, pltpu.roll(xs, LANES - HEAD_DIM // 2, 1), pltpu.roll(xs, HEAD_DIM // 2, 1))


def _fwd_qkv(x, gain, w_qkv_t, b_qkv, cos_t, sin_t, seq, name):
    t_all, d = x.shape
    width = w_qkv_t.n * N_DEV
    kvw = (width - d) // 2
    tt = _token_tile(seq)
    tps = seq // tt
    scale = HEAD_DIM ** -0.5

    def body(x_ref, g_ref, w_ref, b_ref, cos_ref, sin_ref, qkv_ref):
        xv = x_ref[...]
        r, xh = _rms_parts(xv)
        h = (xh * g_ref[...]).astype(BF16)
        qkv = _nt(h, _mat(w_ref)) + b_ref[...]
        cosv = cos_ref[...]
        sinv = sin_ref[...]
        lane_lo = (lax.broadcasted_iota(jnp.int32, (tt, LANES), 1) % HEAD_DIM) < HEAD_DIM // 2
        for s in range((d + kvw) // LANES):
            xs = qkv[:, s * LANES:(s + 1) * LANES]
            roped = xs * cosv + _rope_partner(xs, lane_lo) * sinv
            if s * LANES < d:
                roped = roped * scale
            qkv_ref[:, s * LANES:(s + 1) * LANES] = roped.astype(BF16)
        qkv_ref[:, d + kvw:] = qkv[:, d + kvw:].astype(BF16)

    return pl.pallas_call(
        body, name=name, grid=(t_all // tt,),
        in_specs=[pl.BlockSpec((tt, d), lambda i: (i, 0)), _resident((1, d)), _rows_spec(w_qkv_t),
                  _resident((1, width)), pl.BlockSpec((tt, LANES), lambda i: (i % tps, 0)),
                  pl.BlockSpec((tt, LANES), lambda i: (i % tps, 0))],
        out_specs=pl.BlockSpec((tt, width), lambda i: (i, 0)),
        out_shape=SDS((t_all, width), BF16),
        compiler_params=_cparams(),
    )(x, gain, w_qkv_t.arr, b_qkv, cos_t, sin_t)


def _stack_heads(ref, row0, kh):
    return jnp.concatenate(
        [ref[row0:row0 + WINDOW, (kh * GROUP + g) * HEAD_DIM:(kh * GROUP + g + 1) * HEAD_DIM] for g in range(GROUP)],
        axis=0)


def _band_bias():
    r = lax.broadcasted_iota(jnp.int32, (WINDOW, 2 * WINDOW), 0)
    j = lax.broadcasted_iota(jnp.int32, (WINDOW, 2 * WINDOW), 1)
    base = (j > r) & (j <= r + WINDOW)
    return jnp.where(base, 0.0, MASKED_SCORE), jnp.where(base & (j >= WINDOW), 0.0, MASKED_SCORE)


def _fwd_attention(qkv, x, sinks, w_o, b_o, seq, name):
    t_all, d = x.shape
    width = qkv.shape[1]
    kvw = (width - d) // 2
    n_kv = kvw // HEAD_DIM
    tt = _token_tile(seq)
    tps = seq // tt
    nblk = tt // WINDOW

    def body(sink_ref, qkv_ref, kvp_ref, x_ref, wo_ref, bo_ref, x1_ref, o_ref, p_ref, psink_ref,
             kvext_ref, oscr_ref, bias_ref, s_ref):
        i = pl.program_id(0)

        @pl.when(i == 0)
        def _():
            base, first = _band_bias()
            bias_ref[0], bias_ref[1] = base.T, first.T

        kvext_ref[0:WINDOW, :] = kvp_ref[...]
        kvext_ref[WINDOW:, :] = qkv_ref[:, d:]
        at_seq_start = (i % tps == 0).astype(jnp.int32)
        steps = [(n, kh) for n in range(nblk) for kh in range(n_kv)]

        def scores(step):
            n, kh = steps[step]
            qs = _stack_heads(qkv_ref, n * WINDOW, kh)
            kb = kvext_ref[n * WINDOW:(n + 2) * WINDOW, kh * HEAD_DIM:(kh + 1) * HEAD_DIM]
            s_ref[step % 2] = _nt(kb, qs)

        scores(0)
        for step, (n, kh) in enumerate(steps):
            buf = step % 2
            if step + 1 < len(steps):
                scores(step + 1)
            vb = kvext_ref[n * WINDOW:(n + 2) * WINDOW, kvw + kh * HEAD_DIM:kvw + (kh + 1) * HEAD_DIM]
            bias = bias_ref[at_seq_start if n == 0 else 0]
            for g in range(GROUP):
                cols = slice(g * WINDOW, (g + 1) * WINDOW)
                sink = sink_ref[kh * GROUP + g]
                sv = s_ref[buf, :, cols] + bias
                m = jnp.maximum(jnp.max(sv, axis=0, keepdims=True), sink)
                p = jnp.exp(sv - m)
                e_sink = jnp.exp(sink - m)
                inv = 1.0 / (jnp.sum(p, axis=0, keepdims=True) + e_sink)
                p_ref[n, kh, :, cols] = (p * inv).astype(BF16)
                psink_ref[n, kh, g:g + 1, :] = e_sink * inv
            o_s = _tn(vb, p_ref[n, kh]).T
            for g in range(GROUP):
                hd = kh * GROUP + g
                oscr_ref[n * WINDOW:(n + 1) * WINDOW, hd * HEAD_DIM:(hd + 1) * HEAD_DIM] = (
                    o_s[g * WINDOW:(g + 1) * WINDOW].astype(BF16))
        o = oscr_ref[...]
        o_ref[...] = o
        x1_ref[...] = x_ref[...] + _nn(o, _mat(wo_ref)) + bo_ref[...]

    kv_blocks = tt // WINDOW
    return pl.pallas_call(
        body, name=name, grid=(t_all // tt,),
        in_specs=[pl.BlockSpec(memory_space=pltpu.SMEM),
                  pl.BlockSpec((tt, width), lambda i: (i, 0)),
                  pl.BlockSpec((WINDOW, 2 * kvw), lambda i: (jnp.maximum(i * kv_blocks - 1, 0), d // (2 * kvw))),
                  pl.BlockSpec((tt, d), lambda i: (i, 0)), _rows_spec(w_o), _resident((1, d))],
        out_specs=[pl.BlockSpec((tt, d), lambda i: (i, 0)), pl.BlockSpec((tt, d), lambda i: (i, 0)),
                   pl.BlockSpec((nblk, n_kv, 2 * WINDOW, GROUP * WINDOW), lambda i: (i, 0, 0, 0)),
                   pl.BlockSpec((nblk, n_kv, GROUP, WINDOW), lambda i: (i, 0, 0, 0))],
        out_shape=[SDS((t_all, d), F32), SDS((t_all, d), BF16),
                   SDS((t_all // WINDOW, n_kv, 2 * WINDOW, GROUP * WINDOW), BF16),
                   SDS((t_all // WINDOW, n_kv, GROUP, WINDOW), F32)],
        scratch_shapes=[pltpu.VMEM((tt + WINDOW, 2 * kvw), BF16), pltpu.VMEM((tt, d), BF16),
                        pltpu.VMEM((2, 2 * WINDOW, WINDOW), F32), pltpu.VMEM((2, 2 * WINDOW, GROUP * WINDOW), F32)],
        compiler_params=_cparams(),
    )(sinks, qkv, qkv, x, w_o.arr, b_o)


def _final_norm_loss(x, gain, target, name):
    t_all, d = x.shape
    tt = min(TOKEN_TILE, t_all)

    def body(x_ref, g_ref, t_ref, dx_ref, dg_ref, loss_ref):
        i = pl.program_id(0)
        xv = x_ref[...]
        r, xh = _rms_parts(xv)
        gain_v = g_ref[...]
        e = xh * gain_v - t_ref[...]
        dy = e * (1.0 / d)
        dx_ref[...] = _rms_backward(dy, xh, r, gain_v, 0.0)

        @pl.when(i == 0)
        def _():
            dg_ref[...] = jnp.zeros_like(dg_ref)
            loss_ref[...] = jnp.zeros_like(loss_ref)

        dg_ref[...] += jnp.sum(dy * xh, axis=0, keepdims=True)
        loss_ref[...] += (0.5 / d) * jnp.sum(e * e, axis=0, keepdims=True)

    return pl.pallas_call(
        body, name=name, grid=(t_all // tt,),
        in_specs=[pl.BlockSpec((tt, d), lambda i: (i, 0)), _resident((1, d)), pl.BlockSpec((tt, d), lambda i: (i, 0))],
        out_specs=[pl.BlockSpec((tt, d), lambda i: (i, 0)), pl.BlockSpec((1, d), lambda i: (0, 0)),
                   pl.BlockSpec((1, d), lambda i: (0, 0))],
        out_shape=[SDS((t_all, d), F32), SDS((1, d), F32), SDS((1, d), F32)],
        compiler_params=_cparams(),
    )(x, gain, target)


def _bwd_ffn_inner(dx2, gate, s_act, uds, w_conv, w_down, seq, name):
    t_all, d = dx2.shape
    f = w_down.n * N_DEV
    tt = _token_tile(seq)
    tps = seq // tt
    nt = t_all // tt

    def body(dx_ref, g_ref, s_ref, uds_ref, wc_ref, wd_ref, dgu_ref, dwc_ref, aext_ref, da_ref):
        i = pl.program_id(0)
        ti = nt - 1 - i
        da_ref[...] = _nt(dx_ref[...].astype(BF16), _mat(wd_ref))

        @pl.when(ti % tps == tps - 1)
        def _():
            aext_ref[...] = jnp.zeros_like(aext_ref)

        @pl.when(i == 0)
        def _():
            dwc_ref[...] = jnp.zeros_like(dwc_ref)

        for c in range(f // LANES):
            cols = slice(c * LANES, (c + 1) * LANES)
            da = da_ref[:, cols]
            g = g_ref[:, cols].astype(F32)
            dgc = da * uds_ref[:, cols].astype(F32)
            after = aext_ref[:, cols]
            sh1 = _shifted_rows(dgc, after, 1, False)
            sh2 = _shifted_rows(dgc, after, 2, False)
            aext_ref[:, cols] = dgc[0:SUBLANES, :]
            dg = wc_ref[2:3, cols] * dgc + wc_ref[1:2, cols] * sh1 + wc_ref[0:1, cols] * sh2
            dgu_ref[:, cols] = dg.astype(BF16)
            dgu_ref[:, f + c * LANES:f + (c + 1) * LANES] = (da * s_ref[:, cols].astype(F32)).astype(BF16)
            dwc_ref[0:1, cols] += jnp.sum(g * sh2, axis=0, keepdims=True)
            dwc_ref[1:2, cols] += jnp.sum(g * sh1, axis=0, keepdims=True)
            dwc_ref[2:3, cols] += jnp.sum(g * dgc, axis=0, keepdims=True)

    rev = lambda i: (nt - 1 - i, 0)
    return pl.pallas_call(
        body, name=name, grid=(nt,),
        in_specs=[pl.BlockSpec((tt, d), rev)] + [pl.BlockSpec((tt, f), rev)] * 3 + [_resident((3, f)), _rows_spec(w_down)],
        out_specs=[pl.BlockSpec((tt, 2 * f), rev), pl.BlockSpec((8, f), lambda i: (0, 0))],
        out_shape=[SDS((t_all, 2 * f), BF16), SDS((8, f), F32)],
        scratch_shapes=[pltpu.VMEM((SUBLANES, f), F32), pltpu.VMEM((tt, f), F32)],
        compiler_params=_cparams(),
    )(dx2, gate, s_act, uds, w_conv, w_down.arr)


def _bwd_conv_inner(dx1, bcv, cc, w_conv, w_out, seq, name):
    t_all, d = dx1.shape
    tt = _token_tile(seq)
    tps = seq // tt
    nt = t_all // tt

    def body(dx_ref, bcv_ref, cc_ref, wc_ref, wout_ref, dbcv_ref, dwc_ref, aext_ref, dy_ref):
        i = pl.program_id(0)
        ti = nt - 1 - i
        dy_ref[...] = _nt(dx_ref[...].astype(BF16), _mat(wout_ref))

        @pl.when(ti % tps == tps - 1)
        def _():
            aext_ref[...] = jnp.zeros_like(aext_ref)

        @pl.when(i == 0)
        def _():
            dwc_ref[...] = jnp.zeros_like(dwc_ref)

        for s in range(d // LANES):
            cols = slice(s * LANES, (s + 1) * LANES)
            ccols = slice(d + s * LANES, d + (s + 1) * LANES)
            vcols = slice(2 * d + s * LANES, 2 * d + (s + 1) * LANES)
            dy = dy_ref[:, cols]
            c = bcv_ref[:, ccols].astype(F32)
            v = bcv_ref[:, vcols].astype(F32)
            cv = c * v
            dcc = dy * bcv_ref[:, cols].astype(F32)
            after = aext_ref[:, cols]
            sh1 = _shifted_rows(dcc, after, 1, False)
            sh2 = _shifted_rows(dcc, after, 2, False)
            aext_ref[:, cols] = dcc[0:SUBLANES, :]
            dcv = wc_ref[2:3, cols] * dcc + wc_ref[1:2, cols] * sh1 + wc_ref[0:1, cols] * sh2
            dbcv_ref[:, cols] = (dy * cc_ref[:, cols].astype(F32)).astype(BF16)
            dbcv_ref[:, ccols] = (dcv * v).astype(BF16)
            dbcv_ref[:, vcols] = (dcv * c).astype(BF16)
            dwc_ref[0:1, cols] += jnp.sum(cv * sh2, axis=0, keepdims=True)
            dwc_ref[1:2, cols] += jnp.sum(cv * sh1, axis=0, keepdims=True)
            dwc_ref[2:3, cols] += jnp.sum(cv * dcc, axis=0, keepdims=True)

    return pl.pallas_call(
        body, name=name, grid=(nt,),
        in_specs=[pl.BlockSpec((tt, d), lambda i: (nt - 1 - i, 0)),
                  pl.BlockSpec((tt, 3 * d), lambda i: (nt - 1 - i, 0)),
                  pl.BlockSpec((tt, d), lambda i: (nt - 1 - i, 0)),
                  _resident((3, d)), _rows_spec(w_out)],
        out_specs=[pl.BlockSpec((tt, 3 * d), lambda i: (nt - 1 - i, 0)), pl.BlockSpec((8, d), lambda i: (0, 0))],
        out_shape=[SDS((t_all, 3 * d), BF16), SDS((8, d), F32)],
        scratch_shapes=[pltpu.VMEM((SUBLANES, d), F32), pltpu.VMEM((tt, d), F32)],
        compiler_params=_cparams(),
    )(dx1, bcv, cc, w_conv, w_out.arr)


def _bwd_attention_inner(dx1, qkv, probs_t, p_sink, w_o, cos_t, sin_t, seq, name):
    t_all, d = dx1.shape
    width = qkv.shape[1]
    kvw = (width - d) // 2
    n_kv = kvw // HEAD_DIM
    tt = _token_tile(seq)
    tps = seq // tt
    nt = t_all // tt
    nblk = tt // WINDOW
    scale = HEAD_DIM ** -0.5

    def body(dx_ref, qkv_ref, kvp_ref, p_ref, psink_ref, cos_ref, sin_ref, wo_ref,
             dqkv_ref, dsink_ref, dbqkv_ref, dbo_ref,
             kvext_ref, dkvext_ref, carry_ref, dq_ref, do_ref, dp_ref, ds_ref):
        i = pl.program_id(0)
        dxv = dx_ref[...]
        do_ref[...] = _nt(dxv.astype(BF16), _mat(wo_ref)).astype(BF16)
        kvext_ref[0:WINDOW, :] = kvp_ref[...]
        kvext_ref[WINDOW:, :] = qkv_ref[:, d:]
        dkvext_ref[...] = jnp.zeros_like(dkvext_ref)

        @pl.when(i == 0)
        def _():
            carry_ref[...] = jnp.zeros_like(carry_ref)
            dsink_ref[...] = jnp.zeros_like(dsink_ref)
            dbqkv_ref[...] = jnp.zeros_like(dbqkv_ref)
            dbo_ref[...] = jnp.zeros_like(dbo_ref)

        head_lane = lax.broadcasted_iota(jnp.int32, (1, LANES), 1)
        dsink = jnp.zeros((1, LANES), F32)
        for n in range(nblk):
            for kh in range(n_kv):
                buf = (n * n_kv + kh) % 2
                qs = _stack_heads(qkv_ref, n * WINDOW, kh)
                dos = _stack_heads(do_ref, n * WINDOW, kh)
                kcols = slice(kh * HEAD_DIM, (kh + 1) * HEAD_DIM)
                vcols = slice(kvw + kh * HEAD_DIM, kvw + (kh + 1) * HEAD_DIM)
                band = slice(n * WINDOW, (n + 2) * WINDOW)
                kb = kvext_ref[band, kcols]
                vb = kvext_ref[band, vcols]
                dp_ref[buf] = _nt(vb, dos)
                for g in range(GROUP):
                    hd = kh * GROUP + g
                    cols = slice(g * WINDOW, (g + 1) * WINDOW)
                    probs = p_ref[n, kh, :, cols].astype(F32)
                    dp = dp_ref[buf, :, cols]
                    dsum = jnp.sum(probs * dp, axis=0, keepdims=True)
                    ds_ref[buf, :, cols] = (probs * (dp - dsum)).astype(BF16)
                    dsink = dsink - jnp.where(head_lane == hd, jnp.sum(psink_ref[n, kh, g:g + 1, :] * dsum), 0.0)
                ds_t = ds_ref[buf]
                dkvext_ref[band, vcols] += _nn(p_ref[n, kh], dos)
                dkvext_ref[band, kcols] += _nn(ds_t, qs)
                dq_s = _tn(kb, ds_t).T
                for g in range(GROUP):
                    hd = kh * GROUP + g
                    dq_ref[n * WINDOW:(n + 1) * WINDOW, hd * HEAD_DIM:(hd + 1) * HEAD_DIM] = dq_s[g * WINDOW:(g + 1) * WINDOW]
        dsink_ref[0:1, :] += dsink
        dkvext_ref[tt:tt + WINDOW, :] += carry_ref[...]
        carry_ref[...] = dkvext_ref[0:WINDOW, :]

        cosv = cos_ref[...]
        sinv = sin_ref[...]
        lane_lo = (lax.broadcasted_iota(jnp.int32, (tt, LANES), 1) % HEAD_DIM) < HEAD_DIM // 2
        for s in range((d + kvw) // LANES):
            if s * LANES < d:
                dy = dq_ref[:, s * LANES:(s + 1) * LANES] * scale
            else:
                dy = dkvext_ref[WINDOW:, s * LANES - d:(s + 1) * LANES - d]
            dpre = dy * cosv - _rope_partner(dy, lane_lo) * sinv
            dqkv_ref[:, s * LANES:(s + 1) * LANES] = dpre.astype(BF16)
            dbqkv_ref[0:1, s * LANES:(s + 1) * LANES] += jnp.sum(dpre, axis=0, keepdims=True)
        dv = dkvext_ref[WINDOW:, kvw:]
        dqkv_ref[:, d + kvw:] = dv.astype(BF16)
        dbqkv_ref[0:1, d + kvw:] += jnp.sum(dv, axis=0, keepdims=True)
        dbo_ref[...] += jnp.sum(dxv, axis=0, keepdims=True)

    kv_blocks = tt // WINDOW
    return pl.pallas_call(
        body, name=name, grid=(nt,),
        in_specs=[pl.BlockSpec((tt, d), lambda i: (nt - 1 - i, 0)),
                  pl.BlockSpec((tt, width), lambda i: (nt - 1 - i, 0)),
                  pl.BlockSpec((WINDOW, 2 * kvw), lambda i: (jnp.maximum((nt - 1 - i) * kv_blocks - 1, 0), d // (2 * kvw))),
                  pl.BlockSpec((nblk, n_kv, 2 * WINDOW, GROUP * WINDOW), lambda i: (nt - 1 - i, 0, 0, 0)),
                  pl.BlockSpec((nblk, n_kv, GROUP, WINDOW), lambda i: (nt - 1 - i, 0, 0, 0)),
                  pl.BlockSpec((tt, LANES), lambda i: ((nt - 1 - i) % tps, 0)),
                  pl.BlockSpec((tt, LANES), lambda i: ((nt - 1 - i) % tps, 0)),
                  _rows_spec(w_o)],
        out_specs=[pl.BlockSpec((tt, width), lambda i: (nt - 1 - i, 0)), pl.BlockSpec((8, LANES), lambda i: (0, 0)),
                   pl.BlockSpec((1, width), lambda i: (0, 0)), pl.BlockSpec((1, d), lambda i: (0, 0))],
        out_shape=[SDS((t_all, width), BF16), SDS((8, LANES), F32), SDS((1, width), F32), SDS((1, d), F32)],
        scratch_shapes=[pltpu.VMEM((tt + WINDOW, 2 * kvw), BF16), pltpu.VMEM((tt + WINDOW, 2 * kvw), F32),
                        pltpu.VMEM((WINDOW, 2 * kvw), F32), pltpu.VMEM((tt, d), F32), pltpu.VMEM((tt, d), BF16),
                        pltpu.VMEM((2, 2 * WINDOW, GROUP * WINDOW), F32), pltpu.VMEM((2, 2 * WINDOW, GROUP * WINDOW), BF16)],
        compiler_params=_cparams(),
    )(dx1, qkv, qkv, probs_t, p_sink, cos_t, sin_t, w_o.arr)


def _bwd_dense_norm(dy, w_t, x, gain, dres, name):
    t_all, d = x.shape
    n = dy.shape[1]
    tt = min(TOKEN_TILE, t_all)

    def body(dy_ref, w_ref, x_ref, g_ref, dres_ref, dx_ref, h_ref, dg_ref):
        i = pl.program_id(0)
        dh = _nn(dy_ref[...], _mat(w_ref))
        r, xh = _rms_parts(x_ref[...])
        gain_v = g_ref[...]
        h_ref[...] = (xh * gain_v).astype(BF16)
        dx_ref[...] = _rms_backward(dh, xh, r, gain_v, dres_ref[...])

        @pl.when(i == 0)
        def _():
            dg_ref[...] = jnp.zeros_like(dg_ref)

        dg_ref[...] += jnp.sum(dh * xh, axis=0, keepdims=True)

    return pl.pallas_call(
        body, name=name, grid=(t_all // tt,),
        in_specs=[pl.BlockSpec((tt, n), lambda i: (i, 0)), _rows_spec(w_t), pl.BlockSpec((tt, d), lambda i: (i, 0)),
                  _resident((1, d)), pl.BlockSpec((tt, d), lambda i: (i, 0))],
        out_specs=[pl.BlockSpec((tt, d), lambda i: (i, 0)), pl.BlockSpec((tt, d), lambda i: (i, 0)),
                   pl.BlockSpec((1, d), lambda i: (0, 0))],
        out_shape=[SDS((t_all, d), F32), SDS((t_all, d), BF16), SDS((1, d), F32)],
        compiler_params=_cparams(),
    )(dy, w_t.arr, x, gain, dres)


def _tn_matmul(a, b, dest, name):
    t_all, m = a.shape
    d = b.shape[1]
    n = dest.n
    assert m == N_DEV * n and dest.off % n == 0
    k = max(kk for kk in (1, 2, 4, 8) if kk * n <= max(n, 1536))
    tm = k * n
    tt = min(TN_TOKEN_TILE, t_all)
    n_t = t_all // tt
    fresh = not hasattr(dest.arr, "dtype")

    def body(a_ref, b_ref, *rest):
        o_ref, acc_ref = rest[-2:]
        t = pl.program_id(1)

        @pl.when(t == 0)
        def _():
            acc_ref[...] = jnp.zeros_like(acc_ref)

        acc_ref[...] += _tn(a_ref[...], b_ref[...].astype(BF16))

        @pl.when(t == n_t - 1)
        def _():
            o_ref[...] = acc_ref[...].astype(BF16).reshape(k, n, d)

    block = dest.off // n
    return pl.pallas_call(
        body, name=name, grid=(m // tm, n_t),
        in_specs=[pl.BlockSpec((tt, tm), lambda j, t: (t, j)), pl.BlockSpec((tt, d), lambda j, t: (t, 0))] + ([] if fresh else [ANY]),
        out_specs=pl.BlockSpec((k, n, d), lambda j, t: (j, block, 0)),
        out_shape=SDS(tuple(dest.arr) if fresh else dest.arr.shape, BF16),
        scratch_shapes=[pltpu.VMEM((tm, d), F32)],
        input_output_aliases={} if fresh else {2: 0},
        compiler_params=_cparams(2),
    )(*((a, b) if fresh else (a, b, dest.arr)))


def _my_place():
    return lax.axis_index("x"), lax.axis_index("y"), lax.axis_index("c")


def _other_chips(x, y):
    return [(1 - x, y), (x, 1 - y), (1 - x, 1 - y)]


def _all_gather(blocks, name):
    n_arr = len(blocks)

    def body(*refs):
        in_refs = refs[:n_arr]
        out_refs = refs[n_arr:2 * n_arr]
        send_sems, recv_sems, local_sems = refs[2 * n_arr:]
        x, y, c = _my_place()
        me, sibling = (x, y, c), (x, y, 1 - c)
        chips = _other_chips(x, y)

        def slot(a, place):
            px, py, pc = place
            return out_refs[a].at[4 * px + 2 * py + pc]

        def copy(a, k, block, to, src=None):
            return pltpu.make_async_remote_copy(
                src_ref=slot(a, block) if src is None else src, dst_ref=slot(a, block),
                send_sem=send_sems.at[a, k], recv_sem=recv_sems.at[a, k], device_id=to, device_id_type=MESH)

        started = []
        local = []
        for a in range(n_arr):
            mine = pltpu.make_async_copy(in_refs[a], slot(a, me), local_sems.at[a])
            mine.start()
            local.append(mine)
            first = [copy(a, 0, me, sibling, src=in_refs[a])]
            first += [copy(a, 1 + j, me, (*chip, c), src=in_refs[a]) for j, chip in enumerate(chips)]
            for cp in first:
                cp.start()
            started += first
        for a in range(n_arr):
            for j, chip in enumerate(chips):
                copy(a, 1 + j, (*chip, c), me).wait_recv()
                passed = copy(a, 4 + j, (*chip, c), sibling)
                passed.start()
                started.append(passed)
        for a in range(n_arr):
            copy(a, 0, sibling, me).wait_recv()
            for j, chip in enumerate(chips):
                copy(a, 4 + j, (*chip, 1 - c), me).wait_recv()
        for cp in started:
            cp.wait_send()
        for mine in local:
            mine.wait()

    return pl.pallas_call(
        body, name=name,
        in_specs=[ANY] * n_arr, out_specs=[ANY] * n_arr,
        out_shape=[SDS((N_DEV,) + b.shape, b.dtype) for b in blocks],
        scratch_shapes=[pltpu.SemaphoreType.DMA((n_arr, 7)), pltpu.SemaphoreType.DMA((n_arr, 7)),
                        pltpu.SemaphoreType.DMA((n_arr,))],
    )(*blocks)


def _peer_of(k, x, y, c):
    return x ^ ((k >> 2) & 1), y ^ ((k >> 1) & 1), c ^ (k & 1)


HBM = pl.BlockSpec(memory_space=pltpu.HBM)
SEM = pl.BlockSpec(memory_space=pltpu.SEMAPHORE)
DATAFLOW_EFFECT = pltpu.SideEffectType.DATAFLOW_SIDE_EFFECTING


def _peer_copies(src_ref, land_ref, send_sems, recv_sems, per_peer):
    x, y, c = _my_place()
    me = 4 * x + 2 * y + c
    copies = []
    for k in range(1, N_DEV):
        px, py, pc = _peer_of(k, x, y, c)
        peer = 4 * px + 2 * py + pc
        copies.append(pltpu.make_async_remote_copy(
            src_ref=src_ref.at[peer] if per_peer else src_ref, dst_ref=land_ref.at[me],
            send_sem=send_sems.at[k - 1], recv_sem=recv_sems.at[k - 1], device_id=(px, py, pc), device_id_type=MESH))
    own = pltpu.make_async_copy(src_ref.at[me] if per_peer else src_ref, land_ref.at[me], send_sems.at[N_DEV - 1])
    return copies, own


def _exchange_start(src, after, per_peer, name):
    rows, d = src.shape[-2:]

    def body(src_ref, land_ref, after_ref, send_sems, recv_sems, src_thru, land_thru, token):
        copies, own = _peer_copies(src_ref, land_ref, send_sems, recv_sems, per_peer)
        for cp in copies:
            cp.start()
        own.start()
        token[...] = jnp.zeros_like(token)

    return pl.pallas_call(
        body, name=name,
        out_shape=(pltpu.SemaphoreType.DMA((N_DEV,)), pltpu.SemaphoreType.DMA((N_DEV - 1,)), pltpu.HBM(src.shape, src.dtype),
                   pltpu.HBM((N_DEV, rows, d), src.dtype), SDS((SUBLANES, LANES), F32)),
        in_specs=(HBM, HBM, ANY), out_specs=(SEM, SEM, HBM, HBM, pl.BlockSpec(memory_space=pltpu.VMEM)),
        input_output_aliases={0: 2, 1: 3},
        compiler_params=pltpu.CompilerParams(has_side_effects=DATAFLOW_EFFECT),
    )(pltpu.with_memory_space_constraint(src, pltpu.HBM),
      pltpu.with_memory_space_constraint(lax.empty((N_DEV, rows, d), src.dtype), pltpu.HBM), after)


def _exchange_wait(started, after, per_peer, name):
    send_sems, recv_sems, src_thru, land_thru, _ = started

    def body(src_ref, land_ref, send_sems, recv_sems, after_ref, src_out, land_out):
        copies, own = _peer_copies(src_ref, land_ref, send_sems, recv_sems, per_peer)
        for cp in copies:
            cp.wait_send()
            cp.wait_recv()
        own.wait()

    return pl.pallas_call(
        body, name=name,
        out_shape=(pltpu.HBM(src_thru.shape, src_thru.dtype), pltpu.HBM(land_thru.shape, land_thru.dtype)),
        in_specs=(HBM, HBM, SEM, SEM, ANY), out_specs=(HBM, HBM), input_output_aliases={0: 0, 1: 1},
        compiler_params=pltpu.CompilerParams(has_side_effects=DATAFLOW_EFFECT),
    )(src_thru, land_thru, send_sems, recv_sems, after)


def _sum_slots(slots, name):
    _, rows, d = slots.shape
    tr = _largest_divisor(rows, 512, 16)

    def body(s_ref, o_ref):
        acc = s_ref[0].astype(F32)
        for dev in range(1, N_DEV):
            acc = acc + s_ref[dev].astype(F32)
        o_ref[...] = acc

    return pl.pallas_call(
        body, name=name, grid=(rows // tr,),
        in_specs=[pl.BlockSpec((N_DEV, tr, d), lambda r: (0, r, 0))], out_specs=pl.BlockSpec((tr, d), lambda r: (r, 0)),
        out_shape=SDS((rows, d), F32), compiler_params=_cparams(),
    )(slots)


def _all_reduce_small(part, loss_rows, name):
    rows, lanes = part.shape
    lo, hi = loss_rows

    def body(x_ref, out_ref, loss_ref, gath_ref, send_sems, recv_sems):
        x, y, c = _my_place()
        me = 4 * x + 2 * y + c
        gath_ref[me] = x_ref[...]
        copies = []
        for k in range(1, N_DEV):
            peer = (x ^ ((k >> 2) & 1), y ^ ((k >> 1) & 1), c ^ (k & 1))
            cp = pltpu.make_async_remote_copy(
                src_ref=x_ref, dst_ref=gath_ref.at[me], send_sem=send_sems.at[k - 1], recv_sem=recv_sems.at[k - 1],
                device_id=peer, device_id_type=MESH)
            cp.start()
            copies.append(cp)
        for cp in copies:
            cp.wait_recv()
        for cp in copies:
            cp.wait_send()
        acc = gath_ref[0]
        for dev in range(1, N_DEV):
            acc = acc + gath_ref[dev]
        out_ref[...] = acc
        loss_ref[...] = jnp.full(loss_ref.shape, jnp.sum(acc[lo:hi, :]), F32)

    vmem = pl.BlockSpec(memory_space=pltpu.VMEM)
    return pl.pallas_call(
        body, name=name, in_specs=[vmem], out_specs=[vmem, vmem],
        out_shape=[SDS((rows, lanes), F32), SDS((SUBLANES, LANES), F32)],
        scratch_shapes=[pltpu.VMEM((N_DEV, rows, lanes), F32), pltpu.SemaphoreType.DMA((N_DEV - 1,)),
                        pltpu.SemaphoreType.DMA((N_DEV - 1,))],
    )(part)


def _adamw(w, g, m, v, name):
    rows, cols = w.shape
    tr = rows if rows % SUBLANES else _largest_divisor(rows, 512, SUBLANES)

    def body(w_ref, g_ref, m_ref, v_ref, d_ref, nm_ref, nv_ref):
        gv = g_ref[...]
        nm = ADAM_B1 * m_ref[...] + (1.0 - ADAM_B1) * gv
        nv = ADAM_B2 * v_ref[...] + (1.0 - ADAM_B2) * (gv * gv)
        m_hat = nm / (1.0 - ADAM_B1 ** ADAM_STEP)
        v_hat = nv / (1.0 - ADAM_B2 ** ADAM_STEP)
        d_ref[...] = -ADAM_LR * (m_hat / (jnp.sqrt(v_hat) + ADAM_EPS) + ADAM_WD * w_ref[...])
        nm_ref[...] = nm
        nv_ref[...] = nv

    spec = pl.BlockSpec((tr, cols), lambda i: (i, 0))
    return pl.pallas_call(
        body, name=name, grid=(rows // tr,), in_specs=[spec] * 4, out_specs=[spec] * 3,
        out_shape=[SDS((rows, cols), F32)] * 3, compiler_params=_cparams(),
    )(w, g, m, v)


def _adamw_nd(w, g, m, v, name):
    shape = w.shape
    two_d = (1, shape[0]) if len(shape) == 1 else (-1, shape[-1])
    outs = _adamw(w.reshape(two_d), g.reshape(two_d), m.reshape(two_d), v.reshape(two_d), name)
    return [o.reshape(shape) for o in outs]


def _rope_tables(seq):
    pos = jnp.arange(seq, dtype=F32)
    inv_freq = 1.0 / (ROPE_THETA ** (jnp.arange(0, HEAD_DIM, 2, dtype=F32) / HEAD_DIM))
    ang = pos[:, None] * inv_freq[None, :]
    cos, sin = jnp.cos(ang), jnp.sin(ang)
    reps = LANES // HEAD_DIM
    cos_t = jnp.tile(jnp.concatenate([cos, cos], axis=1), (1, reps))
    sin_t = jnp.tile(jnp.concatenate([-sin, sin], axis=1), (1, reps))
    return cos_t, sin_t


def _flat_pad(a):
    flat = a.reshape(1, -1)
    pad = (-flat.shape[1]) % LANES
    return jnp.pad(flat, ((0, 0), (0, pad))) if pad else flat


def kernel(x, norm_mix, norm_ffn, norm_final, conv_w_in, conv_w_conv, conv_w_out, attn_w_qkv, attn_b_qkv, attn_sinks, attn_w_o, attn_b_o, ffn_w_in, ffn_w_conv, ffn_w_down, loss_target, m_norm_mix, m_norm_ffn, m_norm_final, m_conv_w_in, m_conv_w_conv, m_conv_w_out, m_attn_w_qkv, m_attn_b_qkv, m_attn_sinks, m_attn_w_o, m_attn_b_o, m_ffn_w_in, m_ffn_w_conv, m_ffn_w_down, v_norm_mix, v_norm_ffn, v_norm_final, v_conv_w_in, v_conv_w_conv, v_conv_w_out, v_attn_w_qkv, v_attn_b_qkv, v_attn_sinks, v_attn_w_o, v_attn_b_o, v_ffn_w_in, v_ffn_w_conv, v_ffn_w_down):
    b_loc, seq, d = x.shape
    depth = norm_mix.shape[0]
    n_conv, n_attn = conv_w_in.shape[0], attn_w_qkv.shape[0]
    t_all = b_loc * seq
    my_x, my_y, my_c = _my_place()

    me = 4 * my_x + 2 * my_y + my_c

    groups = []
    for i in range(depth):
        j = i // 2
        if i % 2 == 0:
            mix = [("conv_w_in", j, True, conv_w_in[j].T), ("conv_w_out", j, False, conv_w_out[j])]
        else:
            mix = [("attn_w_qkv", j, True, attn_w_qkv[j].T), ("attn_w_o", j, False, attn_w_o[j])]
        groups.append((("mix", i), mix))
        groups.append((("ffn", i), [("ffn_w_in", i, True, ffn_w_in[i].T), ("ffn_w_down", i, False, ffn_w_down[i])]))
    order = [key for key, _ in groups]
    members_of = dict(groups)

    def layout(key):
        offs, o = [], 0
        for _, _, _, shard in members_of[key]:
            n = shard.shape[0]
            o = -(-o // n) * n
            offs.append(o)
            o += n
        return offs, o

    small = jnp.concatenate([_flat_pad(conv_w_conv), _flat_pad(ffn_w_conv), _flat_pad(attn_b_qkv), _flat_pad(attn_b_o)], axis=1)
    (small_g,) = _all_gather([small], "gather_small")

    gather_started = {}

    def start_gather(idx, after):
        if idx >= len(order):
            return 0.0
        key = order[idx]
        offs, total = layout(key)
        pieces, o = [], 0
        for (_, _, _, shard), off in zip(members_of[key], offs):
            if off > o:
                pieces.append(jnp.zeros((off - o, d), shard.dtype))
            pieces.append(shard)
            o = off + shard.shape[0]
        pack = jnp.concatenate(pieces, axis=0).astype(BF16)
        gather_started[key] = _exchange_start(pack, after, False, f"gather_start_{key[0]}_{key[1]}")
        return gather_started[key][4][0, 0]

    weights = {}

    def finish_gather(key, after):
        _, land = _exchange_wait(gather_started[key], after, False, f"gather_wait_{key[0]}_{key[1]}")
        for (wname, layer, _, shard), off in zip(members_of[key], layout(key)[0]):
            weights[(wname, layer)] = _Rows(land, off, shard.shape[0])

    def take_small(o, shape):
        size = shape[0] * shape[1] * shape[2]
        blk = small_g[:, 0, o:o + size].reshape((N_DEV,) + shape)
        return jnp.moveaxis(blk, 0, 2).reshape(shape[0], shape[1], N_DEV * shape[2])

    so = 0
    wc_conv_full = take_small(so, conv_w_conv.shape); so += _flat_pad(conv_w_conv).shape[1]
    wc_ffn_full = take_small(so, ffn_w_conv.shape); so += _flat_pad(ffn_w_conv).shape[1]
    b_qkv_full = take_small(so, (n_attn, 1, attn_b_qkv.shape[1]))[:, 0]; so += _flat_pad(attn_b_qkv).shape[1]
    b_o_full = take_small(so, (n_attn, 1, attn_b_o.shape[1]))[:, 0]

    cos_t, sin_t = _rope_tables(seq)

    xs = [x.reshape(t_all, d)]
    saved = []
    token = start_gather(0, small_g)
    for i in range(depth):
        j = i // 2
        if i > 0:
            token = start_gather(2 * i + 2, xs[-1])
        gain_mix = norm_mix[i][None, :] + token
        finish_gather(("mix", i), gain_mix if i == 0 else xs[-1])
        if i == 0:
            gain_mix = gain_mix + start_gather(1, weights[("conv_w_out", 0)].arr)
        if i % 2 == 0:
            x1, *mix_saved = _fwd_conv_mixer(xs[-1], gain_mix, weights[("conv_w_in", j)], wc_conv_full[j],
                                             weights[("conv_w_out", j)], seq, f"fwd_conv_{i}")
        else:
            qkv = _fwd_qkv(xs[-1], gain_mix, weights[("attn_w_qkv", j)], b_qkv_full[j][None, :], cos_t, sin_t, seq,
                           f"fwd_qkv_{i}")
            x1, o, probs_t, p_sink = _fwd_attention(qkv, xs[-1], attn_sinks[j], weights[("attn_w_o", j)],
                                                    b_o_full[j][None, :], seq, f"fwd_attn_{i}")
            mix_saved = (qkv, o, probs_t, p_sink)
        token = start_gather(2 * i + 3, x1) + (start_gather(2, x1) if i == 0 else 0.0)
        gain_ffn = norm_ffn[i][None, :] + token
        finish_gather(("ffn", i), gain_ffn)
        x2, *ffn_saved = _fwd_ffn(x1, gain_ffn, weights[("ffn_w_in", i)], wc_ffn_full[i], weights[("ffn_w_down", i)],
                                  seq, f"fwd_ffn_{i}")
        saved.append((xs[-1], x1, mix_saved, ffn_saved))
        xs.append(x2)
        token = 0.0

    dx, dg_final, loss_lanes = _final_norm_loss(xs[-1], norm_final[None, :], loss_target.reshape(t_all, d), "loss_head")

    dg_mix, dg_ffn = [None] * depth, [None] * depth
    dwc_conv, dwc_ffn = [None] * n_conv, [None] * depth
    db_qkv, db_o, dsinks = [None] * n_attn, [None] * n_attn, [None] * n_attn
    scatter_started = {}

    def weight_grads(key, operands):
        offs, total = layout(key)
        parts = (N_DEV, total, d)
        for (wname, layer, _, shard), off, (a, b) in zip(members_of[key], offs, operands):
            parts = _tn_matmul(a, b, _Rows(parts, off, shard.shape[0]), f"dw_{wname}_{layer}")
        scatter_started[key] = _exchange_start(parts, operands[0][1], True, f"scatter_start_{key[0]}_{key[1]}")
        return scatter_started[key][4][0, 0]

    token = 0.0
    for i in reversed(range(depth)):
        j = i // 2
        x0, x1, mix_saved, (gate, s_act, uds, act) = saved[i]
        dgu, dwc = _bwd_ffn_inner(dx, gate, s_act, uds, wc_ffn_full[i] + token, weights[("ffn_w_down", i)], seq, f"bwd_ffn_{i}")
        dwc_ffn[i] = dwc[:3]
        dx1, h2, dg_ffn[i] = _bwd_dense_norm(dgu, weights[("ffn_w_in", i)], x1, norm_ffn[i][None, :], dx, f"bwd_ffn_norm_{i}")
        token = weight_grads(("ffn", i), [(dgu, h2), (act, dx)])
        if i % 2 == 0:
            bcv, cc, y = mix_saved
            dbcv, dwc = _bwd_conv_inner(dx1, bcv, cc, wc_conv_full[j] + token, weights[("conv_w_out", j)], seq, f"bwd_conv_{i}")
            dwc_conv[j] = dwc[:3]
            dx, h, dg_mix[i] = _bwd_dense_norm(dbcv, weights[("conv_w_in", j)], x0, norm_mix[i][None, :], dx1,
                                               f"bwd_conv_norm_{i}")
            token = weight_grads(("mix", i), [(dbcv, h), (y, dx1)])
        else:
            qkv, o, probs_t, p_sink = mix_saved
            dqkv, dsk, dbq, dbo = _bwd_attention_inner(dx1, qkv, probs_t, p_sink + token, weights[("attn_w_o", j)], cos_t,
                                                       sin_t, seq, f"bwd_attn_{i}")
            dsinks[j], db_qkv[j], db_o[j] = dsk[0:1, :attn_sinks.shape[1]], dbq, dbo
            dx, h, dg_mix[i] = _bwd_dense_norm(dqkv, weights[("attn_w_qkv", j)], x0, norm_mix[i][None, :], dx1,
                                               f"bwd_attn_norm_{i}")
            token = weight_grads(("mix", i), [(dqkv, h), (o, dx1)])
    grad_x = dx.reshape(b_loc, seq, d)

    reduced = {}

    def finish_scatter(key, after):
        _, land = _exchange_wait(scatter_started[key], after, True, f"scatter_wait_{key[0]}_{key[1]}")
        total = _sum_slots(land, f"scatter_sum_{key[0]}_{key[1]}")
        for (wname, layer, transposed, shard), off in zip(members_of[key], layout(key)[0]):
            rows = total[off:off + shard.shape[0]]
            reduced[(wname, layer)] = rows.T if transposed else rows

    late_keys = [order[1], order[0]]
    for key in reversed(order[2:]):
        finish_scatter(key, dx)

    small_parts = [jnp.concatenate(dg_mix, axis=0), jnp.concatenate(dg_ffn, axis=0), dg_final,
                   jnp.stack(dwc_conv), jnp.stack(dwc_ffn), jnp.concatenate(db_qkv, axis=0), jnp.concatenate(db_o, axis=0),
                   jnp.concatenate(dsinks, axis=0), loss_lanes]
    flats = [_flat_pad(p) for p in small_parts]
    bounds = []
    so = 0
    for fl in flats:
        bounds.append((so, so + fl.shape[1]))
        so += fl.shape[1]
    small_rows = so // LANES
    pad_rows = (-small_rows) % SUBLANES
    part_small = jnp.pad(jnp.concatenate(flats, axis=1).reshape(small_rows, LANES), ((0, pad_rows), (0, 0)))
    loss_rows = (bounds[-1][0] // LANES, bounds[-1][1] // LANES)
    summed, loss_tile = _all_reduce_small(part_small, loss_rows, "reduce_small")
    summed = summed.reshape(1, -1)

    def small_grad(k, shape):
        lo = bounds[k][0]
        size = 1
        for s_ in shape:
            size *= s_
        return summed[0, lo:lo + size].reshape(shape)

    def my_cols(full, n_local):
        return lax.dynamic_slice_in_dim(full, me * n_local, n_local, axis=full.ndim - 1)

    g_norm_mix = small_grad(0, norm_mix.shape)
    g_norm_ffn = small_grad(1, norm_ffn.shape)
    g_norm_final = small_grad(2, norm_final.shape)
    g_conv_w_conv = my_cols(small_grad(3, (n_conv, 3, d)), conv_w_conv.shape[2])
    g_ffn_w_conv = my_cols(small_grad(4, (depth, 3, ffn_w_conv.shape[2] * N_DEV)), ffn_w_conv.shape[2])
    g_attn_b_qkv = my_cols(small_grad(5, (n_attn, attn_b_qkv.shape[1] * N_DEV)), attn_b_qkv.shape[1])
    g_attn_b_o = my_cols(small_grad(6, (n_attn, d)), attn_b_o.shape[1])
    g_attn_sinks = small_grad(7, attn_sinks.shape)
    loss = loss_tile[0, 0]

    def big_grad(wname, n_layers):
        return jnp.stack([reduced[(wname, layer)] for layer in range(n_layers)])

    grads = {
        "norm_mix": g_norm_mix, "norm_ffn": g_norm_ffn, "norm_final": g_norm_final, "conv_w_conv": g_conv_w_conv,
        "attn_w_qkv": big_grad("attn_w_qkv", n_attn), "attn_b_qkv": g_attn_b_qkv, "attn_sinks": g_attn_sinks,
        "attn_w_o": big_grad("attn_w_o", n_attn), "attn_b_o": g_attn_b_o, "ffn_w_conv": g_ffn_w_conv,
    }
    params = {
        "norm_mix": (norm_mix, m_norm_mix, v_norm_mix), "norm_ffn": (norm_ffn, m_norm_ffn, v_norm_ffn),
        "norm_final": (norm_final, m_norm_final, v_norm_final), "conv_w_in": (conv_w_in, m_conv_w_in, v_conv_w_in),
        "conv_w_conv": (conv_w_conv, m_conv_w_conv, v_conv_w_conv), "conv_w_out": (conv_w_out, m_conv_w_out, v_conv_w_out),
        "attn_w_qkv": (attn_w_qkv, m_attn_w_qkv, v_attn_w_qkv), "attn_b_qkv": (attn_b_qkv, m_attn_b_qkv, v_attn_b_qkv),
        "attn_sinks": (attn_sinks, m_attn_sinks, v_attn_sinks), "attn_w_o": (attn_w_o, m_attn_w_o, v_attn_w_o),
        "attn_b_o": (attn_b_o, m_attn_b_o, v_attn_b_o), "ffn_w_in": (ffn_w_in, m_ffn_w_in, v_ffn_w_in),
        "ffn_w_conv": (ffn_w_conv, m_ffn_w_conv, v_ffn_w_conv), "ffn_w_down": (ffn_w_down, m_ffn_w_down, v_ffn_w_down),
    }
    names = list(params)
    updates = {}

    def update(wname):
        w, m, v = params[wname]
        updates[wname] = _adamw_nd(w, grads[wname], m, v, f"adamw_{wname}")

    late_names = [sorted({wname for wname, _, _, _ in members_of[key]}) for key in late_keys]
    for wname in names:
        if not any(wname in group for group in late_names):
            update(wname)
    behind = updates["attn_w_qkv"][0]
    for key, group in zip(late_keys, late_names):
        finish_scatter(key, behind)
        for wname in group:
            grads[wname] = big_grad(wname, params[wname][0].shape[0])
            update(wname)
        behind = updates[group[0]][0]
    return (loss, grad_x, *[grads[wname] for wname in names], *[updates[wname][0] for wname in names],
            *[updates[wname][1] for wname in names], *[updates[wname][2] for wname in names])
```

```python
from typing import NamedTuple

import jax
import jax.numpy as jnp
from jax import lax
from jax.experimental import pallas as pl
from jax.experimental.pallas import tpu as pltpu

F32 = jnp.float32
BF16 = jnp.bfloat16
SDS = jax.ShapeDtypeStruct
MESH = pl.DeviceIdType.MESH
ANY = pl.BlockSpec(memory_space=pl.ANY)

N_DEV = 8
EPS = 1e-5
HEAD_DIM = 64
GROUP = 4
WINDOW = 128
ROPE_THETA = 10000.0
ADAM_LR, ADAM_B1, ADAM_B2, ADAM_EPS, ADAM_WD, ADAM_STEP = 0.001, 0.9, 0.999, 1e-08, 0.01, 10

V7X_VMEM_BYTES = 64 * 1024 * 1024
VMEM_LIMIT_BYTES = V7X_VMEM_BYTES - 8 * 1024 * 1024
LANES = 128
SUBLANES = 8
TOKEN_TILE = 512
TN_TOKEN_TILE = 2048
MASKED_SCORE = -1e30


def _cparams(n_axes=1):
    return pltpu.CompilerParams(dimension_semantics=("arbitrary",) * n_axes, vmem_limit_bytes=VMEM_LIMIT_BYTES)


def _resident(shape):
    zeros = (0,) * len(shape)
    return pl.BlockSpec(shape, lambda *_: zeros, pipeline_mode=pl.Buffered(1))


class _Rows(NamedTuple):
    arr: jax.Array
    off: int
    n: int


def _rows_spec(w):
    assert w.off % w.n == 0
    block = w.off // w.n
    return pl.BlockSpec((N_DEV, w.n, w.arr.shape[2]), lambda *_: (0, block, 0), pipeline_mode=pl.Buffered(1))


def _mat(ref):
    v = ref[...]
    return v.reshape(v.shape[0] * v.shape[1], v.shape[2])


def _token_tile(seq):
    return min(TOKEN_TILE, seq // 2)


def _largest_divisor(m, cap, mult):
    best = None
    for d in range(mult, min(m, cap) + 1, mult):
        if m % d == 0:
            best = d
    return m if best is None else best


def _nt(a, b):
    return lax.dot_general(a, b, (((1,), (1,)), ((), ())), preferred_element_type=F32)


def _nn(a, b):
    return lax.dot_general(a, b, (((1,), (0,)), ((), ())), preferred_element_type=F32)


def _tn(a, b):
    return lax.dot_general(a, b, (((0,), (0,)), ((), ())), preferred_element_type=F32)


def _rms_parts(xv):
    r = lax.rsqrt(jnp.mean(xv * xv, axis=-1, keepdims=True) + EPS)
    return r, xv * r


def _rms_backward(dh, xh, r, gain, dres):
    u = dh * gain
    return dres + r * (u - xh * jnp.mean(u * xh, axis=-1, keepdims=True))


def _shifted_rows(xv, edge, k, down):
    n = xv.shape[0]
    row = lax.broadcasted_iota(jnp.int32, edge.shape, 0)
    if down:
        rolled = pltpu.roll(xv, k, 0)
        head = jnp.where(row < k, pltpu.roll(edge, k, 0), rolled[0:SUBLANES])
        return jnp.concatenate([head, rolled[SUBLANES:]], axis=0)
    rolled = pltpu.roll(xv, n - k, 0)
    tail = jnp.where(row >= SUBLANES - k, pltpu.roll(edge, SUBLANES - k, 0), rolled[n - SUBLANES:])
    return jnp.concatenate([rolled[:n - SUBLANES], tail], axis=0)


def _causal_conv3(edge_ref, xv, w_ref):
    before = edge_ref[...]
    y = (w_ref[2:3, :] * xv + w_ref[1:2, :] * _shifted_rows(xv, before, 1, True)
         + w_ref[0:1, :] * _shifted_rows(xv, before, 2, True))
    edge_ref[...] = xv[xv.shape[0] - SUBLANES:, :]
    return y


def _sigmoid(z):
    return 1.0 / (1.0 + jnp.exp(-z))


def _fwd_conv_mixer(x, gain, w_in_t, w_conv, w_out, seq, name):
    t_all, d = x.shape
    tt = _token_tile(seq)
    tps = seq // tt

    def body(x_ref, g_ref, win_ref, wc_ref, wout_ref, x1_ref, bcv_ref, cc_ref, y_ref, ext_ref):
        i = pl.program_id(0)
        xv = x_ref[...]
        r, xh = _rms_parts(xv)
        h = (xh * g_ref[...]).astype(BF16)
        bcv = _nt(h, _mat(win_ref))
        bcv_ref[...] = bcv.astype(BF16)

        @pl.when(i % tps == 0)
        def _():
            ext_ref[...] = jnp.zeros_like(ext_ref)

        cc = _causal_conv3(ext_ref, bcv[:, d:2 * d] * bcv[:, 2 * d:], wc_ref)
        cc_ref[...] = cc.astype(BF16)
        y = (bcv[:, :d] * cc).astype(BF16)
        y_ref[...] = y
        x1_ref[...] = xv + _nn(y, _mat(wout_ref))

    tile = pl.BlockSpec((tt, d), lambda i: (i, 0))
    return pl.pallas_call(
        body, name=name, grid=(t_all // tt,),
        in_specs=[tile, _resident((1, d)), _rows_spec(w_in_t), _resident((3, d)), _rows_spec(w_out)],
        out_specs=[tile, pl.BlockSpec((tt, 3 * d), lambda i: (i, 0)), tile, tile],
        out_shape=[SDS((t_all, d), F32), SDS((t_all, 3 * d), BF16), SDS((t_all, d), BF16), SDS((t_all, d), BF16)],
        scratch_shapes=[pltpu.VMEM((SUBLANES, d), F32)],
        compiler_params=_cparams(),
    )(x, gain, w_in_t.arr, w_conv, w_out.arr)


def _fwd_ffn(x, gain, w_in_t, w_conv, w_down, seq, name):
    t_all, d = x.shape
    f = w_down.n * N_DEV
    tt = _token_tile(seq) // 2
    tps = seq // tt

    def body(x_ref, g_ref, win_ref, wc_ref, wd_ref, x2_ref, gate_ref, s_ref, uds_ref, a_ref, ext_ref):
        i = pl.program_id(0)
        xv = x_ref[...]
        r, xh = _rms_parts(xv)
        h = (xh * g_ref[...]).astype(BF16)
        gu = _nt(h, _mat(win_ref))
        gate = gu[:, :f]
        u = gu[:, f:]
        gate_ref[...] = gate.astype(BF16)

        @pl.when(i % tps == 0)
        def _():
            ext_ref[...] = jnp.zeros_like(ext_ref)

        gc = _causal_conv3(ext_ref, gate, wc_ref)
        sig = _sigmoid(gc)
        s = gc * sig
        s_ref[...] = s.astype(BF16)
        uds_ref[...] = (u * (sig * (1.0 + gc * (1.0 - sig)))).astype(BF16)
        a = (s * u).astype(BF16)
        a_ref[...] = a
        x2_ref[...] = xv + _nn(a, _mat(wd_ref))

    wide = pl.BlockSpec((tt, f), lambda i: (i, 0))
    return pl.pallas_call(
        body, name=name, grid=(t_all // tt,),
        in_specs=[pl.BlockSpec((tt, d), lambda i: (i, 0)), _resident((1, d)), _rows_spec(w_in_t),
                  _resident((3, f)), _rows_spec(w_down)],
        out_specs=[pl.BlockSpec((tt, d), lambda i: (i, 0)), wide, wide, wide, wide],
        out_shape=[SDS((t_all, d), F32)] + [SDS((t_all, f), BF16)] * 4,
        scratch_shapes=[pltpu.VMEM((SUBLANES, f), F32)],
        compiler_params=_cparams(),
    )(x, gain, w_in_t.arr, w_conv, w_down.arr)


def _rope_partner(xs, lane_lo):
    return jnp.where(lane_lo, pltpu.roll(xs, LANES - HEAD_DIM // 2, 1), pltpu.roll(xs, HEAD_DIM // 2, 1))


def _fwd_qkv(x, gain, w_qkv_t, b_qkv, cos_t, sin_t, seq, name):
    t_all, d = x.shape
    width = w_qkv_t.n * N_DEV
    kvw = (width - d) // 2
    tt = _token_tile(seq)
    tps = seq // tt
    scale = HEAD_DIM ** -0.5

    def body(x_ref, g_ref, w_ref, b_ref, cos_ref, sin_ref, qkv_ref):
        xv = x_ref[...]
        r, xh = _rms_parts(xv)
        h = (xh * g_ref[...]).astype(BF16)
        qkv = _nt(h, _mat(w_ref)) + b_ref[...]
        cosv = cos_ref[...]
        sinv = sin_ref[...]
        lane_lo = (lax.broadcasted_iota(jnp.int32, (tt, LANES), 1) % HEAD_DIM) < HEAD_DIM // 2
        for s in range((d + kvw) // LANES):
            xs = qkv[:, s * LANES:(s + 1) * LANES]
            roped = xs * cosv + _rope_partner(xs, lane_lo) * sinv
            if s * LANES < d:
                roped = roped * scale
            qkv_ref[:, s * LANES:(s + 1) * LANES] = roped.astype(BF16)
        qkv_ref[:, d + kvw:] = qkv[:, d + kvw:].astype(BF16)

    return pl.pallas_call(
        body, name=name, grid=(t_all // tt,),
        in_specs=[pl.BlockSpec((tt, d), lambda i: (i, 0)), _resident((1, d)), _rows_spec(w_qkv_t),
                  _resident((1, width)), pl.BlockSpec((tt, LANES), lambda i: (i % tps, 0)),
                  pl.BlockSpec((tt, LANES), lambda i: (i % tps, 0))],
        out_specs=pl.BlockSpec((tt, width), lambda i: (i, 0)),
        out_shape=SDS((t_all, width), BF16),
        compiler_params=_cparams(),
    )(x, gain, w_qkv_t.arr, b_qkv, cos_t, sin_t)


def _stack_heads(ref, row0, kh):
    return jnp.concatenate(
        [ref[row0:row0 + WINDOW, (kh * GROUP + g) * HEAD_DIM:(kh * GROUP + g + 1) * HEAD_DIM] for g in range(GROUP)],
        axis=0)


def _band_bias():
    r = lax.broadcasted_iota(jnp.int32, (WINDOW, 2 * WINDOW), 0)
    j = lax.broadcasted_iota(jnp.int32, (WINDOW, 2 * WINDOW), 1)
    base = (j > r) & (j <= r + WINDOW)
    return jnp.where(base, 0.0, MASKED_SCORE), jnp.where(base & (j >= WINDOW), 0.0, MASKED_SCORE)


def _fwd_attention(qkv, x, sinks, w_o, b_o, seq, name):
    t_all, d = x.shape
    width = qkv.shape[1]
    kvw = (width - d) // 2
    n_kv = kvw // HEAD_DIM
    tt = _token_tile(seq)
    tps = seq // tt
    nblk = tt // WINDOW

    def body(sink_ref, qkv_ref, kvp_ref, x_ref, wo_ref, bo_ref, x1_ref, o_ref, kvext_ref, oscr_ref, bias_ref, s_ref, p_ref):
        i = pl.program_id(0)

        @pl.when(i == 0)
        def _():
            base, first = _band_bias()
            bias_ref[0], bias_ref[1] = base.T, first.T

        kvext_ref[0:WINDOW, :] = kvp_ref[...]
        kvext_ref[WINDOW:, :] = qkv_ref[:, d:]
        at_seq_start = (i % tps == 0).astype(jnp.int32)
        steps = [(n, kh) for n in range(nblk) for kh in range(n_kv)]

        def scores(step):
            n, kh = steps[step]
            qs = _stack_heads(qkv_ref, n * WINDOW, kh)
            kb = kvext_ref[n * WINDOW:(n + 2) * WINDOW, kh * HEAD_DIM:(kh + 1) * HEAD_DIM]
            s_ref[step % 2] = _nt(kb, qs)

        scores(0)
        for step, (n, kh) in enumerate(steps):
            buf = step % 2
            if step + 1 < len(steps):
                scores(step + 1)
            vb = kvext_ref[n * WINDOW:(n + 2) * WINDOW, kvw + kh * HEAD_DIM:kvw + (kh + 1) * HEAD_DIM]
            bias = bias_ref[at_seq_start if n == 0 else 0]
            for g in range(GROUP):
                cols = slice(g * WINDOW, (g + 1) * WINDOW)
                sink = sink_ref[kh * GROUP + g]
                sv = s_ref[buf, :, cols] + bias
                m = jnp.maximum(jnp.max(sv, axis=0, keepdims=True), sink)
                p = jnp.exp(sv - m)
                inv = 1.0 / (jnp.sum(p, axis=0, keepdims=True) + jnp.exp(sink - m))
                p_ref[buf, :, cols] = (p * inv).astype(BF16)
            o_s = _tn(vb, p_ref[buf]).T
            for g in range(GROUP):
                hd = kh * GROUP + g
                oscr_ref[n * WINDOW:(n + 1) * WINDOW, hd * HEAD_DIM:(hd + 1) * HEAD_DIM] = (
                    o_s[g * WINDOW:(g + 1) * WINDOW].astype(BF16))
        o = oscr_ref[...]
        o_ref[...] = o
        x1_ref[...] = x_ref[...] + _nn(o, _mat(wo_ref)) + bo_ref[...]

    kv_blocks = tt // WINDOW
    return pl.pallas_call(
        body, name=name, grid=(t_all // tt,),
        in_specs=[pl.BlockSpec(memory_space=pltpu.SMEM),
                  pl.BlockSpec((tt, width), lambda i: (i, 0)),
                  pl.BlockSpec((WINDOW, 2 * kvw), lambda i: (jnp.maximum(i * kv_blocks - 1, 0), d // (2 * kvw))),
                  pl.BlockSpec((tt, d), lambda i: (i, 0)), _rows_spec(w_o), _resident((1, d))],
        out_specs=[pl.BlockSpec((tt, d), lambda i: (i, 0)), pl.BlockSpec((tt, d), lambda i: (i, 0))],
        out_shape=[SDS((t_all, d), F32), SDS((t_all, d), BF16)],
        scratch_shapes=[pltpu.VMEM((tt + WINDOW, 2 * kvw), BF16), pltpu.VMEM((tt, d), BF16),
                        pltpu.VMEM((2, 2 * WINDOW, WINDOW), F32), pltpu.VMEM((2, 2 * WINDOW, GROUP * WINDOW), F32),
                        pltpu.VMEM((2, 2 * WINDOW, GROUP * WINDOW), BF16)],
        compiler_params=_cparams(),
    )(sinks, qkv, qkv, x, w_o.arr, b_o)


def _final_norm_loss(x, gain, target, name):
    t_all, d = x.shape
    tt = min(TOKEN_TILE, t_all)

    def body(x_ref, g_ref, t_ref, dx_ref, dg_ref, loss_ref):
        i = pl.program_id(0)
        xv = x_ref[...]
        r, xh = _rms_parts(xv)
        gain_v = g_ref[...]
        e = xh * gain_v - t_ref[...]
        dy = e * (1.0 / d)
        dx_ref[...] = _rms_backward(dy, xh, r, gain_v, 0.0)

        @pl.when(i == 0)
        def _():
            dg_ref[...] = jnp.zeros_like(dg_ref)
            loss_ref[...] = jnp.zeros_like(loss_ref)

        dg_ref[...] += jnp.sum(dy * xh, axis=0, keepdims=True)
        loss_ref[...] += (0.5 / d) * jnp.sum(e * e, axis=0, keepdims=True)

    return pl.pallas_call(
        body, name=name, grid=(t_all // tt,),
        in_specs=[pl.BlockSpec((tt, d), lambda i: (i, 0)), _resident((1, d)), pl.BlockSpec((tt, d), lambda i: (i, 0))],
        out_specs=[pl.BlockSpec((tt, d), lambda i: (i, 0)), pl.BlockSpec((1, d), lambda i: (0, 0)),
                   pl.BlockSpec((1, d), lambda i: (0, 0))],
        out_shape=[SDS((t_all, d), F32), SDS((1, d), F32), SDS((1, d), F32)],
        compiler_params=_cparams(),
    )(x, gain, target)


def _bwd_ffn_inner(dx2, gate, s_act, uds, w_conv, w_down, seq, name):
    t_all, d = dx2.shape
    f = w_down.n * N_DEV
    tt = _token_tile(seq)
    tps = seq // tt
    nt = t_all // tt

    def body(dx_ref, g_ref, s_ref, uds_ref, wc_ref, wd_ref, dgu_ref, dwc_ref, aext_ref, da_ref):
        i = pl.program_id(0)
        ti = nt - 1 - i
        da_ref[...] = _nt(dx_ref[...].astype(BF16), _mat(wd_ref))

        @pl.when(ti % tps == tps - 1)
        def _():
            aext_ref[...] = jnp.zeros_like(aext_ref)

        @pl.when(i == 0)
        def _():
            dwc_ref[...] = jnp.zeros_like(dwc_ref)

        for c in range(f // LANES):
            cols = slice(c * LANES, (c + 1) * LANES)
            da = da_ref[:, cols]
            g = g_ref[:, cols].astype(F32)
            dgc = da * uds_ref[:, cols].astype(F32)
            after = aext_ref[:, cols]
            sh1 = _shifted_rows(dgc, after, 1, False)
            sh2 = _shifted_rows(dgc, after, 2, False)
            aext_ref[:, cols] = dgc[0:SUBLANES, :]
            dg = wc_ref[2:3, cols] * dgc + wc_ref[1:2, cols] * sh1 + wc_ref[0:1, cols] * sh2
            dgu_ref[:, cols] = dg.astype(BF16)
            dgu_ref[:, f + c * LANES:f + (c + 1) * LANES] = (da * s_ref[:, cols].astype(F32)).astype(BF16)
            dwc_ref[0:1, cols] += jnp.sum(g * sh2, axis=0, keepdims=True)
            dwc_ref[1:2, cols] += jnp.sum(g * sh1, axis=0, keepdims=True)
            dwc_ref[2:3, cols] += jnp.sum(g * dgc, axis=0, keepdims=True)

    rev = lambda i: (nt - 1 - i, 0)
    return pl.pallas_call(
        body, name=name, grid=(nt,),
        in_specs=[pl.BlockSpec((tt, d), rev)] + [pl.BlockSpec((tt, f), rev)] * 3 + [_resident((3, f)), _rows_spec(w_down)],
        out_specs=[pl.BlockSpec((tt, 2 * f), rev), pl.BlockSpec((8, f), lambda i: (0, 0))],
        out_shape=[SDS((t_all, 2 * f), BF16), SDS((8, f), F32)],
        scratch_shapes=[pltpu.VMEM((SUBLANES, f), F32), pltpu.VMEM((tt, f), F32)],
        compiler_params=_cparams(),
    )(dx2, gate, s_act, uds, w_conv, w_down.arr)


def _bwd_conv_inner(dx1, bcv, cc, w_conv, w_out, seq, name):
    t_all, d = dx1.shape
    tt = _token_tile(seq)
    tps = seq // tt
    nt = t_all // tt

    def body(dx_ref, bcv_ref, cc_ref, wc_ref, wout_ref, dbcv_ref, dwc_ref, aext_ref, dy_ref):
        i = pl.program_id(0)
        ti = nt - 1 - i
        dy_ref[...] = _nt(dx_ref[...].astype(BF16), _mat(wout_ref))

        @pl.when(ti % tps == tps - 1)
        def _():
            aext_ref[...] = jnp.zeros_like(aext_ref)

        @pl.when(i == 0)
        def _():
            dwc_ref[...] = jnp.zeros_like(dwc_ref)

        for s in range(d // LANES):
            cols = slice(s * LANES, (s + 1) * LANES)
            ccols = slice(d + s * LANES, d + (s + 1) * LANES)
            vcols = slice(2 * d + s * LANES, 2 * d + (s + 1) * LANES)
            dy = dy_ref[:, cols]
            c = bcv_ref[:, ccols].astype(F32)
            v = bcv_ref[:, vcols].astype(F32)
            cv = c * v
            dcc = dy * bcv_ref[:, cols].astype(F32)
            after = aext_ref[:, cols]
            sh1 = _shifted_rows(dcc, after, 1, False)
            sh2 = _shifted_rows(dcc, after, 2, False)
            aext_ref[:, cols] = dcc[0:SUBLANES, :]
            dcv = wc_ref[2:3, cols] * dcc + wc_ref[1:2, cols] * sh1 + wc_ref[0:1, cols] * sh2
            dbcv_ref[:, cols] = (dy * cc_ref[:, cols].astype(F32)).astype(BF16)
            dbcv_ref[:, ccols] = (dcv * v).astype(BF16)
            dbcv_ref[:, vcols] = (dcv * c).astype(BF16)
            dwc_ref[0:1, cols] += jnp.sum(cv * sh2, axis=0, keepdims=True)
            dwc_ref[1:2, cols] += jnp.sum(cv * sh1, axis=0, keepdims=True)
            dwc_ref[2:3, cols] += jnp.sum(cv * dcc, axis=0, keepdims=True)

    return pl.pallas_call(
        body, name=name, grid=(nt,),
        in_specs=[pl.BlockSpec((tt, d), lambda i: (nt - 1 - i, 0)),
                  pl.BlockSpec((tt, 3 * d), lambda i: (nt - 1 - i, 0)),
                  pl.BlockSpec((tt, d), lambda i: (nt - 1 - i, 0)),
                  _resident((3, d)), _rows_spec(w_out)],
        out_specs=[pl.BlockSpec((tt, 3 * d), lambda i: (nt - 1 - i, 0)), pl.BlockSpec((8, d), lambda i: (0, 0))],
        out_shape=[SDS((t_all, 3 * d), BF16), SDS((8, d), F32)],
        scratch_shapes=[pltpu.VMEM((SUBLANES, d), F32), pltpu.VMEM((tt, d), F32)],
        compiler_params=_cparams(),
    )(dx1, bcv, cc, w_conv, w_out.arr)


def _bwd_attention_inner(dx1, qkv, sinks, w_o, cos_t, sin_t, seq, name):
    t_all, d = dx1.shape
    width = qkv.shape[1]
    kvw = (width - d) // 2
    n_kv = kvw // HEAD_DIM
    tt = _token_tile(seq)
    tps = seq // tt
    nt = t_all // tt
    nblk = tt // WINDOW
    scale = HEAD_DIM ** -0.5

    def body(sink_ref, dx_ref, qkv_ref, kvp_ref, cos_ref, sin_ref, wo_ref,
             dqkv_ref, dsink_ref, dbqkv_ref, dbo_ref,
             kvext_ref, dkvext_ref, carry_ref, dq_ref, do_ref, bias_ref, s_ref, dp_ref, p_ref, ds_ref):
        i = pl.program_id(0)
        ti = nt - 1 - i
        dxv = dx_ref[...]
        do_ref[...] = _nt(dxv.astype(BF16), _mat(wo_ref)).astype(BF16)
        kvext_ref[0:WINDOW, :] = kvp_ref[...]
        kvext_ref[WINDOW:, :] = qkv_ref[:, d:]
        dkvext_ref[...] = jnp.zeros_like(dkvext_ref)

        @pl.when(i == 0)
        def _():
            base, first = _band_bias()
            bias_ref[0], bias_ref[1] = base.T, first.T
            carry_ref[...] = jnp.zeros_like(carry_ref)
            dsink_ref[...] = jnp.zeros_like(dsink_ref)
            dbqkv_ref[...] = jnp.zeros_like(dbqkv_ref)
            dbo_ref[...] = jnp.zeros_like(dbo_ref)

        at_seq_start = (ti % tps == 0).astype(jnp.int32)
        head_lane = lax.broadcasted_iota(jnp.int32, (1, LANES), 1)
        dsink = jnp.zeros((1, LANES), F32)
        for n in range(nblk):
            for kh in range(n_kv):
                buf = (n * n_kv + kh) % 2
                qs = _stack_heads(qkv_ref, n * WINDOW, kh)
                dos = _stack_heads(do_ref, n * WINDOW, kh)
                kcols = slice(kh * HEAD_DIM, (kh + 1) * HEAD_DIM)
                vcols = slice(kvw + kh * HEAD_DIM, kvw + (kh + 1) * HEAD_DIM)
                band = slice(n * WINDOW, (n + 2) * WINDOW)
                kb = kvext_ref[band, kcols]
                vb = kvext_ref[band, vcols]
                s_ref[buf] = _nt(kb, qs)
                dp_ref[buf] = _nt(vb, dos)
                bias = bias_ref[at_seq_start if n == 0 else 0]
                for g in range(GROUP):
                    hd = kh * GROUP + g
                    cols = slice(g * WINDOW, (g + 1) * WINDOW)
                    sink = sink_ref[hd]
                    sv = s_ref[buf, :, cols] + bias
                    m = jnp.maximum(jnp.max(sv, axis=0, keepdims=True), sink)
                    p = jnp.exp(sv - m)
                    e_sink = jnp.exp(sink - m)
                    inv = 1.0 / (jnp.sum(p, axis=0, keepdims=True) + e_sink)
                    probs = p * inv
                    dp = dp_ref[buf, :, cols]
                    dsum = jnp.sum(probs * dp, axis=0, keepdims=True)
                    p_ref[buf, :, cols] = probs.astype(BF16)
                    ds_ref[buf, :, cols] = (probs * (dp - dsum)).astype(BF16)
                    dsink = dsink - jnp.where(head_lane == hd, jnp.sum(e_sink * inv * dsum), 0.0)
                ds_t = ds_ref[buf]
                dkvext_ref[band, vcols] += _nn(p_ref[buf], dos)
                dkvext_ref[band, kcols] += _nn(ds_t, qs)
                dq_s = _tn(kb, ds_t).T
                for g in range(GROUP):
                    hd = kh * GROUP + g
                    dq_ref[n * WINDOW:(n + 1) * WINDOW, hd * HEAD_DIM:(hd + 1) * HEAD_DIM] = dq_s[g * WINDOW:(g + 1) * WINDOW]
        dsink_ref[0:1, :] += dsink
        dkvext_ref[tt:tt + WINDOW, :] += carry_ref[...]
        carry_ref[...] = dkvext_ref[0:WINDOW, :]

        cosv = cos_ref[...]
        sinv = sin_ref[...]
        lane_lo = (lax.broadcasted_iota(jnp.int32, (tt, LANES), 1) % HEAD_DIM) < HEAD_DIM // 2
        for s in range((d + kvw) // LANES):
            if s * LANES < d:
                dy = dq_ref[:, s * LANES:(s + 1) * LANES] * scale
            else:
                dy = dkvext_ref[WINDOW:, s * LANES - d:(s + 1) * LANES - d]
            dpre = dy * cosv - _rope_partner(dy, lane_lo) * sinv
            dqkv_ref[:, s * LANES:(s + 1) * LANES] = dpre.astype(BF16)
            dbqkv_ref[0:1, s * LANES:(s + 1) * LANES] += jnp.sum(dpre, axis=0, keepdims=True)
        dv = dkvext_ref[WINDOW:, kvw:]
        dqkv_ref[:, d + kvw:] = dv.astype(BF16)
        dbqkv_ref[0:1, d + kvw:] += jnp.sum(dv, axis=0, keepdims=True)
        dbo_ref[...] += jnp.sum(dxv, axis=0, keepdims=True)

    kv_blocks = tt // WINDOW
    return pl.pallas_call(
        body, name=name, grid=(nt,),
        in_specs=[pl.BlockSpec(memory_space=pltpu.SMEM),
                  pl.BlockSpec((tt, d), lambda i: (nt - 1 - i, 0)),
                  pl.BlockSpec((tt, width), lambda i: (nt - 1 - i, 0)),
                  pl.BlockSpec((WINDOW, 2 * kvw), lambda i: (jnp.maximum((nt - 1 - i) * kv_blocks - 1, 0), d // (2 * kvw))),
                  pl.BlockSpec((tt, LANES), lambda i: ((nt - 1 - i) % tps, 0)),
                  pl.BlockSpec((tt, LANES), lambda i: ((nt - 1 - i) % tps, 0)),
                  _rows_spec(w_o)],
        out_specs=[pl.BlockSpec((tt, width), lambda i: (nt - 1 - i, 0)), pl.BlockSpec((8, LANES), lambda i: (0, 0)),
                   pl.BlockSpec((1, width), lambda i: (0, 0)), pl.BlockSpec((1, d), lambda i: (0, 0))],
        out_shape=[SDS((t_all, width), BF16), SDS((8, LANES), F32), SDS((1, width), F32), SDS((1, d), F32)],
        scratch_shapes=[pltpu.VMEM((tt + WINDOW, 2 * kvw), BF16), pltpu.VMEM((tt + WINDOW, 2 * kvw), F32),
                        pltpu.VMEM((WINDOW, 2 * kvw), F32), pltpu.VMEM((tt, d), F32), pltpu.VMEM((tt, d), BF16),
                        pltpu.VMEM((2, 2 * WINDOW, WINDOW), F32), pltpu.VMEM((2, 2 * WINDOW, GROUP * WINDOW), F32),
                        pltpu.VMEM((2, 2 * WINDOW, GROUP * WINDOW), F32), pltpu.VMEM((2, 2 * WINDOW, GROUP * WINDOW), BF16),
                        pltpu.VMEM((2, 2 * WINDOW, GROUP * WINDOW), BF16)],
        compiler_params=_cparams(),
    )(sinks, dx1, qkv, qkv, cos_t, sin_t, w_o.arr)


def _bwd_dense_norm(dy, w_t, x, gain, dres, name):
    t_all, d = x.shape
    n = dy.shape[1]
    tt = min(TOKEN_TILE, t_all)

    def body(dy_ref, w_ref, x_ref, g_ref, dres_ref, dx_ref, h_ref, dg_ref):
        i = pl.program_id(0)
        dh = _nn(dy_ref[...], _mat(w_ref))
        r, xh = _rms_parts(x_ref[...])
        gain_v = g_ref[...]
        h_ref[...] = (xh * gain_v).astype(BF16)
        dx_ref[...] = _rms_backward(dh, xh, r, gain_v, dres_ref[...])

        @pl.when(i == 0)
        def _():
            dg_ref[...] = jnp.zeros_like(dg_ref)

        dg_ref[...] += jnp.sum(dh * xh, axis=0, keepdims=True)

    return pl.pallas_call(
        body, name=name, grid=(t_all // tt,),
        in_specs=[pl.BlockSpec((tt, n), lambda i: (i, 0)), _rows_spec(w_t), pl.BlockSpec((tt, d), lambda i: (i, 0)),
                  _resident((1, d)), pl.BlockSpec((tt, d), lambda i: (i, 0))],
        out_specs=[pl.BlockSpec((tt, d), lambda i: (i, 0)), pl.BlockSpec((tt, d), lambda i: (i, 0)),
                   pl.BlockSpec((1, d), lambda i: (0, 0))],
        out_shape=[SDS((t_all, d), F32), SDS((t_all, d), BF16), SDS((1, d), F32)],
        compiler_params=_cparams(),
    )(dy, w_t.arr, x, gain, dres)


def _tn_matmul(a, b, dest, name):
    t_all, m = a.shape
    d = b.shape[1]
    n = dest.n
    assert m == N_DEV * n and dest.off % n == 0
    k = max(kk for kk in (1, 2, 4, 8) if kk * n <= max(n, 1536))
    tm = k * n
    tt = min(TN_TOKEN_TILE, t_all)
    n_t = t_all // tt
    fresh = not hasattr(dest.arr, "dtype")

    def body(a_ref, b_ref, *rest):
        o_ref, acc_ref = rest[-2:]
        t = pl.program_id(1)

        @pl.when(t == 0)
        def _():
            acc_ref[...] = jnp.zeros_like(acc_ref)

        acc_ref[...] += _tn(a_ref[...], b_ref[...].astype(BF16))

        @pl.when(t == n_t - 1)
        def _():
            o_ref[...] = acc_ref[...].astype(BF16).reshape(k, n, d)

    block = dest.off // n
    return pl.pallas_call(
        body, name=name, grid=(m // tm, n_t),
        in_specs=[pl.BlockSpec((tt, tm), lambda j, t: (t, j)), pl.BlockSpec((tt, d), lambda j, t: (t, 0))] + ([] if fresh else [ANY]),
        out_specs=pl.BlockSpec((k, n, d), lambda j, t: (j, block, 0)),
        out_shape=SDS(tuple(dest.arr) if fresh else dest.arr.shape, BF16),
        scratch_shapes=[pltpu.VMEM((tm, d), F32)],
        input_output_aliases={} if fresh else {2: 0},
        compiler_params=_cparams(2),
    )(*((a, b) if fresh else (a, b, dest.arr)))


def _my_place():
    return lax.axis_index("x"), lax.axis_index("y"), lax.axis_index("c")


def _other_chips(x, y):
    return [(1 - x, y), (x, 1 - y), (1 - x, 1 - y)]


def _all_gather(blocks, name):
    n_arr = len(blocks)

    def body(*refs):
        in_refs = refs[:n_arr]
        out_refs = refs[n_arr:2 * n_arr]
        send_sems, recv_sems, local_sems = refs[2 * n_arr:]
        x, y, c = _my_place()
        me, sibling = (x, y, c), (x, y, 1 - c)
        chips = _other_chips(x, y)

        def slot(a, place):
            px, py, pc = place
            return out_refs[a].at[4 * px + 2 * py + pc]

        def copy(a, k, block, to, src=None):
            return pltpu.make_async_remote_copy(
                src_ref=slot(a, block) if src is None else src, dst_ref=slot(a, block),
                send_sem=send_sems.at[a, k], recv_sem=recv_sems.at[a, k], device_id=to, device_id_type=MESH)

        started = []
        local = []
        for a in range(n_arr):
            mine = pltpu.make_async_copy(in_refs[a], slot(a, me), local_sems.at[a])
            mine.start()
            local.append(mine)
            first = [copy(a, 0, me, sibling, src=in_refs[a])]
            first += [copy(a, 1 + j, me, (*chip, c), src=in_refs[a]) for j, chip in enumerate(chips)]
            for cp in first:
                cp.start()
            started += first
        for a in range(n_arr):
            for j, chip in enumerate(chips):
                copy(a, 1 + j, (*chip, c), me).wait_recv()
                passed = copy(a, 4 + j, (*chip, c), sibling)
                passed.start()
                started.append(passed)
        for a in range(n_arr):
            copy(a, 0, sibling, me).wait_recv()
            for j, chip in enumerate(chips):
                copy(a, 4 + j, (*chip, 1 - c), me).wait_recv()
        for cp in started:
            cp.wait_send()
        for mine in local:
            mine.wait()

    return pl.pallas_call(
        body, name=name,
        in_specs=[ANY] * n_arr, out_specs=[ANY] * n_arr,
        out_shape=[SDS((N_DEV,) + b.shape, b.dtype) for b in blocks],
        scratch_shapes=[pltpu.SemaphoreType.DMA((n_arr, 7)), pltpu.SemaphoreType.DMA((n_arr, 7)),
                        pltpu.SemaphoreType.DMA((n_arr,))],
    )(*blocks)


def _peer_of(k, x, y, c):
    return x ^ ((k >> 2) & 1), y ^ ((k >> 1) & 1), c ^ (k & 1)


HBM = pl.BlockSpec(memory_space=pltpu.HBM)
SEM = pl.BlockSpec(memory_space=pltpu.SEMAPHORE)
DATAFLOW_EFFECT = pltpu.SideEffectType.DATAFLOW_SIDE_EFFECTING


def _peer_copies(src_ref, land_ref, send_sems, recv_sems, per_peer):
    x, y, c = _my_place()
    me = 4 * x + 2 * y + c
    copies = []
    for k in range(1, N_DEV):
        px, py, pc = _peer_of(k, x, y, c)
        peer = 4 * px + 2 * py + pc
        copies.append(pltpu.make_async_remote_copy(
            src_ref=src_ref.at[peer] if per_peer else src_ref, dst_ref=land_ref.at[me],
            send_sem=send_sems.at[k - 1], recv_sem=recv_sems.at[k - 1], device_id=(px, py, pc), device_id_type=MESH))
    own = pltpu.make_async_copy(src_ref.at[me] if per_peer else src_ref, land_ref.at[me], send_sems.at[N_DEV - 1])
    return copies, own


def _exchange_start(src, after, per_peer, name):
    rows, d = src.shape[-2:]

    def body(src_ref, land_ref, after_ref, send_sems, recv_sems, src_thru, land_thru, token):
        copies, own = _peer_copies(src_ref, land_ref, send_sems, recv_sems, per_peer)
        for cp in copies:
            cp.start()
        own.start()
        token[...] = jnp.zeros_like(token)

    return pl.pallas_call(
        body, name=name,
        out_shape=(pltpu.SemaphoreType.DMA((N_DEV,)), pltpu.SemaphoreType.DMA((N_DEV - 1,)), pltpu.HBM(src.shape, src.dtype),
                   pltpu.HBM((N_DEV, rows, d), src.dtype), SDS((SUBLANES, LANES), F32)),
        in_specs=(HBM, HBM, ANY), out_specs=(SEM, SEM, HBM, HBM, pl.BlockSpec(memory_space=pltpu.VMEM)),
        input_output_aliases={0: 2, 1: 3},
        compiler_params=pltpu.CompilerParams(has_side_effects=DATAFLOW_EFFECT),
    )(pltpu.with_memory_space_constraint(src, pltpu.HBM),
      pltpu.with_memory_space_constraint(lax.empty((N_DEV, rows, d), src.dtype), pltpu.HBM), after)


def _exchange_wait(started, after, per_peer, name):
    send_sems, recv_sems, src_thru, land_thru, _ = started

    def body(src_ref, land_ref, send_sems, recv_sems, after_ref, src_out, land_out):
        copies, own = _peer_copies(src_ref, land_ref, send_sems, recv_sems, per_peer)
        for cp in copies:
            cp.wait_send()
            cp.wait_recv()
        own.wait()

    return pl.pallas_call(
        body, name=name,
        out_shape=(pltpu.HBM(src_thru.shape, src_thru.dtype), pltpu.HBM(land_thru.shape, land_thru.dtype)),
        in_specs=(HBM, HBM, SEM, SEM, ANY), out_specs=(HBM, HBM), input_output_aliases={0: 0, 1: 1},
        compiler_params=pltpu.CompilerParams(has_side_effects=DATAFLOW_EFFECT),
    )(src_thru, land_thru, send_sems, recv_sems, after)


def _first_level_copies(src_ref, land_ref, send_sems, recv_sems):
    x, y, c = _my_place()
    me = 4 * x + 2 * y + c
    targets = [(x, y, 1 - c)] + [(px, py, c) for px, py in _other_chips(x, y)]
    copies = [pltpu.make_async_remote_copy(src_ref=src_ref, dst_ref=land_ref.at[me], send_sem=send_sems.at[k],
                                           recv_sem=recv_sems.at[k], device_id=t, device_id_type=MESH)
              for k, t in enumerate(targets)]
    own = pltpu.make_async_copy(src_ref, land_ref.at[me], send_sems.at[len(targets)])
    return copies, own


def _second_level_copies(land_ref, send_sems, recv_sems):
    x, y, c = _my_place()
    copies = []
    for j, (px, py) in enumerate(_other_chips(x, y)):
        slot = 4 * px + 2 * py + c
        copies.append(pltpu.make_async_remote_copy(
            src_ref=land_ref.at[slot], dst_ref=land_ref.at[slot], send_sem=send_sems.at[j], recv_sem=recv_sems.at[j],
            device_id=(x, y, 1 - c), device_id_type=MESH))
    return copies


def _gather2_start(src, after, name):
    rows, d = src.shape

    def body(src_ref, land_ref, after_ref, send_sems, recv_sems, src_thru, land_thru, token):
        copies, own = _first_level_copies(src_ref, land_ref, send_sems, recv_sems)
        for cp in copies:
            cp.start()
        own.start()
        token[...] = jnp.zeros_like(token)

    return pl.pallas_call(
        body, name=name,
        out_shape=(pltpu.SemaphoreType.DMA((5,)), pltpu.SemaphoreType.DMA((4,)), pltpu.HBM(src.shape, src.dtype),
                   pltpu.HBM((N_DEV, rows, d), src.dtype), SDS((SUBLANES, LANES), F32)),
        in_specs=(HBM, HBM, ANY), out_specs=(SEM, SEM, HBM, HBM, pl.BlockSpec(memory_space=pltpu.VMEM)),
        input_output_aliases={0: 2, 1: 3},
        compiler_params=pltpu.CompilerParams(has_side_effects=DATAFLOW_EFFECT),
    )(pltpu.with_memory_space_constraint(src, pltpu.HBM),
      pltpu.with_memory_space_constraint(lax.empty((N_DEV, rows, d), src.dtype), pltpu.HBM), after)


def _gather2_forward(started, after, name):
    send_a, recv_a, src_thru, land_thru, _ = started

    def body(src_ref, land_ref, send_a_ref, recv_a_ref, after_ref, src_out, land_out, send_f, recv_f):
        copies, _ = _first_level_copies(src_ref, land_ref, send_a_ref, recv_a_ref)
        for cp in copies[1:]:
            cp.wait_recv()
        for cp in _second_level_copies(land_ref, send_f, recv_f):
            cp.start()

    return pl.pallas_call(
        body, name=name,
        out_shape=(pltpu.HBM(src_thru.shape, src_thru.dtype), pltpu.HBM(land_thru.shape, land_thru.dtype),
                   pltpu.SemaphoreType.DMA((3,)), pltpu.SemaphoreType.DMA((3,))),
        in_specs=(HBM, HBM, SEM, SEM, ANY), out_specs=(HBM, HBM, SEM, SEM), input_output_aliases={0: 0, 1: 1},
        compiler_params=pltpu.CompilerParams(has_side_effects=DATAFLOW_EFFECT),
    )(src_thru, land_thru, send_a, recv_a, after)


def _gather2_wait(started, forwarded, name):
    send_a, recv_a, _, _, _ = started
    src_thru, land_thru, send_f, recv_f = forwarded

    def body(src_ref, land_ref, send_a_ref, recv_a_ref, send_f_ref, recv_f_ref, src_out, land_out):
        copies, own = _first_level_copies(src_ref, land_ref, send_a_ref, recv_a_ref)
        for cp in copies:
            cp.wait_send()
        own.wait()
        copies[0].wait_recv()
        for cp in _second_level_copies(land_ref, send_f_ref, recv_f_ref):
            cp.wait_send()
            cp.wait_recv()

    return pl.pallas_call(
        body, name=name,
        out_shape=(pltpu.HBM(src_thru.shape, src_thru.dtype), pltpu.HBM(land_thru.shape, land_thru.dtype)),
        in_specs=(HBM, HBM, SEM, SEM, SEM, SEM), out_specs=(HBM, HBM), input_output_aliases={0: 0, 1: 1},
        compiler_params=pltpu.CompilerParams(has_side_effects=DATAFLOW_EFFECT),
    )(src_thru, land_thru, send_a, recv_a, send_f, recv_f)[1]


def _sum_slots(slots, name):
    _, rows, d = slots.shape
    tr = _largest_divisor(rows, 512, 16)

    def body(s_ref, o_ref):
        acc = s_ref[0].astype(F32)
        for dev in range(1, N_DEV):
            acc = acc + s_ref[dev].astype(F32)
        o_ref[...] = acc

    return pl.pallas_call(
        body, name=name, grid=(rows // tr,),
        in_specs=[pl.BlockSpec((N_DEV, tr, d), lambda r: (0, r, 0))], out_specs=pl.BlockSpec((tr, d), lambda r: (r, 0)),
        out_shape=SDS((rows, d), F32), compiler_params=_cparams(),
    )(slots)


def _all_reduce_small(part, loss_rows, name):
    rows, lanes = part.shape
    lo, hi = loss_rows

    def body(x_ref, out_ref, loss_ref, gath_ref, send_sems, recv_sems):
        x, y, c = _my_place()
        me = 4 * x + 2 * y + c
        gath_ref[me] = x_ref[...]
        copies = []
        for k in range(1, N_DEV):
            peer = (x ^ ((k >> 2) & 1), y ^ ((k >> 1) & 1), c ^ (k & 1))
            cp = pltpu.make_async_remote_copy(
                src_ref=x_ref, dst_ref=gath_ref.at[me], send_sem=send_sems.at[k - 1], recv_sem=recv_sems.at[k - 1],
                device_id=peer, device_id_type=MESH)
            cp.start()
            copies.append(cp)
        for cp in copies:
            cp.wait_recv()
        for cp in copies:
            cp.wait_send()
        acc = gath_ref[0]
        for dev in range(1, N_DEV):
            acc = acc + gath_ref[dev]
        out_ref[...] = acc
        loss_ref[...] = jnp.full(loss_ref.shape, jnp.sum(acc[lo:hi, :]), F32)

    vmem = pl.BlockSpec(memory_space=pltpu.VMEM)
    return pl.pallas_call(
        body, name=name, in_specs=[vmem], out_specs=[vmem, vmem],
        out_shape=[SDS((rows, lanes), F32), SDS((SUBLANES, LANES), F32)],
        scratch_shapes=[pltpu.VMEM((N_DEV, rows, lanes), F32), pltpu.SemaphoreType.DMA((N_DEV - 1,)),
                        pltpu.SemaphoreType.DMA((N_DEV - 1,))],
    )(part)


def _adamw(w, g, m, v, name):
    rows, cols = w.shape
    tr = rows if rows % SUBLANES else _largest_divisor(rows, 512, SUBLANES)

    def body(w_ref, g_ref, m_ref, v_ref, d_ref, nm_ref, nv_ref):
        gv = g_ref[...]
        nm = ADAM_B1 * m_ref[...] + (1.0 - ADAM_B1) * gv
        nv = ADAM_B2 * v_ref[...] + (1.0 - ADAM_B2) * (gv * gv)
        m_hat = nm / (1.0 - ADAM_B1 ** ADAM_STEP)
        v_hat = nv / (1.0 - ADAM_B2 ** ADAM_STEP)
        d_ref[...] = -ADAM_LR * (m_hat / (jnp.sqrt(v_hat) + ADAM_EPS) + ADAM_WD * w_ref[...])
        nm_ref[...] = nm
        nv_ref[...] = nv

    spec = pl.BlockSpec((tr, cols), lambda i: (i, 0))
    return pl.pallas_call(
        body, name=name, grid=(rows // tr,), in_specs=[spec] * 4, out_specs=[spec] * 3,
        out_shape=[SDS((rows, cols), F32)] * 3, compiler_params=_cparams(),
    )(w, g, m, v)


def _adamw_nd(w, g, m, v, name):
    shape = w.shape
    two_d = (1, shape[0]) if len(shape) == 1 else (-1, shape[-1])
    outs = _adamw(w.reshape(two_d), g.reshape(two_d), m.reshape(two_d), v.reshape(two_d), name)
    return [o.reshape(shape) for o in outs]


def _rope_tables(seq):
    pos = jnp.arange(seq, dtype=F32)
    inv_freq = 1.0 / (ROPE_THETA ** (jnp.arange(0, HEAD_DIM, 2, dtype=F32) / HEAD_DIM))
    ang = pos[:, None] * inv_freq[None, :]
    cos, sin = jnp.cos(ang), jnp.sin(ang)
    reps = LANES // HEAD_DIM
    cos_t = jnp.tile(jnp.concatenate([cos, cos], axis=1), (1, reps))
    sin_t = jnp.tile(jnp.concatenate([-sin, sin], axis=1), (1, reps))
    return cos_t, sin_t


def _flat_pad(a):
    flat = a.reshape(1, -1)
    pad = (-flat.shape[1]) % LANES
    return jnp.pad(flat, ((0, 0), (0, pad))) if pad else flat


def kernel(x, norm_mix, norm_ffn, norm_final, conv_w_in, conv_w_conv, conv_w_out, attn_w_qkv, attn_b_qkv, attn_sinks, attn_w_o, attn_b_o, ffn_w_in, ffn_w_conv, ffn_w_down, loss_target, m_norm_mix, m_norm_ffn, m_norm_final, m_conv_w_in, m_conv_w_conv, m_conv_w_out, m_attn_w_qkv, m_attn_b_qkv, m_attn_sinks, m_attn_w_o, m_attn_b_o, m_ffn_w_in, m_ffn_w_conv, m_ffn_w_down, v_norm_mix, v_norm_ffn, v_norm_final, v_conv_w_in, v_conv_w_conv, v_conv_w_out, v_attn_w_qkv, v_attn_b_qkv, v_attn_sinks, v_attn_w_o, v_attn_b_o, v_ffn_w_in, v_ffn_w_conv, v_ffn_w_down):
    b_loc, seq, d = x.shape
    depth = norm_mix.shape[0]
    n_conv, n_attn = conv_w_in.shape[0], attn_w_qkv.shape[0]
    t_all = b_loc * seq
    my_x, my_y, my_c = _my_place()

    me = 4 * my_x + 2 * my_y + my_c

    groups = []
    for i in range(depth):
        j = i // 2
        if i % 2 == 0:
            mix = [("conv_w_in", j, True, conv_w_in[j].T), ("conv_w_out", j, False, conv_w_out[j])]
        else:
            mix = [("attn_w_qkv", j, True, attn_w_qkv[j].T), ("attn_w_o", j, False, attn_w_o[j])]
        groups.append((("mix", i), mix))
        groups.append((("ffn", i), [("ffn_w_in", i, True, ffn_w_in[i].T), ("ffn_w_down", i, False, ffn_w_down[i])]))
    order = [key for key, _ in groups]
    members_of = dict(groups)

    def layout(key):
        offs, o = [], 0
        for _, _, _, shard in members_of[key]:
            n = shard.shape[0]
            o = -(-o // n) * n
            offs.append(o)
            o += n
        return offs, o

    small = jnp.concatenate([_flat_pad(conv_w_conv), _flat_pad(ffn_w_conv), _flat_pad(attn_b_qkv), _flat_pad(attn_b_o)], axis=1)
    (small_g,) = _all_gather([small], "gather_small")

    gather_started = {}

    def start_gather(idx, after):
        if idx >= len(order):
            return 0.0
        key = order[idx]
        offs, total = layout(key)
        pieces, o = [], 0
        for (_, _, _, shard), off in zip(members_of[key], offs):
            if off > o:
                pieces.append(jnp.zeros((off - o, d), shard.dtype))
            pieces.append(shard)
            o = off + shard.shape[0]
        pack = jnp.concatenate(pieces, axis=0).astype(BF16)
        if idx < 2:
            gather_started[key] = _gather2_start(pack, after, f"gather_start_{key[0]}_{key[1]}")
        else:
            gather_started[key] = _exchange_start(pack, after, False, f"gather_start_{key[0]}_{key[1]}")
        return gather_started[key][4][0, 0]

    weights = {}

    def finish_gather(key, after):
        if key in order[:2]:
            forwarded = _gather2_forward(gather_started[key], after, f"gather_forward_{key[0]}_{key[1]}")
            land = _gather2_wait(gather_started[key], forwarded, f"gather_wait_{key[0]}_{key[1]}")
        else:
            _, land = _exchange_wait(gather_started[key], after, False, f"gather_wait_{key[0]}_{key[1]}")
        for (wname, layer, _, shard), off in zip(members_of[key], layout(key)[0]):
            weights[(wname, layer)] = _Rows(land, off, shard.shape[0])

    def take_small(o, shape):
        size = shape[0] * shape[1] * shape[2]
        blk = small_g[:, 0, o:o + size].reshape((N_DEV,) + shape)
        return jnp.moveaxis(blk, 0, 2).reshape(shape[0], shape[1], N_DEV * shape[2])

    so = 0
    wc_conv_full = take_small(so, conv_w_conv.shape); so += _flat_pad(conv_w_conv).shape[1]
    wc_ffn_full = take_small(so, ffn_w_conv.shape); so += _flat_pad(ffn_w_conv).shape[1]
    b_qkv_full = take_small(so, (n_attn, 1, attn_b_qkv.shape[1]))[:, 0]; so += _flat_pad(attn_b_qkv).shape[1]
    b_o_full = take_small(so, (n_attn, 1, attn_b_o.shape[1]))[:, 0]

    cos_t, sin_t = _rope_tables(seq)

    xs = [x.reshape(t_all, d)]
    saved = []
    token = start_gather(0, small_g)
    for i in range(depth):
        j = i // 2
        if i > 0:
            token = start_gather(2 * i + 2, xs[-1])
        gain_mix = norm_mix[i][None, :] + token
        finish_gather(("mix", i), gain_mix if i == 0 else xs[-1])
        if i == 0:
            gain_mix = gain_mix + start_gather(1, weights[(members_of[order[0]][0][0], 0)].arr)
        if i % 2 == 0:
            x1, *mix_saved = _fwd_conv_mixer(xs[-1], gain_mix, weights[("conv_w_in", j)], wc_conv_full[j],
                                             weights[("conv_w_out", j)], seq, f"fwd_conv_{i}")
        else:
            qkv = _fwd_qkv(xs[-1], gain_mix, weights[("attn_w_qkv", j)], b_qkv_full[j][None, :], cos_t, sin_t, seq,
                           f"fwd_qkv_{i}")
            x1, o = _fwd_attention(qkv, xs[-1], attn_sinks[j], weights[("attn_w_o", j)], b_o_full[j][None, :], seq,
                                   f"fwd_attn_{i}")
            mix_saved = (qkv, o)
        token = start_gather(2 * i + 3, x1) + (start_gather(2, x1) if i == 0 else 0.0)
        gain_ffn = norm_ffn[i][None, :] + token
        finish_gather(("ffn", i), gain_ffn)
        x2, *ffn_saved = _fwd_ffn(x1, gain_ffn, weights[("ffn_w_in", i)], wc_ffn_full[i], weights[("ffn_w_down", i)],
                                  seq, f"fwd_ffn_{i}")
        saved.append((xs[-1], x1, mix_saved, ffn_saved))
        xs.append(x2)
        token = 0.0

    dx, dg_final, loss_lanes = _final_norm_loss(xs[-1], norm_final[None, :], loss_target.reshape(t_all, d), "loss_head")

    dg_mix, dg_ffn = [None] * depth, [None] * depth
    dwc_conv, dwc_ffn = [None] * n_conv, [None] * depth
    db_qkv, db_o, dsinks = [None] * n_attn, [None] * n_attn, [None] * n_attn
    scatter_started = {}

    def weight_grads(key, operands):
        offs, total = layout(key)
        parts = (N_DEV, total, d)
        for (wname, layer, _, shard), off, (a, b) in zip(members_of[key], offs, operands):
            parts = _tn_matmul(a, b, _Rows(parts, off, shard.shape[0]), f"dw_{wname}_{layer}")
        scatter_started[key] = _exchange_start(parts, operands[0][1], True, f"scatter_start_{key[0]}_{key[1]}")
        return scatter_started[key][4][0, 0]

    token = 0.0
    for i in reversed(range(depth)):
        j = i // 2
        x0, x1, mix_saved, (gate, s_act, uds, act) = saved[i]
        dgu, dwc = _bwd_ffn_inner(dx, gate, s_act, uds, wc_ffn_full[i] + token, weights[("ffn_w_down", i)], seq, f"bwd_ffn_{i}")
        dwc_ffn[i] = dwc[:3]
        dx1, h2, dg_ffn[i] = _bwd_dense_norm(dgu, weights[("ffn_w_in", i)], x1, norm_ffn[i][None, :], dx, f"bwd_ffn_norm_{i}")
        token = weight_grads(("ffn", i), [(dgu, h2), (act, dx)])
        if i % 2 == 0:
            bcv, cc, y = mix_saved
            dbcv, dwc = _bwd_conv_inner(dx1, bcv, cc, wc_conv_full[j] + token, weights[("conv_w_out", j)], seq, f"bwd_conv_{i}")
            dwc_conv[j] = dwc[:3]
            dx, h, dg_mix[i] = _bwd_dense_norm(dbcv, weights[("conv_w_in", j)], x0, norm_mix[i][None, :], dx1,
                                               f"bwd_conv_norm_{i}")
            token = weight_grads(("mix", i), [(dbcv, h), (y, dx1)])
        else:
            qkv, o = mix_saved
            dqkv, dsk, dbq, dbo = _bwd_attention_inner(dx1, qkv, attn_sinks[j] + token, weights[("attn_w_o", j)], cos_t, sin_t,
                                                       seq, f"bwd_attn_{i}")
            dsinks[j], db_qkv[j], db_o[j] = dsk[0:1, :attn_sinks.shape[1]], dbq, dbo
            dx, h, dg_mix[i] = _bwd_dense_norm(dqkv, weights[("attn_w_qkv", j)], x0, norm_mix[i][None, :], dx1,
                                               f"bwd_attn_norm_{i}")
            token = weight_grads(("mix", i), [(dqkv, h), (o, dx1)])
    grad_x = dx.reshape(b_loc, seq, d)

    reduced = {}

    def finish_scatter(key, after):
        _, land = _exchange_wait(scatter_started[key], after, True, f"scatter_wait_{key[0]}_{key[1]}")
        total = _sum_slots(land, f"scatter_sum_{key[0]}_{key[1]}")
        for (wname, layer, transposed, shard), off in zip(members_of[key], layout(key)[0]):
            rows = total[off:off + shard.shape[0]]
            reduced[(wname, layer)] = rows.T if transposed else rows

    last_key = order[0]
    for key in reversed(order[1:]):
        finish_scatter(key, dx)

    small_parts = [jnp.concatenate(dg_mix, axis=0), jnp.concatenate(dg_ffn, axis=0), dg_final,
                   jnp.stack(dwc_conv), jnp.stack(dwc_ffn), jnp.concatenate(db_qkv, axis=0), jnp.concatenate(db_o, axis=0),
                   jnp.concatenate(dsinks, axis=0), loss_lanes]
    flats = [_flat_pad(p) for p in small_parts]
    bounds = []
    so = 0
    for fl in flats:
        bounds.append((so, so + fl.shape[1]))
        so += fl.shape[1]
    small_rows = so // LANES
    pad_rows = (-small_rows) % SUBLANES
    part_small = jnp.pad(jnp.concatenate(flats, axis=1).reshape(small_rows, LANES), ((0, pad_rows), (0, 0)))
    loss_rows = (bounds[-1][0] // LANES, bounds[-1][1] // LANES)
    summed, loss_tile = _all_reduce_small(part_small, loss_rows, "reduce_small")
    summed = summed.reshape(1, -1)

    def small_grad(k, shape):
        lo = bounds[k][0]
        size = 1
        for s_ in shape:
            size *= s_
        return summed[0, lo:lo + size].reshape(shape)

    def my_cols(full, n_local):
        return lax.dynamic_slice_in_dim(full, me * n_local, n_local, axis=full.ndim - 1)

    g_norm_mix = small_grad(0, norm_mix.shape)
    g_norm_ffn = small_grad(1, norm_ffn.shape)
    g_norm_final = small_grad(2, norm_final.shape)
    g_conv_w_conv = my_cols(small_grad(3, (n_conv, 3, d)), conv_w_conv.shape[2])
    g_ffn_w_conv = my_cols(small_grad(4, (depth, 3, ffn_w_conv.shape[2] * N_DEV)), ffn_w_conv.shape[2])
    g_attn_b_qkv = my_cols(small_grad(5, (n_attn, attn_b_qkv.shape[1] * N_DEV)), attn_b_qkv.shape[1])
    g_attn_b_o = my_cols(small_grad(6, (n_attn, d)), attn_b_o.shape[1])
    g_attn_sinks = small_grad(7, attn_sinks.shape)
    loss = loss_tile[0, 0]

    def big_grad(wname, n_layers):
        return jnp.stack([reduced[(wname, layer)] for layer in range(n_layers)])

    grads = {
        "norm_mix": g_norm_mix, "norm_ffn": g_norm_ffn, "norm_final": g_norm_final, "conv_w_conv": g_conv_w_conv,
        "attn_w_qkv": big_grad("attn_w_qkv", n_attn), "attn_b_qkv": g_attn_b_qkv, "attn_sinks": g_attn_sinks,
        "attn_w_o": big_grad("attn_w_o", n_attn), "attn_b_o": g_attn_b_o,
        "ffn_w_in": big_grad("ffn_w_in", depth), "ffn_w_conv": g_ffn_w_conv, "ffn_w_down": big_grad("ffn_w_down", depth),
    }
    params = {
        "norm_mix": (norm_mix, m_norm_mix, v_norm_mix), "norm_ffn": (norm_ffn, m_norm_ffn, v_norm_ffn),
        "norm_final": (norm_final, m_norm_final, v_norm_final), "conv_w_in": (conv_w_in, m_conv_w_in, v_conv_w_in),
        "conv_w_conv": (conv_w_conv, m_conv_w_conv, v_conv_w_conv), "conv_w_out": (conv_w_out, m_conv_w_out, v_conv_w_out),
        "attn_w_qkv": (attn_w_qkv, m_attn_w_qkv, v_attn_w_qkv), "attn_b_qkv": (attn_b_qkv, m_attn_b_qkv, v_attn_b_qkv),
        "attn_sinks": (attn_sinks, m_attn_sinks, v_attn_sinks), "attn_w_o": (attn_w_o, m_attn_w_o, v_attn_w_o),
        "attn_b_o": (attn_b_o, m_attn_b_o, v_attn_b_o), "ffn_w_in": (ffn_w_in, m_ffn_w_in, v_ffn_w_in),
        "ffn_w_conv": (ffn_w_conv, m_ffn_w_conv, v_ffn_w_conv), "ffn_w_down": (ffn_w_down, m_ffn_w_down, v_ffn_w_down),
    }
    names = list(params)
    updates = {}

    def update(wname):
        w, m, v = params[wname]
        updates[wname] = _adamw_nd(w, grads[wname], m, v, f"adamw_{wname}")

    last_names = sorted({wname for wname, _, _, _ in members_of[last_key]})
    for wname in names:
        if wname not in last_names:
            update(wname)
    finish_scatter(last_key, updates["ffn_w_in"][0])
    for wname in last_names:
        grads[wname] = big_grad(wname, params[wname][0].shape[0])
        update(wname)
    return (loss, grad_x, *[grads[wname] for wname in names], *[updates[wname][0] for wname in names],
            *[updates[wname][1] for wname in names], *[updates[wname][2] for wname in names])
```

```python
from typing import NamedTuple

import jax
import jax.numpy as jnp
from jax import lax
from jax.experimental import pallas as pl
from jax.experimental.pallas import tpu as pltpu

F32 = jnp.float32
BF16 = jnp.bfloat16
SDS = jax.ShapeDtypeStruct
MESH = pl.DeviceIdType.MESH
ANY = pl.BlockSpec(memory_space=pl.ANY)

N_DEV = 8
EPS = 1e-5
HEAD_DIM = 64
GROUP = 4
WINDOW = 128
ROPE_THETA = 10000.0
ADAM_LR, ADAM_B1, ADAM_B2, ADAM_EPS, ADAM_WD, ADAM_STEP = 0.001, 0.9, 0.999, 1e-08, 0.01, 10

V7X_VMEM_BYTES = 64 * 1024 * 1024
VMEM_LIMIT_BYTES = V7X_VMEM_BYTES - 8 * 1024 * 1024
LANES = 128
SUBLANES = 8
TOKEN_TILE = 512
TN_TOKEN_TILE = 2048
MASKED_SCORE = -1e30


def _cparams(n_axes=1):
    return pltpu.CompilerParams(dimension_semantics=("arbitrary",) * n_axes, vmem_limit_bytes=VMEM_LIMIT_BYTES)


def _resident(shape):
    zeros = (0,) * len(shape)
    return pl.BlockSpec(shape, lambda *_: zeros, pipeline_mode=pl.Buffered(1))


class _Rows(NamedTuple):
    arr: jax.Array
    off: int
    n: int


def _rows_spec(w):
    assert w.off % w.n == 0
    block = w.off // w.n
    return pl.BlockSpec((N_DEV, w.n, w.arr.shape[2]), lambda *_: (0, block, 0), pipeline_mode=pl.Buffered(1))


def _mat(ref):
    v = ref[...]
    return v.reshape(v.shape[0] * v.shape[1], v.shape[2])


def _token_tile(seq):
    return min(TOKEN_TILE, seq // 2)


def _largest_divisor(m, cap, mult):
    best = None
    for d in range(mult, min(m, cap) + 1, mult):
        if m % d == 0:
            best = d
    return m if best is None else best


def _nt(a, b):
    return lax.dot_general(a, b, (((1,), (1,)), ((), ())), preferred_element_type=F32)


def _nn(a, b):
    return lax.dot_general(a, b, (((1,), (0,)), ((), ())), preferred_element_type=F32)


def _tn(a, b):
    return lax.dot_general(a, b, (((0,), (0,)), ((), ())), preferred_element_type=F32)


def _rms_parts(xv):
    r = lax.rsqrt(jnp.mean(xv * xv, axis=-1, keepdims=True) + EPS)
    return r, xv * r


def _rms_backward(dh, xh, r, gain, dres):
    u = dh * gain
    return dres + r * (u - xh * jnp.mean(u * xh, axis=-1, keepdims=True))


def _shifted_rows(xv, edge, k, down):
    n = xv.shape[0]
    row = lax.broadcasted_iota(jnp.int32, edge.shape, 0)
    if down:
        rolled = pltpu.roll(xv, k, 0)
        head = jnp.where(row < k, pltpu.roll(edge, k, 0), rolled[0:SUBLANES])
        return jnp.concatenate([head, rolled[SUBLANES:]], axis=0)
    rolled = pltpu.roll(xv, n - k, 0)
    tail = jnp.where(row >= SUBLANES - k, pltpu.roll(edge, SUBLANES - k, 0), rolled[n - SUBLANES:])
    return jnp.concatenate([rolled[:n - SUBLANES], tail], axis=0)


def _causal_conv3(edge_ref, xv, w_ref):
    before = edge_ref[...]
    y = (w_ref[2:3, :] * xv + w_ref[1:2, :] * _shifted_rows(xv, before, 1, True)
         + w_ref[0:1, :] * _shifted_rows(xv, before, 2, True))
    edge_ref[...] = xv[xv.shape[0] - SUBLANES:, :]
    return y


def _sigmoid(z):
    return 1.0 / (1.0 + jnp.exp(-z))


def _fwd_conv_mixer(x, gain, w_in_t, w_conv, w_out, seq, name):
    t_all, d = x.shape
    tt = _token_tile(seq)
    tps = seq // tt

    def body(x_ref, g_ref, win_ref, wc_ref, wout_ref, x1_ref, bcv_ref, cc_ref, y_ref, ext_ref):
        i = pl.program_id(0)
        xv = x_ref[...]
        r, xh = _rms_parts(xv)
        h = (xh * g_ref[...]).astype(BF16)
        bcv = _nt(h, _mat(win_ref))
        bcv_ref[...] = bcv.astype(BF16)

        @pl.when(i % tps == 0)
        def _():
            ext_ref[...] = jnp.zeros_like(ext_ref)

        cc = _causal_conv3(ext_ref, bcv[:, d:2 * d] * bcv[:, 2 * d:], wc_ref)
        cc_ref[...] = cc.astype(BF16)
        y = (bcv[:, :d] * cc).astype(BF16)
        y_ref[...] = y
        x1_ref[...] = xv + _nn(y, _mat(wout_ref))

    tile = pl.BlockSpec((tt, d), lambda i: (i, 0))
    return pl.pallas_call(
        body, name=name, grid=(t_all // tt,),
        in_specs=[tile, _resident((1, d)), _rows_spec(w_in_t), _resident((3, d)), _rows_spec(w_out)],
        out_specs=[tile, pl.BlockSpec((tt, 3 * d), lambda i: (i, 0)), tile, tile],
        out_shape=[SDS((t_all, d), F32), SDS((t_all, 3 * d), BF16), SDS((t_all, d), BF16), SDS((t_all, d), BF16)],
        scratch_shapes=[pltpu.VMEM((SUBLANES, d), F32)],
        compiler_params=_cparams(),
    )(x, gain, w_in_t.arr, w_conv, w_out.arr)


def _fwd_ffn(x, gain, w_in_t, w_conv, w_down, seq, name):
    t_all, d = x.shape
    f = w_down.n * N_DEV
    tt = _token_tile(seq) // 2
    tps = seq // tt

    def body(x_ref, g_ref, win_ref, wc_ref, wd_ref, x2_ref, gate_ref, s_ref, uds_ref, a_ref, ext_ref):
        i = pl.program_id(0)
        xv = x_ref[...]
        r, xh = _rms_parts(xv)
        h = (xh * g_ref[...]).astype(BF16)
        gu = _nt(h, _mat(win_ref))
        gate = gu[:, :f]
        u = gu[:, f:]
        gate_ref[...] = gate.astype(BF16)

        @pl.when(i % tps == 0)
        def _():
            ext_ref[...] = jnp.zeros_like(ext_ref)

        gc = _causal_conv3(ext_ref, gate, wc_ref)
        sig = _sigmoid(gc)
        s = gc * sig
        s_ref[...] = s.astype(BF16)
        uds_ref[...] = (u * (sig * (1.0 + gc * (1.0 - sig)))).astype(BF16)
        a = (s * u).astype(BF16)
        a_ref[...] = a
        x2_ref[...] = xv + _nn(a, _mat(wd_ref))

    wide = pl.BlockSpec((tt, f), lambda i: (i, 0))
    return pl.pallas_call(
        body, name=name, grid=(t_all // tt,),
        in_specs=[pl.BlockSpec((tt, d), lambda i: (i, 0)), _resident((1, d)), _rows_spec(w_in_t),
                  _resident((3, f)), _rows_spec(w_down)],
        out_specs=[pl.BlockSpec((tt, d), lambda i: (i, 0)), wide, wide, wide, wide],
        out_shape=[SDS((t_all, d), F32)] + [SDS((t_all, f), BF16)] * 4,
        scratch_shapes=[pltpu.VMEM((SUBLANES, f), F32)],
        compiler_params=_cparams(),
    )(x, gain, w_in_t.arr, w_conv, w_down.arr)


def _rope_partner(xs, lane_lo):
    return jnp.where(lane_lo, pltpu.roll(xs, LANES - HEAD_DIM // 2, 1), pltpu.roll(xs, HEAD_DIM // 2, 1))


def _fwd_qkv(x, gain, w_qkv_t, b_qkv, cos_t, sin_t, seq, name):
    t_all, d = x.shape
    width = w_qkv_t.n * N_DEV
    kvw = (width - d) // 2
    tt = _token_tile(seq)
    tps = seq // tt
    scale = HEAD_DIM ** -0.5

    def body(x_ref, g_ref, w_ref, b_ref, cos_ref, sin_ref, qkv_ref):
        xv = x_ref[...]
        r, xh = _rms_parts(xv)
        h = (xh * g_ref[...]).astype(BF16)
        qkv = _nt(h, _mat(w_ref)) + b_ref[...]
        cosv = cos_ref[...]
        sinv = sin_ref[...]
        lane_lo = (lax.broadcasted_iota(jnp.int32, (tt, LANES), 1) % HEAD_DIM) < HEAD_DIM // 2
        for s in range((d + kvw) // LANES):
            xs = qkv[:, s * LANES:(s + 1) * LANES]
            roped = xs * cosv + _rope_partner(xs, lane_lo) * sinv
            if s * LANES < d:
                roped = roped * scale
            qkv_ref[:, s * LANES:(s + 1) * LANES] = roped.astype(BF16)
        qkv_ref[:, d + kvw:] = qkv[:, d + kvw:].astype(BF16)

    return pl.pallas_call(
        body, name=name, grid=(t_all // tt,),
        in_specs=[pl.BlockSpec((tt, d), lambda i: (i, 0)), _resident((1, d)), _rows_spec(w_qkv_t),
                  _resident((1, width)), pl.BlockSpec((tt, LANES), lambda i: (i % tps, 0)),
                  pl.BlockSpec((tt, LANES), lambda i: (i % tps, 0))],
        out_specs=pl.BlockSpec((tt, width), lambda i: (i, 0)),
        out_shape=SDS((t_all, width), BF16),
        compiler_params=_cparams(),
    )(x, gain, w_qkv_t.arr, b_qkv, cos_t, sin_t)


def _stack_heads(ref, row0, kh):
    return jnp.concatenate(
        [ref[row0:row0 + WINDOW, (kh * GROUP + g) * HEAD_DIM:(kh * GROUP + g + 1) * HEAD_DIM] for g in range(GROUP)],
        axis=0)


def _band_bias():
    r = lax.broadcasted_iota(jnp.int32, (WINDOW, 2 * WINDOW), 0)
    j = lax.broadcasted_iota(jnp.int32, (WINDOW, 2 * WINDOW), 1)
    base = (j > r) & (j <= r + WINDOW)
    return jnp.where(base, 0.0, MASKED_SCORE), jnp.where(base & (j >= WINDOW), 0.0, MASKED_SCORE)


def _fwd_attention(qkv, x, sinks, w_o, b_o, seq, name):
    t_all, d = x.shape
    width = qkv.shape[1]
    kvw = (width - d) // 2
    n_kv = kvw // HEAD_DIM
    tt = _token_tile(seq)
    tps = seq // tt
    nblk = tt // WINDOW

    def body(sink_ref, qkv_ref, kvp_ref, x_ref, wo_ref, bo_ref, x1_ref, o_ref, p_ref, psink_ref,
             kvext_ref, oscr_ref, bias_ref, s_ref):
        i = pl.program_id(0)

        @pl.when(i == 0)
        def _():
            base, first = _band_bias()
            bias_ref[0], bias_ref[1] = base.T, first.T

        kvext_ref[0:WINDOW, :] = kvp_ref[...]
        kvext_ref[WINDOW:, :] = qkv_ref[:, d:]
        at_seq_start = (i % tps == 0).astype(jnp.int32)
        steps = [(n, kh) for n in range(nblk) for kh in range(n_kv)]

        def scores(step):
            n, kh = steps[step]
            qs = _stack_heads(qkv_ref, n * WINDOW, kh)
            kb = kvext_ref[n * WINDOW:(n + 2) * WINDOW, kh * HEAD_DIM:(kh + 1) * HEAD_DIM]
            s_ref[step % 2] = _nt(kb, qs)

        scores(0)
        for step, (n, kh) in enumerate(steps):
            buf = step % 2
            if step + 1 < len(steps):
                scores(step + 1)
            vb = kvext_ref[n * WINDOW:(n + 2) * WINDOW, kvw + kh * HEAD_DIM:kvw + (kh + 1) * HEAD_DIM]
            bias = bias_ref[at_seq_start if n == 0 else 0]
            for g in range(GROUP):
                cols = slice(g * WINDOW, (g + 1) * WINDOW)
                sink = sink_ref[kh * GROUP + g]
                sv = s_ref[buf, :, cols] + bias
                m = jnp.maximum(jnp.max(sv, axis=0, keepdims=True), sink)
                p = jnp.exp(sv - m)
                e_sink = jnp.exp(sink - m)
                inv = 1.0 / (jnp.sum(p, axis=0, keepdims=True) + e_sink)
                p_ref[n, kh, :, cols] = (p * inv).astype(BF16)
                psink_ref[n, kh, g:g + 1, :] = e_sink * inv
            o_s = _tn(vb, p_ref[n, kh]).T
            for g in range(GROUP):
                hd = kh * GROUP + g
                oscr_ref[n * WINDOW:(n + 1) * WINDOW, hd * HEAD_DIM:(hd + 1) * HEAD_DIM] = (
                    o_s[g * WINDOW:(g + 1) * WINDOW].astype(BF16))
        o = oscr_ref[...]
        o_ref[...] = o
        x1_ref[...] = x_ref[...] + _nn(o, _mat(wo_ref)) + bo_ref[...]

    kv_blocks = tt // WINDOW
    return pl.pallas_call(
        body, name=name, grid=(t_all // tt,),
        in_specs=[pl.BlockSpec(memory_space=pltpu.SMEM),
                  pl.BlockSpec((tt, width), lambda i: (i, 0)),
                  pl.BlockSpec((WINDOW, 2 * kvw), lambda i: (jnp.maximum(i * kv_blocks - 1, 0), d // (2 * kvw))),
                  pl.BlockSpec((tt, d), lambda i: (i, 0)), _rows_spec(w_o), _resident((1, d))],
        out_specs=[pl.BlockSpec((tt, d), lambda i: (i, 0)), pl.BlockSpec((tt, d), lambda i: (i, 0)),
                   pl.BlockSpec((nblk, n_kv, 2 * WINDOW, GROUP * WINDOW), lambda i: (i, 0, 0, 0)),
                   pl.BlockSpec((nblk, n_kv, GROUP, WINDOW), lambda i: (i, 0, 0, 0))],
        out_shape=[SDS((t_all, d), F32), SDS((t_all, d), BF16),
                   SDS((t_all // WINDOW, n_kv, 2 * WINDOW, GROUP * WINDOW), BF16),
                   SDS((t_all // WINDOW, n_kv, GROUP, WINDOW), F32)],
        scratch_shapes=[pltpu.VMEM((tt + WINDOW, 2 * kvw), BF16), pltpu.VMEM((tt, d), BF16),
                        pltpu.VMEM((2, 2 * WINDOW, WINDOW), F32), pltpu.VMEM((2, 2 * WINDOW, GROUP * WINDOW), F32)],
        compiler_params=_cparams(),
    )(sinks, qkv, qkv, x, w_o.arr, b_o)


def _final_norm_loss(x, gain, target, name):
    t_all, d = x.shape
    tt = min(TOKEN_TILE, t_all)

    def body(x_ref, g_ref, t_ref, dx_ref, dg_ref, loss_ref):
        i = pl.program_id(0)
        xv = x_ref[...]
        r, xh = _rms_parts(xv)
        gain_v = g_ref[...]
        e = xh * gain_v - t_ref[...]
        dy = e * (1.0 / d)
        dx_ref[...] = _rms_backward(dy, xh, r, gain_v, 0.0)

        @pl.when(i == 0)
        def _():
            dg_ref[...] = jnp.zeros_like(dg_ref)
            loss_ref[...] = jnp.zeros_like(loss_ref)

        dg_ref[...] += jnp.sum(dy * xh, axis=0, keepdims=True)
        loss_ref[...] += (0.5 / d) * jnp.sum(e * e, axis=0, keepdims=True)

    return pl.pallas_call(
        body, name=name, grid=(t_all // tt,),
        in_specs=[pl.BlockSpec((tt, d), lambda i: (i, 0)), _resident((1, d)), pl.BlockSpec((tt, d), lambda i: (i, 0))],
        out_specs=[pl.BlockSpec((tt, d), lambda i: (i, 0)), pl.BlockSpec((1, d), lambda i: (0, 0)),
                   pl.BlockSpec((1, d), lambda i: (0, 0))],
        out_shape=[SDS((t_all, d), F32), SDS((1, d), F32), SDS((1, d), F32)],
        compiler_params=_cparams(),
    )(x, gain, target)


def _bwd_ffn_inner(dx2, gate, s_act, uds, w_conv, w_down, seq, name):
    t_all, d = dx2.shape
    f = w_down.n * N_DEV
    tt = _token_tile(seq)
    tps = seq // tt
    nt = t_all // tt

    def body(dx_ref, g_ref, s_ref, uds_ref, wc_ref, wd_ref, dgu_ref, dwc_ref, aext_ref, da_ref):
        i = pl.program_id(0)
        ti = nt - 1 - i
        da_ref[...] = _nt(dx_ref[...].astype(BF16), _mat(wd_ref))

        @pl.when(ti % tps == tps - 1)
        def _():
            aext_ref[...] = jnp.zeros_like(aext_ref)

        @pl.when(i == 0)
        def _():
            dwc_ref[...] = jnp.zeros_like(dwc_ref)

        for c in range(f // LANES):
            cols = slice(c * LANES, (c + 1) * LANES)
            da = da_ref[:, cols]
            g = g_ref[:, cols].astype(F32)
            dgc = da * uds_ref[:, cols].astype(F32)
            after = aext_ref[:, cols]
            sh1 = _shifted_rows(dgc, after, 1, False)
            sh2 = _shifted_rows(dgc, after, 2, False)
            aext_ref[:, cols] = dgc[0:SUBLANES, :]
            dg = wc_ref[2:3, cols] * dgc + wc_ref[1:2, cols] * sh1 + wc_ref[0:1, cols] * sh2
            dgu_ref[:, cols] = dg.astype(BF16)
            dgu_ref[:, f + c * LANES:f + (c + 1) * LANES] = (da * s_ref[:, cols].astype(F32)).astype(BF16)
            dwc_ref[0:1, cols] += jnp.sum(g * sh2, axis=0, keepdims=True)
            dwc_ref[1:2, cols] += jnp.sum(g * sh1, axis=0, keepdims=True)
            dwc_ref[2:3, cols] += jnp.sum(g * dgc, axis=0, keepdims=True)

    rev = lambda i: (nt - 1 - i, 0)
    return pl.pallas_call(
        body, name=name, grid=(nt,),
        in_specs=[pl.BlockSpec((tt, d), rev)] + [pl.BlockSpec((tt, f), rev)] * 3 + [_resident((3, f)), _rows_spec(w_down)],
        out_specs=[pl.BlockSpec((tt, 2 * f), rev), pl.BlockSpec((8, f), lambda i: (0, 0))],
        out_shape=[SDS((t_all, 2 * f), BF16), SDS((8, f), F32)],
        scratch_shapes=[pltpu.VMEM((SUBLANES, f), F32), pltpu.VMEM((tt, f), F32)],
        compiler_params=_cparams(),
    )(dx2, gate, s_act, uds, w_conv, w_down.arr)


def _bwd_conv_inner(dx1, bcv, cc, w_conv, w_out, seq, name):
    t_all, d = dx1.shape
    tt = _token_tile(seq)
    tps = seq // tt
    nt = t_all // tt

    def body(dx_ref, bcv_ref, cc_ref, wc_ref, wout_ref, dbcv_ref, dwc_ref, aext_ref, dy_ref):
        i = pl.program_id(0)
        ti = nt - 1 - i
        dy_ref[...] = _nt(dx_ref[...].astype(BF16), _mat(wout_ref))

        @pl.when(ti % tps == tps - 1)
        def _():
            aext_ref[...] = jnp.zeros_like(aext_ref)

        @pl.when(i == 0)
        def _():
            dwc_ref[...] = jnp.zeros_like(dwc_ref)

        for s in range(d // LANES):
            cols = slice(s * LANES, (s + 1) * LANES)
            ccols = slice(d + s * LANES, d + (s + 1) * LANES)
            vcols = slice(2 * d + s * LANES, 2 * d + (s + 1) * LANES)
            dy = dy_ref[:, cols]
            c = bcv_ref[:, ccols].astype(F32)
            v = bcv_ref[:, vcols].astype(F32)
            cv = c * v
            dcc = dy * bcv_ref[:, cols].astype(F32)
            after = aext_ref[:, cols]
            sh1 = _shifted_rows(dcc, after, 1, False)
            sh2 = _shifted_rows(dcc, after, 2, False)
            aext_ref[:, cols] = dcc[0:SUBLANES, :]
            dcv = wc_ref[2:3, cols] * dcc + wc_ref[1:2, cols] * sh1 + wc_ref[0:1, cols] * sh2
            dbcv_ref[:, cols] = (dy * cc_ref[:, cols].astype(F32)).astype(BF16)
            dbcv_ref[:, ccols] = (dcv * v).astype(BF16)
            dbcv_ref[:, vcols] = (dcv * c).astype(BF16)
            dwc_ref[0:1, cols] += jnp.sum(cv * sh2, axis=0, keepdims=True)
            dwc_ref[1:2, cols] += jnp.sum(cv * sh1, axis=0, keepdims=True)
            dwc_ref[2:3, cols] += jnp.sum(cv * dcc, axis=0, keepdims=True)

    return pl.pallas_call(
        body, name=name, grid=(nt,),
        in_specs=[pl.BlockSpec((tt, d), lambda i: (nt - 1 - i, 0)),
                  pl.BlockSpec((tt, 3 * d), lambda i: (nt - 1 - i, 0)),
                  pl.BlockSpec((tt, d), lambda i: (nt - 1 - i, 0)),
                  _resident((3, d)), _rows_spec(w_out)],
        out_specs=[pl.BlockSpec((tt, 3 * d), lambda i: (nt - 1 - i, 0)), pl.BlockSpec((8, d), lambda i: (0, 0))],
        out_shape=[SDS((t_all, 3 * d), BF16), SDS((8, d), F32)],
        scratch_shapes=[pltpu.VMEM((SUBLANES, d), F32), pltpu.VMEM((tt, d), F32)],
        compiler_params=_cparams(),
    )(dx1, bcv, cc, w_conv, w_out.arr)


def _bwd_attention_inner(dx1, qkv, probs_t, p_sink, w_o, cos_t, sin_t, seq, name):
    t_all, d = dx1.shape
    width = qkv.shape[1]
    kvw = (width - d) // 2
    n_kv = kvw // HEAD_DIM
    tt = _token_tile(seq)
    tps = seq // tt
    nt = t_all // tt
    nblk = tt // WINDOW
    scale = HEAD_DIM ** -0.5

    def body(dx_ref, qkv_ref, kvp_ref, p_ref, psink_ref, cos_ref, sin_ref, wo_ref,
             dqkv_ref, dsink_ref, dbqkv_ref, dbo_ref,
             kvext_ref, dkvext_ref, carry_ref, dq_ref, do_ref, dp_ref, ds_ref):
        i = pl.program_id(0)
        dxv = dx_ref[...]
        do_ref[...] = _nt(dxv.astype(BF16), _mat(wo_ref)).astype(BF16)
        kvext_ref[0:WINDOW, :] = kvp_ref[...]
        kvext_ref[WINDOW:, :] = qkv_ref[:, d:]
        dkvext_ref[...] = jnp.zeros_like(dkvext_ref)

        @pl.when(i == 0)
        def _():
            carry_ref[...] = jnp.zeros_like(carry_ref)
            dsink_ref[...] = jnp.zeros_like(dsink_ref)
            dbqkv_ref[...] = jnp.zeros_like(dbqkv_ref)
            dbo_ref[...] = jnp.zeros_like(dbo_ref)

        head_lane = lax.broadcasted_iota(jnp.int32, (1, LANES), 1)
        dsink = jnp.zeros((1, LANES), F32)
        for n in range(nblk):
            for kh in range(n_kv):
                buf = (n * n_kv + kh) % 2
                qs = _stack_heads(qkv_ref, n * WINDOW, kh)
                dos = _stack_heads(do_ref, n * WINDOW, kh)
                kcols = slice(kh * HEAD_DIM, (kh + 1) * HEAD_DIM)
                vcols = slice(kvw + kh * HEAD_DIM, kvw + (kh + 1) * HEAD_DIM)
                band = slice(n * WINDOW, (n + 2) * WINDOW)
                kb = kvext_ref[band, kcols]
                vb = kvext_ref[band, vcols]
                dp_ref[buf] = _nt(vb, dos)
                for g in range(GROUP):
                    hd = kh * GROUP + g
                    cols = slice(g * WINDOW, (g + 1) * WINDOW)
                    probs = p_ref[n, kh, :, cols].astype(F32)
                    dp = dp_ref[buf, :, cols]
                    dsum = jnp.sum(probs * dp, axis=0, keepdims=True)
                    ds_ref[buf, :, cols] = (probs * (dp - dsum)).astype(BF16)
                    dsink = dsink - jnp.where(head_lane == hd, jnp.sum(psink_ref[n, kh, g:g + 1, :] * dsum), 0.0)
                ds_t = ds_ref[buf]
                dkvext_ref[band, vcols] += _nn(p_ref[n, kh], dos)
                dkvext_ref[band, kcols] += _nn(ds_t, qs)
                dq_s = _tn(kb, ds_t).T
                for g in range(GROUP):
                    hd = kh * GROUP + g
                    dq_ref[n * WINDOW:(n + 1) * WINDOW, hd * HEAD_DIM:(hd + 1) * HEAD_DIM] = dq_s[g * WINDOW:(g + 1) * WINDOW]
        dsink_ref[0:1, :] += dsink
        dkvext_ref[tt:tt + WINDOW, :] += carry_ref[...]
        carry_ref[...] = dkvext_ref[0:WINDOW, :]

        cosv = cos_ref[...]
        sinv = sin_ref[...]
        lane_lo = (lax.broadcasted_iota(jnp.int32, (tt, LANES), 1) % HEAD_DIM) < HEAD_DIM // 2
        for s in range((d + kvw) // LANES):
            if s * LANES < d:
                dy = dq_ref[:, s * LANES:(s + 1) * LANES] * scale
            else:
                dy = dkvext_ref[WINDOW:, s * LANES - d:(s + 1) * LANES - d]
            dpre = dy * cosv - _rope_partner(dy, lane_lo) * sinv
            dqkv_ref[:, s * LANES:(s + 1) * LANES] = dpre.astype(BF16)
            dbqkv_ref[0:1, s * LANES:(s + 1) * LANES] += jnp.sum(dpre, axis=0, keepdims=True)
        dv = dkvext_ref[WINDOW:, kvw:]
        dqkv_ref[:, d + kvw:] = dv.astype(BF16)
        dbqkv_ref[0:1, d + kvw:] += jnp.sum(dv, axis=0, keepdims=True)
        dbo_ref[...] += jnp.sum(dxv, axis=0, keepdims=True)

    kv_blocks = tt // WINDOW
    return pl.pallas_call(
        body, name=name, grid=(nt,),
        in_specs=[pl.BlockSpec((tt, d), lambda i: (nt - 1 - i, 0)),
                  pl.BlockSpec((tt, width), lambda i: (nt - 1 - i, 0)),
                  pl.BlockSpec((WINDOW, 2 * kvw), lambda i: (jnp.maximum((nt - 1 - i) * kv_blocks - 1, 0), d // (2 * kvw))),
                  pl.BlockSpec((nblk, n_kv, 2 * WINDOW, GROUP * WINDOW), lambda i: (nt - 1 - i, 0, 0, 0)),
                  pl.BlockSpec((nblk, n_kv, GROUP, WINDOW), lambda i: (nt - 1 - i, 0, 0, 0)),
                  pl.BlockSpec((tt, LANES), lambda i: ((nt - 1 - i) % tps, 0)),
                  pl.BlockSpec((tt, LANES), lambda i: ((nt - 1 - i) % tps, 0)),
                  _rows_spec(w_o)],
        out_specs=[pl.BlockSpec((tt, width), lambda i: (nt - 1 - i, 0)), pl.BlockSpec((8, LANES), lambda i: (0, 0)),
                   pl.BlockSpec((1, width), lambda i: (0, 0)), pl.BlockSpec((1, d), lambda i: (0, 0))],
        out_shape=[SDS((t_all, width), BF16), SDS((8, LANES), F32), SDS((1, width), F32), SDS((1, d), F32)],
        scratch_shapes=[pltpu.VMEM((tt + WINDOW, 2 * kvw), BF16), pltpu.VMEM((tt + WINDOW, 2 * kvw), F32),
                        pltpu.VMEM((WINDOW, 2 * kvw), F32), pltpu.VMEM((tt, d), F32), pltpu.VMEM((tt, d), BF16),
                        pltpu.VMEM((2, 2 * WINDOW, GROUP * WINDOW), F32), pltpu.VMEM((2, 2 * WINDOW, GROUP * WINDOW), BF16)],
        compiler_params=_cparams(),
    )(dx1, qkv, qkv, probs_t, p_sink, cos_t, sin_t, w_o.arr)


def _bwd_dense_norm(dy, w_t, x, gain, dres, name):
    t_all, d = x.shape
    n = dy.shape[1]
    tt = min(TOKEN_TILE, t_all)

    def body(dy_ref, w_ref, x_ref, g_ref, dres_ref, dx_ref, h_ref, dg_ref):
        i = pl.program_id(0)
        dh = _nn(dy_ref[...], _mat(w_ref))
        r, xh = _rms_parts(x_ref[...])
        gain_v = g_ref[...]
        h_ref[...] = (xh * gain_v).astype(BF16)
        dx_ref[...] = _rms_backward(dh, xh, r, gain_v, dres_ref[...])

        @pl.when(i == 0)
        def _():
            dg_ref[...] = jnp.zeros_like(dg_ref)

        dg_ref[...] += jnp.sum(dh * xh, axis=0, keepdims=True)

    return pl.pallas_call(
        body, name=name, grid=(t_all // tt,),
        in_specs=[pl.BlockSpec((tt, n), lambda i: (i, 0)), _rows_spec(w_t), pl.BlockSpec((tt, d), lambda i: (i, 0)),
                  _resident((1, d)), pl.BlockSpec((tt, d), lambda i: (i, 0))],
        out_specs=[pl.BlockSpec((tt, d), lambda i: (i, 0)), pl.BlockSpec((tt, d), lambda i: (i, 0)),
                   pl.BlockSpec((1, d), lambda i: (0, 0))],
        out_shape=[SDS((t_all, d), F32), SDS((t_all, d), BF16), SDS((1, d), F32)],
        compiler_params=_cparams(),
    )(dy, w_t.arr, x, gain, dres)


def _tn_matmul(a, b, dest, name):
    t_all, m = a.shape
    d = b.shape[1]
    n = dest.n
    assert m == N_DEV * n and dest.off % n == 0
    k = max(kk for kk in (1, 2, 4, 8) if kk * n <= max(n, 1536))
    tm = k * n
    tt = min(TN_TOKEN_TILE, t_all)
    n_t = t_all // tt
    fresh = not hasattr(dest.arr, "dtype")

    def body(a_ref, b_ref, *rest):
        o_ref, acc_ref = rest[-2:]
        t = pl.program_id(1)

        @pl.when(t == 0)
        def _():
            acc_ref[...] = jnp.zeros_like(acc_ref)

        acc_ref[...] += _tn(a_ref[...], b_ref[...].astype(BF16))

        @pl.when(t == n_t - 1)
        def _():
            o_ref[...] = acc_ref[...].astype(BF16).reshape(k, n, d)

    block = dest.off // n
    return pl.pallas_call(
        body, name=name, grid=(m // tm, n_t),
        in_specs=[pl.BlockSpec((tt, tm), lambda j, t: (t, j)), pl.BlockSpec((tt, d), lambda j, t: (t, 0))] + ([] if fresh else [ANY]),
        out_specs=pl.BlockSpec((k, n, d), lambda j, t: (j, block, 0)),
        out_shape=SDS(tuple(dest.arr) if fresh else dest.arr.shape, BF16),
        scratch_shapes=[pltpu.VMEM((tm, d), F32)],
        input_output_aliases={} if fresh else {2: 0},
        compiler_params=_cparams(2),
    )(*((a, b) if fresh else (a, b, dest.arr)))


def _my_place():
    return lax.axis_index("x"), lax.axis_index("y"), lax.axis_index("c")


def _other_chips(x, y):
    return [(1 - x, y), (x, 1 - y), (1 - x, 1 - y)]


def _all_gather(blocks, name):
    n_arr = len(blocks)

    def body(*refs):
        in_refs = refs[:n_arr]
        out_refs = refs[n_arr:2 * n_arr]
        send_sems, recv_sems, local_sems = refs[2 * n_arr:]
        x, y, c = _my_place()
        me, sibling = (x, y, c), (x, y, 1 - c)
        chips = _other_chips(x, y)

        def slot(a, place):
            px, py, pc = place
            return out_refs[a].at[4 * px + 2 * py + pc]

        def copy(a, k, block, to, src=None):
            return pltpu.make_async_remote_copy(
                src_ref=slot(a, block) if src is None else src, dst_ref=slot(a, block),
                send_sem=send_sems.at[a, k], recv_sem=recv_sems.at[a, k], device_id=to, device_id_type=MESH)

        started = []
        local = []
        for a in range(n_arr):
            mine = pltpu.make_async_copy(in_refs[a], slot(a, me), local_sems.at[a])
            mine.start()
            local.append(mine)
            first = [copy(a, 0, me, sibling, src=in_refs[a])]
            first += [copy(a, 1 + j, me, (*chip, c), src=in_refs[a]) for j, chip in enumerate(chips)]
            for cp in first:
                cp.start()
            started += first
        for a in range(n_arr):
            for j, chip in enumerate(chips):
                copy(a, 1 + j, (*chip, c), me).wait_recv()
                passed = copy(a, 4 + j, (*chip, c), sibling)
                passed.start()
                started.append(passed)
        for a in range(n_arr):
            copy(a, 0, sibling, me).wait_recv()
            for j, chip in enumerate(chips):
                copy(a, 4 + j, (*chip, 1 - c), me).wait_recv()
        for cp in started:
            cp.wait_send()
        for mine in local:
            mine.wait()

    return pl.pallas_call(
        body, name=name,
        in_specs=[ANY] * n_arr, out_specs=[ANY] * n_arr,
        out_shape=[SDS((N_DEV,) + b.shape, b.dtype) for b in blocks],
        scratch_shapes=[pltpu.SemaphoreType.DMA((n_arr, 7)), pltpu.SemaphoreType.DMA((n_arr, 7)),
                        pltpu.SemaphoreType.DMA((n_arr,))],
    )(*blocks)


def _peer_of(k, x, y, c):
    return x ^ ((k >> 2) & 1), y ^ ((k >> 1) & 1), c ^ (k & 1)


HBM = pl.BlockSpec(memory_space=pltpu.HBM)
SEM = pl.BlockSpec(memory_space=pltpu.SEMAPHORE)
DATAFLOW_EFFECT = pltpu.SideEffectType.DATAFLOW_SIDE_EFFECTING


def _peer_copies(src_ref, land_ref, send_sems, recv_sems, per_peer):
    x, y, c = _my_place()
    me = 4 * x + 2 * y + c
    copies = []
    for k in range(1, N_DEV):
        px, py, pc = _peer_of(k, x, y, c)
        peer = 4 * px + 2 * py + pc
        copies.append(pltpu.make_async_remote_copy(
            src_ref=src_ref.at[peer] if per_peer else src_ref, dst_ref=land_ref.at[me],
            send_sem=send_sems.at[k - 1], recv_sem=recv_sems.at[k - 1], device_id=(px, py, pc), device_id_type=MESH))
    own = pltpu.make_async_copy(src_ref.at[me] if per_peer else src_ref, land_ref.at[me], send_sems.at[N_DEV - 1])
    return copies, own


def _exchange_start(src, after, per_peer, name):
    rows, d = src.shape[-2:]

    def body(src_ref, land_ref, after_ref, send_sems, recv_sems, src_thru, land_thru, token):
        copies, own = _peer_copies(src_ref, land_ref, send_sems, recv_sems, per_peer)
        for cp in copies:
            cp.start()
        own.start()
        token[...] = jnp.zeros_like(token)

    return pl.pallas_call(
        body, name=name,
        out_shape=(pltpu.SemaphoreType.DMA((N_DEV,)), pltpu.SemaphoreType.DMA((N_DEV - 1,)), pltpu.HBM(src.shape, src.dtype),
                   pltpu.HBM((N_DEV, rows, d), src.dtype), SDS((SUBLANES, LANES), F32)),
        in_specs=(HBM, HBM, ANY), out_specs=(SEM, SEM, HBM, HBM, pl.BlockSpec(memory_space=pltpu.VMEM)),
        input_output_aliases={0: 2, 1: 3},
        compiler_params=pltpu.CompilerParams(has_side_effects=DATAFLOW_EFFECT),
    )(pltpu.with_memory_space_constraint(src, pltpu.HBM),
      pltpu.with_memory_space_constraint(lax.empty((N_DEV, rows, d), src.dtype), pltpu.HBM), after)


def _exchange_wait(started, after, per_peer, name):
    send_sems, recv_sems, src_thru, land_thru, _ = started

    def body(src_ref, land_ref, send_sems, recv_sems, after_ref, src_out, land_out):
        copies, own = _peer_copies(src_ref, land_ref, send_sems, recv_sems, per_peer)
        for cp in copies:
            cp.wait_send()
            cp.wait_recv()
        own.wait()

    return pl.pallas_call(
        body, name=name,
        out_shape=(pltpu.HBM(src_thru.shape, src_thru.dtype), pltpu.HBM(land_thru.shape, land_thru.dtype)),
        in_specs=(HBM, HBM, SEM, SEM, ANY), out_specs=(HBM, HBM), input_output_aliases={0: 0, 1: 1},
        compiler_params=pltpu.CompilerParams(has_side_effects=DATAFLOW_EFFECT),
    )(src_thru, land_thru, send_sems, recv_sems, after)


def _first_level_copies(src_ref, land_ref, send_sems, recv_sems):
    x, y, c = _my_place()
    me = 4 * x + 2 * y + c
    targets = [(x, y, 1 - c)] + [(px, py, c) for px, py in _other_chips(x, y)]
    copies = [pltpu.make_async_remote_copy(src_ref=src_ref, dst_ref=land_ref.at[me], send_sem=send_sems.at[k],
                                           recv_sem=recv_sems.at[k], device_id=t, device_id_type=MESH)
              for k, t in enumerate(targets)]
    own = pltpu.make_async_copy(src_ref, land_ref.at[me], send_sems.at[len(targets)])
    return copies, own


def _second_level_copies(land_ref, send_sems, recv_sems):
    x, y, c = _my_place()
    copies = []
    for j, (px, py) in enumerate(_other_chips(x, y)):
        slot = 4 * px + 2 * py + c
        copies.append(pltpu.make_async_remote_copy(
            src_ref=land_ref.at[slot], dst_ref=land_ref.at[slot], send_sem=send_sems.at[j], recv_sem=recv_sems.at[j],
            device_id=(x, y, 1 - c), device_id_type=MESH))
    return copies


def _gather2_start(src, after, name):
    rows, d = src.shape

    def body(src_ref, land_ref, after_ref, send_sems, recv_sems, src_thru, land_thru, token):
        copies, own = _first_level_copies(src_ref, land_ref, send_sems, recv_sems)
        for cp in copies:
            cp.start()
        own.start()
        token[...] = jnp.zeros_like(token)

    return pl.pallas_call(
        body, name=name,
        out_shape=(pltpu.SemaphoreType.DMA((5,)), pltpu.SemaphoreType.DMA((4,)), pltpu.HBM(src.shape, src.dtype),
                   pltpu.HBM((N_DEV, rows, d), src.dtype), SDS((SUBLANES, LANES), F32)),
        in_specs=(HBM, HBM, ANY), out_specs=(SEM, SEM, HBM, HBM, pl.BlockSpec(memory_space=pltpu.VMEM)),
        input_output_aliases={0: 2, 1: 3},
        compiler_params=pltpu.CompilerParams(has_side_effects=DATAFLOW_EFFECT),
    )(pltpu.with_memory_space_constraint(src, pltpu.HBM),
      pltpu.with_memory_space_constraint(lax.empty((N_DEV, rows, d), src.dtype), pltpu.HBM), after)


def _gather2_forward(started, after, name):
    send_a, recv_a, src_thru, land_thru, _ = started

    def body(src_ref, land_ref, send_a_ref, recv_a_ref, after_ref, src_out, land_out, send_f, recv_f):
        copies, _ = _first_level_copies(src_ref, land_ref, send_a_ref, recv_a_ref)
        for cp in copies[1:]:
            cp.wait_recv()
        for cp in _second_level_copies(land_ref, send_f, recv_f):
            cp.start()

    return pl.pallas_call(
        body, name=name,
        out_shape=(pltpu.HBM(src_thru.shape, src_thru.dtype), pltpu.HBM(land_thru.shape, land_thru.dtype),
                   pltpu.SemaphoreType.DMA((3,)), pltpu.SemaphoreType.DMA((3,))),
        in_specs=(HBM, HBM, SEM, SEM, ANY), out_specs=(HBM, HBM, SEM, SEM), input_output_aliases={0: 0, 1: 1},
        compiler_params=pltpu.CompilerParams(has_side_effects=DATAFLOW_EFFECT),
    )(src_thru, land_thru, send_a, recv_a, after)


def _gather2_wait(started, forwarded, name):
    send_a, recv_a, _, _, _ = started
    src_thru, land_thru, send_f, recv_f = forwarded

    def body(src_ref, land_ref, send_a_ref, recv_a_ref, send_f_ref, recv_f_ref, src_out, land_out):
        copies, own = _first_level_copies(src_ref, land_ref, send_a_ref, recv_a_ref)
        for cp in copies:
            cp.wait_send()
        own.wait()
        copies[0].wait_recv()
        for cp in _second_level_copies(land_ref, send_f_ref, recv_f_ref):
            cp.wait_send()
            cp.wait_recv()

    return pl.pallas_call(
        body, name=name,
        out_shape=(pltpu.HBM(src_thru.shape, src_thru.dtype), pltpu.HBM(land_thru.shape, land_thru.dtype)),
        in_specs=(HBM, HBM, SEM, SEM, SEM, SEM), out_specs=(HBM, HBM), input_output_aliases={0: 0, 1: 1},
        compiler_params=pltpu.CompilerParams(has_side_effects=DATAFLOW_EFFECT),
    )(src_thru, land_thru, send_a, recv_a, send_f, recv_f)[1]


def _sum_slots(slots, name):
    _, rows, d = slots.shape
    tr = _largest_divisor(rows, 512, 16)

    def body(s_ref, o_ref):
        acc = s_ref[0].astype(F32)
        for dev in range(1, N_DEV):
            acc = acc + s_ref[dev].astype(F32)
        o_ref[...] = acc

    return pl.pallas_call(
        body, name=name, grid=(rows // tr,),
        in_specs=[pl.BlockSpec((N_DEV, tr, d), lambda r: (0, r, 0))], out_specs=pl.BlockSpec((tr, d), lambda r: (r, 0)),
        out_shape=SDS((rows, d), F32), compiler_params=_cparams(),
    )(slots)


def _all_reduce_small(part, loss_rows, name):
    rows, lanes = part.shape
    lo, hi = loss_rows

    def body(x_ref, out_ref, loss_ref, gath_ref, send_sems, recv_sems):
        x, y, c = _my_place()
        me = 4 * x + 2 * y + c
        gath_ref[me] = x_ref[...]
        copies = []
        for k in range(1, N_DEV):
            peer = (x ^ ((k >> 2) & 1), y ^ ((k >> 1) & 1), c ^ (k & 1))
            cp = pltpu.make_async_remote_copy(
                src_ref=x_ref, dst_ref=gath_ref.at[me], send_sem=send_sems.at[k - 1], recv_sem=recv_sems.at[k - 1],
                device_id=peer, device_id_type=MESH)
            cp.start()
            copies.append(cp)
        for cp in copies:
            cp.wait_recv()
        for cp in copies:
            cp.wait_send()
        acc = gath_ref[0]
        for dev in range(1, N_DEV):
            acc = acc + gath_ref[dev]
        out_ref[...] = acc
        loss_ref[...] = jnp.full(loss_ref.shape, jnp.sum(acc[lo:hi, :]), F32)

    vmem = pl.BlockSpec(memory_space=pltpu.VMEM)
    return pl.pallas_call(
        body, name=name, in_specs=[vmem], out_specs=[vmem, vmem],
        out_shape=[SDS((rows, lanes), F32), SDS((SUBLANES, LANES), F32)],
        scratch_shapes=[pltpu.VMEM((N_DEV, rows, lanes), F32), pltpu.SemaphoreType.DMA((N_DEV - 1,)),
                        pltpu.SemaphoreType.DMA((N_DEV - 1,))],
    )(part)


def _adamw(w, g, m, v, name):
    rows, cols = w.shape
    tr = rows if rows % SUBLANES else _largest_divisor(rows, 512, SUBLANES)

    def body(w_ref, g_ref, m_ref, v_ref, d_ref, nm_ref, nv_ref):
        gv = g_ref[...]
        nm = ADAM_B1 * m_ref[...] + (1.0 - ADAM_B1) * gv
        nv = ADAM_B2 * v_ref[...] + (1.0 - ADAM_B2) * (gv * gv)
        m_hat = nm / (1.0 - ADAM_B1 ** ADAM_STEP)
        v_hat = nv / (1.0 - ADAM_B2 ** ADAM_STEP)
        d_ref[...] = -ADAM_LR * (m_hat / (jnp.sqrt(v_hat) + ADAM_EPS) + ADAM_WD * w_ref[...])
        nm_ref[...] = nm
        nv_ref[...] = nv

    spec = pl.BlockSpec((tr, cols), lambda i: (i, 0))
    return pl.pallas_call(
        body, name=name, grid=(rows // tr,), in_specs=[spec] * 4, out_specs=[spec] * 3,
        out_shape=[SDS((rows, cols), F32)] * 3, compiler_params=_cparams(),
    )(w, g, m, v)


def _adamw_nd(w, g, m, v, name):
    shape = w.shape
    two_d = (1, shape[0]) if len(shape) == 1 else (-1, shape[-1])
    outs = _adamw(w.reshape(two_d), g.reshape(two_d), m.reshape(two_d), v.reshape(two_d), name)
    return [o.reshape(shape) for o in outs]


def _rope_tables(seq):
    pos = jnp.arange(seq, dtype=F32)
    inv_freq = 1.0 / (ROPE_THETA ** (jnp.arange(0, HEAD_DIM, 2, dtype=F32) / HEAD_DIM))
    ang = pos[:, None] * inv_freq[None, :]
    cos, sin = jnp.cos(ang), jnp.sin(ang)
    reps = LANES // HEAD_DIM
    cos_t = jnp.tile(jnp.concatenate([cos, cos], axis=1), (1, reps))
    sin_t = jnp.tile(jnp.concatenate([-sin, sin], axis=1), (1, reps))
    return cos_t, sin_t


def _flat_pad(a):
    flat = a.reshape(1, -1)
    pad = (-flat.shape[1]) % LANES
    return jnp.pad(flat, ((0, 0), (0, pad))) if pad else flat


def kernel(x, norm_mix, norm_ffn, norm_final, conv_w_in, conv_w_conv, conv_w_out, attn_w_qkv, attn_b_qkv, attn_sinks, attn_w_o, attn_b_o, ffn_w_in, ffn_w_conv, ffn_w_down, loss_target, m_norm_mix, m_norm_ffn, m_norm_final, m_conv_w_in, m_conv_w_conv, m_conv_w_out, m_attn_w_qkv, m_attn_b_qkv, m_attn_sinks, m_attn_w_o, m_attn_b_o, m_ffn_w_in, m_ffn_w_conv, m_ffn_w_down, v_norm_mix, v_norm_ffn, v_norm_final, v_conv_w_in, v_conv_w_conv, v_conv_w_out, v_attn_w_qkv, v_attn_b_qkv, v_attn_sinks, v_attn_w_o, v_attn_b_o, v_ffn_w_in, v_ffn_w_conv, v_ffn_w_down):
    b_loc, seq, d = x.shape
    depth = norm_mix.shape[0]
    n_conv, n_attn = conv_w_in.shape[0], attn_w_qkv.shape[0]
    t_all = b_loc * seq
    my_x, my_y, my_c = _my_place()

    me = 4 * my_x + 2 * my_y + my_c

    groups = []
    for i in range(depth):
        j = i // 2
        if i % 2 == 0:
            mix = [("conv_w_in", j, True, conv_w_in[j].T), ("conv_w_out", j, False, conv_w_out[j])]
        else:
            mix = [("attn_w_qkv", j, True, attn_w_qkv[j].T), ("attn_w_o", j, False, attn_w_o[j])]
        groups.append((("mix", i), mix))
        groups.append((("ffn", i), [("ffn_w_in", i, True, ffn_w_in[i].T), ("ffn_w_down", i, False, ffn_w_down[i])]))
    order = [key for key, _ in groups]
    members_of = dict(groups)

    def layout(key):
        offs, o = [], 0
        for _, _, _, shard in members_of[key]:
            n = shard.shape[0]
            o = -(-o // n) * n
            offs.append(o)
            o += n
        return offs, o

    small = jnp.concatenate([_flat_pad(conv_w_conv), _flat_pad(ffn_w_conv), _flat_pad(attn_b_qkv), _flat_pad(attn_b_o)], axis=1)
    (small_g,) = _all_gather([small], "gather_small")

    gather_started = {}

    def start_gather(idx, after):
        if idx >= len(order):
            return 0.0
        key = order[idx]
        offs, total = layout(key)
        pieces, o = [], 0
        for (_, _, _, shard), off in zip(members_of[key], offs):
            if off > o:
                pieces.append(jnp.zeros((off - o, d), shard.dtype))
            pieces.append(shard)
            o = off + shard.shape[0]
        pack = jnp.concatenate(pieces, axis=0).astype(BF16)
        if idx < 2:
            gather_started[key] = _gather2_start(pack, after, f"gather_start_{key[0]}_{key[1]}")
        else:
            gather_started[key] = _exchange_start(pack, after, False, f"gather_start_{key[0]}_{key[1]}")
        return gather_started[key][4][0, 0]

    weights = {}

    def finish_gather(key, after):
        if key in order[:2]:
            forwarded = _gather2_forward(gather_started[key], after, f"gather_forward_{key[0]}_{key[1]}")
            land = _gather2_wait(gather_started[key], forwarded, f"gather_wait_{key[0]}_{key[1]}")
        else:
            _, land = _exchange_wait(gather_started[key], after, False, f"gather_wait_{key[0]}_{key[1]}")
        for (wname, layer, _, shard), off in zip(members_of[key], layout(key)[0]):
            weights[(wname, layer)] = _Rows(land, off, shard.shape[0])

    def take_small(o, shape):
        size = shape[0] * shape[1] * shape[2]
        blk = small_g[:, 0, o:o + size].reshape((N_DEV,) + shape)
        return jnp.moveaxis(blk, 0, 2).reshape(shape[0], shape[1], N_DEV * shape[2])

    so = 0
    wc_conv_full = take_small(so, conv_w_conv.shape); so += _flat_pad(conv_w_conv).shape[1]
    wc_ffn_full = take_small(so, ffn_w_conv.shape); so += _flat_pad(ffn_w_conv).shape[1]
    b_qkv_full = take_small(so, (n_attn, 1, attn_b_qkv.shape[1]))[:, 0]; so += _flat_pad(attn_b_qkv).shape[1]
    b_o_full = take_small(so, (n_attn, 1, attn_b_o.shape[1]))[:, 0]

    cos_t, sin_t = _rope_tables(seq)

    xs = [x.reshape(t_all, d)]
    saved = []
    token = start_gather(0, small_g)
    for i in range(depth):
        j = i // 2
        if i > 0:
            token = start_gather(2 * i + 2, xs[-1])
        gain_mix = norm_mix[i][None, :] + token
        finish_gather(("mix", i), gain_mix if i == 0 else xs[-1])
        if i == 0:
            gain_mix = gain_mix + start_gather(1, weights[(members_of[order[0]][0][0], 0)].arr)
        if i % 2 == 0:
            x1, *mix_saved = _fwd_conv_mixer(xs[-1], gain_mix, weights[("conv_w_in", j)], wc_conv_full[j],
                                             weights[("conv_w_out", j)], seq, f"fwd_conv_{i}")
        else:
            qkv = _fwd_qkv(xs[-1], gain_mix, weights[("attn_w_qkv", j)], b_qkv_full[j][None, :], cos_t, sin_t, seq,
                           f"fwd_qkv_{i}")
            x1, o, probs_t, p_sink = _fwd_attention(qkv, xs[-1], attn_sinks[j], weights[("attn_w_o", j)],
                                                    b_o_full[j][None, :], seq, f"fwd_attn_{i}")
            mix_saved = (qkv, o, probs_t, p_sink)
        token = start_gather(2 * i + 3, x1) + (start_gather(2, x1) if i == 0 else 0.0)
        gain_ffn = norm_ffn[i][None, :] + token
        finish_gather(("ffn", i), gain_ffn)
        x2, *ffn_saved = _fwd_ffn(x1, gain_ffn, weights[("ffn_w_in", i)], wc_ffn_full[i], weights[("ffn_w_down", i)],
                                  seq, f"fwd_ffn_{i}")
        saved.append((xs[-1], x1, mix_saved, ffn_saved))
        xs.append(x2)
        token = 0.0

    dx, dg_final, loss_lanes = _final_norm_loss(xs[-1], norm_final[None, :], loss_target.reshape(t_all, d), "loss_head")

    dg_mix, dg_ffn = [None] * depth, [None] * depth
    dwc_conv, dwc_ffn = [None] * n_conv, [None] * depth
    db_qkv, db_o, dsinks = [None] * n_attn, [None] * n_attn, [None] * n_attn
    scatter_started = {}

    def weight_grads(key, operands):
        offs, total = layout(key)
        parts = (N_DEV, total, d)
        for (wname, layer, _, shard), off, (a, b) in zip(members_of[key], offs, operands):
            parts = _tn_matmul(a, b, _Rows(parts, off, shard.shape[0]), f"dw_{wname}_{layer}")
        scatter_started[key] = _exchange_start(parts, operands[0][1], True, f"scatter_start_{key[0]}_{key[1]}")
        return scatter_started[key][4][0, 0]

    token = 0.0
    for i in reversed(range(depth)):
        j = i // 2
        x0, x1, mix_saved, (gate, s_act, uds, act) = saved[i]
        dgu, dwc = _bwd_ffn_inner(dx, gate, s_act, uds, wc_ffn_full[i] + token, weights[("ffn_w_down", i)], seq, f"bwd_ffn_{i}")
        dwc_ffn[i] = dwc[:3]
        dx1, h2, dg_ffn[i] = _bwd_dense_norm(dgu, weights[("ffn_w_in", i)], x1, norm_ffn[i][None, :], dx, f"bwd_ffn_norm_{i}")
        token = weight_grads(("ffn", i), [(dgu, h2), (act, dx)])
        if i % 2 == 0:
            bcv, cc, y = mix_saved
            dbcv, dwc = _bwd_conv_inner(dx1, bcv, cc, wc_conv_full[j] + token, weights[("conv_w_out", j)], seq, f"bwd_conv_{i}")
            dwc_conv[j] = dwc[:3]
            dx, h, dg_mix[i] = _bwd_dense_norm(dbcv, weights[("conv_w_in", j)], x0, norm_mix[i][None, :], dx1,
                                               f"bwd_conv_norm_{i}")
            token = weight_grads(("mix", i), [(dbcv, h), (y, dx1)])
        else:
            qkv, o, probs_t, p_sink = mix_saved
            dqkv, dsk, dbq, dbo = _bwd_attention_inner(dx1, qkv, probs_t, p_sink + token, weights[("attn_w_o", j)], cos_t,
                                                       sin_t, seq, f"bwd_attn_{i}")
            dsinks[j], db_qkv[j], db_o[j] = dsk[0:1, :attn_sinks.shape[1]], dbq, dbo
            dx, h, dg_mix[i] = _bwd_dense_norm(dqkv, weights[("attn_w_qkv", j)], x0, norm_mix[i][None, :], dx1,
                                               f"bwd_attn_norm_{i}")
            token = weight_grads(("mix", i), [(dqkv, h), (o, dx1)])
    grad_x = dx.reshape(b_loc, seq, d)

    reduced = {}

    def finish_scatter(key, after):
        _, land = _exchange_wait(scatter_started[key], after, True, f"scatter_wait_{key[0]}_{key[1]}")
        total = _sum_slots(land, f"scatter_sum_{key[0]}_{key[1]}")
        for (wname, layer, transposed, shard), off in zip(members_of[key], layout(key)[0]):
            rows = total[off:off + shard.shape[0]]
            reduced[(wname, layer)] = rows.T if transposed else rows

    last_key = order[0]
    for key in reversed(order[1:]):
        finish_scatter(key, dx)

    small_parts = [jnp.concatenate(dg_mix, axis=0), jnp.concatenate(dg_ffn, axis=0), dg_final,
                   jnp.stack(dwc_conv), jnp.stack(dwc_ffn), jnp.concatenate(db_qkv, axis=0), jnp.concatenate(db_o, axis=0),
                   jnp.concatenate(dsinks, axis=0), loss_lanes]
    flats = [_flat_pad(p) for p in small_parts]
    bounds = []
    so = 0
    for fl in flats:
        bounds.append((so, so + fl.shape[1]))
        so += fl.shape[1]
    small_rows = so // LANES
    pad_rows = (-small_rows) % SUBLANES
    part_small = jnp.pad(jnp.concatenate(flats, axis=1).reshape(small_rows, LANES), ((0, pad_rows), (0, 0)))
    loss_rows = (bounds[-1][0] // LANES, bounds[-1][1] // LANES)
    summed, loss_tile = _all_reduce_small(part_small, loss_rows, "reduce_small")
    summed = summed.reshape(1, -1)

    def small_grad(k, shape):
        lo = bounds[k][0]
        size = 1
        for s_ in shape:
            size *= s_
        return summed[0, lo:lo + size].reshape(shape)

    def my_cols(full, n_local):
        return lax.dynamic_slice_in_dim(full, me * n_local, n_local, axis=full.ndim - 1)

    g_norm_mix = small_grad(0, norm_mix.shape)
    g_norm_ffn = small_grad(1, norm_ffn.shape)
    g_norm_final = small_grad(2, norm_final.shape)
    g_conv_w_conv = my_cols(small_grad(3, (n_conv, 3, d)), conv_w_conv.shape[2])
    g_ffn_w_conv = my_cols(small_grad(4, (depth, 3, ffn_w_conv.shape[2] * N_DEV)), ffn_w_conv.shape[2])
    g_attn_b_qkv = my_cols(small_grad(5, (n_attn, attn_b_qkv.shape[1] * N_DEV)), attn_b_qkv.shape[1])
    g_attn_b_o = my_cols(small_grad(6, (n_attn, d)), attn_b_o.shape[1])
    g_attn_sinks = small_grad(7, attn_sinks.shape)
    loss = loss_tile[0, 0]

    def big_grad(wname, n_layers):
        return jnp.stack([reduced[(wname, layer)] for layer in range(n_layers)])

    grads = {
        "norm_mix": g_norm_mix, "norm_ffn": g_norm_ffn, "norm_final": g_norm_final, "conv_w_conv": g_conv_w_conv,
        "attn_w_qkv": big_grad("attn_w_qkv", n_attn), "attn_b_qkv": g_attn_b_qkv, "attn_sinks": g_attn_sinks,
        "attn_w_o": big_grad("attn_w_o", n_attn), "attn_b_o": g_attn_b_o,
        "ffn_w_in": big_grad("ffn_w_in", depth), "ffn_w_conv": g_ffn_w_conv, "ffn_w_down": big_grad("ffn_w_down", depth),
    }
    params = {
        "norm_mix": (norm_mix, m_norm_mix, v_norm_mix), "norm_ffn": (norm_ffn, m_norm_ffn, v_norm_ffn),
        "norm_final": (norm_final, m_norm_final, v_norm_final), "conv_w_in": (conv_w_in, m_conv_w_in, v_conv_w_in),
        "conv_w_conv": (conv_w_conv, m_conv_w_conv, v_conv_w_conv), "conv_w_out": (conv_w_out, m_conv_w_out, v_conv_w_out),
        "attn_w_qkv": (attn_w_qkv, m_attn_w_qkv, v_attn_w_qkv), "attn_b_qkv": (attn_b_qkv, m_attn_b_qkv, v_attn_b_qkv),
        "attn_sinks": (attn_sinks, m_attn_sinks, v_attn_sinks), "attn_w_o": (attn_w_o, m_attn_w_o, v_attn_w_o),
        "attn_b_o": (attn_b_o, m_attn_b_o, v_attn_b_o), "ffn_w_in": (ffn_w_in, m_ffn_w_in, v_ffn_w_in),
        "ffn_w_conv": (ffn_w_conv, m_ffn_w_conv, v_ffn_w_conv), "ffn_w_down": (ffn_w_down, m_ffn_w_down, v_ffn_w_down),
    }
    names = list(params)
    updates = {}

    def update(wname):
        w, m, v = params[wname]
        updates[wname] = _adamw_nd(w, grads[wname], m, v, f"adamw_{wname}")

    last_names = sorted({wname for wname, _, _, _ in members_of[last_key]})
    for wname in names:
        if wname not in last_names:
            update(wname)
    finish_scatter(last_key, updates["ffn_w_in"][0])
    for wname in last_names:
        grads[wname] = big_grad(wname, params[wname][0].shape[0])
        update(wname)
    return (loss, grad_x, *[grads[wname] for wname in names], *[updates[wname][0] for wname in names],
            *[updates[wname][1] for wname in names], *[updates[wname][2] for wname in names])
```
